```python
import math
import jax, jax.numpy as jnp
from jax import lax
import numpy as np

D_MODEL = 1024
BATCH = 16
SEQ = 4096
DEPTH = 4

PLE_DIM = 256
D_FF = 2816
MIX_WIDTH = D_MODEL
SSM_WIDTH = MIX_WIDTH // 2
POOL_WIDTH = MIX_WIDTH - SSM_WIDTH
SSM_GROUP_CH = 16
SSM_GROUPS = SSM_WIDTH // SSM_GROUP_CH
SSM_STATE = 64
POOL_WINDOWS = (2, 4, 8, 16)
POOL_GROUP_CH = POOL_WIDTH // len(POOL_WINDOWS)
EPS = 1e-6
DT_MIN = 1e-3
DT_MAX = 1e-1

kernel_name = "hybrid_s5_pool_macaron_ple"


def rms_norm(x, g):
    x32 = x.astype(jnp.float32)
    y = x32 * lax.rsqrt(jnp.mean(x32 * x32, axis=-1, keepdims=True) + EPS)
    return (y * g.astype(jnp.float32)).astype(x.dtype)


def swiglu(x, wi, wo):
    gu = x @ wi
    g, u = jnp.split(gu, 2, axis=-1)
    return (jax.nn.silu(g) * u) @ wo


def _ssm_combine(e1, e2):
    a1, b1 = e1
    a2, b2 = e2
    return a1 * a2, a2 * b1 + b2


def s5_mixer(u, lam_re, lam_im, log_dt, b_re, b_im, c_re, c_im, d_skip, w_glu):
    bsz, seq, _ = u.shape
    f32 = jnp.float32
    u32 = u.astype(f32)
    ug = u32.reshape(bsz, seq, SSM_GROUPS, SSM_GROUP_CH)
    lam = lax.complex(lam_re.astype(f32), lam_im.astype(f32))
    dt = jnp.exp(log_dt.astype(f32))[:, None]
    lam_bar = jnp.exp(lam * dt)
    b = lax.complex(b_re.astype(f32), b_im.astype(f32))
    b_bar = ((lam_bar - 1.0) / lam)[..., None] * b
    bu = jnp.einsum('blgh,gph->blgp', ug.astype(jnp.complex64), b_bar)
    a = jnp.broadcast_to(lam_bar[None, None], (1, seq, SSM_GROUPS, SSM_STATE))
    _, states = lax.associative_scan(_ssm_combine, (a, bu), axis=1)
    c = lax.complex(c_re.astype(f32), c_im.astype(f32))
    y = jnp.real(jnp.einsum('blgp,ghp->blgh', states, c)).reshape(bsz, seq, SSM_WIDTH)
    y = y + d_skip.astype(f32) * u32
    y = jax.nn.gelu(y)
    y = y * jax.nn.sigmoid(y @ w_glu.astype(f32))
    return y.astype(u.dtype)


def pool_mixer(u, w_pool, scale):
    bsz, seq, _ = u.shape
    u32 = u.astype(jnp.float32)
    cs = lax.cumsum(u32, axis=1)
    count = jnp.arange(1, seq + 1, dtype=jnp.float32)[:, None]
    outs = []
    for gi, win in enumerate(POOL_WINDOWS):
        sl = slice(gi * POOL_GROUP_CH, (gi + 1) * POOL_GROUP_CH)
        cg = cs[..., sl]
        prev = jnp.pad(cg, ((0, 0), (win, 0), (0, 0)))[:, :seq]
        mean = (cg - prev) / jnp.minimum(count, float(win))
        outs.append((mean - u32[..., sl]) @ w_pool[gi].astype(jnp.float32))
    y = jnp.concatenate(outs, axis=-1) * scale.astype(jnp.float32)
    return y.astype(u.dtype)


def _fwd_setup_inputs(seed: int = 0) -> dict:
    key = jax.random.key(seed)
    ks = jax.random.split(key, 32)
    f32 = jnp.float32
    nrm = lambda k, shape, s: jax.random.normal(k, shape, f32) * s
    n_idx = jnp.arange(SSM_STATE, dtype=f32)
    lam_re = -0.5 + nrm(ks[0], (DEPTH, SSM_GROUPS, SSM_STATE), 0.01)
    lam_im = math.pi * n_idx[None, None, :] + nrm(ks[1], (DEPTH, SSM_GROUPS, SSM_STATE), 0.01)
    log_dt = jax.random.uniform(ks[2], (DEPTH, SSM_GROUPS), f32, math.log(DT_MIN), math.log(DT_MAX))
    return {
        "x": nrm(ks[3], (BATCH, SEQ, D_MODEL), 1.0),
        "p": nrm(ks[4], (DEPTH, BATCH, SEQ, PLE_DIM), 1.0),
        "ffn1_norm": 1.0 + nrm(ks[5], (DEPTH, D_MODEL), 0.02),
        "ffn1_wi": nrm(ks[6], (DEPTH, D_MODEL, 2 * D_FF), D_MODEL ** -0.5),
        "ffn1_wo": nrm(ks[7], (DEPTH, D_FF, D_MODEL), D_FF ** -0.5),
        "mix_norm": 1.0 + nrm(ks[8], (DEPTH, D_MODEL), 0.02),
        "w_in": nrm(ks[9], (DEPTH, D_MODEL, MIX_WIDTH), D_MODEL ** -0.5),
        "ssm_lambda_re": lam_re,
        "ssm_lambda_im": lam_im,
        "ssm_log_dt": log_dt,
        "ssm_b_re": nrm(ks[10], (DEPTH, SSM_GROUPS, SSM_STATE, SSM_GROUP_CH), (2.0 * SSM_GROUP_CH) ** -0.5),
        "ssm_b_im": nrm(ks[11], (DEPTH, SSM_GROUPS, SSM_STATE, SSM_GROUP_CH), (2.0 * SSM_GROUP_CH) ** -0.5),
        "ssm_c_re": nrm(ks[12], (DEPTH, SSM_GROUPS, SSM_GROUP_CH, SSM_STATE), (2.0 * SSM_STATE) ** -0.5),
        "ssm_c_im": nrm(ks[13], (DEPTH, SSM_GROUPS, SSM_GROUP_CH, SSM_STATE), (2.0 * SSM_STATE) ** -0.5),
        "ssm_d": nrm(ks[14], (DEPTH, SSM_WIDTH), 1.0),
        "ssm_w_glu": nrm(ks[15], (DEPTH, SSM_WIDTH, SSM_WIDTH), SSM_WIDTH ** -0.5),
        "pool_w": nrm(ks[16], (DEPTH, len(POOL_WINDOWS), POOL_GROUP_CH, POOL_GROUP_CH), POOL_GROUP_CH ** -0.5),
        "pool_scale": 1.0 + nrm(ks[17], (DEPTH, POOL_WIDTH), 0.02),
        "w_out": nrm(ks[18], (DEPTH, MIX_WIDTH, D_MODEL), MIX_WIDTH ** -0.5),
        "ffn2_norm": 1.0 + nrm(ks[19], (DEPTH, D_MODEL), 0.02),
        "ffn2_wi": nrm(ks[20], (DEPTH, D_MODEL, 2 * D_FF), D_MODEL ** -0.5),
        "ffn2_wo": nrm(ks[21], (DEPTH, D_FF, D_MODEL), D_FF ** -0.5),
        "ple_norm": 1.0 + nrm(ks[22], (DEPTH, D_MODEL), 0.02),
        "ple_w_gate": nrm(ks[23], (DEPTH, D_MODEL, D_MODEL), D_MODEL ** -0.5),
        "ple_w_proj": nrm(ks[24], (DEPTH, PLE_DIM, D_MODEL), PLE_DIM ** -0.5),
        "final_norm": 1.0 + nrm(ks[25], (D_MODEL,), 0.02),
    }


def _fwd_reference(x, p, ffn1_norm, ffn1_wi, ffn1_wo, mix_norm, w_in,
              ssm_lambda_re, ssm_lambda_im, ssm_log_dt, ssm_b_re, ssm_b_im, ssm_c_re, ssm_c_im,
              ssm_d, ssm_w_glu, pool_w, pool_scale, w_out,
              ffn2_norm, ffn2_wi, ffn2_wo, ple_norm, ple_w_gate, ple_w_proj, final_norm):
    h = x
    for i in range(DEPTH):
        h = h + 0.5 * swiglu(rms_norm(h, ffn1_norm[i]), ffn1_wi[i], ffn1_wo[i])
        z = rms_norm(h, mix_norm[i]) @ w_in[i]
        y_ssm = s5_mixer(z[..., :SSM_WIDTH], ssm_lambda_re[i], ssm_lambda_im[i], ssm_log_dt[i],
                         ssm_b_re[i], ssm_b_im[i], ssm_c_re[i], ssm_c_im[i], ssm_d[i], ssm_w_glu[i])
        y_pool = pool_mixer(z[..., SSM_WIDTH:], pool_w[i], pool_scale[i])
        h = h + jnp.concatenate([y_ssm, y_pool], axis=-1) @ w_out[i]
        h = h + 0.5 * swiglu(rms_norm(h, ffn2_norm[i]), ffn2_wi[i], ffn2_wo[i])
        gate = jax.nn.sigmoid((rms_norm(h, ple_norm[i]) @ ple_w_gate[i]).astype(jnp.float32))
        h = h + (gate * (p[i] @ ple_w_proj[i]).astype(jnp.float32)).astype(h.dtype)
    return rms_norm(h, final_norm)


import jax as _jax
import jax.numpy as _jnp

TWIN_FORMAT = 'train_step'
FWD_PARAMS = ['x', 'p', 'ffn1_norm', 'ffn1_wi', 'ffn1_wo', 'mix_norm', 'w_in', 'ssm_lambda_re', 'ssm_lambda_im', 'ssm_log_dt', 'ssm_b_re', 'ssm_b_im', 'ssm_c_re', 'ssm_c_im', 'ssm_d', 'ssm_w_glu', 'pool_w', 'pool_scale', 'w_out', 'ffn2_norm', 'ffn2_wi', 'ffn2_wo', 'ple_norm', 'ple_w_gate', 'ple_w_proj', 'final_norm']
TWIN_WEIGHTS = ['ffn1_norm', 'ffn1_wi', 'ffn1_wo', 'mix_norm', 'w_in', 'ssm_lambda_re', 'ssm_lambda_im', 'ssm_log_dt', 'ssm_b_re', 'ssm_b_im', 'ssm_c_re', 'ssm_c_im', 'ssm_d', 'ssm_w_glu', 'pool_w', 'pool_scale', 'w_out', 'ffn2_norm', 'ffn2_wi', 'ffn2_wo', 'ple_norm', 'ple_w_gate', 'ple_w_proj', 'final_norm']
TWIN_DIFF_INPUT = 'x'
TWIN_INPUTS = ['x', 'p', 'ffn1_norm', 'ffn1_wi', 'ffn1_wo', 'mix_norm', 'w_in', 'ssm_lambda_re', 'ssm_lambda_im', 'ssm_log_dt', 'ssm_b_re', 'ssm_b_im', 'ssm_c_re', 'ssm_c_im', 'ssm_d', 'ssm_w_glu', 'pool_w', 'pool_scale', 'w_out', 'ffn2_norm', 'ffn2_wi', 'ffn2_wo', 'ple_norm', 'ple_w_gate', 'ple_w_proj', 'final_norm', 'loss_target', 'm_ffn1_norm', 'm_ffn1_wi', 'm_ffn1_wo', 'm_mix_norm', 'm_w_in', 'm_ssm_lambda_re', 'm_ssm_lambda_im', 'm_ssm_log_dt', 'm_ssm_b_re', 'm_ssm_b_im', 'm_ssm_c_re', 'm_ssm_c_im', 'm_ssm_d', 'm_ssm_w_glu', 'm_pool_w', 'm_pool_scale', 'm_w_out', 'm_ffn2_norm', 'm_ffn2_wi', 'm_ffn2_wo', 'm_ple_norm', 'm_ple_w_gate', 'm_ple_w_proj', 'm_final_norm', 'v_ffn1_norm', 'v_ffn1_wi', 'v_ffn1_wo', 'v_mix_norm', 'v_w_in', 'v_ssm_lambda_re', 'v_ssm_lambda_im', 'v_ssm_log_dt', 'v_ssm_b_re', 'v_ssm_b_im', 'v_ssm_c_re', 'v_ssm_c_im', 'v_ssm_d', 'v_ssm_w_glu', 'v_pool_w', 'v_pool_scale', 'v_w_out', 'v_ffn2_norm', 'v_ffn2_wi', 'v_ffn2_wo', 'v_ple_norm', 'v_ple_w_gate', 'v_ple_w_proj', 'v_final_norm']
TWIN_OUTPUTS = ['loss', 'grad_x', 'grad_ffn1_norm', 'grad_ffn1_wi', 'grad_ffn1_wo', 'grad_mix_norm', 'grad_w_in', 'grad_ssm_lambda_re', 'grad_ssm_lambda_im', 'grad_ssm_log_dt', 'grad_ssm_b_re', 'grad_ssm_b_im', 'grad_ssm_c_re', 'grad_ssm_c_im', 'grad_ssm_d', 'grad_ssm_w_glu', 'grad_pool_w', 'grad_pool_scale', 'grad_w_out', 'grad_ffn2_norm', 'grad_ffn2_wi', 'grad_ffn2_wo', 'grad_ple_norm', 'grad_ple_w_gate', 'grad_ple_w_proj', 'grad_final_norm', 'delta_ffn1_norm', 'delta_ffn1_wi', 'delta_ffn1_wo', 'delta_mix_norm', 'delta_w_in', 'delta_ssm_lambda_re', 'delta_ssm_lambda_im', 'delta_ssm_log_dt', 'delta_ssm_b_re', 'delta_ssm_b_im', 'delta_ssm_c_re', 'delta_ssm_c_im', 'delta_ssm_d', 'delta_ssm_w_glu', 'delta_pool_w', 'delta_pool_scale', 'delta_w_out', 'delta_ffn2_norm', 'delta_ffn2_wi', 'delta_ffn2_wo', 'delta_ple_norm', 'delta_ple_w_gate', 'delta_ple_w_proj', 'delta_final_norm', 'new_m_ffn1_norm', 'new_m_ffn1_wi', 'new_m_ffn1_wo', 'new_m_mix_norm', 'new_m_w_in', 'new_m_ssm_lambda_re', 'new_m_ssm_lambda_im', 'new_m_ssm_log_dt', 'new_m_ssm_b_re', 'new_m_ssm_b_im', 'new_m_ssm_c_re', 'new_m_ssm_c_im', 'new_m_ssm_d', 'new_m_ssm_w_glu', 'new_m_pool_w', 'new_m_pool_scale', 'new_m_w_out', 'new_m_ffn2_norm', 'new_m_ffn2_wi', 'new_m_ffn2_wo', 'new_m_ple_norm', 'new_m_ple_w_gate', 'new_m_ple_w_proj', 'new_m_final_norm', 'new_v_ffn1_norm', 'new_v_ffn1_wi', 'new_v_ffn1_wo', 'new_v_mix_norm', 'new_v_w_in', 'new_v_ssm_lambda_re', 'new_v_ssm_lambda_im', 'new_v_ssm_log_dt', 'new_v_ssm_b_re', 'new_v_ssm_b_im', 'new_v_ssm_c_re', 'new_v_ssm_c_im', 'new_v_ssm_d', 'new_v_ssm_w_glu', 'new_v_pool_w', 'new_v_pool_scale', 'new_v_w_out', 'new_v_ffn2_norm', 'new_v_ffn2_wi', 'new_v_ffn2_wo', 'new_v_ple_norm', 'new_v_ple_w_gate', 'new_v_ple_w_proj', 'new_v_final_norm']
TWIN_LEAF_KINDS = {'loss': 'loss', 'grad_x': 'grad_x', 'grad_ffn1_norm': 'grad_w', 'grad_ffn1_wi': 'grad_w', 'grad_ffn1_wo': 'grad_w', 'grad_mix_norm': 'grad_w', 'grad_w_in': 'grad_w', 'grad_ssm_lambda_re': 'grad_w', 'grad_ssm_lambda_im': 'grad_w', 'grad_ssm_log_dt': 'grad_w', 'grad_ssm_b_re': 'grad_w', 'grad_ssm_b_im': 'grad_w', 'grad_ssm_c_re': 'grad_w', 'grad_ssm_c_im': 'grad_w', 'grad_ssm_d': 'grad_w', 'grad_ssm_w_glu': 'grad_w', 'grad_pool_w': 'grad_w', 'grad_pool_scale': 'grad_w', 'grad_w_out': 'grad_w', 'grad_ffn2_norm': 'grad_w', 'grad_ffn2_wi': 'grad_w', 'grad_ffn2_wo': 'grad_w', 'grad_ple_norm': 'grad_w', 'grad_ple_w_gate': 'grad_w', 'grad_ple_w_proj': 'grad_w', 'grad_final_norm': 'grad_w', 'delta_ffn1_norm': 'delta_w', 'delta_ffn1_wi': 'delta_w', 'delta_ffn1_wo': 'delta_w', 'delta_mix_norm': 'delta_w', 'delta_w_in': 'delta_w', 'delta_ssm_lambda_re': 'delta_w', 'delta_ssm_lambda_im': 'delta_w', 'delta_ssm_log_dt': 'delta_w', 'delta_ssm_b_re': 'delta_w', 'delta_ssm_b_im': 'delta_w', 'delta_ssm_c_re': 'delta_w', 'delta_ssm_c_im': 'delta_w', 'delta_ssm_d': 'delta_w', 'delta_ssm_w_glu': 'delta_w', 'delta_pool_w': 'delta_w', 'delta_pool_scale': 'delta_w', 'delta_w_out': 'delta_w', 'delta_ffn2_norm': 'delta_w', 'delta_ffn2_wi': 'delta_w', 'delta_ffn2_wo': 'delta_w', 'delta_ple_norm': 'delta_w', 'delta_ple_w_gate': 'delta_w', 'delta_ple_w_proj': 'delta_w', 'delta_final_norm': 'delta_w', 'new_m_ffn1_norm': 'new_m', 'new_m_ffn1_wi': 'new_m', 'new_m_ffn1_wo': 'new_m', 'new_m_mix_norm': 'new_m', 'new_m_w_in': 'new_m', 'new_m_ssm_lambda_re': 'new_m', 'new_m_ssm_lambda_im': 'new_m', 'new_m_ssm_log_dt': 'new_m', 'new_m_ssm_b_re': 'new_m', 'new_m_ssm_b_im': 'new_m', 'new_m_ssm_c_re': 'new_m', 'new_m_ssm_c_im': 'new_m', 'new_m_ssm_d': 'new_m', 'new_m_ssm_w_glu': 'new_m', 'new_m_pool_w': 'new_m', 'new_m_pool_scale': 'new_m', 'new_m_w_out': 'new_m', 'new_m_ffn2_norm': 'new_m', 'new_m_ffn2_wi': 'new_m', 'new_m_ffn2_wo': 'new_m', 'new_m_ple_norm': 'new_m', 'new_m_ple_w_gate': 'new_m', 'new_m_ple_w_proj': 'new_m', 'new_m_final_norm': 'new_m', 'new_v_ffn1_norm': 'new_v', 'new_v_ffn1_wi': 'new_v', 'new_v_ffn1_wo': 'new_v', 'new_v_mix_norm': 'new_v', 'new_v_w_in': 'new_v', 'new_v_ssm_lambda_re': 'new_v', 'new_v_ssm_lambda_im': 'new_v', 'new_v_ssm_log_dt': 'new_v', 'new_v_ssm_b_re': 'new_v', 'new_v_ssm_b_im': 'new_v', 'new_v_ssm_c_re': 'new_v', 'new_v_ssm_c_im': 'new_v', 'new_v_ssm_d': 'new_v', 'new_v_ssm_w_glu': 'new_v', 'new_v_pool_w': 'new_v', 'new_v_pool_scale': 'new_v', 'new_v_w_out': 'new_v', 'new_v_ffn2_norm': 'new_v', 'new_v_ffn2_wi': 'new_v', 'new_v_ffn2_wo': 'new_v', 'new_v_ple_norm': 'new_v', 'new_v_ple_w_gate': 'new_v', 'new_v_ple_w_proj': 'new_v', 'new_v_final_norm': 'new_v'}


def _forward(args):
    return _fwd_reference(*[args[k] for k in FWD_PARAMS])


def _output_shape():
    out = _jax.eval_shape(lambda: _forward(_fwd_setup_inputs(0)))
    return out.shape, out.dtype

N_MICROBATCH = 1
ADAM_LR = 0.001
ADAM_B1 = 0.9
ADAM_B2 = 0.999
ADAM_EPS = 1e-08
ADAM_WD = 0.01
ADAM_STEP = 10
PER_EXAMPLE_BATCH_AXIS = {'x': 0, 'p': 1, 'loss_target': 0}
SHARED_INPUTS = []
_WEIGHT_DTYPES = {'ffn1_norm': _jnp.float32, 'ffn1_wi': _jnp.float32, 'ffn1_wo': _jnp.float32, 'mix_norm': _jnp.float32, 'w_in': _jnp.float32, 'ssm_lambda_re': _jnp.float32, 'ssm_lambda_im': _jnp.float32, 'ssm_log_dt': _jnp.float32, 'ssm_b_re': _jnp.float32, 'ssm_b_im': _jnp.float32, 'ssm_c_re': _jnp.float32, 'ssm_c_im': _jnp.float32, 'ssm_d': _jnp.float32, 'ssm_w_glu': _jnp.float32, 'pool_w': _jnp.float32, 'pool_scale': _jnp.float32, 'w_out': _jnp.float32, 'ffn2_norm': _jnp.float32, 'ffn2_wi': _jnp.float32, 'ffn2_wo': _jnp.float32, 'ple_norm': _jnp.float32, 'ple_w_gate': _jnp.float32, 'ple_w_proj': _jnp.float32, 'final_norm': _jnp.float32}
MOMENT_SCALE = {'ffn1_norm': 7.413294e-02, 'ffn1_wi': 3.150597e-02, 'ffn1_wo': 5.136713e-02, 'mix_norm': 1.138652e-01, 'w_in': 1.049229e-01, 'ssm_lambda_re': 3.587508e-03, 'ssm_lambda_im': 3.269178e-03, 'ssm_log_dt': 1.805057e+00, 'ssm_b_re': 1.973497e-03, 'ssm_b_im': 1.988556e-03, 'ssm_c_re': 4.099619e-03, 'ssm_c_im': 4.012863e-03, 'ssm_d': 5.628845e-02, 'ssm_w_glu': 1.572980e-02, 'pool_w': 1.357421e-01, 'pool_scale': 1.488632e-01, 'w_out': 1.035145e-01, 'ffn2_norm': 6.335618e-02, 'ffn2_wi': 2.681457e-02, 'ffn2_wo': 4.374033e-02, 'ple_norm': 3.106187e-02, 'ple_w_gate': 3.070662e-02, 'ple_w_proj': 7.850828e-02, 'final_norm': 6.383596e+01}


def _to_microbatches(a, axis):
    t = _jnp.moveaxis(a, axis, 0)
    t = t.reshape((N_MICROBATCH, t.shape[0] // N_MICROBATCH) + t.shape[1:])
    return _jnp.moveaxis(t, 1, axis + 1)


def setup_inputs(seed: int = 0) -> dict:
    inp = _fwd_setup_inputs(seed)
    key = _jax.random.fold_in(_jax.random.key(seed), 7919)
    shape, _ = _output_shape()
    out = dict(inp)
    out["loss_target"] = _jax.random.normal(_jax.random.fold_in(key, 0), shape, _jnp.float32)
    for i, name in enumerate(TWIN_WEIGHTS):
        w = inp[name].astype(_jnp.float32)
        if MOMENT_SCALE is None:
            s = _jnp.sqrt(_jnp.mean(_jnp.square(w)) + 1e-30)
        else:
            s = MOMENT_SCALE[name]
        km, kv = _jax.random.split(_jax.random.fold_in(key, i + 1))
        out[name] = w
        out["m_" + name] = s * _jax.random.normal(km, w.shape, _jnp.float32)
        out["v_" + name] = (s * s) * _jax.random.uniform(kv, w.shape, _jnp.float32, 0.5, 1.5)
    if N_MICROBATCH > 1:
        for name, axis in PER_EXAMPLE_BATCH_AXIS.items():
            out[name] = _to_microbatches(out[name], axis)
    return {'x': out['x'], 'p': out['p'], 'ffn1_norm': out['ffn1_norm'], 'ffn1_wi': out['ffn1_wi'], 'ffn1_wo': out['ffn1_wo'], 'mix_norm': out['mix_norm'], 'w_in': out['w_in'], 'ssm_lambda_re': out['ssm_lambda_re'], 'ssm_lambda_im': out['ssm_lambda_im'], 'ssm_log_dt': out['ssm_log_dt'], 'ssm_b_re': out['ssm_b_re'], 'ssm_b_im': out['ssm_b_im'], 'ssm_c_re': out['ssm_c_re'], 'ssm_c_im': out['ssm_c_im'], 'ssm_d': out['ssm_d'], 'ssm_w_glu': out['ssm_w_glu'], 'pool_w': out['pool_w'], 'pool_scale': out['pool_scale'], 'w_out': out['w_out'], 'ffn2_norm': out['ffn2_norm'], 'ffn2_wi': out['ffn2_wi'], 'ffn2_wo': out['ffn2_wo'], 'ple_norm': out['ple_norm'], 'ple_w_gate': out['ple_w_gate'], 'ple_w_proj': out['ple_w_proj'], 'final_norm': out['final_norm'], 'loss_target': out['loss_target'], 'm_ffn1_norm': out['m_ffn1_norm'], 'm_ffn1_wi': out['m_ffn1_wi'], 'm_ffn1_wo': out['m_ffn1_wo'], 'm_mix_norm': out['m_mix_norm'], 'm_w_in': out['m_w_in'], 'm_ssm_lambda_re': out['m_ssm_lambda_re'], 'm_ssm_lambda_im': out['m_ssm_lambda_im'], 'm_ssm_log_dt': out['m_ssm_log_dt'], 'm_ssm_b_re': out['m_ssm_b_re'], 'm_ssm_b_im': out['m_ssm_b_im'], 'm_ssm_c_re': out['m_ssm_c_re'], 'm_ssm_c_im': out['m_ssm_c_im'], 'm_ssm_d': out['m_ssm_d'], 'm_ssm_w_glu': out['m_ssm_w_glu'], 'm_pool_w': out['m_pool_w'], 'm_pool_scale': out['m_pool_scale'], 'm_w_out': out['m_w_out'], 'm_ffn2_norm': out['m_ffn2_norm'], 'm_ffn2_wi': out['m_ffn2_wi'], 'm_ffn2_wo': out['m_ffn2_wo'], 'm_ple_norm': out['m_ple_norm'], 'm_ple_w_gate': out['m_ple_w_gate'], 'm_ple_w_proj': out['m_ple_w_proj'], 'm_final_norm': out['m_final_norm'], 'v_ffn1_norm': out['v_ffn1_norm'], 'v_ffn1_wi': out['v_ffn1_wi'], 'v_ffn1_wo': out['v_ffn1_wo'], 'v_mix_norm': out['v_mix_norm'], 'v_w_in': out['v_w_in'], 'v_ssm_lambda_re': out['v_ssm_lambda_re'], 'v_ssm_lambda_im': out['v_ssm_lambda_im'], 'v_ssm_log_dt': out['v_ssm_log_dt'], 'v_ssm_b_re': out['v_ssm_b_re'], 'v_ssm_b_im': out['v_ssm_b_im'], 'v_ssm_c_re': out['v_ssm_c_re'], 'v_ssm_c_im': out['v_ssm_c_im'], 'v_ssm_d': out['v_ssm_d'], 'v_ssm_w_glu': out['v_ssm_w_glu'], 'v_pool_w': out['v_pool_w'], 'v_pool_scale': out['v_pool_scale'], 'v_w_out': out['v_w_out'], 'v_ffn2_norm': out['v_ffn2_norm'], 'v_ffn2_wi': out['v_ffn2_wi'], 'v_ffn2_wo': out['v_ffn2_wo'], 'v_ple_norm': out['v_ple_norm'], 'v_ple_w_gate': out['v_ple_w_gate'], 'v_ple_w_proj': out['v_ple_w_proj'], 'v_final_norm': out['v_final_norm']}


def _loss(weights, diff, rest, loss_target):
    with _jax.named_scope("forward"):
        args = {**rest, TWIN_DIFF_INPUT: diff, **{k: w.astype(_WEIGHT_DTYPES[k]) for k, w in weights.items()}}
        y = _forward(args)
    with _jax.named_scope("loss_head"):
        err = _jnp.square(y.astype(_jnp.float32) - loss_target)
        return 0.5 * _jnp.sum(_jnp.mean(err, axis=-1)) if err.ndim else 0.5 * err


def _adamw(w, g, m, v):
    m = ADAM_B1 * m + (1.0 - ADAM_B1) * g
    v = ADAM_B2 * v + (1.0 - ADAM_B2) * _jnp.square(g)
    m_hat = m / (1.0 - ADAM_B1 ** ADAM_STEP)
    v_hat = v / (1.0 - ADAM_B2 ** ADAM_STEP)
    delta = -ADAM_LR * (m_hat / (_jnp.sqrt(v_hat) + ADAM_EPS) + ADAM_WD * w)
    return delta, m, v


def reference(x, p, ffn1_norm, ffn1_wi, ffn1_wo, mix_norm, w_in, ssm_lambda_re, ssm_lambda_im, ssm_log_dt, ssm_b_re, ssm_b_im, ssm_c_re, ssm_c_im, ssm_d, ssm_w_glu, pool_w, pool_scale, w_out, ffn2_norm, ffn2_wi, ffn2_wo, ple_norm, ple_w_gate, ple_w_proj, final_norm, loss_target, m_ffn1_norm, m_ffn1_wi, m_ffn1_wo, m_mix_norm, m_w_in, m_ssm_lambda_re, m_ssm_lambda_im, m_ssm_log_dt, m_ssm_b_re, m_ssm_b_im, m_ssm_c_re, m_ssm_c_im, m_ssm_d, m_ssm_w_glu, m_pool_w, m_pool_scale, m_w_out, m_ffn2_norm, m_ffn2_wi, m_ffn2_wo, m_ple_norm, m_ple_w_gate, m_ple_w_proj, m_final_norm, v_ffn1_norm, v_ffn1_wi, v_ffn1_wo, v_mix_norm, v_w_in, v_ssm_lambda_re, v_ssm_lambda_im, v_ssm_log_dt, v_ssm_b_re, v_ssm_b_im, v_ssm_c_re, v_ssm_c_im, v_ssm_d, v_ssm_w_glu, v_pool_w, v_pool_scale, v_w_out, v_ffn2_norm, v_ffn2_wi, v_ffn2_wo, v_ple_norm, v_ple_w_gate, v_ple_w_proj, v_final_norm):
    given = dict(x=x, p=p, ffn1_norm=ffn1_norm, ffn1_wi=ffn1_wi, ffn1_wo=ffn1_wo, mix_norm=mix_norm, w_in=w_in, ssm_lambda_re=ssm_lambda_re, ssm_lambda_im=ssm_lambda_im, ssm_log_dt=ssm_log_dt, ssm_b_re=ssm_b_re, ssm_b_im=ssm_b_im, ssm_c_re=ssm_c_re, ssm_c_im=ssm_c_im, ssm_d=ssm_d, ssm_w_glu=ssm_w_glu, pool_w=pool_w, pool_scale=pool_scale, w_out=w_out, ffn2_norm=ffn2_norm, ffn2_wi=ffn2_wi, ffn2_wo=ffn2_wo, ple_norm=ple_norm, ple_w_gate=ple_w_gate, ple_w_proj=ple_w_proj, final_norm=final_norm, loss_target=loss_target, m_ffn1_norm=m_ffn1_norm, m_ffn1_wi=m_ffn1_wi, m_ffn1_wo=m_ffn1_wo, m_mix_norm=m_mix_norm, m_w_in=m_w_in, m_ssm_lambda_re=m_ssm_lambda_re, m_ssm_lambda_im=m_ssm_lambda_im, m_ssm_log_dt=m_ssm_log_dt, m_ssm_b_re=m_ssm_b_re, m_ssm_b_im=m_ssm_b_im, m_ssm_c_re=m_ssm_c_re, m_ssm_c_im=m_ssm_c_im, m_ssm_d=m_ssm_d, m_ssm_w_glu=m_ssm_w_glu, m_pool_w=m_pool_w, m_pool_scale=m_pool_scale, m_w_out=m_w_out, m_ffn2_norm=m_ffn2_norm, m_ffn2_wi=m_ffn2_wi, m_ffn2_wo=m_ffn2_wo, m_ple_norm=m_ple_norm, m_ple_w_gate=m_ple_w_gate, m_ple_w_proj=m_ple_w_proj, m_final_norm=m_final_norm, v_ffn1_norm=v_ffn1_norm, v_ffn1_wi=v_ffn1_wi, v_ffn1_wo=v_ffn1_wo, v_mix_norm=v_mix_norm, v_w_in=v_w_in, v_ssm_lambda_re=v_ssm_lambda_re, v_ssm_lambda_im=v_ssm_lambda_im, v_ssm_log_dt=v_ssm_log_dt, v_ssm_b_re=v_ssm_b_re, v_ssm_b_im=v_ssm_b_im, v_ssm_c_re=v_ssm_c_re, v_ssm_c_im=v_ssm_c_im, v_ssm_d=v_ssm_d, v_ssm_w_glu=v_ssm_w_glu, v_pool_w=v_pool_w, v_pool_scale=v_pool_scale, v_w_out=v_w_out, v_ffn2_norm=v_ffn2_norm, v_ffn2_wi=v_ffn2_wi, v_ffn2_wo=v_ffn2_wo, v_ple_norm=v_ple_norm, v_ple_w_gate=v_ple_w_gate, v_ple_w_proj=v_ple_w_proj, v_final_norm=v_final_norm)
    weights = {n: given[n] for n in TWIN_WEIGHTS}
    shared = {n: given[n] for n in SHARED_INPUTS}
    per_example = {n: given[n] for n in ['x', 'p']}
    grad_fn = _jax.value_and_grad(_loss, argnums=(0, 1))

    def one_microbatch(ex, loss_target):
        ex = dict(ex)
        diff = ex.pop(TWIN_DIFF_INPUT)
        return grad_fn(weights, diff, {**shared, **ex}, loss_target)

    if N_MICROBATCH == 1:
        loss, (grad_w, grad_x) = one_microbatch(per_example, given["loss_target"])
    else:
        def body(carry, xs):
            loss_sum, grad_sum = carry
            l_k, (gw_k, gx_k) = one_microbatch(xs[0], xs[1])
            with _jax.named_scope("update"):
                return (loss_sum + l_k, _jax.tree.map(_jnp.add, grad_sum, gw_k)), gx_k

        init = (_jnp.zeros((), _jnp.float32), _jax.tree.map(_jnp.zeros_like, weights))
        (loss, grad_w), grad_x = _jax.lax.scan(body, init, (per_example, given["loss_target"]))
    with _jax.named_scope("update"):
        delta_w, new_m, new_v = {}, {}, {}
        for n in TWIN_WEIGHTS:
            delta_w[n], new_m[n], new_v[n] = _adamw(weights[n], grad_w[n], given["m_" + n], given["v_" + n])
    return (loss, grad_x, *[grad_w[n] for n in TWIN_WEIGHTS], *[delta_w[n] for n in TWIN_WEIGHTS],
            *[new_m[n] for n in TWIN_WEIGHTS], *[new_v[n] for n in TWIN_WEIGHTS])
```

```python
import math

import jax
import jax.numpy as jnp
from jax import lax
from jax.experimental import pallas as pl
from jax.experimental.pallas import tpu as pltpu

F32 = jnp.float32
BF16 = jnp.bfloat16
MESH = pl.DeviceIdType.MESH
ANY = pl.BlockSpec(memory_space=pl.ANY)

N_DEV = 8
DEPTH = 4
EPS = 1e-6
SSM_GROUPS = 32
SSM_GROUP_CH = 16
SSM_STATE = 64
SSM_CH = SSM_GROUPS * SSM_STATE
POOL_WINDOWS = (2, 4, 8, 16)
POOL_HALO = 16
ADAM_LR, ADAM_B1, ADAM_B2, ADAM_EPS, ADAM_WD, ADAM_STEP = 0.001, 0.9, 0.999, 1e-08, 0.01, 10

V7X_VMEM_BYTES = 64 * 1024 * 1024
VMEM_LIMIT_BYTES = V7X_VMEM_BYTES - 12 * 1024 * 1024
LANES = 128
SUBLANES = 8
PACK = SUBLANES * LANES


def _params(*sem):
    return pltpu.CompilerParams(dimension_semantics=sem or None, vmem_limit_bytes=VMEM_LIMIT_BYTES)


def _tile(n, pref):
    if n <= pref:
        return n
    t = pref - pref % LANES
    while t >= LANES:
        if n % t == 0:
            return t
        t -= LANES
    raise ValueError(f"no lane-aligned tile for {n}")


def _row_tile(rows, pref):
    if rows <= pref:
        return rows
    t = pref - pref % SUBLANES
    while t >= SUBLANES:
        if rows % t == 0:
            return t
        t -= SUBLANES
    raise ValueError(f"no sublane-aligned tile for {rows}")


_DIMS = {"nn": ((1,), (0,)), "nt": ((1,), (1,)), "tn": ((0,), (0,))}


def matmul(a, b, *, mode, name, out_dtypes=None, epi=None, extras=(), separate=False, tm=1024, tn=1024, tk=1024):
    a_list = list(a) if isinstance(a, (list, tuple)) else [a]
    b_list = list(b) if isinstance(b, (list, tuple)) else [b]
    n_terms = max(len(a_list), len(b_list))
    if len(a_list) == 1:
        a_idx = [0] * n_terms
    else:
        a_idx = list(range(n_terms))
    n_acc = n_terms if separate else 1
    if out_dtypes is None:
        out_dtypes = (F32,) * (n_acc if epi is None else 1)
    if mode == "tn":
        K, M = a_list[0].shape
        K2, N = b_list[0].shape
    elif mode == "nt":
        M, K = a_list[0].shape
        N, K2 = b_list[0].shape
    else:
        M, K = a_list[0].shape
        K2, N = b_list[0].shape
    assert K == K2, (name, a_list[0].shape, b_list[0].shape)
    tm, tn, tk = _tile(M, tm), _tile(N, tn), _tile(K, tk)
    nk = K // tk
    if mode == "tn":
        a_spec = pl.BlockSpec((tk, tm), lambda i, j, k: (k, i))
    else:
        a_spec = pl.BlockSpec((tm, tk), lambda i, j, k: (i, k))
    if mode == "nt":
        b_spec = pl.BlockSpec((tn, tk), lambda i, j, k: (j, k))
    else:
        b_spec = pl.BlockSpec((tk, tn), lambda i, j, k: (k, j))
    ex_specs = []
    for e in extras:
        if e.shape == (M, N):
            ex_specs.append(pl.BlockSpec((tm, tn), lambda i, j, k: (i, j)))
        elif e.shape == (1, N):
            ex_specs.append(pl.BlockSpec((1, tn), lambda i, j, k: (0, j)))
        elif e.shape == (M, 1):
            ex_specs.append(pl.BlockSpec((tm, 1), lambda i, j, k: (i, 0)))
        else:
            raise ValueError((name, e.shape, (M, N)))
    na, nb, ne, no = len(a_list), len(b_list), len(extras), len(out_dtypes)
    dims = (_DIMS[mode], ((), ()))

    def body(*refs):
        a_refs = refs[:na]
        b_refs = refs[na:na + nb]
        ex_refs = refs[na + nb:na + nb + ne]
        out_refs = refs[na + nb + ne:na + nb + ne + no]
        acc_refs = refs[na + nb + ne + no:]
        a_vals = [r[...].astype(BF16) for r in a_refs]
        prods = [lax.dot_general(a_vals[a_idx[t]], b_refs[t][...].astype(BF16), dims, preferred_element_type=F32)
                 for t in range(n_terms)]
        if not separate:
            total = prods[0]
            for p_ in prods[1:]:
                total = total + p_
            prods = [total]

        def finish(accs):
            res = epi(*accs, *[e[...] for e in ex_refs]) if epi is not None else tuple(accs)
            for r, v in zip(out_refs, res):
                r[...] = v.astype(r.dtype)

        if nk == 1:
            finish(prods)
        else:
            k = pl.program_id(2)

            @pl.when(k == 0)
            def _():
                for r, v in zip(acc_refs, prods):
                    r[...] = v

            @pl.when(k > 0)
            def _():
                for r, v in zip(acc_refs, prods):
                    r[...] += v

            @pl.when(k == nk - 1)
            def _():
                finish([r[...] for r in acc_refs])

    outs = pl.pallas_call(
        body,
        name=name,
        grid=(M // tm, N // tn, nk),
        in_specs=[a_spec] * na + [b_spec] * nb + ex_specs,
        out_specs=[pl.BlockSpec((tm, tn), lambda i, j, k: (i, j))] * no,
        out_shape=[jax.ShapeDtypeStruct((M, N), dt) for dt in out_dtypes],
        scratch_shapes=[pltpu.VMEM((tm, tn), F32)] * (n_acc if nk > 1 else 0),
        compiler_params=_params("parallel", "parallel", "arbitrary"),
    )(*a_list, *b_list, *extras)
    return outs


def rowwise(fn, ins, outs, accs=(), *, name, tr=512):
    R = max(x.shape[0] for x in ins)
    tr = _row_tile(R, tr)
    in_specs = []
    for x in ins:
        if x.shape[0] == R and x.ndim == 2:
            in_specs.append(pl.BlockSpec((tr, x.shape[1]), lambda i: (i, 0)))
        else:
            in_specs.append(pl.BlockSpec(x.shape, lambda i, _n=x.ndim: (0,) * _n))
    ni, no = len(ins), len(outs)

    def body(*refs):
        i = pl.program_id(0)
        row_vals, acc_vals = fn(*[r[...] for r in refs[:ni]])
        for r, v in zip(refs[ni:ni + no], row_vals):
            r[...] = v.astype(r.dtype)
        for r, v in zip(refs[ni + no:], acc_vals):
            @pl.when(i == 0)
            def _(r=r, v=v):
                r[...] = v

            @pl.when(i > 0)
            def _(r=r, v=v):
                r[...] += v

    return pl.pallas_call(
        body,
        name=name,
        grid=(R // tr,),
        in_specs=in_specs,
        out_specs=[pl.BlockSpec((tr, c), lambda i: (i, 0)) for c, _ in outs]
        + [pl.BlockSpec(s, lambda i: (0, 0)) for s in accs],
        out_shape=[jax.ShapeDtypeStruct((R, c), dt) for c, dt in outs]
        + [jax.ShapeDtypeStruct(s, F32) for s in accs],
        compiler_params=_params("arbitrary"),
    )(*ins)


def _sigmoid(x):
    return 1.0 / (1.0 + jnp.exp(-x))


_GELU_C = math.sqrt(2.0 / math.pi)


def _gelu(x):
    return 0.5 * x * (1.0 + jnp.tanh(_GELU_C * (x + 0.044715 * (x * x * x))))


def _gelu_grad(x):
    t = jnp.tanh(_GELU_C * (x + 0.044715 * (x * x * x)))
    return 0.5 * (1.0 + t) + 0.5 * x * (1.0 - t * t) * (_GELU_C * (1.0 + 3.0 * 0.044715 * (x * x)))


def rms_fwd(x, g, name):
    def fn(x, g):
        r = lax.rsqrt(jnp.mean(x * x, axis=-1, keepdims=True) + EPS)
        return [x * r * g], []

    return rowwise(fn, [x, g], [(x.shape[1], BF16)], name=name)[0]


def rms_bwd(x, g, dn, dres, name):
    def fn(x, g, dn, dres):
        r = lax.rsqrt(jnp.mean(x * x, axis=-1, keepdims=True) + EPS)
        w = dn * g
        dx = r * w - x * (r * r * r) * jnp.mean(x * w, axis=-1, keepdims=True)
        return [dres + dx], [jnp.sum(dn * (x * r), axis=0, keepdims=True)]

    d = x.shape[1]
    return rowwise(fn, [x, g, dn, dres], [(d, F32)], [(1, d)], name=name)


def _whole(shape):
    return pl.BlockSpec(shape, lambda: (0,) * len(shape))


def _zoh(lr, li, ldt):
    dt = jnp.exp(ldt)
    mag = jnp.exp(lr * dt)
    ar, ai = mag * jnp.cos(li * dt), mag * jnp.sin(li * dt)
    den = lr * lr + li * li
    kr = ((ar - 1.0) * lr + ai * li) / den
    ki = (ai * lr - (ar - 1.0) * li) / den
    return dt, ar, ai, den, kr, ki


def ssm_prep(lam_re, lam_im, log_dt, b_re, b_im, name):
    n = lam_re.shape[0]

    def body(lr_ref, li_ref, ldt_ref, br_ref, bi_ref, ar_ref, ai_ref, bbr_ref, bbi_ref):
        _, ar, ai, _, kr, ki = _zoh(lr_ref[...], li_ref[...], ldt_ref[...])
        br, bi = br_ref[...], bi_ref[...]
        ar_ref[...] = ar
        ai_ref[...] = ai
        bbr_ref[...] = kr * br - ki * bi
        bbi_ref[...] = kr * bi + ki * br

    col, mat = (n, 1), (n, SSM_GROUP_CH)
    return pl.pallas_call(
        body, name=name,
        in_specs=[_whole(col)] * 3 + [_whole(mat)] * 2,
        out_specs=[_whole(col)] * 2 + [_whole(mat)] * 2,
        out_shape=[jax.ShapeDtypeStruct(col, F32)] * 2 + [jax.ShapeDtypeStruct(mat, F32)] * 2,
        compiler_params=_params(),
    )(lam_re, lam_im, log_dt, b_re, b_im)


def ssm_prep_bwd(lam_re, lam_im, log_dt, b_re, b_im, d_ar, d_ai, d_bbr, d_bbi, name):
    n = lam_re.shape[0]
    n_groups = n // SSM_STATE

    def body(lr_ref, li_ref, ldt_ref, br_ref, bi_ref, dar_ref, dai_ref, dbr_ref, dbi_ref,
             glr_ref, gli_ref, gdt_ref, gbr_ref, gbi_ref):
        lr, li = lr_ref[...], li_ref[...]
        dt, ar, ai, den, kr, ki = _zoh(lr, li, ldt_ref[...])
        br, bi, dbr, dbi = br_ref[...], bi_ref[...], dbr_ref[...], dbi_ref[...]
        gbr_ref[...] = kr * dbr + ki * dbi
        gbi_ref[...] = kr * dbi - ki * dbr
        gkr = jnp.sum(br * dbr + bi * dbi, axis=1, keepdims=True)
        gki = jnp.sum(br * dbi - bi * dbr, axis=1, keepdims=True)
        gar = dar_ref[...] + (gkr * lr - gki * li) / den
        gai = dai_ref[...] + (gki * lr + gkr * li) / den
        qr, qi = -(kr * lr + ki * li) / den, -(ki * lr - kr * li) / den
        g1r, g1i = qr * gkr + qi * gki, qr * gki - qi * gkr
        g2r, g2i = dt * (ar * gar + ai * gai), dt * (ar * gai - ai * gar)
        glr_ref[...] = g1r + g2r
        gli_ref[...] = g1i + g2i
        pr, pi_ = lr * ar - li * ai, lr * ai + li * ar
        gdt = (pr * gar + pi_ * gai) * dt
        grp = lax.broadcasted_iota(jnp.int32, (n, n_groups), 0) // SSM_STATE
        sel = grp == lax.broadcasted_iota(jnp.int32, (n, n_groups), 1)
        gdt_ref[...] = jnp.sum(jnp.where(sel, gdt, 0.0), axis=0, keepdims=True)

    col, mat = (n, 1), (n, SSM_GROUP_CH)
    return pl.pallas_call(
        body, name=name,
        in_specs=[_whole(col)] * 3 + [_whole(mat)] * 2 + [_whole(col)] * 2 + [_whole(mat)] * 2,
        out_specs=[_whole(col)] * 2 + [_whole((1, n_groups))] + [_whole(mat)] * 2,
        out_shape=[jax.ShapeDtypeStruct(col, F32)] * 2 + [jax.ShapeDtypeStruct((1, n_groups), F32)]
        + [jax.ShapeDtypeStruct(mat, F32)] * 2,
        compiler_params=_params(),
    )(lam_re, lam_im, log_dt, b_re, b_im, d_ar, d_ai, d_bbr, d_bbi)


def _cmul(ar, ai, br, bi):
    return ar * br - ai * bi, ar * bi + ai * br


def _scan_block(xr, xi, lr, li, carry_r, carry_i, or_ref, oi_ref, loc_r, loc_i, reverse):
    tb, cb = xr.shape
    ng = tb // SUBLANES
    xr = xr.reshape(ng, SUBLANES, cb)
    xi = xi.reshape(ng, SUBLANES, cb)
    rid = lax.broadcasted_iota(jnp.int32, (ng, SUBLANES, cb), 1)
    pr, pi_ = lr.reshape(1, 1, cb), li.reshape(1, 1, cb)
    powers = []
    for k in (1, 2, 4):
        powers.append((pr, pi_))
        shift = SUBLANES - k if reverse else k
        sr, si = pltpu.roll(xr, shift, 1), pltpu.roll(xi, shift, 1)
        keep = (rid < SUBLANES - k) if reverse else (rid >= k)
        tr_, ti_ = _cmul(pr, pi_, sr, si)
        xr = xr + jnp.where(keep, tr_, 0.0)
        xi = xi + jnp.where(keep, ti_, 0.0)
        pr, pi_ = _cmul(pr, pi_, pr, pi_)
    loc_r[...] = xr
    loc_i[...] = xi
    (p1r, p1i), (p2r, p2i), (p4r, p4i) = powers
    dist = lax.broadcasted_iota(jnp.int32, (SUBLANES, cb), 0)
    if reverse:
        dist = SUBLANES - 1 - dist
    wr = jnp.broadcast_to(p1r.reshape(1, cb), (SUBLANES, cb))
    wi = jnp.broadcast_to(p1i.reshape(1, cb), (SUBLANES, cb))
    for bit, (qr, qi) in ((1, (p1r, p1i)), (2, (p2r, p2i)), (4, (p4r, p4i))):
        mr, mi = _cmul(wr, wi, qr.reshape(1, cb), qi.reshape(1, cb))
        on = (dist & bit) != 0
        wr, wi = jnp.where(on, mr, wr), jnp.where(on, mi, wi)
    last = 0 if reverse else SUBLANES - 1

    def step(j, carry):
        cr, ci = carry
        g = (ng - 1 - j) if reverse else j
        fr = loc_r[g] + (wr * cr - wi * ci)
        fi = loc_i[g] + (wr * ci + wi * cr)
        rows = pl.ds(pl.multiple_of(g * SUBLANES, SUBLANES), SUBLANES)
        or_ref[rows, :] = fr
        oi_ref[rows, :] = fi
        return fr[last:last + 1, :], fi[last:last + 1, :]

    cr, ci = lax.fori_loop(0, ng, step, (carry_r[...], carry_i[...]))
    carry_r[...] = cr
    carry_i[...] = ci


def _scan_tiles(seq_len, n_ch):
    return min(256, seq_len), min(512, n_ch)


def ssm_scan(bu, lam, batch, name):
    n, c2 = bu.shape
    nch = c2 // 2
    seq = n // batch
    tb, cb = _scan_tiles(seq, nch)
    nt, nc = seq // tb, nch // cb

    def body(xr_ref, xi_ref, lr_ref, li_ref, or_ref, oi_ref, car_r, car_i, loc_r, loc_i):
        @pl.when(pl.program_id(2) == 0)
        def _():
            car_r[...] = jnp.zeros_like(car_r)
            car_i[...] = jnp.zeros_like(car_i)

        _scan_block(xr_ref[...], xi_ref[...], lr_ref[...], li_ref[...], car_r, car_i, or_ref, oi_ref,
                    loc_r, loc_i, reverse=False)

    re_spec = pl.BlockSpec((tb, cb), lambda c, b, t: (b * nt + t, c))
    im_spec = pl.BlockSpec((tb, cb), lambda c, b, t: (b * nt + t, c + nc))
    lre_spec = pl.BlockSpec((1, cb), lambda c, b, t: (0, c))
    lim_spec = pl.BlockSpec((1, cb), lambda c, b, t: (0, c + nc))
    sr, si = pl.pallas_call(
        body, name=name,
        grid=(nc, batch, nt),
        in_specs=[re_spec, im_spec, lre_spec, lim_spec],
        out_specs=[pl.BlockSpec((tb, cb), lambda c, b, t: (b * nt + t, c))] * 2,
        out_shape=[jax.ShapeDtypeStruct((n, nch), F32)] * 2,
        scratch_shapes=[pltpu.VMEM((1, cb), F32)] * 2 + [pltpu.VMEM((tb // SUBLANES, SUBLANES, cb), F32)] * 2,
        compiler_params=_params("parallel", "arbitrary", "arbitrary"),
    )(bu, bu, lam, lam)
    return sr, si


def ssm_scan_bwd(gd, s_re, s_im, lam, batch, name):
    n, c2 = gd.shape
    nch = c2 // 2
    seq = n // batch
    tb, cb = _scan_tiles(seq, nch)
    nt, nc = seq // tb, nch // cb
    hb = tb // SUBLANES

    def body(xr_ref, xi_ref, sr_ref, si_ref, hr_ref, hi_ref, lr_ref, li_ref, or_ref, oi_ref, dlr_ref, dli_ref,
             car_r, car_i, loc_r, loc_i):
        b, t = pl.program_id(1), pl.program_id(2)

        @pl.when(t == 0)
        def _():
            car_r[...] = jnp.zeros_like(car_r)
            car_i[...] = jnp.zeros_like(car_i)

        _scan_block(xr_ref[...], xi_ref[...], lr_ref[...], -li_ref[...], car_r, car_i, or_ref, oi_ref,
                    loc_r, loc_i, reverse=True)
        first_block = t == nt - 1
        row = lax.broadcasted_iota(jnp.int32, (tb, cb), 0)
        hr = jnp.where(first_block, 0.0, hr_ref[SUBLANES - 1:SUBLANES, :])
        hi = jnp.where(first_block, 0.0, hi_ref[SUBLANES - 1:SUBLANES, :])
        pr = jnp.where(row == 0, hr, pltpu.roll(sr_ref[...], 1, 0))
        pi_ = jnp.where(row == 0, hi, pltpu.roll(si_ref[...], 1, 0))
        gr, gi = or_ref[...], oi_ref[...]
        dlr = jnp.sum(gr * pr + gi * pi_, axis=0, keepdims=True)
        dli = jnp.sum(gi * pr - gr * pi_, axis=0, keepdims=True)
        start = jnp.logical_and(b == 0, t == 0)

        @pl.when(start)
        def _():
            dlr_ref[...] = dlr
            dli_ref[...] = dli

        @pl.when(jnp.logical_not(start))
        def _():
            dlr_ref[...] += dlr
            dli_ref[...] += dli

    def blk(c, b, t):
        return b * nt + (nt - 1 - t)

    re_spec = pl.BlockSpec((tb, cb), lambda c, b, t: (blk(c, b, t), c))
    im_spec = pl.BlockSpec((tb, cb), lambda c, b, t: (blk(c, b, t), c + nc))
    st_spec = pl.BlockSpec((tb, cb), lambda c, b, t: (blk(c, b, t), c))
    halo_spec = pl.BlockSpec((SUBLANES, cb), lambda c, b, t: (jnp.maximum(blk(c, b, t) * hb - 1, 0), c))
    lre_spec = pl.BlockSpec((1, cb), lambda c, b, t: (0, c))
    lim_spec = pl.BlockSpec((1, cb), lambda c, b, t: (0, c + nc))
    acc_spec = pl.BlockSpec((1, cb), lambda c, b, t: (0, c))
    return pl.pallas_call(
        body, name=name,
        grid=(nc, batch, nt),
        in_specs=[re_spec, im_spec, st_spec, st_spec, halo_spec, halo_spec, lre_spec, lim_spec],
        out_specs=[st_spec, st_spec, acc_spec, acc_spec],
        out_shape=[jax.ShapeDtypeStruct((n, nch), F32)] * 2 + [jax.ShapeDtypeStruct((1, nch), F32)] * 2,
        scratch_shapes=[pltpu.VMEM((1, cb), F32)] * 2 + [pltpu.VMEM((tb // SUBLANES, SUBLANES, cb), F32)] * 2,
        compiler_params=_params("parallel", "arbitrary", "arbitrary"),
    )(gd, gd, s_re, s_im, s_re, s_im, lam, lam)


def _pool_tiles(seq_len):
    return min(512, seq_len)


def _window_sums(x, n_steps, forward_in_time):
    rows = x.shape[0]
    k = 1
    for _ in range(n_steps):
        x = x + pltpu.roll(x, k if forward_in_time else rows - k, 0)
        k *= 2
    return x


def pool_fwd(u, w_pool, scale, batch, name):
    n, c = u.shape
    seq = n // batch
    tb = _pool_tiles(seq)
    nt = seq // tb
    gc = c // len(POOL_WINDOWS)
    hb = tb // POOL_HALO

    def body(x_ref, halo_ref, w_ref, sc_ref, y_ref, q_ref):
        t = pl.program_id(1)
        halo = jnp.where(t == 0, 0.0, halo_ref[...])
        full = jnp.concatenate([halo, x_ref[...]], axis=0)
        pos = lax.broadcasted_iota(jnp.int32, (tb, gc), 0) + t * tb + 1
        for gi, win in enumerate(POOL_WINDOWS):
            cols = slice(gi * gc, (gi + 1) * gc)
            sums = _window_sums(full[:, cols], gi + 1, True)[POOL_HALO:, :]
            cnt = jnp.minimum(pos, win).astype(F32)
            q = sums / cnt - x_ref[:, cols]
            r = jnp.dot(q.astype(BF16), w_ref[gi].astype(BF16), preferred_element_type=F32)
            q_ref[:, cols] = q.astype(q_ref.dtype)
            y_ref[:, cols] = (r * sc_ref[:, cols]).astype(y_ref.dtype)

    return pl.pallas_call(
        body, name=name,
        grid=(batch, nt),
        in_specs=[pl.BlockSpec((tb, c), lambda b, t: (b * nt + t, 0)),
                  pl.BlockSpec((POOL_HALO, c), lambda b, t: (jnp.maximum((b * nt + t) * hb - 1, 0), 0)),
                  pl.BlockSpec(w_pool.shape, lambda b, t: (0, 0, 0)),
                  pl.BlockSpec((1, c), lambda b, t: (0, 0))],
        out_specs=[pl.BlockSpec((tb, c), lambda b, t: (b * nt + t, 0))] * 2,
        out_shape=[jax.ShapeDtypeStruct((n, c), BF16)] * 2,
        compiler_params=_params("parallel", "arbitrary"),
    )(u, u, w_pool, scale)


def pool_bwd(dy, q, w_pool, scale, batch, name):
    n, c = dy.shape
    seq = n // batch
    tb = _pool_tiles(seq)
    nt = seq // tb
    ng = len(POOL_WINDOWS)
    gc = c // ng
    hb = tb // POOL_HALO
    n_blocks = n // POOL_HALO

    def body(dy_ref, dyh_ref, q_ref, w_ref, sc_ref, du_ref, dw_ref, dsc_ref):
        b, t = pl.program_id(0), pl.program_id(1)
        last = t == nt - 1
        dy_full = jnp.concatenate([dy_ref[...], jnp.where(last, 0.0, dyh_ref[...])], axis=0)
        pos = lax.broadcasted_iota(jnp.int32, (tb + POOL_HALO, gc), 0) + t * tb + 1
        start = jnp.logical_and(b == 0, t == 0)
        for gi, win in enumerate(POOL_WINDOWS):
            cols = slice(gi * gc, (gi + 1) * gc)
            w = w_ref[gi].astype(BF16)
            dr = dy_full[:, cols] * sc_ref[:, cols]
            dq = lax.dot_general(dr.astype(BF16), w, (((1,), (1,)), ((), ())), preferred_element_type=F32)
            cnt = jnp.minimum(pos, win).astype(F32)
            back = _window_sums(dq / cnt, gi + 1, False)
            du_ref[:, cols] = back[:tb, :] - dq[:tb, :]
            qb = q_ref[:, cols]
            r = jnp.dot(qb, w, preferred_element_type=F32)
            dw = lax.dot_general(qb, dr[:tb, :].astype(BF16), (((0,), (0,)), ((), ())), preferred_element_type=F32)
            dsc = jnp.sum(dy_ref[:, cols] * r, axis=0, keepdims=True)

            @pl.when(start)
            def _(gi=gi, cols=cols, dw=dw, dsc=dsc):
                dw_ref[gi] = dw
                dsc_ref[:, cols] = dsc

            @pl.when(jnp.logical_not(start))
            def _(gi=gi, cols=cols, dw=dw, dsc=dsc):
                dw_ref[gi] += dw
                dsc_ref[:, cols] += dsc

    blk = pl.BlockSpec((tb, c), lambda b, t: (b * nt + t, 0))
    halo = pl.BlockSpec((POOL_HALO, c), lambda b, t: (jnp.minimum((b * nt + t + 1) * hb, n_blocks - 1), 0))
    return pl.pallas_call(
        body, name=name,
        grid=(batch, nt),
        in_specs=[blk, halo, blk,
                  pl.BlockSpec(w_pool.shape, lambda b, t: (0, 0, 0)),
                  pl.BlockSpec((1, c), lambda b, t: (0, 0))],
        out_specs=[blk, pl.BlockSpec(w_pool.shape, lambda b, t: (0, 0, 0)), pl.BlockSpec((1, c), lambda b, t: (0, 0))],
        out_shape=[jax.ShapeDtypeStruct((n, c), F32), jax.ShapeDtypeStruct(w_pool.shape, F32),
                   jax.ShapeDtypeStruct((1, c), F32)],
        compiler_params=_params("arbitrary", "arbitrary"),
    )(dy, dy, q, w_pool, scale)


def _place():
    return lax.axis_index("x"), lax.axis_index("y"), lax.axis_index("c")


def all_gather(arrs, name):
    na = len(arrs)

    def body(*refs):
        ins, outs = refs[:na], refs[na:2 * na]
        send_sems, recv_sems, local_sems = refs[2 * na:]
        x, y, c = _place()
        me, sibling = (x, y, c), (x, y, 1 - c)
        chips = [(1 - x, y), (x, 1 - y), (1 - x, 1 - y)]

        def slot(out, px, py, pc):
            return out.at[4 * px + 2 * py + pc]

        def copy(a, k, block, to, src=None):
            dst = slot(outs[a], *block)
            return pltpu.make_async_remote_copy(
                src_ref=dst if src is None else src, dst_ref=dst,
                send_sem=send_sems.at[a, k], recv_sem=recv_sems.at[a, k], device_id=to, device_id_type=MESH)

        started = []
        for a in range(na):
            mine = pltpu.make_async_copy(ins[a], slot(outs[a], *me), local_sems.at[a])
            mine.start()
            started.append(mine)
        sends = []
        for a in range(na):
            sends.append(copy(a, 0, me, sibling, src=ins[a]))
            sends += [copy(a, 1 + j, me, (*chip, c), src=ins[a]) for j, chip in enumerate(chips)]
        for cp in sends:
            cp.start()
        for j, chip in enumerate(chips):
            for a in range(na):
                copy(a, 1 + j, (*chip, c), me).wait_recv()
                fwd = copy(a, 4 + j, (*chip, c), sibling)
                fwd.start()
                sends.append(fwd)
        for a in range(na):
            copy(a, 0, sibling, me).wait_recv()
            for j, chip in enumerate(chips):
                copy(a, 4 + j, (*chip, 1 - c), me).wait_recv()
        for cp in sends:
            cp.wait_send()
        for cp in started:
            cp.wait()

    return pl.pallas_call(
        body, name=name,
        in_specs=[ANY] * na, out_specs=[ANY] * na,
        out_shape=[jax.ShapeDtypeStruct((N_DEV,) + a.shape, a.dtype) for a in arrs],
        scratch_shapes=[pltpu.SemaphoreType.DMA((na, 7)), pltpu.SemaphoreType.DMA((na, 7)),
                        pltpu.SemaphoreType.DMA((na,))],
    )(*arrs)


def swap_with_sibling(arrs, name):
    na = len(arrs)

    def body(*refs):
        ins, outs = refs[:na], refs[na:2 * na]
        send_sems, recv_sems = refs[2 * na:]
        x, y, c = _place()
        copies = [pltpu.make_async_remote_copy(
            src_ref=ins[a], dst_ref=outs[a], send_sem=send_sems.at[a], recv_sem=recv_sems.at[a],
            device_id=(x, y, 1 - c), device_id_type=MESH) for a in range(na)]
        for cp in copies:
            cp.start()
        for cp in copies:
            cp.wait()

    return pl.pallas_call(
        body, name=name,
        in_specs=[ANY] * na, out_specs=[ANY] * na,
        out_shape=[jax.ShapeDtypeStruct(a.shape, a.dtype) for a in arrs],
        scratch_shapes=[pltpu.SemaphoreType.DMA((na,)), pltpu.SemaphoreType.DMA((na,))],
    )(*arrs)


def scatter_to_chips(arrs, name):
    na = len(arrs)

    def body(*refs):
        ins, outs = refs[:na], refs[na:2 * na]
        send_sems, recv_sems, local_sems = refs[2 * na:]
        x, y, c = _place()
        mine = 2 * x + y
        chips = [(1 - x, y), (x, 1 - y), (1 - x, 1 - y)]
        local = [pltpu.make_async_copy(ins[a].at[mine], outs[a].at[mine], local_sems.at[a]) for a in range(na)]
        for cp in local:
            cp.start()
        sends = []
        for a in range(na):
            for j, (px, py) in enumerate(chips):
                sends.append(pltpu.make_async_remote_copy(
                    src_ref=ins[a].at[2 * px + py], dst_ref=outs[a].at[mine],
                    send_sem=send_sems.at[a, j], recv_sem=recv_sems.at[a, j],
                    device_id=(px, py, c), device_id_type=MESH))
        for cp in sends:
            cp.start()
        for a in range(na):
            for j, (px, py) in enumerate(chips):
                pltpu.make_async_remote_copy(
                    src_ref=ins[a].at[mine], dst_ref=outs[a].at[2 * px + py],
                    send_sem=send_sems.at[a, j], recv_sem=recv_sems.at[a, j],
                    device_id=(px, py, c), device_id_type=MESH).wait_recv()
        for cp in sends:
            cp.wait_send()
        for cp in local:
            cp.wait()

    return pl.pallas_call(
        body, name=name,
        in_specs=[ANY] * na, out_specs=[ANY] * na,
        out_shape=[jax.ShapeDtypeStruct(a.shape, a.dtype) for a in arrs],
        scratch_shapes=[pltpu.SemaphoreType.DMA((na, 3)), pltpu.SemaphoreType.DMA((na, 3)),
                        pltpu.SemaphoreType.DMA((na,))],
    )(*arrs)


def adamw(w, gparts, m, v, name, tr=256):
    rows, cols = w.shape
    parts = gparts.shape[0]
    tr = _row_tile(rows, tr)
    c1 =1.0 - ADAM_B1 ** ADAM_STEP
    c2 = 1.0 - ADAM_B2 ** ADAM_STEP

    def body(w_ref, g_ref, m_ref, v_ref, go_ref, d_ref, mo_ref, vo_ref):
        g = g_ref[0]
        for p_ in range(1, parts):
            g = g + g_ref[p_]
        m_new = ADAM_B1 * m_ref[...] + (1.0 - ADAM_B1) * g
        v_new = ADAM_B2 * v_ref[...] + (1.0 - ADAM_B2) * (g * g)
        m_hat = m_new / c1
        v_hat = v_new / c2
        go_ref[...] = g
        d_ref[...] = -ADAM_LR * (m_hat / (jnp.sqrt(v_hat) + ADAM_EPS) + ADAM_WD * w_ref[...])
        mo_ref[...] = m_new
        vo_ref[...] = v_new

    blk = pl.BlockSpec((tr, cols), lambda i: (i, 0))
    return pl.pallas_call(
        body, name=name,
        grid=(rows // tr,),
        in_specs=[blk, pl.BlockSpec((parts, tr, cols), lambda i: (0, i, 0)), blk, blk],
        out_specs=[blk] * 4,
        out_shape=[jax.ShapeDtypeStruct((rows, cols), F32)] * 4,
        compiler_params=_params("parallel"),
    )(w, gparts, m, v)


def add2(a, b, name):
    return rowwise(lambda a, b: ([a + b], []), [a, b], [(a.shape[1], F32)], name=name, tr=256)[0]


def _block_diag(x):
    g, a, b = x.shape
    eye = jnp.eye(g, dtype=x.dtype)
    return (x[:, :, None, :] * eye[:, None, :, None]).reshape(g * a, g * b)


def _diag_blocks(x, a, b):
    g = x.shape[0] // a
    x4 = x.reshape(g, a, g, b)
    return jnp.stack([x4[i, :, i, :] for i in range(g)], axis=0)


def _swiglu_epi(g, u):
    return g, u, g * _sigmoid(g) * u


def ffn_fwd(h, gain, wi_g, wi_u, wo, tag):
    n = rms_fwd(h, gain, f"{tag}_norm")
    g, u, act = matmul(n, [wi_g, wi_u], mode="nn", name=f"{tag}_in", separate=True, epi=_swiglu_epi,
                       out_dtypes=(BF16, BF16, BF16), tn=1408)
    out, = matmul(act, wo, mode="nn", name=f"{tag}_out", epi=lambda acc, res: (res + 0.5 * acc,), extras=[h], tk=1408)
    return out, (h, n, g, u, act)


def ffn_bwd(dh, saved, gain, wi_g, wi_u, wo, tag):
    h, n, g, u, act = saved

    def epi(acc, g, u):
        g, u = g.astype(F32), u.astype(F32)
        s = _sigmoid(g)
        da = 0.5 * acc
        return da * u * (s * (1.0 + g * (1.0 - s))), da * (g * s)

    dg, du = matmul(dh, wo, mode="nt", name=f"{tag}_out_dx", epi=epi, extras=[g, u], out_dtypes=(BF16, BF16), tn=1408)
    d_wo, = matmul(act, dh, mode="tn", name=f"{tag}_out_dw", epi=lambda acc: (0.5 * acc,), tm=1408, tk=512)
    d_wi_g, d_wi_u = matmul(n, [dg, du], mode="tn", name=f"{tag}_in_dw", separate=True, tn=1408, tk=512)
    dn, = matmul([dg, du], [wi_g, wi_u], mode="nt", name=f"{tag}_in_dx", tk=1408)
    dh_new, d_gain = rms_bwd(h, gain, dn, dh, f"{tag}_norm_bwd")
    return dh_new, d_gain, d_wi_g, d_wi_u, d_wo


def mix_fwd(h, lw, batch, tag):
    sw = h.shape[1] // 2
    n = rms_fwd(h, lw["mix_norm"], f"{tag}_norm")
    us, up = matmul(n, [lw["w_in"][:, :sw], lw["w_in"][:, sw:]], mode="nn", name=f"{tag}_in", separate=True,
                    out_dtypes=(F32, F32))
    lam_r, lam_i, bb_r, bb_i = ssm_prep(lw["lam_re"], lw["lam_im"], lw["log_dt"], lw["b_re"], lw["b_im"], f"{tag}_zoh")
    lam = jnp.concatenate([lam_r.reshape(1, SSM_CH), lam_i.reshape(1, SSM_CH)], axis=1)
    b_mat = jnp.concatenate([_block_diag(bb_r.reshape(SSM_GROUPS, SSM_STATE, SSM_GROUP_CH).transpose(0, 2, 1)),
                             _block_diag(bb_i.reshape(SSM_GROUPS, SSM_STATE, SSM_GROUP_CH).transpose(0, 2, 1))], axis=1)
    c_mat = jnp.concatenate([_block_diag(lw["c_re"].transpose(0, 2, 1)),
                             _block_diag(-lw["c_im"].transpose(0, 2, 1))], axis=0)
    bu, = matmul(us, b_mat.astype(BF16), mode="nn", name=f"{tag}_bu")
    s_re, s_im = ssm_scan(bu, lam, batch, f"{tag}_scan")
    cb = c_mat.astype(BF16)
    y0, y1 = matmul([s_re, s_im], [cb[:SSM_CH], cb[SSM_CH:]], mode="nn", name=f"{tag}_c",
                    epi=lambda acc, u, d: (acc + d * u, _gelu(acc + d * u)), extras=[us, lw["ssm_d"]],
                    out_dtypes=(F32, BF16))
    y2, gl = matmul(y1, lw["w_glu"], mode="nn", name=f"{tag}_glu",
                    epi=lambda acc, y0: (_gelu(y0) * _sigmoid(acc), acc), extras=[y0], out_dtypes=(BF16, F32))
    yp, q = pool_fwd(up, lw["pool_w"], lw["pool_scale"], batch, f"{tag}_pool")
    out, = matmul([y2, yp], [lw["w_out"][:sw], lw["w_out"][sw:]], mode="nn", name=f"{tag}_out",
                  epi=lambda acc, res: (res + acc,), extras=[h])
    return out, (h, n, us, lam, b_mat, cb, s_re, s_im, y0, y1, gl, y2, yp, q)


def mix_bwd(dh, saved, lw, batch, tag):
    h, n, us, lam, b_mat, cb, s_re, s_im, y0, y1, gl, y2, yp, q = saved
    sw = h.shape[1] // 2
    w_out_s, w_out_p = lw["w_out"][:sw], lw["w_out"][sw:]
    d_wo_s, = matmul(y2, dh, mode="tn", name=f"{tag}_out_dw_s", tk=512)
    d_wo_p, = matmul(yp, dh, mode="tn", name=f"{tag}_out_dw_p", tk=512)
    dy2, dyp = (matmul(dh, w, mode="nt", name=f"{tag}_out_dx_{k}")[0] for k, w in (("s", w_out_s), ("p", w_out_p)))
    dup, d_pool_w, d_pool_scale = pool_bwd(dyp, q, lw["pool_w"], lw["pool_scale"], batch, f"{tag}_pool_bwd")
    def glu_fn(dy2, y0, gl):
        sg = _sigmoid(gl)
        return [dy2 * _gelu(y0) * sg * (1.0 - sg)], []

    tg, = rowwise(glu_fn, [dy2, y0, gl], [(sw, BF16)], name=f"{tag}_glu_bwd")

    def dy0_epi(acc, dy2, gl, y0):
        return ((acc + dy2 * _sigmoid(gl)) * _gelu_grad(y0),)

    dy0, = matmul(tg, lw["w_glu"], mode="nt", name=f"{tag}_glu_dx", epi=dy0_epi, extras=[dy2, gl, y0])
    d_w_glu, = matmul(y1, tg, mode="tn", name=f"{tag}_glu_dw", tk=512)
    d_d, = rowwise(lambda dy0, u: ([], [jnp.sum(dy0 * u, axis=0, keepdims=True)]), [dy0, us], [], [(1, sw)],
                     name=f"{tag}_d_skip_dw")
    gd, = matmul(dy0, cb, mode="nt", name=f"{tag}_c_dx")
    d_c_top, = matmul(s_re, dy0, mode="tn", name=f"{tag}_c_dw_re", tk=512)
    d_c_bot, = matmul(s_im, dy0, mode="tn", name=f"{tag}_c_dw_im", tk=512)
    g_re, g_im, d_lam_r, d_lam_i = ssm_scan_bwd(gd, s_re, s_im, lam, batch, f"{tag}_scan_bwd")
    bmat_re, bmat_im = b_mat[:, :SSM_CH].astype(BF16), b_mat[:, SSM_CH:].astype(BF16)
    dus, = matmul([g_re, g_im], [bmat_re, bmat_im], mode="nt", name=f"{tag}_bu_dx",
                  epi=lambda acc, dy0, d: (acc + d * dy0,), extras=[dy0, lw["ssm_d"]])
    d_b_re, d_b_im = (matmul(us, g_, mode="tn", name=f"{tag}_bu_dw_{k}", tk=512)[0] for k, g_ in (("re", g_re), ("im", g_im)))
    d_bb_r = _diag_blocks(d_b_re, SSM_GROUP_CH, SSM_STATE).transpose(0, 2, 1).reshape(SSM_CH, SSM_GROUP_CH)
    d_bb_i = _diag_blocks(d_b_im, SSM_GROUP_CH, SSM_STATE).transpose(0, 2, 1).reshape(SSM_CH, SSM_GROUP_CH)
    d_lr, d_li, d_ldt, d_br, d_bi = ssm_prep_bwd(
        lw["lam_re"], lw["lam_im"], lw["log_dt"], lw["b_re"], lw["b_im"],
        d_lam_r.reshape(SSM_CH, 1), d_lam_i.reshape(SSM_CH, 1), d_bb_r, d_bb_i, f"{tag}_zoh_bwd")
    d_c_re = _diag_blocks(d_c_top, SSM_STATE, SSM_GROUP_CH).transpose(0, 2, 1)
    d_c_im = -_diag_blocks(d_c_bot, SSM_STATE, SSM_GROUP_CH).transpose(0, 2, 1)
    d_w_in_s, d_w_in_p = matmul(n, [dus, dup], mode="tn", name=f"{tag}_in_dw", separate=True, tk=512)
    dn, = matmul([dus, dup], [lw["w_in"][:, :sw], lw["w_in"][:, sw:]], mode="nt", name=f"{tag}_in_dx")
    dh_new, d_gain = rms_bwd(h, lw["mix_norm"], dn, dh, f"{tag}_norm_bwd")
    grads = dict(mix_norm=d_gain, w_in=jnp.concatenate([d_w_in_s, d_w_in_p], axis=1),
                 ssm_lambda_re=d_lr, ssm_lambda_im=d_li, ssm_log_dt=d_ldt, ssm_b_re=d_br, ssm_b_im=d_bi,
                 ssm_c_re=d_c_re, ssm_c_im=d_c_im, ssm_d=d_d, ssm_w_glu=d_w_glu, pool_w=d_pool_w,
                 pool_scale=d_pool_scale, w_out=jnp.concatenate([d_wo_s, d_wo_p], axis=0))
    return dh_new, grads


def ple_fwd(h, p, gain, w_gate, w_proj, tag):
    n = rms_fwd(h, gain, f"{tag}_norm")
    e, = matmul(p, w_proj, mode="nn", name=f"{tag}_proj")
    out, pre = matmul(n, w_gate, mode="nn", name=f"{tag}_gate",
                      epi=lambda acc, e, res: (res + _sigmoid(acc) * e, acc), extras=[e, h], out_dtypes=(F32, F32))
    return out, (h, n, e, pre)


def ple_bwd(dh, saved, p, gain, w_gate, tag):
    h, n, e, pre = saved
    d = h.shape[1]

    def fn(dh, e, pre):
        s = _sigmoid(pre)
        return [dh * e * s * (1.0 - s), dh * s], []

    dpre, de = rowwise(fn, [dh, e, pre], [(d, BF16), (d, BF16)], name=f"{tag}_gate_bwd")
    d_w_gate, = matmul(n, dpre, mode="tn", name=f"{tag}_gate_dw", tk=512)
    d_w_proj, = matmul(p, de, mode="tn", name=f"{tag}_proj_dw", tk=512)
    dn, = matmul(dpre, w_gate, mode="nt", name=f"{tag}_gate_dx")
    dh_new, d_gain = rms_bwd(h, gain, dn, dh, f"{tag}_norm_bwd")
    return dh_new, d_gain, d_w_gate, d_w_proj


def loss_head(h, gain, target, name):
    d = h.shape[1]

    def fn(h, g, t):
        r = lax.rsqrt(jnp.mean(h * h, axis=-1, keepdims=True) + EPS)
        diff = h * r * g - t
        sq = jnp.sum(jnp.sum(diff * diff, axis=1, keepdims=True), axis=0, keepdims=True)
        dy = diff * (1.0 / d)
        w = dy * g
        dh = r * w - h * (r * r * r) * jnp.mean(h * w, axis=-1, keepdims=True)
        return [dh], [sq, jnp.sum(dy * (h * r), axis=0, keepdims=True)]

    dh, sq, d_gain = rowwise(fn, [h, gain, target], [(d, F32)], [(1, 1), (1, d)], name=name)
    return 0.5 / d * sq[0, 0], dh, d_gain


SHARDED = {
    "ffn1_wi": 1, "ffn1_wo": 0, "w_in": 0, "ssm_w_glu": 0, "w_out": 0, "ffn2_wi": 1, "ffn2_wo": 0,
    "ple_w_gate": 0, "ple_w_proj": 1,
}
WEIGHTS = ["ffn1_norm", "ffn1_wi", "ffn1_wo", "mix_norm", "w_in", "ssm_lambda_re", "ssm_lambda_im", "ssm_log_dt",
           "ssm_b_re", "ssm_b_im", "ssm_c_re", "ssm_c_im", "ssm_d", "ssm_w_glu", "pool_w", "pool_scale", "w_out",
           "ffn2_norm", "ffn2_wi", "ffn2_wo", "ple_norm", "ple_w_gate", "ple_w_proj", "final_norm"]
REPLICATED = [n for n in WEIGHTS if n not in SHARDED]


def _unshard(gathered, axis):
    ax = axis + 2
    g = jnp.moveaxis(gathered, 0, ax - 1)
    shp = g.shape
    return g.reshape(shp[:ax - 1] + (shp[ax - 1] * shp[ax],) + shp[ax + 1:])


def _split_for_scatter(full, axis, c):
    ax = axis + 1
    shp = full.shape
    g = full.reshape(shp[:ax] + (4, 2, shp[ax] // N_DEV) + shp[ax + 1:])
    keep = lax.dynamic_index_in_dim(g, c, ax + 1, keepdims=False)
    send = lax.dynamic_index_in_dim(g, 1 - c, ax + 1, keepdims=False)
    return jnp.moveaxis(keep, ax, 0), jnp.moveaxis(send, ax, 0)


def _pack(arrs):
    pieces = []
    for a in arrs:
        flat = a.reshape(-1)
        pad = (-flat.shape[0]) % PACK
        pieces.append(jnp.pad(flat, (0, pad)).reshape(-1, LANES))
    return jnp.concatenate(pieces, axis=0)


def _unpack(packed, shapes):
    out, row = [], 0
    for s in shapes:
        size = math.prod(s)
        rows = (size + PACK - 1) // PACK * SUBLANES
        out.append(packed[row:row + rows].reshape(-1)[:size].reshape(s))
        row += rows
    return out


def local_step(x, p, target, w):
    batch, seq, d = x.shape
    n_tok = batch * seq
    h = x.reshape(n_tok, d)
    saved = []
    for i in range(DEPTH):
        lw = _layer_weights(w, i)
        h, s1 = ffn_fwd(h, lw["ffn1_norm"], lw["ffn1_wi_g"], lw["ffn1_wi_u"], lw["ffn1_wo"], "ffn1")
        h, s2 = mix_fwd(h, lw, batch, "mix")
        h, s3 = ffn_fwd(h, lw["ffn2_norm"], lw["ffn2_wi_g"], lw["ffn2_wi_u"], lw["ffn2_wo"], "ffn2")
        p_i = p[i].reshape(n_tok, -1)
        h, s4 = ple_fwd(h, p_i, lw["ple_norm"], lw["ple_w_gate"], lw["ple_w_proj"], "ple")
        saved.append((s1, s2, s3, s4, p_i))
    loss, dh, d_final = loss_head(h, w["final_norm"].reshape(1, d), target.reshape(n_tok, d), "loss_head")
    per_layer = []
    for i in reversed(range(DEPTH)):
        lw = _layer_weights(w, i)
        s1, s2, s3, s4, p_i = saved[i]
        g = {}
        dh, g["ple_norm"], g["ple_w_gate"], g["ple_w_proj"] = ple_bwd(dh, s4, p_i, lw["ple_norm"], lw["ple_w_gate"], "ple")
        dh, g["ffn2_norm"], g["ffn2_wi_g"], g["ffn2_wi_u"], g["ffn2_wo"] = ffn_bwd(
            dh, s3, lw["ffn2_norm"], lw["ffn2_wi_g"], lw["ffn2_wi_u"], lw["ffn2_wo"], "ffn2")
        dh, gm = mix_bwd(dh, s2, lw, batch, "mix")
        g.update(gm)
        dh, g["ffn1_norm"], g["ffn1_wi_g"], g["ffn1_wi_u"], g["ffn1_wo"] = ffn_bwd(
            dh, s1, lw["ffn1_norm"], lw["ffn1_wi_g"], lw["ffn1_wi_u"], lw["ffn1_wo"], "ffn1")
        per_layer.append(g)
    per_layer.reverse()
    grads = {k: jnp.stack([g[k] for g in per_layer], axis=0) for k in per_layer[0]}
    for k in ("ffn1_wi", "ffn2_wi"):
        grads[k] = jnp.concatenate([grads.pop(k + "_g"), grads.pop(k + "_u")], axis=2)
    grads["final_norm"] = d_final
    return loss, dh.reshape(batch, seq, d), grads


def _layer_weights(w, i):
    d = w["w_in"].shape[1]
    sw = d // 2
    lw = {k: w[k][i] for k in ("ffn1_wi_g", "ffn1_wi_u", "ffn1_wo", "w_in", "ssm_w_glu", "w_out", "ffn2_wi_g",
                               "ffn2_wi_u", "ffn2_wo", "ple_w_gate", "ple_w_proj", "pool_w")}
    lw["w_glu"] = lw.pop("ssm_w_glu")
    for k in ("ffn1_norm", "mix_norm", "ffn2_norm", "ple_norm"):
        lw[k] = w[k][i].reshape(1, d)
    lw["ssm_d"] = w["ssm_d"][i].reshape(1, sw)
    lw["pool_scale"] = w["pool_scale"][i].reshape(1, sw)
    lw["lam_re"] = w["ssm_lambda_re"][i].reshape(SSM_CH, 1)
    lw["lam_im"] = w["ssm_lambda_im"][i].reshape(SSM_CH, 1)
    lw["log_dt"] = jnp.repeat(w["ssm_log_dt"][i], SSM_STATE).reshape(SSM_CH, 1)
    lw["b_re"] = w["ssm_b_re"][i].reshape(SSM_CH, SSM_GROUP_CH)
    lw["b_im"] = w["ssm_b_im"][i].reshape(SSM_CH, SSM_GROUP_CH)
    lw["c_re"] = w["ssm_c_re"][i]
    lw["c_im"] = w["ssm_c_im"][i]
    return lw


def _as_weight_shapes(grads, shapes):
    return {k: grads[k].reshape(shapes[k]) for k in grads}


def kernel(x, p, ffn1_norm, ffn1_wi, ffn1_wo, mix_norm, w_in, ssm_lambda_re, ssm_lambda_im, ssm_log_dt, ssm_b_re, ssm_b_im, ssm_c_re, ssm_c_im, ssm_d, ssm_w_glu, pool_w, pool_scale, w_out, ffn2_norm, ffn2_wi, ffn2_wo, ple_norm, ple_w_gate, ple_w_proj, final_norm, loss_target, m_ffn1_norm, m_ffn1_wi, m_ffn1_wo, m_mix_norm, m_w_in, m_ssm_lambda_re, m_ssm_lambda_im, m_ssm_log_dt, m_ssm_b_re, m_ssm_b_im, m_ssm_c_re, m_ssm_c_im, m_ssm_d, m_ssm_w_glu, m_pool_w, m_pool_scale, m_w_out, m_ffn2_norm, m_ffn2_wi, m_ffn2_wo, m_ple_norm, m_ple_w_gate, m_ple_w_proj, m_final_norm, v_ffn1_norm, v_ffn1_wi, v_ffn1_wo, v_mix_norm, v_w_in, v_ssm_lambda_re, v_ssm_lambda_im, v_ssm_log_dt, v_ssm_b_re, v_ssm_b_im, v_ssm_c_re, v_ssm_c_im, v_ssm_d, v_ssm_w_glu, v_pool_w, v_pool_scale, v_w_out, v_ffn2_norm, v_ffn2_wi, v_ffn2_wo, v_ple_norm, v_ple_w_gate, v_ple_w_proj, v_final_norm):
    args = dict(locals())
    wts = {k: args[k] for k in WEIGHTS}
    c = lax.axis_index("c")

    sharded_names = list(SHARDED)
    gathered = all_gather([wts[k].astype(BF16) for k in sharded_names], "gather_weights")
    full = {k: wts[k] for k in REPLICATED}
    for k, g in zip(sharded_names, gathered):
        full[k] = _unshard(g, SHARDED[k])
    for k in ("ffn1_wi", "ffn2_wi"):
        wi = full.pop(k)
        half = wi.shape[2] // 2
        full[k + "_g"], full[k + "_u"] = wi[:, :, :half], wi[:, :, half:]

    loss_local, grad_x, grads = local_step(x, p, loss_target, full)
    loss = lax.psum(loss_local, ("x", "y", "c"))

    keeps, sends = zip(*[_split_for_scatter(grads[k], SHARDED[k], c) for k in sharded_names])
    from_sibling = swap_with_sibling(list(sends), "reduce_core_pair")
    chip_sums = []
    for k, keep, got in zip(sharded_names, keeps, from_sibling):
        cols = keep.shape[-1]
        chip_sums.append(add2(keep.reshape(-1, cols), got.reshape(-1, cols), f"sum_core_pair_{k}").reshape(keep.shape))
    from_chips = scatter_to_chips(chip_sums, "reduce_chips")

    outs = {}
    for k, parts in zip(sharded_names, from_chips):
        shp = wts[k].shape
        cols = shp[-1]
        res = adamw(wts[k].reshape(-1, cols), parts.reshape(4, -1, cols), args["m_" + k].reshape(-1, cols),
                    args["v_" + k].reshape(-1, cols), f"adamw_{k}")
        outs[k] = [r.reshape(shp) for r in res]

    rep_shapes = [wts[k].shape for k in REPLICATED]
    packed_g = _pack([grads[k].reshape(wts[k].shape) for k in REPLICATED])
    all_g, = all_gather([packed_g], "gather_small_grads")
    res = adamw(_pack([wts[k] for k in REPLICATED]), all_g, _pack([args["m_" + k] for k in REPLICATED]),
                _pack([args["v_" + k] for k in REPLICATED]), "adamw_small")
    unpacked = [_unpack(r, rep_shapes) for r in res]
    for j, k in enumerate(REPLICATED):
        outs[k] = [unpacked[q][j] for q in range(4)]

    result = [loss, grad_x]
    for q in range(4):
        result += [outs[k][q] for k in WEIGHTS]
    return tuple(result)
```

```python
import math

import jax
import jax.numpy as jnp
from jax import lax
from jax.experimental import pallas as pl
from jax.experimental.pallas import tpu as pltpu

F32 = jnp.float32
BF16 = jnp.bfloat16
MESH = pl.DeviceIdType.MESH
ANY = pl.BlockSpec(memory_space=pl.ANY)

N_DEV = 8
DEPTH = 4
EPS = 1e-6
SSM_GROUPS = 32
SSM_GROUP_CH = 16
SSM_STATE = 64
SSM_CH = SSM_GROUPS * SSM_STATE
SSM_SUPER = 2
POOL_WINDOWS = (2, 4, 8, 16)
POOL_HALO = 16
ADAM_LR, ADAM_B1, ADAM_B2, ADAM_EPS, ADAM_WD, ADAM_STEP = 0.001, 0.9, 0.999, 1e-08, 0.01, 10

V7X_VMEM_BYTES = 64 * 1024 * 1024
VMEM_LIMIT_BYTES = V7X_VMEM_BYTES - 12 * 1024 * 1024
LANES = 128
SUBLANES = 8
PACK = SUBLANES * LANES


def _params(*sem):
    return pltpu.CompilerParams(dimension_semantics=sem or None, vmem_limit_bytes=VMEM_LIMIT_BYTES)


def _tile(n, pref):
    if n <= pref:
        return n
    t = pref - pref % LANES
    while t >= LANES:
        if n % t == 0:
            return t
        t -= LANES
    raise ValueError(f"no lane-aligned tile for {n}")


def _row_tile(rows, pref):
    if rows <= pref:
        return rows
    t = pref - pref % SUBLANES
    while t >= SUBLANES:
        if rows % t == 0:
            return t
        t -= SUBLANES
    raise ValueError(f"no sublane-aligned tile for {rows}")


_DIMS = {"nn": ((1,), (0,)), "nt": ((1,), (1,)), "tn": ((0,), (0,))}


def matmul(a, b, *, mode, name, out_dtypes=None, epi=None, extras=(), separate=False, diag=1,
           tm=1024, tn=1024, tk=1024):
    a_list = list(a) if isinstance(a, (list, tuple)) else [a]
    b_list = list(b) if isinstance(b, (list, tuple)) else [b]
    n_terms = max(len(a_list), len(b_list))
    if len(a_list) == 1:
        a_idx = [0] * n_terms
    else:
        a_idx = list(range(n_terms))
    n_acc = n_terms if separate else 1
    if out_dtypes is None:
        out_dtypes = (F32,) * (n_acc if epi is None else 1)
    if mode == "tn":
        K, M = a_list[0].shape
        K2, N = b_list[0].shape
    elif mode == "nt":
        M, K = a_list[0].shape
        N, K2 = b_list[0].shape
    else:
        M, K = a_list[0].shape
        K2, N = b_list[0].shape
    assert K == K2, (name, a_list[0].shape, b_list[0].shape)
    if mode == "tn":
        tm, tn, tk = _tile(M // diag, tm), _tile(N // diag, tn), _tile(K, tk)
        nk = K // tk
        N = N // diag
        row_tiles, col_tiles = (M // diag) // tm, N // tn
        a_spec = pl.BlockSpec((tk, tm), lambda i, j, k: (k, i))
        b_spec = pl.BlockSpec((tk, tn), lambda i, j, k: (k, (i // row_tiles) * col_tiles + j))
    else:
        tm, tn, tk = _tile(M, tm), _tile(N // diag, tn), _tile(K // diag, tk)
        nk = (K // diag) // tk
        col_tiles = (N // diag) // tn
        a_spec = pl.BlockSpec((tm, tk), lambda i, j, k: (i, (j // col_tiles) * nk + k))
        if mode == "nt":
            b_spec = pl.BlockSpec((tn, tk), lambda i, j, k: (j, (j // col_tiles) * nk + k))
        else:
            b_spec = pl.BlockSpec((tk, tn), lambda i, j, k: ((j // col_tiles) * nk + k, j))
    ex_specs = []
    for e in extras:
        if e.shape == (M, N):
            ex_specs.append(pl.BlockSpec((tm, tn), lambda i, j, k: (i, j)))
        elif e.shape == (1, N):
            ex_specs.append(pl.BlockSpec((1, tn), lambda i, j, k: (0, j)))
        elif e.shape == (M, 1):
            ex_specs.append(pl.BlockSpec((tm, 1), lambda i, j, k: (i, 0)))
        else:
            raise ValueError((name, e.shape, (M, N)))
    na, nb, ne, no = len(a_list), len(b_list), len(extras), len(out_dtypes)
    dims = (_DIMS[mode], ((), ()))

    def body(*refs):
        a_refs = refs[:na]
        b_refs = refs[na:na + nb]
        ex_refs = refs[na + nb:na + nb + ne]
        out_refs = refs[na + nb + ne:na + nb + ne + no]
        acc_refs = refs[na + nb + ne + no:]
        a_vals = [r[...].astype(BF16) for r in a_refs]
        prods = [lax.dot_general(a_vals[a_idx[t]], b_refs[t][...].astype(BF16), dims, preferred_element_type=F32)
                 for t in range(n_terms)]
        if not separate:
            total = prods[0]
            for p_ in prods[1:]:
                total = total + p_
            prods = [total]

        def finish(accs):
            res = epi(*accs, *[e[...] for e in ex_refs]) if epi is not None else tuple(accs)
            for r, v in zip(out_refs, res):
                r[...] = v.astype(r.dtype)

        if nk == 1:
            finish(prods)
        else:
            k = pl.program_id(2)

            @pl.when(k == 0)
            def _():
                for r, v in zip(acc_refs, prods):
                    r[...] = v

            @pl.when(k > 0)
            def _():
                for r, v in zip(acc_refs, prods):
                    r[...] += v

            @pl.when(k == nk - 1)
            def _():
                finish([r[...] for r in acc_refs])

    outs = pl.pallas_call(
        body,
        name=name,
        grid=(M // tm, N // tn, nk),
        in_specs=[a_spec] * na + [b_spec] * nb + ex_specs,
        out_specs=[pl.BlockSpec((tm, tn), lambda i, j, k: (i, j))] * no,
        out_shape=[jax.ShapeDtypeStruct((M, N), dt) for dt in out_dtypes],
        scratch_shapes=[pltpu.VMEM((tm, tn), F32)] * (n_acc if nk > 1 else 0),
        compiler_params=_params("parallel", "parallel", "arbitrary"),
    )(*a_list, *b_list, *extras)
    return outs


def rowwise(fn, ins, outs, accs=(), *, name, tr=512):
    R = max(x.shape[0] for x in ins)
    tr = _row_tile(R, tr)
    in_specs = []
    for x in ins:
        if x.shape[0] == R and x.ndim == 2:
            in_specs.append(pl.BlockSpec((tr, x.shape[1]), lambda i: (i, 0)))
        else:
            in_specs.append(pl.BlockSpec(x.shape, lambda i, _n=x.ndim: (0,) * _n))
    ni, no = len(ins), len(outs)

    def body(*refs):
        i = pl.program_id(0)
        row_vals, acc_vals = fn(*[r[...] for r in refs[:ni]])
        for r, v in zip(refs[ni:ni + no], row_vals):
            r[...] = v.astype(r.dtype)
        for r, v in zip(refs[ni + no:], acc_vals):
            @pl.when(i == 0)
            def _(r=r, v=v):
                r[...] = v

            @pl.when(i > 0)
            def _(r=r, v=v):
                r[...] += v

    return pl.pallas_call(
        body,
        name=name,
        grid=(R // tr,),
        in_specs=in_specs,
        out_specs=[pl.BlockSpec((tr, c), lambda i: (i, 0)) for c, _ in outs]
        + [pl.BlockSpec(s, lambda i: (0, 0)) for s in accs],
        out_shape=[jax.ShapeDtypeStruct((R, c), dt) for c, dt in outs]
        + [jax.ShapeDtypeStruct(s, F32) for s in accs],
        compiler_params=_params("arbitrary"),
    )(*ins)


def _sigmoid(x):
    return 1.0 / (1.0 + jnp.exp(-x))


_GELU_C = math.sqrt(2.0 / math.pi)


def _gelu(x):
    return 0.5 * x * (1.0 + jnp.tanh(_GELU_C * (x + 0.044715 * (x * x * x))))


def _gelu_grad(x):
    t = jnp.tanh(_GELU_C * (x + 0.044715 * (x * x * x)))
    return 0.5 * (1.0 + t) + 0.5 * x * (1.0 - t * t) * (_GELU_C * (1.0 + 3.0 * 0.044715 * (x * x)))


def rms_fwd(x, g, name):
    def fn(x, g):
        r = lax.rsqrt(jnp.mean(x * x, axis=-1, keepdims=True) + EPS)
        return [x * r * g], []

    return rowwise(fn, [x, g], [(x.shape[1], BF16)], name=name)[0]


def rms_bwd(x, g, dn, dres, name):
    def fn(x, g, dn, dres):
        r = lax.rsqrt(jnp.mean(x * x, axis=-1, keepdims=True) + EPS)
        w = dn * g
        dx = r * w - x * (r * r * r) * jnp.mean(x * w, axis=-1, keepdims=True)
        return [dres + dx], [jnp.sum(dn * (x * r), axis=0, keepdims=True)]

    d = x.shape[1]
    return rowwise(fn, [x, g, dn, dres], [(d, F32)], [(1, d)], name=name)


def _whole(shape):
    return pl.BlockSpec(shape, lambda: (0,) * len(shape))


def _zoh(lr, li, ldt):
    dt = jnp.exp(ldt)
    mag = jnp.exp(lr * dt)
    ar, ai = mag * jnp.cos(li * dt), mag * jnp.sin(li * dt)
    den = lr * lr + li * li
    kr = ((ar - 1.0) * lr + ai * li) / den
    ki = (ai * lr - (ar - 1.0) * li) / den
    return dt, ar, ai, den, kr, ki


def ssm_prep(lam_re, lam_im, log_dt, b_re, b_im, name):
    n = lam_re.shape[0]

    def body(lr_ref, li_ref, ldt_ref, br_ref, bi_ref, ar_ref, ai_ref, bbr_ref, bbi_ref):
        _, ar, ai, _, kr, ki = _zoh(lr_ref[...], li_ref[...], ldt_ref[...])
        br, bi = br_ref[...], bi_ref[...]
        ar_ref[...] = ar
        ai_ref[...] = ai
        bbr_ref[...] = kr * br - ki * bi
        bbi_ref[...] = kr * bi + ki * br

    col, mat = (n, 1), (n, SSM_GROUP_CH)
    return pl.pallas_call(
        body, name=name,
        in_specs=[_whole(col)] * 3 + [_whole(mat)] * 2,
        out_specs=[_whole(col)] * 2 + [_whole(mat)] * 2,
        out_shape=[jax.ShapeDtypeStruct(col, F32)] * 2 + [jax.ShapeDtypeStruct(mat, F32)] * 2,
        compiler_params=_params(),
    )(lam_re, lam_im, log_dt, b_re, b_im)


def ssm_prep_bwd(lam_re, lam_im, log_dt, b_re, b_im, d_ar, d_ai, d_bbr, d_bbi, name):
    n = lam_re.shape[0]
    n_groups = n // SSM_STATE

    def body(lr_ref, li_ref, ldt_ref, br_ref, bi_ref, dar_ref, dai_ref, dbr_ref, dbi_ref,
             glr_ref, gli_ref, gdt_ref, gbr_ref, gbi_ref):
        lr, li = lr_ref[...], li_ref[...]
        dt, ar, ai, den, kr, ki = _zoh(lr, li, ldt_ref[...])
        br, bi, dbr, dbi = br_ref[...], bi_ref[...], dbr_ref[...], dbi_ref[...]
        gbr_ref[...] = kr * dbr + ki * dbi
        gbi_ref[...] = kr * dbi - ki * dbr
        gkr = jnp.sum(br * dbr + bi * dbi, axis=1, keepdims=True)
        gki = jnp.sum(br * dbi - bi * dbr, axis=1, keepdims=True)
        gar = dar_ref[...] + (gkr * lr - gki * li) / den
        gai = dai_ref[...] + (gki * lr + gkr * li) / den
        qr, qi = -(kr * lr + ki * li) / den, -(ki * lr - kr * li) / den
        g1r, g1i = qr * gkr + qi * gki, qr * gki - qi * gkr
        g2r, g2i = dt * (ar * gar + ai * gai), dt * (ar * gai - ai * gar)
        glr_ref[...] = g1r + g2r
        gli_ref[...] = g1i + g2i
        pr, pi_ = lr * ar - li * ai, lr * ai + li * ar
        gdt = (pr * gar + pi_ * gai) * dt
        grp = lax.broadcasted_iota(jnp.int32, (n, n_groups), 0) // SSM_STATE
        sel = grp == lax.broadcasted_iota(jnp.int32, (n, n_groups), 1)
        gdt_ref[...] = jnp.sum(jnp.where(sel, gdt, 0.0), axis=0, keepdims=True)

    col, mat = (n, 1), (n, SSM_GROUP_CH)
    return pl.pallas_call(
        body, name=name,
        in_specs=[_whole(col)] * 3 + [_whole(mat)] * 2 + [_whole(col)] * 2 + [_whole(mat)] * 2,
        out_specs=[_whole(col)] * 2 + [_whole((1, n_groups))] + [_whole(mat)] * 2,
        out_shape=[jax.ShapeDtypeStruct(col, F32)] * 2 + [jax.ShapeDtypeStruct((1, n_groups), F32)]
        + [jax.ShapeDtypeStruct(mat, F32)] * 2,
        compiler_params=_params(),
    )(lam_re, lam_im, log_dt, b_re, b_im, d_ar, d_ai, d_bbr, d_bbi)


def _cmul(ar, ai, br, bi):
    return ar * br - ai * bi, ar * bi + ai * br


def _scan_block(xr, xi, lr, li, carry_r, carry_i, or_ref, oi_ref, loc_r, loc_i, reverse):
    tb, cb = xr.shape
    ng = tb // SUBLANES
    xr = xr.reshape(ng, SUBLANES, cb)
    xi = xi.reshape(ng, SUBLANES, cb)
    rid = lax.broadcasted_iota(jnp.int32, (ng, SUBLANES, cb), 1)
    pr, pi_ = lr.reshape(1, 1, cb), li.reshape(1, 1, cb)
    powers = []
    for k in (1, 2, 4):
        powers.append((pr, pi_))
        shift = SUBLANES - k if reverse else k
        sr, si = pltpu.roll(xr, shift, 1), pltpu.roll(xi, shift, 1)
        keep = (rid < SUBLANES - k) if reverse else (rid >= k)
        tr_, ti_ = _cmul(pr, pi_, sr, si)
        xr = xr + jnp.where(keep, tr_, 0.0)
        xi = xi + jnp.where(keep, ti_, 0.0)
        pr, pi_ = _cmul(pr, pi_, pr, pi_)
    loc_r[...] = xr
    loc_i[...] = xi
    (p1r, p1i), (p2r, p2i), (p4r, p4i) = powers
    dist = lax.broadcasted_iota(jnp.int32, (SUBLANES, cb), 0)
    if reverse:
        dist = SUBLANES - 1 - dist
    wr = jnp.broadcast_to(p1r.reshape(1, cb), (SUBLANES, cb))
    wi = jnp.broadcast_to(p1i.reshape(1, cb), (SUBLANES, cb))
    for bit, (qr, qi) in ((1, (p1r, p1i)), (2, (p2r, p2i)), (4, (p4r, p4i))):
        mr, mi = _cmul(wr, wi, qr.reshape(1, cb), qi.reshape(1, cb))
        on = (dist & bit) != 0
        wr, wi = jnp.where(on, mr, wr), jnp.where(on, mi, wi)
    last = 0 if reverse else SUBLANES - 1

    def step(j, carry):
        cr, ci = carry
        g = (ng - 1 - j) if reverse else j
        fr = loc_r[g] + (wr * cr - wi * ci)
        fi = loc_i[g] + (wr * ci + wi * cr)
        rows = pl.ds(pl.multiple_of(g * SUBLANES, SUBLANES), SUBLANES)
        or_ref[rows, :] = fr
        oi_ref[rows, :] = fi
        return fr[last:last + 1, :], fi[last:last + 1, :]

    cr, ci = lax.fori_loop(0, ng, step, (carry_r[...], carry_i[...]))
    carry_r[...] = cr
    carry_i[...] = ci


def _scan_tiles(seq_len, n_ch):
    return min(256, seq_len), min(512, n_ch)


def _run_exchange(plan, step, n_steps, refs):
    @pl.when(step == 0)
    def _():
        plan.start(*refs)

    for part, at in plan.relay_steps(n_steps):
        @pl.when(step == at)
        def _(part=part):
            plan.relay(part, *refs)

    @pl.when(step == n_steps - 1)
    def _():
        plan.finish(*refs)


def _exchange_args(plan):
    if plan is None:
        return [], [], [], []
    return list(plan.ins), [ANY] * len(plan.ins), list(plan.out_shape), list(plan.sems)


def ssm_scan(x_re, x_im, lam_re, lam_im, batch, name, exchange=None):
    n, nch = x_re.shape
    seq = n // batch
    tb, cb = _scan_tiles(seq, nch)
    nt, nc = seq // tb, nch // cb
    ex_ins, ex_specs, ex_out, ex_sems = _exchange_args(exchange)
    n_ex = len(ex_ins)

    def body(*refs):
        xr_ref, xi_ref, lr_ref, li_ref = refs[:4]
        or_ref, oi_ref = refs[4 + n_ex:6 + n_ex]
        car_r, car_i, loc_r, loc_i = refs[6 + 2 * n_ex:10 + 2 * n_ex]
        if exchange is not None:
            step = (pl.program_id(0) * batch + pl.program_id(1)) * nt + pl.program_id(2)
            _run_exchange(exchange, step, nc * batch * nt,
                          (refs[4:4 + n_ex], refs[6 + n_ex:6 + 2 * n_ex], refs[10 + 2 * n_ex:]))

        @pl.when(pl.program_id(2) == 0)
        def _():
            car_r[...] = jnp.zeros_like(car_r)
            car_i[...] = jnp.zeros_like(car_i)

        _scan_block(xr_ref[...], xi_ref[...], lr_ref[...], li_ref[...], car_r, car_i, or_ref, oi_ref,
                    loc_r, loc_i, reverse=False)

    blk = pl.BlockSpec((tb, cb), lambda c, b, t: (b * nt + t, c))
    lam_spec = pl.BlockSpec((1, cb), lambda c, b, t: (0, c))
    res = pl.pallas_call(
        body, name=name,
        grid=(nc, batch, nt),
        in_specs=[blk, blk, lam_spec, lam_spec] + ex_specs,
        out_specs=[blk, blk] + ex_specs,
        out_shape=[jax.ShapeDtypeStruct((n, nch), F32)] * 2 + ex_out,
        scratch_shapes=[pltpu.VMEM((1, cb), F32)] * 2 + [pltpu.VMEM((tb // SUBLANES, SUBLANES, cb), F32)] * 2 + ex_sems,
        compiler_params=_params("arbitrary", "arbitrary", "arbitrary"),
    )(x_re, x_im, lam_re, lam_im, *ex_ins)
    return res[0], res[1], list(res[2:])


def ssm_scan_bwd(d_re, d_im, s_re, s_im, lam_re, lam_im, batch, name, exchange=None):
    n, nch = d_re.shape
    seq = n // batch
    tb, cb = _scan_tiles(seq, nch)
    nt, nc = seq // tb, nch // cb
    hb = tb // SUBLANES
    ex_ins, ex_specs, ex_out, ex_sems = _exchange_args(exchange)
    n_ex = len(ex_ins)

    def body(*refs):
        xr_ref, xi_ref, sr_ref, si_ref, hr_ref, hi_ref, lr_ref, li_ref = refs[:8]
        or_ref, oi_ref, dlr_ref, dli_ref = refs[8 + n_ex:12 + n_ex]
        car_r, car_i, loc_r, loc_i = refs[12 + 2 * n_ex:16 + 2 * n_ex]
        b, t = pl.program_id(1), pl.program_id(2)
        if exchange is not None:
            step = (pl.program_id(0) * batch + b) * nt + t
            _run_exchange(exchange, step, nc * batch * nt,
                          (refs[8:8 + n_ex], refs[12 + n_ex:12 + 2 * n_ex], refs[16 + 2 * n_ex:]))

        @pl.when(t == 0)
        def _():
            car_r[...] = jnp.zeros_like(car_r)
            car_i[...] = jnp.zeros_like(car_i)

        _scan_block(xr_ref[...], xi_ref[...], lr_ref[...], -li_ref[...], car_r, car_i, or_ref, oi_ref,
                    loc_r, loc_i, reverse=True)
        first_block = t == nt - 1
        row = lax.broadcasted_iota(jnp.int32, (tb, cb), 0)
        hr = jnp.where(first_block, 0.0, hr_ref[SUBLANES - 1:SUBLANES, :])
        hi = jnp.where(first_block, 0.0, hi_ref[SUBLANES - 1:SUBLANES, :])
        pr = jnp.where(row == 0, hr, pltpu.roll(sr_ref[...], 1, 0))
        pi_ = jnp.where(row == 0, hi, pltpu.roll(si_ref[...], 1, 0))
        gr, gi = or_ref[...], oi_ref[...]
        dlr = jnp.sum(gr * pr + gi * pi_, axis=0, keepdims=True)
        dli = jnp.sum(gi * pr - gr * pi_, axis=0, keepdims=True)
        start = jnp.logical_and(b == 0, t == 0)

        @pl.when(start)
        def _():
            dlr_ref[...] = dlr
            dli_ref[...] = dli

        @pl.when(jnp.logical_not(start))
        def _():
            dlr_ref[...] += dlr
            dli_ref[...] += dli

    def blk(c, b, t):
        return b * nt + (nt - 1 - t)

    st_spec = pl.BlockSpec((tb, cb), lambda c, b, t: (blk(c, b, t), c))
    halo_spec = pl.BlockSpec((SUBLANES, cb), lambda c, b, t: (jnp.maximum(blk(c, b, t) * hb - 1, 0), c))
    row_spec = pl.BlockSpec((1, cb), lambda c, b, t: (0, c))
    res = pl.pallas_call(
        body, name=name,
        grid=(nc, batch, nt),
        in_specs=[st_spec] * 4 + [halo_spec] * 2 + [row_spec] * 2 + ex_specs,
        out_specs=[st_spec, st_spec, row_spec, row_spec] + ex_specs,
        out_shape=[jax.ShapeDtypeStruct((n, nch), F32)] * 2 + [jax.ShapeDtypeStruct((1, nch), F32)] * 2 + ex_out,
        scratch_shapes=[pltpu.VMEM((1, cb), F32)] * 2 + [pltpu.VMEM((tb // SUBLANES, SUBLANES, cb), F32)] * 2 + ex_sems,
        compiler_params=_params("arbitrary", "arbitrary", "arbitrary"),
    )(d_re, d_im, s_re, s_im, s_re, s_im, lam_re, lam_im, *ex_ins)
    return res[0], res[1], res[2], res[3], list(res[4:])


def _pool_tiles(seq_len):
    return min(512, seq_len)


def _window_sums(x, n_steps, forward_in_time):
    rows = x.shape[0]
    k = 1
    for _ in range(n_steps):
        x = x + pltpu.roll(x, k if forward_in_time else rows - k, 0)
        k *= 2
    return x


def pool_fwd(u, w_pool, scale, batch, name):
    n, c = u.shape
    seq = n // batch
    tb = _pool_tiles(seq)
    nt = seq // tb
    gc = c // len(POOL_WINDOWS)
    hb = tb // POOL_HALO

    def body(x_ref, halo_ref, w_ref, sc_ref, y_ref, q_ref):
        t = pl.program_id(1)
        halo = jnp.where(t == 0, 0.0, halo_ref[...])
        full = jnp.concatenate([halo, x_ref[...]], axis=0)
        pos = lax.broadcasted_iota(jnp.int32, (tb, gc), 0) + t * tb + 1
        for gi, win in enumerate(POOL_WINDOWS):
            cols = slice(gi * gc, (gi + 1) * gc)
            sums = _window_sums(full[:, cols], gi + 1, True)[POOL_HALO:, :]
            cnt = jnp.minimum(pos, win).astype(F32)
            q = sums / cnt - x_ref[:, cols]
            r = jnp.dot(q.astype(BF16), w_ref[gi].astype(BF16), preferred_element_type=F32)
            q_ref[:, cols] = q.astype(q_ref.dtype)
            y_ref[:, cols] = (r * sc_ref[:, cols]).astype(y_ref.dtype)

    return pl.pallas_call(
        body, name=name,
        grid=(batch, nt),
        in_specs=[pl.BlockSpec((tb, c), lambda b, t: (b * nt + t, 0)),
                  pl.BlockSpec((POOL_HALO, c), lambda b, t: (jnp.maximum((b * nt + t) * hb - 1, 0), 0)),
                  pl.BlockSpec(w_pool.shape, lambda b, t: (0, 0, 0)),
                  pl.BlockSpec((1, c), lambda b, t: (0, 0))],
        out_specs=[pl.BlockSpec((tb, c), lambda b, t: (b * nt + t, 0))] * 2,
        out_shape=[jax.ShapeDtypeStruct((n, c), BF16)] * 2,
        compiler_params=_params("parallel", "arbitrary"),
    )(u, u, w_pool, scale)


def pool_bwd(dy, q, w_pool, scale, batch, name):
    n, c = dy.shape
    seq = n // batch
    tb = _pool_tiles(seq)
    nt = seq // tb
    ng = len(POOL_WINDOWS)
    gc = c // ng
    hb = tb // POOL_HALO
    n_blocks = n // POOL_HALO

    def body(dy_ref, dyh_ref, q_ref, w_ref, sc_ref, du_ref, dw_ref, dsc_ref):
        b, t = pl.program_id(0), pl.program_id(1)
        last = t == nt - 1
        dy_full = jnp.concatenate([dy_ref[...], jnp.where(last, 0.0, dyh_ref[...])], axis=0)
        pos = lax.broadcasted_iota(jnp.int32, (tb + POOL_HALO, gc), 0) + t * tb + 1
        start = jnp.logical_and(b == 0, t == 0)
        for gi, win in enumerate(POOL_WINDOWS):
            cols = slice(gi * gc, (gi + 1) * gc)
            w = w_ref[gi].astype(BF16)
            dr = dy_full[:, cols] * sc_ref[:, cols]
            dq = lax.dot_general(dr.astype(BF16), w, (((1,), (1,)), ((), ())), preferred_element_type=F32)
            cnt = jnp.minimum(pos, win).astype(F32)
            back = _window_sums(dq / cnt, gi + 1, False)
            du_ref[:, cols] = back[:tb, :] - dq[:tb, :]
            qb = q_ref[:, cols]
            r = jnp.dot(qb, w, preferred_element_type=F32)
            dw = lax.dot_general(qb, dr[:tb, :].astype(BF16), (((0,), (0,)), ((), ())), preferred_element_type=F32)
            dsc = jnp.sum(dy_ref[:, cols] * r, axis=0, keepdims=True)

            @pl.when(start)
            def _(gi=gi, cols=cols, dw=dw, dsc=dsc):
                dw_ref[gi] = dw
                dsc_ref[:, cols] = dsc

            @pl.when(jnp.logical_not(start))
            def _(gi=gi, cols=cols, dw=dw, dsc=dsc):
                dw_ref[gi] += dw
                dsc_ref[:, cols] += dsc

    blk = pl.BlockSpec((tb, c), lambda b, t: (b * nt + t, 0))
    halo = pl.BlockSpec((POOL_HALO, c), lambda b, t: (jnp.minimum((b * nt + t + 1) * hb, n_blocks - 1), 0))
    return pl.pallas_call(
        body, name=name,
        grid=(batch, nt),
        in_specs=[blk, halo, blk,
                  pl.BlockSpec(w_pool.shape, lambda b, t: (0, 0, 0)),
                  pl.BlockSpec((1, c), lambda b, t: (0, 0))],
        out_specs=[blk, pl.BlockSpec(w_pool.shape, lambda b, t: (0, 0, 0)), pl.BlockSpec((1, c), lambda b, t: (0, 0))],
        out_shape=[jax.ShapeDtypeStruct((n, c), F32), jax.ShapeDtypeStruct(w_pool.shape, F32),
                   jax.ShapeDtypeStruct((1, c), F32)],
        compiler_params=_params("arbitrary", "arbitrary"),
    )(dy, dy, q, w_pool, scale)


def _place():
    return lax.axis_index("x"), lax.axis_index("y"), lax.axis_index("c")


class GatherPlan:
    def __init__(self, arrs):
        self.ins = list(arrs)
        na = len(arrs)
        self.out_shape = [jax.ShapeDtypeStruct((N_DEV,) + a.shape, a.dtype) for a in arrs]
        self.sems = [pltpu.SemaphoreType.DMA((na, 7)), pltpu.SemaphoreType.DMA((na, 7)), pltpu.SemaphoreType.DMA((na,))]
        self.sizes = [math.prod(a.shape) * a.dtype.itemsize for a in arrs]

    def relay_steps(self, n_steps):
        total, done, steps = sum(self.sizes), 0, []
        for a, size in enumerate(self.sizes):
            done += size
            steps.append((a, min(n_steps - 1, (done * (n_steps - 1)) // total)))
        return steps

    def _copy(self, outs, sems, a, k, block, to, src=None):
        dst = outs[a].at[4 * block[0] + 2 * block[1] + block[2]]
        return pltpu.make_async_remote_copy(
            src_ref=dst if src is None else src, dst_ref=dst,
            send_sem=sems[0].at[a, k], recv_sem=sems[1].at[a, k], device_id=to, device_id_type=MESH)

    @staticmethod
    def _chips(x, y):
        return [(1 - x, y), (x, 1 - y), (1 - x, 1 - y)]

    def _local(self, ins, outs, sems, a, me):
        return pltpu.make_async_copy(ins[a], outs[a].at[4 * me[0] + 2 * me[1] + me[2]], sems[2].at[a])

    def start(self, ins, outs, sems):
        x, y, c = _place()
        me = (x, y, c)
        for a in range(len(ins)):
            self._local(ins, outs, sems, a, me).start()
            self._copy(outs, sems, a, 0, me, (x, y, 1 - c), src=ins[a]).start()
            for j, chip in enumerate(self._chips(x, y)):
                self._copy(outs, sems, a, 1 + j, me, (*chip, c), src=ins[a]).start()

    def relay(self, a, ins, outs, sems):
        x, y, c = _place()
        for j, chip in enumerate(self._chips(x, y)):
            self._copy(outs, sems, a, 1 + j, (*chip, c), (x, y, c)).wait_recv()
            self._copy(outs, sems, a, 4 + j, (*chip, c), (x, y, 1 - c)).start()

    def finish(self, ins, outs, sems):
        x, y, c = _place()
        me, sibling = (x, y, c), (x, y, 1 - c)
        for a in range(len(ins)):
            self._copy(outs, sems, a, 0, sibling, me).wait_recv()
            for j, chip in enumerate(self._chips(x, y)):
                self._copy(outs, sems, a, 4 + j, (*chip, 1 - c), me).wait_recv()
        for a in range(len(ins)):
            self._copy(outs, sems, a, 0, me, sibling, src=ins[a]).wait_send()
            for j, chip in enumerate(self._chips(x, y)):
                self._copy(outs, sems, a, 1 + j, me, (*chip, c), src=ins[a]).wait_send()
                self._copy(outs, sems, a, 4 + j, (*chip, c), sibling).wait_send()
            self._local(ins, outs, sems, a, me).wait()


class ChipScatterPlan:
    def __init__(self, arrs):
        self.ins = list(arrs)
        na = len(arrs)
        self.out_shape = [jax.ShapeDtypeStruct(a.shape, a.dtype) for a in arrs]
        self.sems = [pltpu.SemaphoreType.DMA((na, 3)), pltpu.SemaphoreType.DMA((na, 3)), pltpu.SemaphoreType.DMA((na,))]

    def relay_steps(self, n_steps):
        return []

    def _copy(self, ins, outs, sems, a, j, chip, c, mine):
        return pltpu.make_async_remote_copy(
            src_ref=ins[a].at[2 * chip[0] + chip[1]], dst_ref=outs[a].at[mine],
            send_sem=sems[0].at[a, j], recv_sem=sems[1].at[a, j], device_id=(*chip, c), device_id_type=MESH)

    def _all(self, ins, outs, sems):
        x, y, c = _place()
        mine = 2 * x + y
        remote = [self._copy(ins, outs, sems, a, j, chip, c, mine)
                  for a in range(len(ins)) for j, chip in enumerate(GatherPlan._chips(x, y))]
        local = [pltpu.make_async_copy(ins[a].at[mine], outs[a].at[mine], sems[2].at[a]) for a in range(len(ins))]
        return remote, local

    def start(self, ins, outs, sems):
        remote, local = self._all(ins, outs, sems)
        for cp in local + remote:
            cp.start()

    def finish(self, ins, outs, sems):
        remote, local = self._all(ins, outs, sems)
        for cp in remote:
            cp.wait_recv()
        for cp in remote:
            cp.wait_send()
        for cp in local:
            cp.wait()


def run_exchange(plan, name):
    n_in, n_out = len(plan.ins), len(plan.out_shape)

    def body(*refs):
        parts = (refs[:n_in], refs[n_in:n_in + n_out], refs[n_in + n_out:])
        plan.start(*parts)
        for part, _ in plan.relay_steps(1):
            plan.relay(part, *parts)
        plan.finish(*parts)

    return pl.pallas_call(
        body, name=name,
        in_specs=[ANY] * n_in, out_specs=[ANY] * n_out,
        out_shape=plan.out_shape, scratch_shapes=plan.sems,
    )(*plan.ins)


def swap_with_sibling(arrs, name):
    na = len(arrs)

    def body(*refs):
        ins, outs = refs[:na], refs[na:2 * na]
        send_sems, recv_sems = refs[2 * na:]
        x, y, c = _place()
        copies = [pltpu.make_async_remote_copy(
            src_ref=ins[a], dst_ref=outs[a], send_sem=send_sems.at[a], recv_sem=recv_sems.at[a],
            device_id=(x, y, 1 - c), device_id_type=MESH) for a in range(na)]
        for cp in copies:
            cp.start()
        for cp in copies:
            cp.wait()

    return pl.pallas_call(
        body, name=name,
        in_specs=[ANY] * na, out_specs=[ANY] * na,
        out_shape=[jax.ShapeDtypeStruct(a.shape, a.dtype) for a in arrs],
        scratch_shapes=[pltpu.SemaphoreType.DMA((na,)), pltpu.SemaphoreType.DMA((na,))],
    )(*arrs)


def adamw(w, gparts, m, v, name, tr=256):
    rows, cols = w.shape
    parts = gparts.shape[0]
    tr = _row_tile(rows, tr)
    c1 =1.0 - ADAM_B1 ** ADAM_STEP
    c2 = 1.0 - ADAM_B2 ** ADAM_STEP

    def body(w_ref, g_ref, m_ref, v_ref, go_ref, d_ref, mo_ref, vo_ref):
        g = g_ref[0].astype(F32)
        for p_ in range(1, parts):
            g = g + g_ref[p_].astype(F32)
        m_new = ADAM_B1 * m_ref[...] + (1.0 - ADAM_B1) * g
        v_new = ADAM_B2 * v_ref[...] + (1.0 - ADAM_B2) * (g * g)
        m_hat = m_new / c1
        v_hat = v_new / c2
        go_ref[...] = g
        d_ref[...] = -ADAM_LR * (m_hat / (jnp.sqrt(v_hat) + ADAM_EPS) + ADAM_WD * w_ref[...])
        mo_ref[...] = m_new
        vo_ref[...] = v_new

    blk = pl.BlockSpec((tr, cols), lambda i: (i, 0))
    return pl.pallas_call(
        body, name=name,
        grid=(rows // tr,),
        in_specs=[blk, pl.BlockSpec((parts, tr, cols), lambda i: (0, i, 0)), blk, blk],
        out_specs=[blk] * 4,
        out_shape=[jax.ShapeDtypeStruct((rows, cols), F32)] * 4,
        compiler_params=_params("parallel"),
    )(w, gparts, m, v)


def add2(a, b, name, out_dtype):
    return rowwise(lambda a, b: ([a.astype(F32) + b.astype(F32)], []), [a, b], [(a.shape[1], out_dtype)],
                   name=name, tr=256)[0]


def _block_diag(x):
    g, a, b = x.shape
    eye = jnp.eye(g, dtype=x.dtype)
    return (x[:, :, None, :] * eye[:, None, :, None]).reshape(g * a, g * b)


def _diag_blocks(x, a, b):
    per = x.shape[1] // b
    x5 = x.reshape(SSM_SUPER, per, a, per, b)
    eye = jnp.eye(per, dtype=x.dtype)
    return jnp.sum(x5 * eye[None, :, None, :, None], axis=3).reshape(SSM_SUPER * per, a, b)


def _swiglu_epi(g, u):
    return g, u, g * _sigmoid(g) * u


def ffn_fwd(h, gain, wi_g, wi_u, wo, tag):
    n = rms_fwd(h, gain, f"{tag}_norm")
    g, u, act = matmul(n, [wi_g, wi_u], mode="nn", name=f"{tag}_in", separate=True, epi=_swiglu_epi,
                       out_dtypes=(BF16, BF16, BF16), tn=1408)
    out, = matmul(act, wo, mode="nn", name=f"{tag}_out", epi=lambda acc, res: (res + 0.5 * acc,), extras=[h], tk=1408)
    return out, (h, n, g, u, act)


def ffn_bwd(dh, saved, gain, wi_g, wi_u, wo, tag):
    h, n, g, u, act = saved

    def epi(acc, g, u):
        g, u = g.astype(F32), u.astype(F32)
        s = _sigmoid(g)
        da = 0.5 * acc
        return da * u * (s * (1.0 + g * (1.0 - s))), da * (g * s)

    dg, du = matmul(dh, wo, mode="nt", name=f"{tag}_out_dx", epi=epi, extras=[g, u], out_dtypes=(BF16, BF16), tn=1408)
    d_wo, = matmul(act, dh, mode="tn", name=f"{tag}_out_dw", epi=lambda acc: (0.5 * acc,), tm=1408, tk=512)
    d_wi_g, d_wi_u = matmul(n, [dg, du], mode="tn", name=f"{tag}_in_dw", separate=True, tn=1408, tk=512)
    dn, = matmul([dg, du], [wi_g, wi_u], mode="nt", name=f"{tag}_in_dx", tk=1408)
    dh_new, d_gain = rms_bwd(h, gain, dn, dh, f"{tag}_norm_bwd")
    return dh_new, d_gain, d_wi_g, d_wi_u, d_wo


def mix_fwd(h, lw, batch, tag, exchange=None):
    sw = h.shape[1] // 2
    n = rms_fwd(h, lw["mix_norm"], f"{tag}_norm")
    us, up = matmul(n, [lw["w_in"][:, :sw], lw["w_in"][:, sw:]], mode="nn", name=f"{tag}_in", separate=True,
                    out_dtypes=(F32, F32))
    lam_r, lam_i, bb_r, bb_i = ssm_prep(lw["lam_re"], lw["lam_im"], lw["log_dt"], lw["b_re"], lw["b_im"], f"{tag}_zoh")
    lam = (lam_r.reshape(1, SSM_CH), lam_i.reshape(1, SSM_CH))
    b_mats = [_block_diag(bb.reshape(SSM_GROUPS, SSM_STATE, SSM_GROUP_CH).transpose(0, 2, 1)).astype(BF16)
              for bb in (bb_r, bb_i)]
    c_mats = [_block_diag(cc.transpose(0, 2, 1)).astype(BF16) for cc in (lw["c_re"], -lw["c_im"])]
    bu_re, bu_im = matmul(us, b_mats, mode="nn", name=f"{tag}_bu", separate=True, diag=SSM_SUPER)
    s_re, s_im, exchanged = ssm_scan(bu_re, bu_im, *lam, batch, f"{tag}_scan", exchange)
    y0, y1 = matmul([s_re, s_im], c_mats, mode="nn", name=f"{tag}_c", diag=SSM_SUPER,
                    epi=lambda acc, u, d: (acc + d * u, _gelu(acc + d * u)), extras=[us, lw["ssm_d"]],
                    out_dtypes=(F32, BF16))
    y2, gl = matmul(y1, lw["w_glu"], mode="nn", name=f"{tag}_glu",
                    epi=lambda acc, y0: (_gelu(y0) * _sigmoid(acc), acc), extras=[y0], out_dtypes=(BF16, F32))
    yp, q = pool_fwd(up, lw["pool_w"], lw["pool_scale"], batch, f"{tag}_pool")
    out, = matmul([y2, yp], [lw["w_out"][:sw], lw["w_out"][sw:]], mode="nn", name=f"{tag}_out",
                  epi=lambda acc, res: (res + acc,), extras=[h])
    return out, (h, n, us, lam, b_mats, c_mats, s_re, s_im, y0, y1, gl, y2, yp, q), exchanged


def mix_bwd(dh, saved, lw, batch, tag, exchange=None):
    h, n, us, lam, b_mats, c_mats, s_re, s_im, y0, y1, gl, y2, yp, q = saved
    sw = h.shape[1] // 2
    w_out_s, w_out_p = lw["w_out"][:sw], lw["w_out"][sw:]
    d_wo_s, = matmul(y2, dh, mode="tn", name=f"{tag}_out_dw_s", tk=512)
    d_wo_p, = matmul(yp, dh, mode="tn", name=f"{tag}_out_dw_p", tk=512)
    dy2, dyp = (matmul(dh, w, mode="nt", name=f"{tag}_out_dx_{k}")[0] for k, w in (("s", w_out_s), ("p", w_out_p)))
    dup, d_pool_w, d_pool_scale = pool_bwd(dyp, q, lw["pool_w"], lw["pool_scale"], batch, f"{tag}_pool_bwd")
    def glu_fn(dy2, y0, gl):
        sg = _sigmoid(gl)
        return [dy2 * _gelu(y0) * sg * (1.0 - sg)], []

    tg, = rowwise(glu_fn, [dy2, y0, gl], [(sw, BF16)], name=f"{tag}_glu_bwd")

    def dy0_epi(acc, dy2, gl, y0):
        return ((acc + dy2 * _sigmoid(gl)) * _gelu_grad(y0),)

    dy0, = matmul(tg, lw["w_glu"], mode="nt", name=f"{tag}_glu_dx", epi=dy0_epi, extras=[dy2, gl, y0])
    d_w_glu, = matmul(y1, tg, mode="tn", name=f"{tag}_glu_dw", tk=512)
    d_d, = rowwise(lambda dy0, u: ([], [jnp.sum(dy0 * u, axis=0, keepdims=True)]), [dy0, us], [], [(1, sw)],
                     name=f"{tag}_d_skip_dw")
    gd_re, gd_im = matmul(dy0, c_mats, mode="nt", name=f"{tag}_c_dx", separate=True, diag=SSM_SUPER)
    d_c_top, = matmul(s_re, dy0, mode="tn", name=f"{tag}_c_dw_re", diag=SSM_SUPER, tk=512)
    d_c_bot, = matmul(s_im, dy0, mode="tn", name=f"{tag}_c_dw_im", diag=SSM_SUPER, tk=512)
    g_re, g_im, d_lam_r, d_lam_i, exchanged = ssm_scan_bwd(gd_re, gd_im, s_re, s_im, *lam, batch, f"{tag}_scan_bwd",
                                                           exchange)
    dus, = matmul([g_re, g_im], b_mats, mode="nt", name=f"{tag}_bu_dx", diag=SSM_SUPER,
                  epi=lambda acc, dy0, d: (acc + d * dy0,), extras=[dy0, lw["ssm_d"]])
    d_b_re, d_b_im = matmul(us, [g_re, g_im], mode="tn", name=f"{tag}_bu_dw", separate=True, diag=SSM_SUPER, tk=512)
    d_bb_r = _diag_blocks(d_b_re, SSM_GROUP_CH, SSM_STATE).transpose(0, 2, 1).reshape(SSM_CH, SSM_GROUP_CH)
    d_bb_i = _diag_blocks(d_b_im, SSM_GROUP_CH, SSM_STATE).transpose(0, 2, 1).reshape(SSM_CH, SSM_GROUP_CH)
    d_lr, d_li, d_ldt, d_br, d_bi = ssm_prep_bwd(
        lw["lam_re"], lw["lam_im"], lw["log_dt"], lw["b_re"], lw["b_im"],
        d_lam_r.reshape(SSM_CH, 1), d_lam_i.reshape(SSM_CH, 1), d_bb_r, d_bb_i, f"{tag}_zoh_bwd")
    d_c_re = _diag_blocks(d_c_top, SSM_STATE, SSM_GROUP_CH).transpose(0, 2, 1)
    d_c_im = -_diag_blocks(d_c_bot, SSM_STATE, SSM_GROUP_CH).transpose(0, 2, 1)
    d_w_in_s, d_w_in_p = matmul(n, [dus, dup], mode="tn", name=f"{tag}_in_dw", separate=True, tk=512)
    dn, = matmul([dus, dup], [lw["w_in"][:, :sw], lw["w_in"][:, sw:]], mode="nt", name=f"{tag}_in_dx")
    dh_new, d_gain = rms_bwd(h, lw["mix_norm"], dn, dh, f"{tag}_norm_bwd")
    grads = dict(mix_norm=d_gain, w_in=jnp.concatenate([d_w_in_s, d_w_in_p], axis=1),
                 ssm_lambda_re=d_lr, ssm_lambda_im=d_li, ssm_log_dt=d_ldt, ssm_b_re=d_br, ssm_b_im=d_bi,
                 ssm_c_re=d_c_re, ssm_c_im=d_c_im, ssm_d=d_d, ssm_w_glu=d_w_glu, pool_w=d_pool_w,
                 pool_scale=d_pool_scale, w_out=jnp.concatenate([d_wo_s, d_wo_p], axis=0))
    return dh_new, grads, exchanged


def ple_fwd(h, p, gain, w_gate, w_proj, tag):
    n = rms_fwd(h, gain, f"{tag}_norm")
    e, = matmul(p, w_proj, mode="nn", name=f"{tag}_proj")
    out, pre = matmul(n, w_gate, mode="nn", name=f"{tag}_gate",
                      epi=lambda acc, e, res: (res + _sigmoid(acc) * e, acc), extras=[e, h], out_dtypes=(F32, F32))
    return out, (h, n, e, pre)


def ple_bwd(dh, saved, p, gain, w_gate, tag):
    h, n, e, pre = saved
    d = h.shape[1]

    def fn(dh, e, pre):
        s = _sigmoid(pre)
        return [dh * e * s * (1.0 - s), dh * s], []

    dpre, de = rowwise(fn, [dh, e, pre], [(d, BF16), (d, BF16)], name=f"{tag}_gate_bwd")
    d_w_gate, = matmul(n, dpre, mode="tn", name=f"{tag}_gate_dw", tk=512)
    d_w_proj, = matmul(p, de, mode="tn", name=f"{tag}_proj_dw", tk=512)
    dn, = matmul(dpre, w_gate, mode="nt", name=f"{tag}_gate_dx")
    dh_new, d_gain = rms_bwd(h, gain, dn, dh, f"{tag}_norm_bwd")
    return dh_new, d_gain, d_w_gate, d_w_proj


def loss_head(h, gain, target, name):
    d = h.shape[1]

    def fn(h, g, t):
        r = lax.rsqrt(jnp.mean(h * h, axis=-1, keepdims=True) + EPS)
        diff = h * r * g - t
        sq = jnp.sum(jnp.sum(diff * diff, axis=1, keepdims=True), axis=0, keepdims=True)
        dy = diff * (1.0 / d)
        w = dy * g
        dh = r * w - h * (r * r * r) * jnp.mean(h * w, axis=-1, keepdims=True)
        return [dh], [sq, jnp.sum(dy * (h * r), axis=0, keepdims=True)]

    dh, sq, d_gain = rowwise(fn, [h, gain, target], [(d, F32)], [(1, 1), (1, d)], name=name)
    return 0.5 / d * sq[0, 0], dh, d_gain


SHARDED = {
    "ffn1_wi": 1, "ffn1_wo": 0, "w_in": 0, "ssm_w_glu": 0, "w_out": 0, "ffn2_wi": 1, "ffn2_wo": 0,
    "ple_w_gate": 0, "ple_w_proj": 1,
}
WEIGHTS = ["ffn1_norm", "ffn1_wi", "ffn1_wo", "mix_norm", "w_in", "ssm_lambda_re", "ssm_lambda_im", "ssm_log_dt",
           "ssm_b_re", "ssm_b_im", "ssm_c_re", "ssm_c_im", "ssm_d", "ssm_w_glu", "pool_w", "pool_scale", "w_out",
           "ffn2_norm", "ffn2_wi", "ffn2_wo", "ple_norm", "ple_w_gate", "ple_w_proj", "final_norm"]
REPLICATED = [n for n in WEIGHTS if n not in SHARDED]


def _unshard(gathered, axis):
    g = jnp.moveaxis(gathered, 0, axis)
    shp = g.shape
    return g.reshape(shp[:axis] + (shp[axis] * shp[axis + 1],) + shp[axis + 2:])


def _split_for_scatter(full, axis, c):
    shp = full.shape
    g = full.reshape(shp[:axis] + (4, 2, shp[axis] // N_DEV) + shp[axis + 1:])
    keep = lax.dynamic_index_in_dim(g, c, axis + 1, keepdims=False)
    send = lax.dynamic_index_in_dim(g, 1 - c, axis + 1, keepdims=False)
    return jnp.moveaxis(keep, axis, 0), jnp.moveaxis(send, axis, 0).astype(BF16)


def _pack(arrs):
    pieces = []
    for a in arrs:
        flat = a.reshape(-1)
        pad = (-flat.shape[0]) % PACK
        pieces.append(jnp.pad(flat, (0, pad)).reshape(-1, LANES))
    return jnp.concatenate(pieces, axis=0)


def _unpack(packed, shapes):
    out, row = [], 0
    for s in shapes:
        size = math.prod(s)
        rows = (size + PACK - 1) // PACK * SUBLANES
        out.append(packed[row:row + rows].reshape(-1)[:size].reshape(s))
        row += rows
    return out


class NoExchange:
    def __init__(self, full):
        self.full = full

    def layer_full(self, i):
        return {k: v[i] for k, v in self.full.items()}

    def fwd_exchange(self, i):
        return None

    def fwd_done(self, i, results):
        pass

    def bwd_exchange(self, i):
        return None

    def bwd_done(self, i, results):
        pass

    def layer_grads(self, i, grads):
        pass


def local_step(x, p, target, rep, hooks):
    batch, seq, d = x.shape
    n_tok = batch * seq
    h = x.reshape(n_tok, d)
    saved = []
    for i in range(DEPTH):
        lw = _layer_weights(rep, hooks.layer_full(i), i)
        h, s1 = ffn_fwd(h, lw["ffn1_norm"], lw["ffn1_wi_g"], lw["ffn1_wi_u"], lw["ffn1_wo"], "ffn1")
        h, s2, exchanged = mix_fwd(h, lw, batch, "mix", hooks.fwd_exchange(i))
        hooks.fwd_done(i, exchanged)
        h, s3 = ffn_fwd(h, lw["ffn2_norm"], lw["ffn2_wi_g"], lw["ffn2_wi_u"], lw["ffn2_wo"], "ffn2")
        p_i = p[i].reshape(n_tok, -1)
        h, s4 = ple_fwd(h, p_i, lw["ple_norm"], lw["ple_w_gate"], lw["ple_w_proj"], "ple")
        saved.append((s1, s2, s3, s4, p_i))
    loss, dh, d_final = loss_head(h, rep["final_norm"].reshape(1, d), target.reshape(n_tok, d), "loss_head")
    per_layer = [None] * DEPTH
    for i in reversed(range(DEPTH)):
        lw = _layer_weights(rep, hooks.layer_full(i), i)
        s1, s2, s3, s4, p_i = saved[i]
        g = {}
        dh, g["ple_norm"], g["ple_w_gate"], g["ple_w_proj"] = ple_bwd(dh, s4, p_i, lw["ple_norm"], lw["ple_w_gate"], "ple")
        dh, g["ffn2_norm"], wi_g, wi_u, g["ffn2_wo"] = ffn_bwd(
            dh, s3, lw["ffn2_norm"], lw["ffn2_wi_g"], lw["ffn2_wi_u"], lw["ffn2_wo"], "ffn2")
        g["ffn2_wi"] = jnp.concatenate([wi_g, wi_u], axis=1)
        dh, gm, exchanged = mix_bwd(dh, s2, lw, batch, "mix", hooks.bwd_exchange(i))
        hooks.bwd_done(i, exchanged)
        g.update(gm)
        dh, g["ffn1_norm"], wi_g, wi_u, g["ffn1_wo"] = ffn_bwd(
            dh, s1, lw["ffn1_norm"], lw["ffn1_wi_g"], lw["ffn1_wi_u"], lw["ffn1_wo"], "ffn1")
        g["ffn1_wi"] = jnp.concatenate([wi_g, wi_u], axis=1)
        hooks.layer_grads(i, g)
        per_layer[i] = g
    return loss, dh.reshape(batch, seq, d), per_layer, d_final


def _layer_weights(w, big, i):
    d = big["w_in"].shape[1]
    sw = d // 2
    lw = {k: big[k] for k in ("ffn1_wo", "w_in", "w_out", "ffn2_wo", "ple_w_gate", "ple_w_proj")}
    lw["w_glu"] = big["ssm_w_glu"]
    for k in ("ffn1_wi", "ffn2_wi"):
        half = big[k].shape[1] // 2
        lw[k + "_g"], lw[k + "_u"] = big[k][:, :half], big[k][:, half:]
    lw["pool_w"] = w["pool_w"][i]
    for k in ("ffn1_norm", "mix_norm", "ffn2_norm", "ple_norm"):
        lw[k] = w[k][i].reshape(1, d)
    lw["ssm_d"] = w["ssm_d"][i].reshape(1, sw)
    lw["pool_scale"] = w["pool_scale"][i].reshape(1, sw)
    lw["lam_re"] = w["ssm_lambda_re"][i].reshape(SSM_CH, 1)
    lw["lam_im"] = w["ssm_lambda_im"][i].reshape(SSM_CH, 1)
    lw["log_dt"] = jnp.repeat(w["ssm_log_dt"][i], SSM_STATE).reshape(SSM_CH, 1)
    lw["b_re"] = w["ssm_b_re"][i].reshape(SSM_CH, SSM_GROUP_CH)
    lw["b_im"] = w["ssm_b_im"][i].reshape(SSM_CH, SSM_GROUP_CH)
    lw["c_re"] = w["ssm_c_re"][i]
    lw["c_im"] = w["ssm_c_im"][i]
    return lw


class MeshExchange:
    def __init__(self, shards):
        self.shards = shards
        self.c = lax.axis_index("c")
        self.gathered = {0: run_exchange(GatherPlan(shards[0]), "gather_first_layer")}
        self.chip_sums = {}
        self.from_chips = {}

    def layer_full(self, i):
        return {k: _unshard(g, SHARDED[k]) for k, g in zip(SHARDED, self.gathered[i])}

    def fwd_exchange(self, i):
        return GatherPlan(self.shards[i + 1]) if i + 1 < DEPTH else None

    def fwd_done(self, i, results):
        if results:
            self.gathered[i + 1] = results

    def layer_grads(self, i, grads):
        keeps, sends = zip(*[_split_for_scatter(grads[k], SHARDED[k], self.c) for k in SHARDED])
        from_sibling = swap_with_sibling(list(sends), "reduce_core_pair")
        sums = []
        for k, keep, got in zip(SHARDED, keeps, from_sibling):
            cols = keep.shape[-1]
            sums.append(add2(keep.reshape(-1, cols), got.reshape(-1, cols), f"sum_core_pair_{k}", BF16).reshape(keep.shape))
        if i == 0:
            self.from_chips[0] = run_exchange(ChipScatterPlan(sums), "reduce_chips_last_layer")
        else:
            self.chip_sums[i] = sums

    def bwd_exchange(self, i):
        return ChipScatterPlan(self.chip_sums[i + 1]) if i + 1 in self.chip_sums else None

    def bwd_done(self, i, results):
        if results:
            self.from_chips[i + 1] = results


def kernel(x, p, ffn1_norm, ffn1_wi, ffn1_wo, mix_norm, w_in, ssm_lambda_re, ssm_lambda_im, ssm_log_dt, ssm_b_re, ssm_b_im, ssm_c_re, ssm_c_im, ssm_d, ssm_w_glu, pool_w, pool_scale, w_out, ffn2_norm, ffn2_wi, ffn2_wo, ple_norm, ple_w_gate, ple_w_proj, final_norm, loss_target, m_ffn1_norm, m_ffn1_wi, m_ffn1_wo, m_mix_norm, m_w_in, m_ssm_lambda_re, m_ssm_lambda_im, m_ssm_log_dt, m_ssm_b_re, m_ssm_b_im, m_ssm_c_re, m_ssm_c_im, m_ssm_d, m_ssm_w_glu, m_pool_w, m_pool_scale, m_w_out, m_ffn2_norm, m_ffn2_wi, m_ffn2_wo, m_ple_norm, m_ple_w_gate, m_ple_w_proj, m_final_norm, v_ffn1_norm, v_ffn1_wi, v_ffn1_wo, v_mix_norm, v_w_in, v_ssm_lambda_re, v_ssm_lambda_im, v_ssm_log_dt, v_ssm_b_re, v_ssm_b_im, v_ssm_c_re, v_ssm_c_im, v_ssm_d, v_ssm_w_glu, v_pool_w, v_pool_scale, v_w_out, v_ffn2_norm, v_ffn2_wi, v_ffn2_wo, v_ple_norm, v_ple_w_gate, v_ple_w_proj, v_final_norm):
    args = dict(locals())
    wts = {k: args[k] for k in WEIGHTS}
    rep = {k: wts[k] for k in REPLICATED}

    shards = [[wts[k][i].astype(BF16) for k in SHARDED] for i in range(DEPTH)]
    exchange = MeshExchange(shards)
    loss_local, grad_x, per_layer, d_final = local_step(x, p, loss_target, rep, exchange)
    loss = lax.psum(loss_local, ("x", "y", "c"))

    outs = {}
    for j, k in enumerate(SHARDED):
        shp = wts[k].shape
        cols = shp[-1]
        parts = jnp.stack([exchange.from_chips[i][j] for i in range(DEPTH)], axis=1)
        res = adamw(wts[k].reshape(-1, cols), parts.reshape(4, -1, cols), args["m_" + k].reshape(-1, cols),
                    args["v_" + k].reshape(-1, cols), f"adamw_{k}")
        outs[k] = [r.reshape(shp) for r in res]

    rep_shapes = [wts[k].shape for k in REPLICATED]
    small = {k: jnp.stack([g[k] for g in per_layer], axis=0) for k in REPLICATED if k != "final_norm"}
    small["final_norm"] = d_final
    packed_g = _pack([small[k].reshape(wts[k].shape) for k in REPLICATED])
    all_g, = run_exchange(GatherPlan([packed_g]), "gather_small_grads")
    res = adamw(_pack([wts[k] for k in REPLICATED]), all_g, _pack([args["m_" + k] for k in REPLICATED]),
                _pack([args["v_" + k] for k in REPLICATED]), "adamw_small")
    unpacked = [_unpack(r, rep_shapes) for r in res]
    for j, k in enumerate(REPLICATED):
        outs[k] = [unpacked[q][j] for q in range(4)]

    result = [loss, grad_x]
    for q in range(4):
        result += [outs[k][q] for k in WEIGHTS]
    return tuple(result)
```

```python
import math

import jax
import jax.numpy as jnp
from jax import lax
from jax.experimental import pallas as pl
from jax.experimental.pallas import tpu as pltpu

F32 = jnp.float32
BF16 = jnp.bfloat16
MESH = pl.DeviceIdType.MESH
ANY = pl.BlockSpec(memory_space=pl.ANY)

N_DEV = 8
DEPTH = 4
EPS = 1e-6
SSM_GROUPS = 32
SSM_GROUP_CH = 16
SSM_STATE = 64
SSM_CH = SSM_GROUPS * SSM_STATE
SSM_SUPER = 2
POOL_WINDOWS = (2, 4, 8, 16)
POOL_HALO = 16
ADAM_LR, ADAM_B1, ADAM_B2, ADAM_EPS, ADAM_WD, ADAM_STEP = 0.001, 0.9, 0.999, 1e-08, 0.01, 10

V7X_VMEM_BYTES = 64 * 1024 * 1024
VMEM_LIMIT_BYTES = V7X_VMEM_BYTES - 12 * 1024 * 1024
LANES = 128
SUBLANES = 8
PACK = SUBLANES * LANES


def _params(*sem):
    return pltpu.CompilerParams(dimension_semantics=sem or None, vmem_limit_bytes=VMEM_LIMIT_BYTES)


def _tile(n, pref):
    if n <= pref:
        return n
    t = pref - pref % LANES
    while t >= LANES:
        if n % t == 0:
            return t
        t -= LANES
    raise ValueError(f"no lane-aligned tile for {n}")


def _row_tile(rows, pref):
    if rows <= pref:
        return rows
    t = pref - pref % SUBLANES
    while t >= SUBLANES:
        if rows % t == 0:
            return t
        t -= SUBLANES
    raise ValueError(f"no sublane-aligned tile for {rows}")


_DIMS = {"nn": ((1,), (0,)), "nt": ((1,), (1,)), "tn": ((0,), (0,))}


def matmul(a, b, *, mode, name, out_dtypes=None, epi=None, extras=(), separate=False, diag=1, col_sums=0,
           tm=1024, tn=1024, tk=1024):
    a_list = list(a) if isinstance(a, (list, tuple)) else [a]
    b_list = list(b) if isinstance(b, (list, tuple)) else [b]
    n_terms = max(len(a_list), len(b_list))
    a_idx = [0] * n_terms if len(a_list) == 1 else list(range(n_terms))
    b_idx = [0] * n_terms if len(b_list) == 1 else list(range(n_terms))
    n_acc = n_terms if separate else 1
    if out_dtypes is None:
        out_dtypes = (F32,) * (n_acc if epi is None else 1)
    if mode == "tn":
        K, M = a_list[0].shape
        K2, N = b_list[0].shape
    elif mode == "nt":
        M, K = a_list[0].shape
        N, K2 = b_list[0].shape
    else:
        M, K = a_list[0].shape
        K2, N = b_list[0].shape
    assert K == K2, (name, a_list[0].shape, b_list[0].shape)
    if mode == "tn":
        tm, tn, tk = _tile(M // diag, tm), _tile(N // diag, tn), _tile(K, tk)
        nk = K // tk
        N = N // diag
        row_tiles, col_tiles = (M // diag) // tm, N // tn
        a_spec = pl.BlockSpec((tk, tm), lambda i, j, k: (k, i))
        b_spec = pl.BlockSpec((tk, tn), lambda i, j, k: (k, (i // row_tiles) * col_tiles + j))
    else:
        tm, tn, tk = _tile(M, tm), _tile(N // diag, tn), _tile(K // diag, tk)
        nk = (K // diag) // tk
        col_tiles = (N // diag) // tn
        a_spec = pl.BlockSpec((tm, tk), lambda i, j, k: (i, (j // col_tiles) * nk + k))
        if mode == "nt":
            b_spec = pl.BlockSpec((tn, tk), lambda i, j, k: (j, (j // col_tiles) * nk + k))
        else:
            b_spec = pl.BlockSpec((tk, tn), lambda i, j, k: ((j // col_tiles) * nk + k, j))
    assert not col_sums or N == tn, (name, N, tn)
    ex_specs = []
    for e in extras:
        if e.shape == (M, N):
            ex_specs.append(pl.BlockSpec((tm, tn), lambda i, j, k: (i, j)))
        elif e.shape == (1, N):
            ex_specs.append(pl.BlockSpec((1, tn), lambda i, j, k: (0, j)))
        elif e.shape == (M, 1):
            ex_specs.append(pl.BlockSpec((tm, 1), lambda i, j, k: (i, 0)))
        else:
            raise ValueError((name, e.shape, (M, N)))
    na, nb, ne, no = len(a_list), len(b_list), len(extras), len(out_dtypes)
    dims = (_DIMS[mode], ((), ()))

    def body(*refs):
        a_refs = refs[:na]
        b_refs = refs[na:na + nb]
        ex_refs = refs[na + nb:na + nb + ne]
        out_refs = refs[na + nb + ne:na + nb + ne + no]
        sum_refs = refs[na + nb + ne + no:na + nb + ne + no + col_sums]
        acc_refs = refs[na + nb + ne + no + col_sums:]
        a_vals = [r[...].astype(BF16) for r in a_refs]
        b_vals = [r[...].astype(BF16) for r in b_refs]
        prods = [lax.dot_general(a_vals[a_idx[t]], b_vals[b_idx[t]], dims, preferred_element_type=F32)
                 for t in range(n_terms)]
        if not separate:
            total = prods[0]
            for p_ in prods[1:]:
                total = total + p_
            prods = [total]

        def finish(accs):
            res = epi(*accs, *[e[...] for e in ex_refs]) if epi is not None else tuple(accs)
            for r, v in zip(out_refs, res[:no]):
                r[...] = v.astype(r.dtype)
            first_rows = pl.program_id(0) == 0
            for r, v in zip(sum_refs, res[no:]):
                @pl.when(first_rows)
                def _(r=r, v=v):
                    r[...] = v

                @pl.when(jnp.logical_not(first_rows))
                def _(r=r, v=v):
                    r[...] += v

        if nk == 1:
            finish(prods)
        else:
            k = pl.program_id(2)

            @pl.when(k == 0)
            def _():
                for r, v in zip(acc_refs, prods):
                    r[...] = v

            @pl.when(jnp.logical_and(k > 0, k < nk - 1))
            def _():
                for r, v in zip(acc_refs, prods):
                    r[...] += v

            @pl.when(k == nk - 1)
            def _():
                finish([r[...] + v for r, v in zip(acc_refs, prods)])

    outs = pl.pallas_call(
        body,
        name=name,
        grid=(M // tm, N // tn, nk),
        in_specs=[a_spec] * na + [b_spec] * nb + ex_specs,
        out_specs=[pl.BlockSpec((tm, tn), lambda i, j, k: (i, j))] * no
        + [pl.BlockSpec((1, tn), lambda i, j, k: (0, j))] * col_sums,
        out_shape=[jax.ShapeDtypeStruct((M, N), dt) for dt in out_dtypes]
        + [jax.ShapeDtypeStruct((1, N), F32)] * col_sums,
        scratch_shapes=[pltpu.VMEM((tm, tn), F32)] * (n_acc if nk > 1 else 0),
        compiler_params=_params(*(("arbitrary",) * 3 if col_sums else ("parallel", "parallel", "arbitrary"))),
    )(*a_list, *b_list, *extras)
    return outs


def rowwise(fn, ins, outs, accs=(), *, name, tr=512):
    R = max(x.shape[0] for x in ins)
    tr = _row_tile(R, tr)
    in_specs = []
    for x in ins:
        if x.shape[0] == R and x.ndim == 2:
            in_specs.append(pl.BlockSpec((tr, x.shape[1]), lambda i: (i, 0)))
        else:
            in_specs.append(pl.BlockSpec(x.shape, lambda i, _n=x.ndim: (0,) * _n))
    ni, no = len(ins), len(outs)

    def body(*refs):
        i = pl.program_id(0)
        row_vals, acc_vals = fn(*[r[...] for r in refs[:ni]])
        for r, v in zip(refs[ni:ni + no], row_vals):
            r[...] = v.astype(r.dtype)
        for r, v in zip(refs[ni + no:], acc_vals):
            @pl.when(i == 0)
            def _(r=r, v=v):
                r[...] = v

            @pl.when(i > 0)
            def _(r=r, v=v):
                r[...] += v

    return pl.pallas_call(
        body,
        name=name,
        grid=(R // tr,),
        in_specs=in_specs,
        out_specs=[pl.BlockSpec((tr, c), lambda i: (i, 0)) for c, _ in outs]
        + [pl.BlockSpec(s, lambda i: (0, 0)) for s in accs],
        out_shape=[jax.ShapeDtypeStruct((R, c), dt) for c, dt in outs]
        + [jax.ShapeDtypeStruct(s, F32) for s in accs],
        compiler_params=_params("arbitrary"),
    )(*ins)


def _sigmoid(x):
    return 1.0 / (1.0 + jnp.exp(-x))


_GELU_C = math.sqrt(2.0 / math.pi)


def _gelu(x):
    return 0.5 * x * (1.0 + jnp.tanh(_GELU_C * (x + 0.044715 * (x * x * x))))


def _gelu_grad(x):
    t = jnp.tanh(_GELU_C * (x + 0.044715 * (x * x * x)))
    return 0.5 * (1.0 + t) + 0.5 * x * (1.0 - t * t) * (_GELU_C * (1.0 + 3.0 * 0.044715 * (x * x)))


def rms_fwd(x, g, name):
    def fn(x, g):
        r = lax.rsqrt(jnp.mean(x * x, axis=-1, keepdims=True) + EPS)
        return [x * r * g], []

    return rowwise(fn, [x, g], [(x.shape[1], BF16)], name=name)[0]


def _rms_tile(x, g):
    return x * lax.rsqrt(jnp.mean(x * x, axis=-1, keepdims=True) + EPS) * g


def _rms_bwd_tile(dn, x, g, dres):
    r = lax.rsqrt(jnp.mean(x * x, axis=-1, keepdims=True) + EPS)
    w = dn * g
    dx = r * w - x * (r * r * r) * jnp.mean(x * w, axis=-1, keepdims=True)
    return dres + dx, jnp.sum(dn * (x * r), axis=0, keepdims=True)


def _whole(shape):
    return pl.BlockSpec(shape, lambda: (0,) * len(shape))


def _zoh(lr, li, ldt):
    dt = jnp.exp(ldt)
    mag = jnp.exp(lr * dt)
    ar, ai = mag * jnp.cos(li * dt), mag * jnp.sin(li * dt)
    den = lr * lr + li * li
    kr = ((ar - 1.0) * lr + ai * li) / den
    ki = (ai * lr - (ar - 1.0) * li) / den
    return dt, ar, ai, den, kr, ki


def ssm_prep(lam_re, lam_im, log_dt, b_re, b_im, name):
    n = lam_re.shape[0]

    def body(lr_ref, li_ref, ldt_ref, br_ref, bi_ref, ar_ref, ai_ref, bbr_ref, bbi_ref):
        _, ar, ai, _, kr, ki = _zoh(lr_ref[...], li_ref[...], ldt_ref[...])
        br, bi = br_ref[...], bi_ref[...]
        ar_ref[...] = ar
        ai_ref[...] = ai
        bbr_ref[...] = kr * br - ki * bi
        bbi_ref[...] = kr * bi + ki * br

    col, mat = (n, 1), (n, SSM_GROUP_CH)
    return pl.pallas_call(
        body, name=name,
        in_specs=[_whole(col)] * 3 + [_whole(mat)] * 2,
        out_specs=[_whole(col)] * 2 + [_whole(mat)] * 2,
        out_shape=[jax.ShapeDtypeStruct(col, F32)] * 2 + [jax.ShapeDtypeStruct(mat, F32)] * 2,
        compiler_params=_params(),
    )(lam_re, lam_im, log_dt, b_re, b_im)


def ssm_prep_bwd(lam_re, lam_im, log_dt, b_re, b_im, d_ar, d_ai, d_bbr, d_bbi, name):
    n = lam_re.shape[0]
    n_groups = n // SSM_STATE

    def body(lr_ref, li_ref, ldt_ref, br_ref, bi_ref, dar_ref, dai_ref, dbr_ref, dbi_ref,
             glr_ref, gli_ref, gdt_ref, gbr_ref, gbi_ref):
        lr, li = lr_ref[...], li_ref[...]
        dt, ar, ai, den, kr, ki = _zoh(lr, li, ldt_ref[...])
        br, bi, dbr, dbi = br_ref[...], bi_ref[...], dbr_ref[...], dbi_ref[...]
        gbr_ref[...] = kr * dbr + ki * dbi
        gbi_ref[...] = kr * dbi - ki * dbr
        gkr = jnp.sum(br * dbr + bi * dbi, axis=1, keepdims=True)
        gki = jnp.sum(br * dbi - bi * dbr, axis=1, keepdims=True)
        gar = dar_ref[...] + (gkr * lr - gki * li) / den
        gai = dai_ref[...] + (gki * lr + gkr * li) / den
        qr, qi = -(kr * lr + ki * li) / den, -(ki * lr - kr * li) / den
        g1r, g1i = qr * gkr + qi * gki, qr * gki - qi * gkr
        g2r, g2i = dt * (ar * gar + ai * gai), dt * (ar * gai - ai * gar)
        glr_ref[...] = g1r + g2r
        gli_ref[...] = g1i + g2i
        pr, pi_ = lr * ar - li * ai, lr * ai + li * ar
        gdt = (pr * gar + pi_ * gai) * dt
        grp = lax.broadcasted_iota(jnp.int32, (n, n_groups), 0) // SSM_STATE
        sel = grp == lax.broadcasted_iota(jnp.int32, (n, n_groups), 1)
        gdt_ref[...] = jnp.sum(jnp.where(sel, gdt, 0.0), axis=0, keepdims=True)

    col, mat = (n, 1), (n, SSM_GROUP_CH)
    return pl.pallas_call(
        body, name=name,
        in_specs=[_whole(col)] * 3 + [_whole(mat)] * 2 + [_whole(col)] * 2 + [_whole(mat)] * 2,
        out_specs=[_whole(col)] * 2 + [_whole((1, n_groups))] + [_whole(mat)] * 2,
        out_shape=[jax.ShapeDtypeStruct(col, F32)] * 2 + [jax.ShapeDtypeStruct((1, n_groups), F32)]
        + [jax.ShapeDtypeStruct(mat, F32)] * 2,
        compiler_params=_params(),
    )(lam_re, lam_im, log_dt, b_re, b_im, d_ar, d_ai, d_bbr, d_bbi)


def _cmul(ar, ai, br, bi):
    return ar * br - ai * bi, ar * bi + ai * br


def _scan_block(xr, xi, lr, li, carry_r, carry_i, or_ref, oi_ref, loc_r, loc_i, reverse):
    tb, cb = xr.shape
    ng = tb // SUBLANES
    xr = xr.reshape(ng, SUBLANES, cb)
    xi = xi.reshape(ng, SUBLANES, cb)
    rid = lax.broadcasted_iota(jnp.int32, (ng, SUBLANES, cb), 1)
    pr, pi_ = lr.reshape(1, 1, cb), li.reshape(1, 1, cb)
    powers = []
    for k in (1, 2, 4):
        powers.append((pr, pi_))
        shift = SUBLANES - k if reverse else k
        sr, si = pltpu.roll(xr, shift, 1), pltpu.roll(xi, shift, 1)
        keep = (rid < SUBLANES - k) if reverse else (rid >= k)
        tr_, ti_ = _cmul(pr, pi_, sr, si)
        xr = xr + jnp.where(keep, tr_, 0.0)
        xi = xi + jnp.where(keep, ti_, 0.0)
        pr, pi_ = _cmul(pr, pi_, pr, pi_)
    loc_r[...] = xr
    loc_i[...] = xi
    (p1r, p1i), (p2r, p2i), (p4r, p4i) = powers
    dist = lax.broadcasted_iota(jnp.int32, (SUBLANES, cb), 0)
    if reverse:
        dist = SUBLANES - 1 - dist
    wr = jnp.broadcast_to(p1r.reshape(1, cb), (SUBLANES, cb))
    wi = jnp.broadcast_to(p1i.reshape(1, cb), (SUBLANES, cb))
    for bit, (qr, qi) in ((1, (p1r, p1i)), (2, (p2r, p2i)), (4, (p4r, p4i))):
        mr, mi = _cmul(wr, wi, qr.reshape(1, cb), qi.reshape(1, cb))
        on = (dist & bit) != 0
        wr, wi = jnp.where(on, mr, wr), jnp.where(on, mi, wi)
    last = 0 if reverse else SUBLANES - 1

    def step(j, carry):
        cr, ci = carry
        g = (ng - 1 - j) if reverse else j
        fr = loc_r[g] + (wr * cr - wi * ci)
        fi = loc_i[g] + (wr * ci + wi * cr)
        rows = pl.ds(pl.multiple_of(g * SUBLANES, SUBLANES), SUBLANES)
        or_ref[rows, :] = fr
        oi_ref[rows, :] = fi
        return fr[last:last + 1, :], fi[last:last + 1, :]

    cr, ci = lax.fori_loop(0, ng, step, (carry_r[...], carry_i[...]))
    carry_r[...] = cr
    carry_i[...] = ci


def _scan_tiles(seq_len, n_ch):
    return min(256, seq_len), min(512, n_ch)


def _run_exchange(plan, step, n_steps, refs):
    @pl.when(step == 0)
    def _():
        plan.start(*refs)

    for part, at in plan.relay_steps(n_steps):
        @pl.when(step == at)
        def _(part=part):
            plan.relay(part, *refs)

    @pl.when(step == n_steps - 1)
    def _():
        plan.finish(*refs)


def _exchange_args(plan):
    if plan is None:
        return [], [], [], []
    return list(plan.ins), [ANY] * len(plan.ins), list(plan.out_shape), list(plan.sems)


def ssm_scan(x_re, x_im, lam_re, lam_im, batch, name, exchange=None):
    n, nch = x_re.shape
    seq = n // batch
    tb, cb = _scan_tiles(seq, nch)
    nt, nc = seq // tb, nch // cb
    ex_ins, ex_specs, ex_out, ex_sems = _exchange_args(exchange)
    n_ex = len(ex_ins)

    def body(*refs):
        xr_ref, xi_ref, lr_ref, li_ref = refs[:4]
        or_ref, oi_ref = refs[4 + n_ex:6 + n_ex]
        car_r, car_i, loc_r, loc_i, s_r, s_i = refs[6 + 2 * n_ex:12 + 2 * n_ex]
        if exchange is not None:
            step = (pl.program_id(0) * batch + pl.program_id(1)) * nt + pl.program_id(2)
            _run_exchange(exchange, step, nc * batch * nt,
                          (refs[4:4 + n_ex], refs[6 + n_ex:6 + 2 * n_ex], refs[12 + 2 * n_ex:]))

        @pl.when(pl.program_id(2) == 0)
        def _():
            car_r[...] = jnp.zeros_like(car_r)
            car_i[...] = jnp.zeros_like(car_i)

        _scan_block(xr_ref[...], xi_ref[...], lr_ref[...], li_ref[...], car_r, car_i, s_r, s_i,
                    loc_r, loc_i, reverse=False)
        or_ref[...] = s_r[...].astype(or_ref.dtype)
        oi_ref[...] = s_i[...].astype(oi_ref.dtype)

    blk = pl.BlockSpec((tb, cb), lambda c, b, t: (b * nt + t, c))
    lam_spec = pl.BlockSpec((1, cb), lambda c, b, t: (0, c))
    res = pl.pallas_call(
        body, name=name,
        grid=(nc, batch, nt),
        in_specs=[blk, blk, lam_spec, lam_spec] + ex_specs,
        out_specs=[blk, blk] + ex_specs,
        out_shape=[jax.ShapeDtypeStruct((n, nch), BF16)] * 2 + ex_out,
        scratch_shapes=[pltpu.VMEM((1, cb), F32)] * 2 + [pltpu.VMEM((tb // SUBLANES, SUBLANES, cb), F32)] * 2
        + [pltpu.VMEM((tb, cb), F32)] * 2 + ex_sems,
        compiler_params=_params("arbitrary", "arbitrary", "arbitrary"),
    )(x_re, x_im, lam_re, lam_im, *ex_ins)
    return res[0], res[1], list(res[2:])


def ssm_scan_bwd(d_re, d_im, s_re, s_im, lam_re, lam_im, batch, name, exchange=None):
    n, nch = d_re.shape
    seq = n // batch
    tb, cb = _scan_tiles(seq, nch)
    nt, nc = seq // tb, nch // cb
    halo_rows = 2 * SUBLANES
    hb = tb // halo_rows
    ex_ins, ex_specs, ex_out, ex_sems = _exchange_args(exchange)
    n_ex = len(ex_ins)

    def body(*refs):
        xr_ref, xi_ref, sr_ref, si_ref, hr_ref, hi_ref, lr_ref, li_ref = refs[:8]
        or_ref, oi_ref, dlr_ref, dli_ref = refs[8 + n_ex:12 + n_ex]
        car_r, car_i, loc_r, loc_i, g_r, g_i = refs[12 + 2 * n_ex:18 + 2 * n_ex]
        b, t = pl.program_id(1), pl.program_id(2)
        if exchange is not None:
            step = (pl.program_id(0) * batch + b) * nt + t
            _run_exchange(exchange, step, nc * batch * nt,
                          (refs[8:8 + n_ex], refs[12 + n_ex:12 + 2 * n_ex], refs[18 + 2 * n_ex:]))

        @pl.when(t == 0)
        def _():
            car_r[...] = jnp.zeros_like(car_r)
            car_i[...] = jnp.zeros_like(car_i)

        _scan_block(xr_ref[...], xi_ref[...], lr_ref[...], -li_ref[...], car_r, car_i, g_r, g_i,
                    loc_r, loc_i, reverse=True)
        gr, gi = g_r[...], g_i[...]
        or_ref[...] = gr.astype(or_ref.dtype)
        oi_ref[...] = gi.astype(oi_ref.dtype)
        first_block = t == nt - 1
        row = lax.broadcasted_iota(jnp.int32, (tb, cb), 0)
        hr = jnp.where(first_block, 0.0, hr_ref[...].astype(F32)[halo_rows - 1:halo_rows, :])
        hi = jnp.where(first_block, 0.0, hi_ref[...].astype(F32)[halo_rows - 1:halo_rows, :])
        pr = jnp.where(row == 0, hr, pltpu.roll(sr_ref[...].astype(F32), 1, 0))
        pi_ = jnp.where(row == 0, hi, pltpu.roll(si_ref[...].astype(F32), 1, 0))
        dlr = jnp.sum(gr * pr + gi * pi_, axis=0, keepdims=True)
        dli = jnp.sum(gi * pr - gr * pi_, axis=0, keepdims=True)
        start = jnp.logical_and(b == 0, t == 0)

        @pl.when(start)
        def _():
            dlr_ref[...] = dlr
            dli_ref[...] = dli

        @pl.when(jnp.logical_not(start))
        def _():
            dlr_ref[...] += dlr
            dli_ref[...] += dli

    def blk(c, b, t):
        return b * nt + (nt - 1 - t)

    st_spec = pl.BlockSpec((tb, cb), lambda c, b, t: (blk(c, b, t), c))
    halo_spec = pl.BlockSpec((halo_rows, cb), lambda c, b, t: (jnp.maximum(blk(c, b, t) * hb - 1, 0), c))
    row_spec = pl.BlockSpec((1, cb), lambda c, b, t: (0, c))
    res = pl.pallas_call(
        body, name=name,
        grid=(nc, batch, nt),
        in_specs=[st_spec] * 4 + [halo_spec] * 2 + [row_spec] * 2 + ex_specs,
        out_specs=[st_spec, st_spec, row_spec, row_spec] + ex_specs,
        out_shape=[jax.ShapeDtypeStruct((n, nch), BF16)] * 2 + [jax.ShapeDtypeStruct((1, nch), F32)] * 2 + ex_out,
        scratch_shapes=[pltpu.VMEM((1, cb), F32)] * 2 + [pltpu.VMEM((tb // SUBLANES, SUBLANES, cb), F32)] * 2
        + [pltpu.VMEM((tb, cb), F32)] * 2 + ex_sems,
        compiler_params=_params("arbitrary", "arbitrary", "arbitrary"),
    )(d_re, d_im, s_re, s_im, s_re, s_im, lam_re, lam_im, *ex_ins)
    return res[0], res[1], res[2], res[3], list(res[4:])


def _pool_tiles(seq_len):
    return min(512, seq_len)


def _window_sums(x, n_steps, forward_in_time):
    rows = x.shape[0]
    k = 1
    for _ in range(n_steps):
        x = x + pltpu.roll(x, k if forward_in_time else rows - k, 0)
        k *= 2
    return x


def pool_fwd(u, w_pool, scale, batch, name):
    n, c = u.shape
    seq = n // batch
    tb = _pool_tiles(seq)
    nt = seq // tb
    gc = c // len(POOL_WINDOWS)
    hb = tb // POOL_HALO

    def body(x_ref, halo_ref, w_ref, sc_ref, y_ref, q_ref):
        t = pl.program_id(1)
        halo = jnp.where(t == 0, 0.0, halo_ref[...])
        full = jnp.concatenate([halo, x_ref[...]], axis=0)
        pos = lax.broadcasted_iota(jnp.int32, (tb, gc), 0) + t * tb + 1
        for gi, win in enumerate(POOL_WINDOWS):
            cols = slice(gi * gc, (gi + 1) * gc)
            sums = _window_sums(full[:, cols], gi + 1, True)[POOL_HALO:, :]
            cnt = jnp.minimum(pos, win).astype(F32)
            q = sums / cnt - x_ref[:, cols]
            r = jnp.dot(q.astype(BF16), w_ref[gi].astype(BF16), preferred_element_type=F32)
            q_ref[:, cols] = q.astype(q_ref.dtype)
            y_ref[:, cols] = (r * sc_ref[:, cols]).astype(y_ref.dtype)

    return pl.pallas_call(
        body, name=name,
        grid=(batch, nt),
        in_specs=[pl.BlockSpec((tb, c), lambda b, t: (b * nt + t, 0)),
                  pl.BlockSpec((POOL_HALO, c), lambda b, t: (jnp.maximum((b * nt + t) * hb - 1, 0), 0)),
                  pl.BlockSpec(w_pool.shape, lambda b, t: (0, 0, 0)),
                  pl.BlockSpec((1, c), lambda b, t: (0, 0))],
        out_specs=[pl.BlockSpec((tb, c), lambda b, t: (b * nt + t, 0))] * 2,
        out_shape=[jax.ShapeDtypeStruct((n, c), BF16)] * 2,
        compiler_params=_params("parallel", "arbitrary"),
    )(u, u, w_pool, scale)


def pool_bwd(dy, q, w_pool, scale, batch, name):
    n, c = dy.shape
    seq = n // batch
    tb = _pool_tiles(seq)
    nt = seq // tb
    ng = len(POOL_WINDOWS)
    gc = c // ng
    hb = tb // POOL_HALO
    n_blocks = n // POOL_HALO

    def body(dy_ref, dyh_ref, q_ref, w_ref, sc_ref, du_ref, dw_ref, dsc_ref):
        b, t = pl.program_id(0), pl.program_id(1)
        last = t == nt - 1
        dy_full = jnp.concatenate([dy_ref[...], jnp.where(last, 0.0, dyh_ref[...])], axis=0)
        pos = lax.broadcasted_iota(jnp.int32, (tb + POOL_HALO, gc), 0) + t * tb + 1
        start = jnp.logical_and(b == 0, t == 0)
        for gi, win in enumerate(POOL_WINDOWS):
            cols = slice(gi * gc, (gi + 1) * gc)
            w = w_ref[gi].astype(BF16)
            dr = dy_full[:, cols] * sc_ref[:, cols]
            dq = lax.dot_general(dr.astype(BF16), w, (((1,), (1,)), ((), ())), preferred_element_type=F32)
            cnt = jnp.minimum(pos, win).astype(F32)
            back = _window_sums(dq / cnt, gi + 1, False)
            du_ref[:, cols] = back[:tb, :] - dq[:tb, :]
            qb = q_ref[:, cols]
            r = jnp.dot(qb, w, preferred_element_type=F32)
            dw = lax.dot_general(qb, dr[:tb, :].astype(BF16), (((0,), (0,)), ((), ())), preferred_element_type=F32)
            dsc = jnp.sum(dy_ref[:, cols] * r, axis=0, keepdims=True)

            @pl.when(start)
            def _(gi=gi, cols=cols, dw=dw, dsc=dsc):
                dw_ref[gi] = dw
                dsc_ref[:, cols] = dsc

            @pl.when(jnp.logical_not(start))
            def _(gi=gi, cols=cols, dw=dw, dsc=dsc):
                dw_ref[gi] += dw
                dsc_ref[:, cols] += dsc

    blk = pl.BlockSpec((tb, c), lambda b, t: (b * nt + t, 0))
    halo = pl.BlockSpec((POOL_HALO, c), lambda b, t: (jnp.minimum((b * nt + t + 1) * hb, n_blocks - 1), 0))
    return pl.pallas_call(
        body, name=name,
        grid=(batch, nt),
        in_specs=[blk, halo, blk,
                  pl.BlockSpec(w_pool.shape, lambda b, t: (0, 0, 0)),
                  pl.BlockSpec((1, c), lambda b, t: (0, 0))],
        out_specs=[blk, pl.BlockSpec(w_pool.shape, lambda b, t: (0, 0, 0)), pl.BlockSpec((1, c), lambda b, t: (0, 0))],
        out_shape=[jax.ShapeDtypeStruct((n, c), F32), jax.ShapeDtypeStruct(w_pool.shape, F32),
                   jax.ShapeDtypeStruct((1, c), F32)],
        compiler_params=_params("arbitrary", "arbitrary"),
    )(dy, dy, q, w_pool, scale)


def _place():
    return lax.axis_index("x"), lax.axis_index("y"), lax.axis_index("c")


class GatherPlan:
    def __init__(self, arrs):
        self.ins = list(arrs)
        na = len(arrs)
        self.out_shape = [jax.ShapeDtypeStruct((N_DEV,) + a.shape, a.dtype) for a in arrs]
        self.sems = [pltpu.SemaphoreType.DMA((na, 7)), pltpu.SemaphoreType.DMA((na, 7)), pltpu.SemaphoreType.DMA((na,))]
        self.sizes = [math.prod(a.shape) * a.dtype.itemsize for a in arrs]

    def relay_steps(self, n_steps):
        total, done, steps = sum(self.sizes), 0, []
        for a, size in enumerate(self.sizes):
            done += size
            steps.append((a, min(n_steps - 1, (done * (n_steps - 1)) // total)))
        return steps

    def _copy(self, outs, sems, a, k, block, to, src=None):
        dst = outs[a].at[4 * block[0] + 2 * block[1] + block[2]]
        return pltpu.make_async_remote_copy(
            src_ref=dst if src is None else src, dst_ref=dst,
            send_sem=sems[0].at[a, k], recv_sem=sems[1].at[a, k], device_id=to, device_id_type=MESH)

    @staticmethod
    def _chips(x, y):
        return [(1 - x, y), (x, 1 - y), (1 - x, 1 - y)]

    def _local(self, ins, outs, sems, a, me):
        return pltpu.make_async_copy(ins[a], outs[a].at[4 * me[0] + 2 * me[1] + me[2]], sems[2].at[a])

    def start(self, ins, outs, sems):
        x, y, c = _place()
        me = (x, y, c)
        for a in range(len(ins)):
            self._local(ins, outs, sems, a, me).start()
            self._copy(outs, sems, a, 0, me, (x, y, 1 - c), src=ins[a]).start()
            for j, chip in enumerate(self._chips(x, y)):
                self._copy(outs, sems, a, 1 + j, me, (*chip, c), src=ins[a]).start()

    def relay(self, a, ins, outs, sems):
        x, y, c = _place()
        for j, chip in enumerate(self._chips(x, y)):
            self._copy(outs, sems, a, 1 + j, (*chip, c), (x, y, c)).wait_recv()
            self._copy(outs, sems, a, 4 + j, (*chip, c), (x, y, 1 - c)).start()

    def finish(self, ins, outs, sems):
        x, y, c = _place()
        me, sibling = (x, y, c), (x, y, 1 - c)
        for a in range(len(ins)):
            self._copy(outs, sems, a, 0, sibling, me).wait_recv()
            for j, chip in enumerate(self._chips(x, y)):
                self._copy(outs, sems, a, 4 + j, (*chip, 1 - c), me).wait_recv()
        for a in range(len(ins)):
            self._copy(outs, sems, a, 0, me, sibling, src=ins[a]).wait_send()
            for j, chip in enumerate(self._chips(x, y)):
                self._copy(outs, sems, a, 1 + j, me, (*chip, c), src=ins[a]).wait_send()
                self._copy(outs, sems, a, 4 + j, (*chip, c), sibling).wait_send()
            self._local(ins, outs, sems, a, me).wait()


class ChipScatterPlan:
    def __init__(self, arrs):
        self.ins = list(arrs)
        na = len(arrs)
        self.out_shape = [jax.ShapeDtypeStruct(a.shape, a.dtype) for a in arrs]
        self.sems = [pltpu.SemaphoreType.DMA((na, 3)), pltpu.SemaphoreType.DMA((na, 3)), pltpu.SemaphoreType.DMA((na,))]

    def relay_steps(self, n_steps):
        return []

    def _copy(self, ins, outs, sems, a, j, chip, c, mine):
        return pltpu.make_async_remote_copy(
            src_ref=ins[a].at[2 * chip[0] + chip[1]], dst_ref=outs[a].at[mine],
            send_sem=sems[0].at[a, j], recv_sem=sems[1].at[a, j], device_id=(*chip, c), device_id_type=MESH)

    def _all(self, ins, outs, sems):
        x, y, c = _place()
        mine = 2 * x + y
        remote = [self._copy(ins, outs, sems, a, j, chip, c, mine)
                  for a in range(len(ins)) for j, chip in enumerate(GatherPlan._chips(x, y))]
        local = [pltpu.make_async_copy(ins[a].at[mine], outs[a].at[mine], sems[2].at[a]) for a in range(len(ins))]
        return remote, local

    def start(self, ins, outs, sems):
        remote, local = self._all(ins, outs, sems)
        for cp in local + remote:
            cp.start()

    def finish(self, ins, outs, sems):
        remote, local = self._all(ins, outs, sems)
        for cp in remote:
            cp.wait_recv()
        for cp in remote:
            cp.wait_send()
        for cp in local:
            cp.wait()


def run_exchange(plan, name):
    n_in, n_out = len(plan.ins), len(plan.out_shape)

    def body(*refs):
        parts = (refs[:n_in], refs[n_in:n_in + n_out], refs[n_in + n_out:])
        plan.start(*parts)
        for part, _ in plan.relay_steps(1):
            plan.relay(part, *parts)
        plan.finish(*parts)

    return pl.pallas_call(
        body, name=name,
        in_specs=[ANY] * n_in, out_specs=[ANY] * n_out,
        out_shape=plan.out_shape, scratch_shapes=plan.sems,
    )(*plan.ins)


def swap_with_sibling(arrs, name):
    na = len(arrs)

    def body(*refs):
        ins, outs = refs[:na], refs[na:2 * na]
        send_sems, recv_sems = refs[2 * na:]
        x, y, c = _place()
        copies = [pltpu.make_async_remote_copy(
            src_ref=ins[a], dst_ref=outs[a], send_sem=send_sems.at[a], recv_sem=recv_sems.at[a],
            device_id=(x, y, 1 - c), device_id_type=MESH) for a in range(na)]
        for cp in copies:
            cp.start()
        for cp in copies:
            cp.wait()

    return pl.pallas_call(
        body, name=name,
        in_specs=[ANY] * na, out_specs=[ANY] * na,
        out_shape=[jax.ShapeDtypeStruct(a.shape, a.dtype) for a in arrs],
        scratch_shapes=[pltpu.SemaphoreType.DMA((na,)), pltpu.SemaphoreType.DMA((na,))],
    )(*arrs)


def adamw(w, gparts, m, v, name, tr=256):
    rows, cols = w.shape
    parts = gparts.shape[0]
    tr = _row_tile(rows, tr)
    c1 =1.0 - ADAM_B1 ** ADAM_STEP
    c2 = 1.0 - ADAM_B2 ** ADAM_STEP

    def body(w_ref, g_ref, m_ref, v_ref, go_ref, d_ref, mo_ref, vo_ref):
        g = g_ref[0].astype(F32)
        for p_ in range(1, parts):
            g = g + g_ref[p_].astype(F32)
        m_new = ADAM_B1 * m_ref[...] + (1.0 - ADAM_B1) * g
        v_new = ADAM_B2 * v_ref[...] + (1.0 - ADAM_B2) * (g * g)
        m_hat = m_new / c1
        v_hat = v_new / c2
        go_ref[...] = g
        d_ref[...] = -ADAM_LR * (m_hat / (jnp.sqrt(v_hat) + ADAM_EPS) + ADAM_WD * w_ref[...])
        mo_ref[...] = m_new
        vo_ref[...] = v_new

    blk = pl.BlockSpec((tr, cols), lambda i: (i, 0))
    return pl.pallas_call(
        body, name=name,
        grid=(rows // tr,),
        in_specs=[blk, pl.BlockSpec((parts, tr, cols), lambda i: (0, i, 0)), blk, blk],
        out_specs=[blk] * 4,
        out_shape=[jax.ShapeDtypeStruct((rows, cols), F32)] * 4,
        compiler_params=_params("parallel"),
    )(w, gparts, m, v)


def add2(a, b, name, out_dtype):
    return rowwise(lambda a, b: ([a.astype(F32) + b.astype(F32)], []), [a, b], [(a.shape[1], out_dtype)],
                   name=name, tr=256)[0]


def _block_diag(x):
    g, a, b = x.shape
    eye = jnp.eye(g, dtype=x.dtype)
    return (x[:, :, None, :] * eye[:, None, :, None]).reshape(g * a, g * b)


def _diag_blocks(x, a, b):
    per = x.shape[1] // b
    x5 = x.reshape(SSM_SUPER, per, a, per, b)
    eye = jnp.eye(per, dtype=x.dtype)
    return jnp.sum(x5 * eye[None, :, None, :, None], axis=3).reshape(SSM_SUPER * per, a, b)


def _swiglu_epi(g, u):
    s = _sigmoid(g)
    silu = g * s
    return u * (s * (1.0 + g * (1.0 - s))), silu, silu * u


def _residual_epi(scale, with_norm):
    if with_norm:
        def epi(acc, res, gain):
            out = res + scale * acc
            return out, _rms_tile(out, gain)
    else:
        def epi(acc, res):
            return (res + scale * acc,)
    return epi


def _next_norm(next_gain):
    if next_gain is None:
        return [], (F32,)
    return [next_gain], (F32, BF16)


def ffn_fwd(h, n, wi_g, wi_u, wo, next_gain, tag):
    dact_g, dact_u, act = matmul(n, [wi_g, wi_u], mode="nn", name=f"{tag}_in", separate=True, epi=_swiglu_epi,
                                 out_dtypes=(BF16, BF16, BF16), tn=1408)
    more, dtypes = _next_norm(next_gain)
    res = matmul(act, wo, mode="nn", name=f"{tag}_out", epi=_residual_epi(0.5, bool(more)), extras=[h] + more,
                 out_dtypes=dtypes, tk=1408)
    return res[0], (res[1] if more else None), (h, n, dact_g, dact_u, act)


def ffn_bwd(dh, saved, gain, wi_g, wi_u, wo, tag):
    h, n, dact_g, dact_u, act = saved
    dg, du = matmul(dh, wo, mode="nt", name=f"{tag}_out_dx", extras=[dact_g, dact_u], out_dtypes=(BF16, BF16),
                    epi=lambda acc, fg, fu: (0.5 * acc * fg, 0.5 * acc * fu), tn=1408)
    d_wo, = matmul(act, dh, mode="tn", name=f"{tag}_out_dw", epi=lambda acc: (0.5 * acc,), tm=1408)
    d_wi_g, d_wi_u = matmul(n, [dg, du], mode="tn", name=f"{tag}_in_dw", separate=True, tm=512, tn=1408)
    dh_new, d_gain = matmul([dg, du], [wi_g, wi_u], mode="nt", name=f"{tag}_in_dx", epi=_rms_bwd_tile,
                            extras=[h, gain, dh], out_dtypes=(F32,), col_sums=1, tm=512, tk=1408)
    return dh_new, d_gain, d_wi_g, d_wi_u, d_wo


def mix_fwd(h, n, lw, batch, next_gain, tag, exchange=None):
    sw = h.shape[1] // 2
    us, up = matmul(n, [lw["w_in"][:, :sw], lw["w_in"][:, sw:]], mode="nn", name=f"{tag}_in", separate=True,
                    out_dtypes=(F32, F32))
    lam_r, lam_i, bb_r, bb_i = ssm_prep(lw["lam_re"], lw["lam_im"], lw["log_dt"], lw["b_re"], lw["b_im"], f"{tag}_zoh")
    lam = (lam_r.reshape(1, SSM_CH), lam_i.reshape(1, SSM_CH))
    b_mats = [_block_diag(bb.reshape(SSM_GROUPS, SSM_STATE, SSM_GROUP_CH).transpose(0, 2, 1)).astype(BF16)
              for bb in (bb_r, bb_i)]
    c_mats = [_block_diag(cc.transpose(0, 2, 1)).astype(BF16) for cc in (lw["c_re"], -lw["c_im"])]
    bu_re, bu_im = matmul(us, b_mats, mode="nn", name=f"{tag}_bu", separate=True, diag=SSM_SUPER)
    s_re, s_im, exchanged = ssm_scan(bu_re, bu_im, *lam, batch, f"{tag}_scan", exchange)
    y0, y1 = matmul([s_re, s_im], c_mats, mode="nn", name=f"{tag}_c", diag=SSM_SUPER,
                    epi=lambda acc, u, d: (acc + d * u, _gelu(acc + d * u)), extras=[us, lw["ssm_d"]],
                    out_dtypes=(F32, BF16))
    y2, gl = matmul(y1, lw["w_glu"], mode="nn", name=f"{tag}_glu",
                    epi=lambda acc, y0: (_gelu(y0) * _sigmoid(acc), acc), extras=[y0], out_dtypes=(BF16, F32))
    yp, q = pool_fwd(up, lw["pool_w"], lw["pool_scale"], batch, f"{tag}_pool")
    more, dtypes = _next_norm(next_gain)
    res = matmul([y2, yp], [lw["w_out"][:sw], lw["w_out"][sw:]], mode="nn", name=f"{tag}_out",
                 epi=_residual_epi(1.0, bool(more)), extras=[h] + more, out_dtypes=dtypes)
    saved = (h, n, us, lam, b_mats, c_mats, s_re, s_im, y0, y1, gl, y2, yp, q)
    return res[0], (res[1] if more else None), saved, exchanged


def mix_bwd(dh, saved, lw, batch, tag, exchange=None):
    h, n, us, lam, b_mats, c_mats, s_re, s_im, y0, y1, gl, y2, yp, q = saved
    sw = h.shape[1] // 2
    w_out_s, w_out_p = lw["w_out"][:sw], lw["w_out"][sw:]
    d_wo_s, d_wo_p = matmul([y2, yp], dh, mode="tn", name=f"{tag}_out_dw", separate=True)
    def out_dx_epi(dy2, dyp, y0, gl):
        sg = _sigmoid(gl)
        return dy2, dyp, dy2 * _gelu(y0) * sg * (1.0 - sg)

    dy2, dyp, tg = matmul(dh, [w_out_s, w_out_p], mode="nt", name=f"{tag}_out_dx", separate=True, epi=out_dx_epi,
                          extras=[y0, gl], out_dtypes=(F32, F32, BF16))
    dup, d_pool_w, d_pool_scale = pool_bwd(dyp, q, lw["pool_w"], lw["pool_scale"], batch, f"{tag}_pool_bwd")
    def dy0_epi(acc, dy2, gl, y0, u):
        dy0 = (acc + dy2 * _sigmoid(gl)) * _gelu_grad(y0)
        return dy0, jnp.sum(dy0 * u, axis=0, keepdims=True)

    dy0, d_d = matmul(tg, lw["w_glu"], mode="nt", name=f"{tag}_glu_dx", epi=dy0_epi, extras=[dy2, gl, y0, us],
                      out_dtypes=(F32,), col_sums=1)
    d_w_glu, = matmul(y1, tg, mode="tn", name=f"{tag}_glu_dw")
    gd_re, gd_im = matmul(dy0, c_mats, mode="nt", name=f"{tag}_c_dx", separate=True, diag=SSM_SUPER)
    d_c_top, d_c_bot = matmul([s_re, s_im], dy0, mode="tn", name=f"{tag}_c_dw", separate=True, diag=SSM_SUPER)
    g_re, g_im, d_lam_r, d_lam_i, exchanged = ssm_scan_bwd(gd_re, gd_im, s_re, s_im, *lam, batch, f"{tag}_scan_bwd",
                                                           exchange)
    dus, = matmul([g_re, g_im], b_mats, mode="nt", name=f"{tag}_bu_dx", diag=SSM_SUPER,
                  epi=lambda acc, dy0, d: (acc + d * dy0,), extras=[dy0, lw["ssm_d"]])
    d_b_re, d_b_im = matmul(us, [g_re, g_im], mode="tn", name=f"{tag}_bu_dw", separate=True, diag=SSM_SUPER)
    d_bb_r = _diag_blocks(d_b_re, SSM_GROUP_CH, SSM_STATE).transpose(0, 2, 1).reshape(SSM_CH, SSM_GROUP_CH)
    d_bb_i = _diag_blocks(d_b_im, SSM_GROUP_CH, SSM_STATE).transpose(0, 2, 1).reshape(SSM_CH, SSM_GROUP_CH)
    d_lr, d_li, d_ldt, d_br, d_bi = ssm_prep_bwd(
        lw["lam_re"], lw["lam_im"], lw["log_dt"], lw["b_re"], lw["b_im"],
        d_lam_r.reshape(SSM_CH, 1), d_lam_i.reshape(SSM_CH, 1), d_bb_r, d_bb_i, f"{tag}_zoh_bwd")
    d_c_re = _diag_blocks(d_c_top, SSM_STATE, SSM_GROUP_CH).transpose(0, 2, 1)
    d_c_im = -_diag_blocks(d_c_bot, SSM_STATE, SSM_GROUP_CH).transpose(0, 2, 1)
    d_w_in_s, d_w_in_p = matmul(n, [dus, dup], mode="tn", name=f"{tag}_in_dw", separate=True)
    dh_new, d_gain = matmul([dus, dup], [lw["w_in"][:, :sw], lw["w_in"][:, sw:]], mode="nt", name=f"{tag}_in_dx",
                            epi=_rms_bwd_tile, extras=[h, lw["mix_norm"], dh], out_dtypes=(F32,), col_sums=1, tm=512)
    grads = dict(mix_norm=d_gain, w_in=jnp.concatenate([d_w_in_s, d_w_in_p], axis=1),
                 ssm_lambda_re=d_lr, ssm_lambda_im=d_li, ssm_log_dt=d_ldt, ssm_b_re=d_br, ssm_b_im=d_bi,
                 ssm_c_re=d_c_re, ssm_c_im=d_c_im, ssm_d=d_d, ssm_w_glu=d_w_glu, pool_w=d_pool_w,
                 pool_scale=d_pool_scale, w_out=jnp.concatenate([d_wo_s, d_wo_p], axis=0))
    return dh_new, grads, exchanged


def ple_fwd(h, n, p, w_gate, w_proj, next_gain, tag):
    e, = matmul(p, w_proj, mode="nn", name=f"{tag}_proj")
    if next_gain is None:
        def epi(acc, e, res):
            return res + _sigmoid(acc) * e, acc
        more, dtypes = [], (F32, F32)
    else:
        def epi(acc, e, res, gain):
            out = res + _sigmoid(acc) * e
            return out, acc, _rms_tile(out, gain)
        more, dtypes = [next_gain], (F32, F32, BF16)
    res = matmul(n, w_gate, mode="nn", name=f"{tag}_gate", epi=epi, extras=[e, h] + more, out_dtypes=dtypes, tm=512)
    return res[0], (res[2] if more else None), (h, n, e, res[1])


def ple_bwd(dh, saved, p, gain, w_gate, tag):
    h, n, e, pre = saved
    d = h.shape[1]

    def fn(dh, e, pre):
        s = _sigmoid(pre)
        return [dh * e * s * (1.0 - s), dh * s], []

    dpre, de = rowwise(fn, [dh, e, pre], [(d, BF16), (d, BF16)], name=f"{tag}_gate_bwd")
    d_w_gate, = matmul(n, dpre, mode="tn", name=f"{tag}_gate_dw")
    d_w_proj, = matmul(p, de, mode="tn", name=f"{tag}_proj_dw")
    dh_new, d_gain = matmul(dpre, w_gate, mode="nt", name=f"{tag}_gate_dx", epi=_rms_bwd_tile,
                            extras=[h, gain, dh], out_dtypes=(F32,), col_sums=1, tm=512)
    return dh_new, d_gain, d_w_gate, d_w_proj


def loss_head(h, gain, target, name):
    d = h.shape[1]

    def fn(h, g, t):
        r = lax.rsqrt(jnp.mean(h * h, axis=-1, keepdims=True) + EPS)
        diff = h * r * g - t
        sq = jnp.sum(jnp.sum(diff * diff, axis=1, keepdims=True), axis=0, keepdims=True)
        dy = diff * (1.0 / d)
        w = dy * g
        dh = r * w - h * (r * r * r) * jnp.mean(h * w, axis=-1, keepdims=True)
        return [dh], [sq, jnp.sum(dy * (h * r), axis=0, keepdims=True)]

    dh, sq, d_gain = rowwise(fn, [h, gain, target], [(d, F32)], [(1, 1), (1, d)], name=name)
    return 0.5 / d * sq[0, 0], dh, d_gain


SHARDED = {
    "ffn1_wi": 1, "ffn1_wo": 0, "w_in": 0, "ssm_w_glu": 0, "w_out": 0, "ffn2_wi": 1, "ffn2_wo": 0,
    "ple_w_gate": 0, "ple_w_proj": 1,
}
WEIGHTS = ["ffn1_norm", "ffn1_wi", "ffn1_wo", "mix_norm", "w_in", "ssm_lambda_re", "ssm_lambda_im", "ssm_log_dt",
           "ssm_b_re", "ssm_b_im", "ssm_c_re", "ssm_c_im", "ssm_d", "ssm_w_glu", "pool_w", "pool_scale", "w_out",
           "ffn2_norm", "ffn2_wi", "ffn2_wo", "ple_norm", "ple_w_gate", "ple_w_proj", "final_norm"]
REPLICATED = [n for n in WEIGHTS if n not in SHARDED]


def _unshard(gathered, axis):
    g = jnp.moveaxis(gathered, 0, axis)
    shp = g.shape
    return g.reshape(shp[:axis] + (shp[axis] * shp[axis + 1],) + shp[axis + 2:])


def _split_for_scatter(full, axis, c):
    shp = full.shape
    g = full.reshape(shp[:axis] + (4, 2, shp[axis] // N_DEV) + shp[axis + 1:])
    keep = lax.dynamic_index_in_dim(g, c, axis + 1, keepdims=False)
    send = lax.dynamic_index_in_dim(g, 1 - c, axis + 1, keepdims=False)
    return jnp.moveaxis(keep, axis, 0), jnp.moveaxis(send, axis, 0).astype(BF16)


def _pack(arrs):
    pieces = []
    for a in arrs:
        flat = a.reshape(-1)
        pad = (-flat.shape[0]) % PACK
        pieces.append(jnp.pad(flat, (0, pad)).reshape(-1, LANES))
    return jnp.concatenate(pieces, axis=0)


def _unpack(packed, shapes):
    out, row = [], 0
    for s in shapes:
        size = math.prod(s)
        rows = (size + PACK - 1) // PACK * SUBLANES
        out.append(packed[row:row + rows].reshape(-1)[:size].reshape(s))
        row += rows
    return out


class NoExchange:
    def __init__(self, full):
        self.full = full

    def layer_full(self, i):
        return {k: v[i] for k, v in self.full.items()}

    def fwd_exchange(self, i):
        return None

    def fwd_done(self, i, results):
        pass

    def bwd_exchange(self, i):
        return None

    def bwd_done(self, i, results):
        pass

    def layer_grads(self, i, grads):
        pass


def local_step(x, p, target, rep, hooks):
    batch, seq, d = x.shape
    n_tok = batch * seq
    h = x.reshape(n_tok, d)
    saved = []
    n = rms_fwd(h, rep["ffn1_norm"][0].reshape(1, d), "first_norm")
    for i in range(DEPTH):
        lw = _layer_weights(rep, hooks.layer_full(i), i)
        next_gain = rep["ffn1_norm"][i + 1].reshape(1, d) if i + 1 < DEPTH else None
        h, n, s1 = ffn_fwd(h, n, lw["ffn1_wi_g"], lw["ffn1_wi_u"], lw["ffn1_wo"], lw["mix_norm"], "ffn1")
        h, n, s2, exchanged = mix_fwd(h, n, lw, batch, lw["ffn2_norm"], "mix", hooks.fwd_exchange(i))
        hooks.fwd_done(i, exchanged)
        h, n, s3 = ffn_fwd(h, n, lw["ffn2_wi_g"], lw["ffn2_wi_u"], lw["ffn2_wo"], lw["ple_norm"], "ffn2")
        p_i = p[i].reshape(n_tok, -1)
        h, n, s4 = ple_fwd(h, n, p_i, lw["ple_w_gate"], lw["ple_w_proj"], next_gain, "ple")
        saved.append((s1, s2, s3, s4, p_i))
    loss, dh, d_final = loss_head(h, rep["final_norm"].reshape(1, d), target.reshape(n_tok, d), "loss_head")
    per_layer = [None] * DEPTH
    for i in reversed(range(DEPTH)):
        lw = _layer_weights(rep, hooks.layer_full(i), i)
        s1, s2, s3, s4, p_i = saved[i]
        g = {}
        dh, g["ple_norm"], g["ple_w_gate"], g["ple_w_proj"] = ple_bwd(dh, s4, p_i, lw["ple_norm"], lw["ple_w_gate"], "ple")
        dh, g["ffn2_norm"], wi_g, wi_u, g["ffn2_wo"] = ffn_bwd(
            dh, s3, lw["ffn2_norm"], lw["ffn2_wi_g"], lw["ffn2_wi_u"], lw["ffn2_wo"], "ffn2")
        g["ffn2_wi"] = jnp.concatenate([wi_g, wi_u], axis=1)
        dh, gm, exchanged = mix_bwd(dh, s2, lw, batch, "mix", hooks.bwd_exchange(i))
        hooks.bwd_done(i, exchanged)
        g.update(gm)
        dh, g["ffn1_norm"], wi_g, wi_u, g["ffn1_wo"] = ffn_bwd(
            dh, s1, lw["ffn1_norm"], lw["ffn1_wi_g"], lw["ffn1_wi_u"], lw["ffn1_wo"], "ffn1")
        g["ffn1_wi"] = jnp.concatenate([wi_g, wi_u], axis=1)
        hooks.layer_grads(i, g)
        per_layer[i] = g
    return loss, dh.reshape(batch, seq, d), per_layer, d_final


def _layer_weights(w, big, i):
    d = big["w_in"].shape[1]
    sw = d // 2
    lw = {k: big[k] for k in ("ffn1_wo", "w_in", "w_out", "ffn2_wo", "ple_w_gate", "ple_w_proj")}
    lw["w_glu"] = big["ssm_w_glu"]
    for k in ("ffn1_wi", "ffn2_wi"):
        half = big[k].shape[1] // 2
        lw[k + "_g"], lw[k + "_u"] = big[k][:, :half], big[k][:, half:]
    lw["pool_w"] = w["pool_w"][i]
    for k in ("ffn1_norm", "mix_norm", "ffn2_norm", "ple_norm"):
        lw[k] = w[k][i].reshape(1, d)
    lw["ssm_d"] = w["ssm_d"][i].reshape(1, sw)
    lw["pool_scale"] = w["pool_scale"][i].reshape(1, sw)
    lw["lam_re"] = w["ssm_lambda_re"][i].reshape(SSM_CH, 1)
    lw["lam_im"] = w["ssm_lambda_im"][i].reshape(SSM_CH, 1)
    lw["log_dt"] = jnp.repeat(w["ssm_log_dt"][i], SSM_STATE).reshape(SSM_CH, 1)
    lw["b_re"] = w["ssm_b_re"][i].reshape(SSM_CH, SSM_GROUP_CH)
    lw["b_im"] = w["ssm_b_im"][i].reshape(SSM_CH, SSM_GROUP_CH)
    lw["c_re"] = w["ssm_c_re"][i]
    lw["c_im"] = w["ssm_c_im"][i]
    return lw


class MeshExchange:
    def __init__(self, shards):
        self.shards = shards
        self.c = lax.axis_index("c")
        self.gathered = {0: run_exchange(GatherPlan(shards[0]), "gather_first_layer")}
        self.chip_sums = {}
        self.from_chips = {}

    def layer_full(self, i):
        return {k: _unshard(g, SHARDED[k]) for k, g in zip(SHARDED, self.gathered[i])}

    def fwd_exchange(self, i):
        return GatherPlan(self.shards[i + 1]) if i + 1 < DEPTH else None

    def fwd_done(self, i, results):
        if results:
            self.gathered[i + 1] = results

    def layer_grads(self, i, grads):
        keeps, sends = zip(*[_split_for_scatter(grads[k], SHARDED[k], self.c) for k in SHARDED])
        from_sibling = swap_with_sibling(list(sends), "reduce_core_pair")
        sums = []
        for k, keep, got in zip(SHARDED, keeps, from_sibling):
            cols = keep.shape[-1]
            sums.append(add2(keep.reshape(-1, cols), got.reshape(-1, cols), f"sum_core_pair_{k}", BF16).reshape(keep.shape))
        if i == 0:
            self.from_chips[0] = run_exchange(ChipScatterPlan(sums), "reduce_chips_last_layer")
        else:
            self.chip_sums[i] = sums

    def bwd_exchange(self, i):
        return ChipScatterPlan(self.chip_sums[i + 1]) if i + 1 in self.chip_sums else None

    def bwd_done(self, i, results):
        if results:
            self.from_chips[i + 1] = results


def kernel(x, p, ffn1_norm, ffn1_wi, ffn1_wo, mix_norm, w_in, ssm_lambda_re, ssm_lambda_im, ssm_log_dt, ssm_b_re, ssm_b_im, ssm_c_re, ssm_c_im, ssm_d, ssm_w_glu, pool_w, pool_scale, w_out, ffn2_norm, ffn2_wi, ffn2_wo, ple_norm, ple_w_gate, ple_w_proj, final_norm, loss_target, m_ffn1_norm, m_ffn1_wi, m_ffn1_wo, m_mix_norm, m_w_in, m_ssm_lambda_re, m_ssm_lambda_im, m_ssm_log_dt, m_ssm_b_re, m_ssm_b_im, m_ssm_c_re, m_ssm_c_im, m_ssm_d, m_ssm_w_glu, m_pool_w, m_pool_scale, m_w_out, m_ffn2_norm, m_ffn2_wi, m_ffn2_wo, m_ple_norm, m_ple_w_gate, m_ple_w_proj, m_final_norm, v_ffn1_norm, v_ffn1_wi, v_ffn1_wo, v_mix_norm, v_w_in, v_ssm_lambda_re, v_ssm_lambda_im, v_ssm_log_dt, v_ssm_b_re, v_ssm_b_im, v_ssm_c_re, v_ssm_c_im, v_ssm_d, v_ssm_w_glu, v_pool_w, v_pool_scale, v_w_out, v_ffn2_norm, v_ffn2_wi, v_ffn2_wo, v_ple_norm, v_ple_w_gate, v_ple_w_proj, v_final_norm):
    args = dict(locals())
    wts = {k: args[k] for k in WEIGHTS}
    rep = {k: wts[k] for k in REPLICATED}

    shards = [[wts[k][i].astype(BF16) for k in SHARDED] for i in range(DEPTH)]
    exchange = MeshExchange(shards)
    loss_local, grad_x, per_layer, d_final = local_step(x, p, loss_target, rep, exchange)
    loss = lax.psum(loss_local, ("x", "y", "c"))

    outs = {}
    for j, k in enumerate(SHARDED):
        shp = wts[k].shape
        cols = shp[-1]
        parts = jnp.stack([exchange.from_chips[i][j] for i in range(DEPTH)], axis=1)
        res = adamw(wts[k].reshape(-1, cols), parts.reshape(4, -1, cols), args["m_" + k].reshape(-1, cols),
                    args["v_" + k].reshape(-1, cols), f"adamw_{k}")
        outs[k] = [r.reshape(shp) for r in res]

    rep_shapes = [wts[k].shape for k in REPLICATED]
    small = {k: jnp.stack([g[k] for g in per_layer], axis=0) for k in REPLICATED if k != "final_norm"}
    small["final_norm"] = d_final
    packed_g = _pack([small[k].reshape(wts[k].shape) for k in REPLICATED])
    all_g, = run_exchange(GatherPlan([packed_g]), "gather_small_grads")
    res = adamw(_pack([wts[k] for k in REPLICATED]), all_g, _pack([args["m_" + k] for k in REPLICATED]),
                _pack([args["v_" + k] for k in REPLICATED]), "adamw_small")
    unpacked = [_unpack(r, rep_shapes) for r in res]
    for j, k in enumerate(REPLICATED):
        outs[k] = [unpacked[q][j] for q in range(4)]

    result = [loss, grad_x]
    for q in range(4):
        result += [outs[k][q] for k in WEIGHTS]
    return tuple(result)
```

```python
import math

import jax
import jax.numpy as jnp
from jax import lax
from jax.experimental import pallas as pl
from jax.experimental.pallas import tpu as pltpu

F32 = jnp.float32
BF16 = jnp.bfloat16
MESH = pl.DeviceIdType.MESH
ANY = pl.BlockSpec(memory_space=pl.ANY)

N_DEV = 8
DEPTH = 4
EPS = 1e-6
SSM_GROUPS = 32
SSM_GROUP_CH = 16
SSM_STATE = 64
SSM_CH = SSM_GROUPS * SSM_STATE
SSM_SUPER = 2
POOL_WINDOWS = (2, 4, 8, 16)
POOL_HALO = 16
ADAM_LR, ADAM_B1, ADAM_B2, ADAM_EPS, ADAM_WD, ADAM_STEP = 0.001, 0.9, 0.999, 1e-08, 0.01, 10

V7X_VMEM_BYTES = 64 * 1024 * 1024
VMEM_LIMIT_BYTES = V7X_VMEM_BYTES - 12 * 1024 * 1024
LANES = 128
SUBLANES = 8
PACK = SUBLANES * LANES


def _params(*sem):
    return pltpu.CompilerParams(dimension_semantics=sem or None, vmem_limit_bytes=VMEM_LIMIT_BYTES)


def _tile(n, pref):
    if n <= pref:
        return n
    t = pref - pref % LANES
    while t >= LANES:
        if n % t == 0:
            return t
        t -= LANES
    raise ValueError(f"no lane-aligned tile for {n}")


def _row_tile(rows, pref):
    if rows <= pref:
        return rows
    t = pref - pref % SUBLANES
    while t >= SUBLANES:
        if rows % t == 0:
            return t
        t -= SUBLANES
    raise ValueError(f"no sublane-aligned tile for {rows}")


_DIMS = {"nn": ((1,), (0,)), "nt": ((1,), (1,)), "tn": ((0,), (0,))}


def matmul(a, b, *, mode, name, out_dtypes=None, epi=None, extras=(), separate=False, diag=1, col_sums=0,
           stack_out=False, scale=1.0, tm=1024, tn=1024, tk=1024):
    a_list = list(a) if isinstance(a, list) else [a]
    b_list = list(b) if isinstance(b, list) else [b]
    a_planes = [x[1] if isinstance(x, tuple) else None for x in a_list]
    b_planes = [x[1] if isinstance(x, tuple) else None for x in b_list]
    a_list = [x[0] if isinstance(x, tuple) else x for x in a_list]
    b_list = [x[0] if isinstance(x, tuple) else x for x in b_list]
    a_shape, b_shape = a_list[0].shape[-2:], b_list[0].shape[-2:]
    n_terms = max(len(a_list), len(b_list))
    a_idx = [0] * n_terms if len(a_list) == 1 else list(range(n_terms))
    b_idx = [0] * n_terms if len(b_list) == 1 else list(range(n_terms))
    n_acc = n_terms if separate else 1
    assert not (stack_out or scale != 1.0) or epi is None
    if out_dtypes is None:
        out_dtypes = (F32,) * (1 if (epi is not None or stack_out) else n_acc)
    in_place = epi is None
    if mode == "tn":
        K, M = a_shape
        K2, N = b_shape
    elif mode == "nt":
        M, K = a_shape
        N, K2 = b_shape
    else:
        M, K = a_shape
        K2, N = b_shape
    assert K == K2, (name, a_shape, b_shape)
    if mode == "tn":
        tm, tn, tk = _tile(M // diag, tm), _tile(N // diag, tn), _tile(K, tk)
        nk = K // tk
        N = N // diag
        row_tiles, col_tiles = (M // diag) // tm, N // tn
        a_blk, a_map = (tk, tm), lambda i, j, k: (k, i)
        b_blk, b_map = (tk, tn), lambda i, j, k: (k, (i // row_tiles) * col_tiles + j)
    else:
        tm, tn, tk = _tile(M, tm), _tile(N // diag, tn), _tile(K // diag, tk)
        nk = (K // diag) // tk
        col_tiles = (N // diag) // tn
        a_blk, a_map = (tm, tk), lambda i, j, k: (i, (j // col_tiles) * nk + k)
        if mode == "nt":
            b_blk, b_map = (tn, tk), lambda i, j, k: (j, (j // col_tiles) * nk + k)
        else:
            b_blk, b_map = (tk, tn), lambda i, j, k: ((j // col_tiles) * nk + k, j)

    def plane_spec(blk, index_map, plane):
        if plane is None:
            return pl.BlockSpec(blk, index_map)
        return pl.BlockSpec((None,) + blk, lambda i, j, k: (plane,) + index_map(i, j, k))

    a_specs = [plane_spec(a_blk, a_map, p_) for p_ in a_planes]
    b_specs = [plane_spec(b_blk, b_map, p_) for p_ in b_planes]
    assert not col_sums or N == tn, (name, N, tn)
    ex_specs = []
    for e in extras:
        if e.shape == (M, N):
            ex_specs.append(pl.BlockSpec((tm, tn), lambda i, j, k: (i, j)))
        elif e.shape == (1, N):
            ex_specs.append(pl.BlockSpec((1, tn), lambda i, j, k: (0, j)))
        elif e.shape == (M, 1):
            ex_specs.append(pl.BlockSpec((tm, 1), lambda i, j, k: (i, 0)))
        else:
            raise ValueError((name, e.shape, (M, N)))
    na, nb, ne, no = len(a_list), len(b_list), len(extras), len(out_dtypes)
    dims = (_DIMS[mode], ((), ()))

    def body(*refs):
        a_refs = refs[:na]
        b_refs = refs[na:na + nb]
        ex_refs = refs[na + nb:na + nb + ne]
        out_refs = refs[na + nb + ne:na + nb + ne + no]
        sum_refs = refs[na + nb + ne + no:na + nb + ne + no + col_sums]
        acc_refs = refs[na + nb + ne + no + col_sums:]
        a_vals = [r[...].astype(BF16) for r in a_refs]
        b_vals = [r[...].astype(BF16) for r in b_refs]
        prods = [lax.dot_general(a_vals[a_idx[t]], b_vals[b_idx[t]], dims, preferred_element_type=F32)
                 for t in range(n_terms)]
        if not separate:
            total = prods[0]
            for p_ in prods[1:]:
                total = total + p_
            prods = [total]

        def finish(accs):
            res = epi(*accs, *[e[...] for e in ex_refs]) if epi is not None else tuple(accs)
            for r, v in zip(out_refs, res[:no]):
                r[...] = v.astype(r.dtype)
            first_rows = pl.program_id(0) == 0
            for r, v in zip(sum_refs, res[no:]):
                @pl.when(first_rows)
                def _(r=r, v=v):
                    r[...] = v

                @pl.when(jnp.logical_not(first_rows))
                def _(r=r, v=v):
                    r[...] += v

        if in_place:
            dst = [(out_refs[0], t) for t in range(n_acc)] if stack_out else [(r, None) for r in out_refs]

            def read(r, t):
                return r[...] if t is None else r[t]

            def write(r, t, v):
                if t is None:
                    r[...] = v
                else:
                    r[t] = v

            if nk == 1:
                for (r, t), v in zip(dst, prods):
                    write(r, t, v * scale if scale != 1.0 else v)
            else:
                k = pl.program_id(2)

                @pl.when(k == 0)
                def _():
                    for (r, t), v in zip(dst, prods):
                        write(r, t, v)

                @pl.when(jnp.logical_and(k > 0, k < nk - 1))
                def _():
                    for (r, t), v in zip(dst, prods):
                        write(r, t, read(r, t) + v)

                @pl.when(k == nk - 1)
                def _():
                    for (r, t), v in zip(dst, prods):
                        total = read(r, t) + v
                        write(r, t, total * scale if scale != 1.0 else total)
        elif nk == 1:
            finish(prods)
        else:
            k = pl.program_id(2)

            @pl.when(k == 0)
            def _():
                for r, v in zip(acc_refs, prods):
                    r[...] = v

            @pl.when(jnp.logical_and(k > 0, k < nk - 1))
            def _():
                for r, v in zip(acc_refs, prods):
                    r[...] += v

            @pl.when(k == nk - 1)
            def _():
                finish([r[...] + v for r, v in zip(acc_refs, prods)])

    if stack_out:
        out_specs = [pl.BlockSpec((n_acc, tm, tn), lambda i, j, k: (0, i, j))]
        out_shape = [jax.ShapeDtypeStruct((n_acc, M, N), F32)]
    else:
        out_specs = [pl.BlockSpec((tm, tn), lambda i, j, k: (i, j))] * no
        out_shape = [jax.ShapeDtypeStruct((M, N), dt) for dt in out_dtypes]
    outs = pl.pallas_call(
        body,
        name=name,
        grid=(M // tm, N // tn, nk),
        in_specs=a_specs + b_specs + ex_specs,
        out_specs=out_specs + [pl.BlockSpec((1, tn), lambda i, j, k: (0, j))] * col_sums,
        out_shape=out_shape + [jax.ShapeDtypeStruct((1, N), F32)] * col_sums,
        scratch_shapes=[pltpu.VMEM((tm, tn), F32)] * (n_acc if (nk > 1 and not in_place) else 0),
        compiler_params=_params(*(("arbitrary",) * 3 if col_sums else ("parallel", "parallel", "arbitrary"))),
    )(*a_list, *b_list, *extras)
    return outs


def rowwise(fn, ins, outs, accs=(), *, name, tr=512):
    R = max(x.shape[0] for x in ins)
    tr = _row_tile(R, tr)
    in_specs = []
    for x in ins:
        if x.shape[0] == R and x.ndim == 2:
            in_specs.append(pl.BlockSpec((tr, x.shape[1]), lambda i: (i, 0)))
        else:
            in_specs.append(pl.BlockSpec(x.shape, lambda i, _n=x.ndim: (0,) * _n))
    ni, no = len(ins), len(outs)

    def body(*refs):
        i = pl.program_id(0)
        row_vals, acc_vals = fn(*[r[...] for r in refs[:ni]])
        for r, v in zip(refs[ni:ni + no], row_vals):
            r[...] = v.astype(r.dtype)
        for r, v in zip(refs[ni + no:], acc_vals):
            @pl.when(i == 0)
            def _(r=r, v=v):
                r[...] = v

            @pl.when(i > 0)
            def _(r=r, v=v):
                r[...] += v

    return pl.pallas_call(
        body,
        name=name,
        grid=(R // tr,),
        in_specs=in_specs,
        out_specs=[pl.BlockSpec((tr, c), lambda i: (i, 0)) for c, _ in outs]
        + [pl.BlockSpec(s, lambda i: (0, 0)) for s in accs],
        out_shape=[jax.ShapeDtypeStruct((R, c), dt) for c, dt in outs]
        + [jax.ShapeDtypeStruct(s, F32) for s in accs],
        compiler_params=_params("arbitrary"),
    )(*ins)


def _sigmoid(x):
    return 1.0 / (1.0 + jnp.exp(-x))


_GELU_C = math.sqrt(2.0 / math.pi)


def _gelu(x):
    return 0.5 * x * (1.0 + jnp.tanh(_GELU_C * (x + 0.044715 * (x * x * x))))


def _gelu_grad(x):
    t = jnp.tanh(_GELU_C * (x + 0.044715 * (x * x * x)))
    return 0.5 * (1.0 + t) + 0.5 * x * (1.0 - t * t) * (_GELU_C * (1.0 + 3.0 * 0.044715 * (x * x)))


def rms_fwd(x, g, name):
    def fn(x, g):
        r = lax.rsqrt(jnp.mean(x * x, axis=-1, keepdims=True) + EPS)
        return [x * r * g], []

    return rowwise(fn, [x, g], [(x.shape[1], BF16)], name=name)[0]


def _rms_tile(x, g):
    return x * lax.rsqrt(jnp.mean(x * x, axis=-1, keepdims=True) + EPS) * g


def _rms_bwd_tile(dn, x, g, dres):
    r = lax.rsqrt(jnp.mean(x * x, axis=-1, keepdims=True) + EPS)
    w = dn * g
    dx = r * w - x * (r * r * r) * jnp.mean(x * w, axis=-1, keepdims=True)
    return dres + dx, jnp.sum(dn * (x * r), axis=0, keepdims=True)


def _whole(shape):
    return pl.BlockSpec(shape, lambda: (0,) * len(shape))


def _zoh(lr, li, ldt):
    dt = jnp.exp(ldt)
    mag = jnp.exp(lr * dt)
    ar, ai = mag * jnp.cos(li * dt), mag * jnp.sin(li * dt)
    den = lr * lr + li * li
    kr = ((ar - 1.0) * lr + ai * li) / den
    ki = (ai * lr - (ar - 1.0) * li) / den
    return dt, ar, ai, den, kr, ki


def ssm_prep(lam_re, lam_im, log_dt, b_re, b_im, name):
    n = lam_re.shape[0]

    def body(lr_ref, li_ref, ldt_ref, br_ref, bi_ref, ar_ref, ai_ref, bbr_ref, bbi_ref):
        _, ar, ai, _, kr, ki = _zoh(lr_ref[...], li_ref[...], ldt_ref[...])
        br, bi = br_ref[...], bi_ref[...]
        ar_ref[...] = ar
        ai_ref[...] = ai
        bbr_ref[...] = kr * br - ki * bi
        bbi_ref[...] = kr * bi + ki * br

    col, mat = (n, 1), (n, SSM_GROUP_CH)
    return pl.pallas_call(
        body, name=name,
        in_specs=[_whole(col)] * 3 + [_whole(mat)] * 2,
        out_specs=[_whole(col)] * 2 + [_whole(mat)] * 2,
        out_shape=[jax.ShapeDtypeStruct(col, F32)] * 2 + [jax.ShapeDtypeStruct(mat, F32)] * 2,
        compiler_params=_params(),
    )(lam_re, lam_im, log_dt, b_re, b_im)


def ssm_prep_bwd(lam_re, lam_im, log_dt, b_re, b_im, d_ar, d_ai, d_bbr, d_bbi, name):
    n = lam_re.shape[0]
    n_groups = n // SSM_STATE

    def body(lr_ref, li_ref, ldt_ref, br_ref, bi_ref, dar_ref, dai_ref, dbr_ref, dbi_ref,
             glr_ref, gli_ref, gdt_ref, gbr_ref, gbi_ref):
        lr, li = lr_ref[...], li_ref[...]
        dt, ar, ai, den, kr, ki = _zoh(lr, li, ldt_ref[...])
        br, bi, dbr, dbi = br_ref[...], bi_ref[...], dbr_ref[...], dbi_ref[...]
        gbr_ref[...] = kr * dbr + ki * dbi
        gbi_ref[...] = kr * dbi - ki * dbr
        gkr = jnp.sum(br * dbr + bi * dbi, axis=1, keepdims=True)
        gki = jnp.sum(br * dbi - bi * dbr, axis=1, keepdims=True)
        gar = dar_ref[...] + (gkr * lr - gki * li) / den
        gai = dai_ref[...] + (gki * lr + gkr * li) / den
        qr, qi = -(kr * lr + ki * li) / den, -(ki * lr - kr * li) / den
        g1r, g1i = qr * gkr + qi * gki, qr * gki - qi * gkr
        g2r, g2i = dt * (ar * gar + ai * gai), dt * (ar * gai - ai * gar)
        glr_ref[...] = g1r + g2r
        gli_ref[...] = g1i + g2i
        pr, pi_ = lr * ar - li * ai, lr * ai + li * ar
        gdt = (pr * gar + pi_ * gai) * dt
        grp = lax.broadcasted_iota(jnp.int32, (n, n_groups), 0) // SSM_STATE
        sel = grp == lax.broadcasted_iota(jnp.int32, (n, n_groups), 1)
        gdt_ref[...] = jnp.sum(jnp.where(sel, gdt, 0.0), axis=0, keepdims=True)

    col, mat = (n, 1), (n, SSM_GROUP_CH)
    return pl.pallas_call(
        body, name=name,
        in_specs=[_whole(col)] * 3 + [_whole(mat)] * 2 + [_whole(col)] * 2 + [_whole(mat)] * 2,
        out_specs=[_whole(col)] * 2 + [_whole((1, n_groups))] + [_whole(mat)] * 2,
        out_shape=[jax.ShapeDtypeStruct(col, F32)] * 2 + [jax.ShapeDtypeStruct((1, n_groups), F32)]
        + [jax.ShapeDtypeStruct(mat, F32)] * 2,
        compiler_params=_params(),
    )(lam_re, lam_im, log_dt, b_re, b_im, d_ar, d_ai, d_bbr, d_bbi)


def _cmul(ar, ai, br, bi):
    return ar * br - ai * bi, ar * bi + ai * br


def _scan_block(xr, xi, lr, li, carry_r, carry_i, or_ref, oi_ref, loc_r, loc_i, reverse):
    tb, cb = xr.shape
    ng = tb // SUBLANES
    xr = xr.reshape(ng, SUBLANES, cb)
    xi = xi.reshape(ng, SUBLANES, cb)
    rid = lax.broadcasted_iota(jnp.int32, (ng, SUBLANES, cb), 1)
    pr, pi_ = lr.reshape(1, 1, cb), li.reshape(1, 1, cb)
    powers = []
    for k in (1, 2, 4):
        powers.append((pr, pi_))
        shift = SUBLANES - k if reverse else k
        sr, si = pltpu.roll(xr, shift, 1), pltpu.roll(xi, shift, 1)
        keep = (rid < SUBLANES - k) if reverse else (rid >= k)
        tr_, ti_ = _cmul(pr, pi_, sr, si)
        xr = xr + jnp.where(keep, tr_, 0.0)
        xi = xi + jnp.where(keep, ti_, 0.0)
        pr, pi_ = _cmul(pr, pi_, pr, pi_)
    loc_r[...] = xr
    loc_i[...] = xi
    (p1r, p1i), (p2r, p2i), (p4r, p4i) = powers
    dist = lax.broadcasted_iota(jnp.int32, (SUBLANES, cb), 0)
    if reverse:
        dist = SUBLANES - 1 - dist
    wr = jnp.broadcast_to(p1r.reshape(1, cb), (SUBLANES, cb))
    wi = jnp.broadcast_to(p1i.reshape(1, cb), (SUBLANES, cb))
    for bit, (qr, qi) in ((1, (p1r, p1i)), (2, (p2r, p2i)), (4, (p4r, p4i))):
        mr, mi = _cmul(wr, wi, qr.reshape(1, cb), qi.reshape(1, cb))
        on = (dist & bit) != 0
        wr, wi = jnp.where(on, mr, wr), jnp.where(on, mi, wi)
    last = 0 if reverse else SUBLANES - 1

    def step(j, carry):
        cr, ci = carry
        g = (ng - 1 - j) if reverse else j
        fr = loc_r[g] + (wr * cr - wi * ci)
        fi = loc_i[g] + (wr * ci + wi * cr)
        rows = pl.ds(pl.multiple_of(g * SUBLANES, SUBLANES), SUBLANES)
        or_ref[rows, :] = fr
        oi_ref[rows, :] = fi
        return fr[last:last + 1, :], fi[last:last + 1, :]

    cr, ci = lax.fori_loop(0, ng, step, (carry_r[...], carry_i[...]))
    carry_r[...] = cr
    carry_i[...] = ci


def _scan_tiles(seq_len, n_ch):
    return min(256, seq_len), min(512, n_ch)


def _run_exchange(plan, step, n_steps, refs):
    @pl.when(step == 0)
    def _():
        plan.start(*refs)

    for part, at in plan.relay_steps(n_steps):
        @pl.when(step == at)
        def _(part=part):
            plan.relay(part, *refs)

    @pl.when(step == n_steps - 1)
    def _():
        plan.finish(*refs)


def _exchange_args(plan):
    if plan is None:
        return [], [], [], []
    return list(plan.ins), [ANY] * len(plan.ins), list(plan.out_shape), list(plan.sems)


def ssm_scan(x_re, x_im, lam_re, lam_im, batch, name, exchange=None):
    n, nch = x_re.shape
    seq = n // batch
    tb, cb = _scan_tiles(seq, nch)
    nt, nc = seq // tb, nch // cb
    ex_ins, ex_specs, ex_out, ex_sems = _exchange_args(exchange)
    n_ex = len(ex_ins)

    def body(*refs):
        xr_ref, xi_ref, lr_ref, li_ref = refs[:4]
        or_ref, oi_ref = refs[4 + n_ex:6 + n_ex]
        car_r, car_i, loc_r, loc_i, s_r, s_i = refs[6 + 2 * n_ex:12 + 2 * n_ex]
        if exchange is not None:
            step = (pl.program_id(0) * batch + pl.program_id(1)) * nt + pl.program_id(2)
            _run_exchange(exchange, step, nc * batch * nt,
                          (refs[4:4 + n_ex], refs[6 + n_ex:6 + 2 * n_ex], refs[12 + 2 * n_ex:]))

        @pl.when(pl.program_id(2) == 0)
        def _():
            car_r[...] = jnp.zeros_like(car_r)
            car_i[...] = jnp.zeros_like(car_i)

        _scan_block(xr_ref[...], xi_ref[...], lr_ref[...], li_ref[...], car_r, car_i, s_r, s_i,
                    loc_r, loc_i, reverse=False)
        or_ref[...] = s_r[...].astype(or_ref.dtype)
        oi_ref[...] = s_i[...].astype(oi_ref.dtype)

    blk = pl.BlockSpec((tb, cb), lambda c, b, t: (b * nt + t, c))
    lam_spec = pl.BlockSpec((1, cb), lambda c, b, t: (0, c))
    res = pl.pallas_call(
        body, name=name,
        grid=(nc, batch, nt),
        in_specs=[blk, blk, lam_spec, lam_spec] + ex_specs,
        out_specs=[blk, blk] + ex_specs,
        out_shape=[jax.ShapeDtypeStruct((n, nch), BF16)] * 2 + ex_out,
        scratch_shapes=[pltpu.VMEM((1, cb), F32)] * 2 + [pltpu.VMEM((tb // SUBLANES, SUBLANES, cb), F32)] * 2
        + [pltpu.VMEM((tb, cb), F32)] * 2 + ex_sems,
        compiler_params=_params("arbitrary", "arbitrary", "arbitrary"),
    )(x_re, x_im, lam_re, lam_im, *ex_ins)
    return res[0], res[1], list(res[2:])


def ssm_scan_bwd(d_re, d_im, s_re, s_im, lam_re, lam_im, batch, name, exchange=None):
    n, nch = d_re.shape
    seq = n // batch
    tb, cb = _scan_tiles(seq, nch)
    nt, nc = seq // tb, nch // cb
    halo_rows = 2 * SUBLANES
    hb = tb // halo_rows
    ex_ins, ex_specs, ex_out, ex_sems = _exchange_args(exchange)
    n_ex = len(ex_ins)

    def body(*refs):
        xr_ref, xi_ref, sr_ref, si_ref, hr_ref, hi_ref, lr_ref, li_ref = refs[:8]
        or_ref, oi_ref, dlr_ref, dli_ref = refs[8 + n_ex:12 + n_ex]
        car_r, car_i, loc_r, loc_i, g_r, g_i = refs[12 + 2 * n_ex:18 + 2 * n_ex]
        b, t = pl.program_id(1), pl.program_id(2)
        if exchange is not None:
            step = (pl.program_id(0) * batch + b) * nt + t
            _run_exchange(exchange, step, nc * batch * nt,
                          (refs[8:8 + n_ex], refs[12 + n_ex:12 + 2 * n_ex], refs[18 + 2 * n_ex:]))

        @pl.when(t == 0)
        def _():
            car_r[...] = jnp.zeros_like(car_r)
            car_i[...] = jnp.zeros_like(car_i)

        _scan_block(xr_ref[...], xi_ref[...], lr_ref[...], -li_ref[...], car_r, car_i, g_r, g_i,
                    loc_r, loc_i, reverse=True)
        gr, gi = g_r[...], g_i[...]
        or_ref[...] = gr.astype(or_ref.dtype)
        oi_ref[...] = gi.astype(oi_ref.dtype)
        first_block = t == nt - 1
        row = lax.broadcasted_iota(jnp.int32, (tb, cb), 0)
        hr = jnp.where(first_block, 0.0, hr_ref[...].astype(F32)[halo_rows - 1:halo_rows, :])
        hi = jnp.where(first_block, 0.0, hi_ref[...].astype(F32)[halo_rows - 1:halo_rows, :])
        pr = jnp.where(row == 0, hr, pltpu.roll(sr_ref[...].astype(F32), 1, 0))
        pi_ = jnp.where(row == 0, hi, pltpu.roll(si_ref[...].astype(F32), 1, 0))
        dlr = jnp.sum(gr * pr + gi * pi_, axis=0, keepdims=True)
        dli = jnp.sum(gi * pr - gr * pi_, axis=0, keepdims=True)
        start = jnp.logical_and(b == 0, t == 0)

        @pl.when(start)
        def _():
            dlr_ref[...] = dlr
            dli_ref[...] = dli

        @pl.when(jnp.logical_not(start))
        def _():
            dlr_ref[...] += dlr
            dli_ref[...] += dli

    def blk(c, b, t):
        return b * nt + (nt - 1 - t)

    st_spec = pl.BlockSpec((tb, cb), lambda c, b, t: (blk(c, b, t), c))
    halo_spec = pl.BlockSpec((halo_rows, cb), lambda c, b, t: (jnp.maximum(blk(c, b, t) * hb - 1, 0), c))
    row_spec = pl.BlockSpec((1, cb), lambda c, b, t: (0, c))
    res = pl.pallas_call(
        body, name=name,
        grid=(nc, batch, nt),
        in_specs=[st_spec] * 4 + [halo_spec] * 2 + [row_spec] * 2 + ex_specs,
        out_specs=[st_spec, st_spec, row_spec, row_spec] + ex_specs,
        out_shape=[jax.ShapeDtypeStruct((n, nch), BF16)] * 2 + [jax.ShapeDtypeStruct((1, nch), F32)] * 2 + ex_out,
        scratch_shapes=[pltpu.VMEM((1, cb), F32)] * 2 + [pltpu.VMEM((tb // SUBLANES, SUBLANES, cb), F32)] * 2
        + [pltpu.VMEM((tb, cb), F32)] * 2 + ex_sems,
        compiler_params=_params("arbitrary", "arbitrary", "arbitrary"),
    )(d_re, d_im, s_re, s_im, s_re, s_im, lam_re, lam_im, *ex_ins)
    return res[0], res[1], res[2], res[3], list(res[4:])


def _pool_tiles(seq_len):
    return min(512, seq_len)


def _window_sums(x, n_steps, forward_in_time):
    rows = x.shape[0]
    k = 1
    for _ in range(n_steps):
        x = x + pltpu.roll(x, k if forward_in_time else rows - k, 0)
        k *= 2
    return x


def pool_fwd(u, w_pool, scale, batch, name):
    n, c = u.shape
    seq = n // batch
    tb = _pool_tiles(seq)
    nt = seq // tb
    gc = c // len(POOL_WINDOWS)
    hb = tb // POOL_HALO

    def body(x_ref, halo_ref, w_ref, sc_ref, y_ref, q_ref):
        t = pl.program_id(1)
        halo = jnp.where(t == 0, 0.0, halo_ref[...])
        full = jnp.concatenate([halo, x_ref[...]], axis=0)
        pos = lax.broadcasted_iota(jnp.int32, (tb, gc), 0) + t * tb + 1
        for gi, win in enumerate(POOL_WINDOWS):
            cols = slice(gi * gc, (gi + 1) * gc)
            sums = _window_sums(full[:, cols], gi + 1, True)[POOL_HALO:, :]
            cnt = jnp.minimum(pos, win).astype(F32)
            q = sums / cnt - x_ref[:, cols]
            r = jnp.dot(q.astype(BF16), w_ref[gi].astype(BF16), preferred_element_type=F32)
            q_ref[:, cols] = q.astype(q_ref.dtype)
            y_ref[:, cols] = (r * sc_ref[:, cols]).astype(y_ref.dtype)

    return pl.pallas_call(
        body, name=name,
        grid=(batch, nt),
        in_specs=[pl.BlockSpec((tb, c), lambda b, t: (b * nt + t, 0)),
                  pl.BlockSpec((POOL_HALO, c), lambda b, t: (jnp.maximum((b * nt + t) * hb - 1, 0), 0)),
                  pl.BlockSpec(w_pool.shape, lambda b, t: (0, 0, 0)),
                  pl.BlockSpec((1, c), lambda b, t: (0, 0))],
        out_specs=[pl.BlockSpec((tb, c), lambda b, t: (b * nt + t, 0))] * 2,
        out_shape=[jax.ShapeDtypeStruct((n, c), BF16)] * 2,
        compiler_params=_params("parallel", "arbitrary"),
    )(u, u, w_pool, scale)


def pool_bwd(dy, q, w_pool, scale, batch, name):
    n, c = dy.shape
    seq = n // batch
    tb = _pool_tiles(seq)
    nt = seq // tb
    ng = len(POOL_WINDOWS)
    gc = c // ng
    hb = tb // POOL_HALO
    n_blocks = n // POOL_HALO

    def body(dy_ref, dyh_ref, q_ref, w_ref, sc_ref, du_ref, dw_ref, dsc_ref):
        b, t = pl.program_id(0), pl.program_id(1)
        last = t == nt - 1
        dy_full = jnp.concatenate([dy_ref[...], jnp.where(last, 0.0, dyh_ref[...])], axis=0)
        pos = lax.broadcasted_iota(jnp.int32, (tb + POOL_HALO, gc), 0) + t * tb + 1
        start = jnp.logical_and(b == 0, t == 0)
        for gi, win in enumerate(POOL_WINDOWS):
            cols = slice(gi * gc, (gi + 1) * gc)
            w = w_ref[gi].astype(BF16)
            dr = dy_full[:, cols] * sc_ref[:, cols]
            dq = lax.dot_general(dr.astype(BF16), w, (((1,), (1,)), ((), ())), preferred_element_type=F32)
            cnt = jnp.minimum(pos, win).astype(F32)
            back = _window_sums(dq / cnt, gi + 1, False)
            du_ref[:, cols] = back[:tb, :] - dq[:tb, :]
            qb = q_ref[:, cols]
            r = jnp.dot(qb, w, preferred_element_type=F32)
            dw = lax.dot_general(qb, dr[:tb, :].astype(BF16), (((0,), (0,)), ((), ())), preferred_element_type=F32)
            dsc = jnp.sum(dy_ref[:, cols] * r, axis=0, keepdims=True)

            @pl.when(start)
            def _(gi=gi, cols=cols, dw=dw, dsc=dsc):
                dw_ref[gi] = dw
                dsc_ref[:, cols] = dsc

            @pl.when(jnp.logical_not(start))
            def _(gi=gi, cols=cols, dw=dw, dsc=dsc):
                dw_ref[gi] += dw
                dsc_ref[:, cols] += dsc

    blk = pl.BlockSpec((tb, c), lambda b, t: (b * nt + t, 0))
    halo = pl.BlockSpec((POOL_HALO, c), lambda b, t: (jnp.minimum((b * nt + t + 1) * hb, n_blocks - 1), 0))
    return pl.pallas_call(
        body, name=name,
        grid=(batch, nt),
        in_specs=[blk, halo, blk,
                  pl.BlockSpec(w_pool.shape, lambda b, t: (0, 0, 0)),
                  pl.BlockSpec((1, c), lambda b, t: (0, 0))],
        out_specs=[blk, pl.BlockSpec(w_pool.shape, lambda b, t: (0, 0, 0)), pl.BlockSpec((1, c), lambda b, t: (0, 0))],
        out_shape=[jax.ShapeDtypeStruct((n, c), F32), jax.ShapeDtypeStruct(w_pool.shape, F32),
                   jax.ShapeDtypeStruct((1, c), F32)],
        compiler_params=_params("arbitrary", "arbitrary"),
    )(dy, dy, q, w_pool, scale)


def _place():
    return lax.axis_index("x"), lax.axis_index("y"), lax.axis_index("c")


class GatherPlan:
    def __init__(self, arrs):
        self.ins = list(arrs)
        na = len(arrs)
        self.out_shape = [jax.ShapeDtypeStruct((N_DEV,) + a.shape, a.dtype) for a in arrs]
        self.sems = [pltpu.SemaphoreType.DMA((na, 7)), pltpu.SemaphoreType.DMA((na, 7)), pltpu.SemaphoreType.DMA((na,))]
        self.sizes = [math.prod(a.shape) * a.dtype.itemsize for a in arrs]

    def relay_steps(self, n_steps):
        total, done, steps = sum(self.sizes), 0, []
        for a, size in enumerate(self.sizes):
            done += size
            steps.append((a, min(n_steps - 1, (done * (n_steps - 1)) // total)))
        return steps

    def _copy(self, outs, sems, a, k, block, to, src=None):
        dst = outs[a].at[4 * block[0] + 2 * block[1] + block[2]]
        return pltpu.make_async_remote_copy(
            src_ref=dst if src is None else src, dst_ref=dst,
            send_sem=sems[0].at[a, k], recv_sem=sems[1].at[a, k], device_id=to, device_id_type=MESH)

    @staticmethod
    def _chips(x, y):
        return [(1 - x, y), (x, 1 - y), (1 - x, 1 - y)]

    def _local(self, ins, outs, sems, a, me):
        return pltpu.make_async_copy(ins[a], outs[a].at[4 * me[0] + 2 * me[1] + me[2]], sems[2].at[a])

    def start(self, ins, outs, sems):
        x, y, c = _place()
        me = (x, y, c)
        for a in range(len(ins)):
            self._local(ins, outs, sems, a, me).start()
            self._copy(outs, sems, a, 0, me, (x, y, 1 - c), src=ins[a]).start()
            for j, chip in enumerate(self._chips(x, y)):
                self._copy(outs, sems, a, 1 + j, me, (*chip, c), src=ins[a]).start()

    def relay(self, a, ins, outs, sems):
        x, y, c = _place()
        for j, chip in enumerate(self._chips(x, y)):
            self._copy(outs, sems, a, 1 + j, (*chip, c), (x, y, c)).wait_recv()
            self._copy(outs, sems, a, 4 + j, (*chip, c), (x, y, 1 - c)).start()

    def finish(self, ins, outs, sems):
        x, y, c = _place()
        me, sibling = (x, y, c), (x, y, 1 - c)
        for a in range(len(ins)):
            self._copy(outs, sems, a, 0, sibling, me).wait_recv()
            for j, chip in enumerate(self._chips(x, y)):
                self._copy(outs, sems, a, 4 + j, (*chip, 1 - c), me).wait_recv()
        for a in range(len(ins)):
            self._copy(outs, sems, a, 0, me, sibling, src=ins[a]).wait_send()
            for j, chip in enumerate(self._chips(x, y)):
                self._copy(outs, sems, a, 1 + j, me, (*chip, c), src=ins[a]).wait_send()
                self._copy(outs, sems, a, 4 + j, (*chip, c), sibling).wait_send()
            self._local(ins, outs, sems, a, me).wait()


class ChipScatterPlan:
    def __init__(self, arrs):
        self.ins = list(arrs)
        na = len(arrs)
        self.out_shape = [jax.ShapeDtypeStruct(a.shape, a.dtype) for a in arrs]
        self.sems = [pltpu.SemaphoreType.DMA((na, 3)), pltpu.SemaphoreType.DMA((na, 3)), pltpu.SemaphoreType.DMA((na,))]

    def relay_steps(self, n_steps):
        return []

    def _copy(self, ins, outs, sems, a, j, chip, c, mine):
        return pltpu.make_async_remote_copy(
            src_ref=ins[a].at[2 * chip[0] + chip[1]], dst_ref=outs[a].at[mine],
            send_sem=sems[0].at[a, j], recv_sem=sems[1].at[a, j], device_id=(*chip, c), device_id_type=MESH)

    def _all(self, ins, outs, sems):
        x, y, c = _place()
        mine = 2 * x + y
        remote = [self._copy(ins, outs, sems, a, j, chip, c, mine)
                  for a in range(len(ins)) for j, chip in enumerate(GatherPlan._chips(x, y))]
        local = [pltpu.make_async_copy(ins[a].at[mine], outs[a].at[mine], sems[2].at[a]) for a in range(len(ins))]
        return remote, local

    def start(self, ins, outs, sems):
        remote, local = self._all(ins, outs, sems)
        for cp in local + remote:
            cp.start()

    def finish(self, ins, outs, sems):
        remote, local = self._all(ins, outs, sems)
        for cp in remote:
            cp.wait_recv()
        for cp in remote:
            cp.wait_send()
        for cp in local:
            cp.wait()


def run_exchange(plan, name):
    n_in, n_out = len(plan.ins), len(plan.out_shape)

    def body(*refs):
        parts = (refs[:n_in], refs[n_in:n_in + n_out], refs[n_in + n_out:])
        plan.start(*parts)
        for part, _ in plan.relay_steps(1):
            plan.relay(part, *parts)
        plan.finish(*parts)

    return pl.pallas_call(
        body, name=name,
        in_specs=[ANY] * n_in, out_specs=[ANY] * n_out,
        out_shape=plan.out_shape, scratch_shapes=plan.sems,
    )(*plan.ins)


def swap_with_sibling(arrs, name):
    na = len(arrs)

    def body(*refs):
        ins, outs = refs[:na], refs[na:2 * na]
        send_sems, recv_sems = refs[2 * na:]
        x, y, c = _place()
        copies = [pltpu.make_async_remote_copy(
            src_ref=ins[a], dst_ref=outs[a], send_sem=send_sems.at[a], recv_sem=recv_sems.at[a],
            device_id=(x, y, 1 - c), device_id_type=MESH) for a in range(na)]
        for cp in copies:
            cp.start()
        for cp in copies:
            cp.wait()

    return pl.pallas_call(
        body, name=name,
        in_specs=[ANY] * na, out_specs=[ANY] * na,
        out_shape=[jax.ShapeDtypeStruct(a.shape, a.dtype) for a in arrs],
        scratch_shapes=[pltpu.SemaphoreType.DMA((na,)), pltpu.SemaphoreType.DMA((na,))],
    )(*arrs)


def adamw(w, gparts, m, v, name, tr=256):
    rows, cols = w.shape
    parts = gparts.shape[0]
    tr = _row_tile(rows, tr)
    c1 =1.0 - ADAM_B1 ** ADAM_STEP
    c2 = 1.0 - ADAM_B2 ** ADAM_STEP

    def body(w_ref, g_ref, m_ref, v_ref, go_ref, d_ref, mo_ref, vo_ref):
        g = g_ref[0].astype(F32)
        for p_ in range(1, parts):
            g = g + g_ref[p_].astype(F32)
        m_new = ADAM_B1 * m_ref[...] + (1.0 - ADAM_B1) * g
        v_new = ADAM_B2 * v_ref[...] + (1.0 - ADAM_B2) * (g * g)
        m_hat = m_new / c1
        v_hat = v_new / c2
        go_ref[...] = g
        d_ref[...] = -ADAM_LR * (m_hat / (jnp.sqrt(v_hat) + ADAM_EPS) + ADAM_WD * w_ref[...])
        mo_ref[...] = m_new
        vo_ref[...] = v_new

    blk = pl.BlockSpec((tr, cols), lambda i: (i, 0))
    return pl.pallas_call(
        body, name=name,
        grid=(rows // tr,),
        in_specs=[blk, pl.BlockSpec((parts, tr, cols), lambda i: (0, i, 0)), blk, blk],
        out_specs=[blk] * 4,
        out_shape=[jax.ShapeDtypeStruct((rows, cols), F32)] * 4,
        compiler_params=_params("parallel"),
    )(w, gparts, m, v)


def add2(a, b, name, out_dtype):
    return rowwise(lambda a, b: ([a.astype(F32) + b.astype(F32)], []), [a, b], [(a.shape[1], out_dtype)],
                   name=name, tr=256)[0]


def _block_diag(x):
    g, a, b = x.shape
    eye = jnp.eye(g, dtype=x.dtype)
    return (x[:, :, None, :] * eye[:, None, :, None]).reshape(g * a, g * b)


def _diag_blocks(x, a, b):
    per = x.shape[1] // b
    x5 = x.reshape(SSM_SUPER, per, a, per, b)
    eye = jnp.eye(per, dtype=x.dtype)
    return jnp.sum(x5 * eye[None, :, None, :, None], axis=3).reshape(SSM_SUPER * per, a, b)


def _swiglu_epi(g, u):
    s = _sigmoid(g)
    silu = g * s
    return u * (s * (1.0 + g * (1.0 - s))), silu, silu * u


def _residual_epi(scale, with_norm):
    if with_norm:
        def epi(acc, res, gain):
            out = res + scale * acc
            return out, _rms_tile(out, gain)
    else:
        def epi(acc, res):
            return (res + scale * acc,)
    return epi


def _next_norm(next_gain):
    if next_gain is None:
        return [], (F32,)
    return [next_gain], (F32, BF16)


FFN_WIDE = 2816


def ffn_fwd(h, n, wi, wo, next_gain, tag):
    dact_g, dact_u, act = matmul(n, [(wi, 0), (wi, 1)], mode="nn", name=f"{tag}_in", separate=True, epi=_swiglu_epi,
                                 out_dtypes=(BF16, BF16, BF16), tm=512, tn=FFN_WIDE)
    more, dtypes = _next_norm(next_gain)
    res = matmul(act, wo, mode="nn", name=f"{tag}_out", epi=_residual_epi(0.5, bool(more)), extras=[h] + more,
                 out_dtypes=dtypes, tm=512, tk=FFN_WIDE)
    return res[0], (res[1] if more else None), (h, n, dact_g, dact_u, act)


def ffn_bwd(dh, saved, gain, wi, wo, tag):
    h, n, dact_g, dact_u, act = saved
    dg, du = matmul(dh, wo, mode="nt", name=f"{tag}_out_dx", extras=[dact_g, dact_u], out_dtypes=(BF16, BF16),
                    epi=lambda acc, fg, fu: (0.5 * acc * fg, 0.5 * acc * fu), tm=512, tn=FFN_WIDE)
    d_wo, = matmul(act, dh, mode="tn", name=f"{tag}_out_dw", scale=0.5, tm=FFN_WIDE, tk=512)
    d_wi, = matmul(n, [dg, du], mode="tn", name=f"{tag}_in_dw", separate=True, stack_out=True,
                   tm=512, tn=FFN_WIDE, tk=512)
    dh_new, d_gain = matmul([dg, du], [(wi, 0), (wi, 1)], mode="nt", name=f"{tag}_in_dx", epi=_rms_bwd_tile,
                            extras=[h, gain, dh], out_dtypes=(F32,), col_sums=1, tm=256, tk=FFN_WIDE)
    return dh_new, d_gain, d_wi, d_wo


def mix_fwd(h, n, lw, batch, next_gain, tag, exchange=None):
    sw = h.shape[1] // 2
    us, up = matmul(n, [lw["w_in"][:, :sw], lw["w_in"][:, sw:]], mode="nn", name=f"{tag}_in", separate=True,
                    out_dtypes=(F32, F32))
    lam_r, lam_i, bb_r, bb_i = ssm_prep(lw["lam_re"], lw["lam_im"], lw["log_dt"], lw["b_re"], lw["b_im"], f"{tag}_zoh")
    lam = (lam_r.reshape(1, SSM_CH), lam_i.reshape(1, SSM_CH))
    b_mats = [_block_diag(bb.reshape(SSM_GROUPS, SSM_STATE, SSM_GROUP_CH).transpose(0, 2, 1)).astype(BF16)
              for bb in (bb_r, bb_i)]
    c_mats = [_block_diag(cc.transpose(0, 2, 1)).astype(BF16) for cc in (lw["c_re"], -lw["c_im"])]
    bu_re, bu_im = matmul(us, b_mats, mode="nn", name=f"{tag}_bu", separate=True, diag=SSM_SUPER)
    s_re, s_im, exchanged = ssm_scan(bu_re, bu_im, *lam, batch, f"{tag}_scan", exchange)
    y0, y1 = matmul([s_re, s_im], c_mats, mode="nn", name=f"{tag}_c", diag=SSM_SUPER,
                    epi=lambda acc, u, d: (acc + d * u, _gelu(acc + d * u)), extras=[us, lw["ssm_d"]],
                    out_dtypes=(F32, BF16))
    y2, gl = matmul(y1, lw["w_glu"], mode="nn", name=f"{tag}_glu",
                    epi=lambda acc, y0: (_gelu(y0) * _sigmoid(acc), acc), extras=[y0], out_dtypes=(BF16, F32))
    yp, q = pool_fwd(up, lw["pool_w"], lw["pool_scale"], batch, f"{tag}_pool")
    more, dtypes = _next_norm(next_gain)
    res = matmul([y2, yp], [lw["w_out"][:sw], lw["w_out"][sw:]], mode="nn", name=f"{tag}_out",
                 epi=_residual_epi(1.0, bool(more)), extras=[h] + more, out_dtypes=dtypes)
    saved = (h, n, us, lam, b_mats, c_mats, s_re, s_im, y0, y1, gl, y2, yp, q)
    return res[0], (res[1] if more else None), saved, exchanged


def mix_bwd(dh, saved, lw, batch, tag, exchange=None):
    h, n, us, lam, b_mats, c_mats, s_re, s_im, y0, y1, gl, y2, yp, q = saved
    sw = h.shape[1] // 2
    w_out_s, w_out_p = lw["w_out"][:sw], lw["w_out"][sw:]
    d_wo_s, d_wo_p = matmul([y2, yp], dh, mode="tn", name=f"{tag}_out_dw", separate=True)
    def out_dx_epi(dy2, dyp, y0, gl):
        sg = _sigmoid(gl)
        return dy2, dyp, dy2 * _gelu(y0) * sg * (1.0 - sg)

    dy2, dyp, tg = matmul(dh, [w_out_s, w_out_p], mode="nt", name=f"{tag}_out_dx", separate=True, epi=out_dx_epi,
                          extras=[y0, gl], out_dtypes=(F32, F32, BF16))
    dup, d_pool_w, d_pool_scale = pool_bwd(dyp, q, lw["pool_w"], lw["pool_scale"], batch, f"{tag}_pool_bwd")
    def dy0_epi(acc, dy2, gl, y0, u):
        dy0 = (acc + dy2 * _sigmoid(gl)) * _gelu_grad(y0)
        return dy0, jnp.sum(dy0 * u, axis=0, keepdims=True)

    dy0, d_d = matmul(tg, lw["w_glu"], mode="nt", name=f"{tag}_glu_dx", epi=dy0_epi, extras=[dy2, gl, y0, us],
                      out_dtypes=(F32,), col_sums=1)
    d_w_glu, = matmul(y1, tg, mode="tn", name=f"{tag}_glu_dw")
    gd_re, gd_im = matmul(dy0, c_mats, mode="nt", name=f"{tag}_c_dx", separate=True, diag=SSM_SUPER)
    d_c_top, d_c_bot = matmul([s_re, s_im], dy0, mode="tn", name=f"{tag}_c_dw", separate=True, diag=SSM_SUPER)
    g_re, g_im, d_lam_r, d_lam_i, exchanged = ssm_scan_bwd(gd_re, gd_im, s_re, s_im, *lam, batch, f"{tag}_scan_bwd",
                                                           exchange)
    dus, = matmul([g_re, g_im], b_mats, mode="nt", name=f"{tag}_bu_dx", diag=SSM_SUPER,
                  epi=lambda acc, dy0, d: (acc + d * dy0,), extras=[dy0, lw["ssm_d"]])
    d_b_re, d_b_im = matmul(us, [g_re, g_im], mode="tn", name=f"{tag}_bu_dw", separate=True, diag=SSM_SUPER)
    d_bb_r = _diag_blocks(d_b_re, SSM_GROUP_CH, SSM_STATE).transpose(0, 2, 1).reshape(SSM_CH, SSM_GROUP_CH)
    d_bb_i = _diag_blocks(d_b_im, SSM_GROUP_CH, SSM_STATE).transpose(0, 2, 1).reshape(SSM_CH, SSM_GROUP_CH)
    d_lr, d_li, d_ldt, d_br, d_bi = ssm_prep_bwd(
        lw["lam_re"], lw["lam_im"], lw["log_dt"], lw["b_re"], lw["b_im"],
        d_lam_r.reshape(SSM_CH, 1), d_lam_i.reshape(SSM_CH, 1), d_bb_r, d_bb_i, f"{tag}_zoh_bwd")
    d_c_re = _diag_blocks(d_c_top, SSM_STATE, SSM_GROUP_CH).transpose(0, 2, 1)
    d_c_im = -_diag_blocks(d_c_bot, SSM_STATE, SSM_GROUP_CH).transpose(0, 2, 1)
    d_w_in_s, d_w_in_p = matmul(n, [dus, dup], mode="tn", name=f"{tag}_in_dw", separate=True)
    dh_new, d_gain = matmul([dus, dup], [lw["w_in"][:, :sw], lw["w_in"][:, sw:]], mode="nt", name=f"{tag}_in_dx",
                            epi=_rms_bwd_tile, extras=[h, lw["mix_norm"], dh], out_dtypes=(F32,), col_sums=1, tm=512)
    grads = dict(mix_norm=d_gain, w_in=jnp.concatenate([d_w_in_s, d_w_in_p], axis=1),
                 ssm_lambda_re=d_lr, ssm_lambda_im=d_li, ssm_log_dt=d_ldt, ssm_b_re=d_br, ssm_b_im=d_bi,
                 ssm_c_re=d_c_re, ssm_c_im=d_c_im, ssm_d=d_d, ssm_w_glu=d_w_glu, pool_w=d_pool_w,
                 pool_scale=d_pool_scale, w_out=jnp.concatenate([d_wo_s, d_wo_p], axis=0))
    return dh_new, grads, exchanged


def ple_fwd(h, n, p, w_gate, w_proj, next_gain, tag):
    e, = matmul(p, w_proj, mode="nn", name=f"{tag}_proj")
    if next_gain is None:
        def epi(acc, e, res):
            return res + _sigmoid(acc) * e, acc
        more, dtypes = [], (F32, F32)
    else:
        def epi(acc, e, res, gain):
            out = res + _sigmoid(acc) * e
            return out, acc, _rms_tile(out, gain)
        more, dtypes = [next_gain], (F32, F32, BF16)
    res = matmul(n, w_gate, mode="nn", name=f"{tag}_gate", epi=epi, extras=[e, h] + more, out_dtypes=dtypes, tm=512)
    return res[0], (res[2] if more else None), (h, n, e, res[1])


def ple_bwd(dh, saved, p, gain, w_gate, tag):
    h, n, e, pre = saved
    d = h.shape[1]

    def fn(dh, e, pre):
        s = _sigmoid(pre)
        return [dh * e * s * (1.0 - s), dh * s], []

    dpre, de = rowwise(fn, [dh, e, pre], [(d, BF16), (d, BF16)], name=f"{tag}_gate_bwd")
    d_w_gate, = matmul(n, dpre, mode="tn", name=f"{tag}_gate_dw")
    d_w_proj, = matmul(p, de, mode="tn", name=f"{tag}_proj_dw")
    dh_new, d_gain = matmul(dpre, w_gate, mode="nt", name=f"{tag}_gate_dx", epi=_rms_bwd_tile,
                            extras=[h, gain, dh], out_dtypes=(F32,), col_sums=1, tm=512)
    return dh_new, d_gain, d_w_gate, d_w_proj


def loss_head(h, gain, target, name):
    d = h.shape[1]

    def fn(h, g, t):
        r = lax.rsqrt(jnp.mean(h * h, axis=-1, keepdims=True) + EPS)
        diff = h * r * g - t
        sq = jnp.sum(jnp.sum(diff * diff, axis=1, keepdims=True), axis=0, keepdims=True)
        dy = diff * (1.0 / d)
        w = dy * g
        dh = r * w - h * (r * r * r) * jnp.mean(h * w, axis=-1, keepdims=True)
        return [dh], [sq, jnp.sum(dy * (h * r), axis=0, keepdims=True)]

    dh, sq, d_gain = rowwise(fn, [h, gain, target], [(d, F32)], [(1, 1), (1, d)], name=name)
    return 0.5 / d * sq[0, 0], dh, d_gain


SHARDED = {
    "ffn1_wi": 1, "ffn1_wo": 0, "w_in": 0, "ssm_w_glu": 0, "w_out": 0, "ffn2_wi": 1, "ffn2_wo": 0,
    "ple_w_gate": 0, "ple_w_proj": 1,
}
WEIGHTS = ["ffn1_norm", "ffn1_wi", "ffn1_wo", "mix_norm", "w_in", "ssm_lambda_re", "ssm_lambda_im", "ssm_log_dt",
           "ssm_b_re", "ssm_b_im", "ssm_c_re", "ssm_c_im", "ssm_d", "ssm_w_glu", "pool_w", "pool_scale", "w_out",
           "ffn2_norm", "ffn2_wi", "ffn2_wo", "ple_norm", "ple_w_gate", "ple_w_proj", "final_norm"]
REPLICATED = [n for n in WEIGHTS if n not in SHARDED]


HALVED = ("ffn1_wi", "ffn2_wi")


def _unshard(gathered, axis, halved):
    if halved:
        _, rows, cols = gathered.shape
        return gathered.reshape(2, 4, rows, cols).transpose(0, 2, 1, 3).reshape(2, rows, 4 * cols)
    g = jnp.moveaxis(gathered, 0, axis)
    shp = g.shape
    return g.reshape(shp[:axis] + (shp[axis] * shp[axis + 1],) + shp[axis + 2:])


def _split_for_scatter(full, axis, c, halved):
    if halved:
        _, rows, cols = full.shape
        g = full.reshape(2, rows, 2, 2, cols // 4)
        pick = lambda cc: lax.dynamic_index_in_dim(g, cc, 3, keepdims=False).transpose(0, 2, 1, 3).reshape(
            4, rows, cols // 4)
        return pick(c), pick(1 - c).astype(BF16)
    shp = full.shape
    g = full.reshape(shp[:axis] + (4, 2, shp[axis] // N_DEV) + shp[axis + 1:])
    keep = lax.dynamic_index_in_dim(g, c, axis + 1, keepdims=False)
    send = lax.dynamic_index_in_dim(g, 1 - c, axis + 1, keepdims=False)
    return jnp.moveaxis(keep, axis, 0), jnp.moveaxis(send, axis, 0).astype(BF16)


def _pack(arrs):
    pieces = []
    for a in arrs:
        flat = a.reshape(-1)
        pad = (-flat.shape[0]) % PACK
        pieces.append(jnp.pad(flat, (0, pad)).reshape(-1, LANES))
    return jnp.concatenate(pieces, axis=0)


def _unpack(packed, shapes):
    out, row = [], 0
    for s in shapes:
        size = math.prod(s)
        rows = (size + PACK - 1) // PACK * SUBLANES
        out.append(packed[row:row + rows].reshape(-1)[:size].reshape(s))
        row += rows
    return out


class NoExchange:
    def __init__(self, full):
        self.full = full

    def layer_full(self, i):
        layer = {k: v[i] for k, v in self.full.items()}
        for k in HALVED:
            rows, cols = layer[k].shape
            layer[k] = layer[k].reshape(rows, 2, cols // 2).transpose(1, 0, 2)
        return layer

    def fwd_exchange(self, i):
        return None

    def fwd_done(self, i, results):
        pass

    def bwd_exchange(self, i):
        return None

    def bwd_done(self, i, results):
        pass

    def layer_grads(self, i, grads):
        pass


def local_step(x, p, target, rep, hooks):
    batch, seq, d = x.shape
    n_tok = batch * seq
    h = x.reshape(n_tok, d)
    saved = []
    n = rms_fwd(h, rep["ffn1_norm"][0].reshape(1, d), "first_norm")
    for i in range(DEPTH):
        lw = _layer_weights(rep, hooks.layer_full(i), i)
        next_gain = rep["ffn1_norm"][i + 1].reshape(1, d) if i + 1 < DEPTH else None
        h, n, s1 = ffn_fwd(h, n, lw["ffn1_wi"], lw["ffn1_wo"], lw["mix_norm"], "ffn1")
        h, n, s2, exchanged = mix_fwd(h, n, lw, batch, lw["ffn2_norm"], "mix", hooks.fwd_exchange(i))
        hooks.fwd_done(i, exchanged)
        h, n, s3 = ffn_fwd(h, n, lw["ffn2_wi"], lw["ffn2_wo"], lw["ple_norm"], "ffn2")
        p_i = p[i].reshape(n_tok, -1)
        h, n, s4 = ple_fwd(h, n, p_i, lw["ple_w_gate"], lw["ple_w_proj"], next_gain, "ple")
        saved.append((s1, s2, s3, s4, p_i))
    loss, dh, d_final = loss_head(h, rep["final_norm"].reshape(1, d), target.reshape(n_tok, d), "loss_head")
    per_layer = [None] * DEPTH
    for i in reversed(range(DEPTH)):
        lw = _layer_weights(rep, hooks.layer_full(i), i)
        s1, s2, s3, s4, p_i = saved[i]
        g = {}
        dh, g["ple_norm"], g["ple_w_gate"], g["ple_w_proj"] = ple_bwd(dh, s4, p_i, lw["ple_norm"], lw["ple_w_gate"], "ple")
        dh, g["ffn2_norm"], g["ffn2_wi"], g["ffn2_wo"] = ffn_bwd(
            dh, s3, lw["ffn2_norm"], lw["ffn2_wi"], lw["ffn2_wo"], "ffn2")
        dh, gm, exchanged = mix_bwd(dh, s2, lw, batch, "mix", hooks.bwd_exchange(i))
        hooks.bwd_done(i, exchanged)
        g.update(gm)
        dh, g["ffn1_norm"], g["ffn1_wi"], g["ffn1_wo"] = ffn_bwd(
            dh, s1, lw["ffn1_norm"], lw["ffn1_wi"], lw["ffn1_wo"], "ffn1")
        hooks.layer_grads(i, g)
        per_layer[i] = g
    return loss, dh.reshape(batch, seq, d), per_layer, d_final


def _layer_weights(w, big, i):
    d = big["w_in"].shape[1]
    sw = d // 2
    lw = {k: big[k] for k in ("ffn1_wi", "ffn1_wo", "w_in", "w_out", "ffn2_wi", "ffn2_wo", "ple_w_gate", "ple_w_proj")}
    lw["w_glu"] = big["ssm_w_glu"]
    lw["pool_w"] = w["pool_w"][i]
    for k in ("ffn1_norm", "mix_norm", "ffn2_norm", "ple_norm"):
        lw[k] = w[k][i].reshape(1, d)
    lw["ssm_d"] = w["ssm_d"][i].reshape(1, sw)
    lw["pool_scale"] = w["pool_scale"][i].reshape(1, sw)
    lw["lam_re"] = w["ssm_lambda_re"][i].reshape(SSM_CH, 1)
    lw["lam_im"] = w["ssm_lambda_im"][i].reshape(SSM_CH, 1)
    lw["log_dt"] = jnp.repeat(w["ssm_log_dt"][i], SSM_STATE).reshape(SSM_CH, 1)
    lw["b_re"] = w["ssm_b_re"][i].reshape(SSM_CH, SSM_GROUP_CH)
    lw["b_im"] = w["ssm_b_im"][i].reshape(SSM_CH, SSM_GROUP_CH)
    lw["c_re"] = w["ssm_c_re"][i]
    lw["c_im"] = w["ssm_c_im"][i]
    return lw


class MeshExchange:
    def __init__(self, shards):
        self.shards = shards
        self.c = lax.axis_index("c")
        self.gathered = {0: run_exchange(GatherPlan(shards[0]), "gather_first_layer")}
        self.chip_sums = {}
        self.from_chips = {}

    def layer_full(self, i):
        return {k: _unshard(g, SHARDED[k], k in HALVED) for k, g in zip(SHARDED, self.gathered[i])}

    def fwd_exchange(self, i):
        return GatherPlan(self.shards[i + 1]) if i + 1 < DEPTH else None

    def fwd_done(self, i, results):
        if results:
            self.gathered[i + 1] = results

    def layer_grads(self, i, grads):
        keeps, sends = zip(*[_split_for_scatter(grads[k], SHARDED[k], self.c, k in HALVED) for k in SHARDED])
        from_sibling = swap_with_sibling(list(sends), "reduce_core_pair")
        sums = []
        for k, keep, got in zip(SHARDED, keeps, from_sibling):
            cols = keep.shape[-1]
            sums.append(add2(keep.reshape(-1, cols), got.reshape(-1, cols), f"sum_core_pair_{k}", BF16).reshape(keep.shape))
        if i == 0:
            self.from_chips[0] = run_exchange(ChipScatterPlan(sums), "reduce_chips_last_layer")
        else:
            self.chip_sums[i] = sums

    def bwd_exchange(self, i):
        return ChipScatterPlan(self.chip_sums[i + 1]) if i + 1 in self.chip_sums else None

    def bwd_done(self, i, results):
        if results:
            self.from_chips[i + 1] = results


def kernel(x, p, ffn1_norm, ffn1_wi, ffn1_wo, mix_norm, w_in, ssm_lambda_re, ssm_lambda_im, ssm_log_dt, ssm_b_re, ssm_b_im, ssm_c_re, ssm_c_im, ssm_d, ssm_w_glu, pool_w, pool_scale, w_out, ffn2_norm, ffn2_wi, ffn2_wo, ple_norm, ple_w_gate, ple_w_proj, final_norm, loss_target, m_ffn1_norm, m_ffn1_wi, m_ffn1_wo, m_mix_norm, m_w_in, m_ssm_lambda_re, m_ssm_lambda_im, m_ssm_log_dt, m_ssm_b_re, m_ssm_b_im, m_ssm_c_re, m_ssm_c_im, m_ssm_d, m_ssm_w_glu, m_pool_w, m_pool_scale, m_w_out, m_ffn2_norm, m_ffn2_wi, m_ffn2_wo, m_ple_norm, m_ple_w_gate, m_ple_w_proj, m_final_norm, v_ffn1_norm, v_ffn1_wi, v_ffn1_wo, v_mix_norm, v_w_in, v_ssm_lambda_re, v_ssm_lambda_im, v_ssm_log_dt, v_ssm_b_re, v_ssm_b_im, v_ssm_c_re, v_ssm_c_im, v_ssm_d, v_ssm_w_glu, v_pool_w, v_pool_scale, v_w_out, v_ffn2_norm, v_ffn2_wi, v_ffn2_wo, v_ple_norm, v_ple_w_gate, v_ple_w_proj, v_final_norm):
    args = dict(locals())
    wts = {k: args[k] for k in WEIGHTS}
    rep = {k: wts[k] for k in REPLICATED}

    shards = [[wts[k][i].astype(BF16) for k in SHARDED] for i in range(DEPTH)]
    exchange = MeshExchange(shards)
    loss_local, grad_x, per_layer, d_final = local_step(x, p, loss_target, rep, exchange)
    loss = lax.psum(loss_local, ("x", "y", "c"))

    outs = {}
    for j, k in enumerate(SHARDED):
        shp = wts[k].shape
        cols = shp[-1]
        parts = jnp.stack([exchange.from_chips[i][j] for i in range(DEPTH)], axis=1)
        res = adamw(wts[k].reshape(-1, cols), parts.reshape(4, -1, cols), args["m_" + k].reshape(-1, cols),
                    args["v_" + k].reshape(-1, cols), f"adamw_{k}")
        outs[k] = [r.reshape(shp) for r in res]

    rep_shapes = [wts[k].shape for k in REPLICATED]
    small = {k: jnp.stack([g[k] for g in per_layer], axis=0) for k in REPLICATED if k != "final_norm"}
    small["final_norm"] = d_final
    packed_g = _pack([small[k].reshape(wts[k].shape) for k in REPLICATED])
    all_g, = run_exchange(GatherPlan([packed_g]), "gather_small_grads")
    res = adamw(_pack([wts[k] for k in REPLICATED]), all_g, _pack([args["m_" + k] for k in REPLICATED]),
                _pack([args["v_" + k] for k in REPLICATED]), "adamw_small")
    unpacked = [_unpack(r, rep_shapes) for r in res]
    for j, k in enumerate(REPLICATED):
        outs[k] = [unpacked[q][j] for q in range(4)]

    result = [loss, grad_x]
    for q in range(4):
        result += [outs[k][q] for k in WEIGHTS]
    return tuple(result)
```

```python
import math

import jax
import jax.numpy as jnp
from jax import lax
from jax.experimental import pallas as pl
from jax.experimental.pallas import tpu as pltpu

F32 = jnp.float32
BF16 = jnp.bfloat16
MESH = pl.DeviceIdType.MESH
ANY = pl.BlockSpec(memory_space=pl.ANY)

N_DEV = 8
DEPTH = 4
EPS = 1e-6
SSM_GROUPS = 32
SSM_GROUP_CH = 16
SSM_STATE = 64
SSM_CH = SSM_GROUPS * SSM_STATE
SSM_SUPER = 2
POOL_WINDOWS = (2, 4, 8, 16)
POOL_HALO = 16
ADAM_LR, ADAM_B1, ADAM_B2, ADAM_EPS, ADAM_WD, ADAM_STEP = 0.001, 0.9, 0.999, 1e-08, 0.01, 10

V7X_VMEM_BYTES = 64 * 1024 * 1024
VMEM_LIMIT_BYTES = V7X_VMEM_BYTES - 12 * 1024 * 1024
LANES = 128
SUBLANES = 8
PACK = SUBLANES * LANES


def _params(*sem):
    return pltpu.CompilerParams(dimension_semantics=sem or None, vmem_limit_bytes=VMEM_LIMIT_BYTES)


def _tile(n, pref):
    if n <= pref:
        return n
    t = pref - pref % LANES
    while t >= LANES:
        if n % t == 0:
            return t
        t -= LANES
    raise ValueError(f"no lane-aligned tile for {n}")


def _row_tile(rows, pref):
    if rows <= pref:
        return rows
    t = pref - pref % SUBLANES
    while t >= SUBLANES:
        if rows % t == 0:
            return t
        t -= SUBLANES
    raise ValueError(f"no sublane-aligned tile for {rows}")


_DIMS = {"nn": ((1,), (0,)), "nt": ((1,), (1,)), "tn": ((0,), (0,))}


def matmul(a, b, *, mode, name, out_dtypes=None, epi=None, extras=(), separate=False, diag=1, col_sums=0,
           stack_out=False, scale=1.0, tm=1024, tn=1024, tk=1024):
    a_list = list(a) if isinstance(a, list) else [a]
    b_list = list(b) if isinstance(b, list) else [b]
    a_planes = [x[1] if isinstance(x, tuple) else None for x in a_list]
    b_planes = [x[1] if isinstance(x, tuple) else None for x in b_list]
    a_list = [x[0] if isinstance(x, tuple) else x for x in a_list]
    b_list = [x[0] if isinstance(x, tuple) else x for x in b_list]
    a_shape, b_shape = a_list[0].shape[-2:], b_list[0].shape[-2:]
    n_terms = max(len(a_list), len(b_list))
    a_idx = [0] * n_terms if len(a_list) == 1 else list(range(n_terms))
    b_idx = [0] * n_terms if len(b_list) == 1 else list(range(n_terms))
    n_acc = n_terms if separate else 1
    assert not (stack_out or scale != 1.0) or epi is None
    if out_dtypes is None:
        out_dtypes = (F32,) * (1 if (epi is not None or stack_out) else n_acc)
    in_place = epi is None
    if mode == "tn":
        K, M = a_shape
        K2, N = b_shape
    elif mode == "nt":
        M, K = a_shape
        N, K2 = b_shape
    else:
        M, K = a_shape
        K2, N = b_shape
    assert K == K2, (name, a_shape, b_shape)
    if mode == "tn":
        tm, tn, tk = _tile(M // diag, tm), _tile(N // diag, tn), _tile(K, tk)
        nk = K // tk
        N = N // diag
        row_tiles, col_tiles = (M // diag) // tm, N // tn
        a_blk, a_map = (tk, tm), lambda i, j, k: (k, i)
        b_blk, b_map = (tk, tn), lambda i, j, k: (k, (i // row_tiles) * col_tiles + j)
    else:
        tm, tn, tk = _tile(M, tm), _tile(N // diag, tn), _tile(K // diag, tk)
        nk = (K // diag) // tk
        col_tiles = (N // diag) // tn
        a_blk, a_map = (tm, tk), lambda i, j, k: (i, (j // col_tiles) * nk + k)
        if mode == "nt":
            b_blk, b_map = (tn, tk), lambda i, j, k: (j, (j // col_tiles) * nk + k)
        else:
            b_blk, b_map = (tk, tn), lambda i, j, k: ((j // col_tiles) * nk + k, j)

    def plane_spec(blk, index_map, plane):
        if plane is None:
            return pl.BlockSpec(blk, index_map)
        return pl.BlockSpec((None,) + blk, lambda i, j, k: (plane,) + index_map(i, j, k))

    a_specs = [plane_spec(a_blk, a_map, p_) for p_ in a_planes]
    b_specs = [plane_spec(b_blk, b_map, p_) for p_ in b_planes]
    assert not col_sums or N == tn, (name, N, tn)
    ex_specs = []
    for e in extras:
        if e.shape == (M, N):
            ex_specs.append(pl.BlockSpec((tm, tn), lambda i, j, k: (i, j)))
        elif e.shape == (1, N):
            ex_specs.append(pl.BlockSpec((1, tn), lambda i, j, k: (0, j)))
        elif e.shape == (M, 1):
            ex_specs.append(pl.BlockSpec((tm, 1), lambda i, j, k: (i, 0)))
        else:
            raise ValueError((name, e.shape, (M, N)))
    na, nb, ne, no = len(a_list), len(b_list), len(extras), len(out_dtypes)
    dims = (_DIMS[mode], ((), ()))

    def body(*refs):
        a_refs = refs[:na]
        b_refs = refs[na:na + nb]
        ex_refs = refs[na + nb:na + nb + ne]
        out_refs = refs[na + nb + ne:na + nb + ne + no]
        sum_refs = refs[na + nb + ne + no:na + nb + ne + no + col_sums]
        acc_refs = refs[na + nb + ne + no + col_sums:]
        a_vals = [r[...].astype(BF16) for r in a_refs]
        b_vals = [r[...].astype(BF16) for r in b_refs]
        prods = [lax.dot_general(a_vals[a_idx[t]], b_vals[b_idx[t]], dims, preferred_element_type=F32)
                 for t in range(n_terms)]
        if not separate:
            total = prods[0]
            for p_ in prods[1:]:
                total = total + p_
            prods = [total]

        def finish(accs):
            res = epi(*accs, *[e[...] for e in ex_refs]) if epi is not None else tuple(accs)
            for r, v in zip(out_refs, res[:no]):
                r[...] = v.astype(r.dtype)
            first_rows = pl.program_id(0) == 0
            for r, v in zip(sum_refs, res[no:]):
                @pl.when(first_rows)
                def _(r=r, v=v):
                    r[...] = v

                @pl.when(jnp.logical_not(first_rows))
                def _(r=r, v=v):
                    r[...] += v

        if in_place:
            dst = [(out_refs[0], t) for t in range(n_acc)] if stack_out else [(r, None) for r in out_refs]

            def read(r, t):
                return r[...] if t is None else r[t]

            def write(r, t, v):
                if t is None:
                    r[...] = v
                else:
                    r[t] = v

            if nk == 1:
                for (r, t), v in zip(dst, prods):
                    write(r, t, v * scale if scale != 1.0 else v)
            else:
                k = pl.program_id(2)

                @pl.when(k == 0)
                def _():
                    for (r, t), v in zip(dst, prods):
                        write(r, t, v)

                @pl.when(jnp.logical_and(k > 0, k < nk - 1))
                def _():
                    for (r, t), v in zip(dst, prods):
                        write(r, t, read(r, t) + v)

                @pl.when(k == nk - 1)
                def _():
                    for (r, t), v in zip(dst, prods):
                        total = read(r, t) + v
                        write(r, t, total * scale if scale != 1.0 else total)
        elif nk == 1:
            finish(prods)
        else:
            k = pl.program_id(2)

            @pl.when(k == 0)
            def _():
                for r, v in zip(acc_refs, prods):
                    r[...] = v

            @pl.when(jnp.logical_and(k > 0, k < nk - 1))
            def _():
                for r, v in zip(acc_refs, prods):
                    r[...] += v

            @pl.when(k == nk - 1)
            def _():
                finish([r[...] + v for r, v in zip(acc_refs, prods)])

    if stack_out:
        out_specs = [pl.BlockSpec((n_acc, tm, tn), lambda i, j, k: (0, i, j))]
        out_shape = [jax.ShapeDtypeStruct((n_acc, M, N), F32)]
    else:
        out_specs = [pl.BlockSpec((tm, tn), lambda i, j, k: (i, j))] * no
        out_shape = [jax.ShapeDtypeStruct((M, N), dt) for dt in out_dtypes]
    outs = pl.pallas_call(
        body,
        name=name,
        grid=(M // tm, N // tn, nk),
        in_specs=a_specs + b_specs + ex_specs,
        out_specs=out_specs + [pl.BlockSpec((1, tn), lambda i, j, k: (0, j))] * col_sums,
        out_shape=out_shape + [jax.ShapeDtypeStruct((1, N), F32)] * col_sums,
        scratch_shapes=[pltpu.VMEM((tm, tn), F32)] * (n_acc if (nk > 1 and not in_place) else 0),
        compiler_params=_params(*(("arbitrary",) * 3 if col_sums else ("parallel", "parallel", "arbitrary"))),
    )(*a_list, *b_list, *extras)
    return outs


def rowwise(fn, ins, outs, accs=(), *, name, tr=512):
    R = max(x.shape[0] for x in ins)
    tr = _row_tile(R, tr)
    in_specs = []
    for x in ins:
        if x.shape[0] == R and x.ndim == 2:
            in_specs.append(pl.BlockSpec((tr, x.shape[1]), lambda i: (i, 0)))
        else:
            in_specs.append(pl.BlockSpec(x.shape, lambda i, _n=x.ndim: (0,) * _n))
    ni, no = len(ins), len(outs)

    def body(*refs):
        i = pl.program_id(0)
        row_vals, acc_vals = fn(*[r[...] for r in refs[:ni]])
        for r, v in zip(refs[ni:ni + no], row_vals):
            r[...] = v.astype(r.dtype)
        for r, v in zip(refs[ni + no:], acc_vals):
            @pl.when(i == 0)
            def _(r=r, v=v):
                r[...] = v

            @pl.when(i > 0)
            def _(r=r, v=v):
                r[...] += v

    return pl.pallas_call(
        body,
        name=name,
        grid=(R // tr,),
        in_specs=in_specs,
        out_specs=[pl.BlockSpec((tr, c), lambda i: (i, 0)) for c, _ in outs]
        + [pl.BlockSpec(s, lambda i: (0, 0)) for s in accs],
        out_shape=[jax.ShapeDtypeStruct((R, c), dt) for c, dt in outs]
        + [jax.ShapeDtypeStruct(s, F32) for s in accs],
        compiler_params=_params("arbitrary"),
    )(*ins)


def _sigmoid(x):
    return 1.0 / (1.0 + jnp.exp(-x))


_GELU_C = math.sqrt(2.0 / math.pi)


def _gelu(x):
    return 0.5 * x * (1.0 + jnp.tanh(_GELU_C * (x + 0.044715 * (x * x * x))))


def _gelu_grad(x):
    t = jnp.tanh(_GELU_C * (x + 0.044715 * (x * x * x)))
    return 0.5 * (1.0 + t) + 0.5 * x * (1.0 - t * t) * (_GELU_C * (1.0 + 3.0 * 0.044715 * (x * x)))


def rms_fwd(x, g, name):
    def fn(x, g):
        r = lax.rsqrt(jnp.mean(x * x, axis=-1, keepdims=True) + EPS)
        return [x * r * g], []

    return rowwise(fn, [x, g], [(x.shape[1], BF16)], name=name)[0]


def _rms_tile(x, g):
    return x * lax.rsqrt(jnp.mean(x * x, axis=-1, keepdims=True) + EPS) * g


def _rms_bwd_tile(dn, x, g, dres):
    r = lax.rsqrt(jnp.mean(x * x, axis=-1, keepdims=True) + EPS)
    w = dn * g
    dx = r * w - x * (r * r * r) * jnp.mean(x * w, axis=-1, keepdims=True)
    return dres + dx, jnp.sum(dn * (x * r), axis=0, keepdims=True)


def _whole(shape):
    return pl.BlockSpec(shape, lambda: (0,) * len(shape))


def _zoh(lr, li, ldt):
    dt = jnp.exp(ldt)
    mag = jnp.exp(lr * dt)
    ar, ai = mag * jnp.cos(li * dt), mag * jnp.sin(li * dt)
    den = lr * lr + li * li
    kr = ((ar - 1.0) * lr + ai * li) / den
    ki = (ai * lr - (ar - 1.0) * li) / den
    return dt, ar, ai, den, kr, ki


def ssm_prep(lam_re, lam_im, log_dt, b_re, b_im, name):
    n = lam_re.shape[0]

    def body(lr_ref, li_ref, ldt_ref, br_ref, bi_ref, ar_ref, ai_ref, bbr_ref, bbi_ref):
        _, ar, ai, _, kr, ki = _zoh(lr_ref[...], li_ref[...], ldt_ref[...])
        br, bi = br_ref[...], bi_ref[...]
        ar_ref[...] = ar
        ai_ref[...] = ai
        bbr_ref[...] = kr * br - ki * bi
        bbi_ref[...] = kr * bi + ki * br

    col, mat = (n, 1), (n, SSM_GROUP_CH)
    return pl.pallas_call(
        body, name=name,
        in_specs=[_whole(col)] * 3 + [_whole(mat)] * 2,
        out_specs=[_whole(col)] * 2 + [_whole(mat)] * 2,
        out_shape=[jax.ShapeDtypeStruct(col, F32)] * 2 + [jax.ShapeDtypeStruct(mat, F32)] * 2,
        compiler_params=_params(),
    )(lam_re, lam_im, log_dt, b_re, b_im)


def ssm_prep_bwd(lam_re, lam_im, log_dt, b_re, b_im, d_ar, d_ai, d_bbr, d_bbi, name):
    n = lam_re.shape[0]
    n_groups = n // SSM_STATE

    def body(lr_ref, li_ref, ldt_ref, br_ref, bi_ref, dar_ref, dai_ref, dbr_ref, dbi_ref,
             glr_ref, gli_ref, gdt_ref, gbr_ref, gbi_ref):
        lr, li = lr_ref[...], li_ref[...]
        dt, ar, ai, den, kr, ki = _zoh(lr, li, ldt_ref[...])
        br, bi, dbr, dbi = br_ref[...], bi_ref[...], dbr_ref[...], dbi_ref[...]
        gbr_ref[...] = kr * dbr + ki * dbi
        gbi_ref[...] = kr * dbi - ki * dbr
        gkr = jnp.sum(br * dbr + bi * dbi, axis=1, keepdims=True)
        gki = jnp.sum(br * dbi - bi * dbr, axis=1, keepdims=True)
        gar = dar_ref[...] + (gkr * lr - gki * li) / den
        gai = dai_ref[...] + (gki * lr + gkr * li) / den
        qr, qi = -(kr * lr + ki * li) / den, -(ki * lr - kr * li) / den
        g1r, g1i = qr * gkr + qi * gki, qr * gki - qi * gkr
        g2r, g2i = dt * (ar * gar + ai * gai), dt * (ar * gai - ai * gar)
        glr_ref[...] = g1r + g2r
        gli_ref[...] = g1i + g2i
        pr, pi_ = lr * ar - li * ai, lr * ai + li * ar
        gdt = (pr * gar + pi_ * gai) * dt
        grp = lax.broadcasted_iota(jnp.int32, (n, n_groups), 0) // SSM_STATE
        sel = grp == lax.broadcasted_iota(jnp.int32, (n, n_groups), 1)
        gdt_ref[...] = jnp.sum(jnp.where(sel, gdt, 0.0), axis=0, keepdims=True)

    col, mat = (n, 1), (n, SSM_GROUP_CH)
    return pl.pallas_call(
        body, name=name,
        in_specs=[_whole(col)] * 3 + [_whole(mat)] * 2 + [_whole(col)] * 2 + [_whole(mat)] * 2,
        out_specs=[_whole(col)] * 2 + [_whole((1, n_groups))] + [_whole(mat)] * 2,
        out_shape=[jax.ShapeDtypeStruct(col, F32)] * 2 + [jax.ShapeDtypeStruct((1, n_groups), F32)]
        + [jax.ShapeDtypeStruct(mat, F32)] * 2,
        compiler_params=_params(),
    )(lam_re, lam_im, log_dt, b_re, b_im, d_ar, d_ai, d_bbr, d_bbi)


def _cmul(ar, ai, br, bi):
    return ar * br - ai * bi, ar * bi + ai * br


def _scan_block(xr, xi, lr, li, carry_r, carry_i, or_ref, oi_ref, loc_r, loc_i, reverse):
    tb, cb = xr.shape
    ng = tb // SUBLANES
    xr = xr.reshape(ng, SUBLANES, cb)
    xi = xi.reshape(ng, SUBLANES, cb)
    rid = lax.broadcasted_iota(jnp.int32, (1, SUBLANES, cb), 1)
    pr, pi_ = lr.reshape(1, 1, cb), li.reshape(1, 1, cb)
    powers = []
    for k in (1, 2, 4):
        powers.append((pr, pi_))
        shift = SUBLANES - k if reverse else k
        sr, si = pltpu.roll(xr, shift, 1), pltpu.roll(xi, shift, 1)
        keep = (rid < SUBLANES - k) if reverse else (rid >= k)
        tr_, ti_ = _cmul(jnp.where(keep, pr, 0.0), jnp.where(keep, pi_, 0.0), sr, si)
        xr = xr + tr_
        xi = xi + ti_
        pr, pi_ = _cmul(pr, pi_, pr, pi_)
    loc_r[...] = xr
    loc_i[...] = xi
    (p1r, p1i), (p2r, p2i), (p4r, p4i) = powers
    dist = lax.broadcasted_iota(jnp.int32, (SUBLANES, cb), 0)
    if reverse:
        dist = SUBLANES - 1 - dist
    wr = jnp.broadcast_to(p1r.reshape(1, cb), (SUBLANES, cb))
    wi = jnp.broadcast_to(p1i.reshape(1, cb), (SUBLANES, cb))
    for bit, (qr, qi) in ((1, (p1r, p1i)), (2, (p2r, p2i)), (4, (p4r, p4i))):
        mr, mi = _cmul(wr, wi, qr.reshape(1, cb), qi.reshape(1, cb))
        on = (dist & bit) != 0
        wr, wi = jnp.where(on, mr, wr), jnp.where(on, mi, wi)
    last = 0 if reverse else SUBLANES - 1

    def step(j, carry):
        cr, ci = carry
        g = (ng - 1 - j) if reverse else j
        fr = loc_r[g] + (wr * cr - wi * ci)
        fi = loc_i[g] + (wr * ci + wi * cr)
        rows = pl.ds(pl.multiple_of(g * SUBLANES, SUBLANES), SUBLANES)
        or_ref[rows, :] = fr
        oi_ref[rows, :] = fi
        return fr[last:last + 1, :], fi[last:last + 1, :]

    cr, ci = lax.fori_loop(0, ng, step, (carry_r[...], carry_i[...]))
    carry_r[...] = cr
    carry_i[...] = ci


def _scan_tiles(seq_len, n_ch):
    return min(256, seq_len), min(512, n_ch)


def _run_exchange(plan, step, n_steps, refs):
    @pl.when(step == 0)
    def _():
        plan.start(*refs)

    for part, at in plan.relay_steps(n_steps):
        @pl.when(step == at)
        def _(part=part):
            plan.relay(part, *refs)

    @pl.when(step == n_steps - 1)
    def _():
        plan.finish(*refs)


def _exchange_args(plan):
    if plan is None:
        return [], [], [], []
    return list(plan.ins), [ANY] * len(plan.ins), list(plan.out_shape), list(plan.sems)


def ssm_scan(x_re, x_im, lam_re, lam_im, batch, name, exchange=None):
    n, nch = x_re.shape
    seq = n // batch
    tb, cb = _scan_tiles(seq, nch)
    nt, nc = seq // tb, nch // cb
    ex_ins, ex_specs, ex_out, ex_sems = _exchange_args(exchange)
    n_ex = len(ex_ins)

    def body(*refs):
        xr_ref, xi_ref, lr_ref, li_ref = refs[:4]
        or_ref, oi_ref = refs[4 + n_ex:6 + n_ex]
        car_r, car_i, loc_r, loc_i, s_r, s_i = refs[6 + 2 * n_ex:12 + 2 * n_ex]
        if exchange is not None:
            step = (pl.program_id(0) * batch + pl.program_id(1)) * nt + pl.program_id(2)
            _run_exchange(exchange, step, nc * batch * nt,
                          (refs[4:4 + n_ex], refs[6 + n_ex:6 + 2 * n_ex], refs[12 + 2 * n_ex:]))

        @pl.when(pl.program_id(2) == 0)
        def _():
            car_r[...] = jnp.zeros_like(car_r)
            car_i[...] = jnp.zeros_like(car_i)

        _scan_block(xr_ref[...], xi_ref[...], lr_ref[...], li_ref[...], car_r, car_i, s_r, s_i,
                    loc_r, loc_i, reverse=False)
        or_ref[...] = s_r[...].astype(or_ref.dtype)
        oi_ref[...] = s_i[...].astype(oi_ref.dtype)

    blk = pl.BlockSpec((tb, cb), lambda c, b, t: (b * nt + t, c))
    lam_spec = pl.BlockSpec((1, cb), lambda c, b, t: (0, c))
    res = pl.pallas_call(
        body, name=name,
        grid=(nc, batch, nt),
        in_specs=[blk, blk, lam_spec, lam_spec] + ex_specs,
        out_specs=[blk, blk] + ex_specs,
        out_shape=[jax.ShapeDtypeStruct((n, nch), BF16)] * 2 + ex_out,
        scratch_shapes=[pltpu.VMEM((1, cb), F32)] * 2 + [pltpu.VMEM((tb // SUBLANES, SUBLANES, cb), F32)] * 2
        + [pltpu.VMEM((tb, cb), F32)] * 2 + ex_sems,
        compiler_params=_params("arbitrary", "arbitrary", "arbitrary"),
    )(x_re, x_im, lam_re, lam_im, *ex_ins)
    return res[0], res[1], list(res[2:])


def ssm_scan_bwd(d_re, d_im, s_re, s_im, lam_re, lam_im, batch, name, exchange=None):
    n, nch = d_re.shape
    seq = n // batch
    tb, cb = _scan_tiles(seq, nch)
    nt, nc = seq // tb, nch // cb
    halo_rows = 2 * SUBLANES
    hb = tb // halo_rows
    ex_ins, ex_specs, ex_out, ex_sems = _exchange_args(exchange)
    n_ex = len(ex_ins)

    def body(*refs):
        xr_ref, xi_ref, sr_ref, si_ref, hr_ref, hi_ref, lr_ref, li_ref = refs[:8]
        or_ref, oi_ref, dlr_ref, dli_ref = refs[8 + n_ex:12 + n_ex]
        car_r, car_i, loc_r, loc_i, g_r, g_i = refs[12 + 2 * n_ex:18 + 2 * n_ex]
        b, t = pl.program_id(1), pl.program_id(2)
        if exchange is not None:
            step = (pl.program_id(0) * batch + b) * nt + t
            _run_exchange(exchange, step, nc * batch * nt,
                          (refs[8:8 + n_ex], refs[12 + n_ex:12 + 2 * n_ex], refs[18 + 2 * n_ex:]))

        @pl.when(t == 0)
        def _():
            car_r[...] = jnp.zeros_like(car_r)
            car_i[...] = jnp.zeros_like(car_i)

        _scan_block(xr_ref[...], xi_ref[...], lr_ref[...], -li_ref[...], car_r, car_i, g_r, g_i,
                    loc_r, loc_i, reverse=True)
        gr, gi = g_r[...], g_i[...]
        or_ref[...] = gr.astype(or_ref.dtype)
        oi_ref[...] = gi.astype(oi_ref.dtype)
        first_block = t == nt - 1
        row = lax.broadcasted_iota(jnp.int32, (tb, cb), 0)
        hr = jnp.where(first_block, 0.0, hr_ref[...].astype(F32)[halo_rows - 1:halo_rows, :])
        hi = jnp.where(first_block, 0.0, hi_ref[...].astype(F32)[halo_rows - 1:halo_rows, :])
        pr = jnp.where(row == 0, hr, pltpu.roll(sr_ref[...].astype(F32), 1, 0))
        pi_ = jnp.where(row == 0, hi, pltpu.roll(si_ref[...].astype(F32), 1, 0))
        dlr = jnp.sum(gr * pr + gi * pi_, axis=0, keepdims=True)
        dli = jnp.sum(gi * pr - gr * pi_, axis=0, keepdims=True)
        start = jnp.logical_and(b == 0, t == 0)

        @pl.when(start)
        def _():
            dlr_ref[...] = dlr
            dli_ref[...] = dli

        @pl.when(jnp.logical_not(start))
        def _():
            dlr_ref[...] += dlr
            dli_ref[...] += dli

    def blk(c, b, t):
        return b * nt + (nt - 1 - t)

    st_spec = pl.BlockSpec((tb, cb), lambda c, b, t: (blk(c, b, t), c))
    halo_spec = pl.BlockSpec((halo_rows, cb), lambda c, b, t: (jnp.maximum(blk(c, b, t) * hb - 1, 0), c))
    row_spec = pl.BlockSpec((1, cb), lambda c, b, t: (0, c))
    res = pl.pallas_call(
        body, name=name,
        grid=(nc, batch, nt),
        in_specs=[st_spec] * 4 + [halo_spec] * 2 + [row_spec] * 2 + ex_specs,
        out_specs=[st_spec, st_spec, row_spec, row_spec] + ex_specs,
        out_shape=[jax.ShapeDtypeStruct((n, nch), BF16)] * 2 + [jax.ShapeDtypeStruct((1, nch), F32)] * 2 + ex_out,
        scratch_shapes=[pltpu.VMEM((1, cb), F32)] * 2 + [pltpu.VMEM((tb // SUBLANES, SUBLANES, cb), F32)] * 2
        + [pltpu.VMEM((tb, cb), F32)] * 2 + ex_sems,
        compiler_params=_params("arbitrary", "arbitrary", "arbitrary"),
    )(d_re, d_im, s_re, s_im, s_re, s_im, lam_re, lam_im, *ex_ins)
    return res[0], res[1], res[2], res[3], list(res[4:])


def _pool_tiles(seq_len):
    return min(512, seq_len)


def _window_sums(x, n_steps, forward_in_time):
    rows = x.shape[0]
    k = 1
    for _ in range(n_steps):
        x = x + pltpu.roll(x, k if forward_in_time else rows - k, 0)
        k *= 2
    return x


def pool_fwd(u, w_pool, scale, batch, name):
    n, c = u.shape
    seq = n // batch
    tb = _pool_tiles(seq)
    nt = seq // tb
    gc = c // len(POOL_WINDOWS)
    hb = tb // POOL_HALO

    def body(x_ref, halo_ref, w_ref, sc_ref, y_ref, q_ref):
        t = pl.program_id(1)
        halo = jnp.where(t == 0, 0.0, halo_ref[...])
        full = jnp.concatenate([halo, x_ref[...]], axis=0)
        pos = lax.broadcasted_iota(jnp.int32, (tb, gc), 0) + t * tb + 1
        for gi, win in enumerate(POOL_WINDOWS):
            cols = slice(gi * gc, (gi + 1) * gc)
            sums = _window_sums(full[:, cols], gi + 1, True)[POOL_HALO:, :]
            cnt = jnp.minimum(pos, win).astype(F32)
            q = sums / cnt - x_ref[:, cols]
            r = jnp.dot(q.astype(BF16), w_ref[gi].astype(BF16), preferred_element_type=F32)
            q_ref[:, cols] = q.astype(q_ref.dtype)
            y_ref[:, cols] = (r * sc_ref[:, cols]).astype(y_ref.dtype)

    return pl.pallas_call(
        body, name=name,
        grid=(batch, nt),
        in_specs=[pl.BlockSpec((tb, c), lambda b, t: (b * nt + t, 0)),
                  pl.BlockSpec((POOL_HALO, c), lambda b, t: (jnp.maximum((b * nt + t) * hb - 1, 0), 0)),
                  pl.BlockSpec(w_pool.shape, lambda b, t: (0, 0, 0)),
                  pl.BlockSpec((1, c), lambda b, t: (0, 0))],
        out_specs=[pl.BlockSpec((tb, c), lambda b, t: (b * nt + t, 0))] * 2,
        out_shape=[jax.ShapeDtypeStruct((n, c), BF16)] * 2,
        compiler_params=_params("parallel", "arbitrary"),
    )(u, u, w_pool, scale)


def pool_bwd(dy, q, w_pool, scale, batch, name):
    n, c = dy.shape
    seq = n // batch
    tb = _pool_tiles(seq)
    nt = seq // tb
    ng = len(POOL_WINDOWS)
    gc = c // ng
    hb = tb // POOL_HALO
    n_blocks = n // POOL_HALO

    def body(dy_ref, dyh_ref, q_ref, w_ref, sc_ref, du_ref, dw_ref, dsc_ref):
        b, t = pl.program_id(0), pl.program_id(1)
        last = t == nt - 1
        dy_full = jnp.concatenate([dy_ref[...], jnp.where(last, 0.0, dyh_ref[...])], axis=0)
        pos = lax.broadcasted_iota(jnp.int32, (tb + POOL_HALO, gc), 0) + t * tb + 1
        start = jnp.logical_and(b == 0, t == 0)
        for gi, win in enumerate(POOL_WINDOWS):
            cols = slice(gi * gc, (gi + 1) * gc)
            w = w_ref[gi].astype(BF16)
            dr = dy_full[:, cols] * sc_ref[:, cols]
            dq = lax.dot_general(dr.astype(BF16), w, (((1,), (1,)), ((), ())), preferred_element_type=F32)
            cnt = jnp.minimum(pos, win).astype(F32)
            back = _window_sums(dq / cnt, gi + 1, False)
            du_ref[:, cols] = back[:tb, :] - dq[:tb, :]
            qb = q_ref[:, cols]
            r = jnp.dot(qb, w, preferred_element_type=F32)
            dw = lax.dot_general(qb, dr[:tb, :].astype(BF16), (((0,), (0,)), ((), ())), preferred_element_type=F32)
            dsc = jnp.sum(dy_ref[:, cols] * r, axis=0, keepdims=True)

            @pl.when(start)
            def _(gi=gi, cols=cols, dw=dw, dsc=dsc):
                dw_ref[gi] = dw
                dsc_ref[:, cols] = dsc

            @pl.when(jnp.logical_not(start))
            def _(gi=gi, cols=cols, dw=dw, dsc=dsc):
                dw_ref[gi] += dw
                dsc_ref[:, cols] += dsc

    blk = pl.BlockSpec((tb, c), lambda b, t: (b * nt + t, 0))
    halo = pl.BlockSpec((POOL_HALO, c), lambda b, t: (jnp.minimum((b * nt + t + 1) * hb, n_blocks - 1), 0))
    return pl.pallas_call(
        body, name=name,
        grid=(batch, nt),
        in_specs=[blk, halo, blk,
                  pl.BlockSpec(w_pool.shape, lambda b, t: (0, 0, 0)),
                  pl.BlockSpec((1, c), lambda b, t: (0, 0))],
        out_specs=[blk, pl.BlockSpec(w_pool.shape, lambda b, t: (0, 0, 0)), pl.BlockSpec((1, c), lambda b, t: (0, 0))],
        out_shape=[jax.ShapeDtypeStruct((n, c), F32), jax.ShapeDtypeStruct(w_pool.shape, F32),
                   jax.ShapeDtypeStruct((1, c), F32)],
        compiler_params=_params("arbitrary", "arbitrary"),
    )(dy, dy, q, w_pool, scale)


def _place():
    return lax.axis_index("x"), lax.axis_index("y"), lax.axis_index("c")


class GatherPlan:
    def __init__(self, arrs):
        self.ins = list(arrs)
        na = len(arrs)
        self.out_shape = [jax.ShapeDtypeStruct((N_DEV,) + a.shape, a.dtype) for a in arrs]
        self.sems = [pltpu.SemaphoreType.DMA((na, 7)), pltpu.SemaphoreType.DMA((na, 7)), pltpu.SemaphoreType.DMA((na,))]
        self.sizes = [math.prod(a.shape) * a.dtype.itemsize for a in arrs]

    def relay_steps(self, n_steps):
        total, done, steps = sum(self.sizes), 0, []
        for a, size in enumerate(self.sizes):
            done += size
            steps.append((a, min(n_steps - 1, (done * (n_steps - 1)) // total)))
        return steps

    def _copy(self, outs, sems, a, k, block, to, src=None):
        dst = outs[a].at[4 * block[0] + 2 * block[1] + block[2]]
        return pltpu.make_async_remote_copy(
            src_ref=dst if src is None else src, dst_ref=dst,
            send_sem=sems[0].at[a, k], recv_sem=sems[1].at[a, k], device_id=to, device_id_type=MESH)

    @staticmethod
    def _chips(x, y):
        return [(1 - x, y), (x, 1 - y), (1 - x, 1 - y)]

    def _local(self, ins, outs, sems, a, me):
        return pltpu.make_async_copy(ins[a], outs[a].at[4 * me[0] + 2 * me[1] + me[2]], sems[2].at[a])

    def start(self, ins, outs, sems):
        x, y, c = _place()
        me = (x, y, c)
        for a in range(len(ins)):
            self._local(ins, outs, sems, a, me).start()
            self._copy(outs, sems, a, 0, me, (x, y, 1 - c), src=ins[a]).start()
            for j, chip in enumerate(self._chips(x, y)):
                self._copy(outs, sems, a, 1 + j, me, (*chip, c), src=ins[a]).start()

    def relay(self, a, ins, outs, sems):
        x, y, c = _place()
        for j, chip in enumerate(self._chips(x, y)):
            self._copy(outs, sems, a, 1 + j, (*chip, c), (x, y, c)).wait_recv()
            self._copy(outs, sems, a, 4 + j, (*chip, c), (x, y, 1 - c)).start()

    def finish(self, ins, outs, sems):
        x, y, c = _place()
        me, sibling = (x, y, c), (x, y, 1 - c)
        for a in range(len(ins)):
            self._copy(outs, sems, a, 0, sibling, me).wait_recv()
            for j, chip in enumerate(self._chips(x, y)):
                self._copy(outs, sems, a, 4 + j, (*chip, 1 - c), me).wait_recv()
        for a in range(len(ins)):
            self._copy(outs, sems, a, 0, me, sibling, src=ins[a]).wait_send()
            for j, chip in enumerate(self._chips(x, y)):
                self._copy(outs, sems, a, 1 + j, me, (*chip, c), src=ins[a]).wait_send()
                self._copy(outs, sems, a, 4 + j, (*chip, c), sibling).wait_send()
            self._local(ins, outs, sems, a, me).wait()


class ChipScatterPlan:
    def __init__(self, arrs):
        self.ins = list(arrs)
        na = len(arrs)
        self.out_shape = [jax.ShapeDtypeStruct(a.shape, a.dtype) for a in arrs]
        self.sems = [pltpu.SemaphoreType.DMA((na, 3)), pltpu.SemaphoreType.DMA((na, 3)), pltpu.SemaphoreType.DMA((na,))]

    def relay_steps(self, n_steps):
        return []

    def _copy(self, ins, outs, sems, a, j, chip, c, mine):
        return pltpu.make_async_remote_copy(
            src_ref=ins[a].at[2 * chip[0] + chip[1]], dst_ref=outs[a].at[mine],
            send_sem=sems[0].at[a, j], recv_sem=sems[1].at[a, j], device_id=(*chip, c), device_id_type=MESH)

    def _all(self, ins, outs, sems):
        x, y, c = _place()
        mine = 2 * x + y
        remote = [self._copy(ins, outs, sems, a, j, chip, c, mine)
                  for a in range(len(ins)) for j, chip in enumerate(GatherPlan._chips(x, y))]
        local = [pltpu.make_async_copy(ins[a].at[mine], outs[a].at[mine], sems[2].at[a]) for a in range(len(ins))]
        return remote, local

    def start(self, ins, outs, sems):
        remote, local = self._all(ins, outs, sems)
        for cp in local + remote:
            cp.start()

    def finish(self, ins, outs, sems):
        remote, local = self._all(ins, outs, sems)
        for cp in remote:
            cp.wait_recv()
        for cp in remote:
            cp.wait_send()
        for cp in local:
            cp.wait()


def run_exchange(plan, name):
    n_in, n_out = len(plan.ins), len(plan.out_shape)

    def body(*refs):
        parts = (refs[:n_in], refs[n_in:n_in + n_out], refs[n_in + n_out:])
        plan.start(*parts)
        for part, _ in plan.relay_steps(1):
            plan.relay(part, *parts)
        plan.finish(*parts)

    return pl.pallas_call(
        body, name=name,
        in_specs=[ANY] * n_in, out_specs=[ANY] * n_out,
        out_shape=plan.out_shape, scratch_shapes=plan.sems,
    )(*plan.ins)


def swap_with_sibling(arrs, name):
    na = len(arrs)

    def body(*refs):
        ins, outs = refs[:na], refs[na:2 * na]
        send_sems, recv_sems = refs[2 * na:]
        x, y, c = _place()
        copies = [pltpu.make_async_remote_copy(
            src_ref=ins[a], dst_ref=outs[a], send_sem=send_sems.at[a], recv_sem=recv_sems.at[a],
            device_id=(x, y, 1 - c), device_id_type=MESH) for a in range(na)]
        for cp in copies:
            cp.start()
        for cp in copies:
            cp.wait()

    return pl.pallas_call(
        body, name=name,
        in_specs=[ANY] * na, out_specs=[ANY] * na,
        out_shape=[jax.ShapeDtypeStruct(a.shape, a.dtype) for a in arrs],
        scratch_shapes=[pltpu.SemaphoreType.DMA((na,)), pltpu.SemaphoreType.DMA((na,))],
    )(*arrs)


def adamw(w, gparts, m, v, name, tr=256):
    rows, cols = w.shape
    parts = gparts.shape[0]
    tr = _row_tile(rows, tr)
    c1 =1.0 - ADAM_B1 ** ADAM_STEP
    c2 = 1.0 - ADAM_B2 ** ADAM_STEP

    def body(w_ref, g_ref, m_ref, v_ref, go_ref, d_ref, mo_ref, vo_ref):
        g = g_ref[0].astype(F32)
        for p_ in range(1, parts):
            g = g + g_ref[p_].astype(F32)
        m_new = ADAM_B1 * m_ref[...] + (1.0 - ADAM_B1) * g
        v_new = ADAM_B2 * v_ref[...] + (1.0 - ADAM_B2) * (g * g)
        m_hat = m_new / c1
        v_hat = v_new / c2
        go_ref[...] = g
        d_ref[...] = -ADAM_LR * (m_hat / (jnp.sqrt(v_hat) + ADAM_EPS) + ADAM_WD * w_ref[...])
        mo_ref[...] = m_new
        vo_ref[...] = v_new

    blk = pl.BlockSpec((tr, cols), lambda i: (i, 0))
    return pl.pallas_call(
        body, name=name,
        grid=(rows // tr,),
        in_specs=[blk, pl.BlockSpec((parts, tr, cols), lambda i: (0, i, 0)), blk, blk],
        out_specs=[blk] * 4,
        out_shape=[jax.ShapeDtypeStruct((rows, cols), F32)] * 4,
        compiler_params=_params("parallel"),
    )(w, gparts, m, v)


def add2(a, b, name, out_dtype):
    return rowwise(lambda a, b: ([a.astype(F32) + b.astype(F32)], []), [a, b], [(a.shape[1], out_dtype)],
                   name=name, tr=256)[0]


def _block_diag(x):
    g, a, b = x.shape
    eye = jnp.eye(g, dtype=x.dtype)
    return (x[:, :, None, :] * eye[:, None, :, None]).reshape(g * a, g * b)


def _diag_blocks(x, a, b):
    per = x.shape[1] // b
    x5 = x.reshape(SSM_SUPER, per, a, per, b)
    eye = jnp.eye(per, dtype=x.dtype)
    return jnp.sum(x5 * eye[None, :, None, :, None], axis=3).reshape(SSM_SUPER * per, a, b)


def _swiglu_epi(g, u):
    s = _sigmoid(g)
    silu = g * s
    return u * (s * (1.0 + g * (1.0 - s))), silu, silu * u


def _residual_epi(scale, with_norm):
    if with_norm:
        def epi(acc, res, gain):
            out = res + scale * acc
            return out, _rms_tile(out, gain)
    else:
        def epi(acc, res):
            return (res + scale * acc,)
    return epi


def _next_norm(next_gain):
    if next_gain is None:
        return [], (F32,)
    return [next_gain], (F32, BF16)


FFN_WIDE = 2816


def ffn_fwd(h, n, wi, wo, next_gain, tag):
    dact_g, dact_u, act = matmul(n, [(wi, 0), (wi, 1)], mode="nn", name=f"{tag}_in", separate=True, epi=_swiglu_epi,
                                 out_dtypes=(BF16, BF16, BF16), tm=512, tn=FFN_WIDE)
    more, dtypes = _next_norm(next_gain)
    res = matmul(act, wo, mode="nn", name=f"{tag}_out", epi=_residual_epi(0.5, bool(more)), extras=[h] + more,
                 out_dtypes=dtypes, tm=512, tk=FFN_WIDE)
    return res[0], (res[1] if more else None), (h, n, dact_g, dact_u, act)


def ffn_bwd(dh, saved, gain, wi, wo, tag):
    h, n, dact_g, dact_u, act = saved
    dg, du = matmul(dh, wo, mode="nt", name=f"{tag}_out_dx", extras=[dact_g, dact_u], out_dtypes=(BF16, BF16),
                    epi=lambda acc, fg, fu: (0.5 * acc * fg, 0.5 * acc * fu), tm=512, tn=FFN_WIDE)
    d_wo, = matmul(act, dh, mode="tn", name=f"{tag}_out_dw", scale=0.5, tm=FFN_WIDE)
    d_wi = [matmul(n, half, mode="tn", name=f"{tag}_in_dw", tn=FFN_WIDE)[0] for half in (dg, du)]
    dh_new, d_gain = matmul([dg, du], [(wi, 0), (wi, 1)], mode="nt", name=f"{tag}_in_dx", epi=_rms_bwd_tile,
                            extras=[h, gain, dh], out_dtypes=(F32,), col_sums=1, tm=256, tk=FFN_WIDE)
    return dh_new, d_gain, d_wi, d_wo


def mix_fwd(h, n, lw, batch, next_gain, tag, exchange=None):
    sw = h.shape[1] // 2
    us, up = matmul(n, [lw["w_in"][:, :sw], lw["w_in"][:, sw:]], mode="nn", name=f"{tag}_in", separate=True,
                    out_dtypes=(F32, F32))
    lam_r, lam_i, bb_r, bb_i = ssm_prep(lw["lam_re"], lw["lam_im"], lw["log_dt"], lw["b_re"], lw["b_im"], f"{tag}_zoh")
    lam = (lam_r.reshape(1, SSM_CH), lam_i.reshape(1, SSM_CH))
    b_mats = [_block_diag(bb.reshape(SSM_GROUPS, SSM_STATE, SSM_GROUP_CH).transpose(0, 2, 1)).astype(BF16)
              for bb in (bb_r, bb_i)]
    c_mats = [_block_diag(cc.transpose(0, 2, 1)).astype(BF16) for cc in (lw["c_re"], -lw["c_im"])]
    bu_re, bu_im = matmul(us, b_mats, mode="nn", name=f"{tag}_bu", separate=True, diag=SSM_SUPER)
    s_re, s_im, exchanged = ssm_scan(bu_re, bu_im, *lam, batch, f"{tag}_scan", exchange)
    y0, y1 = matmul([s_re, s_im], c_mats, mode="nn", name=f"{tag}_c", diag=SSM_SUPER,
                    epi=lambda acc, u, d: (acc + d * u, _gelu(acc + d * u)), extras=[us, lw["ssm_d"]],
                    out_dtypes=(F32, BF16))
    y2, gl = matmul(y1, lw["w_glu"], mode="nn", name=f"{tag}_glu",
                    epi=lambda acc, y0: (_gelu(y0) * _sigmoid(acc), acc), extras=[y0], out_dtypes=(BF16, F32))
    yp, q = pool_fwd(up, lw["pool_w"], lw["pool_scale"], batch, f"{tag}_pool")
    more, dtypes = _next_norm(next_gain)
    res = matmul([y2, yp], [lw["w_out"][:sw], lw["w_out"][sw:]], mode="nn", name=f"{tag}_out",
                 epi=_residual_epi(1.0, bool(more)), extras=[h] + more, out_dtypes=dtypes)
    saved = (h, n, us, lam, b_mats, c_mats, s_re, s_im, y0, y1, gl, y2, yp, q)
    return res[0], (res[1] if more else None), saved, exchanged


def mix_bwd(dh, saved, lw, batch, tag, exchange=None):
    h, n, us, lam, b_mats, c_mats, s_re, s_im, y0, y1, gl, y2, yp, q = saved
    sw = h.shape[1] // 2
    w_out_s, w_out_p = lw["w_out"][:sw], lw["w_out"][sw:]
    d_wo_s, d_wo_p = matmul([y2, yp], dh, mode="tn", name=f"{tag}_out_dw", separate=True)
    def out_dx_epi(dy2, dyp, y0, gl):
        sg = _sigmoid(gl)
        return dy2, dyp, dy2 * _gelu(y0) * sg * (1.0 - sg)

    dy2, dyp, tg = matmul(dh, [w_out_s, w_out_p], mode="nt", name=f"{tag}_out_dx", separate=True, epi=out_dx_epi,
                          extras=[y0, gl], out_dtypes=(F32, F32, BF16))
    dup, d_pool_w, d_pool_scale = pool_bwd(dyp, q, lw["pool_w"], lw["pool_scale"], batch, f"{tag}_pool_bwd")
    def dy0_epi(acc, dy2, gl, y0, u):
        dy0 = (acc + dy2 * _sigmoid(gl)) * _gelu_grad(y0)
        return dy0, jnp.sum(dy0 * u, axis=0, keepdims=True)

    dy0, d_d = matmul(tg, lw["w_glu"], mode="nt", name=f"{tag}_glu_dx", epi=dy0_epi, extras=[dy2, gl, y0, us],
                      out_dtypes=(F32,), col_sums=1)
    d_w_glu, = matmul(y1, tg, mode="tn", name=f"{tag}_glu_dw")
    gd_re, gd_im = matmul(dy0, c_mats, mode="nt", name=f"{tag}_c_dx", separate=True, diag=SSM_SUPER)
    d_c_top, d_c_bot = matmul([s_re, s_im], dy0, mode="tn", name=f"{tag}_c_dw", separate=True, diag=SSM_SUPER)
    g_re, g_im, d_lam_r, d_lam_i, exchanged = ssm_scan_bwd(gd_re, gd_im, s_re, s_im, *lam, batch, f"{tag}_scan_bwd",
                                                           exchange)
    dus, = matmul([g_re, g_im], b_mats, mode="nt", name=f"{tag}_bu_dx", diag=SSM_SUPER,
                  epi=lambda acc, dy0, d: (acc + d * dy0,), extras=[dy0, lw["ssm_d"]])
    d_b_re, d_b_im = matmul(us, [g_re, g_im], mode="tn", name=f"{tag}_bu_dw", separate=True, diag=SSM_SUPER)
    d_bb_r = _diag_blocks(d_b_re, SSM_GROUP_CH, SSM_STATE).transpose(0, 2, 1).reshape(SSM_CH, SSM_GROUP_CH)
    d_bb_i = _diag_blocks(d_b_im, SSM_GROUP_CH, SSM_STATE).transpose(0, 2, 1).reshape(SSM_CH, SSM_GROUP_CH)
    d_lr, d_li, d_ldt, d_br, d_bi = ssm_prep_bwd(
        lw["lam_re"], lw["lam_im"], lw["log_dt"], lw["b_re"], lw["b_im"],
        d_lam_r.reshape(SSM_CH, 1), d_lam_i.reshape(SSM_CH, 1), d_bb_r, d_bb_i, f"{tag}_zoh_bwd")
    d_c_re = _diag_blocks(d_c_top, SSM_STATE, SSM_GROUP_CH).transpose(0, 2, 1)
    d_c_im = -_diag_blocks(d_c_bot, SSM_STATE, SSM_GROUP_CH).transpose(0, 2, 1)
    d_w_in_s, d_w_in_p = matmul(n, [dus, dup], mode="tn", name=f"{tag}_in_dw", separate=True)
    dh_new, d_gain = matmul([dus, dup], [lw["w_in"][:, :sw], lw["w_in"][:, sw:]], mode="nt", name=f"{tag}_in_dx",
                            epi=_rms_bwd_tile, extras=[h, lw["mix_norm"], dh], out_dtypes=(F32,), col_sums=1, tm=512)
    grads = dict(mix_norm=d_gain, w_in=jnp.concatenate([d_w_in_s, d_w_in_p], axis=1),
                 ssm_lambda_re=d_lr, ssm_lambda_im=d_li, ssm_log_dt=d_ldt, ssm_b_re=d_br, ssm_b_im=d_bi,
                 ssm_c_re=d_c_re, ssm_c_im=d_c_im, ssm_d=d_d, ssm_w_glu=d_w_glu, pool_w=d_pool_w,
                 pool_scale=d_pool_scale, w_out=jnp.concatenate([d_wo_s, d_wo_p], axis=0))
    return dh_new, grads, exchanged


def ple_fwd(h, n, p, w_gate, w_proj, next_gain, tag):
    e, = matmul(p, w_proj, mode="nn", name=f"{tag}_proj")
    if next_gain is None:
        def epi(acc, e, res):
            return res + _sigmoid(acc) * e, acc
        more, dtypes = [], (F32, F32)
    else:
        def epi(acc, e, res, gain):
            out = res + _sigmoid(acc) * e
            return out, acc, _rms_tile(out, gain)
        more, dtypes = [next_gain], (F32, F32, BF16)
    res = matmul(n, w_gate, mode="nn", name=f"{tag}_gate", epi=epi, extras=[e, h] + more, out_dtypes=dtypes, tm=512)
    return res[0], (res[2] if more else None), (h, n, e, res[1])


def ple_bwd(dh, saved, p, gain, w_gate, tag):
    h, n, e, pre = saved
    d = h.shape[1]

    def fn(dh, e, pre):
        s = _sigmoid(pre)
        return [dh * e * s * (1.0 - s), dh * s], []

    dpre, de = rowwise(fn, [dh, e, pre], [(d, BF16), (d, BF16)], name=f"{tag}_gate_bwd")
    d_w_gate, = matmul(n, dpre, mode="tn", name=f"{tag}_gate_dw")
    d_w_proj, = matmul(p, de, mode="tn", name=f"{tag}_proj_dw")
    dh_new, d_gain = matmul(dpre, w_gate, mode="nt", name=f"{tag}_gate_dx", epi=_rms_bwd_tile,
                            extras=[h, gain, dh], out_dtypes=(F32,), col_sums=1, tm=512)
    return dh_new, d_gain, d_w_gate, d_w_proj


def loss_head(h, gain, target, name):
    d = h.shape[1]

    def fn(h, g, t):
        r = lax.rsqrt(jnp.mean(h * h, axis=-1, keepdims=True) + EPS)
        diff = h * r * g - t
        sq = jnp.sum(jnp.sum(diff * diff, axis=1, keepdims=True), axis=0, keepdims=True)
        dy = diff * (1.0 / d)
        w = dy * g
        dh = r * w - h * (r * r * r) * jnp.mean(h * w, axis=-1, keepdims=True)
        return [dh], [sq, jnp.sum(dy * (h * r), axis=0, keepdims=True)]

    dh, sq, d_gain = rowwise(fn, [h, gain, target], [(d, F32)], [(1, 1), (1, d)], name=name)
    return 0.5 / d * sq[0, 0], dh, d_gain


SHARDED = {
    "ffn1_wi": 1, "ffn1_wo": 0, "w_in": 0, "ssm_w_glu": 0, "w_out": 0, "ffn2_wi": 1, "ffn2_wo": 0,
    "ple_w_gate": 0, "ple_w_proj": 1,
}
WEIGHTS = ["ffn1_norm", "ffn1_wi", "ffn1_wo", "mix_norm", "w_in", "ssm_lambda_re", "ssm_lambda_im", "ssm_log_dt",
           "ssm_b_re", "ssm_b_im", "ssm_c_re", "ssm_c_im", "ssm_d", "ssm_w_glu", "pool_w", "pool_scale", "w_out",
           "ffn2_norm", "ffn2_wi", "ffn2_wo", "ple_norm", "ple_w_gate", "ple_w_proj", "final_norm"]
REPLICATED = [n for n in WEIGHTS if n not in SHARDED]


HALVED = ("ffn1_wi", "ffn2_wi")


def _unshard(gathered, axis, halved):
    if halved:
        _, rows, cols = gathered.shape
        return gathered.reshape(2, 4, rows, cols).transpose(0, 2, 1, 3).reshape(2, rows, 4 * cols)
    g = jnp.moveaxis(gathered, 0, axis)
    shp = g.shape
    return g.reshape(shp[:axis] + (shp[axis] * shp[axis + 1],) + shp[axis + 2:])


def _split_for_scatter(full, axis, c, halved):
    if halved:
        rows, cols = full[0].shape

        def pick(cc):
            parts = [lax.dynamic_index_in_dim(h.reshape(rows, 2, 2, cols // 4), cc, 2, keepdims=False).transpose(1, 0, 2)
                     for h in full]
            return jnp.concatenate(parts, axis=0)

        return pick(c), pick(1 - c).astype(BF16)
    shp = full.shape
    g = full.reshape(shp[:axis] + (4, 2, shp[axis] // N_DEV) + shp[axis + 1:])
    keep = lax.dynamic_index_in_dim(g, c, axis + 1, keepdims=False)
    send = lax.dynamic_index_in_dim(g, 1 - c, axis + 1, keepdims=False)
    return jnp.moveaxis(keep, axis, 0), jnp.moveaxis(send, axis, 0).astype(BF16)


def _pack(arrs):
    pieces = []
    for a in arrs:
        flat = a.reshape(-1)
        pad = (-flat.shape[0]) % PACK
        pieces.append(jnp.pad(flat, (0, pad)).reshape(-1, LANES))
    return jnp.concatenate(pieces, axis=0)


def _unpack(packed, shapes):
    out, row = [], 0
    for s in shapes:
        size = math.prod(s)
        rows = (size + PACK - 1) // PACK * SUBLANES
        out.append(packed[row:row + rows].reshape(-1)[:size].reshape(s))
        row += rows
    return out


class NoExchange:
    def __init__(self, full):
        self.full = full

    def layer_full(self, i):
        layer = {k: v[i] for k, v in self.full.items()}
        for k in HALVED:
            rows, cols = layer[k].shape
            layer[k] = layer[k].reshape(rows, 2, cols // 2).transpose(1, 0, 2)
        return layer

    def fwd_exchange(self, i):
        return None

    def fwd_done(self, i, results):
        pass

    def bwd_exchange(self, i):
        return None

    def bwd_done(self, i, results):
        pass

    def layer_grads(self, i, grads):
        pass


def local_step(x, p, target, rep, hooks):
    batch, seq, d = x.shape
    n_tok = batch * seq
    h = x.reshape(n_tok, d)
    saved = []
    n = rms_fwd(h, rep["ffn1_norm"][0].reshape(1, d), "first_norm")
    for i in range(DEPTH):
        lw = _layer_weights(rep, hooks.layer_full(i), i)
        next_gain = rep["ffn1_norm"][i + 1].reshape(1, d) if i + 1 < DEPTH else None
        h, n, s1 = ffn_fwd(h, n, lw["ffn1_wi"], lw["ffn1_wo"], lw["mix_norm"], "ffn1")
        h, n, s2, exchanged = mix_fwd(h, n, lw, batch, lw["ffn2_norm"], "mix", hooks.fwd_exchange(i))
        hooks.fwd_done(i, exchanged)
        h, n, s3 = ffn_fwd(h, n, lw["ffn2_wi"], lw["ffn2_wo"], lw["ple_norm"], "ffn2")
        p_i = p[i].reshape(n_tok, -1)
        h, n, s4 = ple_fwd(h, n, p_i, lw["ple_w_gate"], lw["ple_w_proj"], next_gain, "ple")
        saved.append((s1, s2, s3, s4, p_i))
    loss, dh, d_final = loss_head(h, rep["final_norm"].reshape(1, d), target.reshape(n_tok, d), "loss_head")
    per_layer = [None] * DEPTH
    for i in reversed(range(DEPTH)):
        lw = _layer_weights(rep, hooks.layer_full(i), i)
        s1, s2, s3, s4, p_i = saved[i]
        g = {}
        dh, g["ple_norm"], g["ple_w_gate"], g["ple_w_proj"] = ple_bwd(dh, s4, p_i, lw["ple_norm"], lw["ple_w_gate"], "ple")
        dh, g["ffn2_norm"], g["ffn2_wi"], g["ffn2_wo"] = ffn_bwd(
            dh, s3, lw["ffn2_norm"], lw["ffn2_wi"], lw["ffn2_wo"], "ffn2")
        dh, gm, exchanged = mix_bwd(dh, s2, lw, batch, "mix", hooks.bwd_exchange(i))
        hooks.bwd_done(i, exchanged)
        g.update(gm)
        dh, g["ffn1_norm"], g["ffn1_wi"], g["ffn1_wo"] = ffn_bwd(
            dh, s1, lw["ffn1_norm"], lw["ffn1_wi"], lw["ffn1_wo"], "ffn1")
        hooks.layer_grads(i, g)
        per_layer[i] = g
    return loss, dh.reshape(batch, seq, d), per_layer, d_final


def _layer_weights(w, big, i):
    d = big["w_in"].shape[1]
    sw = d // 2
    lw = {k: big[k] for k in ("ffn1_wi", "ffn1_wo", "w_in", "w_out", "ffn2_wi", "ffn2_wo", "ple_w_gate", "ple_w_proj")}
    lw["w_glu"] = big["ssm_w_glu"]
    lw["pool_w"] = w["pool_w"][i]
    for k in ("ffn1_norm", "mix_norm", "ffn2_norm", "ple_norm"):
        lw[k] = w[k][i].reshape(1, d)
    lw["ssm_d"] = w["ssm_d"][i].reshape(1, sw)
    lw["pool_scale"] = w["pool_scale"][i].reshape(1, sw)
    lw["lam_re"] = w["ssm_lambda_re"][i].reshape(SSM_CH, 1)
    lw["lam_im"] = w["ssm_lambda_im"][i].reshape(SSM_CH, 1)
    lw["log_dt"] = jnp.repeat(w["ssm_log_dt"][i], SSM_STATE).reshape(SSM_CH, 1)
    lw["b_re"] = w["ssm_b_re"][i].reshape(SSM_CH, SSM_GROUP_CH)
    lw["b_im"] = w["ssm_b_im"][i].reshape(SSM_CH, SSM_GROUP_CH)
    lw["c_re"] = w["ssm_c_re"][i]
    lw["c_im"] = w["ssm_c_im"][i]
    return lw


class MeshExchange:
    def __init__(self, shards):
        self.shards = shards
        self.c = lax.axis_index("c")
        self.gathered = {0: run_exchange(GatherPlan(shards[0]), "gather_first_layer")}
        self.chip_sums = {}
        self.from_chips = {}

    def layer_full(self, i):
        return {k: _unshard(g, SHARDED[k], k in HALVED) for k, g in zip(SHARDED, self.gathered[i])}

    def fwd_exchange(self, i):
        return GatherPlan(self.shards[i + 1]) if i + 1 < DEPTH else None

    def fwd_done(self, i, results):
        if results:
            self.gathered[i + 1] = results

    def layer_grads(self, i, grads):
        keeps, sends = zip(*[_split_for_scatter(grads[k], SHARDED[k], self.c, k in HALVED) for k in SHARDED])
        from_sibling = swap_with_sibling(list(sends), "reduce_core_pair")
        sums = []
        for k, keep, got in zip(SHARDED, keeps, from_sibling):
            cols = keep.shape[-1]
            sums.append(add2(keep.reshape(-1, cols), got.reshape(-1, cols), f"sum_core_pair_{k}", BF16).reshape(keep.shape))
        if i == 0:
            self.from_chips[0] = run_exchange(ChipScatterPlan(sums), "reduce_chips_last_layer")
        else:
            self.chip_sums[i] = sums

    def bwd_exchange(self, i):
        return ChipScatterPlan(self.chip_sums[i + 1]) if i + 1 in self.chip_sums else None

    def bwd_done(self, i, results):
        if results:
            self.from_chips[i + 1] = results


def kernel(x, p, ffn1_norm, ffn1_wi, ffn1_wo, mix_norm, w_in, ssm_lambda_re, ssm_lambda_im, ssm_log_dt, ssm_b_re, ssm_b_im, ssm_c_re, ssm_c_im, ssm_d, ssm_w_glu, pool_w, pool_scale, w_out, ffn2_norm, ffn2_wi, ffn2_wo, ple_norm, ple_w_gate, ple_w_proj, final_norm, loss_target, m_ffn1_norm, m_ffn1_wi, m_ffn1_wo, m_mix_norm, m_w_in, m_ssm_lambda_re, m_ssm_lambda_im, m_ssm_log_dt, m_ssm_b_re, m_ssm_b_im, m_ssm_c_re, m_ssm_c_im, m_ssm_d, m_ssm_w_glu, m_pool_w, m_pool_scale, m_w_out, m_ffn2_norm, m_ffn2_wi, m_ffn2_wo, m_ple_norm, m_ple_w_gate, m_ple_w_proj, m_final_norm, v_ffn1_norm, v_ffn1_wi, v_ffn1_wo, v_mix_norm, v_w_in, v_ssm_lambda_re, v_ssm_lambda_im, v_ssm_log_dt, v_ssm_b_re, v_ssm_b_im, v_ssm_c_re, v_ssm_c_im, v_ssm_d, v_ssm_w_glu, v_pool_w, v_pool_scale, v_w_out, v_ffn2_norm, v_ffn2_wi, v_ffn2_wo, v_ple_norm, v_ple_w_gate, v_ple_w_proj, v_final_norm):
    args = dict(locals())
    wts = {k: args[k] for k in WEIGHTS}
    rep = {k: wts[k] for k in REPLICATED}

    shards = [[wts[k][i].astype(BF16) for k in SHARDED] for i in range(DEPTH)]
    exchange = MeshExchange(shards)
    loss_local, grad_x, per_layer, d_final = local_step(x, p, loss_target, rep, exchange)
    loss = lax.psum(loss_local, ("x", "y", "c"))

    outs = {}
    for j, k in enumerate(SHARDED):
        shp = wts[k].shape
        cols = shp[-1]
        parts = jnp.stack([exchange.from_chips[i][j] for i in range(DEPTH)], axis=1)
        res = adamw(wts[k].reshape(-1, cols), parts.reshape(4, -1, cols), args["m_" + k].reshape(-1, cols),
                    args["v_" + k].reshape(-1, cols), f"adamw_{k}")
        outs[k] = [r.reshape(shp) for r in res]

    rep_shapes = [wts[k].shape for k in REPLICATED]
    small = {k: jnp.stack([g[k] for g in per_layer], axis=0) for k in REPLICATED if k != "final_norm"}
    small["final_norm"] = d_final
    packed_g = _pack([small[k].reshape(wts[k].shape) for k in REPLICATED])
    all_g, = run_exchange(GatherPlan([packed_g]), "gather_small_grads")
    res = adamw(_pack([wts[k] for k in REPLICATED]), all_g, _pack([args["m_" + k] for k in REPLICATED]),
                _pack([args["v_" + k] for k in REPLICATED]), "adamw_small")
    unpacked = [_unpack(r, rep_shapes) for r in res]
    for j, k in enumerate(REPLICATED):
        outs[k] = [unpacked[q][j] for q in range(4)]

    result = [loss, grad_x]
    for q in range(4):
        result += [outs[k][q] for k in WEIGHTS]
    return tuple(result)
```

```python
import math

import jax
import jax.numpy as jnp
from jax import lax
from jax.experimental import pallas as pl
from jax.experimental.pallas import tpu as pltpu

F32 = jnp.float32
BF16 = jnp.bfloat16
MESH = pl.DeviceIdType.MESH
ANY = pl.BlockSpec(memory_space=pl.ANY)

N_DEV = 8
DEPTH = 4
EPS = 1e-6
SSM_GROUPS = 32
SSM_GROUP_CH = 16
SSM_STATE = 64
SSM_CH = SSM_GROUPS * SSM_STATE
SSM_SUPER = 2
POOL_WINDOWS = (2, 4, 8, 16)
POOL_HALO = 16
ADAM_LR, ADAM_B1, ADAM_B2, ADAM_EPS, ADAM_WD, ADAM_STEP = 0.001, 0.9, 0.999, 1e-08, 0.01, 10

V7X_VMEM_BYTES = 64 * 1024 * 1024
VMEM_LIMIT_BYTES = V7X_VMEM_BYTES - 12 * 1024 * 1024
LANES = 128
SUBLANES = 8
PACK = SUBLANES * LANES


def _params(*sem):
    return pltpu.CompilerParams(dimension_semantics=sem or None, vmem_limit_bytes=VMEM_LIMIT_BYTES)


def _tile(n, pref):
    if n <= pref:
        return n
    t = pref - pref % LANES
    while t >= LANES:
        if n % t == 0:
            return t
        t -= LANES
    raise ValueError(f"no lane-aligned tile for {n}")


def _row_tile(rows, pref):
    if rows <= pref:
        return rows
    t = pref - pref % SUBLANES
    while t >= SUBLANES:
        if rows % t == 0:
            return t
        t -= SUBLANES
    raise ValueError(f"no sublane-aligned tile for {rows}")


_DIMS = {"nn": ((1,), (0,)), "nt": ((1,), (1,)), "tn": ((0,), (0,))}


def matmul(a, b, *, mode, name, out_dtypes=None, epi=None, extras=(), separate=False, diag=1, col_sums=0,
           stack_out=False, scale=1.0, exchange=None, tm=1024, tn=1024, tk=1024):
    a_list = list(a) if isinstance(a, list) else [a]
    b_list = list(b) if isinstance(b, list) else [b]
    a_planes = [x[1] if isinstance(x, tuple) else None for x in a_list]
    b_planes = [x[1] if isinstance(x, tuple) else None for x in b_list]
    a_list = [x[0] if isinstance(x, tuple) else x for x in a_list]
    b_list = [x[0] if isinstance(x, tuple) else x for x in b_list]
    a_shape, b_shape = a_list[0].shape[-2:], b_list[0].shape[-2:]
    n_terms = max(len(a_list), len(b_list))
    a_idx = [0] * n_terms if len(a_list) == 1 else list(range(n_terms))
    b_idx = [0] * n_terms if len(b_list) == 1 else list(range(n_terms))
    n_acc = n_terms if separate else 1
    assert not (stack_out or scale != 1.0) or epi is None
    if out_dtypes is None:
        out_dtypes = (F32,) * (1 if (epi is not None or stack_out) else n_acc)
    in_place = epi is None
    if mode == "tn":
        K, M = a_shape
        K2, N = b_shape
    elif mode == "nt":
        M, K = a_shape
        N, K2 = b_shape
    else:
        M, K = a_shape
        K2, N = b_shape
    assert K == K2, (name, a_shape, b_shape)
    if mode == "tn":
        tm, tn, tk = _tile(M // diag, tm), _tile(N // diag, tn), _tile(K, tk)
        nk = K // tk
        N = N // diag
        row_tiles, col_tiles = (M // diag) // tm, N // tn
        a_blk, a_map = (tk, tm), lambda i, j, k: (k, i)
        b_blk, b_map = (tk, tn), lambda i, j, k: (k, (i // row_tiles) * col_tiles + j)
    else:
        tm, tn, tk = _tile(M, tm), _tile(N // diag, tn), _tile(K // diag, tk)
        nk = (K // diag) // tk
        col_tiles = (N // diag) // tn
        a_blk, a_map = (tm, tk), lambda i, j, k: (i, (j // col_tiles) * nk + k)
        if mode == "nt":
            b_blk, b_map = (tn, tk), lambda i, j, k: (j, (j // col_tiles) * nk + k)
        else:
            b_blk, b_map = (tk, tn), lambda i, j, k: ((j // col_tiles) * nk + k, j)

    def plane_spec(blk, index_map, plane):
        if plane is None:
            return pl.BlockSpec(blk, index_map)
        return pl.BlockSpec((None,) + blk, lambda i, j, k: (plane,) + index_map(i, j, k))

    a_specs = [plane_spec(a_blk, a_map, p_) for p_ in a_planes]
    b_specs = [plane_spec(b_blk, b_map, p_) for p_ in b_planes]
    assert not col_sums or N == tn, (name, N, tn)
    ex_specs = []
    for e in extras:
        if e.shape == (M, N):
            ex_specs.append(pl.BlockSpec((tm, tn), lambda i, j, k: (i, j)))
        elif e.shape == (1, N):
            ex_specs.append(pl.BlockSpec((1, tn), lambda i, j, k: (0, j)))
        elif e.shape == (M, 1):
            ex_specs.append(pl.BlockSpec((tm, 1), lambda i, j, k: (i, 0)))
        else:
            raise ValueError((name, e.shape, (M, N)))
    na, nb, ne, no = len(a_list), len(b_list), len(extras), len(out_dtypes)
    dims = (_DIMS[mode], ((), ()))

    n_scratch = n_acc if (nk > 1 and not in_place) else 0
    x_ins, x_out, x_sems = _exchange_args(exchange)
    grid = (M // tm, N // tn, nk)

    def body(*refs):
        a_refs, b_refs, ex_refs, xin, out_refs, sum_refs, xout, acc_refs, xsems = _split_refs(
            refs, na, nb, ne, len(x_ins), no, col_sums, len(x_out), n_scratch)
        if exchange is not None:
            step = (pl.program_id(0) * grid[1] + pl.program_id(1)) * nk + pl.program_id(2)
            _run_exchange(exchange, step, grid[0] * grid[1] * nk, (xin, xout, xsems))
        a_vals = [r[...].astype(BF16) for r in a_refs]
        b_vals = [r[...].astype(BF16) for r in b_refs]
        prods = [lax.dot_general(a_vals[a_idx[t]], b_vals[b_idx[t]], dims, preferred_element_type=F32)
                 for t in range(n_terms)]
        if not separate:
            total = prods[0]
            for p_ in prods[1:]:
                total = total + p_
            prods = [total]

        def finish(accs):
            res = epi(*accs, *[e[...] for e in ex_refs]) if epi is not None else tuple(accs)
            for r, v in zip(out_refs, res[:no]):
                r[...] = v.astype(r.dtype)
            first_rows = pl.program_id(0) == 0
            for r, v in zip(sum_refs, res[no:]):
                @pl.when(first_rows)
                def _(r=r, v=v):
                    r[...] = v

                @pl.when(jnp.logical_not(first_rows))
                def _(r=r, v=v):
                    r[...] += v

        if in_place:
            dst = [(out_refs[0], t) for t in range(n_acc)] if stack_out else [(r, None) for r in out_refs]

            def read(r, t):
                return r[...] if t is None else r[t]

            def write(r, t, v):
                if t is None:
                    r[...] = v
                else:
                    r[t] = v

            if nk == 1:
                for (r, t), v in zip(dst, prods):
                    write(r, t, v * scale if scale != 1.0 else v)
            else:
                k = pl.program_id(2)

                @pl.when(k == 0)
                def _():
                    for (r, t), v in zip(dst, prods):
                        write(r, t, v)

                @pl.when(jnp.logical_and(k > 0, k < nk - 1))
                def _():
                    for (r, t), v in zip(dst, prods):
                        write(r, t, read(r, t) + v)

                @pl.when(k == nk - 1)
                def _():
                    for (r, t), v in zip(dst, prods):
                        total = read(r, t) + v
                        write(r, t, total * scale if scale != 1.0 else total)
        elif nk == 1:
            finish(prods)
        else:
            k = pl.program_id(2)

            @pl.when(k == 0)
            def _():
                for r, v in zip(acc_refs, prods):
                    r[...] = v

            @pl.when(jnp.logical_and(k > 0, k < nk - 1))
            def _():
                for r, v in zip(acc_refs, prods):
                    r[...] += v

            @pl.when(k == nk - 1)
            def _():
                finish([r[...] + v for r, v in zip(acc_refs, prods)])

    if stack_out:
        out_specs = [pl.BlockSpec((n_acc, tm, tn), lambda i, j, k: (0, i, j))]
        out_shape = [jax.ShapeDtypeStruct((n_acc, M, N), F32)]
    else:
        out_specs = [pl.BlockSpec((tm, tn), lambda i, j, k: (i, j))] * no
        out_shape = [jax.ShapeDtypeStruct((M, N), dt) for dt in out_dtypes]
    sequential = col_sums or exchange is not None
    outs = pl.pallas_call(
        body,
        name=name,
        grid=grid,
        in_specs=a_specs + b_specs + ex_specs + [ANY] * len(x_ins),
        out_specs=out_specs + [pl.BlockSpec((1, tn), lambda i, j, k: (0, j))] * col_sums + [ANY] * len(x_out),
        out_shape=out_shape + [jax.ShapeDtypeStruct((1, N), F32)] * col_sums + x_out,
        scratch_shapes=[pltpu.VMEM((tm, tn), F32)] * n_scratch + x_sems,
        compiler_params=_params(*(("arbitrary",) * 3 if sequential else ("parallel", "parallel", "arbitrary"))),
    )(*a_list, *b_list, *extras, *x_ins)
    if exchange is not None:
        n_own = len(outs) - len(x_out)
        return list(outs[:n_own]), list(outs[n_own:])
    return outs


def rowwise(fn, ins, outs, accs=(), *, name, tr=512):
    R = max(x.shape[0] for x in ins)
    tr = _row_tile(R, tr)
    in_specs = []
    for x in ins:
        if x.shape[0] == R and x.ndim == 2:
            in_specs.append(pl.BlockSpec((tr, x.shape[1]), lambda i: (i, 0)))
        else:
            in_specs.append(pl.BlockSpec(x.shape, lambda i, _n=x.ndim: (0,) * _n))
    ni, no = len(ins), len(outs)

    def body(*refs):
        i = pl.program_id(0)
        row_vals, acc_vals = fn(*[r[...] for r in refs[:ni]])
        for r, v in zip(refs[ni:ni + no], row_vals):
            r[...] = v.astype(r.dtype)
        for r, v in zip(refs[ni + no:], acc_vals):
            @pl.when(i == 0)
            def _(r=r, v=v):
                r[...] = v

            @pl.when(i > 0)
            def _(r=r, v=v):
                r[...] += v

    return pl.pallas_call(
        body,
        name=name,
        grid=(R // tr,),
        in_specs=in_specs,
        out_specs=[pl.BlockSpec((tr, c), lambda i: (i, 0)) for c, _ in outs]
        + [pl.BlockSpec(s, lambda i: (0, 0)) for s in accs],
        out_shape=[jax.ShapeDtypeStruct((R, c), dt) for c, dt in outs]
        + [jax.ShapeDtypeStruct(s, F32) for s in accs],
        compiler_params=_params("arbitrary"),
    )(*ins)


def _sigmoid(x):
    return 1.0 / (1.0 + jnp.exp(-x))


_GELU_C = math.sqrt(2.0 / math.pi)


def _gelu(x):
    return 0.5 * x * (1.0 + jnp.tanh(_GELU_C * (x + 0.044715 * (x * x * x))))


def _gelu_grad(x):
    t = jnp.tanh(_GELU_C * (x + 0.044715 * (x * x * x)))
    return 0.5 * (1.0 + t) + 0.5 * x * (1.0 - t * t) * (_GELU_C * (1.0 + 3.0 * 0.044715 * (x * x)))


def rms_fwd(x, g, name):
    def fn(x, g):
        r = lax.rsqrt(jnp.mean(x * x, axis=-1, keepdims=True) + EPS)
        return [x * r * g], []

    return rowwise(fn, [x, g], [(x.shape[1], BF16)], name=name)[0]


def _rms_tile(x, g):
    return x * lax.rsqrt(jnp.mean(x * x, axis=-1, keepdims=True) + EPS) * g


def _rms_bwd_tile(dn, x, g, dres):
    r = lax.rsqrt(jnp.mean(x * x, axis=-1, keepdims=True) + EPS)
    w = dn * g
    dx = r * w - x * (r * r * r) * jnp.mean(x * w, axis=-1, keepdims=True)
    return dres + dx, jnp.sum(dn * (x * r), axis=0, keepdims=True)


def _whole(shape):
    return pl.BlockSpec(shape, lambda: (0,) * len(shape))


def _zoh(lr, li, ldt):
    dt = jnp.exp(ldt)
    mag = jnp.exp(lr * dt)
    ar, ai = mag * jnp.cos(li * dt), mag * jnp.sin(li * dt)
    den = lr * lr + li * li
    kr = ((ar - 1.0) * lr + ai * li) / den
    ki = (ai * lr - (ar - 1.0) * li) / den
    return dt, ar, ai, den, kr, ki


def ssm_prep(lam_re, lam_im, log_dt, b_re, b_im, name):
    n = lam_re.shape[0]

    def body(lr_ref, li_ref, ldt_ref, br_ref, bi_ref, ar_ref, ai_ref, bbr_ref, bbi_ref):
        _, ar, ai, _, kr, ki = _zoh(lr_ref[...], li_ref[...], ldt_ref[...])
        br, bi = br_ref[...], bi_ref[...]
        ar_ref[...] = ar
        ai_ref[...] = ai
        bbr_ref[...] = kr * br - ki * bi
        bbi_ref[...] = kr * bi + ki * br

    col, mat = (n, 1), (n, SSM_GROUP_CH)
    return pl.pallas_call(
        body, name=name,
        in_specs=[_whole(col)] * 3 + [_whole(mat)] * 2,
        out_specs=[_whole(col)] * 2 + [_whole(mat)] * 2,
        out_shape=[jax.ShapeDtypeStruct(col, F32)] * 2 + [jax.ShapeDtypeStruct(mat, F32)] * 2,
        compiler_params=_params(),
    )(lam_re, lam_im, log_dt, b_re, b_im)


def ssm_prep_bwd(lam_re, lam_im, log_dt, b_re, b_im, d_ar, d_ai, d_bbr, d_bbi, name):
    n = lam_re.shape[0]
    n_groups = n // SSM_STATE

    def body(lr_ref, li_ref, ldt_ref, br_ref, bi_ref, dar_ref, dai_ref, dbr_ref, dbi_ref,
             glr_ref, gli_ref, gdt_ref, gbr_ref, gbi_ref):
        lr, li = lr_ref[...], li_ref[...]
        dt, ar, ai, den, kr, ki = _zoh(lr, li, ldt_ref[...])
        br, bi, dbr, dbi = br_ref[...], bi_ref[...], dbr_ref[...], dbi_ref[...]
        gbr_ref[...] = kr * dbr + ki * dbi
        gbi_ref[...] = kr * dbi - ki * dbr
        gkr = jnp.sum(br * dbr + bi * dbi, axis=1, keepdims=True)
        gki = jnp.sum(br * dbi - bi * dbr, axis=1, keepdims=True)
        gar = dar_ref[...] + (gkr * lr - gki * li) / den
        gai = dai_ref[...] + (gki * lr + gkr * li) / den
        qr, qi = -(kr * lr + ki * li) / den, -(ki * lr - kr * li) / den
        g1r, g1i = qr * gkr + qi * gki, qr * gki - qi * gkr
        g2r, g2i = dt * (ar * gar + ai * gai), dt * (ar * gai - ai * gar)
        glr_ref[...] = g1r + g2r
        gli_ref[...] = g1i + g2i
        pr, pi_ = lr * ar - li * ai, lr * ai + li * ar
        gdt = (pr * gar + pi_ * gai) * dt
        grp = lax.broadcasted_iota(jnp.int32, (n, n_groups), 0) // SSM_STATE
        sel = grp == lax.broadcasted_iota(jnp.int32, (n, n_groups), 1)
        gdt_ref[...] = jnp.sum(jnp.where(sel, gdt, 0.0), axis=0, keepdims=True)

    col, mat = (n, 1), (n, SSM_GROUP_CH)
    return pl.pallas_call(
        body, name=name,
        in_specs=[_whole(col)] * 3 + [_whole(mat)] * 2 + [_whole(col)] * 2 + [_whole(mat)] * 2,
        out_specs=[_whole(col)] * 2 + [_whole((1, n_groups))] + [_whole(mat)] * 2,
        out_shape=[jax.ShapeDtypeStruct(col, F32)] * 2 + [jax.ShapeDtypeStruct((1, n_groups), F32)]
        + [jax.ShapeDtypeStruct(mat, F32)] * 2,
        compiler_params=_params(),
    )(lam_re, lam_im, log_dt, b_re, b_im, d_ar, d_ai, d_bbr, d_bbi)


def _cmul(ar, ai, br, bi):
    return ar * br - ai * bi, ar * bi + ai * br


def _scan_block(xr, xi, lr, li, carry_r, carry_i, or_ref, oi_ref, loc_r, loc_i, reverse):
    tb, cb = xr.shape
    ng = tb // SUBLANES
    xr = xr.reshape(ng, SUBLANES, cb)
    xi = xi.reshape(ng, SUBLANES, cb)
    rid = lax.broadcasted_iota(jnp.int32, (1, SUBLANES, cb), 1)
    pr, pi_ = lr.reshape(1, 1, cb), li.reshape(1, 1, cb)
    powers = []
    for k in (1, 2, 4):
        powers.append((pr, pi_))
        shift = SUBLANES - k if reverse else k
        sr, si = pltpu.roll(xr, shift, 1), pltpu.roll(xi, shift, 1)
        keep = (rid < SUBLANES - k) if reverse else (rid >= k)
        tr_, ti_ = _cmul(jnp.where(keep, pr, 0.0), jnp.where(keep, pi_, 0.0), sr, si)
        xr = xr + tr_
        xi = xi + ti_
        pr, pi_ = _cmul(pr, pi_, pr, pi_)
    loc_r[...] = xr
    loc_i[...] = xi
    (p1r, p1i), (p2r, p2i), (p4r, p4i) = powers
    dist = lax.broadcasted_iota(jnp.int32, (SUBLANES, cb), 0)
    if reverse:
        dist = SUBLANES - 1 - dist
    wr = jnp.broadcast_to(p1r.reshape(1, cb), (SUBLANES, cb))
    wi = jnp.broadcast_to(p1i.reshape(1, cb), (SUBLANES, cb))
    for bit, (qr, qi) in ((1, (p1r, p1i)), (2, (p2r, p2i)), (4, (p4r, p4i))):
        mr, mi = _cmul(wr, wi, qr.reshape(1, cb), qi.reshape(1, cb))
        on = (dist & bit) != 0
        wr, wi = jnp.where(on, mr, wr), jnp.where(on, mi, wi)
    last = 0 if reverse else SUBLANES - 1

    def step(j, carry):
        cr, ci = carry
        g = (ng - 1 - j) if reverse else j
        fr = loc_r[g] + (wr * cr - wi * ci)
        fi = loc_i[g] + (wr * ci + wi * cr)
        rows = pl.ds(pl.multiple_of(g * SUBLANES, SUBLANES), SUBLANES)
        or_ref[rows, :] = fr
        oi_ref[rows, :] = fi
        return fr[last:last + 1, :], fi[last:last + 1, :]

    cr, ci = lax.fori_loop(0, ng, step, (carry_r[...], carry_i[...]))
    carry_r[...] = cr
    carry_i[...] = ci


def _scan_tiles(seq_len, n_ch):
    return min(256, seq_len), min(512, n_ch)


def _run_exchange(plan, step, n_steps, refs):
    @pl.when(step == 0)
    def _():
        plan.start(*refs)

    for part, at in plan.relay_steps(n_steps):
        @pl.when(step == at)
        def _(part=part):
            plan.relay(part, *refs)

    @pl.when(step == n_steps - 1)
    def _():
        plan.finish(*refs)


def _exchange_args(plan):
    if plan is None:
        return [], [], []
    return list(plan.ins), list(plan.out_shape), list(plan.sems)


def _split_refs(refs, *counts):
    groups, at = [], 0
    for n in counts:
        groups.append(refs[at:at + n])
        at += n
    return groups + [refs[at:]]


def ssm_scan(x_re, x_im, lam_re, lam_im, batch, name, exchange=None):
    n, nch = x_re.shape
    seq = n // batch
    tb, cb = _scan_tiles(seq, nch)
    nt, nc = seq // tb, nch // cb
    ex_ins, ex_out, ex_sems = _exchange_args(exchange)

    def body(*refs):
        ins, xin, outs, xout, scratch, xsems = _split_refs(refs, 4, len(ex_ins), 2, len(ex_out), 6)
        xr_ref, xi_ref, lr_ref, li_ref = ins
        or_ref, oi_ref = outs
        car_r, car_i, loc_r, loc_i, s_r, s_i = scratch
        if exchange is not None:
            step = (pl.program_id(0) * batch + pl.program_id(1)) * nt + pl.program_id(2)
            _run_exchange(exchange, step, nc * batch * nt, (xin, xout, xsems))

        @pl.when(pl.program_id(2) == 0)
        def _():
            car_r[...] = jnp.zeros_like(car_r)
            car_i[...] = jnp.zeros_like(car_i)

        _scan_block(xr_ref[...], xi_ref[...], lr_ref[...], li_ref[...], car_r, car_i, s_r, s_i,
                    loc_r, loc_i, reverse=False)
        or_ref[...] = s_r[...].astype(or_ref.dtype)
        oi_ref[...] = s_i[...].astype(oi_ref.dtype)

    blk = pl.BlockSpec((tb, cb), lambda c, b, t: (b * nt + t, c))
    lam_spec = pl.BlockSpec((1, cb), lambda c, b, t: (0, c))
    res = pl.pallas_call(
        body, name=name,
        grid=(nc, batch, nt),
        in_specs=[blk, blk, lam_spec, lam_spec] + [ANY] * len(ex_ins),
        out_specs=[blk, blk] + [ANY] * len(ex_out),
        out_shape=[jax.ShapeDtypeStruct((n, nch), BF16)] * 2 + ex_out,
        scratch_shapes=[pltpu.VMEM((1, cb), F32)] * 2 + [pltpu.VMEM((tb // SUBLANES, SUBLANES, cb), F32)] * 2
        + [pltpu.VMEM((tb, cb), F32)] * 2 + ex_sems,
        compiler_params=_params("arbitrary", "arbitrary", "arbitrary"),
    )(x_re, x_im, lam_re, lam_im, *ex_ins)
    return res[0], res[1], list(res[2:])


def ssm_scan_bwd(d_re, d_im, s_re, s_im, lam_re, lam_im, batch, name, exchange=None):
    n, nch = d_re.shape
    seq = n // batch
    tb, cb = _scan_tiles(seq, nch)
    nt, nc = seq // tb, nch // cb
    halo_rows = 2 * SUBLANES
    hb = tb // halo_rows
    ex_ins, ex_out, ex_sems = _exchange_args(exchange)

    def body(*refs):
        ins, xin, outs, xout, scratch, xsems = _split_refs(refs, 8, len(ex_ins), 4, len(ex_out), 6)
        xr_ref, xi_ref, sr_ref, si_ref, hr_ref, hi_ref, lr_ref, li_ref = ins
        or_ref, oi_ref, dlr_ref, dli_ref = outs
        car_r, car_i, loc_r, loc_i, g_r, g_i = scratch
        b, t = pl.program_id(1), pl.program_id(2)
        if exchange is not None:
            step = (pl.program_id(0) * batch + b) * nt + t
            _run_exchange(exchange, step, nc * batch * nt, (xin, xout, xsems))

        @pl.when(t == 0)
        def _():
            car_r[...] = jnp.zeros_like(car_r)
            car_i[...] = jnp.zeros_like(car_i)

        _scan_block(xr_ref[...], xi_ref[...], lr_ref[...], -li_ref[...], car_r, car_i, g_r, g_i,
                    loc_r, loc_i, reverse=True)
        gr, gi = g_r[...], g_i[...]
        or_ref[...] = gr.astype(or_ref.dtype)
        oi_ref[...] = gi.astype(oi_ref.dtype)
        first_block = t == nt - 1
        row = lax.broadcasted_iota(jnp.int32, (tb, cb), 0)
        hr = jnp.where(first_block, 0.0, hr_ref[...].astype(F32)[halo_rows - 1:halo_rows, :])
        hi = jnp.where(first_block, 0.0, hi_ref[...].astype(F32)[halo_rows - 1:halo_rows, :])
        pr = jnp.where(row == 0, hr, pltpu.roll(sr_ref[...].astype(F32), 1, 0))
        pi_ = jnp.where(row == 0, hi, pltpu.roll(si_ref[...].astype(F32), 1, 0))
        dlr = jnp.sum(gr * pr + gi * pi_, axis=0, keepdims=True)
        dli = jnp.sum(gi * pr - gr * pi_, axis=0, keepdims=True)
        start = jnp.logical_and(b == 0, t == 0)

        @pl.when(start)
        def _():
            dlr_ref[...] = dlr
            dli_ref[...] = dli

        @pl.when(jnp.logical_not(start))
        def _():
            dlr_ref[...] += dlr
            dli_ref[...] += dli

    def blk(c, b, t):
        return b * nt + (nt - 1 - t)

    st_spec = pl.BlockSpec((tb, cb), lambda c, b, t: (blk(c, b, t), c))
    halo_spec = pl.BlockSpec((halo_rows, cb), lambda c, b, t: (jnp.maximum(blk(c, b, t) * hb - 1, 0), c))
    row_spec = pl.BlockSpec((1, cb), lambda c, b, t: (0, c))
    res = pl.pallas_call(
        body, name=name,
        grid=(nc, batch, nt),
        in_specs=[st_spec] * 4 + [halo_spec] * 2 + [row_spec] * 2 + [ANY] * len(ex_ins),
        out_specs=[st_spec, st_spec, row_spec, row_spec] + [ANY] * len(ex_out),
        out_shape=[jax.ShapeDtypeStruct((n, nch), BF16)] * 2 + [jax.ShapeDtypeStruct((1, nch), F32)] * 2 + ex_out,
        scratch_shapes=[pltpu.VMEM((1, cb), F32)] * 2 + [pltpu.VMEM((tb // SUBLANES, SUBLANES, cb), F32)] * 2
        + [pltpu.VMEM((tb, cb), F32)] * 2 + ex_sems,
        compiler_params=_params("arbitrary", "arbitrary", "arbitrary"),
    )(d_re, d_im, s_re, s_im, s_re, s_im, lam_re, lam_im, *ex_ins)
    return res[0], res[1], res[2], res[3], list(res[4:])


def _pool_tiles(seq_len):
    return min(512, seq_len)


def _window_sums(x, n_steps, forward_in_time):
    rows = x.shape[0]
    k = 1
    for _ in range(n_steps):
        x = x + pltpu.roll(x, k if forward_in_time else rows - k, 0)
        k *= 2
    return x


def pool_fwd(u, w_pool, scale, batch, name):
    n, c = u.shape
    seq = n // batch
    tb = _pool_tiles(seq)
    nt = seq // tb
    gc = c // len(POOL_WINDOWS)
    hb = tb // POOL_HALO

    def body(x_ref, halo_ref, w_ref, sc_ref, y_ref, q_ref):
        t = pl.program_id(1)
        halo = jnp.where(t == 0, 0.0, halo_ref[...])
        full = jnp.concatenate([halo, x_ref[...]], axis=0)
        pos = lax.broadcasted_iota(jnp.int32, (tb, gc), 0) + t * tb + 1
        for gi, win in enumerate(POOL_WINDOWS):
            cols = slice(gi * gc, (gi + 1) * gc)
            sums = _window_sums(full[:, cols], gi + 1, True)[POOL_HALO:, :]
            cnt = jnp.minimum(pos, win).astype(F32)
            q = sums / cnt - x_ref[:, cols]
            r = jnp.dot(q.astype(BF16), w_ref[gi].astype(BF16), preferred_element_type=F32)
            q_ref[:, cols] = q.astype(q_ref.dtype)
            y_ref[:, cols] = (r * sc_ref[:, cols]).astype(y_ref.dtype)

    return pl.pallas_call(
        body, name=name,
        grid=(batch, nt),
        in_specs=[pl.BlockSpec((tb, c), lambda b, t: (b * nt + t, 0)),
                  pl.BlockSpec((POOL_HALO, c), lambda b, t: (jnp.maximum((b * nt + t) * hb - 1, 0), 0)),
                  pl.BlockSpec(w_pool.shape, lambda b, t: (0, 0, 0)),
                  pl.BlockSpec((1, c), lambda b, t: (0, 0))],
        out_specs=[pl.BlockSpec((tb, c), lambda b, t: (b * nt + t, 0))] * 2,
        out_shape=[jax.ShapeDtypeStruct((n, c), BF16)] * 2,
        compiler_params=_params("parallel", "arbitrary"),
    )(u, u, w_pool, scale)


def pool_bwd(dy, q, w_pool, scale, batch, name):
    n, c = dy.shape
    seq = n // batch
    tb = _pool_tiles(seq)
    nt = seq // tb
    ng = len(POOL_WINDOWS)
    gc = c // ng
    hb = tb // POOL_HALO
    n_blocks = n // POOL_HALO

    def body(dy_ref, dyh_ref, q_ref, w_ref, sc_ref, du_ref, dw_ref, dsc_ref):
        b, t = pl.program_id(0), pl.program_id(1)
        last = t == nt - 1
        dy_full = jnp.concatenate([dy_ref[...], jnp.where(last, 0.0, dyh_ref[...])], axis=0)
        pos = lax.broadcasted_iota(jnp.int32, (tb + POOL_HALO, gc), 0) + t * tb + 1
        start = jnp.logical_and(b == 0, t == 0)
        for gi, win in enumerate(POOL_WINDOWS):
            cols = slice(gi * gc, (gi + 1) * gc)
            w = w_ref[gi].astype(BF16)
            dr = dy_full[:, cols] * sc_ref[:, cols]
            dq = lax.dot_general(dr.astype(BF16), w, (((1,), (1,)), ((), ())), preferred_element_type=F32)
            cnt = jnp.minimum(pos, win).astype(F32)
            back = _window_sums(dq / cnt, gi + 1, False)
            du_ref[:, cols] = back[:tb, :] - dq[:tb, :]
            qb = q_ref[:, cols]
            r = jnp.dot(qb, w, preferred_element_type=F32)
            dw = lax.dot_general(qb, dr[:tb, :].astype(BF16), (((0,), (0,)), ((), ())), preferred_element_type=F32)
            dsc = jnp.sum(dy_ref[:, cols] * r, axis=0, keepdims=True)

            @pl.when(start)
            def _(gi=gi, cols=cols, dw=dw, dsc=dsc):
                dw_ref[gi] = dw
                dsc_ref[:, cols] = dsc

            @pl.when(jnp.logical_not(start))
            def _(gi=gi, cols=cols, dw=dw, dsc=dsc):
                dw_ref[gi] += dw
                dsc_ref[:, cols] += dsc

    blk = pl.BlockSpec((tb, c), lambda b, t: (b * nt + t, 0))
    halo = pl.BlockSpec((POOL_HALO, c), lambda b, t: (jnp.minimum((b * nt + t + 1) * hb, n_blocks - 1), 0))
    return pl.pallas_call(
        body, name=name,
        grid=(batch, nt),
        in_specs=[blk, halo, blk,
                  pl.BlockSpec(w_pool.shape, lambda b, t: (0, 0, 0)),
                  pl.BlockSpec((1, c), lambda b, t: (0, 0))],
        out_specs=[blk, pl.BlockSpec(w_pool.shape, lambda b, t: (0, 0, 0)), pl.BlockSpec((1, c), lambda b, t: (0, 0))],
        out_shape=[jax.ShapeDtypeStruct((n, c), F32), jax.ShapeDtypeStruct(w_pool.shape, F32),
                   jax.ShapeDtypeStruct((1, c), F32)],
        compiler_params=_params("arbitrary", "arbitrary"),
    )(dy, dy, q, w_pool, scale)


def _place():
    return lax.axis_index("x"), lax.axis_index("y"), lax.axis_index("c")


class GatherPlan:
    def __init__(self, arrs):
        self.ins = list(arrs)
        na = len(arrs)
        self.out_shape = [jax.ShapeDtypeStruct((N_DEV,) + a.shape, a.dtype) for a in arrs]
        self.sems = [pltpu.SemaphoreType.DMA((na, 7)), pltpu.SemaphoreType.DMA((na, 7)), pltpu.SemaphoreType.DMA((na,))]
        self.sizes = [math.prod(a.shape) * a.dtype.itemsize for a in arrs]

    def relay_steps(self, n_steps):
        total, done, steps = sum(self.sizes), 0, []
        for a, size in enumerate(self.sizes):
            done += size
            steps.append((a, min(n_steps - 1, (done * (n_steps - 1)) // total)))
        return steps

    def _copy(self, outs, sems, a, k, block, to, src=None):
        dst = outs[a].at[4 * block[0] + 2 * block[1] + block[2]]
        return pltpu.make_async_remote_copy(
            src_ref=dst if src is None else src, dst_ref=dst,
            send_sem=sems[0].at[a, k], recv_sem=sems[1].at[a, k], device_id=to, device_id_type=MESH)

    @staticmethod
    def _chips(x, y):
        return [(1 - x, y), (x, 1 - y), (1 - x, 1 - y)]

    def _local(self, ins, outs, sems, a, me):
        return pltpu.make_async_copy(ins[a], outs[a].at[4 * me[0] + 2 * me[1] + me[2]], sems[2].at[a])

    def start(self, ins, outs, sems):
        x, y, c = _place()
        me = (x, y, c)
        for a in range(len(ins)):
            self._local(ins, outs, sems, a, me).start()
            self._copy(outs, sems, a, 0, me, (x, y, 1 - c), src=ins[a]).start()
            for j, chip in enumerate(self._chips(x, y)):
                self._copy(outs, sems, a, 1 + j, me, (*chip, c), src=ins[a]).start()

    def relay(self, a, ins, outs, sems):
        x, y, c = _place()
        for j, chip in enumerate(self._chips(x, y)):
            self._copy(outs, sems, a, 1 + j, (*chip, c), (x, y, c)).wait_recv()
            self._copy(outs, sems, a, 4 + j, (*chip, c), (x, y, 1 - c)).start()

    def finish(self, ins, outs, sems):
        x, y, c = _place()
        me, sibling = (x, y, c), (x, y, 1 - c)
        for a in range(len(ins)):
            self._copy(outs, sems, a, 0, sibling, me).wait_recv()
            for j, chip in enumerate(self._chips(x, y)):
                self._copy(outs, sems, a, 4 + j, (*chip, 1 - c), me).wait_recv()
        for a in range(len(ins)):
            self._copy(outs, sems, a, 0, me, sibling, src=ins[a]).wait_send()
            for j, chip in enumerate(self._chips(x, y)):
                self._copy(outs, sems, a, 1 + j, me, (*chip, c), src=ins[a]).wait_send()
                self._copy(outs, sems, a, 4 + j, (*chip, c), sibling).wait_send()
            self._local(ins, outs, sems, a, me).wait()


class ChipScatterPlan:
    def __init__(self, arrs):
        self.groups = [list(a) if isinstance(a, list) else [a] for a in arrs]
        self.ins = [piece for group in self.groups for piece in group]
        self.first = [sum(len(g) for g in self.groups[:a]) for a in range(len(self.groups))]
        na = len(arrs)
        self.out_shape = [jax.ShapeDtypeStruct((4,) + g[0].shape[1:], g[0].dtype) for g in self.groups]
        self.sems = [pltpu.SemaphoreType.DMA((na, 3)), pltpu.SemaphoreType.DMA((na, 3)), pltpu.SemaphoreType.DMA((na,))]

    def relay_steps(self, n_steps):
        return []

    def _row(self, ins, a, px, py):
        if len(self.groups[a]) == 1:
            return ins[self.first[a]].at[2 * px + py]
        return ins[self.first[a] + px].at[py]

    def start(self, ins, outs, sems):
        x, y, c = _place()
        mine = 2 * x + y
        for xs in (0, 1):
            @pl.when(x == xs)
            def _(xs=xs):
                for a in range(len(self.groups)):
                    pltpu.make_async_copy(self._row(ins, a, xs, y), outs[a].at[mine], sems[2].at[a]).start()
                    for j, (px, py) in enumerate([(1 - xs, y), (xs, 1 - y), (1 - xs, 1 - y)]):
                        pltpu.make_async_remote_copy(
                            src_ref=self._row(ins, a, px, py), dst_ref=outs[a].at[mine],
                            send_sem=sems[0].at[a, j], recv_sem=sems[1].at[a, j],
                            device_id=(px, py, c), device_id_type=MESH).start()

    def finish(self, ins, outs, sems):
        x, y, c = _place()
        for wait_recv in (True, False):
            for a in range(len(self.groups)):
                for j in range(3):
                    cp = pltpu.make_async_remote_copy(
                        src_ref=self._row(ins, a, 0, 0), dst_ref=outs[a].at[0],
                        send_sem=sems[0].at[a, j], recv_sem=sems[1].at[a, j],
                        device_id=(x, y, c), device_id_type=MESH)
                    if wait_recv:
                        cp.wait_recv()
                    else:
                        cp.wait_send()
        for a in range(len(self.groups)):
            pltpu.make_async_copy(self._row(ins, a, 0, 0), outs[a].at[0], sems[2].at[a]).wait()


def run_exchange(plan, name):
    n_in, n_out = len(plan.ins), len(plan.out_shape)

    def body(*refs):
        parts = (refs[:n_in], refs[n_in:n_in + n_out], refs[n_in + n_out:])
        plan.start(*parts)
        for part, _ in plan.relay_steps(1):
            plan.relay(part, *parts)
        plan.finish(*parts)

    return pl.pallas_call(
        body, name=name,
        in_specs=[ANY] * n_in, out_specs=[ANY] * n_out,
        out_shape=plan.out_shape, scratch_shapes=plan.sems,
    )(*plan.ins)


def swap_with_sibling(arrs, name):
    na = len(arrs)

    def body(*refs):
        ins, outs = refs[:na], refs[na:2 * na]
        send_sems, recv_sems = refs[2 * na:]
        x, y, c = _place()
        copies = [pltpu.make_async_remote_copy(
            src_ref=ins[a], dst_ref=outs[a], send_sem=send_sems.at[a], recv_sem=recv_sems.at[a],
            device_id=(x, y, 1 - c), device_id_type=MESH) for a in range(na)]
        for cp in copies:
            cp.start()
        for cp in copies:
            cp.wait()

    return pl.pallas_call(
        body, name=name,
        in_specs=[ANY] * na, out_specs=[ANY] * na,
        out_shape=[jax.ShapeDtypeStruct(a.shape, a.dtype) for a in arrs],
        scratch_shapes=[pltpu.SemaphoreType.DMA((na,)), pltpu.SemaphoreType.DMA((na,))],
    )(*arrs)


def adamw(w, gparts, m, v, name, tr=256):
    rows, cols = w.shape
    parts = gparts.shape[0]
    tr = _row_tile(rows, tr)
    c1 =1.0 - ADAM_B1 ** ADAM_STEP
    c2 = 1.0 - ADAM_B2 ** ADAM_STEP

    def body(w_ref, g_ref, m_ref, v_ref, go_ref, d_ref, mo_ref, vo_ref):
        g = g_ref[0].astype(F32)
        for p_ in range(1, parts):
            g = g + g_ref[p_].astype(F32)
        m_new = ADAM_B1 * m_ref[...] + (1.0 - ADAM_B1) * g
        v_new = ADAM_B2 * v_ref[...] + (1.0 - ADAM_B2) * (g * g)
        m_hat = m_new / c1
        v_hat = v_new / c2
        go_ref[...] = g
        d_ref[...] = -ADAM_LR * (m_hat / (jnp.sqrt(v_hat) + ADAM_EPS) + ADAM_WD * w_ref[...])
        mo_ref[...] = m_new
        vo_ref[...] = v_new

    blk = pl.BlockSpec((tr, cols), lambda i: (i, 0))
    return pl.pallas_call(
        body, name=name,
        grid=(rows // tr,),
        in_specs=[blk, pl.BlockSpec((parts, tr, cols), lambda i: (0, i, 0)), blk, blk],
        out_specs=[blk] * 4,
        out_shape=[jax.ShapeDtypeStruct((rows, cols), F32)] * 4,
        compiler_params=_params("parallel"),
    )(w, gparts, m, v)


def add2(a, b, name, out_dtype):
    return rowwise(lambda a, b: ([a.astype(F32) + b.astype(F32)], []), [a, b], [(a.shape[1], out_dtype)],
                   name=name, tr=256)[0]


def _block_diag(x):
    g, a, b = x.shape
    eye = jnp.eye(g, dtype=x.dtype)
    return (x[:, :, None, :] * eye[:, None, :, None]).reshape(g * a, g * b)


def _diag_blocks(x, a, b):
    per = x.shape[1] // b
    x5 = x.reshape(SSM_SUPER, per, a, per, b)
    eye = jnp.eye(per, dtype=x.dtype)
    return jnp.sum(x5 * eye[None, :, None, :, None], axis=3).reshape(SSM_SUPER * per, a, b)


def _swiglu_epi(g, u):
    s = _sigmoid(g)
    silu = g * s
    return u * (s * (1.0 + g * (1.0 - s))), silu, silu * u


def _residual_epi(scale, with_norm):
    if with_norm:
        def epi(acc, res, gain):
            out = res + scale * acc
            return out, _rms_tile(out, gain)
    else:
        def epi(acc, res):
            return (res + scale * acc,)
    return epi


def _next_norm(next_gain):
    if next_gain is None:
        return [], (F32,)
    return [next_gain], (F32, BF16)


FFN_WIDE = 2816


def ffn_fwd(h, n, wi, wo, next_gain, tag, exchange=None):
    res = matmul(n, [(wi, 0), (wi, 1)], mode="nn", name=f"{tag}_in", separate=True, epi=_swiglu_epi,
                 out_dtypes=(BF16, BF16, BF16), exchange=exchange, tm=512, tn=FFN_WIDE)
    (dact_g, dact_u, act), exchanged = res if exchange is not None else (res, [])
    more, dtypes = _next_norm(next_gain)
    res = matmul(act, wo, mode="nn", name=f"{tag}_out", epi=_residual_epi(0.5, bool(more)), extras=[h] + more,
                 out_dtypes=dtypes, tm=512, tk=FFN_WIDE)
    return res[0], (res[1] if more else None), (h, n, dact_g, dact_u, act), exchanged


def ffn_bwd(dh, saved, gain, wi, wo, tag, exchange=None):
    h, n, dact_g, dact_u, act = saved
    dg, du = matmul(dh, wo, mode="nt", name=f"{tag}_out_dx", extras=[dact_g, dact_u], out_dtypes=(BF16, BF16),
                    epi=lambda acc, fg, fu: (0.5 * acc * fg, 0.5 * acc * fu), tm=512, tn=FFN_WIDE)
    d_wo, = matmul(act, dh, mode="tn", name=f"{tag}_out_dw", scale=0.5, tm=FFN_WIDE)
    d_wi = [matmul(n, half, mode="tn", name=f"{tag}_in_dw", tn=FFN_WIDE)[0] for half in (dg, du)]
    res = matmul([dg, du], [(wi, 0), (wi, 1)], mode="nt", name=f"{tag}_in_dx", epi=_rms_bwd_tile,
                 extras=[h, gain, dh], out_dtypes=(F32,), col_sums=1, exchange=exchange, tm=256, tk=FFN_WIDE)
    (dh_new, d_gain), exchanged = res if exchange is not None else (res, [])
    return dh_new, d_gain, d_wi, d_wo, exchanged


def mix_fwd(h, n, lw, batch, next_gain, tag, exchange=None):
    sw = h.shape[1] // 2
    us, up = matmul(n, [lw["w_in"][:, :sw], lw["w_in"][:, sw:]], mode="nn", name=f"{tag}_in", separate=True,
                    out_dtypes=(F32, F32))
    lam_r, lam_i, bb_r, bb_i = ssm_prep(lw["lam_re"], lw["lam_im"], lw["log_dt"], lw["b_re"], lw["b_im"], f"{tag}_zoh")
    lam = (lam_r.reshape(1, SSM_CH), lam_i.reshape(1, SSM_CH))
    b_mats = [_block_diag(bb.reshape(SSM_GROUPS, SSM_STATE, SSM_GROUP_CH).transpose(0, 2, 1)).astype(BF16)
              for bb in (bb_r, bb_i)]
    c_mats = [_block_diag(cc.transpose(0, 2, 1)).astype(BF16) for cc in (lw["c_re"], -lw["c_im"])]
    bu_re, bu_im = matmul(us, b_mats, mode="nn", name=f"{tag}_bu", separate=True, diag=SSM_SUPER)
    s_re, s_im, exchanged = ssm_scan(bu_re, bu_im, *lam, batch, f"{tag}_scan", exchange)
    y0, y1 = matmul([s_re, s_im], c_mats, mode="nn", name=f"{tag}_c", diag=SSM_SUPER,
                    epi=lambda acc, u, d: (acc + d * u, _gelu(acc + d * u)), extras=[us, lw["ssm_d"]],
                    out_dtypes=(F32, BF16))
    y2, gl = matmul(y1, lw["w_glu"], mode="nn", name=f"{tag}_glu",
                    epi=lambda acc, y0: (_gelu(y0) * _sigmoid(acc), acc), extras=[y0], out_dtypes=(BF16, F32))
    yp, q = pool_fwd(up, lw["pool_w"], lw["pool_scale"], batch, f"{tag}_pool")
    more, dtypes = _next_norm(next_gain)
    res = matmul([y2, yp], [lw["w_out"][:sw], lw["w_out"][sw:]], mode="nn", name=f"{tag}_out",
                 epi=_residual_epi(1.0, bool(more)), extras=[h] + more, out_dtypes=dtypes)
    saved = (h, n, us, lam, b_mats, c_mats, s_re, s_im, y0, y1, gl, y2, yp, q)
    return res[0], (res[1] if more else None), saved, exchanged


def mix_bwd(dh, saved, lw, batch, tag, exchange=None):
    h, n, us, lam, b_mats, c_mats, s_re, s_im, y0, y1, gl, y2, yp, q = saved
    sw = h.shape[1] // 2
    w_out_s, w_out_p = lw["w_out"][:sw], lw["w_out"][sw:]
    d_wo_s, d_wo_p = matmul([y2, yp], dh, mode="tn", name=f"{tag}_out_dw", separate=True)
    def out_dx_epi(dy2, dyp, y0, gl):
        sg = _sigmoid(gl)
        return dy2, dyp, dy2 * _gelu(y0) * sg * (1.0 - sg)

    dy2, dyp, tg = matmul(dh, [w_out_s, w_out_p], mode="nt", name=f"{tag}_out_dx", separate=True, epi=out_dx_epi,
                          extras=[y0, gl], out_dtypes=(F32, F32, BF16))
    dup, d_pool_w, d_pool_scale = pool_bwd(dyp, q, lw["pool_w"], lw["pool_scale"], batch, f"{tag}_pool_bwd")
    def dy0_epi(acc, dy2, gl, y0, u):
        dy0 = (acc + dy2 * _sigmoid(gl)) * _gelu_grad(y0)
        return dy0, jnp.sum(dy0 * u, axis=0, keepdims=True)

    dy0, d_d = matmul(tg, lw["w_glu"], mode="nt", name=f"{tag}_glu_dx", epi=dy0_epi, extras=[dy2, gl, y0, us],
                      out_dtypes=(F32,), col_sums=1)
    d_w_glu, = matmul(y1, tg, mode="tn", name=f"{tag}_glu_dw")
    gd_re, gd_im = matmul(dy0, c_mats, mode="nt", name=f"{tag}_c_dx", separate=True, diag=SSM_SUPER)
    d_c_top, d_c_bot = matmul([s_re, s_im], dy0, mode="tn", name=f"{tag}_c_dw", separate=True, diag=SSM_SUPER)
    g_re, g_im, d_lam_r, d_lam_i, exchanged = ssm_scan_bwd(gd_re, gd_im, s_re, s_im, *lam, batch, f"{tag}_scan_bwd",
                                                           exchange)
    dus, = matmul([g_re, g_im], b_mats, mode="nt", name=f"{tag}_bu_dx", diag=SSM_SUPER,
                  epi=lambda acc, dy0, d: (acc + d * dy0,), extras=[dy0, lw["ssm_d"]])
    d_b_re, d_b_im = matmul(us, [g_re, g_im], mode="tn", name=f"{tag}_bu_dw", separate=True, diag=SSM_SUPER)
    d_bb_r = _diag_blocks(d_b_re, SSM_GROUP_CH, SSM_STATE).transpose(0, 2, 1).reshape(SSM_CH, SSM_GROUP_CH)
    d_bb_i = _diag_blocks(d_b_im, SSM_GROUP_CH, SSM_STATE).transpose(0, 2, 1).reshape(SSM_CH, SSM_GROUP_CH)
    d_lr, d_li, d_ldt, d_br, d_bi = ssm_prep_bwd(
        lw["lam_re"], lw["lam_im"], lw["log_dt"], lw["b_re"], lw["b_im"],
        d_lam_r.reshape(SSM_CH, 1), d_lam_i.reshape(SSM_CH, 1), d_bb_r, d_bb_i, f"{tag}_zoh_bwd")
    d_c_re = _diag_blocks(d_c_top, SSM_STATE, SSM_GROUP_CH).transpose(0, 2, 1)
    d_c_im = -_diag_blocks(d_c_bot, SSM_STATE, SSM_GROUP_CH).transpose(0, 2, 1)
    d_w_in_s, d_w_in_p = matmul(n, [dus, dup], mode="tn", name=f"{tag}_in_dw", separate=True)
    dh_new, d_gain = matmul([dus, dup], [lw["w_in"][:, :sw], lw["w_in"][:, sw:]], mode="nt", name=f"{tag}_in_dx",
                            epi=_rms_bwd_tile, extras=[h, lw["mix_norm"], dh], out_dtypes=(F32,), col_sums=1, tm=512)
    grads = dict(mix_norm=d_gain, w_in=jnp.concatenate([d_w_in_s, d_w_in_p], axis=1),
                 ssm_lambda_re=d_lr, ssm_lambda_im=d_li, ssm_log_dt=d_ldt, ssm_b_re=d_br, ssm_b_im=d_bi,
                 ssm_c_re=d_c_re, ssm_c_im=d_c_im, ssm_d=d_d, ssm_w_glu=d_w_glu, pool_w=d_pool_w,
                 pool_scale=d_pool_scale, w_out=jnp.concatenate([d_wo_s, d_wo_p], axis=0))
    return dh_new, grads, exchanged


def ple_fwd(h, n, p, w_gate, w_proj, next_gain, tag):
    e, = matmul(p, w_proj, mode="nn", name=f"{tag}_proj")
    if next_gain is None:
        def epi(acc, e, res):
            return res + _sigmoid(acc) * e, acc
        more, dtypes = [], (F32, F32)
    else:
        def epi(acc, e, res, gain):
            out = res + _sigmoid(acc) * e
            return out, acc, _rms_tile(out, gain)
        more, dtypes = [next_gain], (F32, F32, BF16)
    res = matmul(n, w_gate, mode="nn", name=f"{tag}_gate", epi=epi, extras=[e, h] + more, out_dtypes=dtypes, tm=512)
    return res[0], (res[2] if more else None), (h, n, e, res[1])


def ple_bwd(dh, saved, p, gain, w_gate, tag):
    h, n, e, pre = saved
    d = h.shape[1]

    def fn(dh, e, pre):
        s = _sigmoid(pre)
        return [dh * e * s * (1.0 - s), dh * s], []

    dpre, de = rowwise(fn, [dh, e, pre], [(d, BF16), (d, BF16)], name=f"{tag}_gate_bwd")
    d_w_gate, = matmul(n, dpre, mode="tn", name=f"{tag}_gate_dw")
    d_w_proj, = matmul(p, de, mode="tn", name=f"{tag}_proj_dw")
    dh_new, d_gain = matmul(dpre, w_gate, mode="nt", name=f"{tag}_gate_dx", epi=_rms_bwd_tile,
                            extras=[h, gain, dh], out_dtypes=(F32,), col_sums=1, tm=512)
    return dh_new, d_gain, d_w_gate, d_w_proj


def loss_head(h, gain, target, name):
    d = h.shape[1]

    def fn(h, g, t):
        r = lax.rsqrt(jnp.mean(h * h, axis=-1, keepdims=True) + EPS)
        diff = h * r * g - t
        sq = jnp.sum(jnp.sum(diff * diff, axis=1, keepdims=True), axis=0, keepdims=True)
        dy = diff * (1.0 / d)
        w = dy * g
        dh = r * w - h * (r * r * r) * jnp.mean(h * w, axis=-1, keepdims=True)
        return [dh], [sq, jnp.sum(dy * (h * r), axis=0, keepdims=True)]

    dh, sq, d_gain = rowwise(fn, [h, gain, target], [(d, F32)], [(1, 1), (1, d)], name=name)
    return 0.5 / d * sq[0, 0], dh, d_gain


SHARDED = {
    "ffn1_wi": 1, "ffn1_wo": 0, "w_in": 0, "ssm_w_glu": 0, "w_out": 0, "ffn2_wi": 1, "ffn2_wo": 0,
    "ple_w_gate": 0, "ple_w_proj": 1,
}
WEIGHTS = ["ffn1_norm", "ffn1_wi", "ffn1_wo", "mix_norm", "w_in", "ssm_lambda_re", "ssm_lambda_im", "ssm_log_dt",
           "ssm_b_re", "ssm_b_im", "ssm_c_re", "ssm_c_im", "ssm_d", "ssm_w_glu", "pool_w", "pool_scale", "w_out",
           "ffn2_norm", "ffn2_wi", "ffn2_wo", "ple_norm", "ple_w_gate", "ple_w_proj", "final_norm"]
REPLICATED = [n for n in WEIGHTS if n not in SHARDED]


HALVED = ("ffn1_wi", "ffn2_wi")


def _unshard(gathered, axis, halved):
    if halved:
        _, rows, cols = gathered.shape
        return gathered.reshape(2, 4, rows, cols).transpose(0, 2, 1, 3).reshape(2, rows, 4 * cols)
    g = jnp.moveaxis(gathered, 0, axis)
    shp = g.shape
    return g.reshape(shp[:axis] + (shp[axis] * shp[axis + 1],) + shp[axis + 2:])


def _split_for_scatter(full, axis, c, halved):
    if halved:
        rows, cols = full[0].shape

        def pick(cc):
            return [lax.dynamic_index_in_dim(h.reshape(rows, 2, 2, cols // 4), cc, 2, keepdims=False).transpose(1, 0, 2)
                    for h in full]

        return pick(c), [s.astype(BF16) for s in pick(1 - c)]
    shp = full.shape
    g = full.reshape(shp[:axis] + (4, 2, shp[axis] // N_DEV) + shp[axis + 1:])
    keep = lax.dynamic_index_in_dim(g, c, axis + 1, keepdims=False)
    send = lax.dynamic_index_in_dim(g, 1 - c, axis + 1, keepdims=False)
    return jnp.moveaxis(keep, axis, 0), jnp.moveaxis(send, axis, 0).astype(BF16)


def _pack(arrs):
    pieces = []
    for a in arrs:
        flat = a.reshape(-1)
        pad = (-flat.shape[0]) % PACK
        pieces.append(jnp.pad(flat, (0, pad)).reshape(-1, LANES))
    return jnp.concatenate(pieces, axis=0)


def _unpack(packed, shapes):
    out, row = [], 0
    for s in shapes:
        size = math.prod(s)
        rows = (size + PACK - 1) // PACK * SUBLANES
        out.append(packed[row:row + rows].reshape(-1)[:size].reshape(s))
        row += rows
    return out


class NoExchange:
    def __init__(self, full):
        self.full = full

    def big(self, i, name):
        w = self.full[name][i]
        if name in HALVED:
            rows, cols = w.shape
            w = w.reshape(rows, 2, cols // 2).transpose(1, 0, 2)
        return w

    def fwd_exchange(self, i, where):
        return None

    def fwd_done(self, i, where, results):
        pass

    def bwd_exchange(self, i, where):
        return None

    def bwd_done(self, i, where, results):
        pass

    def layer_grads(self, i, grads, names):
        pass


def local_step(x, p, target, rep, hooks):
    batch, seq, d = x.shape
    n_tok = batch * seq
    h = x.reshape(n_tok, d)
    saved = []
    n = rms_fwd(h, rep["ffn1_norm"][0].reshape(1, d), "first_norm")
    for i in range(DEPTH):
        lw = _layer_weights(rep, i, d)
        big = lambda name, i=i: hooks.big(i, name)
        next_gain = rep["ffn1_norm"][i + 1].reshape(1, d) if i + 1 < DEPTH else None
        h, n, s1, exchanged = ffn_fwd(h, n, big("ffn1_wi"), big("ffn1_wo"), lw["mix_norm"], "ffn1",
                                      hooks.fwd_exchange(i, "ffn1_in"))
        hooks.fwd_done(i, "ffn1_in", exchanged)
        lw.update(w_in=big("w_in"), w_glu=big("ssm_w_glu"), w_out=big("w_out"))
        h, n, s2, exchanged = mix_fwd(h, n, lw, batch, lw["ffn2_norm"], "mix", hooks.fwd_exchange(i, "scan"))
        hooks.fwd_done(i, "scan", exchanged)
        h, n, s3, exchanged = ffn_fwd(h, n, big("ffn2_wi"), big("ffn2_wo"), lw["ple_norm"], "ffn2",
                                      hooks.fwd_exchange(i, "ffn2_in"))
        hooks.fwd_done(i, "ffn2_in", exchanged)
        p_i = p[i].reshape(n_tok, -1)
        h, n, s4 = ple_fwd(h, n, p_i, big("ple_w_gate"), big("ple_w_proj"), next_gain, "ple")
        saved.append((s1, s2, s3, s4, p_i))
    loss, dh, d_final = loss_head(h, rep["final_norm"].reshape(1, d), target.reshape(n_tok, d), "loss_head")
    per_layer = [None] * DEPTH
    for i in reversed(range(DEPTH)):
        lw = _layer_weights(rep, i, d)
        big = lambda name, i=i: hooks.big(i, name)
        lw.update(w_in=big("w_in"), w_glu=big("ssm_w_glu"), w_out=big("w_out"))
        s1, s2, s3, s4, p_i = saved[i]
        g = {}
        dh, g["ple_norm"], g["ple_w_gate"], g["ple_w_proj"] = ple_bwd(dh, s4, p_i, lw["ple_norm"], big("ple_w_gate"), "ple")
        dh, g["ffn2_norm"], g["ffn2_wi"], g["ffn2_wo"], exchanged = ffn_bwd(
            dh, s3, lw["ffn2_norm"], big("ffn2_wi"), big("ffn2_wo"), "ffn2", hooks.bwd_exchange(i, "ffn2_in_dx"))
        hooks.bwd_done(i, "ffn2_in_dx", exchanged)
        dh, gm, exchanged = mix_bwd(dh, s2, lw, batch, "mix", hooks.bwd_exchange(i, "scan"))
        hooks.bwd_done(i, "scan", exchanged)
        g.update(gm)
        hooks.layer_grads(i, g, [k for k in SHARDED if k not in LAST_GRADS])
        dh, g["ffn1_norm"], g["ffn1_wi"], g["ffn1_wo"], exchanged = ffn_bwd(
            dh, s1, lw["ffn1_norm"], big("ffn1_wi"), big("ffn1_wo"), "ffn1", hooks.bwd_exchange(i, "ffn1_in_dx"))
        hooks.bwd_done(i, "ffn1_in_dx", exchanged)
        hooks.layer_grads(i, g, list(LAST_GRADS))
        per_layer[i] = g
    return loss, dh.reshape(batch, seq, d), per_layer, d_final


LAST_GRADS = ("ffn1_wi", "ffn1_wo")


def _layer_weights(w, i, d):
    sw = d // 2
    lw = {}
    lw["pool_w"] = w["pool_w"][i]
    for k in ("ffn1_norm", "mix_norm", "ffn2_norm", "ple_norm"):
        lw[k] = w[k][i].reshape(1, d)
    lw["ssm_d"] = w["ssm_d"][i].reshape(1, sw)
    lw["pool_scale"] = w["pool_scale"][i].reshape(1, sw)
    lw["lam_re"] = w["ssm_lambda_re"][i].reshape(SSM_CH, 1)
    lw["lam_im"] = w["ssm_lambda_im"][i].reshape(SSM_CH, 1)
    lw["log_dt"] = jnp.repeat(w["ssm_log_dt"][i], SSM_STATE).reshape(SSM_CH, 1)
    lw["b_re"] = w["ssm_b_re"][i].reshape(SSM_CH, SSM_GROUP_CH)
    lw["b_im"] = w["ssm_b_im"][i].reshape(SSM_CH, SSM_GROUP_CH)
    lw["c_re"] = w["ssm_c_re"][i]
    lw["c_im"] = w["ssm_c_im"][i]
    return lw


class MeshExchange:
    FIRST = ("ffn1_wi", "ffn1_wo")
    FWD_PLAN = {"scan": ("ffn1_wi", "ffn1_wo", "w_in", "ssm_w_glu", "w_out"),
                "ffn2_in": ("ffn2_wi", "ffn2_wo", "ple_w_gate", "ple_w_proj")}
    BWD_PLAN = {"ffn2_in_dx": ("ffn1_wo", "ffn2_wo", "ssm_w_glu", "w_out", "ple_w_gate", "ple_w_proj"),
                "scan": ("ffn1_wi", "ffn2_wi", "w_in")}

    def __init__(self, shards):
        self.shards = shards
        self.c = lax.axis_index("c")
        self.gathered = {}
        self.chip_sums = {}
        self.from_chips = {}
        self.pending = None
        first = run_exchange(GatherPlan([shards[0][k] for k in self.FIRST]), "gather_first_weights")
        self._store(self.gathered, 0, self.FIRST, first)

    @staticmethod
    def _store(where, layer, names, results):
        for k, r in zip(names, results):
            where[(layer, k)] = r

    def big(self, i, name):
        return _unshard(self.gathered[(i, name)], SHARDED[name], name in HALVED)

    def fwd_exchange(self, i, where):
        if i == 0 and where == "ffn1_in":
            layer, names = 0, [k for k in SHARDED if k not in self.FIRST]
        elif where in self.FWD_PLAN and i + 1 < DEPTH:
            layer, names = i + 1, self.FWD_PLAN[where]
        else:
            return None
        self.pending = (layer, names)
        return GatherPlan([self.shards[layer][k] for k in names])

    def fwd_done(self, i, where, results):
        if results:
            self._store(self.gathered, *self.pending, results)

    def layer_grads(self, i, grads, names):
        pieces, sends = [], []
        for k in names:
            keep, send = _split_for_scatter(grads[k], SHARDED[k], self.c, k in HALVED)
            keep, send = (keep, send) if isinstance(keep, list) else ([keep], [send])
            pieces.append(keep)
            sends += send
        got = iter(swap_with_sibling(sends, "reduce_core_pair"))
        for k, keep in zip(names, pieces):
            sums = []
            for part in keep:
                cols = part.shape[-1]
                sums.append(add2(part.reshape(-1, cols), next(got).reshape(-1, cols), f"sum_core_pair_{k}",
                                 BF16).reshape(part.shape))
            self.chip_sums[(i, k)] = sums if len(sums) > 1 else sums[0]
        if i == 0 and tuple(names) == LAST_GRADS:
            last = run_exchange(ChipScatterPlan([self.chip_sums[(0, k)] for k in names]), "reduce_chips_last")
            self._store(self.from_chips, 0, names, last)

    def bwd_exchange(self, i, where):
        if where in self.BWD_PLAN and i + 1 < DEPTH:
            layer, names = i + 1, self.BWD_PLAN[where]
        elif i == 0 and where == "ffn1_in_dx":
            layer, names = 0, [k for k in SHARDED if k not in LAST_GRADS]
        else:
            return None
        self.pending = (layer, names)
        return ChipScatterPlan([self.chip_sums[(layer, k)] for k in names])

    def bwd_done(self, i, where, results):
        if results:
            self._store(self.from_chips, *self.pending, results)


def kernel(x, p, ffn1_norm, ffn1_wi, ffn1_wo, mix_norm, w_in, ssm_lambda_re, ssm_lambda_im, ssm_log_dt, ssm_b_re, ssm_b_im, ssm_c_re, ssm_c_im, ssm_d, ssm_w_glu, pool_w, pool_scale, w_out, ffn2_norm, ffn2_wi, ffn2_wo, ple_norm, ple_w_gate, ple_w_proj, final_norm, loss_target, m_ffn1_norm, m_ffn1_wi, m_ffn1_wo, m_mix_norm, m_w_in, m_ssm_lambda_re, m_ssm_lambda_im, m_ssm_log_dt, m_ssm_b_re, m_ssm_b_im, m_ssm_c_re, m_ssm_c_im, m_ssm_d, m_ssm_w_glu, m_pool_w, m_pool_scale, m_w_out, m_ffn2_norm, m_ffn2_wi, m_ffn2_wo, m_ple_norm, m_ple_w_gate, m_ple_w_proj, m_final_norm, v_ffn1_norm, v_ffn1_wi, v_ffn1_wo, v_mix_norm, v_w_in, v_ssm_lambda_re, v_ssm_lambda_im, v_ssm_log_dt, v_ssm_b_re, v_ssm_b_im, v_ssm_c_re, v_ssm_c_im, v_ssm_d, v_ssm_w_glu, v_pool_w, v_pool_scale, v_w_out, v_ffn2_norm, v_ffn2_wi, v_ffn2_wo, v_ple_norm, v_ple_w_gate, v_ple_w_proj, v_final_norm):
    args = dict(locals())
    wts = {k: args[k] for k in WEIGHTS}
    rep = {k: wts[k] for k in REPLICATED}

    shards = [{k: wts[k][i].astype(BF16) for k in SHARDED} for i in range(DEPTH)]
    exchange = MeshExchange(shards)
    loss_local, grad_x, per_layer, d_final = local_step(x, p, loss_target, rep, exchange)
    loss = lax.psum(loss_local, ("x", "y", "c"))

    outs = {}
    for k in SHARDED:
        shp = wts[k].shape
        cols = shp[-1]
        parts = jnp.stack([exchange.from_chips[(i, k)] for i in range(DEPTH)], axis=1)
        res = adamw(wts[k].reshape(-1, cols), parts.reshape(4, -1, cols), args["m_" + k].reshape(-1, cols),
                    args["v_" + k].reshape(-1, cols), f"adamw_{k}")
        outs[k] = [r.reshape(shp) for r in res]

    rep_shapes = [wts[k].shape for k in REPLICATED]
    small = {k: jnp.stack([g[k] for g in per_layer], axis=0) for k in REPLICATED if k != "final_norm"}
    small["final_norm"] = d_final
    packed_g = _pack([small[k].reshape(wts[k].shape) for k in REPLICATED])
    all_g, = run_exchange(GatherPlan([packed_g]), "gather_small_grads")
    res = adamw(_pack([wts[k] for k in REPLICATED]), all_g, _pack([args["m_" + k] for k in REPLICATED]),
                _pack([args["v_" + k] for k in REPLICATED]), "adamw_small")
    unpacked = [_unpack(r, rep_shapes) for r in res]
    for j, k in enumerate(REPLICATED):
        outs[k] = [unpacked[q][j] for q in range(4)]

    result = [loss, grad_x]
    for q in range(4):
        result += [outs[k][q] for k in WEIGHTS]
    return tuple(result)
```

```python
import math

import jax
import jax.numpy as jnp
from jax import lax
from jax.experimental import pallas as pl
from jax.experimental.pallas import tpu as pltpu

F32 = jnp.float32
BF16 = jnp.bfloat16
MESH = pl.DeviceIdType.MESH
ANY = pl.BlockSpec(memory_space=pl.ANY)

N_DEV = 8
DEPTH = 4
EPS = 1e-6
SSM_GROUPS = 32
SSM_GROUP_CH = 16
SSM_STATE = 64
SSM_CH = SSM_GROUPS * SSM_STATE
SSM_SUPER = 2
POOL_WINDOWS = (2, 4, 8, 16)
POOL_HALO = 16
ADAM_LR, ADAM_B1, ADAM_B2, ADAM_EPS, ADAM_WD, ADAM_STEP = 0.001, 0.9, 0.999, 1e-08, 0.01, 10

V7X_VMEM_BYTES = 64 * 1024 * 1024
VMEM_LIMIT_BYTES = V7X_VMEM_BYTES - 12 * 1024 * 1024
LANES = 128
SUBLANES = 8
PACK = SUBLANES * LANES


def _params(*sem):
    return pltpu.CompilerParams(dimension_semantics=sem or None, vmem_limit_bytes=VMEM_LIMIT_BYTES)


def _tile(n, pref):
    if n <= pref:
        return n
    t = pref - pref % LANES
    while t >= LANES:
        if n % t == 0:
            return t
        t -= LANES
    raise ValueError(f"no lane-aligned tile for {n}")


def _row_tile(rows, pref):
    if rows <= pref:
        return rows
    t = pref - pref % SUBLANES
    while t >= SUBLANES:
        if rows % t == 0:
            return t
        t -= SUBLANES
    raise ValueError(f"no sublane-aligned tile for {rows}")


_DIMS = {"nn": ((1,), (0,)), "nt": ((1,), (1,)), "tn": ((0,), (0,))}


def matmul(a, b, *, mode, name, out_dtypes=None, epi=None, extras=(), separate=False, diag=1, col_sums=0,
           stack_out=False, scale=1.0, exchange=None, tm=1024, tn=1024, tk=1024):
    a_list = list(a) if isinstance(a, list) else [a]
    b_list = list(b) if isinstance(b, list) else [b]
    a_planes = [x[1] if isinstance(x, tuple) else None for x in a_list]
    b_planes = [x[1] if isinstance(x, tuple) else None for x in b_list]
    a_list = [x[0] if isinstance(x, tuple) else x for x in a_list]
    b_list = [x[0] if isinstance(x, tuple) else x for x in b_list]
    a_shape, b_shape = a_list[0].shape[-2:], b_list[0].shape[-2:]
    n_terms = max(len(a_list), len(b_list))
    a_idx = [0] * n_terms if len(a_list) == 1 else list(range(n_terms))
    b_idx = [0] * n_terms if len(b_list) == 1 else list(range(n_terms))
    n_acc = n_terms if separate else 1
    assert not (stack_out or scale != 1.0) or epi is None
    if out_dtypes is None:
        out_dtypes = (F32,) * (1 if (epi is not None or stack_out) else n_acc)
    in_place = epi is None
    if mode == "tn":
        K, M = a_shape
        K2, N = b_shape
    elif mode == "nt":
        M, K = a_shape
        N, K2 = b_shape
    else:
        M, K = a_shape
        K2, N = b_shape
    assert K == K2, (name, a_shape, b_shape)
    if mode == "tn":
        tm, tn, tk = _tile(M // diag, tm), _tile(N // diag, tn), _tile(K, tk)
        nk = K // tk
        N = N // diag
        row_tiles, col_tiles = (M // diag) // tm, N // tn
        a_blk, a_map = (tk, tm), lambda i, j, k: (k, i)
        b_blk, b_map = (tk, tn), lambda i, j, k: (k, (i // row_tiles) * col_tiles + j)
    else:
        tm, tn, tk = _tile(M, tm), _tile(N // diag, tn), _tile(K // diag, tk)
        nk = (K // diag) // tk
        col_tiles = (N // diag) // tn
        a_blk, a_map = (tm, tk), lambda i, j, k: (i, (j // col_tiles) * nk + k)
        if mode == "nt":
            b_blk, b_map = (tn, tk), lambda i, j, k: (j, (j // col_tiles) * nk + k)
        else:
            b_blk, b_map = (tk, tn), lambda i, j, k: ((j // col_tiles) * nk + k, j)

    def plane_spec(blk, index_map, plane):
        if plane is None:
            return pl.BlockSpec(blk, index_map)
        return pl.BlockSpec((None,) + blk, lambda i, j, k: (plane,) + index_map(i, j, k))

    a_specs = [plane_spec(a_blk, a_map, p_) for p_ in a_planes]
    b_specs = [plane_spec(b_blk, b_map, p_) for p_ in b_planes]
    assert not col_sums or N == tn, (name, N, tn)
    ex_specs = []
    for e in extras:
        if e.shape == (M, N):
            ex_specs.append(pl.BlockSpec((tm, tn), lambda i, j, k: (i, j)))
        elif e.shape == (1, N):
            ex_specs.append(pl.BlockSpec((1, tn), lambda i, j, k: (0, j)))
        elif e.shape == (M, 1):
            ex_specs.append(pl.BlockSpec((tm, 1), lambda i, j, k: (i, 0)))
        else:
            raise ValueError((name, e.shape, (M, N)))
    na, nb, ne, no = len(a_list), len(b_list), len(extras), len(out_dtypes)
    dims = (_DIMS[mode], ((), ()))

    n_scratch = n_acc if (nk > 1 and not in_place) else 0
    x_ins, x_out, x_sems = _exchange_args(exchange)
    grid = (M // tm, N // tn, nk)

    def body(*refs):
        a_refs, b_refs, ex_refs, xin, out_refs, sum_refs, xout, acc_refs, xsems = _split_refs(
            refs, na, nb, ne, len(x_ins), no, col_sums, len(x_out), n_scratch)
        if exchange is not None:
            step = (pl.program_id(0) * grid[1] + pl.program_id(1)) * nk + pl.program_id(2)
            _run_exchange(exchange, step, grid[0] * grid[1] * nk, (xin, xout, xsems))
        a_vals = [r[...].astype(BF16) for r in a_refs]
        b_vals = [r[...].astype(BF16) for r in b_refs]
        prods = [lax.dot_general(a_vals[a_idx[t]], b_vals[b_idx[t]], dims, preferred_element_type=F32)
                 for t in range(n_terms)]
        if not separate:
            total = prods[0]
            for p_ in prods[1:]:
                total = total + p_
            prods = [total]

        def finish(accs):
            res = epi(*accs, *[e[...] for e in ex_refs]) if epi is not None else tuple(accs)
            for r, v in zip(out_refs, res[:no]):
                r[...] = v.astype(r.dtype)
            first_rows = pl.program_id(0) == 0
            for r, v in zip(sum_refs, res[no:]):
                @pl.when(first_rows)
                def _(r=r, v=v):
                    r[...] = v

                @pl.when(jnp.logical_not(first_rows))
                def _(r=r, v=v):
                    r[...] += v

        if in_place:
            dst = [(out_refs[0], t) for t in range(n_acc)] if stack_out else [(r, None) for r in out_refs]

            def read(r, t):
                return r[...] if t is None else r[t]

            assert nk == 1 or all(dt == F32 for dt in out_dtypes), name

            def write(r, t, v):
                if t is None:
                    r[...] = v.astype(r.dtype)
                else:
                    r[t] = v.astype(r.dtype)

            if nk == 1:
                for (r, t), v in zip(dst, prods):
                    write(r, t, v * scale if scale != 1.0 else v)
            else:
                k = pl.program_id(2)

                @pl.when(k == 0)
                def _():
                    for (r, t), v in zip(dst, prods):
                        write(r, t, v)

                @pl.when(jnp.logical_and(k > 0, k < nk - 1))
                def _():
                    for (r, t), v in zip(dst, prods):
                        write(r, t, read(r, t) + v)

                @pl.when(k == nk - 1)
                def _():
                    for (r, t), v in zip(dst, prods):
                        total = read(r, t) + v
                        write(r, t, total * scale if scale != 1.0 else total)
        elif nk == 1:
            finish(prods)
        else:
            k = pl.program_id(2)

            @pl.when(k == 0)
            def _():
                for r, v in zip(acc_refs, prods):
                    r[...] = v

            @pl.when(jnp.logical_and(k > 0, k < nk - 1))
            def _():
                for r, v in zip(acc_refs, prods):
                    r[...] += v

            @pl.when(k == nk - 1)
            def _():
                finish([r[...] + v for r, v in zip(acc_refs, prods)])

    if stack_out:
        out_specs = [pl.BlockSpec((n_acc, tm, tn), lambda i, j, k: (0, i, j))]
        out_shape = [jax.ShapeDtypeStruct((n_acc, M, N), F32)]
    else:
        out_specs = [pl.BlockSpec((tm, tn), lambda i, j, k: (i, j))] * no
        out_shape = [jax.ShapeDtypeStruct((M, N), dt) for dt in out_dtypes]
    sequential = col_sums or exchange is not None
    outs = pl.pallas_call(
        body,
        name=name,
        grid=grid,
        in_specs=a_specs + b_specs + ex_specs + [ANY] * len(x_ins),
        out_specs=out_specs + [pl.BlockSpec((1, tn), lambda i, j, k: (0, j))] * col_sums + [ANY] * len(x_out),
        out_shape=out_shape + [jax.ShapeDtypeStruct((1, N), F32)] * col_sums + x_out,
        scratch_shapes=[pltpu.VMEM((tm, tn), F32)] * n_scratch + x_sems,
        compiler_params=_params(*(("arbitrary",) * 3 if sequential else ("parallel", "parallel", "arbitrary"))),
    )(*a_list, *b_list, *extras, *x_ins)
    if exchange is not None:
        n_own = len(outs) - len(x_out)
        return list(outs[:n_own]), list(outs[n_own:])
    return outs


def rowwise(fn, ins, outs, accs=(), *, name, tr=512):
    R = max(x.shape[0] for x in ins)
    tr = _row_tile(R, tr)
    in_specs = []
    for x in ins:
        if x.shape[0] == R and x.ndim == 2:
            in_specs.append(pl.BlockSpec((tr, x.shape[1]), lambda i: (i, 0)))
        else:
            in_specs.append(pl.BlockSpec(x.shape, lambda i, _n=x.ndim: (0,) * _n))
    ni, no = len(ins), len(outs)

    def body(*refs):
        i = pl.program_id(0)
        row_vals, acc_vals = fn(*[r[...] for r in refs[:ni]])
        for r, v in zip(refs[ni:ni + no], row_vals):
            r[...] = v.astype(r.dtype)
        for r, v in zip(refs[ni + no:], acc_vals):
            @pl.when(i == 0)
            def _(r=r, v=v):
                r[...] = v

            @pl.when(i > 0)
            def _(r=r, v=v):
                r[...] += v

    return pl.pallas_call(
        body,
        name=name,
        grid=(R // tr,),
        in_specs=in_specs,
        out_specs=[pl.BlockSpec((tr, c), lambda i: (i, 0)) for c, _ in outs]
        + [pl.BlockSpec(s, lambda i: (0, 0)) for s in accs],
        out_shape=[jax.ShapeDtypeStruct((R, c), dt) for c, dt in outs]
        + [jax.ShapeDtypeStruct(s, F32) for s in accs],
        compiler_params=_params("arbitrary"),
    )(*ins)


def _sigmoid(x):
    return 1.0 / (1.0 + jnp.exp(-x))


_GELU_C = math.sqrt(2.0 / math.pi)


def _gelu(x):
    return 0.5 * x * (1.0 + jnp.tanh(_GELU_C * (x + 0.044715 * (x * x * x))))


def _gelu_grad(x):
    t = jnp.tanh(_GELU_C * (x + 0.044715 * (x * x * x)))
    return 0.5 * (1.0 + t) + 0.5 * x * (1.0 - t * t) * (_GELU_C * (1.0 + 3.0 * 0.044715 * (x * x)))


def rms_fwd(x, g, name):
    def fn(x, g):
        r = lax.rsqrt(jnp.mean(x * x, axis=-1, keepdims=True) + EPS)
        return [x * r * g], []

    return rowwise(fn, [x, g], [(x.shape[1], BF16)], name=name)[0]


def _rms_tile(x, g):
    return x * lax.rsqrt(jnp.mean(x * x, axis=-1, keepdims=True) + EPS) * g


def _rms_bwd_tile(dn, x, g, dres):
    r = lax.rsqrt(jnp.mean(x * x, axis=-1, keepdims=True) + EPS)
    w = dn * g
    dx = r * w - x * (r * r * r) * jnp.mean(x * w, axis=-1, keepdims=True)
    return dres + dx, jnp.sum(dn * (x * r), axis=0, keepdims=True)


def _whole(shape):
    return pl.BlockSpec(shape, lambda: (0,) * len(shape))


def _zoh(lr, li, ldt):
    dt = jnp.exp(ldt)
    mag = jnp.exp(lr * dt)
    ar, ai = mag * jnp.cos(li * dt), mag * jnp.sin(li * dt)
    den = lr * lr + li * li
    kr = ((ar - 1.0) * lr + ai * li) / den
    ki = (ai * lr - (ar - 1.0) * li) / den
    return dt, ar, ai, den, kr, ki


def ssm_prep(lam_re, lam_im, log_dt, b_re, b_im, name):
    n = lam_re.shape[0]

    def body(lr_ref, li_ref, ldt_ref, br_ref, bi_ref, ar_ref, ai_ref, bbr_ref, bbi_ref):
        _, ar, ai, _, kr, ki = _zoh(lr_ref[...], li_ref[...], ldt_ref[...])
        br, bi = br_ref[...], bi_ref[...]
        ar_ref[...] = ar
        ai_ref[...] = ai
        bbr_ref[...] = kr * br - ki * bi
        bbi_ref[...] = kr * bi + ki * br

    col, mat = (n, 1), (n, SSM_GROUP_CH)
    return pl.pallas_call(
        body, name=name,
        in_specs=[_whole(col)] * 3 + [_whole(mat)] * 2,
        out_specs=[_whole(col)] * 2 + [_whole(mat)] * 2,
        out_shape=[jax.ShapeDtypeStruct(col, F32)] * 2 + [jax.ShapeDtypeStruct(mat, F32)] * 2,
        compiler_params=_params(),
    )(lam_re, lam_im, log_dt, b_re, b_im)


def ssm_prep_bwd(lam_re, lam_im, log_dt, b_re, b_im, d_ar, d_ai, d_bbr, d_bbi, name):
    n = lam_re.shape[0]
    n_groups = n // SSM_STATE

    def body(lr_ref, li_ref, ldt_ref, br_ref, bi_ref, dar_ref, dai_ref, dbr_ref, dbi_ref,
             glr_ref, gli_ref, gdt_ref, gbr_ref, gbi_ref):
        lr, li = lr_ref[...], li_ref[...]
        dt, ar, ai, den, kr, ki = _zoh(lr, li, ldt_ref[...])
        br, bi, dbr, dbi = br_ref[...], bi_ref[...], dbr_ref[...], dbi_ref[...]
        gbr_ref[...] = kr * dbr + ki * dbi
        gbi_ref[...] = kr * dbi - ki * dbr
        gkr = jnp.sum(br * dbr + bi * dbi, axis=1, keepdims=True)
        gki = jnp.sum(br * dbi - bi * dbr, axis=1, keepdims=True)
        gar = dar_ref[...] + (gkr * lr - gki * li) / den
        gai = dai_ref[...] + (gki * lr + gkr * li) / den
        qr, qi = -(kr * lr + ki * li) / den, -(ki * lr - kr * li) / den
        g1r, g1i = qr * gkr + qi * gki, qr * gki - qi * gkr
        g2r, g2i = dt * (ar * gar + ai * gai), dt * (ar * gai - ai * gar)
        glr_ref[...] = g1r + g2r
        gli_ref[...] = g1i + g2i
        pr, pi_ = lr * ar - li * ai, lr * ai + li * ar
        gdt = (pr * gar + pi_ * gai) * dt
        grp = lax.broadcasted_iota(jnp.int32, (n, n_groups), 0) // SSM_STATE
        sel = grp == lax.broadcasted_iota(jnp.int32, (n, n_groups), 1)
        gdt_ref[...] = jnp.sum(jnp.where(sel, gdt, 0.0), axis=0, keepdims=True)

    col, mat = (n, 1), (n, SSM_GROUP_CH)
    return pl.pallas_call(
        body, name=name,
        in_specs=[_whole(col)] * 3 + [_whole(mat)] * 2 + [_whole(col)] * 2 + [_whole(mat)] * 2,
        out_specs=[_whole(col)] * 2 + [_whole((1, n_groups))] + [_whole(mat)] * 2,
        out_shape=[jax.ShapeDtypeStruct(col, F32)] * 2 + [jax.ShapeDtypeStruct((1, n_groups), F32)]
        + [jax.ShapeDtypeStruct(mat, F32)] * 2,
        compiler_params=_params(),
    )(lam_re, lam_im, log_dt, b_re, b_im, d_ar, d_ai, d_bbr, d_bbi)


def _cmul(ar, ai, br, bi):
    return ar * br - ai * bi, ar * bi + ai * br


def _scan_block(xr, xi, lr, li, carry_r, carry_i, or_ref, oi_ref, loc_r, loc_i, reverse):
    tb, cb = xr.shape
    ng = tb // SUBLANES
    xr = xr.reshape(ng, SUBLANES, cb)
    xi = xi.reshape(ng, SUBLANES, cb)
    rid = lax.broadcasted_iota(jnp.int32, (1, SUBLANES, cb), 1)
    pr, pi_ = lr.reshape(1, 1, cb), li.reshape(1, 1, cb)
    powers = []
    for k in (1, 2, 4):
        powers.append((pr, pi_))
        shift = SUBLANES - k if reverse else k
        sr, si = pltpu.roll(xr, shift, 1), pltpu.roll(xi, shift, 1)
        keep = (rid < SUBLANES - k) if reverse else (rid >= k)
        tr_, ti_ = _cmul(jnp.where(keep, pr, 0.0), jnp.where(keep, pi_, 0.0), sr, si)
        xr = xr + tr_
        xi = xi + ti_
        pr, pi_ = _cmul(pr, pi_, pr, pi_)
    loc_r[...] = xr
    loc_i[...] = xi
    (p1r, p1i), (p2r, p2i), (p4r, p4i) = powers
    dist = lax.broadcasted_iota(jnp.int32, (SUBLANES, cb), 0)
    if reverse:
        dist = SUBLANES - 1 - dist
    wr = jnp.broadcast_to(p1r.reshape(1, cb), (SUBLANES, cb))
    wi = jnp.broadcast_to(p1i.reshape(1, cb), (SUBLANES, cb))
    for bit, (qr, qi) in ((1, (p1r, p1i)), (2, (p2r, p2i)), (4, (p4r, p4i))):
        mr, mi = _cmul(wr, wi, qr.reshape(1, cb), qi.reshape(1, cb))
        on = (dist & bit) != 0
        wr, wi = jnp.where(on, mr, wr), jnp.where(on, mi, wi)
    last = 0 if reverse else SUBLANES - 1

    def step(j, carry):
        cr, ci = carry
        g = (ng - 1 - j) if reverse else j
        fr = loc_r[g] + (wr * cr - wi * ci)
        fi = loc_i[g] + (wr * ci + wi * cr)
        rows = pl.ds(pl.multiple_of(g * SUBLANES, SUBLANES), SUBLANES)
        or_ref[rows, :] = fr
        oi_ref[rows, :] = fi
        return fr[last:last + 1, :], fi[last:last + 1, :]

    cr, ci = lax.fori_loop(0, ng, step, (carry_r[...], carry_i[...]))
    carry_r[...] = cr
    carry_i[...] = ci


def _scan_tiles(seq_len, n_ch):
    return min(256, seq_len), min(512, n_ch)


def _run_exchange(plan, step, n_steps, refs):
    @pl.when(step == 0)
    def _():
        plan.start(*refs)

    for part, at in plan.relay_steps(n_steps):
        @pl.when(step == at)
        def _(part=part):
            plan.relay(part, *refs)

    @pl.when(step == n_steps - 1)
    def _():
        plan.finish(*refs)


def _exchange_args(plan):
    if plan is None:
        return [], [], []
    return list(plan.ins), list(plan.out_shape), list(plan.sems)


def _split_refs(refs, *counts):
    groups, at = [], 0
    for n in counts:
        groups.append(refs[at:at + n])
        at += n
    return groups + [refs[at:]]


def ssm_scan(x_re, x_im, lam_re, lam_im, batch, name, exchange=None):
    n, nch = x_re.shape
    seq = n // batch
    tb, cb = _scan_tiles(seq, nch)
    nt, nc = seq // tb, nch // cb
    ex_ins, ex_out, ex_sems = _exchange_args(exchange)

    def body(*refs):
        ins, xin, outs, xout, scratch, xsems = _split_refs(refs, 4, len(ex_ins), 2, len(ex_out), 6)
        xr_ref, xi_ref, lr_ref, li_ref = ins
        or_ref, oi_ref = outs
        car_r, car_i, loc_r, loc_i, s_r, s_i = scratch
        if exchange is not None:
            step = (pl.program_id(0) * batch + pl.program_id(1)) * nt + pl.program_id(2)
            _run_exchange(exchange, step, nc * batch * nt, (xin, xout, xsems))

        @pl.when(pl.program_id(2) == 0)
        def _():
            car_r[...] = jnp.zeros_like(car_r)
            car_i[...] = jnp.zeros_like(car_i)

        _scan_block(xr_ref[...].astype(F32), xi_ref[...].astype(F32), lr_ref[...], li_ref[...], car_r, car_i,
                    s_r, s_i, loc_r, loc_i, reverse=False)
        or_ref[...] = s_r[...].astype(or_ref.dtype)
        oi_ref[...] = s_i[...].astype(oi_ref.dtype)

    blk = pl.BlockSpec((tb, cb), lambda c, b, t: (b * nt + t, c))
    lam_spec = pl.BlockSpec((1, cb), lambda c, b, t: (0, c))
    res = pl.pallas_call(
        body, name=name,
        grid=(nc, batch, nt),
        in_specs=[blk, blk, lam_spec, lam_spec] + [ANY] * len(ex_ins),
        out_specs=[blk, blk] + [ANY] * len(ex_out),
        out_shape=[jax.ShapeDtypeStruct((n, nch), BF16)] * 2 + ex_out,
        scratch_shapes=[pltpu.VMEM((1, cb), F32)] * 2 + [pltpu.VMEM((tb // SUBLANES, SUBLANES, cb), F32)] * 2
        + [pltpu.VMEM((tb, cb), F32)] * 2 + ex_sems,
        compiler_params=_params("arbitrary", "arbitrary", "arbitrary"),
    )(x_re, x_im, lam_re, lam_im, *ex_ins)
    return res[0], res[1], list(res[2:])


def ssm_scan_bwd(d_re, d_im, s_re, s_im, lam_re, lam_im, batch, name, exchange=None):
    n, nch = d_re.shape
    seq = n // batch
    tb, cb = _scan_tiles(seq, nch)
    nt, nc = seq // tb, nch // cb
    halo_rows = 2 * SUBLANES
    hb = tb // halo_rows
    ex_ins, ex_out, ex_sems = _exchange_args(exchange)

    def body(*refs):
        ins, xin, outs, xout, scratch, xsems = _split_refs(refs, 8, len(ex_ins), 4, len(ex_out), 6)
        xr_ref, xi_ref, sr_ref, si_ref, hr_ref, hi_ref, lr_ref, li_ref = ins
        or_ref, oi_ref, dlr_ref, dli_ref = outs
        car_r, car_i, loc_r, loc_i, g_r, g_i = scratch
        b, t = pl.program_id(1), pl.program_id(2)
        if exchange is not None:
            step = (pl.program_id(0) * batch + b) * nt + t
            _run_exchange(exchange, step, nc * batch * nt, (xin, xout, xsems))

        @pl.when(t == 0)
        def _():
            car_r[...] = jnp.zeros_like(car_r)
            car_i[...] = jnp.zeros_like(car_i)

        _scan_block(xr_ref[...].astype(F32), xi_ref[...].astype(F32), lr_ref[...], -li_ref[...], car_r, car_i,
                    g_r, g_i, loc_r, loc_i, reverse=True)
        gr, gi = g_r[...], g_i[...]
        or_ref[...] = gr.astype(or_ref.dtype)
        oi_ref[...] = gi.astype(oi_ref.dtype)
        first_block = t == nt - 1
        row = lax.broadcasted_iota(jnp.int32, (tb, cb), 0)
        hr = jnp.where(first_block, 0.0, hr_ref[...].astype(F32)[halo_rows - 1:halo_rows, :])
        hi = jnp.where(first_block, 0.0, hi_ref[...].astype(F32)[halo_rows - 1:halo_rows, :])
        pr = jnp.where(row == 0, hr, pltpu.roll(sr_ref[...].astype(F32), 1, 0))
        pi_ = jnp.where(row == 0, hi, pltpu.roll(si_ref[...].astype(F32), 1, 0))
        dlr = jnp.sum(gr * pr + gi * pi_, axis=0, keepdims=True)
        dli = jnp.sum(gi * pr - gr * pi_, axis=0, keepdims=True)
        start = jnp.logical_and(b == 0, t == 0)

        @pl.when(start)
        def _():
            dlr_ref[...] = dlr
            dli_ref[...] = dli

        @pl.when(jnp.logical_not(start))
        def _():
            dlr_ref[...] += dlr
            dli_ref[...] += dli

    def blk(c, b, t):
        return b * nt + (nt - 1 - t)

    st_spec = pl.BlockSpec((tb, cb), lambda c, b, t: (blk(c, b, t), c))
    halo_spec = pl.BlockSpec((halo_rows, cb), lambda c, b, t: (jnp.maximum(blk(c, b, t) * hb - 1, 0), c))
    row_spec = pl.BlockSpec((1, cb), lambda c, b, t: (0, c))
    res = pl.pallas_call(
        body, name=name,
        grid=(nc, batch, nt),
        in_specs=[st_spec] * 4 + [halo_spec] * 2 + [row_spec] * 2 + [ANY] * len(ex_ins),
        out_specs=[st_spec, st_spec, row_spec, row_spec] + [ANY] * len(ex_out),
        out_shape=[jax.ShapeDtypeStruct((n, nch), BF16)] * 2 + [jax.ShapeDtypeStruct((1, nch), F32)] * 2 + ex_out,
        scratch_shapes=[pltpu.VMEM((1, cb), F32)] * 2 + [pltpu.VMEM((tb // SUBLANES, SUBLANES, cb), F32)] * 2
        + [pltpu.VMEM((tb, cb), F32)] * 2 + ex_sems,
        compiler_params=_params("arbitrary", "arbitrary", "arbitrary"),
    )(d_re, d_im, s_re, s_im, s_re, s_im, lam_re, lam_im, *ex_ins)
    return res[0], res[1], res[2], res[3], list(res[4:])


def _pool_tiles(seq_len):
    return min(512, seq_len)


def _window_sums(x, n_steps, forward_in_time):
    rows = x.shape[0]
    k = 1
    for _ in range(n_steps):
        x = x + pltpu.roll(x, k if forward_in_time else rows - k, 0)
        k *= 2
    return x


def pool_fwd(u, w_pool, scale, batch, name):
    n, c = u.shape
    seq = n // batch
    tb = _pool_tiles(seq)
    nt = seq // tb
    gc = c // len(POOL_WINDOWS)
    hb = tb // POOL_HALO

    def body(x_ref, halo_ref, w_ref, sc_ref, y_ref, q_ref):
        t = pl.program_id(1)
        halo = jnp.where(t == 0, 0.0, halo_ref[...])
        full = jnp.concatenate([halo, x_ref[...]], axis=0)
        pos = lax.broadcasted_iota(jnp.int32, (tb, gc), 0) + t * tb + 1
        for gi, win in enumerate(POOL_WINDOWS):
            cols = slice(gi * gc, (gi + 1) * gc)
            sums = _window_sums(full[:, cols], gi + 1, True)[POOL_HALO:, :]
            cnt = jnp.minimum(pos, win).astype(F32)
            q = sums / cnt - x_ref[:, cols]
            r = jnp.dot(q.astype(BF16), w_ref[gi].astype(BF16), preferred_element_type=F32)
            q_ref[:, cols] = q.astype(q_ref.dtype)
            y_ref[:, cols] = (r * sc_ref[:, cols]).astype(y_ref.dtype)

    return pl.pallas_call(
        body, name=name,
        grid=(batch, nt),
        in_specs=[pl.BlockSpec((tb, c), lambda b, t: (b * nt + t, 0)),
                  pl.BlockSpec((POOL_HALO, c), lambda b, t: (jnp.maximum((b * nt + t) * hb - 1, 0), 0)),
                  pl.BlockSpec(w_pool.shape, lambda b, t: (0, 0, 0)),
                  pl.BlockSpec((1, c), lambda b, t: (0, 0))],
        out_specs=[pl.BlockSpec((tb, c), lambda b, t: (b * nt + t, 0))] * 2,
        out_shape=[jax.ShapeDtypeStruct((n, c), BF16)] * 2,
        compiler_params=_params("parallel", "arbitrary"),
    )(u, u, w_pool, scale)


def pool_bwd(dy, q, w_pool, scale, batch, name):
    n, c = dy.shape
    seq = n // batch
    tb = _pool_tiles(seq)
    nt = seq // tb
    ng = len(POOL_WINDOWS)
    gc = c // ng
    hb = tb // POOL_HALO
    n_blocks = n // POOL_HALO

    def body(dy_ref, dyh_ref, q_ref, w_ref, sc_ref, du_ref, dw_ref, dsc_ref):
        b, t = pl.program_id(0), pl.program_id(1)
        last = t == nt - 1
        dy_full = jnp.concatenate([dy_ref[...], jnp.where(last, 0.0, dyh_ref[...])], axis=0)
        pos = lax.broadcasted_iota(jnp.int32, (tb + POOL_HALO, gc), 0) + t * tb + 1
        start = jnp.logical_and(b == 0, t == 0)
        for gi, win in enumerate(POOL_WINDOWS):
            cols = slice(gi * gc, (gi + 1) * gc)
            w = w_ref[gi].astype(BF16)
            dr = dy_full[:, cols] * sc_ref[:, cols]
            dq = lax.dot_general(dr.astype(BF16), w, (((1,), (1,)), ((), ())), preferred_element_type=F32)
            cnt = jnp.minimum(pos, win).astype(F32)
            back = _window_sums(dq / cnt, gi + 1, False)
            du_ref[:, cols] = back[:tb, :] - dq[:tb, :]
            qb = q_ref[:, cols]
            r = jnp.dot(qb, w, preferred_element_type=F32)
            dw = lax.dot_general(qb, dr[:tb, :].astype(BF16), (((0,), (0,)), ((), ())), preferred_element_type=F32)
            dsc = jnp.sum(dy_ref[:, cols] * r, axis=0, keepdims=True)

            @pl.when(start)
            def _(gi=gi, cols=cols, dw=dw, dsc=dsc):
                dw_ref[gi] = dw
                dsc_ref[:, cols] = dsc

            @pl.when(jnp.logical_not(start))
            def _(gi=gi, cols=cols, dw=dw, dsc=dsc):
                dw_ref[gi] += dw
                dsc_ref[:, cols] += dsc

    blk = pl.BlockSpec((tb, c), lambda b, t: (b * nt + t, 0))
    halo = pl.BlockSpec((POOL_HALO, c), lambda b, t: (jnp.minimum((b * nt + t + 1) * hb, n_blocks - 1), 0))
    return pl.pallas_call(
        body, name=name,
        grid=(batch, nt),
        in_specs=[blk, halo, blk,
                  pl.BlockSpec(w_pool.shape, lambda b, t: (0, 0, 0)),
                  pl.BlockSpec((1, c), lambda b, t: (0, 0))],
        out_specs=[blk, pl.BlockSpec(w_pool.shape, lambda b, t: (0, 0, 0)), pl.BlockSpec((1, c), lambda b, t: (0, 0))],
        out_shape=[jax.ShapeDtypeStruct((n, c), F32), jax.ShapeDtypeStruct(w_pool.shape, F32),
                   jax.ShapeDtypeStruct((1, c), F32)],
        compiler_params=_params("arbitrary", "arbitrary"),
    )(dy, dy, q, w_pool, scale)


def _place():
    return lax.axis_index("x"), lax.axis_index("y"), lax.axis_index("c")


class GatherPlan:
    def __init__(self, arrs):
        self.ins = list(arrs)
        na = len(arrs)
        self.out_shape = [jax.ShapeDtypeStruct((N_DEV,) + a.shape, a.dtype) for a in arrs]
        self.sems = [pltpu.SemaphoreType.DMA((na, 7)), pltpu.SemaphoreType.DMA((na, 7)), pltpu.SemaphoreType.DMA((na,))]
        self.sizes = [math.prod(a.shape) * a.dtype.itemsize for a in arrs]

    def relay_steps(self, n_steps):
        total, done, steps = sum(self.sizes), 0, []
        for a, size in enumerate(self.sizes):
            done += size
            steps.append((a, min(n_steps - 1, (done * (n_steps - 1)) // total)))
        return steps

    def _copy(self, outs, sems, a, k, block, to, src=None):
        dst = outs[a].at[4 * block[0] + 2 * block[1] + block[2]]
        return pltpu.make_async_remote_copy(
            src_ref=dst if src is None else src, dst_ref=dst,
            send_sem=sems[0].at[a, k], recv_sem=sems[1].at[a, k], device_id=to, device_id_type=MESH)

    @staticmethod
    def _chips(x, y):
        return [(1 - x, y), (x, 1 - y), (1 - x, 1 - y)]

    def _local(self, ins, outs, sems, a, me):
        return pltpu.make_async_copy(ins[a], outs[a].at[4 * me[0] + 2 * me[1] + me[2]], sems[2].at[a])

    def start(self, ins, outs, sems):
        x, y, c = _place()
        me = (x, y, c)
        for a in range(len(ins)):
            self._local(ins, outs, sems, a, me).start()
            self._copy(outs, sems, a, 0, me, (x, y, 1 - c), src=ins[a]).start()
            for j, chip in enumerate(self._chips(x, y)):
                self._copy(outs, sems, a, 1 + j, me, (*chip, c), src=ins[a]).start()

    def relay(self, a, ins, outs, sems):
        x, y, c = _place()
        for j, chip in enumerate(self._chips(x, y)):
            self._copy(outs, sems, a, 1 + j, (*chip, c), (x, y, c)).wait_recv()
            self._copy(outs, sems, a, 4 + j, (*chip, c), (x, y, 1 - c)).start()

    def finish(self, ins, outs, sems):
        x, y, c = _place()
        me, sibling = (x, y, c), (x, y, 1 - c)
        for a in range(len(ins)):
            self._copy(outs, sems, a, 0, sibling, me).wait_recv()
            for j, chip in enumerate(self._chips(x, y)):
                self._copy(outs, sems, a, 4 + j, (*chip, 1 - c), me).wait_recv()
        for a in range(len(ins)):
            self._copy(outs, sems, a, 0, me, sibling, src=ins[a]).wait_send()
            for j, chip in enumerate(self._chips(x, y)):
                self._copy(outs, sems, a, 1 + j, me, (*chip, c), src=ins[a]).wait_send()
                self._copy(outs, sems, a, 4 + j, (*chip, c), sibling).wait_send()
            self._local(ins, outs, sems, a, me).wait()


class ChipScatterPlan:
    def __init__(self, arrs):
        self.groups = [list(a) if isinstance(a, list) else [a] for a in arrs]
        self.ins = [piece for group in self.groups for piece in group]
        self.first = [sum(len(g) for g in self.groups[:a]) for a in range(len(self.groups))]
        na = len(arrs)
        self.out_shape = [jax.ShapeDtypeStruct((4,) + g[0].shape[1:], g[0].dtype) for g in self.groups]
        self.sems = [pltpu.SemaphoreType.DMA((na, 3)), pltpu.SemaphoreType.DMA((na, 3)), pltpu.SemaphoreType.DMA((na,))]

    def relay_steps(self, n_steps):
        return []

    def _row(self, ins, a, px, py):
        if len(self.groups[a]) == 1:
            return ins[self.first[a]].at[2 * px + py]
        return ins[self.first[a] + px].at[py]

    def start(self, ins, outs, sems):
        x, y, c = _place()
        mine = 2 * x + y
        for xs in (0, 1):
            @pl.when(x == xs)
            def _(xs=xs):
                for a in range(len(self.groups)):
                    pltpu.make_async_copy(self._row(ins, a, xs, y), outs[a].at[mine], sems[2].at[a]).start()
                    for j, (px, py) in enumerate([(1 - xs, y), (xs, 1 - y), (1 - xs, 1 - y)]):
                        pltpu.make_async_remote_copy(
                            src_ref=self._row(ins, a, px, py), dst_ref=outs[a].at[mine],
                            send_sem=sems[0].at[a, j], recv_sem=sems[1].at[a, j],
                            device_id=(px, py, c), device_id_type=MESH).start()

    def finish(self, ins, outs, sems):
        x, y, c = _place()
        for wait_recv in (True, False):
            for a in range(len(self.groups)):
                for j in range(3):
                    cp = pltpu.make_async_remote_copy(
                        src_ref=self._row(ins, a, 0, 0), dst_ref=outs[a].at[0],
                        send_sem=sems[0].at[a, j], recv_sem=sems[1].at[a, j],
                        device_id=(x, y, c), device_id_type=MESH)
                    if wait_recv:
                        cp.wait_recv()
                    else:
                        cp.wait_send()
        for a in range(len(self.groups)):
            pltpu.make_async_copy(self._row(ins, a, 0, 0), outs[a].at[0], sems[2].at[a]).wait()


def run_exchange(plan, name):
    n_in, n_out = len(plan.ins), len(plan.out_shape)

    def body(*refs):
        parts = (refs[:n_in], refs[n_in:n_in + n_out], refs[n_in + n_out:])
        plan.start(*parts)
        for part, _ in plan.relay_steps(1):
            plan.relay(part, *parts)
        plan.finish(*parts)

    return pl.pallas_call(
        body, name=name,
        in_specs=[ANY] * n_in, out_specs=[ANY] * n_out,
        out_shape=plan.out_shape, scratch_shapes=plan.sems,
    )(*plan.ins)


def swap_with_sibling(arrs, name):
    na = len(arrs)

    def body(*refs):
        ins, outs = refs[:na], refs[na:2 * na]
        send_sems, recv_sems = refs[2 * na:]
        x, y, c = _place()
        copies = [pltpu.make_async_remote_copy(
            src_ref=ins[a], dst_ref=outs[a], send_sem=send_sems.at[a], recv_sem=recv_sems.at[a],
            device_id=(x, y, 1 - c), device_id_type=MESH) for a in range(na)]
        for cp in copies:
            cp.start()
        for cp in copies:
            cp.wait()

    return pl.pallas_call(
        body, name=name,
        in_specs=[ANY] * na, out_specs=[ANY] * na,
        out_shape=[jax.ShapeDtypeStruct(a.shape, a.dtype) for a in arrs],
        scratch_shapes=[pltpu.SemaphoreType.DMA((na,)), pltpu.SemaphoreType.DMA((na,))],
    )(*arrs)


def adamw(w, gparts, m, v, name, tr=256):
    rows, cols = w.shape
    parts = gparts.shape[0]
    tr = _row_tile(rows, tr)
    c1 =1.0 - ADAM_B1 ** ADAM_STEP
    c2 = 1.0 - ADAM_B2 ** ADAM_STEP

    def body(w_ref, g_ref, m_ref, v_ref, go_ref, d_ref, mo_ref, vo_ref):
        g = g_ref[0].astype(F32)
        for p_ in range(1, parts):
            g = g + g_ref[p_].astype(F32)
        m_new = ADAM_B1 * m_ref[...] + (1.0 - ADAM_B1) * g
        v_new = ADAM_B2 * v_ref[...] + (1.0 - ADAM_B2) * (g * g)
        m_hat = m_new / c1
        v_hat = v_new / c2
        go_ref[...] = g
        d_ref[...] = -ADAM_LR * (m_hat / (jnp.sqrt(v_hat) + ADAM_EPS) + ADAM_WD * w_ref[...])
        mo_ref[...] = m_new
        vo_ref[...] = v_new

    blk = pl.BlockSpec((tr, cols), lambda i: (i, 0))
    return pl.pallas_call(
        body, name=name,
        grid=(rows // tr,),
        in_specs=[blk, pl.BlockSpec((parts, tr, cols), lambda i: (0, i, 0)), blk, blk],
        out_specs=[blk] * 4,
        out_shape=[jax.ShapeDtypeStruct((rows, cols), F32)] * 4,
        compiler_params=_params("parallel"),
    )(w, gparts, m, v)


def add2(a, b, name, out_dtype):
    return rowwise(lambda a, b: ([a.astype(F32) + b.astype(F32)], []), [a, b], [(a.shape[1], out_dtype)],
                   name=name, tr=256)[0]


def _block_diag(x):
    g, a, b = x.shape
    eye = jnp.eye(g, dtype=x.dtype)
    return (x[:, :, None, :] * eye[:, None, :, None]).reshape(g * a, g * b)


def _diag_blocks(x, a, b):
    per = x.shape[1] // b
    x5 = x.reshape(SSM_SUPER, per, a, per, b)
    eye = jnp.eye(per, dtype=x.dtype)
    return jnp.sum(x5 * eye[None, :, None, :, None], axis=3).reshape(SSM_SUPER * per, a, b)


def _swiglu_epi(g, u):
    s = _sigmoid(g)
    silu = g * s
    return u * (s * (1.0 + g * (1.0 - s))), silu, silu * u


def _residual_epi(scale, with_norm):
    if with_norm:
        def epi(acc, res, gain):
            out = res + scale * acc
            return out, _rms_tile(out, gain)
    else:
        def epi(acc, res):
            return (res + scale * acc,)
    return epi


def _next_norm(next_gain):
    if next_gain is None:
        return [], (F32,)
    return [next_gain], (F32, BF16)


FFN_WIDE = 2816


def ffn_fwd(h, n, wi, wo, next_gain, tag, exchange=None):
    res = matmul(n, [(wi, 0), (wi, 1)], mode="nn", name=f"{tag}_in", separate=True, epi=_swiglu_epi,
                 out_dtypes=(BF16, BF16, BF16), exchange=exchange, tm=512, tn=FFN_WIDE)
    (dact_g, dact_u, act), exchanged = res if exchange is not None else (res, [])
    more, dtypes = _next_norm(next_gain)
    res = matmul(act, wo, mode="nn", name=f"{tag}_out", epi=_residual_epi(0.5, bool(more)), extras=[h] + more,
                 out_dtypes=dtypes, tm=512, tk=FFN_WIDE)
    return res[0], (res[1] if more else None), (h, n, dact_g, dact_u, act), exchanged


def ffn_bwd(dh, saved, gain, wi, wo, tag, exchange=None):
    h, n, dact_g, dact_u, act = saved
    dg, du = matmul(dh, wo, mode="nt", name=f"{tag}_out_dx", extras=[dact_g, dact_u], out_dtypes=(BF16, BF16),
                    epi=lambda acc, fg, fu: (0.5 * acc * fg, 0.5 * acc * fu), tm=512, tn=FFN_WIDE)
    d_wo, = matmul(act, dh, mode="tn", name=f"{tag}_out_dw", scale=0.5, tm=FFN_WIDE)
    d_wi = [matmul(n, half, mode="tn", name=f"{tag}_in_dw", tn=FFN_WIDE)[0] for half in (dg, du)]
    res = matmul([dg, du], [(wi, 0), (wi, 1)], mode="nt", name=f"{tag}_in_dx", epi=_rms_bwd_tile,
                 extras=[h, gain, dh], out_dtypes=(F32,), col_sums=1, exchange=exchange, tm=256, tk=FFN_WIDE)
    (dh_new, d_gain), exchanged = res if exchange is not None else (res, [])
    return dh_new, d_gain, d_wi, d_wo, exchanged


def mix_fwd(h, n, lw, batch, next_gain, tag, exchange=None):
    sw = h.shape[1] // 2
    us, up = matmul(n, [lw["w_in"][:, :sw], lw["w_in"][:, sw:]], mode="nn", name=f"{tag}_in", separate=True,
                    out_dtypes=(F32, F32))
    lam_r, lam_i, bb_r, bb_i = ssm_prep(lw["lam_re"], lw["lam_im"], lw["log_dt"], lw["b_re"], lw["b_im"], f"{tag}_zoh")
    lam = (lam_r.reshape(1, SSM_CH), lam_i.reshape(1, SSM_CH))
    b_mats = [_block_diag(bb.reshape(SSM_GROUPS, SSM_STATE, SSM_GROUP_CH).transpose(0, 2, 1)).astype(BF16)
              for bb in (bb_r, bb_i)]
    c_mats = [_block_diag(cc.transpose(0, 2, 1)).astype(BF16) for cc in (lw["c_re"], -lw["c_im"])]
    bu_re, bu_im = matmul(us, b_mats, mode="nn", name=f"{tag}_bu", separate=True, diag=SSM_SUPER,
                          out_dtypes=(BF16, BF16))
    s_re, s_im, exchanged = ssm_scan(bu_re, bu_im, *lam, batch, f"{tag}_scan", exchange)
    y0, y1 = matmul([s_re, s_im], c_mats, mode="nn", name=f"{tag}_c", diag=SSM_SUPER,
                    epi=lambda acc, u, d: (acc + d * u, _gelu(acc + d * u)), extras=[us, lw["ssm_d"]],
                    out_dtypes=(F32, BF16))
    y2, gl = matmul(y1, lw["w_glu"], mode="nn", name=f"{tag}_glu",
                    epi=lambda acc, y0: (_gelu(y0) * _sigmoid(acc), acc), extras=[y0], out_dtypes=(BF16, F32))
    yp, q = pool_fwd(up, lw["pool_w"], lw["pool_scale"], batch, f"{tag}_pool")
    more, dtypes = _next_norm(next_gain)
    res = matmul([y2, yp], [lw["w_out"][:sw], lw["w_out"][sw:]], mode="nn", name=f"{tag}_out",
                 epi=_residual_epi(1.0, bool(more)), extras=[h] + more, out_dtypes=dtypes)
    saved = (h, n, us, lam, b_mats, c_mats, s_re, s_im, y0, y1, gl, y2, yp, q)
    return res[0], (res[1] if more else None), saved, exchanged


def mix_bwd(dh, saved, lw, batch, tag, exchange=None):
    h, n, us, lam, b_mats, c_mats, s_re, s_im, y0, y1, gl, y2, yp, q = saved
    sw = h.shape[1] // 2
    w_out_s, w_out_p = lw["w_out"][:sw], lw["w_out"][sw:]
    d_wo_s, d_wo_p = matmul([y2, yp], dh, mode="tn", name=f"{tag}_out_dw", separate=True)
    def out_dx_epi(dy2, dyp, y0, gl):
        sg = _sigmoid(gl)
        return dy2, dyp, dy2 * _gelu(y0) * sg * (1.0 - sg)

    dy2, dyp, tg = matmul(dh, [w_out_s, w_out_p], mode="nt", name=f"{tag}_out_dx", separate=True, epi=out_dx_epi,
                          extras=[y0, gl], out_dtypes=(F32, F32, BF16))
    dup, d_pool_w, d_pool_scale = pool_bwd(dyp, q, lw["pool_w"], lw["pool_scale"], batch, f"{tag}_pool_bwd")
    def dy0_epi(acc, dy2, gl, y0, u):
        dy0 = (acc + dy2 * _sigmoid(gl)) * _gelu_grad(y0)
        return dy0, jnp.sum(dy0 * u, axis=0, keepdims=True)

    dy0, d_d = matmul(tg, lw["w_glu"], mode="nt", name=f"{tag}_glu_dx", epi=dy0_epi, extras=[dy2, gl, y0, us],
                      out_dtypes=(F32,), col_sums=1)
    d_w_glu, = matmul(y1, tg, mode="tn", name=f"{tag}_glu_dw")
    gd_re, gd_im = matmul(dy0, c_mats, mode="nt", name=f"{tag}_c_dx", separate=True, diag=SSM_SUPER,
                          out_dtypes=(BF16, BF16))
    d_c_top, d_c_bot = matmul([s_re, s_im], dy0, mode="tn", name=f"{tag}_c_dw", separate=True, diag=SSM_SUPER)
    g_re, g_im, d_lam_r, d_lam_i, exchanged = ssm_scan_bwd(gd_re, gd_im, s_re, s_im, *lam, batch, f"{tag}_scan_bwd",
                                                           exchange)
    dus, = matmul([g_re, g_im], b_mats, mode="nt", name=f"{tag}_bu_dx", diag=SSM_SUPER,
                  epi=lambda acc, dy0, d: (acc + d * dy0,), extras=[dy0, lw["ssm_d"]])
    d_b_re, d_b_im = matmul(us, [g_re, g_im], mode="tn", name=f"{tag}_bu_dw", separate=True, diag=SSM_SUPER)
    d_bb_r = _diag_blocks(d_b_re, SSM_GROUP_CH, SSM_STATE).transpose(0, 2, 1).reshape(SSM_CH, SSM_GROUP_CH)
    d_bb_i = _diag_blocks(d_b_im, SSM_GROUP_CH, SSM_STATE).transpose(0, 2, 1).reshape(SSM_CH, SSM_GROUP_CH)
    d_lr, d_li, d_ldt, d_br, d_bi = ssm_prep_bwd(
        lw["lam_re"], lw["lam_im"], lw["log_dt"], lw["b_re"], lw["b_im"],
        d_lam_r.reshape(SSM_CH, 1), d_lam_i.reshape(SSM_CH, 1), d_bb_r, d_bb_i, f"{tag}_zoh_bwd")
    d_c_re = _diag_blocks(d_c_top, SSM_STATE, SSM_GROUP_CH).transpose(0, 2, 1)
    d_c_im = -_diag_blocks(d_c_bot, SSM_STATE, SSM_GROUP_CH).transpose(0, 2, 1)
    d_w_in_s, d_w_in_p = matmul(n, [dus, dup], mode="tn", name=f"{tag}_in_dw", separate=True)
    dh_new, d_gain = matmul([dus, dup], [lw["w_in"][:, :sw], lw["w_in"][:, sw:]], mode="nt", name=f"{tag}_in_dx",
                            epi=_rms_bwd_tile, extras=[h, lw["mix_norm"], dh], out_dtypes=(F32,), col_sums=1, tm=512)
    grads = dict(mix_norm=d_gain, w_in=jnp.concatenate([d_w_in_s, d_w_in_p], axis=1),
                 ssm_lambda_re=d_lr, ssm_lambda_im=d_li, ssm_log_dt=d_ldt, ssm_b_re=d_br, ssm_b_im=d_bi,
                 ssm_c_re=d_c_re, ssm_c_im=d_c_im, ssm_d=d_d, ssm_w_glu=d_w_glu, pool_w=d_pool_w,
                 pool_scale=d_pool_scale, w_out=jnp.concatenate([d_wo_s, d_wo_p], axis=0))
    return dh_new, grads, exchanged


def ple_fwd(h, n, p, w_gate, w_proj, next_gain, tag):
    e, = matmul(p, w_proj, mode="nn", name=f"{tag}_proj")
    if next_gain is None:
        def epi(acc, e, res):
            return res + _sigmoid(acc) * e, acc
        more, dtypes = [], (F32, F32)
    else:
        def epi(acc, e, res, gain):
            out = res + _sigmoid(acc) * e
            return out, acc, _rms_tile(out, gain)
        more, dtypes = [next_gain], (F32, F32, BF16)
    res = matmul(n, w_gate, mode="nn", name=f"{tag}_gate", epi=epi, extras=[e, h] + more, out_dtypes=dtypes, tm=512)
    return res[0], (res[2] if more else None), (h, n, e, res[1])


def ple_bwd(dh, saved, p, gain, w_gate, tag):
    h, n, e, pre = saved
    d = h.shape[1]

    def fn(dh, e, pre):
        s = _sigmoid(pre)
        return [dh * e * s * (1.0 - s), dh * s], []

    dpre, de = rowwise(fn, [dh, e, pre], [(d, BF16), (d, BF16)], name=f"{tag}_gate_bwd")
    d_w_gate, = matmul(n, dpre, mode="tn", name=f"{tag}_gate_dw")
    d_w_proj, = matmul(p, de, mode="tn", name=f"{tag}_proj_dw")
    dh_new, d_gain = matmul(dpre, w_gate, mode="nt", name=f"{tag}_gate_dx", epi=_rms_bwd_tile,
                            extras=[h, gain, dh], out_dtypes=(F32,), col_sums=1, tm=512)
    return dh_new, d_gain, d_w_gate, d_w_proj


def loss_head(h, gain, target, name):
    d = h.shape[1]

    def fn(h, g, t):
        r = lax.rsqrt(jnp.mean(h * h, axis=-1, keepdims=True) + EPS)
        diff = h * r * g - t
        sq = jnp.sum(jnp.sum(diff * diff, axis=1, keepdims=True), axis=0, keepdims=True)
        dy = diff * (1.0 / d)
        w = dy * g
        dh = r * w - h * (r * r * r) * jnp.mean(h * w, axis=-1, keepdims=True)
        return [dh], [sq, jnp.sum(dy * (h * r), axis=0, keepdims=True)]

    dh, sq, d_gain = rowwise(fn, [h, gain, target], [(d, F32)], [(1, 1), (1, d)], name=name)
    return 0.5 / d * sq[0, 0], dh, d_gain


SHARDED = {
    "ffn1_wi": 1, "ffn1_wo": 0, "w_in": 0, "ssm_w_glu": 0, "w_out": 0, "ffn2_wi": 1, "ffn2_wo": 0,
    "ple_w_gate": 0, "ple_w_proj": 1,
}
WEIGHTS = ["ffn1_norm", "ffn1_wi", "ffn1_wo", "mix_norm", "w_in", "ssm_lambda_re", "ssm_lambda_im", "ssm_log_dt",
           "ssm_b_re", "ssm_b_im", "ssm_c_re", "ssm_c_im", "ssm_d", "ssm_w_glu", "pool_w", "pool_scale", "w_out",
           "ffn2_norm", "ffn2_wi", "ffn2_wo", "ple_norm", "ple_w_gate", "ple_w_proj", "final_norm"]
REPLICATED = [n for n in WEIGHTS if n not in SHARDED]


HALVED = ("ffn1_wi", "ffn2_wi")


def _unshard(gathered, axis, halved):
    if halved:
        _, rows, cols = gathered.shape
        return gathered.reshape(2, 4, rows, cols).transpose(0, 2, 1, 3).reshape(2, rows, 4 * cols)
    g = jnp.moveaxis(gathered, 0, axis)
    shp = g.shape
    return g.reshape(shp[:axis] + (shp[axis] * shp[axis + 1],) + shp[axis + 2:])


def _split_for_scatter(full, axis, c, halved):
    if halved:
        rows, cols = full[0].shape

        def pick(cc):
            return [lax.dynamic_index_in_dim(h.reshape(rows, 2, 2, cols // 4), cc, 2, keepdims=False).transpose(1, 0, 2)
                    for h in full]

        return pick(c), [s.astype(BF16) for s in pick(1 - c)]
    shp = full.shape
    g = full.reshape(shp[:axis] + (4, 2, shp[axis] // N_DEV) + shp[axis + 1:])
    keep = lax.dynamic_index_in_dim(g, c, axis + 1, keepdims=False)
    send = lax.dynamic_index_in_dim(g, 1 - c, axis + 1, keepdims=False)
    return jnp.moveaxis(keep, axis, 0), jnp.moveaxis(send, axis, 0).astype(BF16)


def _pack(arrs):
    pieces = []
    for a in arrs:
        flat = a.reshape(-1)
        pad = (-flat.shape[0]) % PACK
        pieces.append(jnp.pad(flat, (0, pad)).reshape(-1, LANES))
    return jnp.concatenate(pieces, axis=0)


def _unpack(packed, shapes):
    out, row = [], 0
    for s in shapes:
        size = math.prod(s)
        rows = (size + PACK - 1) // PACK * SUBLANES
        out.append(packed[row:row + rows].reshape(-1)[:size].reshape(s))
        row += rows
    return out


class NoExchange:
    def __init__(self, full):
        self.full = full

    def big(self, i, name):
        w = self.full[name][i]
        if name in HALVED:
            rows, cols = w.shape
            w = w.reshape(rows, 2, cols // 2).transpose(1, 0, 2)
        return w

    def fwd_exchange(self, i, where):
        return None

    def fwd_done(self, i, where, results):
        pass

    def bwd_exchange(self, i, where):
        return None

    def bwd_done(self, i, where, results):
        pass

    def layer_grads(self, i, grads, names):
        pass


def local_step(x, p, target, rep, hooks):
    batch, seq, d = x.shape
    n_tok = batch * seq
    h = x.reshape(n_tok, d)
    saved = []
    n = rms_fwd(h, rep["ffn1_norm"][0].reshape(1, d), "first_norm")
    for i in range(DEPTH):
        lw = _layer_weights(rep, i, d)
        big = lambda name, i=i: hooks.big(i, name)
        next_gain = rep["ffn1_norm"][i + 1].reshape(1, d) if i + 1 < DEPTH else None
        h, n, s1, exchanged = ffn_fwd(h, n, big("ffn1_wi"), big("ffn1_wo"), lw["mix_norm"], "ffn1",
                                      hooks.fwd_exchange(i, "ffn1_in"))
        hooks.fwd_done(i, "ffn1_in", exchanged)
        lw.update(w_in=big("w_in"), w_glu=big("ssm_w_glu"), w_out=big("w_out"))
        h, n, s2, exchanged = mix_fwd(h, n, lw, batch, lw["ffn2_norm"], "mix", hooks.fwd_exchange(i, "scan"))
        hooks.fwd_done(i, "scan", exchanged)
        h, n, s3, exchanged = ffn_fwd(h, n, big("ffn2_wi"), big("ffn2_wo"), lw["ple_norm"], "ffn2",
                                      hooks.fwd_exchange(i, "ffn2_in"))
        hooks.fwd_done(i, "ffn2_in", exchanged)
        p_i = p[i].reshape(n_tok, -1)
        h, n, s4 = ple_fwd(h, n, p_i, big("ple_w_gate"), big("ple_w_proj"), next_gain, "ple")
        saved.append((s1, s2, s3, s4, p_i))
    loss, dh, d_final = loss_head(h, rep["final_norm"].reshape(1, d), target.reshape(n_tok, d), "loss_head")
    per_layer = [None] * DEPTH
    for i in reversed(range(DEPTH)):
        lw = _layer_weights(rep, i, d)
        big = lambda name, i=i: hooks.big(i, name)
        lw.update(w_in=big("w_in"), w_glu=big("ssm_w_glu"), w_out=big("w_out"))
        s1, s2, s3, s4, p_i = saved[i]
        g = {}
        dh, g["ple_norm"], g["ple_w_gate"], g["ple_w_proj"] = ple_bwd(dh, s4, p_i, lw["ple_norm"], big("ple_w_gate"), "ple")
        dh, g["ffn2_norm"], g["ffn2_wi"], g["ffn2_wo"], exchanged = ffn_bwd(
            dh, s3, lw["ffn2_norm"], big("ffn2_wi"), big("ffn2_wo"), "ffn2", hooks.bwd_exchange(i, "ffn2_in_dx"))
        hooks.bwd_done(i, "ffn2_in_dx", exchanged)
        dh, gm, exchanged = mix_bwd(dh, s2, lw, batch, "mix", hooks.bwd_exchange(i, "scan"))
        hooks.bwd_done(i, "scan", exchanged)
        g.update(gm)
        hooks.layer_grads(i, g, [k for k in SHARDED if k not in LAST_GRADS])
        dh, g["ffn1_norm"], g["ffn1_wi"], g["ffn1_wo"], exchanged = ffn_bwd(
            dh, s1, lw["ffn1_norm"], big("ffn1_wi"), big("ffn1_wo"), "ffn1", hooks.bwd_exchange(i, "ffn1_in_dx"))
        hooks.bwd_done(i, "ffn1_in_dx", exchanged)
        hooks.layer_grads(i, g, list(LAST_GRADS))
        per_layer[i] = g
    return loss, dh.reshape(batch, seq, d), per_layer, d_final


LAST_GRADS = ("ffn1_wi", "ffn1_wo")


def _layer_weights(w, i, d):
    sw = d // 2
    lw = {}
    lw["pool_w"] = w["pool_w"][i]
    for k in ("ffn1_norm", "mix_norm", "ffn2_norm", "ple_norm"):
        lw[k] = w[k][i].reshape(1, d)
    lw["ssm_d"] = w["ssm_d"][i].reshape(1, sw)
    lw["pool_scale"] = w["pool_scale"][i].reshape(1, sw)
    lw["lam_re"] = w["ssm_lambda_re"][i].reshape(SSM_CH, 1)
    lw["lam_im"] = w["ssm_lambda_im"][i].reshape(SSM_CH, 1)
    lw["log_dt"] = jnp.repeat(w["ssm_log_dt"][i], SSM_STATE).reshape(SSM_CH, 1)
    lw["b_re"] = w["ssm_b_re"][i].reshape(SSM_CH, SSM_GROUP_CH)
    lw["b_im"] = w["ssm_b_im"][i].reshape(SSM_CH, SSM_GROUP_CH)
    lw["c_re"] = w["ssm_c_re"][i]
    lw["c_im"] = w["ssm_c_im"][i]
    return lw


class MeshExchange:
    FIRST = ("ffn1_wi", "ffn1_wo")
    FWD_PLAN = {"scan": ("ffn1_wi", "ffn1_wo", "w_in", "ssm_w_glu", "w_out"),
                "ffn2_in": ("ffn2_wi", "ffn2_wo", "ple_w_gate", "ple_w_proj")}
    BWD_PLAN = {"ffn2_in_dx": ("ffn1_wo", "ffn2_wo", "ssm_w_glu", "w_out", "ple_w_gate", "ple_w_proj"),
                "scan": ("ffn1_wi", "ffn2_wi", "w_in")}

    def __init__(self, shards):
        self.shards = shards
        self.c = lax.axis_index("c")
        self.gathered = {}
        self.chip_sums = {}
        self.from_chips = {}
        self.pending = None
        first = run_exchange(GatherPlan([shards[0][k] for k in self.FIRST]), "gather_first_weights")
        self._store(self.gathered, 0, self.FIRST, first)

    @staticmethod
    def _store(where, layer, names, results):
        for k, r in zip(names, results):
            where[(layer, k)] = r

    def big(self, i, name):
        return _unshard(self.gathered[(i, name)], SHARDED[name], name in HALVED)

    def fwd_exchange(self, i, where):
        if i == 0 and where == "ffn1_in":
            layer, names = 0, [k for k in SHARDED if k not in self.FIRST]
        elif where in self.FWD_PLAN and i + 1 < DEPTH:
            layer, names = i + 1, self.FWD_PLAN[where]
        else:
            return None
        self.pending = (layer, names)
        return GatherPlan([self.shards[layer][k] for k in names])

    def fwd_done(self, i, where, results):
        if results:
            self._store(self.gathered, *self.pending, results)

    def layer_grads(self, i, grads, names):
        pieces, sends = [], []
        for k in names:
            keep, send = _split_for_scatter(grads[k], SHARDED[k], self.c, k in HALVED)
            keep, send = (keep, send) if isinstance(keep, list) else ([keep], [send])
            pieces.append(keep)
            sends += send
        got = iter(swap_with_sibling(sends, "reduce_core_pair"))
        for k, keep in zip(names, pieces):
            sums = []
            for part in keep:
                cols = part.shape[-1]
                sums.append(add2(part.reshape(-1, cols), next(got).reshape(-1, cols), f"sum_core_pair_{k}",
                                 BF16).reshape(part.shape))
            self.chip_sums[(i, k)] = sums if len(sums) > 1 else sums[0]
        if i == 0 and tuple(names) == LAST_GRADS:
            last = run_exchange(ChipScatterPlan([self.chip_sums[(0, k)] for k in names]), "reduce_chips_last")
            self._store(self.from_chips, 0, names, last)

    def bwd_exchange(self, i, where):
        if where in self.BWD_PLAN and i + 1 < DEPTH:
            layer, names = i + 1, self.BWD_PLAN[where]
        elif i == 0 and where == "ffn1_in_dx":
            layer, names = 0, [k for k in SHARDED if k not in LAST_GRADS]
        else:
            return None
        self.pending = (layer, names)
        return ChipScatterPlan([self.chip_sums[(layer, k)] for k in names])

    def bwd_done(self, i, where, results):
        if results:
            self._store(self.from_chips, *self.pending, results)


def kernel(x, p, ffn1_norm, ffn1_wi, ffn1_wo, mix_norm, w_in, ssm_lambda_re, ssm_lambda_im, ssm_log_dt, ssm_b_re, ssm_b_im, ssm_c_re, ssm_c_im, ssm_d, ssm_w_glu, pool_w, pool_scale, w_out, ffn2_norm, ffn2_wi, ffn2_wo, ple_norm, ple_w_gate, ple_w_proj, final_norm, loss_target, m_ffn1_norm, m_ffn1_wi, m_ffn1_wo, m_mix_norm, m_w_in, m_ssm_lambda_re, m_ssm_lambda_im, m_ssm_log_dt, m_ssm_b_re, m_ssm_b_im, m_ssm_c_re, m_ssm_c_im, m_ssm_d, m_ssm_w_glu, m_pool_w, m_pool_scale, m_w_out, m_ffn2_norm, m_ffn2_wi, m_ffn2_wo, m_ple_norm, m_ple_w_gate, m_ple_w_proj, m_final_norm, v_ffn1_norm, v_ffn1_wi, v_ffn1_wo, v_mix_norm, v_w_in, v_ssm_lambda_re, v_ssm_lambda_im, v_ssm_log_dt, v_ssm_b_re, v_ssm_b_im, v_ssm_c_re, v_ssm_c_im, v_ssm_d, v_ssm_w_glu, v_pool_w, v_pool_scale, v_w_out, v_ffn2_norm, v_ffn2_wi, v_ffn2_wo, v_ple_norm, v_ple_w_gate, v_ple_w_proj, v_final_norm):
    args = dict(locals())
    wts = {k: args[k] for k in WEIGHTS}
    rep = {k: wts[k] for k in REPLICATED}

    shards = [{k: wts[k][i].astype(BF16) for k in SHARDED} for i in range(DEPTH)]
    exchange = MeshExchange(shards)
    loss_local, grad_x, per_layer, d_final = local_step(x, p, loss_target, rep, exchange)
    loss = lax.psum(loss_local, ("x", "y", "c"))

    outs = {}
    for k in SHARDED:
        shp = wts[k].shape
        cols = shp[-1]
        parts = jnp.stack([exchange.from_chips[(i, k)] for i in range(DEPTH)], axis=1)
        res = adamw(wts[k].reshape(-1, cols), parts.reshape(4, -1, cols), args["m_" + k].reshape(-1, cols),
                    args["v_" + k].reshape(-1, cols), f"adamw_{k}")
        outs[k] = [r.reshape(shp) for r in res]

    rep_shapes = [wts[k].shape for k in REPLICATED]
    small = {k: jnp.stack([g[k] for g in per_layer], axis=0) for k in REPLICATED if k != "final_norm"}
    small["final_norm"] = d_final
    packed_g = _pack([small[k].reshape(wts[k].shape) for k in REPLICATED])
    all_g, = run_exchange(GatherPlan([packed_g]), "gather_small_grads")
    res = adamw(_pack([wts[k] for k in REPLICATED]), all_g, _pack([args["m_" + k] for k in REPLICATED]),
                _pack([args["v_" + k] for k in REPLICATED]), "adamw_small")
    unpacked = [_unpack(r, rep_shapes) for r in res]
    for j, k in enumerate(REPLICATED):
        outs[k] = [unpacked[q][j] for q in range(4)]

    result = [loss, grad_x]
    for q in range(4):
        result += [outs[k][q] for k in WEIGHTS]
    return tuple(result)
```

```python
import math

import jax
import jax.numpy as jnp
from jax import lax
from jax.experimental import pallas as pl
from jax.experimental.pallas import tpu as pltpu

F32 = jnp.float32
BF16 = jnp.bfloat16
MESH = pl.DeviceIdType.MESH
ANY = pl.BlockSpec(memory_space=pl.ANY)

N_DEV = 8
DEPTH = 4
EPS = 1e-6
SSM_GROUPS = 32
SSM_GROUP_CH = 16
SSM_STATE = 64
SSM_CH = SSM_GROUPS * SSM_STATE
SSM_SUPER = 2
POOL_WINDOWS = (2, 4, 8, 16)
POOL_HALO = 16
ADAM_LR, ADAM_B1, ADAM_B2, ADAM_EPS, ADAM_WD, ADAM_STEP = 0.001, 0.9, 0.999, 1e-08, 0.01, 10

V7X_VMEM_BYTES = 64 * 1024 * 1024
VMEM_LIMIT_BYTES = V7X_VMEM_BYTES - 12 * 1024 * 1024
LANES = 128
SUBLANES = 8
PACK = SUBLANES * LANES


def _params(*sem):
    return pltpu.CompilerParams(dimension_semantics=sem or None, vmem_limit_bytes=VMEM_LIMIT_BYTES)


def _tile(n, pref):
    if n <= pref:
        return n
    t = pref - pref % LANES
    while t >= LANES:
        if n % t == 0:
            return t
        t -= LANES
    raise ValueError(f"no lane-aligned tile for {n}")


def _row_tile(rows, pref):
    if rows <= pref:
        return rows
    t = pref - pref % SUBLANES
    while t >= SUBLANES:
        if rows % t == 0:
            return t
        t -= SUBLANES
    raise ValueError(f"no sublane-aligned tile for {rows}")


_DIMS = {"nn": ((1,), (0,)), "nt": ((1,), (1,)), "tn": ((0,), (0,))}


def matmul(a, b, *, mode, name, out_dtypes=None, epi=None, extras=(), separate=False, diag=1, col_sums=0,
           stack_out=False, scale=1.0, exchange=None, tm=1024, tn=1024, tk=1024):
    a_list = list(a) if isinstance(a, list) else [a]
    b_list = list(b) if isinstance(b, list) else [b]
    a_planes = [x[1] if isinstance(x, tuple) else None for x in a_list]
    b_planes = [x[1] if isinstance(x, tuple) else None for x in b_list]
    a_list = [x[0] if isinstance(x, tuple) else x for x in a_list]
    b_list = [x[0] if isinstance(x, tuple) else x for x in b_list]
    a_shape, b_shape = a_list[0].shape[-2:], b_list[0].shape[-2:]
    n_terms = max(len(a_list), len(b_list))
    a_idx = [0] * n_terms if len(a_list) == 1 else list(range(n_terms))
    b_idx = [0] * n_terms if len(b_list) == 1 else list(range(n_terms))
    n_acc = n_terms if separate else 1
    assert not (stack_out or scale != 1.0) or epi is None
    if out_dtypes is None:
        out_dtypes = (F32,) * (1 if (epi is not None or stack_out) else n_acc)
    in_place = epi is None
    if mode == "tn":
        K, M = a_shape
        K2, N = b_shape
    elif mode == "nt":
        M, K = a_shape
        N, K2 = b_shape
    else:
        M, K = a_shape
        K2, N = b_shape
    assert K == K2, (name, a_shape, b_shape)
    if mode == "tn":
        tm, tn, tk = _tile(M // diag, tm), _tile(N // diag, tn), _tile(K, tk)
        nk = K // tk
        N = N // diag
        row_tiles, col_tiles = (M // diag) // tm, N // tn
        a_blk, a_map = (tk, tm), lambda i, j, k: (k, i)
        b_blk, b_map = (tk, tn), lambda i, j, k: (k, (i // row_tiles) * col_tiles + j)
    else:
        tm, tn, tk = _tile(M, tm), _tile(N // diag, tn), _tile(K // diag, tk)
        nk = (K // diag) // tk
        col_tiles = (N // diag) // tn
        a_blk, a_map = (tm, tk), lambda i, j, k: (i, (j // col_tiles) * nk + k)
        if mode == "nt":
            b_blk, b_map = (tn, tk), lambda i, j, k: (j, (j // col_tiles) * nk + k)
        else:
            b_blk, b_map = (tk, tn), lambda i, j, k: ((j // col_tiles) * nk + k, j)

    def plane_spec(blk, index_map, plane):
        if plane is None:
            return pl.BlockSpec(blk, index_map)
        return pl.BlockSpec((None,) + blk, lambda i, j, k: (plane,) + index_map(i, j, k))

    a_specs = [plane_spec(a_blk, a_map, p_) for p_ in a_planes]
    b_specs = [plane_spec(b_blk, b_map, p_) for p_ in b_planes]
    assert not col_sums or N == tn, (name, N, tn)
    ex_specs = []
    for e in extras:
        if e.shape == (M, N):
            ex_specs.append(pl.BlockSpec((tm, tn), lambda i, j, k: (i, j)))
        elif e.shape == (1, N):
            ex_specs.append(pl.BlockSpec((1, tn), lambda i, j, k: (0, j)))
        elif e.shape == (M, 1):
            ex_specs.append(pl.BlockSpec((tm, 1), lambda i, j, k: (i, 0)))
        else:
            raise ValueError((name, e.shape, (M, N)))
    na, nb, ne, no = len(a_list), len(b_list), len(extras), len(out_dtypes)
    dims = (_DIMS[mode], ((), ()))

    n_scratch = n_acc if (nk > 1 and not in_place) else 0
    x_ins, x_out, x_sems = _exchange_args(exchange)
    grid = (M // tm, N // tn, nk)

    def body(*refs):
        a_refs, b_refs, ex_refs, xin, out_refs, sum_refs, xout, acc_refs, xsems = _split_refs(
            refs, na, nb, ne, len(x_ins), no, col_sums, len(x_out), n_scratch)
        if exchange is not None:
            step = (pl.program_id(0) * grid[1] + pl.program_id(1)) * nk + pl.program_id(2)
            _run_exchange(exchange, step, grid[0] * grid[1] * nk, (xin, xout, xsems))
        a_vals = [r[...].astype(BF16) for r in a_refs]
        b_vals = [r[...].astype(BF16) for r in b_refs]
        prods = [lax.dot_general(a_vals[a_idx[t]], b_vals[b_idx[t]], dims, preferred_element_type=F32)
                 for t in range(n_terms)]
        if not separate:
            total = prods[0]
            for p_ in prods[1:]:
                total = total + p_
            prods = [total]

        def finish(accs):
            res = epi(*accs, *[e[...] for e in ex_refs]) if epi is not None else tuple(accs)
            for r, v in zip(out_refs, res[:no]):
                r[...] = v.astype(r.dtype)
            first_rows = pl.program_id(0) == 0
            for r, v in zip(sum_refs, res[no:]):
                @pl.when(first_rows)
                def _(r=r, v=v):
                    r[...] = v

                @pl.when(jnp.logical_not(first_rows))
                def _(r=r, v=v):
                    r[...] += v

        if in_place:
            dst = [(out_refs[0], t) for t in range(n_acc)] if stack_out else [(r, None) for r in out_refs]

            def read(r, t):
                return r[...] if t is None else r[t]

            assert nk == 1 or all(dt == F32 for dt in out_dtypes), name

            def write(r, t, v):
                if t is None:
                    r[...] = v.astype(r.dtype)
                else:
                    r[t] = v.astype(r.dtype)

            if nk == 1:
                for (r, t), v in zip(dst, prods):
                    write(r, t, v * scale if scale != 1.0 else v)
            else:
                k = pl.program_id(2)

                @pl.when(k == 0)
                def _():
                    for (r, t), v in zip(dst, prods):
                        write(r, t, v)

                @pl.when(jnp.logical_and(k > 0, k < nk - 1))
                def _():
                    for (r, t), v in zip(dst, prods):
                        write(r, t, read(r, t) + v)

                @pl.when(k == nk - 1)
                def _():
                    for (r, t), v in zip(dst, prods):
                        total = read(r, t) + v
                        write(r, t, total * scale if scale != 1.0 else total)
        elif nk == 1:
            finish(prods)
        else:
            k = pl.program_id(2)

            @pl.when(k == 0)
            def _():
                for r, v in zip(acc_refs, prods):
                    r[...] = v

            @pl.when(jnp.logical_and(k > 0, k < nk - 1))
            def _():
                for r, v in zip(acc_refs, prods):
                    r[...] += v

            @pl.when(k == nk - 1)
            def _():
                finish([r[...] + v for r, v in zip(acc_refs, prods)])

    if stack_out:
        out_specs = [pl.BlockSpec((n_acc, tm, tn), lambda i, j, k: (0, i, j))]
        out_shape = [jax.ShapeDtypeStruct((n_acc, M, N), F32)]
    else:
        out_specs = [pl.BlockSpec((tm, tn), lambda i, j, k: (i, j))] * no
        out_shape = [jax.ShapeDtypeStruct((M, N), dt) for dt in out_dtypes]
    sequential = col_sums or exchange is not None
    outs = pl.pallas_call(
        body,
        name=name,
        grid=grid,
        in_specs=a_specs + b_specs + ex_specs + [ANY] * len(x_ins),
        out_specs=out_specs + [pl.BlockSpec((1, tn), lambda i, j, k: (0, j))] * col_sums + [ANY] * len(x_out),
        out_shape=out_shape + [jax.ShapeDtypeStruct((1, N), F32)] * col_sums + x_out,
        scratch_shapes=[pltpu.VMEM((tm, tn), F32)] * n_scratch + x_sems,
        compiler_params=_params(*(("arbitrary",) * 3 if sequential else ("parallel", "parallel", "arbitrary"))),
    )(*a_list, *b_list, *extras, *x_ins)
    if exchange is not None:
        n_own = len(outs) - len(x_out)
        return list(outs[:n_own]), list(outs[n_own:])
    return outs


def rowwise(fn, ins, outs, accs=(), *, name, tr=512):
    R = max(x.shape[0] for x in ins)
    tr = _row_tile(R, tr)
    in_specs = []
    for x in ins:
        if x.shape[0] == R and x.ndim == 2:
            in_specs.append(pl.BlockSpec((tr, x.shape[1]), lambda i: (i, 0)))
        else:
            in_specs.append(pl.BlockSpec(x.shape, lambda i, _n=x.ndim: (0,) * _n))
    ni, no = len(ins), len(outs)

    def body(*refs):
        i = pl.program_id(0)
        row_vals, acc_vals = fn(*[r[...] for r in refs[:ni]])
        for r, v in zip(refs[ni:ni + no], row_vals):
            r[...] = v.astype(r.dtype)
        for r, v in zip(refs[ni + no:], acc_vals):
            @pl.when(i == 0)
            def _(r=r, v=v):
                r[...] = v

            @pl.when(i > 0)
            def _(r=r, v=v):
                r[...] += v

    return pl.pallas_call(
        body,
        name=name,
        grid=(R // tr,),
        in_specs=in_specs,
        out_specs=[pl.BlockSpec((tr, c), lambda i: (i, 0)) for c, _ in outs]
        + [pl.BlockSpec(s, lambda i: (0, 0)) for s in accs],
        out_shape=[jax.ShapeDtypeStruct((R, c), dt) for c, dt in outs]
        + [jax.ShapeDtypeStruct(s, F32) for s in accs],
        compiler_params=_params("arbitrary"),
    )(*ins)


def _sigmoid(x):
    return 1.0 / (1.0 + jnp.exp(-x))


_GELU_C = math.sqrt(2.0 / math.pi)


def _gelu(x):
    return 0.5 * x * (1.0 + jnp.tanh(_GELU_C * (x + 0.044715 * (x * x * x))))


def _gelu_grad(x):
    t = jnp.tanh(_GELU_C * (x + 0.044715 * (x * x * x)))
    return 0.5 * (1.0 + t) + 0.5 * x * (1.0 - t * t) * (_GELU_C * (1.0 + 3.0 * 0.044715 * (x * x)))


def rms_fwd(x, g, name):
    def fn(x, g):
        r = lax.rsqrt(jnp.mean(x * x, axis=-1, keepdims=True) + EPS)
        return [x * r * g], []

    return rowwise(fn, [x, g], [(x.shape[1], BF16)], name=name)[0]


def _rms_tile(x, g):
    return x * lax.rsqrt(jnp.mean(x * x, axis=-1, keepdims=True) + EPS) * g


def _rms_bwd_tile(dn, x, g, dres):
    r = lax.rsqrt(jnp.mean(x * x, axis=-1, keepdims=True) + EPS)
    w = dn * g
    dx = r * w - x * (r * r * r) * jnp.mean(x * w, axis=-1, keepdims=True)
    return dres + dx, jnp.sum(dn * (x * r), axis=0, keepdims=True)


def _whole(shape):
    return pl.BlockSpec(shape, lambda: (0,) * len(shape))


def _zoh(lr, li, ldt):
    dt = jnp.exp(ldt)
    mag = jnp.exp(lr * dt)
    ar, ai = mag * jnp.cos(li * dt), mag * jnp.sin(li * dt)
    den = lr * lr + li * li
    kr = ((ar - 1.0) * lr + ai * li) / den
    ki = (ai * lr - (ar - 1.0) * li) / den
    return dt, ar, ai, den, kr, ki


def ssm_prep(lam_re, lam_im, log_dt, b_re, b_im, name):
    n = lam_re.shape[0]

    def body(lr_ref, li_ref, ldt_ref, br_ref, bi_ref, ar_ref, ai_ref, bbr_ref, bbi_ref):
        _, ar, ai, _, kr, ki = _zoh(lr_ref[...], li_ref[...], ldt_ref[...])
        br, bi = br_ref[...], bi_ref[...]
        ar_ref[...] = ar
        ai_ref[...] = ai
        bbr_ref[...] = kr * br - ki * bi
        bbi_ref[...] = kr * bi + ki * br

    col, mat = (n, 1), (n, SSM_GROUP_CH)
    return pl.pallas_call(
        body, name=name,
        in_specs=[_whole(col)] * 3 + [_whole(mat)] * 2,
        out_specs=[_whole(col)] * 2 + [_whole(mat)] * 2,
        out_shape=[jax.ShapeDtypeStruct(col, F32)] * 2 + [jax.ShapeDtypeStruct(mat, F32)] * 2,
        compiler_params=_params(),
    )(lam_re, lam_im, log_dt, b_re, b_im)


def ssm_prep_bwd(lam_re, lam_im, log_dt, b_re, b_im, d_ar, d_ai, d_bbr, d_bbi, name):
    n = lam_re.shape[0]
    n_groups = n // SSM_STATE

    def body(lr_ref, li_ref, ldt_ref, br_ref, bi_ref, dar_ref, dai_ref, dbr_ref, dbi_ref,
             glr_ref, gli_ref, gdt_ref, gbr_ref, gbi_ref):
        lr, li = lr_ref[...], li_ref[...]
        dt, ar, ai, den, kr, ki = _zoh(lr, li, ldt_ref[...])
        br, bi, dbr, dbi = br_ref[...], bi_ref[...], dbr_ref[...], dbi_ref[...]
        gbr_ref[...] = kr * dbr + ki * dbi
        gbi_ref[...] = kr * dbi - ki * dbr
        gkr = jnp.sum(br * dbr + bi * dbi, axis=1, keepdims=True)
        gki = jnp.sum(br * dbi - bi * dbr, axis=1, keepdims=True)
        gar = dar_ref[...] + (gkr * lr - gki * li) / den
        gai = dai_ref[...] + (gki * lr + gkr * li) / den
        qr, qi = -(kr * lr + ki * li) / den, -(ki * lr - kr * li) / den
        g1r, g1i = qr * gkr + qi * gki, qr * gki - qi * gkr
        g2r, g2i = dt * (ar * gar + ai * gai), dt * (ar * gai - ai * gar)
        glr_ref[...] = g1r + g2r
        gli_ref[...] = g1i + g2i
        pr, pi_ = lr * ar - li * ai, lr * ai + li * ar
        gdt = (pr * gar + pi_ * gai) * dt
        grp = lax.broadcasted_iota(jnp.int32, (n, n_groups), 0) // SSM_STATE
        sel = grp == lax.broadcasted_iota(jnp.int32, (n, n_groups), 1)
        gdt_ref[...] = jnp.sum(jnp.where(sel, gdt, 0.0), axis=0, keepdims=True)

    col, mat = (n, 1), (n, SSM_GROUP_CH)
    return pl.pallas_call(
        body, name=name,
        in_specs=[_whole(col)] * 3 + [_whole(mat)] * 2 + [_whole(col)] * 2 + [_whole(mat)] * 2,
        out_specs=[_whole(col)] * 2 + [_whole((1, n_groups))] + [_whole(mat)] * 2,
        out_shape=[jax.ShapeDtypeStruct(col, F32)] * 2 + [jax.ShapeDtypeStruct((1, n_groups), F32)]
        + [jax.ShapeDtypeStruct(mat, F32)] * 2,
        compiler_params=_params(),
    )(lam_re, lam_im, log_dt, b_re, b_im, d_ar, d_ai, d_bbr, d_bbi)


def _cmul(ar, ai, br, bi):
    return ar * br - ai * bi, ar * bi + ai * br


def _scan_block(xr, xi, lr, li, carry_r, carry_i, or_ref, oi_ref, loc_r, loc_i, reverse):
    tb, cb = xr.shape
    ng = tb // SUBLANES
    xr = xr.reshape(ng, SUBLANES, cb)
    xi = xi.reshape(ng, SUBLANES, cb)
    rid = lax.broadcasted_iota(jnp.int32, (1, SUBLANES, cb), 1)
    pr, pi_ = lr.reshape(1, 1, cb), li.reshape(1, 1, cb)
    powers = []
    for k in (1, 2, 4):
        powers.append((pr, pi_))
        shift = SUBLANES - k if reverse else k
        sr, si = pltpu.roll(xr, shift, 1), pltpu.roll(xi, shift, 1)
        keep = (rid < SUBLANES - k) if reverse else (rid >= k)
        tr_, ti_ = _cmul(jnp.where(keep, pr, 0.0), jnp.where(keep, pi_, 0.0), sr, si)
        xr = xr + tr_
        xi = xi + ti_
        pr, pi_ = _cmul(pr, pi_, pr, pi_)
    loc_r[...] = xr
    loc_i[...] = xi
    (p1r, p1i), (p2r, p2i), (p4r, p4i) = powers
    dist = lax.broadcasted_iota(jnp.int32, (SUBLANES, cb), 0)
    if reverse:
        dist = SUBLANES - 1 - dist
    wr = jnp.broadcast_to(p1r.reshape(1, cb), (SUBLANES, cb))
    wi = jnp.broadcast_to(p1i.reshape(1, cb), (SUBLANES, cb))
    for bit, (qr, qi) in ((1, (p1r, p1i)), (2, (p2r, p2i)), (4, (p4r, p4i))):
        mr, mi = _cmul(wr, wi, qr.reshape(1, cb), qi.reshape(1, cb))
        on = (dist & bit) != 0
        wr, wi = jnp.where(on, mr, wr), jnp.where(on, mi, wi)
    last = 0 if reverse else SUBLANES - 1

    def step(j, carry):
        cr, ci = carry
        g = (ng - 1 - j) if reverse else j
        fr = loc_r[g] + (wr * cr - wi * ci)
        fi = loc_i[g] + (wr * ci + wi * cr)
        rows = pl.ds(pl.multiple_of(g * SUBLANES, SUBLANES), SUBLANES)
        or_ref[rows, :] = fr
        oi_ref[rows, :] = fi
        return fr[last:last + 1, :], fi[last:last + 1, :]

    cr, ci = lax.fori_loop(0, ng, step, (carry_r[...], carry_i[...]))
    carry_r[...] = cr
    carry_i[...] = ci


def _scan_tiles(seq_len, n_ch):
    return min(256, seq_len), min(512, n_ch)


def _run_exchange(plan, step, n_steps, refs):
    @pl.when(step == 0)
    def _():
        plan.start(*refs)

    for part, at in plan.relay_steps(n_steps):
        @pl.when(step == at)
        def _(part=part):
            plan.relay(part, *refs)

    @pl.when(step == n_steps - 1)
    def _():
        plan.finish(*refs)


def _exchange_args(plan):
    if plan is None:
        return [], [], []
    return list(plan.ins), list(plan.out_shape), list(plan.sems)


def _split_refs(refs, *counts):
    groups, at = [], 0
    for n in counts:
        groups.append(refs[at:at + n])
        at += n
    return groups + [refs[at:]]


def ssm_scan(x_re, x_im, lam_re, lam_im, batch, name, exchange=None):
    n, nch = x_re.shape
    seq = n // batch
    tb, cb = _scan_tiles(seq, nch)
    nt, nc = seq // tb, nch // cb
    ex_ins, ex_out, ex_sems = _exchange_args(exchange)

    def body(*refs):
        ins, xin, outs, xout, scratch, xsems = _split_refs(refs, 4, len(ex_ins), 2, len(ex_out), 6)
        xr_ref, xi_ref, lr_ref, li_ref = ins
        or_ref, oi_ref = outs
        car_r, car_i, loc_r, loc_i, s_r, s_i = scratch
        if exchange is not None:
            step = (pl.program_id(0) * batch + pl.program_id(1)) * nt + pl.program_id(2)
            _run_exchange(exchange, step, nc * batch * nt, (xin, xout, xsems))

        @pl.when(pl.program_id(2) == 0)
        def _():
            car_r[...] = jnp.zeros_like(car_r)
            car_i[...] = jnp.zeros_like(car_i)

        _scan_block(xr_ref[...].astype(F32), xi_ref[...].astype(F32), lr_ref[...], li_ref[...], car_r, car_i,
                    s_r, s_i, loc_r, loc_i, reverse=False)
        or_ref[...] = s_r[...].astype(or_ref.dtype)
        oi_ref[...] = s_i[...].astype(oi_ref.dtype)

    blk = pl.BlockSpec((tb, cb), lambda c, b, t: (b * nt + t, c))
    lam_spec = pl.BlockSpec((1, cb), lambda c, b, t: (0, c))
    res = pl.pallas_call(
        body, name=name,
        grid=(nc, batch, nt),
        in_specs=[blk, blk, lam_spec, lam_spec] + [ANY] * len(ex_ins),
        out_specs=[blk, blk] + [ANY] * len(ex_out),
        out_shape=[jax.ShapeDtypeStruct((n, nch), BF16)] * 2 + ex_out,
        scratch_shapes=[pltpu.VMEM((1, cb), F32)] * 2 + [pltpu.VMEM((tb // SUBLANES, SUBLANES, cb), F32)] * 2
        + [pltpu.VMEM((tb, cb), F32)] * 2 + ex_sems,
        compiler_params=_params("arbitrary", "arbitrary", "arbitrary"),
    )(x_re, x_im, lam_re, lam_im, *ex_ins)
    return res[0], res[1], list(res[2:])


def ssm_scan_bwd(d_re, d_im, s_re, s_im, lam_re, lam_im, batch, name, exchange=None):
    n, nch = d_re.shape
    seq = n // batch
    tb, cb = _scan_tiles(seq, nch)
    nt, nc = seq // tb, nch // cb
    halo_rows = 2 * SUBLANES
    hb = tb // halo_rows
    ex_ins, ex_out, ex_sems = _exchange_args(exchange)

    def body(*refs):
        ins, xin, outs, xout, scratch, xsems = _split_refs(refs, 8, len(ex_ins), 4, len(ex_out), 6)
        xr_ref, xi_ref, sr_ref, si_ref, hr_ref, hi_ref, lr_ref, li_ref = ins
        or_ref, oi_ref, dlr_ref, dli_ref = outs
        car_r, car_i, loc_r, loc_i, g_r, g_i = scratch
        b, t = pl.program_id(1), pl.program_id(2)
        if exchange is not None:
            step = (pl.program_id(0) * batch + b) * nt + t
            _run_exchange(exchange, step, nc * batch * nt, (xin, xout, xsems))

        @pl.when(t == 0)
        def _():
            car_r[...] = jnp.zeros_like(car_r)
            car_i[...] = jnp.zeros_like(car_i)

        _scan_block(xr_ref[...].astype(F32), xi_ref[...].astype(F32), lr_ref[...], -li_ref[...], car_r, car_i,
                    g_r, g_i, loc_r, loc_i, reverse=True)
        gr, gi = g_r[...], g_i[...]
        or_ref[...] = gr.astype(or_ref.dtype)
        oi_ref[...] = gi.astype(oi_ref.dtype)
        first_block = t == nt - 1
        row = lax.broadcasted_iota(jnp.int32, (tb, cb), 0)
        hr = jnp.where(first_block, 0.0, hr_ref[...].astype(F32)[halo_rows - 1:halo_rows, :])
        hi = jnp.where(first_block, 0.0, hi_ref[...].astype(F32)[halo_rows - 1:halo_rows, :])
        pr = jnp.where(row == 0, hr, pltpu.roll(sr_ref[...].astype(F32), 1, 0))
        pi_ = jnp.where(row == 0, hi, pltpu.roll(si_ref[...].astype(F32), 1, 0))
        dlr = jnp.sum(gr * pr + gi * pi_, axis=0, keepdims=True)
        dli = jnp.sum(gi * pr - gr * pi_, axis=0, keepdims=True)
        start = jnp.logical_and(b == 0, t == 0)

        @pl.when(start)
        def _():
            dlr_ref[...] = dlr
            dli_ref[...] = dli

        @pl.when(jnp.logical_not(start))
        def _():
            dlr_ref[...] += dlr
            dli_ref[...] += dli

    def blk(c, b, t):
        return b * nt + (nt - 1 - t)

    st_spec = pl.BlockSpec((tb, cb), lambda c, b, t: (blk(c, b, t), c))
    halo_spec = pl.BlockSpec((halo_rows, cb), lambda c, b, t: (jnp.maximum(blk(c, b, t) * hb - 1, 0), c))
    row_spec = pl.BlockSpec((1, cb), lambda c, b, t: (0, c))
    res = pl.pallas_call(
        body, name=name,
        grid=(nc, batch, nt),
        in_specs=[st_spec] * 4 + [halo_spec] * 2 + [row_spec] * 2 + [ANY] * len(ex_ins),
        out_specs=[st_spec, st_spec, row_spec, row_spec] + [ANY] * len(ex_out),
        out_shape=[jax.ShapeDtypeStruct((n, nch), BF16)] * 2 + [jax.ShapeDtypeStruct((1, nch), F32)] * 2 + ex_out,
        scratch_shapes=[pltpu.VMEM((1, cb), F32)] * 2 + [pltpu.VMEM((tb // SUBLANES, SUBLANES, cb), F32)] * 2
        + [pltpu.VMEM((tb, cb), F32)] * 2 + ex_sems,
        compiler_params=_params("arbitrary", "arbitrary", "arbitrary"),
    )(d_re, d_im, s_re, s_im, s_re, s_im, lam_re, lam_im, *ex_ins)
    return res[0], res[1], res[2], res[3], list(res[4:])


def _pool_tiles(seq_len):
    return min(512, seq_len)


def _window_sums(x, n_steps, forward_in_time):
    rows = x.shape[0]
    k = 1
    for _ in range(n_steps):
        x = x + pltpu.roll(x, k if forward_in_time else rows - k, 0)
        k *= 2
    return x


def pool_fwd(u, w_pool, scale, batch, name):
    n, c = u.shape
    seq = n // batch
    tb = _pool_tiles(seq)
    nt = seq // tb
    gc = c // len(POOL_WINDOWS)
    hb = tb // POOL_HALO

    def body(x_ref, halo_ref, w_ref, sc_ref, y_ref, q_ref):
        t = pl.program_id(1)
        halo = jnp.where(t == 0, 0.0, halo_ref[...])
        full = jnp.concatenate([halo, x_ref[...]], axis=0)
        pos = lax.broadcasted_iota(jnp.int32, (tb, gc), 0) + t * tb + 1
        for gi, win in enumerate(POOL_WINDOWS):
            cols = slice(gi * gc, (gi + 1) * gc)
            sums = _window_sums(full[:, cols], gi + 1, True)[POOL_HALO:, :]
            cnt = jnp.minimum(pos, win).astype(F32)
            q = sums / cnt - x_ref[:, cols]
            r = jnp.dot(q.astype(BF16), w_ref[gi].astype(BF16), preferred_element_type=F32)
            q_ref[:, cols] = q.astype(q_ref.dtype)
            y_ref[:, cols] = (r * sc_ref[:, cols]).astype(y_ref.dtype)

    return pl.pallas_call(
        body, name=name,
        grid=(batch, nt),
        in_specs=[pl.BlockSpec((tb, c), lambda b, t: (b * nt + t, 0)),
                  pl.BlockSpec((POOL_HALO, c), lambda b, t: (jnp.maximum((b * nt + t) * hb - 1, 0), 0)),
                  pl.BlockSpec(w_pool.shape, lambda b, t: (0, 0, 0)),
                  pl.BlockSpec((1, c), lambda b, t: (0, 0))],
        out_specs=[pl.BlockSpec((tb, c), lambda b, t: (b * nt + t, 0))] * 2,
        out_shape=[jax.ShapeDtypeStruct((n, c), BF16)] * 2,
        compiler_params=_params("parallel", "arbitrary"),
    )(u, u, w_pool, scale)


def pool_bwd(dy, q, w_pool, scale, batch, name):
    n, c = dy.shape
    seq = n // batch
    tb = _pool_tiles(seq)
    nt = seq // tb
    ng = len(POOL_WINDOWS)
    gc = c // ng
    hb = tb // POOL_HALO
    n_blocks = n // POOL_HALO

    def body(dy_ref, dyh_ref, q_ref, w_ref, sc_ref, du_ref, dw_ref, dsc_ref):
        b, t = pl.program_id(0), pl.program_id(1)
        last = t == nt - 1
        dy_full = jnp.concatenate([dy_ref[...], jnp.where(last, 0.0, dyh_ref[...])], axis=0)
        pos = lax.broadcasted_iota(jnp.int32, (tb + POOL_HALO, gc), 0) + t * tb + 1
        start = jnp.logical_and(b == 0, t == 0)
        for gi, win in enumerate(POOL_WINDOWS):
            cols = slice(gi * gc, (gi + 1) * gc)
            w = w_ref[gi].astype(BF16)
            dr = dy_full[:, cols] * sc_ref[:, cols]
            dq = lax.dot_general(dr.astype(BF16), w, (((1,), (1,)), ((), ())), preferred_element_type=F32)
            cnt = jnp.minimum(pos, win).astype(F32)
            back = _window_sums(dq / cnt, gi + 1, False)
            du_ref[:, cols] = back[:tb, :] - dq[:tb, :]
            qb = q_ref[:, cols]
            r = jnp.dot(qb, w, preferred_element_type=F32)
            dw = lax.dot_general(qb, dr[:tb, :].astype(BF16), (((0,), (0,)), ((), ())), preferred_element_type=F32)
            dsc = jnp.sum(dy_ref[:, cols] * r, axis=0, keepdims=True)

            @pl.when(start)
            def _(gi=gi, cols=cols, dw=dw, dsc=dsc):
                dw_ref[gi] = dw
                dsc_ref[:, cols] = dsc

            @pl.when(jnp.logical_not(start))
            def _(gi=gi, cols=cols, dw=dw, dsc=dsc):
                dw_ref[gi] += dw
                dsc_ref[:, cols] += dsc

    blk = pl.BlockSpec((tb, c), lambda b, t: (b * nt + t, 0))
    halo = pl.BlockSpec((POOL_HALO, c), lambda b, t: (jnp.minimum((b * nt + t + 1) * hb, n_blocks - 1), 0))
    return pl.pallas_call(
        body, name=name,
        grid=(batch, nt),
        in_specs=[blk, halo, blk,
                  pl.BlockSpec(w_pool.shape, lambda b, t: (0, 0, 0)),
                  pl.BlockSpec((1, c), lambda b, t: (0, 0))],
        out_specs=[blk, pl.BlockSpec(w_pool.shape, lambda b, t: (0, 0, 0)), pl.BlockSpec((1, c), lambda b, t: (0, 0))],
        out_shape=[jax.ShapeDtypeStruct((n, c), F32), jax.ShapeDtypeStruct(w_pool.shape, F32),
                   jax.ShapeDtypeStruct((1, c), F32)],
        compiler_params=_params("arbitrary", "arbitrary"),
    )(dy, dy, q, w_pool, scale)


def _place():
    return lax.axis_index("x"), lax.axis_index("y"), lax.axis_index("c")


class GatherPlan:
    def __init__(self, arrs):
        self.ins = list(arrs)
        na = len(arrs)
        self.out_shape = [jax.ShapeDtypeStruct((N_DEV,) + a.shape, a.dtype) for a in arrs]
        self.sems = [pltpu.SemaphoreType.DMA((na, 7)), pltpu.SemaphoreType.DMA((na, 7)), pltpu.SemaphoreType.DMA((na,))]
        self.sizes = [math.prod(a.shape) * a.dtype.itemsize for a in arrs]

    def relay_steps(self, n_steps):
        total, done, steps = sum(self.sizes), 0, []
        for a, size in enumerate(self.sizes):
            done += size
            steps.append((a, min(n_steps - 1, (done * (n_steps - 1)) // total)))
        return steps

    def _copy(self, outs, sems, a, k, block, to, src=None):
        dst = outs[a].at[4 * block[0] + 2 * block[1] + block[2]]
        return pltpu.make_async_remote_copy(
            src_ref=dst if src is None else src, dst_ref=dst,
            send_sem=sems[0].at[a, k], recv_sem=sems[1].at[a, k], device_id=to, device_id_type=MESH)

    @staticmethod
    def _chips(x, y):
        return [(1 - x, y), (x, 1 - y), (1 - x, 1 - y)]

    def _local(self, ins, outs, sems, a, me):
        return pltpu.make_async_copy(ins[a], outs[a].at[4 * me[0] + 2 * me[1] + me[2]], sems[2].at[a])

    def start(self, ins, outs, sems):
        x, y, c = _place()
        me = (x, y, c)
        for a in range(len(ins)):
            self._local(ins, outs, sems, a, me).start()
            self._copy(outs, sems, a, 0, me, (x, y, 1 - c), src=ins[a]).start()
            for j, chip in enumerate(self._chips(x, y)):
                self._copy(outs, sems, a, 1 + j, me, (*chip, c), src=ins[a]).start()

    def relay(self, a, ins, outs, sems):
        x, y, c = _place()
        for j, chip in enumerate(self._chips(x, y)):
            self._copy(outs, sems, a, 1 + j, (*chip, c), (x, y, c)).wait_recv()
            self._copy(outs, sems, a, 4 + j, (*chip, c), (x, y, 1 - c)).start()

    def finish(self, ins, outs, sems):
        x, y, c = _place()
        me, sibling = (x, y, c), (x, y, 1 - c)
        for a in range(len(ins)):
            self._copy(outs, sems, a, 0, sibling, me).wait_recv()
            for j, chip in enumerate(self._chips(x, y)):
                self._copy(outs, sems, a, 4 + j, (*chip, 1 - c), me).wait_recv()
        for a in range(len(ins)):
            self._copy(outs, sems, a, 0, me, sibling, src=ins[a]).wait_send()
            for j, chip in enumerate(self._chips(x, y)):
                self._copy(outs, sems, a, 1 + j, me, (*chip, c), src=ins[a]).wait_send()
                self._copy(outs, sems, a, 4 + j, (*chip, c), sibling).wait_send()
            self._local(ins, outs, sems, a, me).wait()


class ChipScatterPlan:
    def __init__(self, arrs):
        self.groups = [list(a) if isinstance(a, list) else [a] for a in arrs]
        self.ins = [piece for group in self.groups for piece in group]
        self.first = [sum(len(g) for g in self.groups[:a]) for a in range(len(self.groups))]
        na = len(arrs)
        self.out_shape = [jax.ShapeDtypeStruct((4,) + g[0].shape[1:], g[0].dtype) for g in self.groups]
        self.sems = [pltpu.SemaphoreType.DMA((na, 3)), pltpu.SemaphoreType.DMA((na, 3)), pltpu.SemaphoreType.DMA((na,))]

    def relay_steps(self, n_steps):
        return []

    def _row(self, ins, a, px, py):
        if len(self.groups[a]) == 1:
            return ins[self.first[a]].at[2 * px + py]
        return ins[self.first[a] + px].at[py]

    def start(self, ins, outs, sems):
        x, y, c = _place()
        mine = 2 * x + y
        for xs in (0, 1):
            @pl.when(x == xs)
            def _(xs=xs):
                for a in range(len(self.groups)):
                    pltpu.make_async_copy(self._row(ins, a, xs, y), outs[a].at[mine], sems[2].at[a]).start()
                    for j, (px, py) in enumerate([(1 - xs, y), (xs, 1 - y), (1 - xs, 1 - y)]):
                        pltpu.make_async_remote_copy(
                            src_ref=self._row(ins, a, px, py), dst_ref=outs[a].at[mine],
                            send_sem=sems[0].at[a, j], recv_sem=sems[1].at[a, j],
                            device_id=(px, py, c), device_id_type=MESH).start()

    def finish(self, ins, outs, sems):
        x, y, c = _place()
        for wait_recv in (True, False):
            for a in range(len(self.groups)):
                for j in range(3):
                    cp = pltpu.make_async_remote_copy(
                        src_ref=self._row(ins, a, 0, 0), dst_ref=outs[a].at[0],
                        send_sem=sems[0].at[a, j], recv_sem=sems[1].at[a, j],
                        device_id=(x, y, c), device_id_type=MESH)
                    if wait_recv:
                        cp.wait_recv()
                    else:
                        cp.wait_send()
        for a in range(len(self.groups)):
            pltpu.make_async_copy(self._row(ins, a, 0, 0), outs[a].at[0], sems[2].at[a]).wait()


def run_exchange(plan, name):
    n_in, n_out = len(plan.ins), len(plan.out_shape)

    def body(*refs):
        parts = (refs[:n_in], refs[n_in:n_in + n_out], refs[n_in + n_out:])
        plan.start(*parts)
        for part, _ in plan.relay_steps(1):
            plan.relay(part, *parts)
        plan.finish(*parts)

    return pl.pallas_call(
        body, name=name,
        in_specs=[ANY] * n_in, out_specs=[ANY] * n_out,
        out_shape=plan.out_shape, scratch_shapes=plan.sems,
    )(*plan.ins)


def swap_with_sibling(arrs, name):
    na = len(arrs)

    def body(*refs):
        ins, outs = refs[:na], refs[na:2 * na]
        send_sems, recv_sems = refs[2 * na:]
        x, y, c = _place()
        copies = [pltpu.make_async_remote_copy(
            src_ref=ins[a], dst_ref=outs[a], send_sem=send_sems.at[a], recv_sem=recv_sems.at[a],
            device_id=(x, y, 1 - c), device_id_type=MESH) for a in range(na)]
        for cp in copies:
            cp.start()
        for cp in copies:
            cp.wait()

    return pl.pallas_call(
        body, name=name,
        in_specs=[ANY] * na, out_specs=[ANY] * na,
        out_shape=[jax.ShapeDtypeStruct(a.shape, a.dtype) for a in arrs],
        scratch_shapes=[pltpu.SemaphoreType.DMA((na,)), pltpu.SemaphoreType.DMA((na,))],
    )(*arrs)


def adamw(w, gparts, m, v, name, tr=256):
    rows, cols = w.shape
    parts = gparts.shape[0]
    tr = _row_tile(rows, tr)
    c1 =1.0 - ADAM_B1 ** ADAM_STEP
    c2 = 1.0 - ADAM_B2 ** ADAM_STEP

    def body(w_ref, g_ref, m_ref, v_ref, go_ref, d_ref, mo_ref, vo_ref):
        g = g_ref[0].astype(F32)
        for p_ in range(1, parts):
            g = g + g_ref[p_].astype(F32)
        m_new = ADAM_B1 * m_ref[...] + (1.0 - ADAM_B1) * g
        v_new = ADAM_B2 * v_ref[...] + (1.0 - ADAM_B2) * (g * g)
        m_hat = m_new / c1
        v_hat = v_new / c2
        go_ref[...] = g
        d_ref[...] = -ADAM_LR * (m_hat / (jnp.sqrt(v_hat) + ADAM_EPS) + ADAM_WD * w_ref[...])
        mo_ref[...] = m_new
        vo_ref[...] = v_new

    blk = pl.BlockSpec((tr, cols), lambda i: (i, 0))
    return pl.pallas_call(
        body, name=name,
        grid=(rows // tr,),
        in_specs=[blk, pl.BlockSpec((parts, tr, cols), lambda i: (0, i, 0)), blk, blk],
        out_specs=[blk] * 4,
        out_shape=[jax.ShapeDtypeStruct((rows, cols), F32)] * 4,
        compiler_params=_params("parallel"),
    )(w, gparts, m, v)


def add2(a, b, name, out_dtype):
    return rowwise(lambda a, b: ([a.astype(F32) + b.astype(F32)], []), [a, b], [(a.shape[1], out_dtype)],
                   name=name, tr=256)[0]


def _block_diag(x):
    g, a, b = x.shape
    eye = jnp.eye(g, dtype=x.dtype)
    return (x[:, :, None, :] * eye[:, None, :, None]).reshape(g * a, g * b)


def _diag_blocks(x, a, b):
    per = x.shape[1] // b
    x5 = x.reshape(SSM_SUPER, per, a, per, b)
    eye = jnp.eye(per, dtype=x.dtype)
    return jnp.sum(x5 * eye[None, :, None, :, None], axis=3).reshape(SSM_SUPER * per, a, b)


def _swiglu_epi(g, u):
    s = _sigmoid(g)
    silu = g * s
    return u * (s * (1.0 + g * (1.0 - s))), silu, silu * u


def _residual_epi(scale, with_norm):
    if with_norm:
        def epi(acc, res, gain):
            out = res + scale * acc
            return out, _rms_tile(out, gain)
    else:
        def epi(acc, res):
            return (res + scale * acc,)
    return epi


def _next_norm(next_gain):
    if next_gain is None:
        return [], (F32,)
    return [next_gain], (F32, BF16)


FFN_WIDE = 2816


def ffn_fwd(h, n, wi, wo, next_gain, tag, exchange=None):
    res = matmul(n, [(wi, 0), (wi, 1)], mode="nn", name=f"{tag}_in", separate=True, epi=_swiglu_epi,
                 out_dtypes=(BF16, BF16, BF16), exchange=exchange, tm=512, tn=FFN_WIDE)
    (dact_g, dact_u, act), exchanged = res if exchange is not None else (res, [])
    more, dtypes = _next_norm(next_gain)
    res = matmul(act, wo, mode="nn", name=f"{tag}_out", epi=_residual_epi(0.5, bool(more)), extras=[h] + more,
                 out_dtypes=dtypes, tm=512, tk=FFN_WIDE)
    return res[0], (res[1] if more else None), (h, n, dact_g, dact_u, act), exchanged


def ffn_bwd(dh, saved, gain, wi, wo, tag, carry=None, gate=None, on_weight_grads=None):
    h, n, dact_g, dact_u, act = saved
    if carry is None:
        carry = lambda where, run: run(None)[0]
    dg, du = carry(f"{tag}_out_dx", lambda ex: _with_exchange(matmul(
        dh, wo, mode="nt", name=f"{tag}_out_dx", extras=[dact_g, dact_u], out_dtypes=(BF16, BF16),
        epi=lambda acc, fg, fu: (0.5 * acc * fg, 0.5 * acc * fu), exchange=ex, tm=512, tn=FFN_WIDE), ex))
    d_wo, = carry(f"{tag}_out_dw", lambda ex: _with_exchange(matmul(
        act, dh, mode="tn", name=f"{tag}_out_dw", scale=0.5, exchange=ex, tm=FFN_WIDE), ex))
    d_wi = [carry(f"{tag}_in_dw_{k}", lambda ex, half=half: _with_exchange(matmul(
        n, half, mode="tn", name=f"{tag}_in_dw", exchange=ex, tn=FFN_WIDE), ex))[0] for k, half in (("g", dg), ("u", du))]
    if on_weight_grads is not None:
        on_weight_grads(d_wi, d_wo)
    if gate is None:
        epi, more, dtypes = _rms_bwd_tile, [], (F32,)
    else:
        def epi(acc, x, g, dres, e, pre):
            dh_new, d_gain = _rms_bwd_tile(acc, x, g, dres)
            return (dh_new, *_gate_cotangents(dh_new, e, pre), d_gain)
        more, dtypes = list(gate), (F32, BF16, BF16)
    res = carry(f"{tag}_in_dx", lambda ex: _with_exchange(matmul(
        [dg, du], [(wi, 0), (wi, 1)], mode="nt", name=f"{tag}_in_dx", epi=epi, extras=[h, gain, dh] + more,
        out_dtypes=dtypes, col_sums=1, exchange=ex, tm=256, tk=FFN_WIDE), ex))
    return res[0], res[-1], d_wi, d_wo, tuple(res[1:-1])


def _with_exchange(result, exchange):
    return result if exchange is not None else (result, [])


def mix_fwd(h, n, lw, batch, next_gain, tag, exchange=None):
    sw = h.shape[1] // 2
    us, up = matmul(n, [lw["w_in"][:, :sw], lw["w_in"][:, sw:]], mode="nn", name=f"{tag}_in", separate=True,
                    out_dtypes=(F32, F32))
    lam_r, lam_i, bb_r, bb_i = ssm_prep(lw["lam_re"], lw["lam_im"], lw["log_dt"], lw["b_re"], lw["b_im"], f"{tag}_zoh")
    lam = (lam_r.reshape(1, SSM_CH), lam_i.reshape(1, SSM_CH))
    b_mats = [_block_diag(bb.reshape(SSM_GROUPS, SSM_STATE, SSM_GROUP_CH).transpose(0, 2, 1)).astype(BF16)
              for bb in (bb_r, bb_i)]
    c_mats = [_block_diag(cc.transpose(0, 2, 1)).astype(BF16) for cc in (lw["c_re"], -lw["c_im"])]
    bu_re, bu_im = matmul(us, b_mats, mode="nn", name=f"{tag}_bu", separate=True, diag=SSM_SUPER,
                          out_dtypes=(BF16, BF16))
    s_re, s_im, exchanged = ssm_scan(bu_re, bu_im, *lam, batch, f"{tag}_scan", exchange)
    y0, y1 = matmul([s_re, s_im], c_mats, mode="nn", name=f"{tag}_c", diag=SSM_SUPER,
                    epi=lambda acc, u, d: (acc + d * u, _gelu(acc + d * u)), extras=[us, lw["ssm_d"]],
                    out_dtypes=(F32, BF16))
    y2, gl = matmul(y1, lw["w_glu"], mode="nn", name=f"{tag}_glu",
                    epi=lambda acc, y0: (_gelu(y0) * _sigmoid(acc), acc), extras=[y0], out_dtypes=(BF16, F32))
    yp, q = pool_fwd(up, lw["pool_w"], lw["pool_scale"], batch, f"{tag}_pool")
    more, dtypes = _next_norm(next_gain)
    res = matmul([y2, yp], [lw["w_out"][:sw], lw["w_out"][sw:]], mode="nn", name=f"{tag}_out",
                 epi=_residual_epi(1.0, bool(more)), extras=[h] + more, out_dtypes=dtypes)
    saved = (h, n, us, lam, b_mats, c_mats, s_re, s_im, y0, y1, gl, y2, yp, q)
    return res[0], (res[1] if more else None), saved, exchanged


def mix_bwd(dh, saved, lw, batch, tag, exchange=None):
    h, n, us, lam, b_mats, c_mats, s_re, s_im, y0, y1, gl, y2, yp, q = saved
    sw = h.shape[1] // 2
    w_out_s, w_out_p = lw["w_out"][:sw], lw["w_out"][sw:]
    d_wo_s, d_wo_p = matmul([y2, yp], dh, mode="tn", name=f"{tag}_out_dw", separate=True)
    def out_dx_epi(dy2, dyp, y0, gl):
        sg = _sigmoid(gl)
        return dy2, dyp, dy2 * _gelu(y0) * sg * (1.0 - sg)

    dy2, dyp, tg = matmul(dh, [w_out_s, w_out_p], mode="nt", name=f"{tag}_out_dx", separate=True, epi=out_dx_epi,
                          extras=[y0, gl], out_dtypes=(F32, F32, BF16))
    dup, d_pool_w, d_pool_scale = pool_bwd(dyp, q, lw["pool_w"], lw["pool_scale"], batch, f"{tag}_pool_bwd")
    def dy0_epi(acc, dy2, gl, y0, u):
        dy0 = (acc + dy2 * _sigmoid(gl)) * _gelu_grad(y0)
        return dy0, jnp.sum(dy0 * u, axis=0, keepdims=True)

    dy0, d_d = matmul(tg, lw["w_glu"], mode="nt", name=f"{tag}_glu_dx", epi=dy0_epi, extras=[dy2, gl, y0, us],
                      out_dtypes=(F32,), col_sums=1)
    d_w_glu, = matmul(y1, tg, mode="tn", name=f"{tag}_glu_dw")
    gd_re, gd_im = matmul(dy0, c_mats, mode="nt", name=f"{tag}_c_dx", separate=True, diag=SSM_SUPER,
                          out_dtypes=(BF16, BF16))
    d_c_top, d_c_bot = matmul([s_re, s_im], dy0, mode="tn", name=f"{tag}_c_dw", separate=True, diag=SSM_SUPER)
    g_re, g_im, d_lam_r, d_lam_i, exchanged = ssm_scan_bwd(gd_re, gd_im, s_re, s_im, *lam, batch, f"{tag}_scan_bwd",
                                                           exchange)
    dus, = matmul([g_re, g_im], b_mats, mode="nt", name=f"{tag}_bu_dx", diag=SSM_SUPER,
                  epi=lambda acc, dy0, d: (acc + d * dy0,), extras=[dy0, lw["ssm_d"]])
    d_b_re, d_b_im = matmul(us, [g_re, g_im], mode="tn", name=f"{tag}_bu_dw", separate=True, diag=SSM_SUPER)
    d_bb_r = _diag_blocks(d_b_re, SSM_GROUP_CH, SSM_STATE).transpose(0, 2, 1).reshape(SSM_CH, SSM_GROUP_CH)
    d_bb_i = _diag_blocks(d_b_im, SSM_GROUP_CH, SSM_STATE).transpose(0, 2, 1).reshape(SSM_CH, SSM_GROUP_CH)
    d_lr, d_li, d_ldt, d_br, d_bi = ssm_prep_bwd(
        lw["lam_re"], lw["lam_im"], lw["log_dt"], lw["b_re"], lw["b_im"],
        d_lam_r.reshape(SSM_CH, 1), d_lam_i.reshape(SSM_CH, 1), d_bb_r, d_bb_i, f"{tag}_zoh_bwd")
    d_c_re = _diag_blocks(d_c_top, SSM_STATE, SSM_GROUP_CH).transpose(0, 2, 1)
    d_c_im = -_diag_blocks(d_c_bot, SSM_STATE, SSM_GROUP_CH).transpose(0, 2, 1)
    d_w_in_s, d_w_in_p = matmul(n, [dus, dup], mode="tn", name=f"{tag}_in_dw", separate=True)
    dh_new, d_gain = matmul([dus, dup], [lw["w_in"][:, :sw], lw["w_in"][:, sw:]], mode="nt", name=f"{tag}_in_dx",
                            epi=_rms_bwd_tile, extras=[h, lw["mix_norm"], dh], out_dtypes=(F32,), col_sums=1, tm=512)
    grads = dict(mix_norm=d_gain, w_in=jnp.concatenate([d_w_in_s, d_w_in_p], axis=1),
                 ssm_lambda_re=d_lr, ssm_lambda_im=d_li, ssm_log_dt=d_ldt, ssm_b_re=d_br, ssm_b_im=d_bi,
                 ssm_c_re=d_c_re, ssm_c_im=d_c_im, ssm_d=d_d, ssm_w_glu=d_w_glu, pool_w=d_pool_w,
                 pool_scale=d_pool_scale, w_out=jnp.concatenate([d_wo_s, d_wo_p], axis=0))
    return dh_new, grads, exchanged


def ple_fwd(h, n, p, w_gate, w_proj, next_gain, tag):
    e, = matmul(p, w_proj, mode="nn", name=f"{tag}_proj")
    if next_gain is None:
        def epi(acc, e, res):
            return res + _sigmoid(acc) * e, acc
        more, dtypes = [], (F32, F32)
    else:
        def epi(acc, e, res, gain):
            out = res + _sigmoid(acc) * e
            return out, acc, _rms_tile(out, gain)
        more, dtypes = [next_gain], (F32, F32, BF16)
    res = matmul(n, w_gate, mode="nn", name=f"{tag}_gate", epi=epi, extras=[e, h] + more, out_dtypes=dtypes, tm=512)
    return res[0], (res[2] if more else None), (h, n, e, res[1])


def _gate_cotangents(dh, e, pre):
    s = _sigmoid(pre)
    return dh * e * s * (1.0 - s), dh * s


def ple_bwd(dh, dpre, de, saved, p, gain, w_gate, tag):
    h, n, e, pre = saved
    d_w_gate, = matmul(n, dpre, mode="tn", name=f"{tag}_gate_dw")
    d_w_proj, = matmul(p, de, mode="tn", name=f"{tag}_proj_dw")
    dh_new, d_gain = matmul(dpre, w_gate, mode="nt", name=f"{tag}_gate_dx", epi=_rms_bwd_tile,
                            extras=[h, gain, dh], out_dtypes=(F32,), col_sums=1, tm=512)
    return dh_new, d_gain, d_w_gate, d_w_proj


def loss_head(h, gain, target, e, pre, name):
    d = h.shape[1]

    def fn(h, g, t, e, pre):
        r = lax.rsqrt(jnp.mean(h * h, axis=-1, keepdims=True) + EPS)
        diff = h * r * g - t
        sq = jnp.sum(jnp.sum(diff * diff, axis=1, keepdims=True), axis=0, keepdims=True)
        dy = diff * (1.0 / d)
        w = dy * g
        dh = r * w - h * (r * r * r) * jnp.mean(h * w, axis=-1, keepdims=True)
        return [dh, *_gate_cotangents(dh, e, pre)], [sq, jnp.sum(dy * (h * r), axis=0, keepdims=True)]

    dh, dpre, de, sq, d_gain = rowwise(fn, [h, gain, target, e, pre], [(d, F32), (d, BF16), (d, BF16)],
                                       [(1, 1), (1, d)], name=name, tr=256)
    return 0.5 / d * sq[0, 0], dh, dpre, de, d_gain


SHARDED = {
    "ffn1_wi": 1, "ffn1_wo": 0, "w_in": 0, "ssm_w_glu": 0, "w_out": 0, "ffn2_wi": 1, "ffn2_wo": 0,
    "ple_w_gate": 0, "ple_w_proj": 1,
}
WEIGHTS = ["ffn1_norm", "ffn1_wi", "ffn1_wo", "mix_norm", "w_in", "ssm_lambda_re", "ssm_lambda_im", "ssm_log_dt",
           "ssm_b_re", "ssm_b_im", "ssm_c_re", "ssm_c_im", "ssm_d", "ssm_w_glu", "pool_w", "pool_scale", "w_out",
           "ffn2_norm", "ffn2_wi", "ffn2_wo", "ple_norm", "ple_w_gate", "ple_w_proj", "final_norm"]
REPLICATED = [n for n in WEIGHTS if n not in SHARDED]


HALVED = ("ffn1_wi", "ffn2_wi")


def _unshard(gathered, axis, halved):
    if halved:
        _, rows, cols = gathered.shape
        return gathered.reshape(2, 4, rows, cols).transpose(0, 2, 1, 3).reshape(2, rows, 4 * cols)
    g = jnp.moveaxis(gathered, 0, axis)
    shp = g.shape
    return g.reshape(shp[:axis] + (shp[axis] * shp[axis + 1],) + shp[axis + 2:])


def _split_for_scatter(full, axis, c, halved):
    if halved:
        rows, cols = full[0].shape

        def pick(cc):
            return [lax.dynamic_index_in_dim(h.reshape(rows, 2, 2, cols // 4), cc, 2, keepdims=False).transpose(1, 0, 2)
                    for h in full]

        return pick(c), [s.astype(BF16) for s in pick(1 - c)]
    shp = full.shape
    g = full.reshape(shp[:axis] + (4, 2, shp[axis] // N_DEV) + shp[axis + 1:])
    keep = lax.dynamic_index_in_dim(g, c, axis + 1, keepdims=False)
    send = lax.dynamic_index_in_dim(g, 1 - c, axis + 1, keepdims=False)
    return jnp.moveaxis(keep, axis, 0), jnp.moveaxis(send, axis, 0).astype(BF16)


def _pack(arrs):
    pieces = []
    for a in arrs:
        flat = a.reshape(-1)
        pad = (-flat.shape[0]) % PACK
        pieces.append(jnp.pad(flat, (0, pad)).reshape(-1, LANES))
    return jnp.concatenate(pieces, axis=0)


def _unpack(packed, shapes):
    out, row = [], 0
    for s in shapes:
        size = math.prod(s)
        rows = (size + PACK - 1) // PACK * SUBLANES
        out.append(packed[row:row + rows].reshape(-1)[:size].reshape(s))
        row += rows
    return out


class NoExchange:
    def __init__(self, full):
        self.full = full

    def big(self, i, name):
        w = self.full[name][i]
        if name in HALVED:
            rows, cols = w.shape
            w = w.reshape(rows, 2, cols // 2).transpose(1, 0, 2)
        return w

    def fwd_exchange(self, i, where):
        return None

    def fwd_done(self, i, where, results):
        pass

    def bwd_exchange(self, i, where):
        return None

    def bwd_done(self, i, where, results):
        pass

    def layer_grads(self, i, grads, names):
        pass

    def small_grads(self, i, grads):
        pass

    def final_grad(self, d_final):
        pass


def local_step(x, p, target, rep, hooks):
    batch, seq, d = x.shape
    n_tok = batch * seq
    h = x.reshape(n_tok, d)
    saved = []
    n = rms_fwd(h, rep["ffn1_norm"][0].reshape(1, d), "first_norm")
    for i in range(DEPTH):
        lw = _layer_weights(rep, i, d)
        big = lambda name, i=i: hooks.big(i, name)
        next_gain = rep["ffn1_norm"][i + 1].reshape(1, d) if i + 1 < DEPTH else None
        h, n, s1, exchanged = ffn_fwd(h, n, big("ffn1_wi"), big("ffn1_wo"), lw["mix_norm"], "ffn1",
                                      hooks.fwd_exchange(i, "ffn1_in"))
        hooks.fwd_done(i, "ffn1_in", exchanged)
        lw.update(w_in=big("w_in"), w_glu=big("ssm_w_glu"), w_out=big("w_out"))
        h, n, s2, exchanged = mix_fwd(h, n, lw, batch, lw["ffn2_norm"], "mix", hooks.fwd_exchange(i, "scan"))
        hooks.fwd_done(i, "scan", exchanged)
        h, n, s3, exchanged = ffn_fwd(h, n, big("ffn2_wi"), big("ffn2_wo"), lw["ple_norm"], "ffn2",
                                      hooks.fwd_exchange(i, "ffn2_in"))
        hooks.fwd_done(i, "ffn2_in", exchanged)
        p_i = p[i].reshape(n_tok, -1)
        h, n, s4 = ple_fwd(h, n, p_i, big("ple_w_gate"), big("ple_w_proj"), next_gain, "ple")
        saved.append((s1, s2, s3, s4, p_i))
    last_gate = saved[-1][3][2:]
    loss, dh, dpre, de, d_final = loss_head(h, rep["final_norm"].reshape(1, d), target.reshape(n_tok, d), *last_gate,
                                            "loss_head")
    hooks.final_grad(d_final)
    per_layer = [None] * DEPTH
    for i in reversed(range(DEPTH)):
        lw = _layer_weights(rep, i, d)
        big = lambda name, i=i: hooks.big(i, name)
        lw.update(w_in=big("w_in"), w_glu=big("ssm_w_glu"), w_out=big("w_out"))
        s1, s2, s3, s4, p_i = saved[i]
        g = {}
        dh, g["ple_norm"], g["ple_w_gate"], g["ple_w_proj"] = ple_bwd(dh, dpre, de, s4, p_i, lw["ple_norm"],
                                                                     big("ple_w_gate"), "ple")
        def carry(where, run, i=i):
            outs, exchanged = run(hooks.bwd_exchange(i, where))
            hooks.bwd_done(i, where, exchanged)
            return outs

        dh, g["ffn2_norm"], g["ffn2_wi"], g["ffn2_wo"], _ = ffn_bwd(
            dh, s3, lw["ffn2_norm"], big("ffn2_wi"), big("ffn2_wo"), "ffn2", carry)
        dh, gm, exchanged = mix_bwd(dh, s2, lw, batch, "mix", hooks.bwd_exchange(i, "scan"))
        hooks.bwd_done(i, "scan", exchanged)
        g.update(gm)
        hooks.layer_grads(i, g, [k for k in SHARDED if k not in LAST_GRADS])
        hooks.small_grads(i, {k: g[k] for k in REPLICATED if k in g})
        below = saved[i - 1][3][2:] if i > 0 else None

        def ffn1_grads(d_wi, d_wo, i=i, g=g):
            g["ffn1_wi"], g["ffn1_wo"] = d_wi, d_wo
            hooks.layer_grads(i, g, list(LAST_GRADS))

        dh, g["ffn1_norm"], _, _, gate_ct = ffn_bwd(
            dh, s1, lw["ffn1_norm"], big("ffn1_wi"), big("ffn1_wo"), "ffn1", carry, below, ffn1_grads)
        hooks.small_grads(i, {"ffn1_norm": g["ffn1_norm"]})
        if i > 0:
            dpre, de = gate_ct
        per_layer[i] = g
    return loss, dh.reshape(batch, seq, d), per_layer, d_final


LAST_GRADS = ("ffn1_wi", "ffn1_wo")


def _layer_weights(w, i, d):
    sw = d // 2
    lw = {}
    lw["pool_w"] = w["pool_w"][i]
    for k in ("ffn1_norm", "mix_norm", "ffn2_norm", "ple_norm"):
        lw[k] = w[k][i].reshape(1, d)
    lw["ssm_d"] = w["ssm_d"][i].reshape(1, sw)
    lw["pool_scale"] = w["pool_scale"][i].reshape(1, sw)
    lw["lam_re"] = w["ssm_lambda_re"][i].reshape(SSM_CH, 1)
    lw["lam_im"] = w["ssm_lambda_im"][i].reshape(SSM_CH, 1)
    lw["log_dt"] = jnp.repeat(w["ssm_log_dt"][i], SSM_STATE).reshape(SSM_CH, 1)
    lw["b_re"] = w["ssm_b_re"][i].reshape(SSM_CH, SSM_GROUP_CH)
    lw["b_im"] = w["ssm_b_im"][i].reshape(SSM_CH, SSM_GROUP_CH)
    lw["c_re"] = w["ssm_c_re"][i]
    lw["c_im"] = w["ssm_c_im"][i]
    return lw


class MeshExchange:
    FIRST = ("ffn1_wi", "ffn1_wo")
    FWD_PLAN = {"scan": ("ffn1_wi", "ffn1_wo", "w_in", "ssm_w_glu", "w_out"),
                "ffn2_in": ("ffn2_wi", "ffn2_wo", "ple_w_gate", "ple_w_proj")}
    BWD_PLAN = {"ffn2_in_dx": ("ffn1_wo", "ffn2_wo", "ssm_w_glu", "w_out", "ple_w_gate", "ple_w_proj"),
                "scan": ("ffn1_wi", "ffn2_wi", "w_in")}
    LAST_LAYER_PLAN = {"ffn1_out_dx": ("ffn2_wi", "w_in"),
                       "ffn1_out_dw": ("ffn2_wo", "w_out", "ple_w_gate", "ssm_w_glu", "ple_w_proj"),
                       "ffn1_in_dx": LAST_GRADS}

    def __init__(self, shards):
        self.small = {}
        self.d_final = None
        self.small_main = None
        self.shards = shards
        self.c = lax.axis_index("c")
        self.gathered = {}
        self.chip_sums = {}
        self.from_chips = {}
        self.pending = None
        first = run_exchange(GatherPlan([shards[0][k] for k in self.FIRST]), "gather_first_weights")
        self._store(self.gathered, 0, self.FIRST, first)

    @staticmethod
    def _store(where, layer, names, results):
        for k, r in zip(names, results):
            where[(layer, k)] = r

    def big(self, i, name):
        return _unshard(self.gathered[(i, name)], SHARDED[name], name in HALVED)

    def fwd_exchange(self, i, where):
        if i == 0 and where == "ffn1_in":
            layer, names = 0, [k for k in SHARDED if k not in self.FIRST]
        elif where in self.FWD_PLAN and i + 1 < DEPTH:
            layer, names = i + 1, self.FWD_PLAN[where]
        else:
            return None
        self.pending = (layer, names)
        return GatherPlan([self.shards[layer][k] for k in names])

    def fwd_done(self, i, where, results):
        if results:
            self._store(self.gathered, *self.pending, results)

    def layer_grads(self, i, grads, names):
        pieces, sends = [], []
        for k in names:
            keep, send = _split_for_scatter(grads[k], SHARDED[k], self.c, k in HALVED)
            keep, send = (keep, send) if isinstance(keep, list) else ([keep], [send])
            pieces.append(keep)
            sends += send
        got = iter(swap_with_sibling(sends, "reduce_core_pair"))
        for k, keep in zip(names, pieces):
            sums = []
            for part in keep:
                cols = part.shape[-1]
                sums.append(add2(part.reshape(-1, cols), next(got).reshape(-1, cols), f"sum_core_pair_{k}",
                                 BF16).reshape(part.shape))
            self.chip_sums[(i, k)] = sums if len(sums) > 1 else sums[0]

    def small_grads(self, i, grads):
        self.small.setdefault(i, {}).update(grads)

    def final_grad(self, d_final):
        self.d_final = d_final

    def _small_pack(self, d_final, first_norm_grad=None):
        pieces = []
        for k in REPLICATED:
            if k == "final_norm":
                pieces.append(d_final)
            elif k == "ffn1_norm" and first_norm_grad is None:
                pieces.append(jnp.stack([self.small[i][k] for i in range(1, DEPTH)], axis=0))
            else:
                pieces.append(jnp.stack([self.small[i][k] for i in range(DEPTH)], axis=0))
        return _pack(pieces)

    def bwd_exchange(self, i, where):
        if where in self.BWD_PLAN and i + 1 < DEPTH:
            layer, names = i + 1, self.BWD_PLAN[where]
        elif i == 0 and where in self.LAST_LAYER_PLAN:
            layer, names = 0, self.LAST_LAYER_PLAN[where]
        elif i == 0 and where == "ffn1_in_dw_g":
            self.pending = "small"
            return GatherPlan([self._small_pack(self.d_final)])
        else:
            return None
        self.pending = (layer, names)
        return ChipScatterPlan([self.chip_sums[(layer, k)] for k in names])

    def bwd_done(self, i, where, results):
        if not results:
            return
        if self.pending == "small":
            self.small_main, = results
        else:
            self._store(self.from_chips, *self.pending, results)

    def small_gathered(self):
        first = self.small[0]["ffn1_norm"].reshape(SUBLANES, LANES)
        first_all, = run_exchange(GatherPlan([first]), "gather_first_gain_grad")
        return jnp.concatenate([first_all, self.small_main], axis=1)


def kernel(x, p, ffn1_norm, ffn1_wi, ffn1_wo, mix_norm, w_in, ssm_lambda_re, ssm_lambda_im, ssm_log_dt, ssm_b_re, ssm_b_im, ssm_c_re, ssm_c_im, ssm_d, ssm_w_glu, pool_w, pool_scale, w_out, ffn2_norm, ffn2_wi, ffn2_wo, ple_norm, ple_w_gate, ple_w_proj, final_norm, loss_target, m_ffn1_norm, m_ffn1_wi, m_ffn1_wo, m_mix_norm, m_w_in, m_ssm_lambda_re, m_ssm_lambda_im, m_ssm_log_dt, m_ssm_b_re, m_ssm_b_im, m_ssm_c_re, m_ssm_c_im, m_ssm_d, m_ssm_w_glu, m_pool_w, m_pool_scale, m_w_out, m_ffn2_norm, m_ffn2_wi, m_ffn2_wo, m_ple_norm, m_ple_w_gate, m_ple_w_proj, m_final_norm, v_ffn1_norm, v_ffn1_wi, v_ffn1_wo, v_mix_norm, v_w_in, v_ssm_lambda_re, v_ssm_lambda_im, v_ssm_log_dt, v_ssm_b_re, v_ssm_b_im, v_ssm_c_re, v_ssm_c_im, v_ssm_d, v_ssm_w_glu, v_pool_w, v_pool_scale, v_w_out, v_ffn2_norm, v_ffn2_wi, v_ffn2_wo, v_ple_norm, v_ple_w_gate, v_ple_w_proj, v_final_norm):
    args = dict(locals())
    wts = {k: args[k] for k in WEIGHTS}
    rep = {k: wts[k] for k in REPLICATED}

    shards = [{k: wts[k][i].astype(BF16) for k in SHARDED} for i in range(DEPTH)]
    exchange = MeshExchange(shards)
    loss_local, grad_x, per_layer, d_final = local_step(x, p, loss_target, rep, exchange)
    loss = lax.psum(loss_local, ("x", "y", "c"))

    outs = {}
    for k in SHARDED:
        shp = wts[k].shape
        cols = shp[-1]
        parts = jnp.stack([exchange.from_chips[(i, k)] for i in range(DEPTH)], axis=1)
        res = adamw(wts[k].reshape(-1, cols), parts.reshape(4, -1, cols), args["m_" + k].reshape(-1, cols),
                    args["v_" + k].reshape(-1, cols), f"adamw_{k}")
        outs[k] = [r.reshape(shp) for r in res]

    rep_shapes = [wts[k].shape for k in REPLICATED]
    all_g = exchange.small_gathered()
    res = adamw(_pack([wts[k] for k in REPLICATED]), all_g, _pack([args["m_" + k] for k in REPLICATED]),
                _pack([args["v_" + k] for k in REPLICATED]), "adamw_small")
    unpacked = [_unpack(r, rep_shapes) for r in res]
    for j, k in enumerate(REPLICATED):
        outs[k] = [unpacked[q][j] for q in range(4)]

    result = [loss, grad_x]
    for q in range(4):
        result += [outs[k][q] for k in WEIGHTS]
    return tuple(result)
```

```python
import math

import jax
import jax.numpy as jnp
from jax import lax
from jax.experimental import pallas as pl
from jax.experimental.pallas import tpu as pltpu

F32 = jnp.float32
BF16 = jnp.bfloat16
MESH = pl.DeviceIdType.MESH
ANY = pl.BlockSpec(memory_space=pl.ANY)

N_DEV = 8
DEPTH = 4
EPS = 1e-6
SSM_GROUPS = 32
SSM_GROUP_CH = 16
SSM_STATE = 64
SSM_CH = SSM_GROUPS * SSM_STATE
SSM_SUPER = 2
POOL_WINDOWS = (2, 4, 8, 16)
POOL_HALO = 16
ADAM_LR, ADAM_B1, ADAM_B2, ADAM_EPS, ADAM_WD, ADAM_STEP = 0.001, 0.9, 0.999, 1e-08, 0.01, 10

V7X_VMEM_BYTES = 64 * 1024 * 1024
VMEM_LIMIT_BYTES = V7X_VMEM_BYTES - 12 * 1024 * 1024
LANES = 128
SUBLANES = 8
PACK = SUBLANES * LANES


def _params(*sem):
    return pltpu.CompilerParams(dimension_semantics=sem or None, vmem_limit_bytes=VMEM_LIMIT_BYTES)


def _tile(n, pref):
    if n <= pref:
        return n
    t = pref - pref % LANES
    while t >= LANES:
        if n % t == 0:
            return t
        t -= LANES
    raise ValueError(f"no lane-aligned tile for {n}")


def _row_tile(rows, pref):
    if rows <= pref:
        return rows
    t = pref - pref % SUBLANES
    while t >= SUBLANES:
        if rows % t == 0:
            return t
        t -= SUBLANES
    raise ValueError(f"no sublane-aligned tile for {rows}")


_DIMS = {"nn": ((1,), (0,)), "nt": ((1,), (1,)), "tn": ((0,), (0,))}


def matmul(a, b, *, mode, name, out_dtypes=None, epi=None, extras=(), separate=False, diag=1, col_sums=0,
           stack_out=False, scale=1.0, exchange=None, tm=1024, tn=1024, tk=1024):
    a_list = list(a) if isinstance(a, list) else [a]
    b_list = list(b) if isinstance(b, list) else [b]
    a_planes = [x[1] if isinstance(x, tuple) else None for x in a_list]
    b_planes = [x[1] if isinstance(x, tuple) else None for x in b_list]
    a_list = [x[0] if isinstance(x, tuple) else x for x in a_list]
    b_list = [x[0] if isinstance(x, tuple) else x for x in b_list]
    a_shape, b_shape = a_list[0].shape[-2:], b_list[0].shape[-2:]
    n_terms = max(len(a_list), len(b_list))
    a_idx = [0] * n_terms if len(a_list) == 1 else list(range(n_terms))
    b_idx = [0] * n_terms if len(b_list) == 1 else list(range(n_terms))
    n_acc = n_terms if separate else 1
    assert not (stack_out or scale != 1.0) or epi is None
    if out_dtypes is None:
        out_dtypes = (F32,) * (1 if (epi is not None or stack_out) else n_acc)
    in_place = epi is None
    if mode == "tn":
        K, M = a_shape
        K2, N = b_shape
    elif mode == "nt":
        M, K = a_shape
        N, K2 = b_shape
    else:
        M, K = a_shape
        K2, N = b_shape
    assert K == K2, (name, a_shape, b_shape)
    if mode == "tn":
        tm, tn, tk = _tile(M // diag, tm), _tile(N // diag, tn), _tile(K, tk)
        nk = K // tk
        N = N // diag
        row_tiles, col_tiles = (M // diag) // tm, N // tn
        a_blk, a_map = (tk, tm), lambda i, j, k: (k, i)
        b_blk, b_map = (tk, tn), lambda i, j, k: (k, (i // row_tiles) * col_tiles + j)
    else:
        tm, tn, tk = _tile(M, tm), _tile(N // diag, tn), _tile(K // diag, tk)
        nk = (K // diag) // tk
        col_tiles = (N // diag) // tn
        a_blk, a_map = (tm, tk), lambda i, j, k: (i, (j // col_tiles) * nk + k)
        if mode == "nt":
            b_blk, b_map = (tn, tk), lambda i, j, k: (j, (j // col_tiles) * nk + k)
        else:
            b_blk, b_map = (tk, tn), lambda i, j, k: ((j // col_tiles) * nk + k, j)

    def plane_spec(blk, index_map, plane):
        if plane is None:
            return pl.BlockSpec(blk, index_map)
        return pl.BlockSpec((None,) + blk, lambda i, j, k: (plane,) + index_map(i, j, k))

    a_specs = [plane_spec(a_blk, a_map, p_) for p_ in a_planes]
    b_specs = [plane_spec(b_blk, b_map, p_) for p_ in b_planes]
    assert not col_sums or N == tn, (name, N, tn)
    ex_specs = []
    for e in extras:
        if e.shape == (M, N):
            ex_specs.append(pl.BlockSpec((tm, tn), lambda i, j, k: (i, j)))
        elif e.shape == (1, N):
            ex_specs.append(pl.BlockSpec((1, tn), lambda i, j, k: (0, j)))
        elif e.shape == (M, 1):
            ex_specs.append(pl.BlockSpec((tm, 1), lambda i, j, k: (i, 0)))
        else:
            raise ValueError((name, e.shape, (M, N)))
    na, nb, ne, no = len(a_list), len(b_list), len(extras), len(out_dtypes)
    dims = (_DIMS[mode], ((), ()))

    n_scratch = n_acc if (nk > 1 and not in_place) else 0
    x_ins, x_out, x_sems = _exchange_args(exchange)
    grid = (M // tm, N // tn, nk)

    def body(*refs):
        a_refs, b_refs, ex_refs, xin, out_refs, sum_refs, xout, acc_refs, xsems = _split_refs(
            refs, na, nb, ne, len(x_ins), no, col_sums, len(x_out), n_scratch)
        if exchange is not None:
            step = (pl.program_id(0) * grid[1] + pl.program_id(1)) * nk + pl.program_id(2)
            _run_exchange(exchange, step, grid[0] * grid[1] * nk, (xin, xout, xsems))
        a_vals = [r[...].astype(BF16) for r in a_refs]
        b_vals = [r[...].astype(BF16) for r in b_refs]
        prods = [lax.dot_general(a_vals[a_idx[t]], b_vals[b_idx[t]], dims, preferred_element_type=F32)
                 for t in range(n_terms)]
        if not separate:
            total = prods[0]
            for p_ in prods[1:]:
                total = total + p_
            prods = [total]

        def finish(accs):
            res = epi(*accs, *[e[...] for e in ex_refs]) if epi is not None else tuple(accs)
            for r, v in zip(out_refs, res[:no]):
                r[...] = v.astype(r.dtype)
            first_rows = pl.program_id(0) == 0
            for r, v in zip(sum_refs, res[no:]):
                @pl.when(first_rows)
                def _(r=r, v=v):
                    r[...] = v

                @pl.when(jnp.logical_not(first_rows))
                def _(r=r, v=v):
                    r[...] += v

        if in_place:
            dst = [(out_refs[0], t) for t in range(n_acc)] if stack_out else [(r, None) for r in out_refs]

            def read(r, t):
                return r[...] if t is None else r[t]

            assert nk == 1 or all(dt == F32 for dt in out_dtypes), name

            def write(r, t, v):
                if t is None:
                    r[...] = v.astype(r.dtype)
                else:
                    r[t] = v.astype(r.dtype)

            if nk == 1:
                for (r, t), v in zip(dst, prods):
                    write(r, t, v * scale if scale != 1.0 else v)
            else:
                k = pl.program_id(2)

                @pl.when(k == 0)
                def _():
                    for (r, t), v in zip(dst, prods):
                        write(r, t, v)

                @pl.when(jnp.logical_and(k > 0, k < nk - 1))
                def _():
                    for (r, t), v in zip(dst, prods):
                        write(r, t, read(r, t) + v)

                @pl.when(k == nk - 1)
                def _():
                    for (r, t), v in zip(dst, prods):
                        total = read(r, t) + v
                        write(r, t, total * scale if scale != 1.0 else total)
        elif nk == 1:
            finish(prods)
        else:
            k = pl.program_id(2)

            @pl.when(k == 0)
            def _():
                for r, v in zip(acc_refs, prods):
                    r[...] = v

            @pl.when(jnp.logical_and(k > 0, k < nk - 1))
            def _():
                for r, v in zip(acc_refs, prods):
                    r[...] += v

            @pl.when(k == nk - 1)
            def _():
                finish([r[...] + v for r, v in zip(acc_refs, prods)])

    if stack_out:
        out_specs = [pl.BlockSpec((n_acc, tm, tn), lambda i, j, k: (0, i, j))]
        out_shape = [jax.ShapeDtypeStruct((n_acc, M, N), F32)]
    else:
        out_specs = [pl.BlockSpec((tm, tn), lambda i, j, k: (i, j))] * no
        out_shape = [jax.ShapeDtypeStruct((M, N), dt) for dt in out_dtypes]
    sequential = col_sums or exchange is not None
    outs = pl.pallas_call(
        body,
        name=name,
        grid=grid,
        in_specs=a_specs + b_specs + ex_specs + [ANY] * len(x_ins),
        out_specs=out_specs + [pl.BlockSpec((1, tn), lambda i, j, k: (0, j))] * col_sums + [ANY] * len(x_out),
        out_shape=out_shape + [jax.ShapeDtypeStruct((1, N), F32)] * col_sums + x_out,
        scratch_shapes=[pltpu.VMEM((tm, tn), F32)] * n_scratch + x_sems,
        compiler_params=_params(*(("arbitrary",) * 3 if sequential else ("parallel", "parallel", "arbitrary"))),
    )(*a_list, *b_list, *extras, *x_ins)
    if exchange is not None:
        n_own = len(outs) - len(x_out)
        return list(outs[:n_own]), list(outs[n_own:])
    return outs


def rowwise(fn, ins, outs, accs=(), *, name, tr=512):
    R = max(x.shape[0] for x in ins)
    tr = _row_tile(R, tr)
    in_specs = []
    for x in ins:
        if x.shape[0] == R and x.ndim == 2:
            in_specs.append(pl.BlockSpec((tr, x.shape[1]), lambda i: (i, 0)))
        else:
            in_specs.append(pl.BlockSpec(x.shape, lambda i, _n=x.ndim: (0,) * _n))
    ni, no = len(ins), len(outs)

    def body(*refs):
        i = pl.program_id(0)
        row_vals, acc_vals = fn(*[r[...] for r in refs[:ni]])
        for r, v in zip(refs[ni:ni + no], row_vals):
            r[...] = v.astype(r.dtype)
        for r, v in zip(refs[ni + no:], acc_vals):
            @pl.when(i == 0)
            def _(r=r, v=v):
                r[...] = v

            @pl.when(i > 0)
            def _(r=r, v=v):
                r[...] += v

    return pl.pallas_call(
        body,
        name=name,
        grid=(R // tr,),
        in_specs=in_specs,
        out_specs=[pl.BlockSpec((tr, c), lambda i: (i, 0)) for c, _ in outs]
        + [pl.BlockSpec(s, lambda i: (0, 0)) for s in accs],
        out_shape=[jax.ShapeDtypeStruct((R, c), dt) for c, dt in outs]
        + [jax.ShapeDtypeStruct(s, F32) for s in accs],
        compiler_params=_params("arbitrary"),
    )(*ins)


def _sigmoid(x):
    return 1.0 / (1.0 + jnp.exp(-x))


_GELU_C = math.sqrt(2.0 / math.pi)


def _gelu(x):
    return 0.5 * x * (1.0 + jnp.tanh(_GELU_C * (x + 0.044715 * (x * x * x))))


def _gelu_grad(x):
    t = jnp.tanh(_GELU_C * (x + 0.044715 * (x * x * x)))
    return 0.5 * (1.0 + t) + 0.5 * x * (1.0 - t * t) * (_GELU_C * (1.0 + 3.0 * 0.044715 * (x * x)))


def rms_fwd(x, g, name):
    def fn(x, g):
        r = lax.rsqrt(jnp.mean(x * x, axis=-1, keepdims=True) + EPS)
        return [x * r * g], []

    return rowwise(fn, [x, g], [(x.shape[1], BF16)], name=name)[0]


def _rms_tile(x, g):
    return x * lax.rsqrt(jnp.mean(x * x, axis=-1, keepdims=True) + EPS) * g


def _rms_bwd_tile(dn, x, g, dres):
    r = lax.rsqrt(jnp.mean(x * x, axis=-1, keepdims=True) + EPS)
    w = dn * g
    dx = r * w - x * (r * r * r) * jnp.mean(x * w, axis=-1, keepdims=True)
    return dres + dx, jnp.sum(dn * (x * r), axis=0, keepdims=True)


def _whole(shape):
    return pl.BlockSpec(shape, lambda: (0,) * len(shape))


def _zoh(lr, li, ldt):
    dt = jnp.exp(ldt)
    mag = jnp.exp(lr * dt)
    ar, ai = mag * jnp.cos(li * dt), mag * jnp.sin(li * dt)
    den = lr * lr + li * li
    kr = ((ar - 1.0) * lr + ai * li) / den
    ki = (ai * lr - (ar - 1.0) * li) / den
    return dt, ar, ai, den, kr, ki


def ssm_prep(lam_re, lam_im, log_dt, b_re, b_im, name):
    n = lam_re.shape[0]

    def body(lr_ref, li_ref, ldt_ref, br_ref, bi_ref, ar_ref, ai_ref, bbr_ref, bbi_ref):
        _, ar, ai, _, kr, ki = _zoh(lr_ref[...], li_ref[...], ldt_ref[...])
        br, bi = br_ref[...], bi_ref[...]
        ar_ref[...] = ar
        ai_ref[...] = ai
        bbr_ref[...] = kr * br - ki * bi
        bbi_ref[...] = kr * bi + ki * br

    col, mat = (n, 1), (n, SSM_GROUP_CH)
    return pl.pallas_call(
        body, name=name,
        in_specs=[_whole(col)] * 3 + [_whole(mat)] * 2,
        out_specs=[_whole(col)] * 2 + [_whole(mat)] * 2,
        out_shape=[jax.ShapeDtypeStruct(col, F32)] * 2 + [jax.ShapeDtypeStruct(mat, F32)] * 2,
        compiler_params=_params(),
    )(lam_re, lam_im, log_dt, b_re, b_im)


def ssm_prep_bwd(lam_re, lam_im, log_dt, b_re, b_im, d_ar, d_ai, d_bbr, d_bbi, name):
    n = lam_re.shape[0]
    n_groups = n // SSM_STATE

    def body(lr_ref, li_ref, ldt_ref, br_ref, bi_ref, dar_ref, dai_ref, dbr_ref, dbi_ref,
             glr_ref, gli_ref, gdt_ref, gbr_ref, gbi_ref):
        lr, li = lr_ref[...], li_ref[...]
        dt, ar, ai, den, kr, ki = _zoh(lr, li, ldt_ref[...])
        br, bi, dbr, dbi = br_ref[...], bi_ref[...], dbr_ref[...], dbi_ref[...]
        gbr_ref[...] = kr * dbr + ki * dbi
        gbi_ref[...] = kr * dbi - ki * dbr
        gkr = jnp.sum(br * dbr + bi * dbi, axis=1, keepdims=True)
        gki = jnp.sum(br * dbi - bi * dbr, axis=1, keepdims=True)
        gar = dar_ref[...] + (gkr * lr - gki * li) / den
        gai = dai_ref[...] + (gki * lr + gkr * li) / den
        qr, qi = -(kr * lr + ki * li) / den, -(ki * lr - kr * li) / den
        g1r, g1i = qr * gkr + qi * gki, qr * gki - qi * gkr
        g2r, g2i = dt * (ar * gar + ai * gai), dt * (ar * gai - ai * gar)
        glr_ref[...] = g1r + g2r
        gli_ref[...] = g1i + g2i
        pr, pi_ = lr * ar - li * ai, lr * ai + li * ar
        gdt = (pr * gar + pi_ * gai) * dt
        grp = lax.broadcasted_iota(jnp.int32, (n, n_groups), 0) // SSM_STATE
        sel = grp == lax.broadcasted_iota(jnp.int32, (n, n_groups), 1)
        gdt_ref[...] = jnp.sum(jnp.where(sel, gdt, 0.0), axis=0, keepdims=True)

    col, mat = (n, 1), (n, SSM_GROUP_CH)
    return pl.pallas_call(
        body, name=name,
        in_specs=[_whole(col)] * 3 + [_whole(mat)] * 2 + [_whole(col)] * 2 + [_whole(mat)] * 2,
        out_specs=[_whole(col)] * 2 + [_whole((1, n_groups))] + [_whole(mat)] * 2,
        out_shape=[jax.ShapeDtypeStruct(col, F32)] * 2 + [jax.ShapeDtypeStruct((1, n_groups), F32)]
        + [jax.ShapeDtypeStruct(mat, F32)] * 2,
        compiler_params=_params(),
    )(lam_re, lam_im, log_dt, b_re, b_im, d_ar, d_ai, d_bbr, d_bbi)


def _cmul(ar, ai, br, bi):
    return ar * br - ai * bi, ar * bi + ai * br


def _scan_block(xr, xi, lr, li, carry_r, carry_i, or_ref, oi_ref, loc_r, loc_i, reverse):
    tb, cb = xr.shape
    ng = tb // SUBLANES
    xr = xr.reshape(ng, SUBLANES, cb)
    xi = xi.reshape(ng, SUBLANES, cb)
    rid = lax.broadcasted_iota(jnp.int32, (1, SUBLANES, cb), 1)
    pr, pi_ = lr.reshape(1, 1, cb), li.reshape(1, 1, cb)
    powers = []
    for k in (1, 2, 4):
        powers.append((pr, pi_))
        shift = SUBLANES - k if reverse else k
        sr, si = pltpu.roll(xr, shift, 1), pltpu.roll(xi, shift, 1)
        keep = (rid < SUBLANES - k) if reverse else (rid >= k)
        tr_, ti_ = _cmul(jnp.where(keep, pr, 0.0), jnp.where(keep, pi_, 0.0), sr, si)
        xr = xr + tr_
        xi = xi + ti_
        pr, pi_ = _cmul(pr, pi_, pr, pi_)
    loc_r[...] = xr
    loc_i[...] = xi
    (p1r, p1i), (p2r, p2i), (p4r, p4i) = powers
    dist = lax.broadcasted_iota(jnp.int32, (SUBLANES, cb), 0)
    if reverse:
        dist = SUBLANES - 1 - dist
    wr = jnp.broadcast_to(p1r.reshape(1, cb), (SUBLANES, cb))
    wi = jnp.broadcast_to(p1i.reshape(1, cb), (SUBLANES, cb))
    for bit, (qr, qi) in ((1, (p1r, p1i)), (2, (p2r, p2i)), (4, (p4r, p4i))):
        mr, mi = _cmul(wr, wi, qr.reshape(1, cb), qi.reshape(1, cb))
        on = (dist & bit) != 0
        wr, wi = jnp.where(on, mr, wr), jnp.where(on, mi, wi)
    last = 0 if reverse else SUBLANES - 1

    def step(j, carry):
        cr, ci = carry
        g = (ng - 1 - j) if reverse else j
        fr = loc_r[g] + (wr * cr - wi * ci)
        fi = loc_i[g] + (wr * ci + wi * cr)
        rows = pl.ds(pl.multiple_of(g * SUBLANES, SUBLANES), SUBLANES)
        or_ref[rows, :] = fr
        oi_ref[rows, :] = fi
        return fr[last:last + 1, :], fi[last:last + 1, :]

    cr, ci = lax.fori_loop(0, ng, step, (carry_r[...], carry_i[...]))
    carry_r[...] = cr
    carry_i[...] = ci


def _scan_tiles(seq_len, n_ch):
    return min(256, seq_len), min(512, n_ch)


def _run_exchange(plan, step, n_steps, refs):
    @pl.when(step == 0)
    def _():
        plan.start(*refs)

    for part, at in plan.relay_steps(n_steps):
        @pl.when(step == at)
        def _(part=part):
            plan.relay(part, *refs)

    @pl.when(step == n_steps - 1)
    def _():
        plan.finish(*refs)


def _exchange_args(plan):
    if plan is None:
        return [], [], []
    return list(plan.ins), list(plan.out_shape), list(plan.sems)


def _split_refs(refs, *counts):
    groups, at = [], 0
    for n in counts:
        groups.append(refs[at:at + n])
        at += n
    return groups + [refs[at:]]


def ssm_scan(x_re, x_im, lam_re, lam_im, batch, name, exchange=None):
    n, nch = x_re.shape
    seq = n // batch
    tb, cb = _scan_tiles(seq, nch)
    nt, nc = seq // tb, nch // cb
    ex_ins, ex_out, ex_sems = _exchange_args(exchange)

    def body(*refs):
        ins, xin, outs, xout, scratch, xsems = _split_refs(refs, 4, len(ex_ins), 2, len(ex_out), 6)
        xr_ref, xi_ref, lr_ref, li_ref = ins
        or_ref, oi_ref = outs
        car_r, car_i, loc_r, loc_i, s_r, s_i = scratch
        if exchange is not None:
            step = (pl.program_id(0) * batch + pl.program_id(1)) * nt + pl.program_id(2)
            _run_exchange(exchange, step, nc * batch * nt, (xin, xout, xsems))

        @pl.when(pl.program_id(2) == 0)
        def _():
            car_r[...] = jnp.zeros_like(car_r)
            car_i[...] = jnp.zeros_like(car_i)

        _scan_block(xr_ref[...].astype(F32), xi_ref[...].astype(F32), lr_ref[...], li_ref[...], car_r, car_i,
                    s_r, s_i, loc_r, loc_i, reverse=False)
        or_ref[...] = s_r[...].astype(or_ref.dtype)
        oi_ref[...] = s_i[...].astype(oi_ref.dtype)

    blk = pl.BlockSpec((tb, cb), lambda c, b, t: (b * nt + t, c))
    lam_spec = pl.BlockSpec((1, cb), lambda c, b, t: (0, c))
    res = pl.pallas_call(
        body, name=name,
        grid=(nc, batch, nt),
        in_specs=[blk, blk, lam_spec, lam_spec] + [ANY] * len(ex_ins),
        out_specs=[blk, blk] + [ANY] * len(ex_out),
        out_shape=[jax.ShapeDtypeStruct((n, nch), BF16)] * 2 + ex_out,
        scratch_shapes=[pltpu.VMEM((1, cb), F32)] * 2 + [pltpu.VMEM((tb // SUBLANES, SUBLANES, cb), F32)] * 2
        + [pltpu.VMEM((tb, cb), F32)] * 2 + ex_sems,
        compiler_params=_params("arbitrary", "arbitrary", "arbitrary"),
    )(x_re, x_im, lam_re, lam_im, *ex_ins)
    return res[0], res[1], list(res[2:])


def ssm_scan_bwd(d_re, d_im, s_re, s_im, lam_re, lam_im, batch, name, exchange=None):
    n, nch = d_re.shape
    seq = n // batch
    tb, cb = _scan_tiles(seq, nch)
    nt, nc = seq // tb, nch // cb
    halo_rows = 2 * SUBLANES
    hb = tb // halo_rows
    ex_ins, ex_out, ex_sems = _exchange_args(exchange)

    def body(*refs):
        ins, xin, outs, xout, scratch, xsems = _split_refs(refs, 8, len(ex_ins), 4, len(ex_out), 6)
        xr_ref, xi_ref, sr_ref, si_ref, hr_ref, hi_ref, lr_ref, li_ref = ins
        or_ref, oi_ref, dlr_ref, dli_ref = outs
        car_r, car_i, loc_r, loc_i, g_r, g_i = scratch
        b, t = pl.program_id(1), pl.program_id(2)
        if exchange is not None:
            step = (pl.program_id(0) * batch + b) * nt + t
            _run_exchange(exchange, step, nc * batch * nt, (xin, xout, xsems))

        @pl.when(t == 0)
        def _():
            car_r[...] = jnp.zeros_like(car_r)
            car_i[...] = jnp.zeros_like(car_i)

        _scan_block(xr_ref[...].astype(F32), xi_ref[...].astype(F32), lr_ref[...], -li_ref[...], car_r, car_i,
                    g_r, g_i, loc_r, loc_i, reverse=True)
        gr, gi = g_r[...], g_i[...]
        or_ref[...] = gr.astype(or_ref.dtype)
        oi_ref[...] = gi.astype(oi_ref.dtype)
        first_block = t == nt - 1
        row = lax.broadcasted_iota(jnp.int32, (tb, cb), 0)
        hr = jnp.where(first_block, 0.0, hr_ref[...].astype(F32)[halo_rows - 1:halo_rows, :])
        hi = jnp.where(first_block, 0.0, hi_ref[...].astype(F32)[halo_rows - 1:halo_rows, :])
        pr = jnp.where(row == 0, hr, pltpu.roll(sr_ref[...].astype(F32), 1, 0))
        pi_ = jnp.where(row == 0, hi, pltpu.roll(si_ref[...].astype(F32), 1, 0))
        dlr = jnp.sum(gr * pr + gi * pi_, axis=0, keepdims=True)
        dli = jnp.sum(gi * pr - gr * pi_, axis=0, keepdims=True)
        start = jnp.logical_and(b == 0, t == 0)

        @pl.when(start)
        def _():
            dlr_ref[...] = dlr
            dli_ref[...] = dli

        @pl.when(jnp.logical_not(start))
        def _():
            dlr_ref[...] += dlr
            dli_ref[...] += dli

    def blk(c, b, t):
        return b * nt + (nt - 1 - t)

    st_spec = pl.BlockSpec((tb, cb), lambda c, b, t: (blk(c, b, t), c))
    halo_spec = pl.BlockSpec((halo_rows, cb), lambda c, b, t: (jnp.maximum(blk(c, b, t) * hb - 1, 0), c))
    row_spec = pl.BlockSpec((1, cb), lambda c, b, t: (0, c))
    res = pl.pallas_call(
        body, name=name,
        grid=(nc, batch, nt),
        in_specs=[st_spec] * 4 + [halo_spec] * 2 + [row_spec] * 2 + [ANY] * len(ex_ins),
        out_specs=[st_spec, st_spec, row_spec, row_spec] + [ANY] * len(ex_out),
        out_shape=[jax.ShapeDtypeStruct((n, nch), BF16)] * 2 + [jax.ShapeDtypeStruct((1, nch), F32)] * 2 + ex_out,
        scratch_shapes=[pltpu.VMEM((1, cb), F32)] * 2 + [pltpu.VMEM((tb // SUBLANES, SUBLANES, cb), F32)] * 2
        + [pltpu.VMEM((tb, cb), F32)] * 2 + ex_sems,
        compiler_params=_params("arbitrary", "arbitrary", "arbitrary"),
    )(d_re, d_im, s_re, s_im, s_re, s_im, lam_re, lam_im, *ex_ins)
    return res[0], res[1], res[2], res[3], list(res[4:])


def _pool_tiles(seq_len):
    return min(512, seq_len)


def _window_sums(x, n_steps, forward_in_time):
    rows = x.shape[0]
    k = 1
    for _ in range(n_steps):
        x = x + pltpu.roll(x, k if forward_in_time else rows - k, 0)
        k *= 2
    return x


def pool_fwd(u, w_pool, scale, batch, name):
    n, c = u.shape
    seq = n // batch
    tb = _pool_tiles(seq)
    nt = seq // tb
    gc = c // len(POOL_WINDOWS)
    hb = tb // POOL_HALO

    def body(x_ref, halo_ref, w_ref, sc_ref, y_ref, q_ref):
        t = pl.program_id(1)
        halo = jnp.where(t == 0, 0.0, halo_ref[...])
        full = jnp.concatenate([halo, x_ref[...]], axis=0)
        pos = lax.broadcasted_iota(jnp.int32, (tb, gc), 0) + t * tb + 1
        for gi, win in enumerate(POOL_WINDOWS):
            cols = slice(gi * gc, (gi + 1) * gc)
            sums = _window_sums(full[:, cols], gi + 1, True)[POOL_HALO:, :]
            cnt = jnp.minimum(pos, win).astype(F32)
            q = sums / cnt - x_ref[:, cols]
            r = jnp.dot(q.astype(BF16), w_ref[gi].astype(BF16), preferred_element_type=F32)
            q_ref[:, cols] = q.astype(q_ref.dtype)
            y_ref[:, cols] = (r * sc_ref[:, cols]).astype(y_ref.dtype)

    return pl.pallas_call(
        body, name=name,
        grid=(batch, nt),
        in_specs=[pl.BlockSpec((tb, c), lambda b, t: (b * nt + t, 0)),
                  pl.BlockSpec((POOL_HALO, c), lambda b, t: (jnp.maximum((b * nt + t) * hb - 1, 0), 0)),
                  pl.BlockSpec(w_pool.shape, lambda b, t: (0, 0, 0)),
                  pl.BlockSpec((1, c), lambda b, t: (0, 0))],
        out_specs=[pl.BlockSpec((tb, c), lambda b, t: (b * nt + t, 0))] * 2,
        out_shape=[jax.ShapeDtypeStruct((n, c), BF16)] * 2,
        compiler_params=_params("parallel", "arbitrary"),
    )(u, u, w_pool, scale)


def pool_bwd(dy, q, w_pool, scale, batch, name):
    n, c = dy.shape
    seq = n // batch
    tb = _pool_tiles(seq)
    nt = seq // tb
    ng = len(POOL_WINDOWS)
    gc = c // ng
    hb = tb // POOL_HALO
    n_blocks = n // POOL_HALO

    def body(dy_ref, dyh_ref, q_ref, w_ref, sc_ref, du_ref, dw_ref, dsc_ref):
        b, t = pl.program_id(0), pl.program_id(1)
        last = t == nt - 1
        dy_full = jnp.concatenate([dy_ref[...], jnp.where(last, 0.0, dyh_ref[...])], axis=0)
        pos = lax.broadcasted_iota(jnp.int32, (tb + POOL_HALO, gc), 0) + t * tb + 1
        start = jnp.logical_and(b == 0, t == 0)
        for gi, win in enumerate(POOL_WINDOWS):
            cols = slice(gi * gc, (gi + 1) * gc)
            w = w_ref[gi].astype(BF16)
            dr = dy_full[:, cols] * sc_ref[:, cols]
            dq = lax.dot_general(dr.astype(BF16), w, (((1,), (1,)), ((), ())), preferred_element_type=F32)
            cnt = jnp.minimum(pos, win).astype(F32)
            back = _window_sums(dq / cnt, gi + 1, False)
            du_ref[:, cols] = back[:tb, :] - dq[:tb, :]
            qb = q_ref[:, cols]
            r = jnp.dot(qb, w, preferred_element_type=F32)
            dw = lax.dot_general(qb, dr[:tb, :].astype(BF16), (((0,), (0,)), ((), ())), preferred_element_type=F32)
            dsc = jnp.sum(dy_ref[:, cols] * r, axis=0, keepdims=True)

            @pl.when(start)
            def _(gi=gi, cols=cols, dw=dw, dsc=dsc):
                dw_ref[gi] = dw
                dsc_ref[:, cols] = dsc

            @pl.when(jnp.logical_not(start))
            def _(gi=gi, cols=cols, dw=dw, dsc=dsc):
                dw_ref[gi] += dw
                dsc_ref[:, cols] += dsc

    blk = pl.BlockSpec((tb, c), lambda b, t: (b * nt + t, 0))
    halo = pl.BlockSpec((POOL_HALO, c), lambda b, t: (jnp.minimum((b * nt + t + 1) * hb, n_blocks - 1), 0))
    return pl.pallas_call(
        body, name=name,
        grid=(batch, nt),
        in_specs=[blk, halo, blk,
                  pl.BlockSpec(w_pool.shape, lambda b, t: (0, 0, 0)),
                  pl.BlockSpec((1, c), lambda b, t: (0, 0))],
        out_specs=[blk, pl.BlockSpec(w_pool.shape, lambda b, t: (0, 0, 0)), pl.BlockSpec((1, c), lambda b, t: (0, 0))],
        out_shape=[jax.ShapeDtypeStruct((n, c), F32), jax.ShapeDtypeStruct(w_pool.shape, F32),
                   jax.ShapeDtypeStruct((1, c), F32)],
        compiler_params=_params("arbitrary", "arbitrary"),
    )(dy, dy, q, w_pool, scale)


def _place():
    return lax.axis_index("x"), lax.axis_index("y"), lax.axis_index("c")


class GatherPlan:
    def __init__(self, arrs):
        self.ins = list(arrs)
        na = len(arrs)
        self.out_shape = [jax.ShapeDtypeStruct((N_DEV,) + a.shape, a.dtype) for a in arrs]
        self.sems = [pltpu.SemaphoreType.DMA((na, 7)), pltpu.SemaphoreType.DMA((na, 7)), pltpu.SemaphoreType.DMA((na,))]
        self.sizes = [math.prod(a.shape) * a.dtype.itemsize for a in arrs]

    def relay_steps(self, n_steps):
        total, done, steps = sum(self.sizes), 0, []
        for a, size in enumerate(self.sizes):
            done += size
            steps.append((a, min(n_steps - 1, (done * (n_steps - 1)) // total)))
        return steps

    def _copy(self, outs, sems, a, k, block, to, src=None):
        dst = outs[a].at[4 * block[0] + 2 * block[1] + block[2]]
        return pltpu.make_async_remote_copy(
            src_ref=dst if src is None else src, dst_ref=dst,
            send_sem=sems[0].at[a, k], recv_sem=sems[1].at[a, k], device_id=to, device_id_type=MESH)

    @staticmethod
    def _chips(x, y):
        return [(1 - x, y), (x, 1 - y), (1 - x, 1 - y)]

    def _local(self, ins, outs, sems, a, me):
        return pltpu.make_async_copy(ins[a], outs[a].at[4 * me[0] + 2 * me[1] + me[2]], sems[2].at[a])

    def start(self, ins, outs, sems):
        x, y, c = _place()
        me = (x, y, c)
        for a in range(len(ins)):
            self._local(ins, outs, sems, a, me).start()
            self._copy(outs, sems, a, 0, me, (x, y, 1 - c), src=ins[a]).start()
            for j, chip in enumerate(self._chips(x, y)):
                self._copy(outs, sems, a, 1 + j, me, (*chip, c), src=ins[a]).start()

    def relay(self, a, ins, outs, sems):
        x, y, c = _place()
        for j, chip in enumerate(self._chips(x, y)):
            self._copy(outs, sems, a, 1 + j, (*chip, c), (x, y, c)).wait_recv()
            self._copy(outs, sems, a, 4 + j, (*chip, c), (x, y, 1 - c)).start()

    def finish(self, ins, outs, sems):
        x, y, c = _place()
        me, sibling = (x, y, c), (x, y, 1 - c)
        for a in range(len(ins)):
            self._copy(outs, sems, a, 0, sibling, me).wait_recv()
            for j, chip in enumerate(self._chips(x, y)):
                self._copy(outs, sems, a, 4 + j, (*chip, 1 - c), me).wait_recv()
        for a in range(len(ins)):
            self._copy(outs, sems, a, 0, me, sibling, src=ins[a]).wait_send()
            for j, chip in enumerate(self._chips(x, y)):
                self._copy(outs, sems, a, 1 + j, me, (*chip, c), src=ins[a]).wait_send()
                self._copy(outs, sems, a, 4 + j, (*chip, c), sibling).wait_send()
            self._local(ins, outs, sems, a, me).wait()


class ChipScatterPlan:
    def __init__(self, arrs):
        self.groups = [list(a) if isinstance(a, list) else [a] for a in arrs]
        self.ins = [piece for group in self.groups for piece in group]
        self.first = [sum(len(g) for g in self.groups[:a]) for a in range(len(self.groups))]
        na = len(arrs)
        self.out_shape = [jax.ShapeDtypeStruct((4,) + g[0].shape[1:], g[0].dtype) for g in self.groups]
        self.sems = [pltpu.SemaphoreType.DMA((na, 3)), pltpu.SemaphoreType.DMA((na, 3)), pltpu.SemaphoreType.DMA((na,))]

    def relay_steps(self, n_steps):
        return []

    def _row(self, ins, a, px, py):
        if len(self.groups[a]) == 1:
            return ins[self.first[a]].at[2 * px + py]
        return ins[self.first[a] + px].at[py]

    def start(self, ins, outs, sems):
        x, y, c = _place()
        mine = 2 * x + y
        for xs in (0, 1):
            @pl.when(x == xs)
            def _(xs=xs):
                for a in range(len(self.groups)):
                    pltpu.make_async_copy(self._row(ins, a, xs, y), outs[a].at[mine], sems[2].at[a]).start()
                    for j, (px, py) in enumerate([(1 - xs, y), (xs, 1 - y), (1 - xs, 1 - y)]):
                        pltpu.make_async_remote_copy(
                            src_ref=self._row(ins, a, px, py), dst_ref=outs[a].at[mine],
                            send_sem=sems[0].at[a, j], recv_sem=sems[1].at[a, j],
                            device_id=(px, py, c), device_id_type=MESH).start()

    def finish(self, ins, outs, sems):
        x, y, c = _place()
        for wait_recv in (True, False):
            for a in range(len(self.groups)):
                for j in range(3):
                    cp = pltpu.make_async_remote_copy(
                        src_ref=self._row(ins, a, 0, 0), dst_ref=outs[a].at[0],
                        send_sem=sems[0].at[a, j], recv_sem=sems[1].at[a, j],
                        device_id=(x, y, c), device_id_type=MESH)
                    if wait_recv:
                        cp.wait_recv()
                    else:
                        cp.wait_send()
        for a in range(len(self.groups)):
            pltpu.make_async_copy(self._row(ins, a, 0, 0), outs[a].at[0], sems[2].at[a]).wait()


def run_exchange(plan, name):
    n_in, n_out = len(plan.ins), len(plan.out_shape)

    def body(*refs):
        parts = (refs[:n_in], refs[n_in:n_in + n_out], refs[n_in + n_out:])
        plan.start(*parts)
        for part, _ in plan.relay_steps(1):
            plan.relay(part, *parts)
        plan.finish(*parts)

    return pl.pallas_call(
        body, name=name,
        in_specs=[ANY] * n_in, out_specs=[ANY] * n_out,
        out_shape=plan.out_shape, scratch_shapes=plan.sems,
    )(*plan.ins)


class SiblingSwapPlan:
    def __init__(self, arrs):
        self.ins = list(arrs)
        na = len(arrs)
        self.out_shape = [jax.ShapeDtypeStruct(a.shape, a.dtype) for a in arrs]
        self.sems = [pltpu.SemaphoreType.DMA((na,)), pltpu.SemaphoreType.DMA((na,))]

    def relay_steps(self, n_steps):
        return []

    @staticmethod
    def _copies(ins, outs, sems):
        x, y, c = _place()
        return [pltpu.make_async_remote_copy(
            src_ref=ins[a], dst_ref=outs[a], send_sem=sems[0].at[a], recv_sem=sems[1].at[a],
            device_id=(x, y, 1 - c), device_id_type=MESH) for a in range(len(ins))]

    def start(self, ins, outs, sems):
        for cp in self._copies(ins, outs, sems):
            cp.start()

    def finish(self, ins, outs, sems):
        for cp in self._copies(ins, outs, sems):
            cp.wait()


def adamw(w, gparts, m, v, name, tr=256):
    rows, cols = w.shape
    parts = gparts.shape[0]
    tr = _row_tile(rows, tr)
    c1 =1.0 - ADAM_B1 ** ADAM_STEP
    c2 = 1.0 - ADAM_B2 ** ADAM_STEP

    def body(w_ref, g_ref, m_ref, v_ref, go_ref, d_ref, mo_ref, vo_ref):
        g = g_ref[0].astype(F32)
        for p_ in range(1, parts):
            g = g + g_ref[p_].astype(F32)
        m_new = ADAM_B1 * m_ref[...] + (1.0 - ADAM_B1) * g
        v_new = ADAM_B2 * v_ref[...] + (1.0 - ADAM_B2) * (g * g)
        m_hat = m_new / c1
        v_hat = v_new / c2
        go_ref[...] = g
        d_ref[...] = -ADAM_LR * (m_hat / (jnp.sqrt(v_hat) + ADAM_EPS) + ADAM_WD * w_ref[...])
        mo_ref[...] = m_new
        vo_ref[...] = v_new

    blk = pl.BlockSpec((tr, cols), lambda i: (i, 0))
    return pl.pallas_call(
        body, name=name,
        grid=(rows // tr,),
        in_specs=[blk, pl.BlockSpec((parts, tr, cols), lambda i: (0, i, 0)), blk, blk],
        out_specs=[blk] * 4,
        out_shape=[jax.ShapeDtypeStruct((rows, cols), F32)] * 4,
        compiler_params=_params("parallel"),
    )(w, gparts, m, v)


def add2(a, b, name, out_dtype):
    return rowwise(lambda a, b: ([a.astype(F32) + b.astype(F32)], []), [a, b], [(a.shape[1], out_dtype)],
                   name=name, tr=256)[0]


def _block_diag(x):
    g, a, b = x.shape
    eye = jnp.eye(g, dtype=x.dtype)
    return (x[:, :, None, :] * eye[:, None, :, None]).reshape(g * a, g * b)


def _diag_blocks(x, a, b):
    per = x.shape[1] // b
    x5 = x.reshape(SSM_SUPER, per, a, per, b)
    eye = jnp.eye(per, dtype=x.dtype)
    return jnp.sum(x5 * eye[None, :, None, :, None], axis=3).reshape(SSM_SUPER * per, a, b)


def _swiglu_epi(g, u):
    s = _sigmoid(g)
    silu = g * s
    return u * (s * (1.0 + g * (1.0 - s))), silu, silu * u


def _residual_epi(scale, with_norm):
    if with_norm:
        def epi(acc, res, gain):
            out = res + scale * acc
            return out, _rms_tile(out, gain)
    else:
        def epi(acc, res):
            return (res + scale * acc,)
    return epi


def _next_norm(next_gain):
    if next_gain is None:
        return [], (F32,)
    return [next_gain], (F32, BF16)


FFN_WIDE = 2816


def ffn_fwd(h, n, wi, wo, next_gain, tag, exchange=None):
    res = matmul(n, [(wi, 0), (wi, 1)], mode="nn", name=f"{tag}_in", separate=True, epi=_swiglu_epi,
                 out_dtypes=(BF16, BF16, BF16), exchange=exchange, tm=512, tn=FFN_WIDE)
    (dact_g, dact_u, act), exchanged = res if exchange is not None else (res, [])
    more, dtypes = _next_norm(next_gain)
    res = matmul(act, wo, mode="nn", name=f"{tag}_out", epi=_residual_epi(0.5, bool(more)), extras=[h] + more,
                 out_dtypes=dtypes, tm=512, tk=FFN_WIDE)
    return res[0], (res[1] if more else None), (h, n, dact_g, dact_u, act), exchanged


def ffn_bwd(dh, saved, gain, wi, wo, tag, carry=None, gate=None, on_weight_grads=None):
    h, n, dact_g, dact_u, act = saved
    if carry is None:
        carry = lambda where, run: run(None)[0]
    dg, du = carry(f"{tag}_out_dx", lambda ex: _with_exchange(matmul(
        dh, wo, mode="nt", name=f"{tag}_out_dx", extras=[dact_g, dact_u], out_dtypes=(BF16, BF16),
        epi=lambda acc, fg, fu: (0.5 * acc * fg, 0.5 * acc * fu), exchange=ex, tm=512, tn=FFN_WIDE), ex))
    d_wo, = carry(f"{tag}_out_dw", lambda ex: _with_exchange(matmul(
        act, dh, mode="tn", name=f"{tag}_out_dw", scale=0.5, exchange=ex, tm=FFN_WIDE), ex))
    d_wi = [carry(f"{tag}_in_dw_{k}", lambda ex, half=half: _with_exchange(matmul(
        n, half, mode="tn", name=f"{tag}_in_dw", exchange=ex, tn=FFN_WIDE), ex))[0] for k, half in (("g", dg), ("u", du))]
    if on_weight_grads is not None:
        on_weight_grads(d_wi, d_wo)
    if gate is None:
        epi, more, dtypes = _rms_bwd_tile, [], (F32,)
    else:
        def epi(acc, x, g, dres, e, pre):
            dh_new, d_gain = _rms_bwd_tile(acc, x, g, dres)
            return (dh_new, *_gate_cotangents(dh_new, e, pre), d_gain)
        more, dtypes = list(gate), (F32, BF16, BF16)
    res = carry(f"{tag}_in_dx", lambda ex: _with_exchange(matmul(
        [dg, du], [(wi, 0), (wi, 1)], mode="nt", name=f"{tag}_in_dx", epi=epi, extras=[h, gain, dh] + more,
        out_dtypes=dtypes, col_sums=1, exchange=ex, tm=256, tk=FFN_WIDE), ex))
    return res[0], res[-1], d_wi, d_wo, tuple(res[1:-1])


def _with_exchange(result, exchange):
    return result if exchange is not None else (result, [])


def mix_fwd(h, n, lw, batch, next_gain, tag, exchange=None):
    sw = h.shape[1] // 2
    us, up = matmul(n, [lw["w_in"][:, :sw], lw["w_in"][:, sw:]], mode="nn", name=f"{tag}_in", separate=True,
                    out_dtypes=(F32, F32))
    lam_r, lam_i, bb_r, bb_i = ssm_prep(lw["lam_re"], lw["lam_im"], lw["log_dt"], lw["b_re"], lw["b_im"], f"{tag}_zoh")
    lam = (lam_r.reshape(1, SSM_CH), lam_i.reshape(1, SSM_CH))
    b_mats = [_block_diag(bb.reshape(SSM_GROUPS, SSM_STATE, SSM_GROUP_CH).transpose(0, 2, 1)).astype(BF16)
              for bb in (bb_r, bb_i)]
    c_mats = [_block_diag(cc.transpose(0, 2, 1)).astype(BF16) for cc in (lw["c_re"], -lw["c_im"])]
    bu_re, bu_im = matmul(us, b_mats, mode="nn", name=f"{tag}_bu", separate=True, diag=SSM_SUPER,
                          out_dtypes=(BF16, BF16))
    s_re, s_im, exchanged = ssm_scan(bu_re, bu_im, *lam, batch, f"{tag}_scan", exchange)
    y0, y1 = matmul([s_re, s_im], c_mats, mode="nn", name=f"{tag}_c", diag=SSM_SUPER,
                    epi=lambda acc, u, d: (acc + d * u, _gelu(acc + d * u)), extras=[us, lw["ssm_d"]],
                    out_dtypes=(F32, BF16))
    y2, gl = matmul(y1, lw["w_glu"], mode="nn", name=f"{tag}_glu",
                    epi=lambda acc, y0: (_gelu(y0) * _sigmoid(acc), acc), extras=[y0], out_dtypes=(BF16, F32))
    yp, q = pool_fwd(up, lw["pool_w"], lw["pool_scale"], batch, f"{tag}_pool")
    more, dtypes = _next_norm(next_gain)
    res = matmul([y2, yp], [lw["w_out"][:sw], lw["w_out"][sw:]], mode="nn", name=f"{tag}_out",
                 epi=_residual_epi(1.0, bool(more)), extras=[h] + more, out_dtypes=dtypes)
    saved = (h, n, us, lam, b_mats, c_mats, s_re, s_im, y0, y1, gl, y2, yp, q)
    return res[0], (res[1] if more else None), saved, exchanged


def mix_bwd(dh, saved, lw, batch, tag, exchange=None):
    h, n, us, lam, b_mats, c_mats, s_re, s_im, y0, y1, gl, y2, yp, q = saved
    sw = h.shape[1] // 2
    w_out_s, w_out_p = lw["w_out"][:sw], lw["w_out"][sw:]
    d_wo_s, d_wo_p = matmul([y2, yp], dh, mode="tn", name=f"{tag}_out_dw", separate=True)
    def out_dx_epi(dy2, dyp, y0, gl):
        sg = _sigmoid(gl)
        return dy2, dyp, dy2 * _gelu(y0) * sg * (1.0 - sg)

    dy2, dyp, tg = matmul(dh, [w_out_s, w_out_p], mode="nt", name=f"{tag}_out_dx", separate=True, epi=out_dx_epi,
                          extras=[y0, gl], out_dtypes=(F32, F32, BF16))
    dup, d_pool_w, d_pool_scale = pool_bwd(dyp, q, lw["pool_w"], lw["pool_scale"], batch, f"{tag}_pool_bwd")
    def dy0_epi(acc, dy2, gl, y0, u):
        dy0 = (acc + dy2 * _sigmoid(gl)) * _gelu_grad(y0)
        return dy0, jnp.sum(dy0 * u, axis=0, keepdims=True)

    dy0, d_d = matmul(tg, lw["w_glu"], mode="nt", name=f"{tag}_glu_dx", epi=dy0_epi, extras=[dy2, gl, y0, us],
                      out_dtypes=(F32,), col_sums=1)
    d_w_glu, = matmul(y1, tg, mode="tn", name=f"{tag}_glu_dw")
    gd_re, gd_im = matmul(dy0, c_mats, mode="nt", name=f"{tag}_c_dx", separate=True, diag=SSM_SUPER,
                          out_dtypes=(BF16, BF16))
    d_c_top, d_c_bot = matmul([s_re, s_im], dy0, mode="tn", name=f"{tag}_c_dw", separate=True, diag=SSM_SUPER)
    g_re, g_im, d_lam_r, d_lam_i, exchanged = ssm_scan_bwd(gd_re, gd_im, s_re, s_im, *lam, batch, f"{tag}_scan_bwd",
                                                           exchange)
    dus, = matmul([g_re, g_im], b_mats, mode="nt", name=f"{tag}_bu_dx", diag=SSM_SUPER,
                  epi=lambda acc, dy0, d: (acc + d * dy0,), extras=[dy0, lw["ssm_d"]])
    d_b_re, d_b_im = matmul(us, [g_re, g_im], mode="tn", name=f"{tag}_bu_dw", separate=True, diag=SSM_SUPER)
    d_bb_r = _diag_blocks(d_b_re, SSM_GROUP_CH, SSM_STATE).transpose(0, 2, 1).reshape(SSM_CH, SSM_GROUP_CH)
    d_bb_i = _diag_blocks(d_b_im, SSM_GROUP_CH, SSM_STATE).transpose(0, 2, 1).reshape(SSM_CH, SSM_GROUP_CH)
    d_lr, d_li, d_ldt, d_br, d_bi = ssm_prep_bwd(
        lw["lam_re"], lw["lam_im"], lw["log_dt"], lw["b_re"], lw["b_im"],
        d_lam_r.reshape(SSM_CH, 1), d_lam_i.reshape(SSM_CH, 1), d_bb_r, d_bb_i, f"{tag}_zoh_bwd")
    d_c_re = _diag_blocks(d_c_top, SSM_STATE, SSM_GROUP_CH).transpose(0, 2, 1)
    d_c_im = -_diag_blocks(d_c_bot, SSM_STATE, SSM_GROUP_CH).transpose(0, 2, 1)
    d_w_in_s, d_w_in_p = matmul(n, [dus, dup], mode="tn", name=f"{tag}_in_dw", separate=True)
    dh_new, d_gain = matmul([dus, dup], [lw["w_in"][:, :sw], lw["w_in"][:, sw:]], mode="nt", name=f"{tag}_in_dx",
                            epi=_rms_bwd_tile, extras=[h, lw["mix_norm"], dh], out_dtypes=(F32,), col_sums=1, tm=512)
    grads = dict(mix_norm=d_gain, w_in=jnp.concatenate([d_w_in_s, d_w_in_p], axis=1),
                 ssm_lambda_re=d_lr, ssm_lambda_im=d_li, ssm_log_dt=d_ldt, ssm_b_re=d_br, ssm_b_im=d_bi,
                 ssm_c_re=d_c_re, ssm_c_im=d_c_im, ssm_d=d_d, ssm_w_glu=d_w_glu, pool_w=d_pool_w,
                 pool_scale=d_pool_scale, w_out=jnp.concatenate([d_wo_s, d_wo_p], axis=0))
    return dh_new, grads, exchanged


def ple_fwd(h, n, p, w_gate, w_proj, next_gain, tag):
    e, = matmul(p, w_proj, mode="nn", name=f"{tag}_proj")
    if next_gain is None:
        def epi(acc, e, res):
            return res + _sigmoid(acc) * e, acc
        more, dtypes = [], (F32, F32)
    else:
        def epi(acc, e, res, gain):
            out = res + _sigmoid(acc) * e
            return out, acc, _rms_tile(out, gain)
        more, dtypes = [next_gain], (F32, F32, BF16)
    res = matmul(n, w_gate, mode="nn", name=f"{tag}_gate", epi=epi, extras=[e, h] + more, out_dtypes=dtypes, tm=512)
    return res[0], (res[2] if more else None), (h, n, e, res[1])


def _gate_cotangents(dh, e, pre):
    s = _sigmoid(pre)
    return dh * e * s * (1.0 - s), dh * s


def ple_bwd(dh, dpre, de, saved, p, gain, w_gate, tag):
    h, n, e, pre = saved
    d_w_gate, = matmul(n, dpre, mode="tn", name=f"{tag}_gate_dw")
    d_w_proj, = matmul(p, de, mode="tn", name=f"{tag}_proj_dw")
    dh_new, d_gain = matmul(dpre, w_gate, mode="nt", name=f"{tag}_gate_dx", epi=_rms_bwd_tile,
                            extras=[h, gain, dh], out_dtypes=(F32,), col_sums=1, tm=512)
    return dh_new, d_gain, d_w_gate, d_w_proj


def loss_head(h, gain, target, e, pre, name):
    d = h.shape[1]

    def fn(h, g, t, e, pre):
        r = lax.rsqrt(jnp.mean(h * h, axis=-1, keepdims=True) + EPS)
        diff = h * r * g - t
        sq = jnp.sum(jnp.sum(diff * diff, axis=1, keepdims=True), axis=0, keepdims=True)
        dy = diff * (1.0 / d)
        w = dy * g
        dh = r * w - h * (r * r * r) * jnp.mean(h * w, axis=-1, keepdims=True)
        return [dh, *_gate_cotangents(dh, e, pre)], [sq, jnp.sum(dy * (h * r), axis=0, keepdims=True)]

    dh, dpre, de, sq, d_gain = rowwise(fn, [h, gain, target, e, pre], [(d, F32), (d, BF16), (d, BF16)],
                                       [(1, 1), (1, d)], name=name, tr=256)
    return 0.5 / d * sq[0, 0], dh, dpre, de, d_gain


SHARDED = {
    "ffn1_wi": 1, "ffn1_wo": 0, "w_in": 0, "ssm_w_glu": 0, "w_out": 0, "ffn2_wi": 1, "ffn2_wo": 0,
    "ple_w_gate": 0, "ple_w_proj": 1,
}
WEIGHTS = ["ffn1_norm", "ffn1_wi", "ffn1_wo", "mix_norm", "w_in", "ssm_lambda_re", "ssm_lambda_im", "ssm_log_dt",
           "ssm_b_re", "ssm_b_im", "ssm_c_re", "ssm_c_im", "ssm_d", "ssm_w_glu", "pool_w", "pool_scale", "w_out",
           "ffn2_norm", "ffn2_wi", "ffn2_wo", "ple_norm", "ple_w_gate", "ple_w_proj", "final_norm"]
REPLICATED = [n for n in WEIGHTS if n not in SHARDED]


HALVED = ("ffn1_wi", "ffn2_wi")


def _unshard(gathered, axis, halved):
    if halved:
        _, rows, cols = gathered.shape
        return gathered.reshape(2, 4, rows, cols).transpose(0, 2, 1, 3).reshape(2, rows, 4 * cols)
    g = jnp.moveaxis(gathered, 0, axis)
    shp = g.shape
    return g.reshape(shp[:axis] + (shp[axis] * shp[axis + 1],) + shp[axis + 2:])


def _split_for_scatter(full, axis, c, halved):
    if halved:
        rows, cols = full[0].shape

        def pick(cc):
            return [lax.dynamic_index_in_dim(h.reshape(rows, 2, 2, cols // 4), cc, 2, keepdims=False).transpose(1, 0, 2)
                    for h in full]

        return pick(c), [s.astype(BF16) for s in pick(1 - c)]
    shp = full.shape
    g = full.reshape(shp[:axis] + (4, 2, shp[axis] // N_DEV) + shp[axis + 1:])
    keep = lax.dynamic_index_in_dim(g, c, axis + 1, keepdims=False)
    send = lax.dynamic_index_in_dim(g, 1 - c, axis + 1, keepdims=False)
    return jnp.moveaxis(keep, axis, 0), jnp.moveaxis(send, axis, 0).astype(BF16)


def _pack(arrs):
    pieces = []
    for a in arrs:
        flat = a.reshape(-1)
        pad = (-flat.shape[0]) % PACK
        pieces.append(jnp.pad(flat, (0, pad)).reshape(-1, LANES))
    return jnp.concatenate(pieces, axis=0)


def _unpack(packed, shapes):
    out, row = [], 0
    for s in shapes:
        size = math.prod(s)
        rows = (size + PACK - 1) // PACK * SUBLANES
        out.append(packed[row:row + rows].reshape(-1)[:size].reshape(s))
        row += rows
    return out


class NoExchange:
    def __init__(self, full):
        self.full = full

    def big(self, i, name):
        w = self.full[name][i]
        if name in HALVED:
            rows, cols = w.shape
            w = w.reshape(rows, 2, cols // 2).transpose(1, 0, 2)
        return w

    def fwd_exchange(self, i, where):
        return None

    def fwd_done(self, i, where, results):
        pass

    def bwd_exchange(self, i, where):
        return None

    def bwd_done(self, i, where, results):
        pass

    def layer_grads(self, i, grads, names):
        pass

    def small_grads(self, i, grads):
        pass

    def final_grad(self, d_final):
        pass


def local_step(x, p, target, rep, hooks):
    batch, seq, d = x.shape
    n_tok = batch * seq
    h = x.reshape(n_tok, d)
    saved = []
    n = rms_fwd(h, rep["ffn1_norm"][0].reshape(1, d), "first_norm")
    for i in range(DEPTH):
        lw = _layer_weights(rep, i, d)
        big = lambda name, i=i: hooks.big(i, name)
        next_gain = rep["ffn1_norm"][i + 1].reshape(1, d) if i + 1 < DEPTH else None
        h, n, s1, exchanged = ffn_fwd(h, n, big("ffn1_wi"), big("ffn1_wo"), lw["mix_norm"], "ffn1",
                                      hooks.fwd_exchange(i, "ffn1_in"))
        hooks.fwd_done(i, "ffn1_in", exchanged)
        lw.update(w_in=big("w_in"), w_glu=big("ssm_w_glu"), w_out=big("w_out"))
        h, n, s2, exchanged = mix_fwd(h, n, lw, batch, lw["ffn2_norm"], "mix", hooks.fwd_exchange(i, "scan"))
        hooks.fwd_done(i, "scan", exchanged)
        h, n, s3, exchanged = ffn_fwd(h, n, big("ffn2_wi"), big("ffn2_wo"), lw["ple_norm"], "ffn2",
                                      hooks.fwd_exchange(i, "ffn2_in"))
        hooks.fwd_done(i, "ffn2_in", exchanged)
        p_i = p[i].reshape(n_tok, -1)
        h, n, s4 = ple_fwd(h, n, p_i, big("ple_w_gate"), big("ple_w_proj"), next_gain, "ple")
        saved.append((s1, s2, s3, s4, p_i))
    last_gate = saved[-1][3][2:]
    loss, dh, dpre, de, d_final = loss_head(h, rep["final_norm"].reshape(1, d), target.reshape(n_tok, d), *last_gate,
                                            "loss_head")
    hooks.final_grad(d_final)
    per_layer = [None] * DEPTH
    for i in reversed(range(DEPTH)):
        lw = _layer_weights(rep, i, d)
        big = lambda name, i=i: hooks.big(i, name)
        lw.update(w_in=big("w_in"), w_glu=big("ssm_w_glu"), w_out=big("w_out"))
        s1, s2, s3, s4, p_i = saved[i]
        g = {}
        dh, g["ple_norm"], g["ple_w_gate"], g["ple_w_proj"] = ple_bwd(dh, dpre, de, s4, p_i, lw["ple_norm"],
                                                                     big("ple_w_gate"), "ple")
        def carry(where, run, i=i):
            outs, exchanged = run(hooks.bwd_exchange(i, where))
            hooks.bwd_done(i, where, exchanged)
            return outs

        dh, g["ffn2_norm"], g["ffn2_wi"], g["ffn2_wo"], _ = ffn_bwd(
            dh, s3, lw["ffn2_norm"], big("ffn2_wi"), big("ffn2_wo"), "ffn2", carry)
        dh, gm, exchanged = mix_bwd(dh, s2, lw, batch, "mix", hooks.bwd_exchange(i, "scan"))
        hooks.bwd_done(i, "scan", exchanged)
        g.update(gm)
        hooks.layer_grads(i, g, [k for k in SHARDED if k not in LAST_GRADS])
        hooks.small_grads(i, {k: g[k] for k in REPLICATED if k in g})
        below = saved[i - 1][3][2:] if i > 0 else None

        def ffn1_grads(d_wi, d_wo, i=i, g=g):
            g["ffn1_wi"], g["ffn1_wo"] = d_wi, d_wo
            hooks.layer_grads(i, g, list(LAST_GRADS))

        dh, g["ffn1_norm"], _, _, gate_ct = ffn_bwd(
            dh, s1, lw["ffn1_norm"], big("ffn1_wi"), big("ffn1_wo"), "ffn1", carry, below, ffn1_grads)
        hooks.small_grads(i, {"ffn1_norm": g["ffn1_norm"]})
        if i > 0:
            dpre, de = gate_ct
        per_layer[i] = g
    return loss, dh.reshape(batch, seq, d), per_layer, d_final


LAST_GRADS = ("ffn1_wi", "ffn1_wo")


def _layer_weights(w, i, d):
    sw = d // 2
    lw = {}
    lw["pool_w"] = w["pool_w"][i]
    for k in ("ffn1_norm", "mix_norm", "ffn2_norm", "ple_norm"):
        lw[k] = w[k][i].reshape(1, d)
    lw["ssm_d"] = w["ssm_d"][i].reshape(1, sw)
    lw["pool_scale"] = w["pool_scale"][i].reshape(1, sw)
    lw["lam_re"] = w["ssm_lambda_re"][i].reshape(SSM_CH, 1)
    lw["lam_im"] = w["ssm_lambda_im"][i].reshape(SSM_CH, 1)
    lw["log_dt"] = jnp.repeat(w["ssm_log_dt"][i], SSM_STATE).reshape(SSM_CH, 1)
    lw["b_re"] = w["ssm_b_re"][i].reshape(SSM_CH, SSM_GROUP_CH)
    lw["b_im"] = w["ssm_b_im"][i].reshape(SSM_CH, SSM_GROUP_CH)
    lw["c_re"] = w["ssm_c_re"][i]
    lw["c_im"] = w["ssm_c_im"][i]
    return lw


class MeshExchange:
    FIRST = ("ffn1_wi", "ffn1_wo")
    FWD_PLAN = {"scan": ("ffn1_wi", "ffn1_wo", "w_in", "ssm_w_glu", "w_out"),
                "ffn2_in": ("ffn2_wi", "ffn2_wo", "ple_w_gate", "ple_w_proj")}
    BWD_PLAN = {"ffn2_in_dx": ("ffn1_wo", "ffn2_wo", "ssm_w_glu", "w_out", "ple_w_gate", "ple_w_proj"),
                "scan": ("ffn1_wi", "ffn2_wi", "w_in")}
    LAST_LAYER_PLAN = {"ffn1_out_dx": ("ffn2_wi", "w_in"),
                       "ffn1_out_dw": ("ffn2_wo", "w_out", "ple_w_gate", "ssm_w_glu", "ple_w_proj"),
                       "ffn1_in_dx": LAST_GRADS}

    def __init__(self, shards):
        self.small = {}
        self.d_final = None
        self.small_parts = []
        self.small_rest = None
        self.swaps = {}
        self.shards = shards
        self.c = lax.axis_index("c")
        self.gathered = {}
        self.chip_sums = {}
        self.from_chips = {}
        self.pending = None
        first = run_exchange(GatherPlan([shards[0][k] for k in self.FIRST]), "gather_first_weights")
        self._store(self.gathered, 0, self.FIRST, first)

    @staticmethod
    def _store(where, layer, names, results):
        for k, r in zip(names, results):
            where[(layer, k)] = r

    def big(self, i, name):
        return _unshard(self.gathered[(i, name)], SHARDED[name], name in HALVED)

    def fwd_exchange(self, i, where):
        if i == 0 and where == "ffn1_in":
            layer, names = 0, [k for k in SHARDED if k not in self.FIRST]
        elif where in self.FWD_PLAN and i + 1 < DEPTH:
            layer, names = i + 1, self.FWD_PLAN[where]
        else:
            return None
        self.pending = (layer, names)
        return GatherPlan([self.shards[layer][k] for k in names])

    def fwd_done(self, i, where, results):
        if results:
            self._store(self.gathered, *self.pending, results)

    def layer_grads(self, i, grads, names):
        pieces, sends = [], []
        for k in names:
            keep, send = _split_for_scatter(grads[k], SHARDED[k], self.c, k in HALVED)
            keep, send = (keep, send) if isinstance(keep, list) else ([keep], [send])
            pieces.append(keep)
            sends += send
        if i == 0:
            self._add_sibling(0, names, pieces, run_exchange(SiblingSwapPlan(sends), "reduce_core_pair"))
        else:
            self.swaps[(i, tuple(names) == LAST_GRADS)] = (names, pieces, sends)

    def _add_sibling(self, i, names, pieces, from_sibling):
        got = iter(from_sibling)
        for k, keep in zip(names, pieces):
            sums = []
            for part in keep:
                cols = part.shape[-1]
                sums.append(add2(part.reshape(-1, cols), next(got).reshape(-1, cols), f"sum_core_pair_{k}",
                                 BF16).reshape(part.shape))
            self.chip_sums[(i, k)] = sums if len(sums) > 1 else sums[0]

    def small_grads(self, i, grads):
        self.small.setdefault(i, {}).update(grads)

    def final_grad(self, d_final):
        self.d_final = d_final

    def _small_pack(self, d_final, first_norm_grad=None):
        pieces = []
        for k in REPLICATED:
            if k == "final_norm":
                pieces.append(d_final)
            elif k == "ffn1_norm" and first_norm_grad is None:
                pieces.append(jnp.stack([self.small[i][k] for i in range(1, DEPTH)], axis=0))
            else:
                pieces.append(jnp.stack([self.small[i][k] for i in range(DEPTH)], axis=0))
        return _pack(pieces)

    def bwd_exchange(self, i, where):
        if where in self.BWD_PLAN and i + 1 < DEPTH:
            layer, names = i + 1, self.BWD_PLAN[where]
        elif i == 0 and where in self.LAST_LAYER_PLAN:
            layer, names = 0, self.LAST_LAYER_PLAN[where]
        elif i == 0 and where in ("ffn1_in_dw_g", "ffn1_in_dw_u"):
            if where.endswith("g"):
                pack = self._small_pack(self.d_final)
                half = pack.shape[0] // 2 // SUBLANES * SUBLANES
                part, self.small_rest = pack[:half], pack[half:]
            else:
                part = self.small_rest
            self.pending = "small"
            return GatherPlan([part])
        elif i > 0 and where in ("ffn1_out_dx", "ffn1_in_dx"):
            self.pending = ("swap", i, where == "ffn1_in_dx")
            return SiblingSwapPlan(self.swaps[self.pending[1:]][2])
        else:
            return None
        self.pending = (layer, names)
        return ChipScatterPlan([self.chip_sums[(layer, k)] for k in names])

    def bwd_done(self, i, where, results):
        if not results:
            return
        if self.pending == "small":
            self.small_parts += results
        elif self.pending[0] == "swap":
            names, pieces, _ = self.swaps[self.pending[1:]]
            self._add_sibling(self.pending[1], names, pieces, results)
        else:
            self._store(self.from_chips, *self.pending, results)

    def small_gathered(self):
        first = self.small[0]["ffn1_norm"].reshape(SUBLANES, LANES)
        first_all, = run_exchange(GatherPlan([first]), "gather_first_gain_grad")
        return jnp.concatenate([first_all] + self.small_parts, axis=1)


def kernel(x, p, ffn1_norm, ffn1_wi, ffn1_wo, mix_norm, w_in, ssm_lambda_re, ssm_lambda_im, ssm_log_dt, ssm_b_re, ssm_b_im, ssm_c_re, ssm_c_im, ssm_d, ssm_w_glu, pool_w, pool_scale, w_out, ffn2_norm, ffn2_wi, ffn2_wo, ple_norm, ple_w_gate, ple_w_proj, final_norm, loss_target, m_ffn1_norm, m_ffn1_wi, m_ffn1_wo, m_mix_norm, m_w_in, m_ssm_lambda_re, m_ssm_lambda_im, m_ssm_log_dt, m_ssm_b_re, m_ssm_b_im, m_ssm_c_re, m_ssm_c_im, m_ssm_d, m_ssm_w_glu, m_pool_w, m_pool_scale, m_w_out, m_ffn2_norm, m_ffn2_wi, m_ffn2_wo, m_ple_norm, m_ple_w_gate, m_ple_w_proj, m_final_norm, v_ffn1_norm, v_ffn1_wi, v_ffn1_wo, v_mix_norm, v_w_in, v_ssm_lambda_re, v_ssm_lambda_im, v_ssm_log_dt, v_ssm_b_re, v_ssm_b_im, v_ssm_c_re, v_ssm_c_im, v_ssm_d, v_ssm_w_glu, v_pool_w, v_pool_scale, v_w_out, v_ffn2_norm, v_ffn2_wi, v_ffn2_wo, v_ple_norm, v_ple_w_gate, v_ple_w_proj, v_final_norm):
    args = dict(locals())
    wts = {k: args[k] for k in WEIGHTS}
    rep = {k: wts[k] for k in REPLICATED}

    shards = [{k: wts[k][i].astype(BF16) for k in SHARDED} for i in range(DEPTH)]
    exchange = MeshExchange(shards)
    loss_local, grad_x, per_layer, d_final = local_step(x, p, loss_target, rep, exchange)
    loss = lax.psum(loss_local, ("x", "y", "c"))

    outs = {}
    for k in SHARDED:
        shp = wts[k].shape
        cols = shp[-1]
        parts = jnp.stack([exchange.from_chips[(i, k)] for i in range(DEPTH)], axis=1)
        res = adamw(wts[k].reshape(-1, cols), parts.reshape(4, -1, cols), args["m_" + k].reshape(-1, cols),
                    args["v_" + k].reshape(-1, cols), f"adamw_{k}")
        outs[k] = [r.reshape(shp) for r in res]

    rep_shapes = [wts[k].shape for k in REPLICATED]
    all_g = exchange.small_gathered()
    res = adamw(_pack([wts[k] for k in REPLICATED]), all_g, _pack([args["m_" + k] for k in REPLICATED]),
                _pack([args["v_" + k] for k in REPLICATED]), "adamw_small")
    unpacked = [_unpack(r, rep_shapes) for r in res]
    for j, k in enumerate(REPLICATED):
        outs[k] = [unpacked[q][j] for q in range(4)]

    result = [loss, grad_x]
    for q in range(4):
        result += [outs[k][q] for k in WEIGHTS]
    return tuple(result)
```

```python
import math

import jax
import jax.numpy as jnp
from jax import lax
from jax.experimental import pallas as pl
from jax.experimental.pallas import tpu as pltpu

F32 = jnp.float32
BF16 = jnp.bfloat16
MESH = pl.DeviceIdType.MESH
ANY = pl.BlockSpec(memory_space=pl.ANY)

N_DEV = 8
DEPTH = 4
EPS = 1e-6
SSM_GROUPS = 32
SSM_GROUP_CH = 16
SSM_STATE = 64
SSM_CH = SSM_GROUPS * SSM_STATE
SSM_SUPER = 2
POOL_WINDOWS = (2, 4, 8, 16)
POOL_HALO = 16
ADAM_LR, ADAM_B1, ADAM_B2, ADAM_EPS, ADAM_WD, ADAM_STEP = 0.001, 0.9, 0.999, 1e-08, 0.01, 10

V7X_VMEM_BYTES = 64 * 1024 * 1024
VMEM_LIMIT_BYTES = V7X_VMEM_BYTES - 12 * 1024 * 1024
LANES = 128
SUBLANES = 8
PACK = SUBLANES * LANES


def _params(*sem):
    return pltpu.CompilerParams(dimension_semantics=sem or None, vmem_limit_bytes=VMEM_LIMIT_BYTES)


def _tile(n, pref):
    if n <= pref:
        return n
    t = pref - pref % LANES
    while t >= LANES:
        if n % t == 0:
            return t
        t -= LANES
    raise ValueError(f"no lane-aligned tile for {n}")


def _row_tile(rows, pref):
    if rows <= pref:
        return rows
    t = pref - pref % SUBLANES
    while t >= SUBLANES:
        if rows % t == 0:
            return t
        t -= SUBLANES
    raise ValueError(f"no sublane-aligned tile for {rows}")


_DIMS = {"nn": ((1,), (0,)), "nt": ((1,), (1,)), "tn": ((0,), (0,))}


def matmul(a, b, *, mode, name, out_dtypes=None, epi=None, extras=(), separate=False, diag=1, col_sums=0,
           stack_out=False, scale=1.0, exchange=None, tm=1024, tn=1024, tk=1024):
    a_list = list(a) if isinstance(a, list) else [a]
    b_list = list(b) if isinstance(b, list) else [b]
    a_planes = [x[1] if isinstance(x, tuple) else None for x in a_list]
    b_planes = [x[1] if isinstance(x, tuple) else None for x in b_list]
    a_list = [x[0] if isinstance(x, tuple) else x for x in a_list]
    b_list = [x[0] if isinstance(x, tuple) else x for x in b_list]
    a_shape, b_shape = a_list[0].shape[-2:], b_list[0].shape[-2:]
    n_terms = max(len(a_list), len(b_list))
    a_idx = [0] * n_terms if len(a_list) == 1 else list(range(n_terms))
    b_idx = [0] * n_terms if len(b_list) == 1 else list(range(n_terms))
    n_acc = n_terms if separate else 1
    assert not (stack_out or scale != 1.0) or epi is None
    if out_dtypes is None:
        out_dtypes = (F32,) * (1 if (epi is not None or stack_out) else n_acc)
    in_place = epi is None
    if mode == "tn":
        K, M = a_shape
        K2, N = b_shape
    elif mode == "nt":
        M, K = a_shape
        N, K2 = b_shape
    else:
        M, K = a_shape
        K2, N = b_shape
    assert K == K2, (name, a_shape, b_shape)
    if mode == "tn":
        tm, tn, tk = _tile(M // diag, tm), _tile(N // diag, tn), _tile(K, tk)
        nk = K // tk
        N = N // diag
        row_tiles, col_tiles = (M // diag) // tm, N // tn
        a_blk, a_map = (tk, tm), lambda i, j, k: (k, i)
        b_blk, b_map = (tk, tn), lambda i, j, k: (k, (i // row_tiles) * col_tiles + j)
    else:
        tm, tn, tk = _tile(M, tm), _tile(N // diag, tn), _tile(K // diag, tk)
        nk = (K // diag) // tk
        col_tiles = (N // diag) // tn
        a_blk, a_map = (tm, tk), lambda i, j, k: (i, (j // col_tiles) * nk + k)
        if mode == "nt":
            b_blk, b_map = (tn, tk), lambda i, j, k: (j, (j // col_tiles) * nk + k)
        else:
            b_blk, b_map = (tk, tn), lambda i, j, k: ((j // col_tiles) * nk + k, j)

    def plane_spec(blk, index_map, plane):
        if plane is None:
            return pl.BlockSpec(blk, index_map)
        return pl.BlockSpec((None,) + blk, lambda i, j, k: (plane,) + index_map(i, j, k))

    a_specs = [plane_spec(a_blk, a_map, p_) for p_ in a_planes]
    b_specs = [plane_spec(b_blk, b_map, p_) for p_ in b_planes]
    assert not col_sums or N == tn, (name, N, tn)
    ex_specs = []
    for e in extras:
        if e.shape == (M, N):
            ex_specs.append(pl.BlockSpec((tm, tn), lambda i, j, k: (i, j)))
        elif e.shape == (1, N):
            ex_specs.append(pl.BlockSpec((1, tn), lambda i, j, k: (0, j)))
        elif e.shape == (M, 1):
            ex_specs.append(pl.BlockSpec((tm, 1), lambda i, j, k: (i, 0)))
        else:
            raise ValueError((name, e.shape, (M, N)))
    na, nb, ne, no = len(a_list), len(b_list), len(extras), len(out_dtypes)
    dims = (_DIMS[mode], ((), ()))

    n_scratch = n_acc if (nk > 1 and not in_place) else 0
    x_ins, x_out, x_sems = _exchange_args(exchange)
    grid = (M // tm, N // tn, nk)

    def body(*refs):
        a_refs, b_refs, ex_refs, xin, out_refs, sum_refs, xout, acc_refs, xsems = _split_refs(
            refs, na, nb, ne, len(x_ins), no, col_sums, len(x_out), n_scratch)
        if exchange is not None:
            step = (pl.program_id(0) * grid[1] + pl.program_id(1)) * nk + pl.program_id(2)
            _run_exchange(exchange, step, grid[0] * grid[1] * nk, (xin, xout, xsems))
        a_vals = [r[...].astype(BF16) for r in a_refs]
        b_vals = [r[...].astype(BF16) for r in b_refs]
        prods = [lax.dot_general(a_vals[a_idx[t]], b_vals[b_idx[t]], dims, preferred_element_type=F32)
                 for t in range(n_terms)]
        if not separate:
            total = prods[0]
            for p_ in prods[1:]:
                total = total + p_
            prods = [total]

        def finish(accs):
            res = epi(*accs, *[e[...] for e in ex_refs]) if epi is not None else tuple(accs)
            for r, v in zip(out_refs, res[:no]):
                r[...] = v.astype(r.dtype)
            first_rows = pl.program_id(0) == 0
            for r, v in zip(sum_refs, res[no:]):
                @pl.when(first_rows)
                def _(r=r, v=v):
                    r[...] = v

                @pl.when(jnp.logical_not(first_rows))
                def _(r=r, v=v):
                    r[...] += v

        if in_place:
            dst = [(out_refs[0], t) for t in range(n_acc)] if stack_out else [(r, None) for r in out_refs]

            def read(r, t):
                return r[...] if t is None else r[t]

            assert nk == 1 or all(dt == F32 for dt in out_dtypes), name

            def write(r, t, v):
                if t is None:
                    r[...] = v.astype(r.dtype)
                else:
                    r[t] = v.astype(r.dtype)

            if nk == 1:
                for (r, t), v in zip(dst, prods):
                    write(r, t, v * scale if scale != 1.0 else v)
            else:
                k = pl.program_id(2)

                @pl.when(k == 0)
                def _():
                    for (r, t), v in zip(dst, prods):
                        write(r, t, v)

                @pl.when(jnp.logical_and(k > 0, k < nk - 1))
                def _():
                    for (r, t), v in zip(dst, prods):
                        write(r, t, read(r, t) + v)

                @pl.when(k == nk - 1)
                def _():
                    for (r, t), v in zip(dst, prods):
                        total = read(r, t) + v
                        write(r, t, total * scale if scale != 1.0 else total)
        elif nk == 1:
            finish(prods)
        else:
            k = pl.program_id(2)

            @pl.when(k == 0)
            def _():
                for r, v in zip(acc_refs, prods):
                    r[...] = v

            @pl.when(jnp.logical_and(k > 0, k < nk - 1))
            def _():
                for r, v in zip(acc_refs, prods):
                    r[...] += v

            @pl.when(k == nk - 1)
            def _():
                finish([r[...] + v for r, v in zip(acc_refs, prods)])

    if stack_out:
        out_specs = [pl.BlockSpec((n_acc, tm, tn), lambda i, j, k: (0, i, j))]
        out_shape = [jax.ShapeDtypeStruct((n_acc, M, N), F32)]
    else:
        out_specs = [pl.BlockSpec((tm, tn), lambda i, j, k: (i, j))] * no
        out_shape = [jax.ShapeDtypeStruct((M, N), dt) for dt in out_dtypes]
    sequential = col_sums or exchange is not None
    outs = pl.pallas_call(
        body,
        name=name,
        grid=grid,
        in_specs=a_specs + b_specs + ex_specs + [ANY] * len(x_ins),
        out_specs=out_specs + [pl.BlockSpec((1, tn), lambda i, j, k: (0, j))] * col_sums + [ANY] * len(x_out),
        out_shape=out_shape + [jax.ShapeDtypeStruct((1, N), F32)] * col_sums + x_out,
        scratch_shapes=[pltpu.VMEM((tm, tn), F32)] * n_scratch + x_sems,
        compiler_params=_params(*(("arbitrary",) * 3 if sequential else ("parallel", "parallel", "arbitrary"))),
    )(*a_list, *b_list, *extras, *x_ins)
    if exchange is not None:
        n_own = len(outs) - len(x_out)
        return list(outs[:n_own]), list(outs[n_own:])
    return outs


def rowwise(fn, ins, outs, accs=(), *, name, tr=512):
    R = max(x.shape[0] for x in ins)
    tr = _row_tile(R, tr)
    in_specs = []
    for x in ins:
        if x.shape[0] == R and x.ndim == 2:
            in_specs.append(pl.BlockSpec((tr, x.shape[1]), lambda i: (i, 0)))
        else:
            in_specs.append(pl.BlockSpec(x.shape, lambda i, _n=x.ndim: (0,) * _n))
    ni, no = len(ins), len(outs)

    def body(*refs):
        i = pl.program_id(0)
        row_vals, acc_vals = fn(*[r[...] for r in refs[:ni]])
        for r, v in zip(refs[ni:ni + no], row_vals):
            r[...] = v.astype(r.dtype)
        for r, v in zip(refs[ni + no:], acc_vals):
            @pl.when(i == 0)
            def _(r=r, v=v):
                r[...] = v

            @pl.when(i > 0)
            def _(r=r, v=v):
                r[...] += v

    return pl.pallas_call(
        body,
        name=name,
        grid=(R // tr,),
        in_specs=in_specs,
        out_specs=[pl.BlockSpec((tr, c), lambda i: (i, 0)) for c, _ in outs]
        + [pl.BlockSpec(s, lambda i: (0, 0)) for s in accs],
        out_shape=[jax.ShapeDtypeStruct((R, c), dt) for c, dt in outs]
        + [jax.ShapeDtypeStruct(s, F32) for s in accs],
        compiler_params=_params("arbitrary"),
    )(*ins)


def _sigmoid(x):
    return 1.0 / (1.0 + jnp.exp(-x))


_GELU_C = math.sqrt(2.0 / math.pi)


def _gelu(x):
    return 0.5 * x * (1.0 + jnp.tanh(_GELU_C * (x + 0.044715 * (x * x * x))))


def _gelu_grad(x):
    t = jnp.tanh(_GELU_C * (x + 0.044715 * (x * x * x)))
    return 0.5 * (1.0 + t) + 0.5 * x * (1.0 - t * t) * (_GELU_C * (1.0 + 3.0 * 0.044715 * (x * x)))


def rms_fwd(x, g, name):
    def fn(x, g):
        r = lax.rsqrt(jnp.mean(x * x, axis=-1, keepdims=True) + EPS)
        return [x * r * g], []

    return rowwise(fn, [x, g], [(x.shape[1], BF16)], name=name)[0]


def _rms_tile(x, g):
    return x * lax.rsqrt(jnp.mean(x * x, axis=-1, keepdims=True) + EPS) * g


def _rms_bwd_tile(dn, x, g, dres):
    r = lax.rsqrt(jnp.mean(x * x, axis=-1, keepdims=True) + EPS)
    w = dn * g
    dx = r * w - x * (r * r * r) * jnp.mean(x * w, axis=-1, keepdims=True)
    return dres + dx, jnp.sum(dn * (x * r), axis=0, keepdims=True)


def _whole(shape):
    return pl.BlockSpec(shape, lambda: (0,) * len(shape))


def _zoh(lr, li, ldt):
    dt = jnp.exp(ldt)
    mag = jnp.exp(lr * dt)
    ar, ai = mag * jnp.cos(li * dt), mag * jnp.sin(li * dt)
    den = lr * lr + li * li
    kr = ((ar - 1.0) * lr + ai * li) / den
    ki = (ai * lr - (ar - 1.0) * li) / den
    return dt, ar, ai, den, kr, ki


def ssm_prep(lam_re, lam_im, log_dt, b_re, b_im, name):
    n = lam_re.shape[0]

    def body(lr_ref, li_ref, ldt_ref, br_ref, bi_ref, ar_ref, ai_ref, bbr_ref, bbi_ref):
        _, ar, ai, _, kr, ki = _zoh(lr_ref[...], li_ref[...], ldt_ref[...])
        br, bi = br_ref[...], bi_ref[...]
        ar_ref[...] = ar
        ai_ref[...] = ai
        bbr_ref[...] = kr * br - ki * bi
        bbi_ref[...] = kr * bi + ki * br

    col, mat = (n, 1), (n, SSM_GROUP_CH)
    return pl.pallas_call(
        body, name=name,
        in_specs=[_whole(col)] * 3 + [_whole(mat)] * 2,
        out_specs=[_whole(col)] * 2 + [_whole(mat)] * 2,
        out_shape=[jax.ShapeDtypeStruct(col, F32)] * 2 + [jax.ShapeDtypeStruct(mat, F32)] * 2,
        compiler_params=_params(),
    )(lam_re, lam_im, log_dt, b_re, b_im)


def ssm_prep_bwd(lam_re, lam_im, log_dt, b_re, b_im, d_ar, d_ai, d_bbr, d_bbi, name):
    n = lam_re.shape[0]
    n_groups = n // SSM_STATE

    def body(lr_ref, li_ref, ldt_ref, br_ref, bi_ref, dar_ref, dai_ref, dbr_ref, dbi_ref,
             glr_ref, gli_ref, gdt_ref, gbr_ref, gbi_ref):
        lr, li = lr_ref[...], li_ref[...]
        dt, ar, ai, den, kr, ki = _zoh(lr, li, ldt_ref[...])
        br, bi, dbr, dbi = br_ref[...], bi_ref[...], dbr_ref[...], dbi_ref[...]
        gbr_ref[...] = kr * dbr + ki * dbi
        gbi_ref[...] = kr * dbi - ki * dbr
        gkr = jnp.sum(br * dbr + bi * dbi, axis=1, keepdims=True)
        gki = jnp.sum(br * dbi - bi * dbr, axis=1, keepdims=True)
        gar = dar_ref[...] + (gkr * lr - gki * li) / den
        gai = dai_ref[...] + (gki * lr + gkr * li) / den
        qr, qi = -(kr * lr + ki * li) / den, -(ki * lr - kr * li) / den
        g1r, g1i = qr * gkr + qi * gki, qr * gki - qi * gkr
        g2r, g2i = dt * (ar * gar + ai * gai), dt * (ar * gai - ai * gar)
        glr_ref[...] = g1r + g2r
        gli_ref[...] = g1i + g2i
        pr, pi_ = lr * ar - li * ai, lr * ai + li * ar
        gdt = (pr * gar + pi_ * gai) * dt
        grp = lax.broadcasted_iota(jnp.int32, (n, n_groups), 0) // SSM_STATE
        sel = grp == lax.broadcasted_iota(jnp.int32, (n, n_groups), 1)
        gdt_ref[...] = jnp.sum(jnp.where(sel, gdt, 0.0), axis=0, keepdims=True)

    col, mat = (n, 1), (n, SSM_GROUP_CH)
    return pl.pallas_call(
        body, name=name,
        in_specs=[_whole(col)] * 3 + [_whole(mat)] * 2 + [_whole(col)] * 2 + [_whole(mat)] * 2,
        out_specs=[_whole(col)] * 2 + [_whole((1, n_groups))] + [_whole(mat)] * 2,
        out_shape=[jax.ShapeDtypeStruct(col, F32)] * 2 + [jax.ShapeDtypeStruct((1, n_groups), F32)]
        + [jax.ShapeDtypeStruct(mat, F32)] * 2,
        compiler_params=_params(),
    )(lam_re, lam_im, log_dt, b_re, b_im, d_ar, d_ai, d_bbr, d_bbi)


def _cmul(ar, ai, br, bi):
    return ar * br - ai * bi, ar * bi + ai * br


def _scan_block(xr, xi, lr, li, carry_r, carry_i, or_ref, oi_ref, loc_r, loc_i, reverse):
    tb, cb = xr.shape
    ng = tb // SUBLANES
    xr = xr.reshape(ng, SUBLANES, cb)
    xi = xi.reshape(ng, SUBLANES, cb)
    rid = lax.broadcasted_iota(jnp.int32, (1, SUBLANES, cb), 1)
    pr, pi_ = lr.reshape(1, 1, cb), li.reshape(1, 1, cb)
    powers = []
    for k in (1, 2, 4):
        powers.append((pr, pi_))
        shift = SUBLANES - k if reverse else k
        sr, si = pltpu.roll(xr, shift, 1), pltpu.roll(xi, shift, 1)
        keep = (rid < SUBLANES - k) if reverse else (rid >= k)
        tr_, ti_ = _cmul(jnp.where(keep, pr, 0.0), jnp.where(keep, pi_, 0.0), sr, si)
        xr = xr + tr_
        xi = xi + ti_
        pr, pi_ = _cmul(pr, pi_, pr, pi_)
    loc_r[...] = xr
    loc_i[...] = xi
    (p1r, p1i), (p2r, p2i), (p4r, p4i) = powers
    dist = lax.broadcasted_iota(jnp.int32, (SUBLANES, cb), 0)
    if reverse:
        dist = SUBLANES - 1 - dist
    wr = jnp.broadcast_to(p1r.reshape(1, cb), (SUBLANES, cb))
    wi = jnp.broadcast_to(p1i.reshape(1, cb), (SUBLANES, cb))
    for bit, (qr, qi) in ((1, (p1r, p1i)), (2, (p2r, p2i)), (4, (p4r, p4i))):
        mr, mi = _cmul(wr, wi, qr.reshape(1, cb), qi.reshape(1, cb))
        on = (dist & bit) != 0
        wr, wi = jnp.where(on, mr, wr), jnp.where(on, mi, wi)
    last = 0 if reverse else SUBLANES - 1

    def step(j, carry):
        cr, ci = carry
        g = (ng - 1 - j) if reverse else j
        fr = loc_r[g] + (wr * cr - wi * ci)
        fi = loc_i[g] + (wr * ci + wi * cr)
        rows = pl.ds(pl.multiple_of(g * SUBLANES, SUBLANES), SUBLANES)
        or_ref[rows, :] = fr
        oi_ref[rows, :] = fi
        return fr[last:last + 1, :], fi[last:last + 1, :]

    cr, ci = lax.fori_loop(0, ng, step, (carry_r[...], carry_i[...]))
    carry_r[...] = cr
    carry_i[...] = ci


def _scan_tiles(seq_len, n_ch):
    return min(256, seq_len), min(512, n_ch)


def _run_exchange(plan, step, n_steps, refs):
    @pl.when(step == 0)
    def _():
        plan.start(*refs)

    for part, at in plan.relay_steps(n_steps):
        @pl.when(step == at)
        def _(part=part):
            plan.relay(part, *refs)

    @pl.when(step == n_steps - 1)
    def _():
        plan.finish(*refs)


def _exchange_args(plan):
    if plan is None:
        return [], [], []
    return list(plan.ins), list(plan.out_shape), list(plan.sems)


def _split_refs(refs, *counts):
    groups, at = [], 0
    for n in counts:
        groups.append(refs[at:at + n])
        at += n
    return groups + [refs[at:]]


def ssm_scan(x_re, x_im, lam_re, lam_im, batch, name, exchange=None):
    n, nch = x_re.shape
    seq = n // batch
    tb, cb = _scan_tiles(seq, nch)
    nt, nc = seq // tb, nch // cb
    ex_ins, ex_out, ex_sems = _exchange_args(exchange)

    def body(*refs):
        ins, xin, outs, xout, scratch, xsems = _split_refs(refs, 4, len(ex_ins), 2, len(ex_out), 6)
        xr_ref, xi_ref, lr_ref, li_ref = ins
        or_ref, oi_ref = outs
        car_r, car_i, loc_r, loc_i, s_r, s_i = scratch
        if exchange is not None:
            step = (pl.program_id(0) * batch + pl.program_id(1)) * nt + pl.program_id(2)
            _run_exchange(exchange, step, nc * batch * nt, (xin, xout, xsems))

        @pl.when(pl.program_id(2) == 0)
        def _():
            car_r[...] = jnp.zeros_like(car_r)
            car_i[...] = jnp.zeros_like(car_i)

        _scan_block(xr_ref[...].astype(F32), xi_ref[...].astype(F32), lr_ref[...], li_ref[...], car_r, car_i,
                    s_r, s_i, loc_r, loc_i, reverse=False)
        or_ref[...] = s_r[...].astype(or_ref.dtype)
        oi_ref[...] = s_i[...].astype(oi_ref.dtype)

    blk = pl.BlockSpec((tb, cb), lambda c, b, t: (b * nt + t, c))
    lam_spec = pl.BlockSpec((1, cb), lambda c, b, t: (0, c))
    res = pl.pallas_call(
        body, name=name,
        grid=(nc, batch, nt),
        in_specs=[blk, blk, lam_spec, lam_spec] + [ANY] * len(ex_ins),
        out_specs=[blk, blk] + [ANY] * len(ex_out),
        out_shape=[jax.ShapeDtypeStruct((n, nch), BF16)] * 2 + ex_out,
        scratch_shapes=[pltpu.VMEM((1, cb), F32)] * 2 + [pltpu.VMEM((tb // SUBLANES, SUBLANES, cb), F32)] * 2
        + [pltpu.VMEM((tb, cb), F32)] * 2 + ex_sems,
        compiler_params=_params("arbitrary", "arbitrary", "arbitrary"),
    )(x_re, x_im, lam_re, lam_im, *ex_ins)
    return res[0], res[1], list(res[2:])


def ssm_scan_bwd(d_re, d_im, s_re, s_im, lam_re, lam_im, batch, name, exchange=None):
    n, nch = d_re.shape
    seq = n // batch
    tb, cb = _scan_tiles(seq, nch)
    nt, nc = seq // tb, nch // cb
    halo_rows = 2 * SUBLANES
    hb = tb // halo_rows
    ex_ins, ex_out, ex_sems = _exchange_args(exchange)

    def body(*refs):
        ins, xin, outs, xout, scratch, xsems = _split_refs(refs, 8, len(ex_ins), 4, len(ex_out), 6)
        xr_ref, xi_ref, sr_ref, si_ref, hr_ref, hi_ref, lr_ref, li_ref = ins
        or_ref, oi_ref, dlr_ref, dli_ref = outs
        car_r, car_i, loc_r, loc_i, g_r, g_i = scratch
        b, t = pl.program_id(1), pl.program_id(2)
        if exchange is not None:
            step = (pl.program_id(0) * batch + b) * nt + t
            _run_exchange(exchange, step, nc * batch * nt, (xin, xout, xsems))

        @pl.when(t == 0)
        def _():
            car_r[...] = jnp.zeros_like(car_r)
            car_i[...] = jnp.zeros_like(car_i)

        _scan_block(xr_ref[...].astype(F32), xi_ref[...].astype(F32), lr_ref[...], -li_ref[...], car_r, car_i,
                    g_r, g_i, loc_r, loc_i, reverse=True)
        gr, gi = g_r[...], g_i[...]
        or_ref[...] = gr.astype(or_ref.dtype)
        oi_ref[...] = gi.astype(oi_ref.dtype)
        first_block = t == nt - 1
        row = lax.broadcasted_iota(jnp.int32, (tb, cb), 0)
        hr = jnp.where(first_block, 0.0, hr_ref[...].astype(F32)[halo_rows - 1:halo_rows, :])
        hi = jnp.where(first_block, 0.0, hi_ref[...].astype(F32)[halo_rows - 1:halo_rows, :])
        pr = jnp.where(row == 0, hr, pltpu.roll(sr_ref[...].astype(F32), 1, 0))
        pi_ = jnp.where(row == 0, hi, pltpu.roll(si_ref[...].astype(F32), 1, 0))
        dlr = jnp.sum(gr * pr + gi * pi_, axis=0, keepdims=True)
        dli = jnp.sum(gi * pr - gr * pi_, axis=0, keepdims=True)
        start = jnp.logical_and(b == 0, t == 0)

        @pl.when(start)
        def _():
            dlr_ref[...] = dlr
            dli_ref[...] = dli

        @pl.when(jnp.logical_not(start))
        def _():
            dlr_ref[...] += dlr
            dli_ref[...] += dli

    def blk(c, b, t):
        return b * nt + (nt - 1 - t)

    st_spec = pl.BlockSpec((tb, cb), lambda c, b, t: (blk(c, b, t), c))
    halo_spec = pl.BlockSpec((halo_rows, cb), lambda c, b, t: (jnp.maximum(blk(c, b, t) * hb - 1, 0), c))
    row_spec = pl.BlockSpec((1, cb), lambda c, b, t: (0, c))
    res = pl.pallas_call(
        body, name=name,
        grid=(nc, batch, nt),
        in_specs=[st_spec] * 4 + [halo_spec] * 2 + [row_spec] * 2 + [ANY] * len(ex_ins),
        out_specs=[st_spec, st_spec, row_spec, row_spec] + [ANY] * len(ex_out),
        out_shape=[jax.ShapeDtypeStruct((n, nch), BF16)] * 2 + [jax.ShapeDtypeStruct((1, nch), F32)] * 2 + ex_out,
        scratch_shapes=[pltpu.VMEM((1, cb), F32)] * 2 + [pltpu.VMEM((tb // SUBLANES, SUBLANES, cb), F32)] * 2
        + [pltpu.VMEM((tb, cb), F32)] * 2 + ex_sems,
        compiler_params=_params("arbitrary", "arbitrary", "arbitrary"),
    )(d_re, d_im, s_re, s_im, s_re, s_im, lam_re, lam_im, *ex_ins)
    return res[0], res[1], res[2], res[3], list(res[4:])


def _pool_tiles(seq_len):
    return min(512, seq_len)


def _window_sums(x, n_steps, forward_in_time):
    rows = x.shape[0]
    k = 1
    for _ in range(n_steps):
        x = x + pltpu.roll(x, k if forward_in_time else rows - k, 0)
        k *= 2
    return x


def pool_fwd(u, w_pool, scale, batch, name):
    n, c = u.shape
    seq = n // batch
    tb = _pool_tiles(seq)
    nt = seq // tb
    gc = c // len(POOL_WINDOWS)
    hb = tb // POOL_HALO

    def body(x_ref, halo_ref, w_ref, sc_ref, y_ref, q_ref):
        t = pl.program_id(1)
        halo = jnp.where(t == 0, 0.0, halo_ref[...])
        full = jnp.concatenate([halo, x_ref[...]], axis=0)
        pos = lax.broadcasted_iota(jnp.int32, (tb, gc), 0) + t * tb + 1
        for gi, win in enumerate(POOL_WINDOWS):
            cols = slice(gi * gc, (gi + 1) * gc)
            sums = _window_sums(full[:, cols], gi + 1, True)[POOL_HALO:, :]
            cnt = jnp.minimum(pos, win).astype(F32)
            q = sums / cnt - x_ref[:, cols]
            r = jnp.dot(q.astype(BF16), w_ref[gi].astype(BF16), preferred_element_type=F32)
            q_ref[:, cols] = q.astype(q_ref.dtype)
            y_ref[:, cols] = (r * sc_ref[:, cols]).astype(y_ref.dtype)

    return pl.pallas_call(
        body, name=name,
        grid=(batch, nt),
        in_specs=[pl.BlockSpec((tb, c), lambda b, t: (b * nt + t, 0)),
                  pl.BlockSpec((POOL_HALO, c), lambda b, t: (jnp.maximum((b * nt + t) * hb - 1, 0), 0)),
                  pl.BlockSpec(w_pool.shape, lambda b, t: (0, 0, 0)),
                  pl.BlockSpec((1, c), lambda b, t: (0, 0))],
        out_specs=[pl.BlockSpec((tb, c), lambda b, t: (b * nt + t, 0))] * 2,
        out_shape=[jax.ShapeDtypeStruct((n, c), BF16)] * 2,
        compiler_params=_params("parallel", "arbitrary"),
    )(u, u, w_pool, scale)


def pool_bwd(dy, q, w_pool, scale, batch, name):
    n, c = dy.shape
    seq = n // batch
    tb = _pool_tiles(seq)
    nt = seq // tb
    ng = len(POOL_WINDOWS)
    gc = c // ng
    hb = tb // POOL_HALO
    n_blocks = n // POOL_HALO

    def body(dy_ref, dyh_ref, q_ref, w_ref, sc_ref, du_ref, dw_ref, dsc_ref):
        b, t = pl.program_id(0), pl.program_id(1)
        last = t == nt - 1
        dy_full = jnp.concatenate([dy_ref[...], jnp.where(last, 0.0, dyh_ref[...])], axis=0)
        pos = lax.broadcasted_iota(jnp.int32, (tb + POOL_HALO, gc), 0) + t * tb + 1
        start = jnp.logical_and(b == 0, t == 0)
        for gi, win in enumerate(POOL_WINDOWS):
            cols = slice(gi * gc, (gi + 1) * gc)
            w = w_ref[gi].astype(BF16)
            dr = dy_full[:, cols] * sc_ref[:, cols]
            dq = lax.dot_general(dr.astype(BF16), w, (((1,), (1,)), ((), ())), preferred_element_type=F32)
            cnt = jnp.minimum(pos, win).astype(F32)
            back = _window_sums(dq / cnt, gi + 1, False)
            du_ref[:, cols] = back[:tb, :] - dq[:tb, :]
            qb = q_ref[:, cols]
            r = jnp.dot(qb, w, preferred_element_type=F32)
            dw = lax.dot_general(qb, dr[:tb, :].astype(BF16), (((0,), (0,)), ((), ())), preferred_element_type=F32)
            dsc = jnp.sum(dy_ref[:, cols] * r, axis=0, keepdims=True)

            @pl.when(start)
            def _(gi=gi, cols=cols, dw=dw, dsc=dsc):
                dw_ref[gi] = dw
                dsc_ref[:, cols] = dsc

            @pl.when(jnp.logical_not(start))
            def _(gi=gi, cols=cols, dw=dw, dsc=dsc):
                dw_ref[gi] += dw
                dsc_ref[:, cols] += dsc

    blk = pl.BlockSpec((tb, c), lambda b, t: (b * nt + t, 0))
    halo = pl.BlockSpec((POOL_HALO, c), lambda b, t: (jnp.minimum((b * nt + t + 1) * hb, n_blocks - 1), 0))
    return pl.pallas_call(
        body, name=name,
        grid=(batch, nt),
        in_specs=[blk, halo, blk,
                  pl.BlockSpec(w_pool.shape, lambda b, t: (0, 0, 0)),
                  pl.BlockSpec((1, c), lambda b, t: (0, 0))],
        out_specs=[blk, pl.BlockSpec(w_pool.shape, lambda b, t: (0, 0, 0)), pl.BlockSpec((1, c), lambda b, t: (0, 0))],
        out_shape=[jax.ShapeDtypeStruct((n, c), F32), jax.ShapeDtypeStruct(w_pool.shape, F32),
                   jax.ShapeDtypeStruct((1, c), F32)],
        compiler_params=_params("arbitrary", "arbitrary"),
    )(dy, dy, q, w_pool, scale)


def _place():
    return lax.axis_index("x"), lax.axis_index("y"), lax.axis_index("c")


class GatherPlan:
    def __init__(self, arrs):
        self.ins = list(arrs)
        na = len(arrs)
        self.out_shape = [jax.ShapeDtypeStruct((N_DEV,) + a.shape, a.dtype) for a in arrs]
        self.sems = [pltpu.SemaphoreType.DMA((na, 7)), pltpu.SemaphoreType.DMA((na, 7)), pltpu.SemaphoreType.DMA((na,))]
        self.sizes = [math.prod(a.shape) * a.dtype.itemsize for a in arrs]

    def relay_steps(self, n_steps):
        total, done, steps = sum(self.sizes), 0, []
        for a, size in enumerate(self.sizes):
            done += size
            steps.append((a, min(n_steps - 1, (done * (n_steps - 1)) // total)))
        return steps

    def _copy(self, outs, sems, a, k, block, to, src=None):
        dst = outs[a].at[4 * block[0] + 2 * block[1] + block[2]]
        return pltpu.make_async_remote_copy(
            src_ref=dst if src is None else src, dst_ref=dst,
            send_sem=sems[0].at[a, k], recv_sem=sems[1].at[a, k], device_id=to, device_id_type=MESH)

    @staticmethod
    def _chips(x, y):
        return [(1 - x, y), (x, 1 - y), (1 - x, 1 - y)]

    def _local(self, ins, outs, sems, a, me):
        return pltpu.make_async_copy(ins[a], outs[a].at[4 * me[0] + 2 * me[1] + me[2]], sems[2].at[a])

    def start(self, ins, outs, sems):
        x, y, c = _place()
        me = (x, y, c)
        for a in range(len(ins)):
            self._local(ins, outs, sems, a, me).start()
            self._copy(outs, sems, a, 0, me, (x, y, 1 - c), src=ins[a]).start()
            for j, chip in enumerate(self._chips(x, y)):
                self._copy(outs, sems, a, 1 + j, me, (*chip, c), src=ins[a]).start()

    def relay(self, a, ins, outs, sems):
        x, y, c = _place()
        for j, chip in enumerate(self._chips(x, y)):
            self._copy(outs, sems, a, 1 + j, (*chip, c), (x, y, c)).wait_recv()
            self._copy(outs, sems, a, 4 + j, (*chip, c), (x, y, 1 - c)).start()

    def finish(self, ins, outs, sems):
        x, y, c = _place()
        me, sibling = (x, y, c), (x, y, 1 - c)
        for a in range(len(ins)):
            self._copy(outs, sems, a, 0, sibling, me).wait_recv()
            for j, chip in enumerate(self._chips(x, y)):
                self._copy(outs, sems, a, 4 + j, (*chip, 1 - c), me).wait_recv()
        for a in range(len(ins)):
            self._copy(outs, sems, a, 0, me, sibling, src=ins[a]).wait_send()
            for j, chip in enumerate(self._chips(x, y)):
                self._copy(outs, sems, a, 1 + j, me, (*chip, c), src=ins[a]).wait_send()
                self._copy(outs, sems, a, 4 + j, (*chip, c), sibling).wait_send()
            self._local(ins, outs, sems, a, me).wait()


class ChipScatterPlan:
    def __init__(self, arrs):
        self.groups = [list(a) if isinstance(a, list) else [a] for a in arrs]
        self.ins = [piece for group in self.groups for piece in group]
        self.first = [sum(len(g) for g in self.groups[:a]) for a in range(len(self.groups))]
        na = len(arrs)
        self.out_shape = [jax.ShapeDtypeStruct((4,) + g[0].shape[1:], g[0].dtype) for g in self.groups]
        self.sems = [pltpu.SemaphoreType.DMA((na, 3)), pltpu.SemaphoreType.DMA((na, 3)), pltpu.SemaphoreType.DMA((na,))]

    def relay_steps(self, n_steps):
        return []

    def _row(self, ins, a, px, py):
        if len(self.groups[a]) == 1:
            return ins[self.first[a]].at[2 * px + py]
        return ins[self.first[a] + px].at[py]

    def start(self, ins, outs, sems):
        x, y, c = _place()
        mine = 2 * x + y
        for xs in (0, 1):
            @pl.when(x == xs)
            def _(xs=xs):
                for a in range(len(self.groups)):
                    pltpu.make_async_copy(self._row(ins, a, xs, y), outs[a].at[mine], sems[2].at[a]).start()
                    for j, (px, py) in enumerate([(1 - xs, y), (xs, 1 - y), (1 - xs, 1 - y)]):
                        pltpu.make_async_remote_copy(
                            src_ref=self._row(ins, a, px, py), dst_ref=outs[a].at[mine],
                            send_sem=sems[0].at[a, j], recv_sem=sems[1].at[a, j],
                            device_id=(px, py, c), device_id_type=MESH).start()

    def finish(self, ins, outs, sems):
        x, y, c = _place()
        for wait_recv in (True, False):
            for a in range(len(self.groups)):
                for j in range(3):
                    cp = pltpu.make_async_remote_copy(
                        src_ref=self._row(ins, a, 0, 0), dst_ref=outs[a].at[0],
                        send_sem=sems[0].at[a, j], recv_sem=sems[1].at[a, j],
                        device_id=(x, y, c), device_id_type=MESH)
                    if wait_recv:
                        cp.wait_recv()
                    else:
                        cp.wait_send()
        for a in range(len(self.groups)):
            pltpu.make_async_copy(self._row(ins, a, 0, 0), outs[a].at[0], sems[2].at[a]).wait()


def run_exchange(plan, name):
    n_in, n_out = len(plan.ins), len(plan.out_shape)

    def body(*refs):
        parts = (refs[:n_in], refs[n_in:n_in + n_out], refs[n_in + n_out:])
        plan.start(*parts)
        for part, _ in plan.relay_steps(1):
            plan.relay(part, *parts)
        plan.finish(*parts)

    return pl.pallas_call(
        body, name=name,
        in_specs=[ANY] * n_in, out_specs=[ANY] * n_out,
        out_shape=plan.out_shape, scratch_shapes=plan.sems,
    )(*plan.ins)


class SiblingSwapPlan:
    def __init__(self, arrs):
        self.ins = list(arrs)
        na = len(arrs)
        self.out_shape = [jax.ShapeDtypeStruct(a.shape, a.dtype) for a in arrs]
        self.sems = [pltpu.SemaphoreType.DMA((na,)), pltpu.SemaphoreType.DMA((na,))]

    def relay_steps(self, n_steps):
        return []

    @staticmethod
    def _copies(ins, outs, sems):
        x, y, c = _place()
        return [pltpu.make_async_remote_copy(
            src_ref=ins[a], dst_ref=outs[a], send_sem=sems[0].at[a], recv_sem=sems[1].at[a],
            device_id=(x, y, 1 - c), device_id_type=MESH) for a in range(len(ins))]

    def start(self, ins, outs, sems):
        for cp in self._copies(ins, outs, sems):
            cp.start()

    def finish(self, ins, outs, sems):
        for cp in self._copies(ins, outs, sems):
            cp.wait()


def adamw(w, gparts, m, v, name, tr=256):
    rows, cols = w.shape
    parts = gparts.shape[0]
    tr = _row_tile(rows, tr)
    c1 =1.0 - ADAM_B1 ** ADAM_STEP
    c2 = 1.0 - ADAM_B2 ** ADAM_STEP

    def body(w_ref, g_ref, m_ref, v_ref, go_ref, d_ref, mo_ref, vo_ref):
        g = g_ref[0].astype(F32)
        for p_ in range(1, parts):
            g = g + g_ref[p_].astype(F32)
        m_new = ADAM_B1 * m_ref[...] + (1.0 - ADAM_B1) * g
        v_new = ADAM_B2 * v_ref[...] + (1.0 - ADAM_B2) * (g * g)
        m_hat = m_new / c1
        v_hat = v_new / c2
        go_ref[...] = g
        d_ref[...] = -ADAM_LR * (m_hat / (jnp.sqrt(v_hat) + ADAM_EPS) + ADAM_WD * w_ref[...])
        mo_ref[...] = m_new
        vo_ref[...] = v_new

    blk = pl.BlockSpec((tr, cols), lambda i: (i, 0))
    return pl.pallas_call(
        body, name=name,
        grid=(rows // tr,),
        in_specs=[blk, pl.BlockSpec((parts, tr, cols), lambda i: (0, i, 0)), blk, blk],
        out_specs=[blk] * 4,
        out_shape=[jax.ShapeDtypeStruct((rows, cols), F32)] * 4,
        compiler_params=_params("parallel"),
    )(w, gparts, m, v)


def add2(a, b, name, out_dtype):
    return rowwise(lambda a, b: ([a.astype(F32) + b.astype(F32)], []), [a, b], [(a.shape[1], out_dtype)],
                   name=name, tr=256)[0]


def _block_diag(x):
    g, a, b = x.shape
    eye = jnp.eye(g, dtype=x.dtype)
    return (x[:, :, None, :] * eye[:, None, :, None]).reshape(g * a, g * b)


def _diag_blocks(x, a, b):
    per = x.shape[1] // b
    x5 = x.reshape(SSM_SUPER, per, a, per, b)
    eye = jnp.eye(per, dtype=x.dtype)
    return jnp.sum(x5 * eye[None, :, None, :, None], axis=3).reshape(SSM_SUPER * per, a, b)


def _swiglu_epi(g, u):
    s = _sigmoid(g)
    silu = g * s
    return u * (s * (1.0 + g * (1.0 - s))), silu, silu * u


def _residual_epi(scale, with_norm):
    if with_norm:
        def epi(acc, res, gain):
            out = res + scale * acc
            return out, _rms_tile(out, gain)
    else:
        def epi(acc, res):
            return (res + scale * acc,)
    return epi


def _next_norm(next_gain):
    if next_gain is None:
        return [], (F32,)
    return [next_gain], (F32, BF16)


FFN_WIDE = 2816


def ffn_fwd(h, n, wi, wo, next_gain, tag, carry=None):
    if carry is None:
        carry = lambda where, run: run(None)[0]
    dact_g, dact_u, act = carry(f"{tag}_in", lambda ex: _with_exchange(matmul(
        n, [(wi, 0), (wi, 1)], mode="nn", name=f"{tag}_in", separate=True, epi=_swiglu_epi,
        out_dtypes=(BF16, BF16, BF16), exchange=ex, tm=512, tn=FFN_WIDE), ex))
    more, dtypes = _next_norm(next_gain)
    res = carry(f"{tag}_out", lambda ex: _with_exchange(matmul(
        act, wo, mode="nn", name=f"{tag}_out", epi=_residual_epi(0.5, bool(more)), extras=[h] + more,
        out_dtypes=dtypes, exchange=ex, tm=512, tk=FFN_WIDE), ex))
    return res[0], (res[1] if more else None), (h, n, dact_g, dact_u, act)


def ffn_bwd(dh, saved, gain, wi, wo, tag, carry=None, gate=None, on_weight_grads=None):
    h, n, dact_g, dact_u, act = saved
    if carry is None:
        carry = lambda where, run: run(None)[0]
    dg, du = carry(f"{tag}_out_dx", lambda ex: _with_exchange(matmul(
        dh, wo, mode="nt", name=f"{tag}_out_dx", extras=[dact_g, dact_u], out_dtypes=(BF16, BF16),
        epi=lambda acc, fg, fu: (0.5 * acc * fg, 0.5 * acc * fu), exchange=ex, tm=512, tn=FFN_WIDE), ex))
    d_wo, = carry(f"{tag}_out_dw", lambda ex: _with_exchange(matmul(
        act, dh, mode="tn", name=f"{tag}_out_dw", scale=0.5, exchange=ex, tm=FFN_WIDE), ex))
    d_wi = [carry(f"{tag}_in_dw_{k}", lambda ex, half=half: _with_exchange(matmul(
        n, half, mode="tn", name=f"{tag}_in_dw", exchange=ex, tn=FFN_WIDE), ex))[0] for k, half in (("g", dg), ("u", du))]
    if on_weight_grads is not None:
        on_weight_grads(d_wi, d_wo)
    if gate is None:
        epi, more, dtypes = _rms_bwd_tile, [], (F32,)
    else:
        def epi(acc, x, g, dres, e, pre):
            dh_new, d_gain = _rms_bwd_tile(acc, x, g, dres)
            return (dh_new, *_gate_cotangents(dh_new, e, pre), d_gain)
        more, dtypes = list(gate), (F32, BF16, BF16)
    res = carry(f"{tag}_in_dx", lambda ex: _with_exchange(matmul(
        [dg, du], [(wi, 0), (wi, 1)], mode="nt", name=f"{tag}_in_dx", epi=epi, extras=[h, gain, dh] + more,
        out_dtypes=dtypes, col_sums=1, exchange=ex, tm=256, tk=FFN_WIDE), ex))
    return res[0], res[-1], d_wi, d_wo, tuple(res[1:-1])


def _with_exchange(result, exchange):
    return result if exchange is not None else (result, [])


def mix_fwd(h, n, lw, batch, next_gain, tag, exchange=None):
    sw = h.shape[1] // 2
    us, up = matmul(n, [lw["w_in"][:, :sw], lw["w_in"][:, sw:]], mode="nn", name=f"{tag}_in", separate=True,
                    out_dtypes=(F32, F32))
    lam_r, lam_i, bb_r, bb_i = ssm_prep(lw["lam_re"], lw["lam_im"], lw["log_dt"], lw["b_re"], lw["b_im"], f"{tag}_zoh")
    lam = (lam_r.reshape(1, SSM_CH), lam_i.reshape(1, SSM_CH))
    b_mats = [_block_diag(bb.reshape(SSM_GROUPS, SSM_STATE, SSM_GROUP_CH).transpose(0, 2, 1)).astype(BF16)
              for bb in (bb_r, bb_i)]
    c_mats = [_block_diag(cc.transpose(0, 2, 1)).astype(BF16) for cc in (lw["c_re"], -lw["c_im"])]
    bu_re, bu_im = matmul(us, b_mats, mode="nn", name=f"{tag}_bu", separate=True, diag=SSM_SUPER,
                          out_dtypes=(BF16, BF16))
    s_re, s_im, exchanged = ssm_scan(bu_re, bu_im, *lam, batch, f"{tag}_scan", exchange)
    y0, y1 = matmul([s_re, s_im], c_mats, mode="nn", name=f"{tag}_c", diag=SSM_SUPER,
                    epi=lambda acc, u, d: (acc + d * u, _gelu(acc + d * u)), extras=[us, lw["ssm_d"]],
                    out_dtypes=(F32, BF16))
    y2, gl = matmul(y1, lw["w_glu"], mode="nn", name=f"{tag}_glu",
                    epi=lambda acc, y0: (_gelu(y0) * _sigmoid(acc), acc), extras=[y0], out_dtypes=(BF16, F32))
    yp, q = pool_fwd(up, lw["pool_w"], lw["pool_scale"], batch, f"{tag}_pool")
    more, dtypes = _next_norm(next_gain)
    res = matmul([y2, yp], [lw["w_out"][:sw], lw["w_out"][sw:]], mode="nn", name=f"{tag}_out",
                 epi=_residual_epi(1.0, bool(more)), extras=[h] + more, out_dtypes=dtypes)
    saved = (h, n, us, lam, b_mats, c_mats, s_re, s_im, y0, y1, gl, y2, yp, q)
    return res[0], (res[1] if more else None), saved, exchanged


def mix_bwd(dh, saved, lw, batch, tag, exchange=None):
    h, n, us, lam, b_mats, c_mats, s_re, s_im, y0, y1, gl, y2, yp, q = saved
    sw = h.shape[1] // 2
    w_out_s, w_out_p = lw["w_out"][:sw], lw["w_out"][sw:]
    d_wo_s, d_wo_p = matmul([y2, yp], dh, mode="tn", name=f"{tag}_out_dw", separate=True)
    def out_dx_epi(dy2, dyp, y0, gl):
        sg = _sigmoid(gl)
        return dy2, dyp, dy2 * _gelu(y0) * sg * (1.0 - sg)

    dy2, dyp, tg = matmul(dh, [w_out_s, w_out_p], mode="nt", name=f"{tag}_out_dx", separate=True, epi=out_dx_epi,
                          extras=[y0, gl], out_dtypes=(F32, F32, BF16))
    dup, d_pool_w, d_pool_scale = pool_bwd(dyp, q, lw["pool_w"], lw["pool_scale"], batch, f"{tag}_pool_bwd")
    def dy0_epi(acc, dy2, gl, y0, u):
        dy0 = (acc + dy2 * _sigmoid(gl)) * _gelu_grad(y0)
        return dy0, jnp.sum(dy0 * u, axis=0, keepdims=True)

    dy0, d_d = matmul(tg, lw["w_glu"], mode="nt", name=f"{tag}_glu_dx", epi=dy0_epi, extras=[dy2, gl, y0, us],
                      out_dtypes=(F32,), col_sums=1)
    d_w_glu, = matmul(y1, tg, mode="tn", name=f"{tag}_glu_dw")
    gd_re, gd_im = matmul(dy0, c_mats, mode="nt", name=f"{tag}_c_dx", separate=True, diag=SSM_SUPER,
                          out_dtypes=(BF16, BF16))
    d_c_top, d_c_bot = matmul([s_re, s_im], dy0, mode="tn", name=f"{tag}_c_dw", separate=True, diag=SSM_SUPER)
    g_re, g_im, d_lam_r, d_lam_i, exchanged = ssm_scan_bwd(gd_re, gd_im, s_re, s_im, *lam, batch, f"{tag}_scan_bwd",
                                                           exchange)
    dus, = matmul([g_re, g_im], b_mats, mode="nt", name=f"{tag}_bu_dx", diag=SSM_SUPER,
                  epi=lambda acc, dy0, d: (acc + d * dy0,), extras=[dy0, lw["ssm_d"]])
    d_b_re, d_b_im = matmul(us, [g_re, g_im], mode="tn", name=f"{tag}_bu_dw", separate=True, diag=SSM_SUPER)
    d_bb_r = _diag_blocks(d_b_re, SSM_GROUP_CH, SSM_STATE).transpose(0, 2, 1).reshape(SSM_CH, SSM_GROUP_CH)
    d_bb_i = _diag_blocks(d_b_im, SSM_GROUP_CH, SSM_STATE).transpose(0, 2, 1).reshape(SSM_CH, SSM_GROUP_CH)
    d_lr, d_li, d_ldt, d_br, d_bi = ssm_prep_bwd(
        lw["lam_re"], lw["lam_im"], lw["log_dt"], lw["b_re"], lw["b_im"],
        d_lam_r.reshape(SSM_CH, 1), d_lam_i.reshape(SSM_CH, 1), d_bb_r, d_bb_i, f"{tag}_zoh_bwd")
    d_c_re = _diag_blocks(d_c_top, SSM_STATE, SSM_GROUP_CH).transpose(0, 2, 1)
    d_c_im = -_diag_blocks(d_c_bot, SSM_STATE, SSM_GROUP_CH).transpose(0, 2, 1)
    d_w_in_s, d_w_in_p = matmul(n, [dus, dup], mode="tn", name=f"{tag}_in_dw", separate=True)
    dh_new, d_gain = matmul([dus, dup], [lw["w_in"][:, :sw], lw["w_in"][:, sw:]], mode="nt", name=f"{tag}_in_dx",
                            epi=_rms_bwd_tile, extras=[h, lw["mix_norm"], dh], out_dtypes=(F32,), col_sums=1, tm=512)
    grads = dict(mix_norm=d_gain, w_in=jnp.concatenate([d_w_in_s, d_w_in_p], axis=1),
                 ssm_lambda_re=d_lr, ssm_lambda_im=d_li, ssm_log_dt=d_ldt, ssm_b_re=d_br, ssm_b_im=d_bi,
                 ssm_c_re=d_c_re, ssm_c_im=d_c_im, ssm_d=d_d, ssm_w_glu=d_w_glu, pool_w=d_pool_w,
                 pool_scale=d_pool_scale, w_out=jnp.concatenate([d_wo_s, d_wo_p], axis=0))
    return dh_new, grads, exchanged


def ple_fwd(h, n, p, w_gate, w_proj, next_gain, tag):
    e, = matmul(p, w_proj, mode="nn", name=f"{tag}_proj")
    if next_gain is None:
        def epi(acc, e, res):
            return res + _sigmoid(acc) * e, acc
        more, dtypes = [], (F32, F32)
    else:
        def epi(acc, e, res, gain):
            out = res + _sigmoid(acc) * e
            return out, acc, _rms_tile(out, gain)
        more, dtypes = [next_gain], (F32, F32, BF16)
    res = matmul(n, w_gate, mode="nn", name=f"{tag}_gate", epi=epi, extras=[e, h] + more, out_dtypes=dtypes, tm=512)
    return res[0], (res[2] if more else None), (h, n, e, res[1])


def _gate_cotangents(dh, e, pre):
    s = _sigmoid(pre)
    return dh * e * s * (1.0 - s), dh * s


def ple_bwd(dh, dpre, de, saved, p, gain, w_gate, tag):
    h, n, e, pre = saved
    d_w_gate, = matmul(n, dpre, mode="tn", name=f"{tag}_gate_dw")
    d_w_proj, = matmul(p, de, mode="tn", name=f"{tag}_proj_dw")
    dh_new, d_gain = matmul(dpre, w_gate, mode="nt", name=f"{tag}_gate_dx", epi=_rms_bwd_tile,
                            extras=[h, gain, dh], out_dtypes=(F32,), col_sums=1, tm=512)
    return dh_new, d_gain, d_w_gate, d_w_proj


def loss_head(h, gain, target, e, pre, name):
    d = h.shape[1]

    def fn(h, g, t, e, pre):
        r = lax.rsqrt(jnp.mean(h * h, axis=-1, keepdims=True) + EPS)
        diff = h * r * g - t
        sq = jnp.sum(jnp.sum(diff * diff, axis=1, keepdims=True), axis=0, keepdims=True)
        dy = diff * (1.0 / d)
        w = dy * g
        dh = r * w - h * (r * r * r) * jnp.mean(h * w, axis=-1, keepdims=True)
        return [dh, *_gate_cotangents(dh, e, pre)], [sq, jnp.sum(dy * (h * r), axis=0, keepdims=True)]

    dh, dpre, de, sq, d_gain = rowwise(fn, [h, gain, target, e, pre], [(d, F32), (d, BF16), (d, BF16)],
                                       [(1, 1), (1, d)], name=name, tr=256)
    return 0.5 / d * sq[0, 0], dh, dpre, de, d_gain


SHARDED = {
    "ffn1_wi": 1, "ffn1_wo": 0, "w_in": 0, "ssm_w_glu": 0, "w_out": 0, "ffn2_wi": 1, "ffn2_wo": 0,
    "ple_w_gate": 0, "ple_w_proj": 1,
}
WEIGHTS = ["ffn1_norm", "ffn1_wi", "ffn1_wo", "mix_norm", "w_in", "ssm_lambda_re", "ssm_lambda_im", "ssm_log_dt",
           "ssm_b_re", "ssm_b_im", "ssm_c_re", "ssm_c_im", "ssm_d", "ssm_w_glu", "pool_w", "pool_scale", "w_out",
           "ffn2_norm", "ffn2_wi", "ffn2_wo", "ple_norm", "ple_w_gate", "ple_w_proj", "final_norm"]
REPLICATED = [n for n in WEIGHTS if n not in SHARDED]


HALVED = ("ffn1_wi", "ffn2_wi")


def _unshard(gathered, axis, halved):
    if halved:
        _, rows, cols = gathered.shape
        return gathered.reshape(2, 4, rows, cols).transpose(0, 2, 1, 3).reshape(2, rows, 4 * cols)
    g = jnp.moveaxis(gathered, 0, axis)
    shp = g.shape
    return g.reshape(shp[:axis] + (shp[axis] * shp[axis + 1],) + shp[axis + 2:])


def _split_for_scatter(full, axis, c, halved):
    if halved:
        rows, cols = full[0].shape

        def pick(cc):
            return [lax.dynamic_index_in_dim(h.reshape(rows, 2, 2, cols // 4), cc, 2, keepdims=False).transpose(1, 0, 2)
                    for h in full]

        return pick(c), [s.astype(BF16) for s in pick(1 - c)]
    shp = full.shape
    g = full.reshape(shp[:axis] + (4, 2, shp[axis] // N_DEV) + shp[axis + 1:])
    keep = lax.dynamic_index_in_dim(g, c, axis + 1, keepdims=False)
    send = lax.dynamic_index_in_dim(g, 1 - c, axis + 1, keepdims=False)
    return jnp.moveaxis(keep, axis, 0), jnp.moveaxis(send, axis, 0).astype(BF16)


def _pack(arrs):
    pieces = []
    for a in arrs:
        flat = a.reshape(-1)
        pad = (-flat.shape[0]) % PACK
        pieces.append(jnp.pad(flat, (0, pad)).reshape(-1, LANES))
    return jnp.concatenate(pieces, axis=0)


def _unpack(packed, shapes):
    out, row = [], 0
    for s in shapes:
        size = math.prod(s)
        rows = (size + PACK - 1) // PACK * SUBLANES
        out.append(packed[row:row + rows].reshape(-1)[:size].reshape(s))
        row += rows
    return out


class NoExchange:
    def __init__(self, full):
        self.full = full

    def big(self, i, name):
        w = self.full[name][i]
        if name in HALVED:
            rows, cols = w.shape
            w = w.reshape(rows, 2, cols // 2).transpose(1, 0, 2)
        return w

    def fwd_exchange(self, i, where):
        return None

    def fwd_done(self, i, where, results):
        pass

    def bwd_exchange(self, i, where):
        return None

    def bwd_done(self, i, where, results):
        pass

    def layer_grads(self, i, grads, names):
        pass

    def small_grads(self, i, grads):
        pass

    def final_grad(self, d_final):
        pass


def local_step(x, p, target, rep, hooks):
    batch, seq, d = x.shape
    n_tok = batch * seq
    h = x.reshape(n_tok, d)
    saved = []
    n = rms_fwd(h, rep["ffn1_norm"][0].reshape(1, d), "first_norm")
    for i in range(DEPTH):
        lw = _layer_weights(rep, i, d)
        big = lambda name, i=i: hooks.big(i, name)
        next_gain = rep["ffn1_norm"][i + 1].reshape(1, d) if i + 1 < DEPTH else None
        def carry(where, run, i=i):
            outs, exchanged = run(hooks.fwd_exchange(i, where))
            hooks.fwd_done(i, where, exchanged)
            return outs

        h, n, s1 = ffn_fwd(h, n, big("ffn1_wi"), big("ffn1_wo"), lw["mix_norm"], "ffn1", carry)
        lw.update(w_in=big("w_in"), w_glu=big("ssm_w_glu"), w_out=big("w_out"))
        h, n, s2, exchanged = mix_fwd(h, n, lw, batch, lw["ffn2_norm"], "mix", hooks.fwd_exchange(i, "scan"))
        hooks.fwd_done(i, "scan", exchanged)
        h, n, s3 = ffn_fwd(h, n, big("ffn2_wi"), big("ffn2_wo"), lw["ple_norm"], "ffn2", carry)
        p_i = p[i].reshape(n_tok, -1)
        h, n, s4 = ple_fwd(h, n, p_i, big("ple_w_gate"), big("ple_w_proj"), next_gain, "ple")
        saved.append((s1, s2, s3, s4, p_i))
    last_gate = saved[-1][3][2:]
    loss, dh, dpre, de, d_final = loss_head(h, rep["final_norm"].reshape(1, d), target.reshape(n_tok, d), *last_gate,
                                            "loss_head")
    hooks.final_grad(d_final)
    per_layer = [None] * DEPTH
    for i in reversed(range(DEPTH)):
        lw = _layer_weights(rep, i, d)
        big = lambda name, i=i: hooks.big(i, name)
        lw.update(w_in=big("w_in"), w_glu=big("ssm_w_glu"), w_out=big("w_out"))
        s1, s2, s3, s4, p_i = saved[i]
        g = {}
        dh, g["ple_norm"], g["ple_w_gate"], g["ple_w_proj"] = ple_bwd(dh, dpre, de, s4, p_i, lw["ple_norm"],
                                                                     big("ple_w_gate"), "ple")
        def carry(where, run, i=i):
            outs, exchanged = run(hooks.bwd_exchange(i, where))
            hooks.bwd_done(i, where, exchanged)
            return outs

        dh, g["ffn2_norm"], g["ffn2_wi"], g["ffn2_wo"], _ = ffn_bwd(
            dh, s3, lw["ffn2_norm"], big("ffn2_wi"), big("ffn2_wo"), "ffn2", carry)
        dh, gm, exchanged = mix_bwd(dh, s2, lw, batch, "mix", hooks.bwd_exchange(i, "scan"))
        hooks.bwd_done(i, "scan", exchanged)
        g.update(gm)
        hooks.layer_grads(i, g, [k for k in SHARDED if k not in LAST_GRADS])
        hooks.small_grads(i, {k: g[k] for k in REPLICATED if k in g})
        below = saved[i - 1][3][2:] if i > 0 else None

        def ffn1_grads(d_wi, d_wo, i=i, g=g):
            g["ffn1_wi"], g["ffn1_wo"] = d_wi, d_wo
            hooks.layer_grads(i, g, list(LAST_GRADS))

        dh, g["ffn1_norm"], _, _, gate_ct = ffn_bwd(
            dh, s1, lw["ffn1_norm"], big("ffn1_wi"), big("ffn1_wo"), "ffn1", carry, below, ffn1_grads)
        hooks.small_grads(i, {"ffn1_norm": g["ffn1_norm"]})
        if i > 0:
            dpre, de = gate_ct
        per_layer[i] = g
    return loss, dh.reshape(batch, seq, d), per_layer, d_final


LAST_GRADS = ("ffn1_wi", "ffn1_wo")


def _layer_weights(w, i, d):
    sw = d // 2
    lw = {}
    lw["pool_w"] = w["pool_w"][i]
    for k in ("ffn1_norm", "mix_norm", "ffn2_norm", "ple_norm"):
        lw[k] = w[k][i].reshape(1, d)
    lw["ssm_d"] = w["ssm_d"][i].reshape(1, sw)
    lw["pool_scale"] = w["pool_scale"][i].reshape(1, sw)
    lw["lam_re"] = w["ssm_lambda_re"][i].reshape(SSM_CH, 1)
    lw["lam_im"] = w["ssm_lambda_im"][i].reshape(SSM_CH, 1)
    lw["log_dt"] = jnp.repeat(w["ssm_log_dt"][i], SSM_STATE).reshape(SSM_CH, 1)
    lw["b_re"] = w["ssm_b_re"][i].reshape(SSM_CH, SSM_GROUP_CH)
    lw["b_im"] = w["ssm_b_im"][i].reshape(SSM_CH, SSM_GROUP_CH)
    lw["c_re"] = w["ssm_c_re"][i]
    lw["c_im"] = w["ssm_c_im"][i]
    return lw


class MeshExchange:
    FIRST = ("ffn1_wi", "ffn1_wo")
    FIRST_LAYER_PLAN = {"ffn1_in": ("w_in", "ssm_w_glu", "w_out", "ffn2_wi", "ffn2_wo"),
                        "ffn1_out": ("ple_w_gate", "ple_w_proj")}
    FWD_PLAN = {"scan": ("ffn1_wi", "ffn1_wo", "w_in", "ssm_w_glu", "w_out"),
                "ffn2_in": ("ffn2_wi", "ffn2_wo"), "ffn2_out": ("ple_w_gate", "ple_w_proj")}
    BWD_PLAN = {"ffn2_in_dx": ("ffn1_wo", "ffn2_wo", "ssm_w_glu", "w_out", "ple_w_gate", "ple_w_proj"),
                "scan": ("ffn1_wi", "ffn2_wi", "w_in")}
    LAST_LAYER_PLAN = {"ffn1_out_dx": ("ffn2_wi", "w_in"),
                       "ffn1_out_dw": ("ffn2_wo", "w_out", "ple_w_gate", "ssm_w_glu", "ple_w_proj"),
                       "ffn1_in_dx": LAST_GRADS}

    def __init__(self, shards):
        self.small = {}
        self.d_final = None
        self.small_parts = []
        self.small_rest = None
        self.swaps = {}
        self.shards = shards
        self.c = lax.axis_index("c")
        self.gathered = {}
        self.chip_sums = {}
        self.from_chips = {}
        self.pending = None
        first = run_exchange(GatherPlan([shards[0][k] for k in self.FIRST]), "gather_first_weights")
        self._store(self.gathered, 0, self.FIRST, first)

    @staticmethod
    def _store(where, layer, names, results):
        for k, r in zip(names, results):
            where[(layer, k)] = r

    def big(self, i, name):
        return _unshard(self.gathered[(i, name)], SHARDED[name], name in HALVED)

    def fwd_exchange(self, i, where):
        if i == 0 and where in self.FIRST_LAYER_PLAN:
            layer, names = 0, self.FIRST_LAYER_PLAN[where]
        elif where in self.FWD_PLAN and i + 1 < DEPTH:
            layer, names = i + 1, self.FWD_PLAN[where]
        else:
            return None
        self.pending = (layer, names)
        return GatherPlan([self.shards[layer][k] for k in names])

    def fwd_done(self, i, where, results):
        if results:
            self._store(self.gathered, *self.pending, results)

    def layer_grads(self, i, grads, names):
        pieces, sends = [], []
        for k in names:
            keep, send = _split_for_scatter(grads[k], SHARDED[k], self.c, k in HALVED)
            keep, send = (keep, send) if isinstance(keep, list) else ([keep], [send])
            pieces.append(keep)
            sends += send
        if i == 0:
            self._add_sibling(0, names, pieces, run_exchange(SiblingSwapPlan(sends), "reduce_core_pair"))
        else:
            self.swaps[(i, tuple(names) == LAST_GRADS)] = (names, pieces, sends)

    def _add_sibling(self, i, names, pieces, from_sibling):
        got = iter(from_sibling)
        for k, keep in zip(names, pieces):
            sums = []
            for part in keep:
                cols = part.shape[-1]
                sums.append(add2(part.reshape(-1, cols), next(got).reshape(-1, cols), f"sum_core_pair_{k}",
                                 BF16).reshape(part.shape))
            self.chip_sums[(i, k)] = sums if len(sums) > 1 else sums[0]

    def small_grads(self, i, grads):
        self.small.setdefault(i, {}).update(grads)

    def final_grad(self, d_final):
        self.d_final = d_final

    def _small_pack(self, d_final, first_norm_grad=None):
        pieces = []
        for k in REPLICATED:
            if k == "final_norm":
                pieces.append(d_final)
            elif k == "ffn1_norm" and first_norm_grad is None:
                pieces.append(jnp.stack([self.small[i][k] for i in range(1, DEPTH)], axis=0))
            else:
                pieces.append(jnp.stack([self.small[i][k] for i in range(DEPTH)], axis=0))
        return _pack(pieces)

    def bwd_exchange(self, i, where):
        if where in self.BWD_PLAN and i + 1 < DEPTH:
            layer, names = i + 1, self.BWD_PLAN[where]
        elif i == 0 and where in self.LAST_LAYER_PLAN:
            layer, names = 0, self.LAST_LAYER_PLAN[where]
        elif i == 0 and where in ("ffn1_in_dw_g", "ffn1_in_dw_u"):
            if where.endswith("g"):
                pack = self._small_pack(self.d_final)
                half = pack.shape[0] // 2 // SUBLANES * SUBLANES
                part, self.small_rest = pack[:half], pack[half:]
            else:
                part = self.small_rest
            self.pending = "small"
            return GatherPlan([part])
        elif i > 0 and where in ("ffn1_out_dx", "ffn1_in_dx"):
            self.pending = ("swap", i, where == "ffn1_in_dx")
            return SiblingSwapPlan(self.swaps[self.pending[1:]][2])
        else:
            return None
        self.pending = (layer, names)
        return ChipScatterPlan([self.chip_sums[(layer, k)] for k in names])

    def bwd_done(self, i, where, results):
        if not results:
            return
        if self.pending == "small":
            self.small_parts += results
        elif self.pending[0] == "swap":
            names, pieces, _ = self.swaps[self.pending[1:]]
            self._add_sibling(self.pending[1], names, pieces, results)
        else:
            self._store(self.from_chips, *self.pending, results)

    def small_gathered(self):
        first = self.small[0]["ffn1_norm"].reshape(SUBLANES, LANES)
        first_all, = run_exchange(GatherPlan([first]), "gather_first_gain_grad")
        return jnp.concatenate([first_all] + self.small_parts, axis=1)


def kernel(x, p, ffn1_norm, ffn1_wi, ffn1_wo, mix_norm, w_in, ssm_lambda_re, ssm_lambda_im, ssm_log_dt, ssm_b_re, ssm_b_im, ssm_c_re, ssm_c_im, ssm_d, ssm_w_glu, pool_w, pool_scale, w_out, ffn2_norm, ffn2_wi, ffn2_wo, ple_norm, ple_w_gate, ple_w_proj, final_norm, loss_target, m_ffn1_norm, m_ffn1_wi, m_ffn1_wo, m_mix_norm, m_w_in, m_ssm_lambda_re, m_ssm_lambda_im, m_ssm_log_dt, m_ssm_b_re, m_ssm_b_im, m_ssm_c_re, m_ssm_c_im, m_ssm_d, m_ssm_w_glu, m_pool_w, m_pool_scale, m_w_out, m_ffn2_norm, m_ffn2_wi, m_ffn2_wo, m_ple_norm, m_ple_w_gate, m_ple_w_proj, m_final_norm, v_ffn1_norm, v_ffn1_wi, v_ffn1_wo, v_mix_norm, v_w_in, v_ssm_lambda_re, v_ssm_lambda_im, v_ssm_log_dt, v_ssm_b_re, v_ssm_b_im, v_ssm_c_re, v_ssm_c_im, v_ssm_d, v_ssm_w_glu, v_pool_w, v_pool_scale, v_w_out, v_ffn2_norm, v_ffn2_wi, v_ffn2_wo, v_ple_norm, v_ple_w_gate, v_ple_w_proj, v_final_norm):
    args = dict(locals())
    wts = {k: args[k] for k in WEIGHTS}
    rep = {k: wts[k] for k in REPLICATED}

    shards = [{k: wts[k][i].astype(BF16) for k in SHARDED} for i in range(DEPTH)]
    exchange = MeshExchange(shards)
    loss_local, grad_x, per_layer, d_final = local_step(x, p, loss_target, rep, exchange)
    loss = lax.psum(loss_local, ("x", "y", "c"))

    outs = {}
    for k in SHARDED:
        shp = wts[k].shape
        cols = shp[-1]
        parts = jnp.stack([exchange.from_chips[(i, k)] for i in range(DEPTH)], axis=1)
        res = adamw(wts[k].reshape(-1, cols), parts.reshape(4, -1, cols), args["m_" + k].reshape(-1, cols),
                    args["v_" + k].reshape(-1, cols), f"adamw_{k}")
        outs[k] = [r.reshape(shp) for r in res]

    rep_shapes = [wts[k].shape for k in REPLICATED]
    all_g = exchange.small_gathered()
    res = adamw(_pack([wts[k] for k in REPLICATED]), all_g, _pack([args["m_" + k] for k in REPLICATED]),
                _pack([args["v_" + k] for k in REPLICATED]), "adamw_small")
    unpacked = [_unpack(r, rep_shapes) for r in res]
    for j, k in enumerate(REPLICATED):
        outs[k] = [unpacked[q][j] for q in range(4)]

    result = [loss, grad_x]
    for q in range(4):
        result += [outs[k][q] for k in WEIGHTS]
    return tuple(result)
```

```python
import math

import jax
import jax.numpy as jnp
from jax import lax
from jax.experimental import pallas as pl
from jax.experimental.pallas import tpu as pltpu

F32 = jnp.float32
BF16 = jnp.bfloat16
MESH = pl.DeviceIdType.MESH
ANY = pl.BlockSpec(memory_space=pl.ANY)

N_DEV = 8
DEPTH = 4
EPS = 1e-6
SSM_GROUPS = 32
SSM_GROUP_CH = 16
SSM_STATE = 64
SSM_CH = SSM_GROUPS * SSM_STATE
SSM_SUPER = 2
POOL_WINDOWS = (2, 4, 8, 16)
POOL_HALO = 16
ADAM_LR, ADAM_B1, ADAM_B2, ADAM_EPS, ADAM_WD, ADAM_STEP = 0.001, 0.9, 0.999, 1e-08, 0.01, 10

V7X_VMEM_BYTES = 64 * 1024 * 1024
VMEM_LIMIT_BYTES = V7X_VMEM_BYTES - 12 * 1024 * 1024
LANES = 128
SUBLANES = 8
PACK = SUBLANES * LANES


def _params(*sem):
    return pltpu.CompilerParams(dimension_semantics=sem or None, vmem_limit_bytes=VMEM_LIMIT_BYTES)


def _tile(n, pref):
    if n <= pref:
        return n
    t = pref - pref % LANES
    while t >= LANES:
        if n % t == 0:
            return t
        t -= LANES
    raise ValueError(f"no lane-aligned tile for {n}")


def _row_tile(rows, pref):
    if rows <= pref:
        return rows
    t = pref - pref % SUBLANES
    while t >= SUBLANES:
        if rows % t == 0:
            return t
        t -= SUBLANES
    raise ValueError(f"no sublane-aligned tile for {rows}")


_DIMS = {"nn": ((1,), (0,)), "nt": ((1,), (1,)), "tn": ((0,), (0,))}


def matmul(a, b, *, mode, name, out_dtypes=None, epi=None, extras=(), separate=False, diag=1, col_sums=0,
           stack_out=False, scale=1.0, exchange=None, tm=1024, tn=1024, tk=1024):
    a_list = list(a) if isinstance(a, list) else [a]
    b_list = list(b) if isinstance(b, list) else [b]
    a_planes = [x[1] if isinstance(x, tuple) else None for x in a_list]
    b_planes = [x[1] if isinstance(x, tuple) else None for x in b_list]
    a_list = [x[0] if isinstance(x, tuple) else x for x in a_list]
    b_list = [x[0] if isinstance(x, tuple) else x for x in b_list]
    a_shape, b_shape = a_list[0].shape[-2:], b_list[0].shape[-2:]
    n_terms = max(len(a_list), len(b_list))
    a_idx = [0] * n_terms if len(a_list) == 1 else list(range(n_terms))
    b_idx = [0] * n_terms if len(b_list) == 1 else list(range(n_terms))
    n_acc = n_terms if separate else 1
    assert not (stack_out or scale != 1.0) or epi is None
    if out_dtypes is None:
        out_dtypes = (F32,) * (1 if (epi is not None or stack_out) else n_acc)
    in_place = epi is None
    if mode == "tn":
        K, M = a_shape
        K2, N = b_shape
    elif mode == "nt":
        M, K = a_shape
        N, K2 = b_shape
    else:
        M, K = a_shape
        K2, N = b_shape
    assert K == K2, (name, a_shape, b_shape)
    if mode == "tn":
        tm, tn, tk = _tile(M // diag, tm), _tile(N // diag, tn), _tile(K, tk)
        nk = K // tk
        N = N // diag
        row_tiles, col_tiles = (M // diag) // tm, N // tn
        a_blk, a_map = (tk, tm), lambda i, j, k: (k, i)
        b_blk, b_map = (tk, tn), lambda i, j, k: (k, (i // row_tiles) * col_tiles + j)
    else:
        tm, tn, tk = _tile(M, tm), _tile(N // diag, tn), _tile(K // diag, tk)
        nk = (K // diag) // tk
        col_tiles = (N // diag) // tn
        a_blk, a_map = (tm, tk), lambda i, j, k: (i, (j // col_tiles) * nk + k)
        if mode == "nt":
            b_blk, b_map = (tn, tk), lambda i, j, k: (j, (j // col_tiles) * nk + k)
        else:
            b_blk, b_map = (tk, tn), lambda i, j, k: ((j // col_tiles) * nk + k, j)

    def plane_spec(blk, index_map, plane):
        if plane is None:
            return pl.BlockSpec(blk, index_map)
        return pl.BlockSpec((None,) + blk, lambda i, j, k: (plane,) + index_map(i, j, k))

    a_specs = [plane_spec(a_blk, a_map, p_) for p_ in a_planes]
    b_specs = [plane_spec(b_blk, b_map, p_) for p_ in b_planes]
    assert not col_sums or N == tn, (name, N, tn)
    ex_specs = []
    for e in extras:
        if e.shape == (M, N):
            ex_specs.append(pl.BlockSpec((tm, tn), lambda i, j, k: (i, j)))
        elif e.shape == (1, N):
            ex_specs.append(pl.BlockSpec((1, tn), lambda i, j, k: (0, j)))
        elif e.shape == (M, 1):
            ex_specs.append(pl.BlockSpec((tm, 1), lambda i, j, k: (i, 0)))
        else:
            raise ValueError((name, e.shape, (M, N)))
    na, nb, ne, no = len(a_list), len(b_list), len(extras), len(out_dtypes)
    dims = (_DIMS[mode], ((), ()))

    n_scratch = n_acc if (nk > 1 and not in_place) else 0
    x_ins, x_out, x_sems = _exchange_args(exchange)
    grid = (M // tm, N // tn, nk)

    def body(*refs):
        a_refs, b_refs, ex_refs, xin, out_refs, sum_refs, xout, acc_refs, xsems = _split_refs(
            refs, na, nb, ne, len(x_ins), no, col_sums, len(x_out), n_scratch)
        if exchange is not None:
            step = (pl.program_id(0) * grid[1] + pl.program_id(1)) * nk + pl.program_id(2)
            _run_exchange(exchange, step, grid[0] * grid[1] * nk, (xin, xout, xsems))
        a_vals = [r[...].astype(BF16) for r in a_refs]
        b_vals = [r[...].astype(BF16) for r in b_refs]
        prods = [lax.dot_general(a_vals[a_idx[t]], b_vals[b_idx[t]], dims, preferred_element_type=F32)
                 for t in range(n_terms)]
        if not separate:
            total = prods[0]
            for p_ in prods[1:]:
                total = total + p_
            prods = [total]

        def finish(accs):
            res = epi(*accs, *[e[...] for e in ex_refs]) if epi is not None else tuple(accs)
            for r, v in zip(out_refs, res[:no]):
                r[...] = v.astype(r.dtype)
            first_rows = pl.program_id(0) == 0
            for r, v in zip(sum_refs, res[no:]):
                @pl.when(first_rows)
                def _(r=r, v=v):
                    r[...] = v

                @pl.when(jnp.logical_not(first_rows))
                def _(r=r, v=v):
                    r[...] += v

        if in_place:
            dst = [(out_refs[0], t) for t in range(n_acc)] if stack_out else [(r, None) for r in out_refs]

            def read(r, t):
                return r[...] if t is None else r[t]

            assert nk == 1 or all(dt == F32 for dt in out_dtypes), name

            def write(r, t, v):
                if t is None:
                    r[...] = v.astype(r.dtype)
                else:
                    r[t] = v.astype(r.dtype)

            if nk == 1:
                for (r, t), v in zip(dst, prods):
                    write(r, t, v * scale if scale != 1.0 else v)
            else:
                k = pl.program_id(2)

                @pl.when(k == 0)
                def _():
                    for (r, t), v in zip(dst, prods):
                        write(r, t, v)

                @pl.when(jnp.logical_and(k > 0, k < nk - 1))
                def _():
                    for (r, t), v in zip(dst, prods):
                        write(r, t, read(r, t) + v)

                @pl.when(k == nk - 1)
                def _():
                    for (r, t), v in zip(dst, prods):
                        total = read(r, t) + v
                        write(r, t, total * scale if scale != 1.0 else total)
        elif nk == 1:
            finish(prods)
        else:
            k = pl.program_id(2)

            @pl.when(k == 0)
            def _():
                for r, v in zip(acc_refs, prods):
                    r[...] = v

            @pl.when(jnp.logical_and(k > 0, k < nk - 1))
            def _():
                for r, v in zip(acc_refs, prods):
                    r[...] += v

            @pl.when(k == nk - 1)
            def _():
                finish([r[...] + v for r, v in zip(acc_refs, prods)])

    if stack_out:
        out_specs = [pl.BlockSpec((n_acc, tm, tn), lambda i, j, k: (0, i, j))]
        out_shape = [jax.ShapeDtypeStruct((n_acc, M, N), F32)]
    else:
        out_specs = [pl.BlockSpec((tm, tn), lambda i, j, k: (i, j))] * no
        out_shape = [jax.ShapeDtypeStruct((M, N), dt) for dt in out_dtypes]
    sequential = col_sums or exchange is not None
    outs = pl.pallas_call(
        body,
        name=name,
        grid=grid,
        in_specs=a_specs + b_specs + ex_specs + [ANY] * len(x_ins),
        out_specs=out_specs + [pl.BlockSpec((1, tn), lambda i, j, k: (0, j))] * col_sums + [ANY] * len(x_out),
        out_shape=out_shape + [jax.ShapeDtypeStruct((1, N), F32)] * col_sums + x_out,
        scratch_shapes=[pltpu.VMEM((tm, tn), F32)] * n_scratch + x_sems,
        compiler_params=_params(*(("arbitrary",) * 3 if sequential else ("parallel", "parallel", "arbitrary"))),
    )(*a_list, *b_list, *extras, *x_ins)
    if exchange is not None:
        n_own = len(outs) - len(x_out)
        return list(outs[:n_own]), list(outs[n_own:])
    return outs


def rowwise(fn, ins, outs, accs=(), *, name, tr=512):
    R = max(x.shape[0] for x in ins)
    tr = _row_tile(R, tr)
    in_specs = []
    for x in ins:
        if x.shape[0] == R and x.ndim == 2:
            in_specs.append(pl.BlockSpec((tr, x.shape[1]), lambda i: (i, 0)))
        else:
            in_specs.append(pl.BlockSpec(x.shape, lambda i, _n=x.ndim: (0,) * _n))
    ni, no = len(ins), len(outs)

    def body(*refs):
        i = pl.program_id(0)
        row_vals, acc_vals = fn(*[r[...] for r in refs[:ni]])
        for r, v in zip(refs[ni:ni + no], row_vals):
            r[...] = v.astype(r.dtype)
        for r, v in zip(refs[ni + no:], acc_vals):
            @pl.when(i == 0)
            def _(r=r, v=v):
                r[...] = v

            @pl.when(i > 0)
            def _(r=r, v=v):
                r[...] += v

    return pl.pallas_call(
        body,
        name=name,
        grid=(R // tr,),
        in_specs=in_specs,
        out_specs=[pl.BlockSpec((tr, c), lambda i: (i, 0)) for c, _ in outs]
        + [pl.BlockSpec(s, lambda i: (0, 0)) for s in accs],
        out_shape=[jax.ShapeDtypeStruct((R, c), dt) for c, dt in outs]
        + [jax.ShapeDtypeStruct(s, F32) for s in accs],
        compiler_params=_params("arbitrary"),
    )(*ins)


def _sigmoid(x):
    return 1.0 / (1.0 + jnp.exp(-x))


_GELU_C = math.sqrt(2.0 / math.pi)


def _gelu(x):
    return 0.5 * x * (1.0 + jnp.tanh(_GELU_C * (x + 0.044715 * (x * x * x))))


def _gelu_grad(x):
    t = jnp.tanh(_GELU_C * (x + 0.044715 * (x * x * x)))
    return 0.5 * (1.0 + t) + 0.5 * x * (1.0 - t * t) * (_GELU_C * (1.0 + 3.0 * 0.044715 * (x * x)))


def rms_fwd(x, g, name):
    def fn(x, g):
        r = lax.rsqrt(jnp.mean(x * x, axis=-1, keepdims=True) + EPS)
        return [x * r * g], []

    return rowwise(fn, [x, g], [(x.shape[1], BF16)], name=name)[0]


def _rms_tile(x, g):
    return x * lax.rsqrt(jnp.mean(x * x, axis=-1, keepdims=True) + EPS) * g


def _rms_bwd_tile(dn, x, g, dres):
    r = lax.rsqrt(jnp.mean(x * x, axis=-1, keepdims=True) + EPS)
    w = dn * g
    dx = r * w - x * (r * r * r) * jnp.mean(x * w, axis=-1, keepdims=True)
    return dres + dx, jnp.sum(dn * (x * r), axis=0, keepdims=True)


def _whole(shape):
    return pl.BlockSpec(shape, lambda: (0,) * len(shape))


def _zoh(lr, li, ldt):
    dt = jnp.exp(ldt)
    mag = jnp.exp(lr * dt)
    ar, ai = mag * jnp.cos(li * dt), mag * jnp.sin(li * dt)
    den = lr * lr + li * li
    kr = ((ar - 1.0) * lr + ai * li) / den
    ki = (ai * lr - (ar - 1.0) * li) / den
    return dt, ar, ai, den, kr, ki


def ssm_prep(lam_re, lam_im, log_dt, b_re, b_im, name):
    n = SSM_CH

    def body(lr_ref, li_ref, ldt_ref, br_ref, bi_ref, ar_ref, ai_ref, bbr_ref, bbi_ref):
        _, ar, ai, _, kr, ki = _zoh(lr_ref[...], li_ref[...], ldt_ref[...])
        br, bi = br_ref[...], bi_ref[...]
        ar_ref[...] = ar
        ai_ref[...] = ai
        bbr_ref[...] = kr * br - ki * bi
        bbi_ref[...] = kr * bi + ki * br

    rows = lam_re.shape[0]
    col, mat = pl.BlockSpec((n, 1), lambda i: (i, 0)), pl.BlockSpec((n, SSM_GROUP_CH), lambda i: (i, 0))
    return pl.pallas_call(
        body, name=name,
        grid=(rows // n,),
        in_specs=[col] * 3 + [mat] * 2,
        out_specs=[col] * 2 + [mat] * 2,
        out_shape=[jax.ShapeDtypeStruct((rows, 1), F32)] * 2 + [jax.ShapeDtypeStruct((rows, SSM_GROUP_CH), F32)] * 2,
        compiler_params=_params("parallel"),
    )(lam_re, lam_im, log_dt, b_re, b_im)


def ssm_prep_bwd(lam_re, lam_im, log_dt, b_re, b_im, d_ar, d_ai, d_bbr, d_bbi, name):
    n = lam_re.shape[0]
    n_groups = n // SSM_STATE

    def body(lr_ref, li_ref, ldt_ref, br_ref, bi_ref, dar_ref, dai_ref, dbr_ref, dbi_ref,
             glr_ref, gli_ref, gdt_ref, gbr_ref, gbi_ref):
        lr, li = lr_ref[...], li_ref[...]
        dt, ar, ai, den, kr, ki = _zoh(lr, li, ldt_ref[...])
        br, bi, dbr, dbi = br_ref[...], bi_ref[...], dbr_ref[...], dbi_ref[...]
        gbr_ref[...] = kr * dbr + ki * dbi
        gbi_ref[...] = kr * dbi - ki * dbr
        gkr = jnp.sum(br * dbr + bi * dbi, axis=1, keepdims=True)
        gki = jnp.sum(br * dbi - bi * dbr, axis=1, keepdims=True)
        gar = dar_ref[...] + (gkr * lr - gki * li) / den
        gai = dai_ref[...] + (gki * lr + gkr * li) / den
        qr, qi = -(kr * lr + ki * li) / den, -(ki * lr - kr * li) / den
        g1r, g1i = qr * gkr + qi * gki, qr * gki - qi * gkr
        g2r, g2i = dt * (ar * gar + ai * gai), dt * (ar * gai - ai * gar)
        glr_ref[...] = g1r + g2r
        gli_ref[...] = g1i + g2i
        pr, pi_ = lr * ar - li * ai, lr * ai + li * ar
        gdt = (pr * gar + pi_ * gai) * dt
        grp = lax.broadcasted_iota(jnp.int32, (n, n_groups), 0) // SSM_STATE
        sel = grp == lax.broadcasted_iota(jnp.int32, (n, n_groups), 1)
        gdt_ref[...] = jnp.sum(jnp.where(sel, gdt, 0.0), axis=0, keepdims=True)

    col, mat = (n, 1), (n, SSM_GROUP_CH)
    return pl.pallas_call(
        body, name=name,
        in_specs=[_whole(col)] * 3 + [_whole(mat)] * 2 + [_whole(col)] * 2 + [_whole(mat)] * 2,
        out_specs=[_whole(col)] * 2 + [_whole((1, n_groups))] + [_whole(mat)] * 2,
        out_shape=[jax.ShapeDtypeStruct(col, F32)] * 2 + [jax.ShapeDtypeStruct((1, n_groups), F32)]
        + [jax.ShapeDtypeStruct(mat, F32)] * 2,
        compiler_params=_params(),
    )(lam_re, lam_im, log_dt, b_re, b_im, d_ar, d_ai, d_bbr, d_bbi)


def _cmul(ar, ai, br, bi):
    return ar * br - ai * bi, ar * bi + ai * br


def _scan_block(xr, xi, lr, li, carry_r, carry_i, or_ref, oi_ref, loc_r, loc_i, reverse):
    tb, cb = xr.shape
    ng = tb // SUBLANES
    xr = xr.reshape(ng, SUBLANES, cb)
    xi = xi.reshape(ng, SUBLANES, cb)
    rid = lax.broadcasted_iota(jnp.int32, (1, SUBLANES, cb), 1)
    pr, pi_ = lr.reshape(1, 1, cb), li.reshape(1, 1, cb)
    powers = []
    for k in (1, 2, 4):
        powers.append((pr, pi_))
        shift = SUBLANES - k if reverse else k
        sr, si = pltpu.roll(xr, shift, 1), pltpu.roll(xi, shift, 1)
        keep = (rid < SUBLANES - k) if reverse else (rid >= k)
        tr_, ti_ = _cmul(jnp.where(keep, pr, 0.0), jnp.where(keep, pi_, 0.0), sr, si)
        xr = xr + tr_
        xi = xi + ti_
        pr, pi_ = _cmul(pr, pi_, pr, pi_)
    loc_r[...] = xr
    loc_i[...] = xi
    (p1r, p1i), (p2r, p2i), (p4r, p4i) = powers
    dist = lax.broadcasted_iota(jnp.int32, (SUBLANES, cb), 0)
    if reverse:
        dist = SUBLANES - 1 - dist
    wr = jnp.broadcast_to(p1r.reshape(1, cb), (SUBLANES, cb))
    wi = jnp.broadcast_to(p1i.reshape(1, cb), (SUBLANES, cb))
    for bit, (qr, qi) in ((1, (p1r, p1i)), (2, (p2r, p2i)), (4, (p4r, p4i))):
        mr, mi = _cmul(wr, wi, qr.reshape(1, cb), qi.reshape(1, cb))
        on = (dist & bit) != 0
        wr, wi = jnp.where(on, mr, wr), jnp.where(on, mi, wi)
    last = 0 if reverse else SUBLANES - 1

    def step(j, carry):
        cr, ci = carry
        g = (ng - 1 - j) if reverse else j
        fr = loc_r[g] + (wr * cr - wi * ci)
        fi = loc_i[g] + (wr * ci + wi * cr)
        rows = pl.ds(pl.multiple_of(g * SUBLANES, SUBLANES), SUBLANES)
        or_ref[rows, :] = fr
        oi_ref[rows, :] = fi
        return fr[last:last + 1, :], fi[last:last + 1, :]

    cr, ci = lax.fori_loop(0, ng, step, (carry_r[...], carry_i[...]))
    carry_r[...] = cr
    carry_i[...] = ci


def _scan_tiles(seq_len, n_ch):
    return min(256, seq_len), min(512, n_ch)


def _run_exchange(plan, step, n_steps, refs):
    @pl.when(step == 0)
    def _():
        plan.start(*refs)

    for part, at in plan.relay_steps(n_steps):
        @pl.when(step == at)
        def _(part=part):
            plan.relay(part, *refs)

    @pl.when(step == n_steps - 1)
    def _():
        plan.finish(*refs)


def _exchange_args(plan):
    if plan is None:
        return [], [], []
    return list(plan.ins), list(plan.out_shape), list(plan.sems)


def _split_refs(refs, *counts):
    groups, at = [], 0
    for n in counts:
        groups.append(refs[at:at + n])
        at += n
    return groups + [refs[at:]]


def ssm_scan(x_re, x_im, lam_re, lam_im, batch, name, exchange=None):
    n, nch = x_re.shape
    seq = n // batch
    tb, cb = _scan_tiles(seq, nch)
    nt, nc = seq // tb, nch // cb
    ex_ins, ex_out, ex_sems = _exchange_args(exchange)

    def body(*refs):
        ins, xin, outs, xout, scratch, xsems = _split_refs(refs, 4, len(ex_ins), 2, len(ex_out), 6)
        xr_ref, xi_ref, lr_ref, li_ref = ins
        or_ref, oi_ref = outs
        car_r, car_i, loc_r, loc_i, s_r, s_i = scratch
        if exchange is not None:
            step = (pl.program_id(0) * batch + pl.program_id(1)) * nt + pl.program_id(2)
            _run_exchange(exchange, step, nc * batch * nt, (xin, xout, xsems))

        @pl.when(pl.program_id(2) == 0)
        def _():
            car_r[...] = jnp.zeros_like(car_r)
            car_i[...] = jnp.zeros_like(car_i)

        _scan_block(xr_ref[...].astype(F32), xi_ref[...].astype(F32), lr_ref[...], li_ref[...], car_r, car_i,
                    s_r, s_i, loc_r, loc_i, reverse=False)
        or_ref[...] = s_r[...].astype(or_ref.dtype)
        oi_ref[...] = s_i[...].astype(oi_ref.dtype)

    blk = pl.BlockSpec((tb, cb), lambda c, b, t: (b * nt + t, c))
    lam_spec = pl.BlockSpec((1, cb), lambda c, b, t: (0, c))
    res = pl.pallas_call(
        body, name=name,
        grid=(nc, batch, nt),
        in_specs=[blk, blk, lam_spec, lam_spec] + [ANY] * len(ex_ins),
        out_specs=[blk, blk] + [ANY] * len(ex_out),
        out_shape=[jax.ShapeDtypeStruct((n, nch), BF16)] * 2 + ex_out,
        scratch_shapes=[pltpu.VMEM((1, cb), F32)] * 2 + [pltpu.VMEM((tb // SUBLANES, SUBLANES, cb), F32)] * 2
        + [pltpu.VMEM((tb, cb), F32)] * 2 + ex_sems,
        compiler_params=_params("arbitrary", "arbitrary", "arbitrary"),
    )(x_re, x_im, lam_re, lam_im, *ex_ins)
    return res[0], res[1], list(res[2:])


def ssm_scan_bwd(d_re, d_im, s_re, s_im, lam_re, lam_im, batch, name, exchange=None):
    n, nch = d_re.shape
    seq = n // batch
    tb, cb = _scan_tiles(seq, nch)
    nt, nc = seq // tb, nch // cb
    halo_rows = 2 * SUBLANES
    hb = tb // halo_rows
    ex_ins, ex_out, ex_sems = _exchange_args(exchange)

    def body(*refs):
        ins, xin, outs, xout, scratch, xsems = _split_refs(refs, 8, len(ex_ins), 4, len(ex_out), 6)
        xr_ref, xi_ref, sr_ref, si_ref, hr_ref, hi_ref, lr_ref, li_ref = ins
        or_ref, oi_ref, dlr_ref, dli_ref = outs
        car_r, car_i, loc_r, loc_i, g_r, g_i = scratch
        b, t = pl.program_id(1), pl.program_id(2)
        if exchange is not None:
            step = (pl.program_id(0) * batch + b) * nt + t
            _run_exchange(exchange, step, nc * batch * nt, (xin, xout, xsems))

        @pl.when(t == 0)
        def _():
            car_r[...] = jnp.zeros_like(car_r)
            car_i[...] = jnp.zeros_like(car_i)

        _scan_block(xr_ref[...].astype(F32), xi_ref[...].astype(F32), lr_ref[...], -li_ref[...], car_r, car_i,
                    g_r, g_i, loc_r, loc_i, reverse=True)
        gr, gi = g_r[...], g_i[...]
        or_ref[...] = gr.astype(or_ref.dtype)
        oi_ref[...] = gi.astype(oi_ref.dtype)
        first_block = t == nt - 1
        row = lax.broadcasted_iota(jnp.int32, (tb, cb), 0)
        hr = jnp.where(first_block, 0.0, hr_ref[...].astype(F32)[halo_rows - 1:halo_rows, :])
        hi = jnp.where(first_block, 0.0, hi_ref[...].astype(F32)[halo_rows - 1:halo_rows, :])
        pr = jnp.where(row == 0, hr, pltpu.roll(sr_ref[...].astype(F32), 1, 0))
        pi_ = jnp.where(row == 0, hi, pltpu.roll(si_ref[...].astype(F32), 1, 0))
        dlr = jnp.sum(gr * pr + gi * pi_, axis=0, keepdims=True)
        dli = jnp.sum(gi * pr - gr * pi_, axis=0, keepdims=True)
        start = jnp.logical_and(b == 0, t == 0)

        @pl.when(start)
        def _():
            dlr_ref[...] = dlr
            dli_ref[...] = dli

        @pl.when(jnp.logical_not(start))
        def _():
            dlr_ref[...] += dlr
            dli_ref[...] += dli

    def blk(c, b, t):
        return b * nt + (nt - 1 - t)

    st_spec = pl.BlockSpec((tb, cb), lambda c, b, t: (blk(c, b, t), c))
    halo_spec = pl.BlockSpec((halo_rows, cb), lambda c, b, t: (jnp.maximum(blk(c, b, t) * hb - 1, 0), c))
    row_spec = pl.BlockSpec((1, cb), lambda c, b, t: (0, c))
    res = pl.pallas_call(
        body, name=name,
        grid=(nc, batch, nt),
        in_specs=[st_spec] * 4 + [halo_spec] * 2 + [row_spec] * 2 + [ANY] * len(ex_ins),
        out_specs=[st_spec, st_spec, row_spec, row_spec] + [ANY] * len(ex_out),
        out_shape=[jax.ShapeDtypeStruct((n, nch), BF16)] * 2 + [jax.ShapeDtypeStruct((1, nch), F32)] * 2 + ex_out,
        scratch_shapes=[pltpu.VMEM((1, cb), F32)] * 2 + [pltpu.VMEM((tb // SUBLANES, SUBLANES, cb), F32)] * 2
        + [pltpu.VMEM((tb, cb), F32)] * 2 + ex_sems,
        compiler_params=_params("arbitrary", "arbitrary", "arbitrary"),
    )(d_re, d_im, s_re, s_im, s_re, s_im, lam_re, lam_im, *ex_ins)
    return res[0], res[1], res[2], res[3], list(res[4:])


def _pool_tiles(seq_len):
    return min(512, seq_len)


def _window_sums(x, n_steps, forward_in_time):
    rows = x.shape[0]
    k = 1
    for _ in range(n_steps):
        x = x + pltpu.roll(x, k if forward_in_time else rows - k, 0)
        k *= 2
    return x


def pool_fwd(u, w_pool, scale, batch, name):
    n, c = u.shape
    seq = n // batch
    tb = _pool_tiles(seq)
    nt = seq // tb
    gc = c // len(POOL_WINDOWS)
    hb = tb // POOL_HALO

    def body(x_ref, halo_ref, w_ref, sc_ref, y_ref, q_ref):
        t = pl.program_id(1)
        halo = jnp.where(t == 0, 0.0, halo_ref[...])
        full = jnp.concatenate([halo, x_ref[...]], axis=0)
        pos = lax.broadcasted_iota(jnp.int32, (tb, gc), 0) + t * tb + 1
        for gi, win in enumerate(POOL_WINDOWS):
            cols = slice(gi * gc, (gi + 1) * gc)
            sums = _window_sums(full[:, cols], gi + 1, True)[POOL_HALO:, :]
            cnt = jnp.minimum(pos, win).astype(F32)
            q = sums / cnt - x_ref[:, cols]
            r = jnp.dot(q.astype(BF16), w_ref[gi].astype(BF16), preferred_element_type=F32)
            q_ref[:, cols] = q.astype(q_ref.dtype)
            y_ref[:, cols] = (r * sc_ref[:, cols]).astype(y_ref.dtype)

    return pl.pallas_call(
        body, name=name,
        grid=(batch, nt),
        in_specs=[pl.BlockSpec((tb, c), lambda b, t: (b * nt + t, 0)),
                  pl.BlockSpec((POOL_HALO, c), lambda b, t: (jnp.maximum((b * nt + t) * hb - 1, 0), 0)),
                  pl.BlockSpec(w_pool.shape, lambda b, t: (0, 0, 0)),
                  pl.BlockSpec((1, c), lambda b, t: (0, 0))],
        out_specs=[pl.BlockSpec((tb, c), lambda b, t: (b * nt + t, 0))] * 2,
        out_shape=[jax.ShapeDtypeStruct((n, c), BF16)] * 2,
        compiler_params=_params("parallel", "arbitrary"),
    )(u, u, w_pool, scale)


def pool_bwd(dy, q, w_pool, scale, batch, name):
    n, c = dy.shape
    seq = n // batch
    tb = _pool_tiles(seq)
    nt = seq // tb
    ng = len(POOL_WINDOWS)
    gc = c // ng
    hb = tb // POOL_HALO
    n_blocks = n // POOL_HALO

    def body(dy_ref, dyh_ref, q_ref, w_ref, sc_ref, du_ref, dw_ref, dsc_ref):
        b, t = pl.program_id(0), pl.program_id(1)
        last = t == nt - 1
        dy_full = jnp.concatenate([dy_ref[...], jnp.where(last, 0.0, dyh_ref[...])], axis=0)
        pos = lax.broadcasted_iota(jnp.int32, (tb + POOL_HALO, gc), 0) + t * tb + 1
        start = jnp.logical_and(b == 0, t == 0)
        for gi, win in enumerate(POOL_WINDOWS):
            cols = slice(gi * gc, (gi + 1) * gc)
            w = w_ref[gi].astype(BF16)
            dr = dy_full[:, cols] * sc_ref[:, cols]
            dq = lax.dot_general(dr.astype(BF16), w, (((1,), (1,)), ((), ())), preferred_element_type=F32)
            cnt = jnp.minimum(pos, win).astype(F32)
            back = _window_sums(dq / cnt, gi + 1, False)
            du_ref[:, cols] = back[:tb, :] - dq[:tb, :]
            qb = q_ref[:, cols]
            r = jnp.dot(qb, w, preferred_element_type=F32)
            dw = lax.dot_general(qb, dr[:tb, :].astype(BF16), (((0,), (0,)), ((), ())), preferred_element_type=F32)
            dsc = jnp.sum(dy_ref[:, cols] * r, axis=0, keepdims=True)

            @pl.when(start)
            def _(gi=gi, cols=cols, dw=dw, dsc=dsc):
                dw_ref[gi] = dw
                dsc_ref[:, cols] = dsc

            @pl.when(jnp.logical_not(start))
            def _(gi=gi, cols=cols, dw=dw, dsc=dsc):
                dw_ref[gi] += dw
                dsc_ref[:, cols] += dsc

    blk = pl.BlockSpec((tb, c), lambda b, t: (b * nt + t, 0))
    halo = pl.BlockSpec((POOL_HALO, c), lambda b, t: (jnp.minimum((b * nt + t + 1) * hb, n_blocks - 1), 0))
    return pl.pallas_call(
        body, name=name,
        grid=(batch, nt),
        in_specs=[blk, halo, blk,
                  pl.BlockSpec(w_pool.shape, lambda b, t: (0, 0, 0)),
                  pl.BlockSpec((1, c), lambda b, t: (0, 0))],
        out_specs=[blk, pl.BlockSpec(w_pool.shape, lambda b, t: (0, 0, 0)), pl.BlockSpec((1, c), lambda b, t: (0, 0))],
        out_shape=[jax.ShapeDtypeStruct((n, c), F32), jax.ShapeDtypeStruct(w_pool.shape, F32),
                   jax.ShapeDtypeStruct((1, c), F32)],
        compiler_params=_params("arbitrary", "arbitrary"),
    )(dy, dy, q, w_pool, scale)


def _place():
    return lax.axis_index("x"), lax.axis_index("y"), lax.axis_index("c")


class GatherPlan:
    def __init__(self, arrs):
        self.ins = list(arrs)
        na = len(arrs)
        self.out_shape = [jax.ShapeDtypeStruct((N_DEV,) + a.shape, a.dtype) for a in arrs]
        self.sems = [pltpu.SemaphoreType.DMA((na, 7)), pltpu.SemaphoreType.DMA((na, 7)), pltpu.SemaphoreType.DMA((na,))]
        self.sizes = [math.prod(a.shape) * a.dtype.itemsize for a in arrs]

    def relay_steps(self, n_steps):
        total, done, steps = sum(self.sizes), 0, []
        for a, size in enumerate(self.sizes):
            done += size
            steps.append((a, min(n_steps - 1, (done * (n_steps - 1)) // total)))
        return steps

    def _copy(self, outs, sems, a, k, block, to, src=None):
        dst = outs[a].at[4 * block[0] + 2 * block[1] + block[2]]
        return pltpu.make_async_remote_copy(
            src_ref=dst if src is None else src, dst_ref=dst,
            send_sem=sems[0].at[a, k], recv_sem=sems[1].at[a, k], device_id=to, device_id_type=MESH)

    @staticmethod
    def _chips(x, y):
        return [(1 - x, y), (x, 1 - y), (1 - x, 1 - y)]

    def _local(self, ins, outs, sems, a, me):
        return pltpu.make_async_copy(ins[a], outs[a].at[4 * me[0] + 2 * me[1] + me[2]], sems[2].at[a])

    def start(self, ins, outs, sems):
        x, y, c = _place()
        me = (x, y, c)
        for a in range(len(ins)):
            self._local(ins, outs, sems, a, me).start()
            self._copy(outs, sems, a, 0, me, (x, y, 1 - c), src=ins[a]).start()
            for j, chip in enumerate(self._chips(x, y)):
                self._copy(outs, sems, a, 1 + j, me, (*chip, c), src=ins[a]).start()

    def relay(self, a, ins, outs, sems):
        x, y, c = _place()
        for j, chip in enumerate(self._chips(x, y)):
            self._copy(outs, sems, a, 1 + j, (*chip, c), (x, y, c)).wait_recv()
            self._copy(outs, sems, a, 4 + j, (*chip, c), (x, y, 1 - c)).start()

    def finish(self, ins, outs, sems):
        x, y, c = _place()
        me, sibling = (x, y, c), (x, y, 1 - c)
        for a in range(len(ins)):
            self._copy(outs, sems, a, 0, sibling, me).wait_recv()
            for j, chip in enumerate(self._chips(x, y)):
                self._copy(outs, sems, a, 4 + j, (*chip, 1 - c), me).wait_recv()
        for a in range(len(ins)):
            self._copy(outs, sems, a, 0, me, sibling, src=ins[a]).wait_send()
            for j, chip in enumerate(self._chips(x, y)):
                self._copy(outs, sems, a, 1 + j, me, (*chip, c), src=ins[a]).wait_send()
                self._copy(outs, sems, a, 4 + j, (*chip, c), sibling).wait_send()
            self._local(ins, outs, sems, a, me).wait()


class ChipScatterPlan:
    def __init__(self, arrs):
        self.groups = [list(a) if isinstance(a, list) else [a] for a in arrs]
        self.ins = [piece for group in self.groups for piece in group]
        self.first = [sum(len(g) for g in self.groups[:a]) for a in range(len(self.groups))]
        na = len(arrs)
        self.out_shape = [jax.ShapeDtypeStruct((4,) + g[0].shape[1:], g[0].dtype) for g in self.groups]
        self.sems = [pltpu.SemaphoreType.DMA((na, 3)), pltpu.SemaphoreType.DMA((na, 3)), pltpu.SemaphoreType.DMA((na,))]

    def relay_steps(self, n_steps):
        return []

    def _row(self, ins, a, px, py):
        if len(self.groups[a]) == 1:
            return ins[self.first[a]].at[2 * px + py]
        return ins[self.first[a] + px].at[py]

    def start(self, ins, outs, sems):
        x, y, c = _place()
        mine = 2 * x + y
        for xs in (0, 1):
            @pl.when(x == xs)
            def _(xs=xs):
                for a in range(len(self.groups)):
                    pltpu.make_async_copy(self._row(ins, a, xs, y), outs[a].at[mine], sems[2].at[a]).start()
                    for j, (px, py) in enumerate([(1 - xs, y), (xs, 1 - y), (1 - xs, 1 - y)]):
                        pltpu.make_async_remote_copy(
                            src_ref=self._row(ins, a, px, py), dst_ref=outs[a].at[mine],
                            send_sem=sems[0].at[a, j], recv_sem=sems[1].at[a, j],
                            device_id=(px, py, c), device_id_type=MESH).start()

    def finish(self, ins, outs, sems):
        x, y, c = _place()
        for wait_recv in (True, False):
            for a in range(len(self.groups)):
                for j in range(3):
                    cp = pltpu.make_async_remote_copy(
                        src_ref=self._row(ins, a, 0, 0), dst_ref=outs[a].at[0],
                        send_sem=sems[0].at[a, j], recv_sem=sems[1].at[a, j],
                        device_id=(x, y, c), device_id_type=MESH)
                    if wait_recv:
                        cp.wait_recv()
                    else:
                        cp.wait_send()
        for a in range(len(self.groups)):
            pltpu.make_async_copy(self._row(ins, a, 0, 0), outs[a].at[0], sems[2].at[a]).wait()


def run_exchange(plan, name):
    n_in, n_out = len(plan.ins), len(plan.out_shape)

    def body(*refs):
        parts = (refs[:n_in], refs[n_in:n_in + n_out], refs[n_in + n_out:])
        plan.start(*parts)
        for part, _ in plan.relay_steps(1):
            plan.relay(part, *parts)
        plan.finish(*parts)

    return pl.pallas_call(
        body, name=name,
        in_specs=[ANY] * n_in, out_specs=[ANY] * n_out,
        out_shape=plan.out_shape, scratch_shapes=plan.sems,
    )(*plan.ins)


class SiblingSwapPlan:
    def __init__(self, arrs):
        self.ins = list(arrs)
        na = len(arrs)
        self.out_shape = [jax.ShapeDtypeStruct(a.shape, a.dtype) for a in arrs]
        self.sems = [pltpu.SemaphoreType.DMA((na,)), pltpu.SemaphoreType.DMA((na,))]

    def relay_steps(self, n_steps):
        return []

    @staticmethod
    def _copies(ins, outs, sems):
        x, y, c = _place()
        return [pltpu.make_async_remote_copy(
            src_ref=ins[a], dst_ref=outs[a], send_sem=sems[0].at[a], recv_sem=sems[1].at[a],
            device_id=(x, y, 1 - c), device_id_type=MESH) for a in range(len(ins))]

    def start(self, ins, outs, sems):
        for cp in self._copies(ins, outs, sems):
            cp.start()

    def finish(self, ins, outs, sems):
        for cp in self._copies(ins, outs, sems):
            cp.wait()


def adamw(w, gparts, m, v, name, tr=256):
    rows, cols = w.shape
    parts = gparts.shape[0]
    tr = _row_tile(rows, tr)
    c1 = 1.0 - ADAM_B1 ** ADAM_STEP
    c2 = 1.0 - ADAM_B2 ** ADAM_STEP

    def body(w_ref, g_ref, m_ref, v_ref, go_ref, d_ref, mo_ref, vo_ref):
        g = g_ref[0].astype(F32)
        for p_ in range(1, parts):
            g = g + g_ref[p_].astype(F32)
        m_new = ADAM_B1 * m_ref[...] + (1.0 - ADAM_B1) * g
        v_new = ADAM_B2 * v_ref[...] + (1.0 - ADAM_B2) * (g * g)
        m_hat = m_new / c1
        v_hat = v_new / c2
        go_ref[...] = g
        d_ref[...] = -ADAM_LR * (m_hat / (jnp.sqrt(v_hat) + ADAM_EPS) + ADAM_WD * w_ref[...])
        mo_ref[...] = m_new
        vo_ref[...] = v_new

    blk = pl.BlockSpec((tr, cols), lambda i: (i, 0))
    return pl.pallas_call(
        body, name=name,
        grid=(rows // tr,),
        in_specs=[blk, pl.BlockSpec((parts, tr, cols), lambda i: (0, i, 0)), blk, blk],
        out_specs=[blk] * 4,
        out_shape=[jax.ShapeDtypeStruct((rows, cols), F32)] * 4,
        compiler_params=_params("parallel"),
    )(w, gparts, m, v)


def add2(a, b, name, out_dtype):
    return rowwise(lambda a, b: ([a.astype(F32) + b.astype(F32)], []), [a, b], [(a.shape[1], out_dtype)],
                   name=name, tr=256)[0]


def _block_diag(x):
    g, a, b = x.shape
    eye = jnp.eye(g, dtype=x.dtype)
    return (x[:, :, None, :] * eye[:, None, :, None]).reshape(g * a, g * b)


def _diag_blocks(x, a, b):
    per = x.shape[1] // b
    x5 = x.reshape(SSM_SUPER, per, a, per, b)
    eye = jnp.eye(per, dtype=x.dtype)
    return jnp.sum(x5 * eye[None, :, None, :, None], axis=3).reshape(SSM_SUPER * per, a, b)


def _swiglu_epi(g, u):
    s = _sigmoid(g)
    silu = g * s
    return u * (s * (1.0 + g * (1.0 - s))), silu, silu * u


def _residual_epi(scale, with_norm):
    if with_norm:
        def epi(acc, res, gain):
            out = res + scale * acc
            return out, _rms_tile(out, gain)
    else:
        def epi(acc, res):
            return (res + scale * acc,)
    return epi


def _next_norm(next_gain):
    if next_gain is None:
        return [], (F32,)
    return [next_gain], (F32, BF16)


FFN_WIDE = 2816


def ffn_fwd(h, n, wi, wo, next_gain, tag, carry=None):
    if carry is None:
        carry = lambda where, run: run(None)[0]
    dact_g, dact_u, act = carry(f"{tag}_in", lambda ex: _with_exchange(matmul(
        n, [(wi, 0), (wi, 1)], mode="nn", name=f"{tag}_in", separate=True, epi=_swiglu_epi,
        out_dtypes=(BF16, BF16, BF16), exchange=ex, tm=512, tn=FFN_WIDE), ex))
    more, dtypes = _next_norm(next_gain)
    res = carry(f"{tag}_out", lambda ex: _with_exchange(matmul(
        act, wo, mode="nn", name=f"{tag}_out", epi=_residual_epi(0.5, bool(more)), extras=[h] + more,
        out_dtypes=dtypes, exchange=ex, tm=512, tk=FFN_WIDE), ex))
    return res[0], (res[1] if more else None), (h, n, dact_g, dact_u, act)


def ffn_bwd(dh, saved, gain, wi, wo, tag, carry=None, gate=None, on_weight_grads=None):
    h, n, dact_g, dact_u, act = saved
    if carry is None:
        carry = lambda where, run: run(None)[0]
    dg, du = carry(f"{tag}_out_dx", lambda ex: _with_exchange(matmul(
        dh, wo, mode="nt", name=f"{tag}_out_dx", extras=[dact_g, dact_u], out_dtypes=(BF16, BF16),
        epi=lambda acc, fg, fu: (0.5 * acc * fg, 0.5 * acc * fu), exchange=ex, tm=512, tn=FFN_WIDE), ex))
    d_wo, = carry(f"{tag}_out_dw", lambda ex: _with_exchange(matmul(
        act, dh, mode="tn", name=f"{tag}_out_dw", scale=0.5, exchange=ex, tm=FFN_WIDE), ex))
    d_wi = [carry(f"{tag}_in_dw_{k}", lambda ex, half=half: _with_exchange(matmul(
        n, half, mode="tn", name=f"{tag}_in_dw", exchange=ex, tn=FFN_WIDE), ex))[0] for k, half in (("g", dg), ("u", du))]
    if on_weight_grads is not None:
        on_weight_grads(d_wi, d_wo)
    if gate is None:
        epi, more, dtypes = _rms_bwd_tile, [], (F32,)
    else:
        def epi(acc, x, g, dres, e, pre):
            dh_new, d_gain = _rms_bwd_tile(acc, x, g, dres)
            return (dh_new, *_gate_cotangents(dh_new, e, pre), d_gain)
        more, dtypes = list(gate), (F32, BF16, BF16)
    res = carry(f"{tag}_in_dx", lambda ex: _with_exchange(matmul(
        [dg, du], [(wi, 0), (wi, 1)], mode="nt", name=f"{tag}_in_dx", epi=epi, extras=[h, gain, dh] + more,
        out_dtypes=dtypes, col_sums=1, exchange=ex, tm=256, tk=FFN_WIDE), ex))
    return res[0], res[-1], d_wi, d_wo, tuple(res[1:-1])


def _with_exchange(result, exchange):
    return result if exchange is not None else (result, [])


def ssm_tables(rep):
    rows = DEPTH * SSM_CH
    lam_r, lam_i, bb_r, bb_i = ssm_prep(
        rep["ssm_lambda_re"].reshape(rows, 1), rep["ssm_lambda_im"].reshape(rows, 1),
        jnp.repeat(rep["ssm_log_dt"].reshape(-1), SSM_STATE).reshape(rows, 1),
        rep["ssm_b_re"].reshape(rows, SSM_GROUP_CH), rep["ssm_b_im"].reshape(rows, SSM_GROUP_CH), "ssm_zoh")
    lam = (lam_r.reshape(DEPTH, 1, SSM_CH), lam_i.reshape(DEPTH, 1, SSM_CH))

    def dense(x):
        return jax.vmap(_block_diag)(x).astype(BF16)

    b_mats = [dense(bb.reshape(DEPTH, SSM_GROUPS, SSM_STATE, SSM_GROUP_CH).transpose(0, 1, 3, 2)) for bb in (bb_r, bb_i)]
    c_mats = [dense(cc.transpose(0, 1, 3, 2)) for cc in (rep["ssm_c_re"], -rep["ssm_c_im"])]
    return lam, b_mats, c_mats


def mix_fwd(h, n, lw, tables, layer, batch, next_gain, tag, exchange=None):
    sw = h.shape[1] // 2
    us, up = matmul(n, [lw["w_in"][:, :sw], lw["w_in"][:, sw:]], mode="nn", name=f"{tag}_in", separate=True,
                    out_dtypes=(F32, F32))
    lam = tuple(t[layer] for t in tables[0])
    b_mats = [(m, layer) for m in tables[1]]
    c_mats = [(m, layer) for m in tables[2]]
    bu_re, bu_im = matmul(us, b_mats, mode="nn", name=f"{tag}_bu", separate=True, diag=SSM_SUPER,
                          out_dtypes=(BF16, BF16))
    s_re, s_im, exchanged = ssm_scan(bu_re, bu_im, *lam, batch, f"{tag}_scan", exchange)
    y0, y1 = matmul([s_re, s_im], c_mats, mode="nn", name=f"{tag}_c", diag=SSM_SUPER,
                    epi=lambda acc, u, d: (acc + d * u, _gelu(acc + d * u)), extras=[us, lw["ssm_d"]],
                    out_dtypes=(F32, BF16))
    y2, gl = matmul(y1, lw["w_glu"], mode="nn", name=f"{tag}_glu",
                    epi=lambda acc, y0: (_gelu(y0) * _sigmoid(acc), acc), extras=[y0], out_dtypes=(BF16, F32))
    yp, q = pool_fwd(up, lw["pool_w"], lw["pool_scale"], batch, f"{tag}_pool")
    more, dtypes = _next_norm(next_gain)
    res = matmul([y2, yp], [lw["w_out"][:sw], lw["w_out"][sw:]], mode="nn", name=f"{tag}_out",
                 epi=_residual_epi(1.0, bool(more)), extras=[h] + more, out_dtypes=dtypes)
    saved = (h, n, us, lam, b_mats, c_mats, s_re, s_im, y0, y1, gl, y2, yp, q)
    return res[0], (res[1] if more else None), saved, exchanged


def mix_bwd(dh, saved, lw, batch, tag, exchange=None):
    h, n, us, lam, b_mats, c_mats, s_re, s_im, y0, y1, gl, y2, yp, q = saved
    sw = h.shape[1] // 2
    w_out_s, w_out_p = lw["w_out"][:sw], lw["w_out"][sw:]
    d_wo_s, d_wo_p = matmul([y2, yp], dh, mode="tn", name=f"{tag}_out_dw", separate=True)
    def out_dx_epi(dy2, dyp, y0, gl):
        sg = _sigmoid(gl)
        return dy2, dyp, dy2 * _gelu(y0) * sg * (1.0 - sg)

    dy2, dyp, tg = matmul(dh, [w_out_s, w_out_p], mode="nt", name=f"{tag}_out_dx", separate=True, epi=out_dx_epi,
                          extras=[y0, gl], out_dtypes=(F32, F32, BF16))
    dup, d_pool_w, d_pool_scale = pool_bwd(dyp, q, lw["pool_w"], lw["pool_scale"], batch, f"{tag}_pool_bwd")
    def dy0_epi(acc, dy2, gl, y0, u):
        dy0 = (acc + dy2 * _sigmoid(gl)) * _gelu_grad(y0)
        return dy0, jnp.sum(dy0 * u, axis=0, keepdims=True)

    dy0, d_d = matmul(tg, lw["w_glu"], mode="nt", name=f"{tag}_glu_dx", epi=dy0_epi, extras=[dy2, gl, y0, us],
                      out_dtypes=(F32,), col_sums=1)
    d_w_glu, = matmul(y1, tg, mode="tn", name=f"{tag}_glu_dw")
    gd_re, gd_im = matmul(dy0, c_mats, mode="nt", name=f"{tag}_c_dx", separate=True, diag=SSM_SUPER,
                          out_dtypes=(BF16, BF16))
    d_c_top, d_c_bot = matmul([s_re, s_im], dy0, mode="tn", name=f"{tag}_c_dw", separate=True, diag=SSM_SUPER)
    g_re, g_im, d_lam_r, d_lam_i, exchanged = ssm_scan_bwd(gd_re, gd_im, s_re, s_im, *lam, batch, f"{tag}_scan_bwd",
                                                           exchange)
    dus, = matmul([g_re, g_im], b_mats, mode="nt", name=f"{tag}_bu_dx", diag=SSM_SUPER,
                  epi=lambda acc, dy0, d: (acc + d * dy0,), extras=[dy0, lw["ssm_d"]])
    d_b_re, d_b_im = matmul(us, [g_re, g_im], mode="tn", name=f"{tag}_bu_dw", separate=True, diag=SSM_SUPER)
    d_bb_r = _diag_blocks(d_b_re, SSM_GROUP_CH, SSM_STATE).transpose(0, 2, 1).reshape(SSM_CH, SSM_GROUP_CH)
    d_bb_i = _diag_blocks(d_b_im, SSM_GROUP_CH, SSM_STATE).transpose(0, 2, 1).reshape(SSM_CH, SSM_GROUP_CH)
    d_lr, d_li, d_ldt, d_br, d_bi = ssm_prep_bwd(
        lw["lam_re"], lw["lam_im"], lw["log_dt"], lw["b_re"], lw["b_im"],
        d_lam_r.reshape(SSM_CH, 1), d_lam_i.reshape(SSM_CH, 1), d_bb_r, d_bb_i, f"{tag}_zoh_bwd")
    d_c_re = _diag_blocks(d_c_top, SSM_STATE, SSM_GROUP_CH).transpose(0, 2, 1)
    d_c_im = -_diag_blocks(d_c_bot, SSM_STATE, SSM_GROUP_CH).transpose(0, 2, 1)
    d_w_in_s, d_w_in_p = matmul(n, [dus, dup], mode="tn", name=f"{tag}_in_dw", separate=True)
    dh_new, d_gain = matmul([dus, dup], [lw["w_in"][:, :sw], lw["w_in"][:, sw:]], mode="nt", name=f"{tag}_in_dx",
                            epi=_rms_bwd_tile, extras=[h, lw["mix_norm"], dh], out_dtypes=(F32,), col_sums=1, tm=512)
    grads = dict(mix_norm=d_gain, w_in=jnp.concatenate([d_w_in_s, d_w_in_p], axis=1),
                 ssm_lambda_re=d_lr, ssm_lambda_im=d_li, ssm_log_dt=d_ldt, ssm_b_re=d_br, ssm_b_im=d_bi,
                 ssm_c_re=d_c_re, ssm_c_im=d_c_im, ssm_d=d_d, ssm_w_glu=d_w_glu, pool_w=d_pool_w,
                 pool_scale=d_pool_scale, w_out=jnp.concatenate([d_wo_s, d_wo_p], axis=0))
    return dh_new, grads, exchanged


def ple_fwd(h, n, p, w_gate, w_proj, next_gain, tag):
    e, = matmul(p, w_proj, mode="nn", name=f"{tag}_proj")
    if next_gain is None:
        def epi(acc, e, res):
            return res + _sigmoid(acc) * e, acc
        more, dtypes = [], (F32, F32)
    else:
        def epi(acc, e, res, gain):
            out = res + _sigmoid(acc) * e
            return out, acc, _rms_tile(out, gain)
        more, dtypes = [next_gain], (F32, F32, BF16)
    res = matmul(n, w_gate, mode="nn", name=f"{tag}_gate", epi=epi, extras=[e, h] + more, out_dtypes=dtypes, tm=512)
    return res[0], (res[2] if more else None), (h, n, e, res[1])


def _gate_cotangents(dh, e, pre):
    s = _sigmoid(pre)
    return dh * e * s * (1.0 - s), dh * s


def ple_bwd(dh, dpre, de, saved, p, gain, w_gate, tag):
    h, n, e, pre = saved
    d_w_gate, = matmul(n, dpre, mode="tn", name=f"{tag}_gate_dw")
    d_w_proj, = matmul(p, de, mode="tn", name=f"{tag}_proj_dw")
    dh_new, d_gain = matmul(dpre, w_gate, mode="nt", name=f"{tag}_gate_dx", epi=_rms_bwd_tile,
                            extras=[h, gain, dh], out_dtypes=(F32,), col_sums=1, tm=512)
    return dh_new, d_gain, d_w_gate, d_w_proj


def loss_head(h, gain, target, e, pre, name):
    d = h.shape[1]

    def fn(h, g, t, e, pre):
        r = lax.rsqrt(jnp.mean(h * h, axis=-1, keepdims=True) + EPS)
        diff = h * r * g - t
        sq = jnp.sum(jnp.sum(diff * diff, axis=1, keepdims=True), axis=0, keepdims=True)
        dy = diff * (1.0 / d)
        w = dy * g
        dh = r * w - h * (r * r * r) * jnp.mean(h * w, axis=-1, keepdims=True)
        return [dh, *_gate_cotangents(dh, e, pre)], [sq, jnp.sum(dy * (h * r), axis=0, keepdims=True)]

    dh, dpre, de, sq, d_gain = rowwise(fn, [h, gain, target, e, pre], [(d, F32), (d, BF16), (d, BF16)],
                                       [(1, 1), (1, d)], name=name, tr=256)
    return 0.5 / d * sq[0, 0], dh, dpre, de, d_gain


SHARDED = {
    "ffn1_wi": 1, "ffn1_wo": 0, "w_in": 0, "ssm_w_glu": 0, "w_out": 0, "ffn2_wi": 1, "ffn2_wo": 0,
    "ple_w_gate": 0, "ple_w_proj": 1,
}
WEIGHTS = ["ffn1_norm", "ffn1_wi", "ffn1_wo", "mix_norm", "w_in", "ssm_lambda_re", "ssm_lambda_im", "ssm_log_dt",
           "ssm_b_re", "ssm_b_im", "ssm_c_re", "ssm_c_im", "ssm_d", "ssm_w_glu", "pool_w", "pool_scale", "w_out",
           "ffn2_norm", "ffn2_wi", "ffn2_wo", "ple_norm", "ple_w_gate", "ple_w_proj", "final_norm"]
REPLICATED = [n for n in WEIGHTS if n not in SHARDED]


HALVED = ("ffn1_wi", "ffn2_wi")


def _unshard(gathered, axis, halved):
    if halved:
        _, rows, cols = gathered.shape
        return gathered.reshape(2, 4, rows, cols).transpose(0, 2, 1, 3).reshape(2, rows, 4 * cols)
    g = jnp.moveaxis(gathered, 0, axis)
    shp = g.shape
    return g.reshape(shp[:axis] + (shp[axis] * shp[axis + 1],) + shp[axis + 2:])


def _split_for_scatter(full, axis, c, halved):
    if halved:
        rows, cols = full[0].shape

        def pick(cc):
            return [lax.dynamic_index_in_dim(h.reshape(rows, 2, 2, cols // 4), cc, 2, keepdims=False).transpose(1, 0, 2)
                    for h in full]

        return pick(c), [s.astype(BF16) for s in pick(1 - c)]
    shp = full.shape
    g = full.reshape(shp[:axis] + (4, 2, shp[axis] // N_DEV) + shp[axis + 1:])
    keep = lax.dynamic_index_in_dim(g, c, axis + 1, keepdims=False)
    send = lax.dynamic_index_in_dim(g, 1 - c, axis + 1, keepdims=False)
    return jnp.moveaxis(keep, axis, 0), jnp.moveaxis(send, axis, 0).astype(BF16)


def _pack(arrs):
    pieces = []
    for a in arrs:
        flat = a.reshape(-1)
        pad = (-flat.shape[0]) % PACK
        pieces.append(jnp.pad(flat, (0, pad)).reshape(-1, LANES))
    return jnp.concatenate(pieces, axis=0)


def _unpack(packed, shapes):
    out, row = [], 0
    for s in shapes:
        size = math.prod(s)
        rows = (size + PACK - 1) // PACK * SUBLANES
        out.append(packed[row:row + rows].reshape(-1)[:size].reshape(s))
        row += rows
    return out


def local_step(x, p, target, rep, hooks):
    batch, seq, d = x.shape
    n_tok = batch * seq
    h = x.reshape(n_tok, d)
    saved = []
    n = rms_fwd(h, rep["ffn1_norm"][0].reshape(1, d), "first_norm")
    tables = ssm_tables(rep)
    for i in range(DEPTH):
        lw = _layer_weights(rep, i, d)
        big = lambda name, i=i: hooks.big(i, name)
        next_gain = rep["ffn1_norm"][i + 1].reshape(1, d) if i + 1 < DEPTH else None
        def carry(where, run, i=i):
            outs, exchanged = run(hooks.fwd_exchange(i, where))
            hooks.fwd_done(i, where, exchanged)
            return outs

        h, n, s1 = ffn_fwd(h, n, big("ffn1_wi"), big("ffn1_wo"), lw["mix_norm"], "ffn1", carry)
        lw.update(w_in=big("w_in"), w_glu=big("ssm_w_glu"), w_out=big("w_out"))
        h, n, s2, exchanged = mix_fwd(h, n, lw, tables, i, batch, lw["ffn2_norm"], "mix", hooks.fwd_exchange(i, "scan"))
        hooks.fwd_done(i, "scan", exchanged)
        h, n, s3 = ffn_fwd(h, n, big("ffn2_wi"), big("ffn2_wo"), lw["ple_norm"], "ffn2", carry)
        p_i = p[i].reshape(n_tok, -1)
        h, n, s4 = ple_fwd(h, n, p_i, big("ple_w_gate"), big("ple_w_proj"), next_gain, "ple")
        saved.append((s1, s2, s3, s4, p_i))
    last_gate = saved[-1][3][2:]
    loss, dh, dpre, de, d_final = loss_head(h, rep["final_norm"].reshape(1, d), target.reshape(n_tok, d), *last_gate,
                                            "loss_head")
    hooks.final_grad(d_final)
    per_layer = [None] * DEPTH
    for i in reversed(range(DEPTH)):
        lw = _layer_weights(rep, i, d)
        big = lambda name, i=i: hooks.big(i, name)
        lw.update(w_in=big("w_in"), w_glu=big("ssm_w_glu"), w_out=big("w_out"))
        s1, s2, s3, s4, p_i = saved[i]
        g = {}
        dh, g["ple_norm"], g["ple_w_gate"], g["ple_w_proj"] = ple_bwd(dh, dpre, de, s4, p_i, lw["ple_norm"],
                                                                     big("ple_w_gate"), "ple")
        def carry(where, run, i=i):
            outs, exchanged = run(hooks.bwd_exchange(i, where))
            hooks.bwd_done(i, where, exchanged)
            return outs

        dh, g["ffn2_norm"], g["ffn2_wi"], g["ffn2_wo"], _ = ffn_bwd(
            dh, s3, lw["ffn2_norm"], big("ffn2_wi"), big("ffn2_wo"), "ffn2", carry)
        dh, gm, exchanged = mix_bwd(dh, s2, lw, batch, "mix", hooks.bwd_exchange(i, "scan"))
        hooks.bwd_done(i, "scan", exchanged)
        g.update(gm)
        hooks.layer_grads(i, g, [k for k in SHARDED if k not in LAST_GRADS])
        hooks.small_grads(i, {k: g[k] for k in REPLICATED if k in g})
        below = saved[i - 1][3][2:] if i > 0 else None

        def ffn1_grads(d_wi, d_wo, i=i, g=g):
            g["ffn1_wi"], g["ffn1_wo"] = d_wi, d_wo
            hooks.layer_grads(i, g, list(LAST_GRADS))

        dh, g["ffn1_norm"], _, _, gate_ct = ffn_bwd(
            dh, s1, lw["ffn1_norm"], big("ffn1_wi"), big("ffn1_wo"), "ffn1", carry, below, ffn1_grads)
        hooks.small_grads(i, {"ffn1_norm": g["ffn1_norm"]})
        if i > 0:
            dpre, de = gate_ct
        per_layer[i] = g
    return loss, dh.reshape(batch, seq, d), per_layer, d_final


LAST_GRADS = ("ffn1_wi", "ffn1_wo")


def _layer_weights(w, i, d):
    sw = d // 2
    lw = {}
    lw["pool_w"] = w["pool_w"][i]
    for k in ("ffn1_norm", "mix_norm", "ffn2_norm", "ple_norm"):
        lw[k] = w[k][i].reshape(1, d)
    lw["ssm_d"] = w["ssm_d"][i].reshape(1, sw)
    lw["pool_scale"] = w["pool_scale"][i].reshape(1, sw)
    lw["lam_re"] = w["ssm_lambda_re"][i].reshape(SSM_CH, 1)
    lw["lam_im"] = w["ssm_lambda_im"][i].reshape(SSM_CH, 1)
    lw["log_dt"] = jnp.repeat(w["ssm_log_dt"][i], SSM_STATE).reshape(SSM_CH, 1)
    lw["b_re"] = w["ssm_b_re"][i].reshape(SSM_CH, SSM_GROUP_CH)
    lw["b_im"] = w["ssm_b_im"][i].reshape(SSM_CH, SSM_GROUP_CH)
    lw["c_re"] = w["ssm_c_re"][i]
    lw["c_im"] = w["ssm_c_im"][i]
    return lw


class MeshExchange:
    FIRST = ("ffn1_wi", "ffn1_wo")
    FIRST_LAYER_PLAN = {"ffn1_in": ("w_in", "ssm_w_glu", "w_out", "ffn2_wi", "ffn2_wo"),
                        "ffn1_out": ("ple_w_gate", "ple_w_proj")}
    FWD_PLAN = {"scan": ("ffn1_wi", "ffn1_wo", "w_in", "ssm_w_glu", "w_out"),
                "ffn2_in": ("ffn2_wi", "ffn2_wo"), "ffn2_out": ("ple_w_gate", "ple_w_proj")}
    BWD_PLAN = {"ffn2_in_dx": ("ffn1_wo", "ffn2_wo", "ssm_w_glu", "w_out", "ple_w_gate", "ple_w_proj"),
                "scan": ("ffn1_wi", "ffn2_wi", "w_in")}
    LAST_LAYER_PLAN = {"ffn1_out_dx": ("ffn2_wi", "w_in"),
                       "ffn1_out_dw": ("ffn2_wo", "w_out", "ple_w_gate", "ssm_w_glu", "ple_w_proj"),
                       "ffn1_in_dx": LAST_GRADS}

    def __init__(self, shards):
        self.small = {}
        self.d_final = None
        self.small_parts = []
        self.small_rest = None
        self.swaps = {}
        self.shards = shards
        self.c = lax.axis_index("c")
        self.gathered = {}
        self.chip_sums = {}
        self.from_chips = {}
        self.pending = None
        first = run_exchange(GatherPlan([shards[0][k] for k in self.FIRST]), "gather_first_weights")
        self._store(self.gathered, 0, self.FIRST, first)

    @staticmethod
    def _store(where, layer, names, results):
        for k, r in zip(names, results):
            where[(layer, k)] = r

    def big(self, i, name):
        return _unshard(self.gathered[(i, name)], SHARDED[name], name in HALVED)

    def fwd_exchange(self, i, where):
        if i == 0 and where in self.FIRST_LAYER_PLAN:
            layer, names = 0, self.FIRST_LAYER_PLAN[where]
        elif where in self.FWD_PLAN and i + 1 < DEPTH:
            layer, names = i + 1, self.FWD_PLAN[where]
        else:
            return None
        self.pending = (layer, names)
        return GatherPlan([self.shards[layer][k] for k in names])

    def fwd_done(self, i, where, results):
        if results:
            self._store(self.gathered, *self.pending, results)

    def layer_grads(self, i, grads, names):
        pieces, sends = [], []
        for k in names:
            keep, send = _split_for_scatter(grads[k], SHARDED[k], self.c, k in HALVED)
            keep, send = (keep, send) if isinstance(keep, list) else ([keep], [send])
            pieces.append(keep)
            sends += send
        if i == 0:
            self._add_sibling(0, names, pieces, run_exchange(SiblingSwapPlan(sends), "reduce_core_pair"))
        else:
            self.swaps[(i, tuple(names) == LAST_GRADS)] = (names, pieces, sends)

    def _add_sibling(self, i, names, pieces, from_sibling):
        got = iter(from_sibling)
        for k, keep in zip(names, pieces):
            sums = []
            for part in keep:
                cols = part.shape[-1]
                sums.append(add2(part.reshape(-1, cols), next(got).reshape(-1, cols), f"sum_core_pair_{k}",
                                 BF16).reshape(part.shape))
            self.chip_sums[(i, k)] = sums if len(sums) > 1 else sums[0]

    def small_grads(self, i, grads):
        self.small.setdefault(i, {}).update(grads)

    def final_grad(self, d_final):
        self.d_final = d_final

    def _small_pack(self, d_final, first_norm_grad=None):
        pieces = []
        for k in REPLICATED:
            if k == "final_norm":
                pieces.append(d_final)
            elif k == "ffn1_norm" and first_norm_grad is None:
                pieces.append(jnp.stack([self.small[i][k] for i in range(1, DEPTH)], axis=0))
            else:
                pieces.append(jnp.stack([self.small[i][k] for i in range(DEPTH)], axis=0))
        return _pack(pieces)

    def bwd_exchange(self, i, where):
        if where in self.BWD_PLAN and i + 1 < DEPTH:
            layer, names = i + 1, self.BWD_PLAN[where]
        elif i == 0 and where in self.LAST_LAYER_PLAN:
            layer, names = 0, self.LAST_LAYER_PLAN[where]
        elif i == 0 and where in ("ffn1_in_dw_g", "ffn1_in_dw_u"):
            if where.endswith("g"):
                pack = self._small_pack(self.d_final)
                half = pack.shape[0] // 2 // SUBLANES * SUBLANES
                part, self.small_rest = pack[:half], pack[half:]
            else:
                part = self.small_rest
            self.pending = "small"
            return GatherPlan([part])
        elif i > 0 and where in ("ffn1_out_dx", "ffn1_in_dx"):
            self.pending = ("swap", i, where == "ffn1_in_dx")
            return SiblingSwapPlan(self.swaps[self.pending[1:]][2])
        else:
            return None
        self.pending = (layer, names)
        return ChipScatterPlan([self.chip_sums[(layer, k)] for k in names])

    def bwd_done(self, i, where, results):
        if not results:
            return
        if self.pending == "small":
            self.small_parts += results
        elif self.pending[0] == "swap":
            names, pieces, _ = self.swaps[self.pending[1:]]
            self._add_sibling(self.pending[1], names, pieces, results)
        else:
            self._store(self.from_chips, *self.pending, results)

    def small_gathered(self):
        first = self.small[0]["ffn1_norm"].reshape(SUBLANES, LANES)
        first_all, = run_exchange(GatherPlan([first]), "gather_first_gain_grad")
        return jnp.concatenate([first_all] + self.small_parts, axis=1)


def kernel(x, p, ffn1_norm, ffn1_wi, ffn1_wo, mix_norm, w_in, ssm_lambda_re, ssm_lambda_im, ssm_log_dt, ssm_b_re, ssm_b_im, ssm_c_re, ssm_c_im, ssm_d, ssm_w_glu, pool_w, pool_scale, w_out, ffn2_norm, ffn2_wi, ffn2_wo, ple_norm, ple_w_gate, ple_w_proj, final_norm, loss_target, m_ffn1_norm, m_ffn1_wi, m_ffn1_wo, m_mix_norm, m_w_in, m_ssm_lambda_re, m_ssm_lambda_im, m_ssm_log_dt, m_ssm_b_re, m_ssm_b_im, m_ssm_c_re, m_ssm_c_im, m_ssm_d, m_ssm_w_glu, m_pool_w, m_pool_scale, m_w_out, m_ffn2_norm, m_ffn2_wi, m_ffn2_wo, m_ple_norm, m_ple_w_gate, m_ple_w_proj, m_final_norm, v_ffn1_norm, v_ffn1_wi, v_ffn1_wo, v_mix_norm, v_w_in, v_ssm_lambda_re, v_ssm_lambda_im, v_ssm_log_dt, v_ssm_b_re, v_ssm_b_im, v_ssm_c_re, v_ssm_c_im, v_ssm_d, v_ssm_w_glu, v_pool_w, v_pool_scale, v_w_out, v_ffn2_norm, v_ffn2_wi, v_ffn2_wo, v_ple_norm, v_ple_w_gate, v_ple_w_proj, v_final_norm):
    args = dict(locals())
    wts = {k: args[k] for k in WEIGHTS}
    rep = {k: wts[k] for k in REPLICATED}

    shards = [{k: wts[k][i].astype(BF16) for k in SHARDED} for i in range(DEPTH)]
    exchange = MeshExchange(shards)
    loss_local, grad_x, per_layer, d_final = local_step(x, p, loss_target, rep, exchange)
    loss = lax.psum(loss_local, ("x", "y", "c"))

    outs = {}
    for k in SHARDED:
        shp = wts[k].shape
        cols = shp[-1]
        parts = jnp.stack([exchange.from_chips[(i, k)] for i in range(DEPTH)], axis=1)
        res = adamw(wts[k].reshape(-1, cols), parts.reshape(4, -1, cols), args["m_" + k].reshape(-1, cols),
                    args["v_" + k].reshape(-1, cols), f"adamw_{k}")
        outs[k] = [r.reshape(shp) for r in res]

    rep_shapes = [wts[k].shape for k in REPLICATED]
    all_g = exchange.small_gathered()
    res = adamw(_pack([wts[k] for k in REPLICATED]), all_g, _pack([args["m_" + k] for k in REPLICATED]),
                _pack([args["v_" + k] for k in REPLICATED]), "adamw_small")
    unpacked = [_unpack(r, rep_shapes) for r in res]
    for j, k in enumerate(REPLICATED):
        outs[k] = [unpacked[q][j] for q in range(4)]

    result = [loss, grad_x]
    for q in range(4):
        result += [outs[k][q] for k in WEIGHTS]
    return tuple(result)
```

```python
import math

import jax
import jax.numpy as jnp
from jax import lax
from jax.experimental import pallas as pl
from jax.experimental.pallas import tpu as pltpu

F32 = jnp.float32
BF16 = jnp.bfloat16
MESH = pl.DeviceIdType.MESH
ANY = pl.BlockSpec(memory_space=pl.ANY)

N_DEV = 8
DEPTH = 4
EPS = 1e-6
SSM_GROUPS = 32
SSM_GROUP_CH = 16
SSM_STATE = 64
SSM_CH = SSM_GROUPS * SSM_STATE
SSM_SUPER = 2
POOL_WINDOWS = (2, 4, 8, 16)
POOL_HALO = 16
ADAM_LR, ADAM_B1, ADAM_B2, ADAM_EPS, ADAM_WD, ADAM_STEP = 0.001, 0.9, 0.999, 1e-08, 0.01, 10

V7X_VMEM_BYTES = 64 * 1024 * 1024
VMEM_LIMIT_BYTES = V7X_VMEM_BYTES - 12 * 1024 * 1024
LANES = 128
SUBLANES = 8
PACK = SUBLANES * LANES


def _params(*sem):
    return pltpu.CompilerParams(dimension_semantics=sem or None, vmem_limit_bytes=VMEM_LIMIT_BYTES)


def _tile(n, pref):
    if n <= pref:
        return n
    t = pref - pref % LANES
    while t >= LANES:
        if n % t == 0:
            return t
        t -= LANES
    raise ValueError(f"no lane-aligned tile for {n}")


def _row_tile(rows, pref):
    if rows <= pref:
        return rows
    t = pref - pref % SUBLANES
    while t >= SUBLANES:
        if rows % t == 0:
            return t
        t -= SUBLANES
    raise ValueError(f"no sublane-aligned tile for {rows}")


_DIMS = {"nn": ((1,), (0,)), "nt": ((1,), (1,)), "tn": ((0,), (0,))}


def matmul(a, b, *, mode, name, out_dtypes=None, epi=None, extras=(), separate=False, diag=1, col_sums=0,
           stack_out=False, scale=1.0, exchange=None, tm=1024, tn=1024, tk=1024):
    a_list = list(a) if isinstance(a, list) else [a]
    b_list = list(b) if isinstance(b, list) else [b]
    a_planes = [x[1] if isinstance(x, tuple) else None for x in a_list]
    b_planes = [x[1] if isinstance(x, tuple) else None for x in b_list]
    a_list = [x[0] if isinstance(x, tuple) else x for x in a_list]
    b_list = [x[0] if isinstance(x, tuple) else x for x in b_list]
    a_shape, b_shape = a_list[0].shape[-2:], b_list[0].shape[-2:]
    n_terms = max(len(a_list), len(b_list))
    a_idx = [0] * n_terms if len(a_list) == 1 else list(range(n_terms))
    b_idx = [0] * n_terms if len(b_list) == 1 else list(range(n_terms))
    n_acc = n_terms if separate else 1
    assert not (stack_out or scale != 1.0) or epi is None
    if out_dtypes is None:
        out_dtypes = (F32,) * (1 if (epi is not None or stack_out) else n_acc)
    in_place = epi is None
    if mode == "tn":
        K, M = a_shape
        K2, N = b_shape
    elif mode == "nt":
        M, K = a_shape
        N, K2 = b_shape
    else:
        M, K = a_shape
        K2, N = b_shape
    assert K == K2, (name, a_shape, b_shape)
    if mode == "tn":
        tm, tn, tk = _tile(M // diag, tm), _tile(N // diag, tn), _tile(K, tk)
        nk = K // tk
        N = N // diag
        row_tiles, col_tiles = (M // diag) // tm, N // tn
        a_blk, a_map = (tk, tm), lambda i, j, k: (k, i)
        b_blk, b_map = (tk, tn), lambda i, j, k: (k, (i // row_tiles) * col_tiles + j)
    else:
        tm, tn, tk = _tile(M, tm), _tile(N // diag, tn), _tile(K // diag, tk)
        nk = (K // diag) // tk
        col_tiles = (N // diag) // tn
        a_blk, a_map = (tm, tk), lambda i, j, k: (i, (j // col_tiles) * nk + k)
        if mode == "nt":
            b_blk, b_map = (tn, tk), lambda i, j, k: (j, (j // col_tiles) * nk + k)
        else:
            b_blk, b_map = (tk, tn), lambda i, j, k: ((j // col_tiles) * nk + k, j)

    def plane_spec(blk, index_map, plane):
        if plane is None:
            return pl.BlockSpec(blk, index_map)
        return pl.BlockSpec((None,) + blk, lambda i, j, k: (plane,) + index_map(i, j, k))

    a_specs = [plane_spec(a_blk, a_map, p_) for p_ in a_planes]
    b_specs = [plane_spec(b_blk, b_map, p_) for p_ in b_planes]
    assert not col_sums or N == tn, (name, N, tn)
    ex_specs = []
    for e in extras:
        if e.shape == (M, N):
            ex_specs.append(pl.BlockSpec((tm, tn), lambda i, j, k: (i, j)))
        elif e.shape == (1, N):
            ex_specs.append(pl.BlockSpec((1, tn), lambda i, j, k: (0, j)))
        elif e.shape == (M, 1):
            ex_specs.append(pl.BlockSpec((tm, 1), lambda i, j, k: (i, 0)))
        else:
            raise ValueError((name, e.shape, (M, N)))
    na, nb, ne, no = len(a_list), len(b_list), len(extras), len(out_dtypes)
    dims = (_DIMS[mode], ((), ()))

    n_scratch = n_acc if (nk > 1 and not in_place) else 0
    x_ins, x_out, x_sems = _exchange_args(exchange)
    grid = (M // tm, N // tn, nk)

    def body(*refs):
        a_refs, b_refs, ex_refs, xin, out_refs, sum_refs, xout, acc_refs, xsems = _split_refs(
            refs, na, nb, ne, len(x_ins), no, col_sums, len(x_out), n_scratch)
        if exchange is not None:
            step = (pl.program_id(0) * grid[1] + pl.program_id(1)) * nk + pl.program_id(2)
            _run_exchange(exchange, step, grid[0] * grid[1] * nk, (xin, xout, xsems))
        a_vals = [r[...].astype(BF16) for r in a_refs]
        b_vals = [r[...].astype(BF16) for r in b_refs]
        prods = [lax.dot_general(a_vals[a_idx[t]], b_vals[b_idx[t]], dims, preferred_element_type=F32)
                 for t in range(n_terms)]
        if not separate:
            total = prods[0]
            for p_ in prods[1:]:
                total = total + p_
            prods = [total]

        def finish(accs):
            res = epi(*accs, *[e[...] for e in ex_refs]) if epi is not None else tuple(accs)
            for r, v in zip(out_refs, res[:no]):
                r[...] = v.astype(r.dtype)
            first_rows = pl.program_id(0) == 0
            for r, v in zip(sum_refs, res[no:]):
                @pl.when(first_rows)
                def _(r=r, v=v):
                    r[...] = v

                @pl.when(jnp.logical_not(first_rows))
                def _(r=r, v=v):
                    r[...] += v

        if in_place:
            dst = [(out_refs[0], t) for t in range(n_acc)] if stack_out else [(r, None) for r in out_refs]

            def read(r, t):
                return r[...] if t is None else r[t]

            assert nk == 1 or all(dt == F32 for dt in out_dtypes), name

            def write(r, t, v):
                if t is None:
                    r[...] = v.astype(r.dtype)
                else:
                    r[t] = v.astype(r.dtype)

            if nk == 1:
                for (r, t), v in zip(dst, prods):
                    write(r, t, v * scale if scale != 1.0 else v)
            else:
                k = pl.program_id(2)

                @pl.when(k == 0)
                def _():
                    for (r, t), v in zip(dst, prods):
                        write(r, t, v)

                @pl.when(jnp.logical_and(k > 0, k < nk - 1))
                def _():
                    for (r, t), v in zip(dst, prods):
                        write(r, t, read(r, t) + v)

                @pl.when(k == nk - 1)
                def _():
                    for (r, t), v in zip(dst, prods):
                        total = read(r, t) + v
                        write(r, t, total * scale if scale != 1.0 else total)
        elif nk == 1:
            finish(prods)
        else:
            k = pl.program_id(2)

            @pl.when(k == 0)
            def _():
                for r, v in zip(acc_refs, prods):
                    r[...] = v

            @pl.when(jnp.logical_and(k > 0, k < nk - 1))
            def _():
                for r, v in zip(acc_refs, prods):
                    r[...] += v

            @pl.when(k == nk - 1)
            def _():
                finish([r[...] + v for r, v in zip(acc_refs, prods)])

    if stack_out:
        out_specs = [pl.BlockSpec((n_acc, tm, tn), lambda i, j, k: (0, i, j))]
        out_shape = [jax.ShapeDtypeStruct((n_acc, M, N), F32)]
    else:
        out_specs = [pl.BlockSpec((tm, tn), lambda i, j, k: (i, j))] * no
        out_shape = [jax.ShapeDtypeStruct((M, N), dt) for dt in out_dtypes]
    sequential = col_sums or exchange is not None
    outs = pl.pallas_call(
        body,
        name=name,
        grid=grid,
        in_specs=a_specs + b_specs + ex_specs + [ANY] * len(x_ins),
        out_specs=out_specs + [pl.BlockSpec((1, tn), lambda i, j, k: (0, j))] * col_sums + [ANY] * len(x_out),
        out_shape=out_shape + [jax.ShapeDtypeStruct((1, N), F32)] * col_sums + x_out,
        scratch_shapes=[pltpu.VMEM((tm, tn), F32)] * n_scratch + x_sems,
        compiler_params=_params(*(("arbitrary",) * 3 if sequential else ("parallel", "parallel", "arbitrary"))),
    )(*a_list, *b_list, *extras, *x_ins)
    if exchange is not None:
        n_own = len(outs) - len(x_out)
        return list(outs[:n_own]), list(outs[n_own:])
    return outs


def rowwise(fn, ins, outs, accs=(), *, name, tr=512):
    R = max(x.shape[0] for x in ins)
    tr = _row_tile(R, tr)
    in_specs = []
    for x in ins:
        if x.shape[0] == R and x.ndim == 2:
            in_specs.append(pl.BlockSpec((tr, x.shape[1]), lambda i: (i, 0)))
        else:
            in_specs.append(pl.BlockSpec(x.shape, lambda i, _n=x.ndim: (0,) * _n))
    ni, no = len(ins), len(outs)

    def body(*refs):
        i = pl.program_id(0)
        row_vals, acc_vals = fn(*[r[...] for r in refs[:ni]])
        for r, v in zip(refs[ni:ni + no], row_vals):
            r[...] = v.astype(r.dtype)
        for r, v in zip(refs[ni + no:], acc_vals):
            @pl.when(i == 0)
            def _(r=r, v=v):
                r[...] = v

            @pl.when(i > 0)
            def _(r=r, v=v):
                r[...] += v

    return pl.pallas_call(
        body,
        name=name,
        grid=(R // tr,),
        in_specs=in_specs,
        out_specs=[pl.BlockSpec((tr, c), lambda i: (i, 0)) for c, _ in outs]
        + [pl.BlockSpec(s, lambda i: (0, 0)) for s in accs],
        out_shape=[jax.ShapeDtypeStruct((R, c), dt) for c, dt in outs]
        + [jax.ShapeDtypeStruct(s, F32) for s in accs],
        compiler_params=_params("arbitrary"),
    )(*ins)


def _sigmoid(x):
    return 1.0 / (1.0 + jnp.exp(-x))


_GELU_C = math.sqrt(2.0 / math.pi)


def _gelu(x):
    return 0.5 * x * (1.0 + jnp.tanh(_GELU_C * (x + 0.044715 * (x * x * x))))


def _gelu_grad(x):
    t = jnp.tanh(_GELU_C * (x + 0.044715 * (x * x * x)))
    return 0.5 * (1.0 + t) + 0.5 * x * (1.0 - t * t) * (_GELU_C * (1.0 + 3.0 * 0.044715 * (x * x)))


def rms_fwd(x, g, name):
    def fn(x, g):
        r = lax.rsqrt(jnp.mean(x * x, axis=-1, keepdims=True) + EPS)
        return [x * r * g], []

    return rowwise(fn, [x, g], [(x.shape[1], BF16)], name=name)[0]


def _rms_tile(x, g):
    return x * lax.rsqrt(jnp.mean(x * x, axis=-1, keepdims=True) + EPS) * g


def _rms_bwd_tile(dn, x, g, dres):
    r = lax.rsqrt(jnp.mean(x * x, axis=-1, keepdims=True) + EPS)
    w = dn * g
    dx = r * w - x * (r * r * r) * jnp.mean(x * w, axis=-1, keepdims=True)
    return dres + dx, jnp.sum(dn * (x * r), axis=0, keepdims=True)


def _whole(shape):
    return pl.BlockSpec(shape, lambda: (0,) * len(shape))


def _zoh(lr, li, ldt):
    dt = jnp.exp(ldt)
    mag = jnp.exp(lr * dt)
    ar, ai = mag * jnp.cos(li * dt), mag * jnp.sin(li * dt)
    den = lr * lr + li * li
    kr = ((ar - 1.0) * lr + ai * li) / den
    ki = (ai * lr - (ar - 1.0) * li) / den
    return dt, ar, ai, den, kr, ki


def ssm_prep(lam_re, lam_im, log_dt, b_re, b_im, name):
    n = SSM_CH

    def body(lr_ref, li_ref, ldt_ref, br_ref, bi_ref, ar_ref, ai_ref, bbr_ref, bbi_ref):
        _, ar, ai, _, kr, ki = _zoh(lr_ref[...], li_ref[...], ldt_ref[...])
        br, bi = br_ref[...], bi_ref[...]
        ar_ref[...] = ar
        ai_ref[...] = ai
        bbr_ref[...] = kr * br - ki * bi
        bbi_ref[...] = kr * bi + ki * br

    rows = lam_re.shape[0]
    col, mat = pl.BlockSpec((n, 1), lambda i: (i, 0)), pl.BlockSpec((n, SSM_GROUP_CH), lambda i: (i, 0))
    return pl.pallas_call(
        body, name=name,
        grid=(rows // n,),
        in_specs=[col] * 3 + [mat] * 2,
        out_specs=[col] * 2 + [mat] * 2,
        out_shape=[jax.ShapeDtypeStruct((rows, 1), F32)] * 2 + [jax.ShapeDtypeStruct((rows, SSM_GROUP_CH), F32)] * 2,
        compiler_params=_params("parallel"),
    )(lam_re, lam_im, log_dt, b_re, b_im)


def ssm_prep_bwd(lam_re, lam_im, log_dt, b_re, b_im, d_ar, d_ai, d_bbr, d_bbi, name):
    n = lam_re.shape[0]
    n_groups = n // SSM_STATE

    def body(lr_ref, li_ref, ldt_ref, br_ref, bi_ref, dar_ref, dai_ref, dbr_ref, dbi_ref,
             glr_ref, gli_ref, gdt_ref, gbr_ref, gbi_ref):
        lr, li = lr_ref[...], li_ref[...]
        dt, ar, ai, den, kr, ki = _zoh(lr, li, ldt_ref[...])
        br, bi, dbr, dbi = br_ref[...], bi_ref[...], dbr_ref[...], dbi_ref[...]
        gbr_ref[...] = kr * dbr + ki * dbi
        gbi_ref[...] = kr * dbi - ki * dbr
        gkr = jnp.sum(br * dbr + bi * dbi, axis=1, keepdims=True)
        gki = jnp.sum(br * dbi - bi * dbr, axis=1, keepdims=True)
        gar = dar_ref[...] + (gkr * lr - gki * li) / den
        gai = dai_ref[...] + (gki * lr + gkr * li) / den
        qr, qi = -(kr * lr + ki * li) / den, -(ki * lr - kr * li) / den
        g1r, g1i = qr * gkr + qi * gki, qr * gki - qi * gkr
        g2r, g2i = dt * (ar * gar + ai * gai), dt * (ar * gai - ai * gar)
        glr_ref[...] = g1r + g2r
        gli_ref[...] = g1i + g2i
        pr, pi_ = lr * ar - li * ai, lr * ai + li * ar
        gdt = (pr * gar + pi_ * gai) * dt
        grp = lax.broadcasted_iota(jnp.int32, (n, n_groups), 0) // SSM_STATE
        sel = grp == lax.broadcasted_iota(jnp.int32, (n, n_groups), 1)
        gdt_ref[...] = jnp.sum(jnp.where(sel, gdt, 0.0), axis=0, keepdims=True)

    col, mat = (n, 1), (n, SSM_GROUP_CH)
    return pl.pallas_call(
        body, name=name,
        in_specs=[_whole(col)] * 3 + [_whole(mat)] * 2 + [_whole(col)] * 2 + [_whole(mat)] * 2,
        out_specs=[_whole(col)] * 2 + [_whole((1, n_groups))] + [_whole(mat)] * 2,
        out_shape=[jax.ShapeDtypeStruct(col, F32)] * 2 + [jax.ShapeDtypeStruct((1, n_groups), F32)]
        + [jax.ShapeDtypeStruct(mat, F32)] * 2,
        compiler_params=_params(),
    )(lam_re, lam_im, log_dt, b_re, b_im, d_ar, d_ai, d_bbr, d_bbi)


def _cmul(ar, ai, br, bi):
    return ar * br - ai * bi, ar * bi + ai * br


def _scan_block(xr, xi, lr, li, carry_r, carry_i, or_ref, oi_ref, loc_r, loc_i, reverse):
    tb, cb = xr.shape
    ng = tb // SUBLANES
    xr = xr.reshape(ng, SUBLANES, cb)
    xi = xi.reshape(ng, SUBLANES, cb)
    rid = lax.broadcasted_iota(jnp.int32, (1, SUBLANES, cb), 1)
    pr, pi_ = lr.reshape(1, 1, cb), li.reshape(1, 1, cb)
    powers = []
    for k in (1, 2, 4):
        powers.append((pr, pi_))
        shift = SUBLANES - k if reverse else k
        sr, si = pltpu.roll(xr, shift, 1), pltpu.roll(xi, shift, 1)
        keep = (rid < SUBLANES - k) if reverse else (rid >= k)
        tr_, ti_ = _cmul(jnp.where(keep, pr, 0.0), jnp.where(keep, pi_, 0.0), sr, si)
        xr = xr + tr_
        xi = xi + ti_
        pr, pi_ = _cmul(pr, pi_, pr, pi_)
    loc_r[...] = xr
    loc_i[...] = xi
    (p1r, p1i), (p2r, p2i), (p4r, p4i) = powers
    dist = lax.broadcasted_iota(jnp.int32, (SUBLANES, cb), 0)
    if reverse:
        dist = SUBLANES - 1 - dist
    wr = jnp.broadcast_to(p1r.reshape(1, cb), (SUBLANES, cb))
    wi = jnp.broadcast_to(p1i.reshape(1, cb), (SUBLANES, cb))
    for bit, (qr, qi) in ((1, (p1r, p1i)), (2, (p2r, p2i)), (4, (p4r, p4i))):
        mr, mi = _cmul(wr, wi, qr.reshape(1, cb), qi.reshape(1, cb))
        on = (dist & bit) != 0
        wr, wi = jnp.where(on, mr, wr), jnp.where(on, mi, wi)
    last = 0 if reverse else SUBLANES - 1

    def step(j, carry):
        cr, ci = carry
        g = (ng - 1 - j) if reverse else j
        fr = loc_r[g] + (wr * cr - wi * ci)
        fi = loc_i[g] + (wr * ci + wi * cr)
        rows = pl.ds(pl.multiple_of(g * SUBLANES, SUBLANES), SUBLANES)
        or_ref[rows, :] = fr
        oi_ref[rows, :] = fi
        return fr[last:last + 1, :], fi[last:last + 1, :]

    cr, ci = lax.fori_loop(0, ng, step, (carry_r[...], carry_i[...]))
    carry_r[...] = cr
    carry_i[...] = ci


def _scan_tiles(seq_len, n_ch):
    return min(256, seq_len), min(512, n_ch)


def _run_exchange(plan, step, n_steps, refs):
    @pl.when(step == 0)
    def _():
        plan.start(*refs)

    for part, at in plan.relay_steps(n_steps):
        @pl.when(step == at)
        def _(part=part):
            plan.relay(part, *refs)

    @pl.when(step == n_steps - 1)
    def _():
        plan.finish(*refs)


def _exchange_args(plan):
    if plan is None:
        return [], [], []
    return list(plan.ins), list(plan.out_shape), list(plan.sems)


def _split_refs(refs, *counts):
    groups, at = [], 0
    for n in counts:
        groups.append(refs[at:at + n])
        at += n
    return groups + [refs[at:]]


def ssm_scan(x_re, x_im, lam_re, lam_im, batch, name, exchange=None):
    n, nch = x_re.shape
    seq = n // batch
    tb, cb = _scan_tiles(seq, nch)
    nt, nc = seq // tb, nch // cb
    ex_ins, ex_out, ex_sems = _exchange_args(exchange)

    def body(*refs):
        ins, xin, outs, xout, scratch, xsems = _split_refs(refs, 4, len(ex_ins), 2, len(ex_out), 6)
        xr_ref, xi_ref, lr_ref, li_ref = ins
        or_ref, oi_ref = outs
        car_r, car_i, loc_r, loc_i, s_r, s_i = scratch
        if exchange is not None:
            step = (pl.program_id(0) * batch + pl.program_id(1)) * nt + pl.program_id(2)
            _run_exchange(exchange, step, nc * batch * nt, (xin, xout, xsems))

        @pl.when(pl.program_id(2) == 0)
        def _():
            car_r[...] = jnp.zeros_like(car_r)
            car_i[...] = jnp.zeros_like(car_i)

        _scan_block(xr_ref[...].astype(F32), xi_ref[...].astype(F32), lr_ref[...], li_ref[...], car_r, car_i,
                    s_r, s_i, loc_r, loc_i, reverse=False)
        or_ref[...] = s_r[...].astype(or_ref.dtype)
        oi_ref[...] = s_i[...].astype(oi_ref.dtype)

    blk = pl.BlockSpec((tb, cb), lambda c, b, t: (b * nt + t, c))
    lam_spec = pl.BlockSpec((1, cb), lambda c, b, t: (0, c))
    res = pl.pallas_call(
        body, name=name,
        grid=(nc, batch, nt),
        in_specs=[blk, blk, lam_spec, lam_spec] + [ANY] * len(ex_ins),
        out_specs=[blk, blk] + [ANY] * len(ex_out),
        out_shape=[jax.ShapeDtypeStruct((n, nch), BF16)] * 2 + ex_out,
        scratch_shapes=[pltpu.VMEM((1, cb), F32)] * 2 + [pltpu.VMEM((tb // SUBLANES, SUBLANES, cb), F32)] * 2
        + [pltpu.VMEM((tb, cb), F32)] * 2 + ex_sems,
        compiler_params=_params("arbitrary", "arbitrary", "arbitrary"),
    )(x_re, x_im, lam_re, lam_im, *ex_ins)
    return res[0], res[1], list(res[2:])


def ssm_scan_bwd(d_re, d_im, s_re, s_im, lam_re, lam_im, batch, name, exchange=None):
    n, nch = d_re.shape
    seq = n // batch
    tb, cb = _scan_tiles(seq, nch)
    nt, nc = seq // tb, nch // cb
    halo_rows = 2 * SUBLANES
    hb = tb // halo_rows
    ex_ins, ex_out, ex_sems = _exchange_args(exchange)

    def body(*refs):
        ins, xin, outs, xout, scratch, xsems = _split_refs(refs, 8, len(ex_ins), 4, len(ex_out), 6)
        xr_ref, xi_ref, sr_ref, si_ref, hr_ref, hi_ref, lr_ref, li_ref = ins
        or_ref, oi_ref, dlr_ref, dli_ref = outs
        car_r, car_i, loc_r, loc_i, g_r, g_i = scratch
        b, t = pl.program_id(1), pl.program_id(2)
        if exchange is not None:
            step = (pl.program_id(0) * batch + b) * nt + t
            _run_exchange(exchange, step, nc * batch * nt, (xin, xout, xsems))

        @pl.when(t == 0)
        def _():
            car_r[...] = jnp.zeros_like(car_r)
            car_i[...] = jnp.zeros_like(car_i)

        _scan_block(xr_ref[...].astype(F32), xi_ref[...].astype(F32), lr_ref[...], -li_ref[...], car_r, car_i,
                    g_r, g_i, loc_r, loc_i, reverse=True)
        gr, gi = g_r[...], g_i[...]
        or_ref[...] = gr.astype(or_ref.dtype)
        oi_ref[...] = gi.astype(oi_ref.dtype)
        first_block = t == nt - 1
        row = lax.broadcasted_iota(jnp.int32, (tb, cb), 0)
        hr = jnp.where(first_block, 0.0, hr_ref[...].astype(F32)[halo_rows - 1:halo_rows, :])
        hi = jnp.where(first_block, 0.0, hi_ref[...].astype(F32)[halo_rows - 1:halo_rows, :])
        pr = jnp.where(row == 0, hr, pltpu.roll(sr_ref[...].astype(F32), 1, 0))
        pi_ = jnp.where(row == 0, hi, pltpu.roll(si_ref[...].astype(F32), 1, 0))
        dlr = jnp.sum(gr * pr + gi * pi_, axis=0, keepdims=True)
        dli = jnp.sum(gi * pr - gr * pi_, axis=0, keepdims=True)
        start = jnp.logical_and(b == 0, t == 0)

        @pl.when(start)
        def _():
            dlr_ref[...] = dlr
            dli_ref[...] = dli

        @pl.when(jnp.logical_not(start))
        def _():
            dlr_ref[...] += dlr
            dli_ref[...] += dli

    def blk(c, b, t):
        return b * nt + (nt - 1 - t)

    st_spec = pl.BlockSpec((tb, cb), lambda c, b, t: (blk(c, b, t), c))
    halo_spec = pl.BlockSpec((halo_rows, cb), lambda c, b, t: (jnp.maximum(blk(c, b, t) * hb - 1, 0), c))
    row_spec = pl.BlockSpec((1, cb), lambda c, b, t: (0, c))
    res = pl.pallas_call(
        body, name=name,
        grid=(nc, batch, nt),
        in_specs=[st_spec] * 4 + [halo_spec] * 2 + [row_spec] * 2 + [ANY] * len(ex_ins),
        out_specs=[st_spec, st_spec, row_spec, row_spec] + [ANY] * len(ex_out),
        out_shape=[jax.ShapeDtypeStruct((n, nch), BF16)] * 2 + [jax.ShapeDtypeStruct((1, nch), F32)] * 2 + ex_out,
        scratch_shapes=[pltpu.VMEM((1, cb), F32)] * 2 + [pltpu.VMEM((tb // SUBLANES, SUBLANES, cb), F32)] * 2
        + [pltpu.VMEM((tb, cb), F32)] * 2 + ex_sems,
        compiler_params=_params("arbitrary", "arbitrary", "arbitrary"),
    )(d_re, d_im, s_re, s_im, s_re, s_im, lam_re, lam_im, *ex_ins)
    return res[0], res[1], res[2], res[3], list(res[4:])


def _pool_tiles(seq_len):
    return min(512, seq_len)


def _window_sums(x, n_steps, forward_in_time):
    rows = x.shape[0]
    k = 1
    for _ in range(n_steps):
        x = x + pltpu.roll(x, k if forward_in_time else rows - k, 0)
        k *= 2
    return x


def pool_fwd(u, w_pool, scale, batch, name):
    n, c = u.shape
    seq = n // batch
    tb = _pool_tiles(seq)
    nt = seq // tb
    gc = c // len(POOL_WINDOWS)
    hb = tb // POOL_HALO

    def body(x_ref, halo_ref, w_ref, sc_ref, y_ref, q_ref):
        t = pl.program_id(1)
        halo = jnp.where(t == 0, 0.0, halo_ref[...])
        full = jnp.concatenate([halo, x_ref[...]], axis=0)
        pos = lax.broadcasted_iota(jnp.int32, (tb, gc), 0) + t * tb + 1
        for gi, win in enumerate(POOL_WINDOWS):
            cols = slice(gi * gc, (gi + 1) * gc)
            sums = _window_sums(full[:, cols], gi + 1, True)[POOL_HALO:, :]
            cnt = jnp.minimum(pos, win).astype(F32)
            q = sums / cnt - x_ref[:, cols]
            r = jnp.dot(q.astype(BF16), w_ref[gi].astype(BF16), preferred_element_type=F32)
            q_ref[:, cols] = q.astype(q_ref.dtype)
            y_ref[:, cols] = (r * sc_ref[:, cols]).astype(y_ref.dtype)

    return pl.pallas_call(
        body, name=name,
        grid=(batch, nt),
        in_specs=[pl.BlockSpec((tb, c), lambda b, t: (b * nt + t, 0)),
                  pl.BlockSpec((POOL_HALO, c), lambda b, t: (jnp.maximum((b * nt + t) * hb - 1, 0), 0)),
                  pl.BlockSpec(w_pool.shape, lambda b, t: (0, 0, 0)),
                  pl.BlockSpec((1, c), lambda b, t: (0, 0))],
        out_specs=[pl.BlockSpec((tb, c), lambda b, t: (b * nt + t, 0))] * 2,
        out_shape=[jax.ShapeDtypeStruct((n, c), BF16)] * 2,
        compiler_params=_params("parallel", "arbitrary"),
    )(u, u, w_pool, scale)


def pool_bwd(dy, q, w_pool, scale, batch, name):
    n, c = dy.shape
    seq = n // batch
    tb = _pool_tiles(seq)
    nt = seq // tb
    ng = len(POOL_WINDOWS)
    gc = c // ng
    hb = tb // POOL_HALO
    n_blocks = n // POOL_HALO

    def body(dy_ref, dyh_ref, q_ref, w_ref, sc_ref, du_ref, dw_ref, dsc_ref):
        b, t = pl.program_id(0), pl.program_id(1)
        last = t == nt - 1
        dy_full = jnp.concatenate([dy_ref[...], jnp.where(last, 0.0, dyh_ref[...])], axis=0)
        pos = lax.broadcasted_iota(jnp.int32, (tb + POOL_HALO, gc), 0) + t * tb + 1
        start = jnp.logical_and(b == 0, t == 0)
        for gi, win in enumerate(POOL_WINDOWS):
            cols = slice(gi * gc, (gi + 1) * gc)
            w = w_ref[gi].astype(BF16)
            dr = dy_full[:, cols] * sc_ref[:, cols]
            dq = lax.dot_general(dr.astype(BF16), w, (((1,), (1,)), ((), ())), preferred_element_type=F32)
            cnt = jnp.minimum(pos, win).astype(F32)
            back = _window_sums(dq / cnt, gi + 1, False)
            du_ref[:, cols] = back[:tb, :] - dq[:tb, :]
            qb = q_ref[:, cols]
            r = jnp.dot(qb, w, preferred_element_type=F32)
            dw = lax.dot_general(qb, dr[:tb, :].astype(BF16), (((0,), (0,)), ((), ())), preferred_element_type=F32)
            dsc = jnp.sum(dy_ref[:, cols] * r, axis=0, keepdims=True)

            @pl.when(start)
            def _(gi=gi, cols=cols, dw=dw, dsc=dsc):
                dw_ref[gi] = dw
                dsc_ref[:, cols] = dsc

            @pl.when(jnp.logical_not(start))
            def _(gi=gi, cols=cols, dw=dw, dsc=dsc):
                dw_ref[gi] += dw
                dsc_ref[:, cols] += dsc

    blk = pl.BlockSpec((tb, c), lambda b, t: (b * nt + t, 0))
    halo = pl.BlockSpec((POOL_HALO, c), lambda b, t: (jnp.minimum((b * nt + t + 1) * hb, n_blocks - 1), 0))
    return pl.pallas_call(
        body, name=name,
        grid=(batch, nt),
        in_specs=[blk, halo, blk,
                  pl.BlockSpec(w_pool.shape, lambda b, t: (0, 0, 0)),
                  pl.BlockSpec((1, c), lambda b, t: (0, 0))],
        out_specs=[blk, pl.BlockSpec(w_pool.shape, lambda b, t: (0, 0, 0)), pl.BlockSpec((1, c), lambda b, t: (0, 0))],
        out_shape=[jax.ShapeDtypeStruct((n, c), F32), jax.ShapeDtypeStruct(w_pool.shape, F32),
                   jax.ShapeDtypeStruct((1, c), F32)],
        compiler_params=_params("arbitrary", "arbitrary"),
    )(dy, dy, q, w_pool, scale)


def _place():
    return lax.axis_index("x"), lax.axis_index("y"), lax.axis_index("c")


class GatherPlan:
    def __init__(self, arrs):
        self.ins = list(arrs)
        na = len(arrs)
        self.out_shape = [jax.ShapeDtypeStruct((N_DEV,) + a.shape, a.dtype) for a in arrs]
        self.sems = [pltpu.SemaphoreType.DMA((na, 7)), pltpu.SemaphoreType.DMA((na, 7)), pltpu.SemaphoreType.DMA((na,))]
        self.sizes = [math.prod(a.shape) * a.dtype.itemsize for a in arrs]

    def relay_steps(self, n_steps):
        total, done, steps = sum(self.sizes), 0, []
        for a, size in enumerate(self.sizes):
            done += size
            steps.append((a, min(n_steps - 1, (done * (n_steps - 1)) // total)))
        return steps

    def _copy(self, outs, sems, a, k, block, to, src=None):
        dst = outs[a].at[4 * block[0] + 2 * block[1] + block[2]]
        return pltpu.make_async_remote_copy(
            src_ref=dst if src is None else src, dst_ref=dst,
            send_sem=sems[0].at[a, k], recv_sem=sems[1].at[a, k], device_id=to, device_id_type=MESH)

    @staticmethod
    def _chips(x, y):
        return [(1 - x, y), (x, 1 - y), (1 - x, 1 - y)]

    def _local(self, ins, outs, sems, a, me):
        return pltpu.make_async_copy(ins[a], outs[a].at[4 * me[0] + 2 * me[1] + me[2]], sems[2].at[a])

    def start(self, ins, outs, sems):
        x, y, c = _place()
        me = (x, y, c)
        for a in range(len(ins)):
            self._local(ins, outs, sems, a, me).start()
            self._copy(outs, sems, a, 0, me, (x, y, 1 - c), src=ins[a]).start()
            for j, chip in enumerate(self._chips(x, y)):
                self._copy(outs, sems, a, 1 + j, me, (*chip, c), src=ins[a]).start()

    def relay(self, a, ins, outs, sems):
        x, y, c = _place()
        for j, chip in enumerate(self._chips(x, y)):
            self._copy(outs, sems, a, 1 + j, (*chip, c), (x, y, c)).wait_recv()
            self._copy(outs, sems, a, 4 + j, (*chip, c), (x, y, 1 - c)).start()

    def finish(self, ins, outs, sems):
        x, y, c = _place()
        me, sibling = (x, y, c), (x, y, 1 - c)
        for a in range(len(ins)):
            self._copy(outs, sems, a, 0, sibling, me).wait_recv()
            for j, chip in enumerate(self._chips(x, y)):
                self._copy(outs, sems, a, 4 + j, (*chip, 1 - c), me).wait_recv()
        for a in range(len(ins)):
            self._copy(outs, sems, a, 0, me, sibling, src=ins[a]).wait_send()
            for j, chip in enumerate(self._chips(x, y)):
                self._copy(outs, sems, a, 1 + j, me, (*chip, c), src=ins[a]).wait_send()
                self._copy(outs, sems, a, 4 + j, (*chip, c), sibling).wait_send()
            self._local(ins, outs, sems, a, me).wait()


class ChipScatterPlan:
    def __init__(self, arrs):
        self.groups = [list(a) if isinstance(a, list) else [a] for a in arrs]
        self.ins = [piece for group in self.groups for piece in group]
        self.first = [sum(len(g) for g in self.groups[:a]) for a in range(len(self.groups))]
        na = len(arrs)
        self.out_shape = [jax.ShapeDtypeStruct((4,) + g[0].shape[1:], g[0].dtype) for g in self.groups]
        self.sems = [pltpu.SemaphoreType.DMA((na, 3)), pltpu.SemaphoreType.DMA((na, 3)), pltpu.SemaphoreType.DMA((na,))]

    def relay_steps(self, n_steps):
        return []

    def _row(self, ins, a, px, py):
        if len(self.groups[a]) == 1:
            return ins[self.first[a]].at[2 * px + py]
        return ins[self.first[a] + px].at[py]

    def start(self, ins, outs, sems):
        x, y, c = _place()
        mine = 2 * x + y
        for xs in (0, 1):
            @pl.when(x == xs)
            def _(xs=xs):
                for a in range(len(self.groups)):
                    pltpu.make_async_copy(self._row(ins, a, xs, y), outs[a].at[mine], sems[2].at[a]).start()
                    for j, (px, py) in enumerate([(1 - xs, y), (xs, 1 - y), (1 - xs, 1 - y)]):
                        pltpu.make_async_remote_copy(
                            src_ref=self._row(ins, a, px, py), dst_ref=outs[a].at[mine],
                            send_sem=sems[0].at[a, j], recv_sem=sems[1].at[a, j],
                            device_id=(px, py, c), device_id_type=MESH).start()

    def finish(self, ins, outs, sems):
        x, y, c = _place()
        for wait_recv in (True, False):
            for a in range(len(self.groups)):
                for j in range(3):
                    cp = pltpu.make_async_remote_copy(
                        src_ref=self._row(ins, a, 0, 0), dst_ref=outs[a].at[0],
                        send_sem=sems[0].at[a, j], recv_sem=sems[1].at[a, j],
                        device_id=(x, y, c), device_id_type=MESH)
                    if wait_recv:
                        cp.wait_recv()
                    else:
                        cp.wait_send()
        for a in range(len(self.groups)):
            pltpu.make_async_copy(self._row(ins, a, 0, 0), outs[a].at[0], sems[2].at[a]).wait()


def run_exchange(plan, name):
    n_in, n_out = len(plan.ins), len(plan.out_shape)

    def body(*refs):
        parts = (refs[:n_in], refs[n_in:n_in + n_out], refs[n_in + n_out:])
        plan.start(*parts)
        for part, _ in plan.relay_steps(1):
            plan.relay(part, *parts)
        plan.finish(*parts)

    return pl.pallas_call(
        body, name=name,
        in_specs=[ANY] * n_in, out_specs=[ANY] * n_out,
        out_shape=plan.out_shape, scratch_shapes=plan.sems,
    )(*plan.ins)


class SiblingSwapPlan:
    def __init__(self, arrs):
        self.ins = list(arrs)
        na = len(arrs)
        self.out_shape = [jax.ShapeDtypeStruct(a.shape, a.dtype) for a in arrs]
        self.sems = [pltpu.SemaphoreType.DMA((na,)), pltpu.SemaphoreType.DMA((na,))]

    def relay_steps(self, n_steps):
        return []

    @staticmethod
    def _copies(ins, outs, sems):
        x, y, c = _place()
        return [pltpu.make_async_remote_copy(
            src_ref=ins[a], dst_ref=outs[a], send_sem=sems[0].at[a], recv_sem=sems[1].at[a],
            device_id=(x, y, 1 - c), device_id_type=MESH) for a in range(len(ins))]

    def start(self, ins, outs, sems):
        for cp in self._copies(ins, outs, sems):
            cp.start()

    def finish(self, ins, outs, sems):
        for cp in self._copies(ins, outs, sems):
            cp.wait()


def adamw(w, gparts, m, v, name, tr=256):
    rows, cols = w.shape
    parts = gparts.shape[0]
    tr = _row_tile(rows, tr)
    c1 = 1.0 - ADAM_B1 ** ADAM_STEP
    c2 = 1.0 - ADAM_B2 ** ADAM_STEP

    def body(w_ref, g_ref, m_ref, v_ref, go_ref, d_ref, mo_ref, vo_ref):
        g = g_ref[0].astype(F32)
        for p_ in range(1, parts):
            g = g + g_ref[p_].astype(F32)
        m_new = ADAM_B1 * m_ref[...] + (1.0 - ADAM_B1) * g
        v_new = ADAM_B2 * v_ref[...] + (1.0 - ADAM_B2) * (g * g)
        m_hat = m_new / c1
        v_hat = v_new / c2
        go_ref[...] = g
        d_ref[...] = -ADAM_LR * (m_hat / (jnp.sqrt(v_hat) + ADAM_EPS) + ADAM_WD * w_ref[...])
        mo_ref[...] = m_new
        vo_ref[...] = v_new

    blk = pl.BlockSpec((tr, cols), lambda i: (i, 0))
    return pl.pallas_call(
        body, name=name,
        grid=(rows // tr,),
        in_specs=[blk, pl.BlockSpec((parts, tr, cols), lambda i: (0, i, 0)), blk, blk],
        out_specs=[blk] * 4,
        out_shape=[jax.ShapeDtypeStruct((rows, cols), F32)] * 4,
        compiler_params=_params("parallel"),
    )(w, gparts, m, v)


def add2(a, b, name, out_dtype):
    return rowwise(lambda a, b: ([a.astype(F32) + b.astype(F32)], []), [a, b], [(a.shape[1], out_dtype)],
                   name=name, tr=256)[0]


def _block_diag(x):
    g, a, b = x.shape
    eye = jnp.eye(g, dtype=x.dtype)
    return (x[:, :, None, :] * eye[:, None, :, None]).reshape(g * a, g * b)


def _diag_blocks(x, a, b):
    per = x.shape[1] // b
    x5 = x.reshape(SSM_SUPER, per, a, per, b)
    eye = jnp.eye(per, dtype=x.dtype)
    return jnp.sum(x5 * eye[None, :, None, :, None], axis=3).reshape(SSM_SUPER * per, a, b)


def _swiglu_epi(g, u):
    s = _sigmoid(g)
    silu = g * s
    return u * (s * (1.0 + g * (1.0 - s))), silu, silu * u


def _residual_epi(scale, with_norm):
    if with_norm:
        def epi(acc, res, gain):
            out = res + scale * acc
            return out, _rms_tile(out, gain)
    else:
        def epi(acc, res):
            return (res + scale * acc,)
    return epi


def _next_norm(next_gain):
    if next_gain is None:
        return [], (F32,)
    return [next_gain], (F32, BF16)


FFN_WIDE = 2816


def ffn_fwd(h, n, wi, wo, next_gain, tag, carry=None):
    if carry is None:
        carry = lambda where, run: run(None)[0]
    dact_g, dact_u, act = carry(f"{tag}_in", lambda ex: _with_exchange(matmul(
        n, [(wi, 0), (wi, 1)], mode="nn", name=f"{tag}_in", separate=True, epi=_swiglu_epi,
        out_dtypes=(BF16, BF16, BF16), exchange=ex, tm=512, tn=FFN_WIDE), ex))
    more, dtypes = _next_norm(next_gain)
    res = carry(f"{tag}_out", lambda ex: _with_exchange(matmul(
        act, wo, mode="nn", name=f"{tag}_out", epi=_residual_epi(0.5, bool(more)), extras=[h] + more,
        out_dtypes=dtypes, exchange=ex, tm=512, tk=FFN_WIDE), ex))
    return res[0], (res[1] if more else None), (h, n, dact_g, dact_u, act)


def ffn_bwd(dh, saved, gain, wi, wo, tag, carry=None, gate=None, on_weight_grads=None):
    h, n, dact_g, dact_u, act = saved
    if carry is None:
        carry = lambda where, run: run(None)[0]
    dg, du = carry(f"{tag}_out_dx", lambda ex: _with_exchange(matmul(
        dh, wo, mode="nt", name=f"{tag}_out_dx", extras=[dact_g, dact_u], out_dtypes=(BF16, BF16),
        epi=lambda acc, fg, fu: (0.5 * acc * fg, 0.5 * acc * fu), exchange=ex, tm=512, tn=FFN_WIDE), ex))
    d_wo, = carry(f"{tag}_out_dw", lambda ex: _with_exchange(matmul(
        act, dh, mode="tn", name=f"{tag}_out_dw", scale=0.5, exchange=ex, tm=FFN_WIDE), ex))
    d_wi = [carry(f"{tag}_in_dw_{k}", lambda ex, half=half: _with_exchange(matmul(
        n, half, mode="tn", name=f"{tag}_in_dw", exchange=ex, tn=FFN_WIDE), ex))[0] for k, half in (("g", dg), ("u", du))]
    if on_weight_grads is not None:
        on_weight_grads(d_wi, d_wo)
    if gate is None:
        epi, more, dtypes = _rms_bwd_tile, [], (F32,)
    else:
        def epi(acc, x, g, dres, e, pre):
            dh_new, d_gain = _rms_bwd_tile(acc, x, g, dres)
            return (dh_new, *_gate_cotangents(dh_new, e, pre), d_gain)
        more, dtypes = list(gate), (F32, BF16, BF16)
    res = carry(f"{tag}_in_dx", lambda ex: _with_exchange(matmul(
        [dg, du], [(wi, 0), (wi, 1)], mode="nt", name=f"{tag}_in_dx", epi=epi, extras=[h, gain, dh] + more,
        out_dtypes=dtypes, col_sums=1, exchange=ex, tm=256, tk=FFN_WIDE), ex))
    return res[0], res[-1], d_wi, d_wo, tuple(res[1:-1])


def _with_exchange(result, exchange):
    return result if exchange is not None else (result, [])


def ssm_tables(rep):
    rows = DEPTH * SSM_CH
    lam_r, lam_i, bb_r, bb_i = ssm_prep(
        rep["ssm_lambda_re"].reshape(rows, 1), rep["ssm_lambda_im"].reshape(rows, 1),
        jnp.repeat(rep["ssm_log_dt"].reshape(-1), SSM_STATE).reshape(rows, 1),
        rep["ssm_b_re"].reshape(rows, SSM_GROUP_CH), rep["ssm_b_im"].reshape(rows, SSM_GROUP_CH), "ssm_zoh")
    lam = (lam_r.reshape(DEPTH, 1, SSM_CH), lam_i.reshape(DEPTH, 1, SSM_CH))

    def dense(x):
        return jax.vmap(_block_diag)(x.astype(BF16))

    b_mats = [dense(bb.reshape(DEPTH, SSM_GROUPS, SSM_STATE, SSM_GROUP_CH).transpose(0, 1, 3, 2)) for bb in (bb_r, bb_i)]
    c_mats = [dense(cc.transpose(0, 1, 3, 2)) for cc in (rep["ssm_c_re"], -rep["ssm_c_im"])]
    return lam, b_mats, c_mats


def mix_fwd(h, n, lw, tables, layer, batch, next_gain, tag, exchange=None):
    sw = h.shape[1] // 2
    us, up = matmul(n, [lw["w_in"][:, :sw], lw["w_in"][:, sw:]], mode="nn", name=f"{tag}_in", separate=True,
                    out_dtypes=(F32, F32))
    lam = tuple(t[layer] for t in tables[0])
    b_mats = [(m, layer) for m in tables[1]]
    c_mats = [(m, layer) for m in tables[2]]
    bu_re, bu_im = matmul(us, b_mats, mode="nn", name=f"{tag}_bu", separate=True, diag=SSM_SUPER,
                          out_dtypes=(BF16, BF16))
    s_re, s_im, exchanged = ssm_scan(bu_re, bu_im, *lam, batch, f"{tag}_scan", exchange)
    y0, y1 = matmul([s_re, s_im], c_mats, mode="nn", name=f"{tag}_c", diag=SSM_SUPER,
                    epi=lambda acc, u, d: (acc + d * u, _gelu(acc + d * u)), extras=[us, lw["ssm_d"]],
                    out_dtypes=(F32, BF16))
    y2, gl = matmul(y1, lw["w_glu"], mode="nn", name=f"{tag}_glu",
                    epi=lambda acc, y0: (_gelu(y0) * _sigmoid(acc), acc), extras=[y0], out_dtypes=(BF16, F32))
    yp, q = pool_fwd(up, lw["pool_w"], lw["pool_scale"], batch, f"{tag}_pool")
    more, dtypes = _next_norm(next_gain)
    res = matmul([y2, yp], [lw["w_out"][:sw], lw["w_out"][sw:]], mode="nn", name=f"{tag}_out",
                 epi=_residual_epi(1.0, bool(more)), extras=[h] + more, out_dtypes=dtypes)
    saved = (h, n, us, lam, b_mats, c_mats, s_re, s_im, y0, y1, gl, y2, yp, q)
    return res[0], (res[1] if more else None), saved, exchanged


def mix_bwd(dh, saved, lw, batch, tag, exchange=None):
    h, n, us, lam, b_mats, c_mats, s_re, s_im, y0, y1, gl, y2, yp, q = saved
    sw = h.shape[1] // 2
    w_out_s, w_out_p = lw["w_out"][:sw], lw["w_out"][sw:]
    d_wo_s, d_wo_p = matmul([y2, yp], dh, mode="tn", name=f"{tag}_out_dw", separate=True)
    def out_dx_epi(dy2, dyp, y0, gl):
        sg = _sigmoid(gl)
        return dy2, dyp, dy2 * _gelu(y0) * sg * (1.0 - sg)

    dy2, dyp, tg = matmul(dh, [w_out_s, w_out_p], mode="nt", name=f"{tag}_out_dx", separate=True, epi=out_dx_epi,
                          extras=[y0, gl], out_dtypes=(F32, F32, BF16))
    dup, d_pool_w, d_pool_scale = pool_bwd(dyp, q, lw["pool_w"], lw["pool_scale"], batch, f"{tag}_pool_bwd")
    def dy0_epi(acc, dy2, gl, y0, u):
        dy0 = (acc + dy2 * _sigmoid(gl)) * _gelu_grad(y0)
        return dy0, jnp.sum(dy0 * u, axis=0, keepdims=True)

    dy0, d_d = matmul(tg, lw["w_glu"], mode="nt", name=f"{tag}_glu_dx", epi=dy0_epi, extras=[dy2, gl, y0, us],
                      out_dtypes=(F32,), col_sums=1)
    d_w_glu, = matmul(y1, tg, mode="tn", name=f"{tag}_glu_dw")
    gd_re, gd_im = matmul(dy0, c_mats, mode="nt", name=f"{tag}_c_dx", separate=True, diag=SSM_SUPER,
                          out_dtypes=(BF16, BF16))
    d_c_top, d_c_bot = matmul([s_re, s_im], dy0, mode="tn", name=f"{tag}_c_dw", separate=True, diag=SSM_SUPER)
    g_re, g_im, d_lam_r, d_lam_i, exchanged = ssm_scan_bwd(gd_re, gd_im, s_re, s_im, *lam, batch, f"{tag}_scan_bwd",
                                                           exchange)
    dus, = matmul([g_re, g_im], b_mats, mode="nt", name=f"{tag}_bu_dx", diag=SSM_SUPER,
                  epi=lambda acc, dy0, d: (acc + d * dy0,), extras=[dy0, lw["ssm_d"]])
    d_b_re, d_b_im = matmul(us, [g_re, g_im], mode="tn", name=f"{tag}_bu_dw", separate=True, diag=SSM_SUPER)
    d_bb_r = _diag_blocks(d_b_re, SSM_GROUP_CH, SSM_STATE).transpose(0, 2, 1).reshape(SSM_CH, SSM_GROUP_CH)
    d_bb_i = _diag_blocks(d_b_im, SSM_GROUP_CH, SSM_STATE).transpose(0, 2, 1).reshape(SSM_CH, SSM_GROUP_CH)
    d_lr, d_li, d_ldt, d_br, d_bi = ssm_prep_bwd(
        lw["lam_re"], lw["lam_im"], lw["log_dt"], lw["b_re"], lw["b_im"],
        d_lam_r.reshape(SSM_CH, 1), d_lam_i.reshape(SSM_CH, 1), d_bb_r, d_bb_i, f"{tag}_zoh_bwd")
    d_c_re = _diag_blocks(d_c_top, SSM_STATE, SSM_GROUP_CH).transpose(0, 2, 1)
    d_c_im = -_diag_blocks(d_c_bot, SSM_STATE, SSM_GROUP_CH).transpose(0, 2, 1)
    d_w_in_s, d_w_in_p = matmul(n, [dus, dup], mode="tn", name=f"{tag}_in_dw", separate=True)
    dh_new, d_gain = matmul([dus, dup], [lw["w_in"][:, :sw], lw["w_in"][:, sw:]], mode="nt", name=f"{tag}_in_dx",
                            epi=_rms_bwd_tile, extras=[h, lw["mix_norm"], dh], out_dtypes=(F32,), col_sums=1, tm=512)
    grads = dict(mix_norm=d_gain, w_in=jnp.concatenate([d_w_in_s, d_w_in_p], axis=1),
                 ssm_lambda_re=d_lr, ssm_lambda_im=d_li, ssm_log_dt=d_ldt, ssm_b_re=d_br, ssm_b_im=d_bi,
                 ssm_c_re=d_c_re, ssm_c_im=d_c_im, ssm_d=d_d, ssm_w_glu=d_w_glu, pool_w=d_pool_w,
                 pool_scale=d_pool_scale, w_out=jnp.concatenate([d_wo_s, d_wo_p], axis=0))
    return dh_new, grads, exchanged


def ple_fwd(h, n, p, w_gate, w_proj, next_gain, tag):
    e, = matmul(p, w_proj, mode="nn", name=f"{tag}_proj")
    if next_gain is None:
        def epi(acc, e, res):
            return res + _sigmoid(acc) * e, acc
        more, dtypes = [], (F32, F32)
    else:
        def epi(acc, e, res, gain):
            out = res + _sigmoid(acc) * e
            return out, acc, _rms_tile(out, gain)
        more, dtypes = [next_gain], (F32, F32, BF16)
    res = matmul(n, w_gate, mode="nn", name=f"{tag}_gate", epi=epi, extras=[e, h] + more, out_dtypes=dtypes, tm=512)
    return res[0], (res[2] if more else None), (h, n, e, res[1])


def _gate_cotangents(dh, e, pre):
    s = _sigmoid(pre)
    return dh * e * s * (1.0 - s), dh * s


def ple_bwd(dh, dpre, de, saved, p, gain, w_gate, tag):
    h, n, e, pre = saved
    d_w_gate, = matmul(n, dpre, mode="tn", name=f"{tag}_gate_dw")
    d_w_proj, = matmul(p, de, mode="tn", name=f"{tag}_proj_dw")
    dh_new, d_gain = matmul(dpre, w_gate, mode="nt", name=f"{tag}_gate_dx", epi=_rms_bwd_tile,
                            extras=[h, gain, dh], out_dtypes=(F32,), col_sums=1, tm=512)
    return dh_new, d_gain, d_w_gate, d_w_proj


def loss_head(h, gain, target, e, pre, name):
    d = h.shape[1]

    def fn(h, g, t, e, pre):
        r = lax.rsqrt(jnp.mean(h * h, axis=-1, keepdims=True) + EPS)
        diff = h * r * g - t
        sq = jnp.sum(jnp.sum(diff * diff, axis=1, keepdims=True), axis=0, keepdims=True)
        dy = diff * (1.0 / d)
        w = dy * g
        dh = r * w - h * (r * r * r) * jnp.mean(h * w, axis=-1, keepdims=True)
        return [dh, *_gate_cotangents(dh, e, pre)], [sq, jnp.sum(dy * (h * r), axis=0, keepdims=True)]

    dh, dpre, de, sq, d_gain = rowwise(fn, [h, gain, target, e, pre], [(d, F32), (d, BF16), (d, BF16)],
                                       [(1, 1), (1, d)], name=name, tr=256)
    return 0.5 / d * sq[0, 0], dh, dpre, de, d_gain


SHARDED = {
    "ffn1_wi": 1, "ffn1_wo": 0, "w_in": 0, "ssm_w_glu": 0, "w_out": 0, "ffn2_wi": 1, "ffn2_wo": 0,
    "ple_w_gate": 0, "ple_w_proj": 1,
}
WEIGHTS = ["ffn1_norm", "ffn1_wi", "ffn1_wo", "mix_norm", "w_in", "ssm_lambda_re", "ssm_lambda_im", "ssm_log_dt",
           "ssm_b_re", "ssm_b_im", "ssm_c_re", "ssm_c_im", "ssm_d", "ssm_w_glu", "pool_w", "pool_scale", "w_out",
           "ffn2_norm", "ffn2_wi", "ffn2_wo", "ple_norm", "ple_w_gate", "ple_w_proj", "final_norm"]
REPLICATED = [n for n in WEIGHTS if n not in SHARDED]


HALVED = ("ffn1_wi", "ffn2_wi")


def _unshard(gathered, axis, halved):
    if halved:
        _, rows, cols = gathered.shape
        return gathered.reshape(2, 4, rows, cols).transpose(0, 2, 1, 3).reshape(2, rows, 4 * cols)
    g = jnp.moveaxis(gathered, 0, axis)
    shp = g.shape
    return g.reshape(shp[:axis] + (shp[axis] * shp[axis + 1],) + shp[axis + 2:])


def _split_for_scatter(full, axis, c, halved):
    if halved:
        rows, cols = full[0].shape

        def pick(cc):
            return [lax.dynamic_index_in_dim(h.reshape(rows, 2, 2, cols // 4), cc, 2, keepdims=False).transpose(1, 0, 2)
                    for h in full]

        return pick(c), [s.astype(BF16) for s in pick(1 - c)]
    shp = full.shape
    g = full.reshape(shp[:axis] + (4, 2, shp[axis] // N_DEV) + shp[axis + 1:])
    keep = lax.dynamic_index_in_dim(g, c, axis + 1, keepdims=False)
    send = lax.dynamic_index_in_dim(g, 1 - c, axis + 1, keepdims=False)
    return jnp.moveaxis(keep, axis, 0), jnp.moveaxis(send, axis, 0).astype(BF16)


def _pack(arrs):
    pieces = []
    for a in arrs:
        flat = a.reshape(-1)
        pad = (-flat.shape[0]) % PACK
        pieces.append(jnp.pad(flat, (0, pad)).reshape(-1, LANES))
    return jnp.concatenate(pieces, axis=0)


def _unpack(packed, shapes):
    out, row = [], 0
    for s in shapes:
        size = math.prod(s)
        rows = (size + PACK - 1) // PACK * SUBLANES
        out.append(packed[row:row + rows].reshape(-1)[:size].reshape(s))
        row += rows
    return out


def local_step(x, p, target, rep, hooks):
    batch, seq, d = x.shape
    n_tok = batch * seq
    h = x.reshape(n_tok, d)
    saved = []
    n = rms_fwd(h, rep["ffn1_norm"][0].reshape(1, d), "first_norm")
    tables = ssm_tables(rep)
    for i in range(DEPTH):
        lw = _layer_weights(rep, i, d)
        big = lambda name, i=i: hooks.big(i, name)
        next_gain = rep["ffn1_norm"][i + 1].reshape(1, d) if i + 1 < DEPTH else None
        def carry(where, run, i=i):
            outs, exchanged = run(hooks.fwd_exchange(i, where))
            hooks.fwd_done(i, where, exchanged)
            return outs

        h, n, s1 = ffn_fwd(h, n, big("ffn1_wi"), big("ffn1_wo"), lw["mix_norm"], "ffn1", carry)
        lw.update(w_in=big("w_in"), w_glu=big("ssm_w_glu"), w_out=big("w_out"))
        h, n, s2, exchanged = mix_fwd(h, n, lw, tables, i, batch, lw["ffn2_norm"], "mix", hooks.fwd_exchange(i, "scan"))
        hooks.fwd_done(i, "scan", exchanged)
        h, n, s3 = ffn_fwd(h, n, big("ffn2_wi"), big("ffn2_wo"), lw["ple_norm"], "ffn2", carry)
        p_i = p[i].reshape(n_tok, -1)
        h, n, s4 = ple_fwd(h, n, p_i, big("ple_w_gate"), big("ple_w_proj"), next_gain, "ple")
        saved.append((s1, s2, s3, s4, p_i))
    last_gate = saved[-1][3][2:]
    loss, dh, dpre, de, d_final = loss_head(h, rep["final_norm"].reshape(1, d), target.reshape(n_tok, d), *last_gate,
                                            "loss_head")
    hooks.final_grad(d_final)
    per_layer = [None] * DEPTH
    for i in reversed(range(DEPTH)):
        lw = _layer_weights(rep, i, d)
        big = lambda name, i=i: hooks.big(i, name)
        lw.update(w_in=big("w_in"), w_glu=big("ssm_w_glu"), w_out=big("w_out"))
        s1, s2, s3, s4, p_i = saved[i]
        g = {}
        dh, g["ple_norm"], g["ple_w_gate"], g["ple_w_proj"] = ple_bwd(dh, dpre, de, s4, p_i, lw["ple_norm"],
                                                                     big("ple_w_gate"), "ple")
        def carry(where, run, i=i):
            outs, exchanged = run(hooks.bwd_exchange(i, where))
            hooks.bwd_done(i, where, exchanged)
            return outs

        dh, g["ffn2_norm"], g["ffn2_wi"], g["ffn2_wo"], _ = ffn_bwd(
            dh, s3, lw["ffn2_norm"], big("ffn2_wi"), big("ffn2_wo"), "ffn2", carry)
        dh, gm, exchanged = mix_bwd(dh, s2, lw, batch, "mix", hooks.bwd_exchange(i, "scan"))
        hooks.bwd_done(i, "scan", exchanged)
        g.update(gm)
        hooks.layer_grads(i, g, [k for k in SHARDED if k not in LAST_GRADS])
        hooks.small_grads(i, {k: g[k] for k in REPLICATED if k in g})
        below = saved[i - 1][3][2:] if i > 0 else None

        def ffn1_grads(d_wi, d_wo, i=i, g=g):
            g["ffn1_wi"], g["ffn1_wo"] = d_wi, d_wo
            hooks.layer_grads(i, g, list(LAST_GRADS))

        dh, g["ffn1_norm"], _, _, gate_ct = ffn_bwd(
            dh, s1, lw["ffn1_norm"], big("ffn1_wi"), big("ffn1_wo"), "ffn1", carry, below, ffn1_grads)
        hooks.small_grads(i, {"ffn1_norm": g["ffn1_norm"]})
        if i > 0:
            dpre, de = gate_ct
        per_layer[i] = g
    return loss, dh.reshape(batch, seq, d), per_layer, d_final


LAST_GRADS = ("ffn1_wi", "ffn1_wo")


def _layer_weights(w, i, d):
    sw = d // 2
    lw = {}
    lw["pool_w"] = w["pool_w"][i]
    for k in ("ffn1_norm", "mix_norm", "ffn2_norm", "ple_norm"):
        lw[k] = w[k][i].reshape(1, d)
    lw["ssm_d"] = w["ssm_d"][i].reshape(1, sw)
    lw["pool_scale"] = w["pool_scale"][i].reshape(1, sw)
    lw["lam_re"] = w["ssm_lambda_re"][i].reshape(SSM_CH, 1)
    lw["lam_im"] = w["ssm_lambda_im"][i].reshape(SSM_CH, 1)
    lw["log_dt"] = jnp.repeat(w["ssm_log_dt"][i], SSM_STATE).reshape(SSM_CH, 1)
    lw["b_re"] = w["ssm_b_re"][i].reshape(SSM_CH, SSM_GROUP_CH)
    lw["b_im"] = w["ssm_b_im"][i].reshape(SSM_CH, SSM_GROUP_CH)
    lw["c_re"] = w["ssm_c_re"][i]
    lw["c_im"] = w["ssm_c_im"][i]
    return lw


class MeshExchange:
    FIRST = ("ffn1_wi", "ffn1_wo")
    FIRST_LAYER_PLAN = {"ffn1_in": ("w_in", "ssm_w_glu", "w_out", "ffn2_wi", "ffn2_wo"),
                        "ffn1_out": ("ple_w_gate", "ple_w_proj")}
    FWD_PLAN = {"scan": ("ffn1_wi", "ffn1_wo", "w_in", "ssm_w_glu", "w_out"),
                "ffn2_in": ("ffn2_wi", "ffn2_wo"), "ffn2_out": ("ple_w_gate", "ple_w_proj")}
    BWD_PLAN = {"ffn2_in_dx": ("ffn1_wo", "ffn2_wo", "ssm_w_glu", "w_out", "ple_w_gate", "ple_w_proj"),
                "scan": ("ffn1_wi", "ffn2_wi", "w_in")}
    LAST_LAYER_PLAN = {"ffn1_out_dx": ("ffn2_wi", "w_in"),
                       "ffn1_out_dw": ("ffn2_wo", "w_out", "ple_w_gate", "ssm_w_glu", "ple_w_proj"),
                       "ffn1_in_dx": LAST_GRADS}

    def __init__(self, shards):
        self.small = {}
        self.d_final = None
        self.small_parts = []
        self.small_rest = None
        self.swaps = {}
        self.shards = shards
        self.c = lax.axis_index("c")
        self.gathered = {}
        self.chip_sums = {}
        self.from_chips = {}
        self.pending = None
        first = run_exchange(GatherPlan([shards[0][k] for k in self.FIRST]), "gather_first_weights")
        self._store(self.gathered, 0, self.FIRST, first)

    @staticmethod
    def _store(where, layer, names, results):
        for k, r in zip(names, results):
            where[(layer, k)] = r

    def big(self, i, name):
        return _unshard(self.gathered[(i, name)], SHARDED[name], name in HALVED)

    def fwd_exchange(self, i, where):
        if i == 0 and where in self.FIRST_LAYER_PLAN:
            layer, names = 0, self.FIRST_LAYER_PLAN[where]
        elif where in self.FWD_PLAN and i + 1 < DEPTH:
            layer, names = i + 1, self.FWD_PLAN[where]
        else:
            return None
        self.pending = (layer, names)
        return GatherPlan([self.shards[layer][k] for k in names])

    def fwd_done(self, i, where, results):
        if results:
            self._store(self.gathered, *self.pending, results)

    def layer_grads(self, i, grads, names):
        pieces, sends = [], []
        for k in names:
            keep, send = _split_for_scatter(grads[k], SHARDED[k], self.c, k in HALVED)
            keep, send = (keep, send) if isinstance(keep, list) else ([keep], [send])
            pieces.append(keep)
            sends += send
        if i == 0:
            self._add_sibling(0, names, pieces, run_exchange(SiblingSwapPlan(sends), "reduce_core_pair"))
        else:
            self.swaps[(i, tuple(names) == LAST_GRADS)] = (names, pieces, sends)

    def _add_sibling(self, i, names, pieces, from_sibling):
        got = iter(from_sibling)
        for k, keep in zip(names, pieces):
            sums = []
            for part in keep:
                cols = part.shape[-1]
                sums.append(add2(part.reshape(-1, cols), next(got).reshape(-1, cols), f"sum_core_pair_{k}",
                                 BF16).reshape(part.shape))
            self.chip_sums[(i, k)] = sums if len(sums) > 1 else sums[0]

    def small_grads(self, i, grads):
        self.small.setdefault(i, {}).update(grads)

    def final_grad(self, d_final):
        self.d_final = d_final

    def _small_pack(self, d_final, first_norm_grad=None):
        pieces = []
        for k in REPLICATED:
            if k == "final_norm":
                pieces.append(d_final)
            elif k == "ffn1_norm" and first_norm_grad is None:
                pieces.append(jnp.stack([self.small[i][k] for i in range(1, DEPTH)], axis=0))
            else:
                pieces.append(jnp.stack([self.small[i][k] for i in range(DEPTH)], axis=0))
        return _pack(pieces)

    def bwd_exchange(self, i, where):
        if where in self.BWD_PLAN and i + 1 < DEPTH:
            layer, names = i + 1, self.BWD_PLAN[where]
        elif i == 0 and where in self.LAST_LAYER_PLAN:
            layer, names = 0, self.LAST_LAYER_PLAN[where]
        elif i == 0 and where in ("ffn1_in_dw_g", "ffn1_in_dw_u"):
            if where.endswith("g"):
                pack = self._small_pack(self.d_final)
                half = pack.shape[0] // 2 // SUBLANES * SUBLANES
                part, self.small_rest = pack[:half], pack[half:]
            else:
                part = self.small_rest
            self.pending = "small"
            return GatherPlan([part])
        elif i > 0 and where in ("ffn1_out_dx", "ffn1_in_dx"):
            self.pending = ("swap", i, where == "ffn1_in_dx")
            return SiblingSwapPlan(self.swaps[self.pending[1:]][2])
        else:
            return None
        self.pending = (layer, names)
        return ChipScatterPlan([self.chip_sums[(layer, k)] for k in names])

    def bwd_done(self, i, where, results):
        if not results:
            return
        if self.pending == "small":
            self.small_parts += results
        elif self.pending[0] == "swap":
            names, pieces, _ = self.swaps[self.pending[1:]]
            self._add_sibling(self.pending[1], names, pieces, results)
        else:
            self._store(self.from_chips, *self.pending, results)

    def small_gathered(self):
        first = self.small[0]["ffn1_norm"].reshape(SUBLANES, LANES)
        first_all, = run_exchange(GatherPlan([first]), "gather_first_gain_grad")
        return jnp.concatenate([first_all] + self.small_parts, axis=1)


def kernel(x, p, ffn1_norm, ffn1_wi, ffn1_wo, mix_norm, w_in, ssm_lambda_re, ssm_lambda_im, ssm_log_dt, ssm_b_re, ssm_b_im, ssm_c_re, ssm_c_im, ssm_d, ssm_w_glu, pool_w, pool_scale, w_out, ffn2_norm, ffn2_wi, ffn2_wo, ple_norm, ple_w_gate, ple_w_proj, final_norm, loss_target, m_ffn1_norm, m_ffn1_wi, m_ffn1_wo, m_mix_norm, m_w_in, m_ssm_lambda_re, m_ssm_lambda_im, m_ssm_log_dt, m_ssm_b_re, m_ssm_b_im, m_ssm_c_re, m_ssm_c_im, m_ssm_d, m_ssm_w_glu, m_pool_w, m_pool_scale, m_w_out, m_ffn2_norm, m_ffn2_wi, m_ffn2_wo, m_ple_norm, m_ple_w_gate, m_ple_w_proj, m_final_norm, v_ffn1_norm, v_ffn1_wi, v_ffn1_wo, v_mix_norm, v_w_in, v_ssm_lambda_re, v_ssm_lambda_im, v_ssm_log_dt, v_ssm_b_re, v_ssm_b_im, v_ssm_c_re, v_ssm_c_im, v_ssm_d, v_ssm_w_glu, v_pool_w, v_pool_scale, v_w_out, v_ffn2_norm, v_ffn2_wi, v_ffn2_wo, v_ple_norm, v_ple_w_gate, v_ple_w_proj, v_final_norm):
    args = dict(locals())
    wts = {k: args[k] for k in WEIGHTS}
    rep = {k: wts[k] for k in REPLICATED}

    shards = [{k: wts[k][i].astype(BF16) for k in SHARDED} for i in range(DEPTH)]
    exchange = MeshExchange(shards)
    loss_local, grad_x, per_layer, d_final = local_step(x, p, loss_target, rep, exchange)
    loss = lax.psum(loss_local, ("x", "y", "c"))

    outs = {}
    for k in SHARDED:
        shp = wts[k].shape
        cols = shp[-1]
        parts = jnp.stack([exchange.from_chips[(i, k)] for i in range(DEPTH)], axis=1)
        res = adamw(wts[k].reshape(-1, cols), parts.reshape(4, -1, cols), args["m_" + k].reshape(-1, cols),
                    args["v_" + k].reshape(-1, cols), f"adamw_{k}")
        outs[k] = [r.reshape(shp) for r in res]

    rep_shapes = [wts[k].shape for k in REPLICATED]
    all_g = exchange.small_gathered()
    res = adamw(_pack([wts[k] for k in REPLICATED]), all_g, _pack([args["m_" + k] for k in REPLICATED]),
                _pack([args["v_" + k] for k in REPLICATED]), "adamw_small")
    unpacked = [_unpack(r, rep_shapes) for r in res]
    for j, k in enumerate(REPLICATED):
        outs[k] = [unpacked[q][j] for q in range(4)]

    result = [loss, grad_x]
    for q in range(4):
        result += [outs[k][q] for k in WEIGHTS]
    return tuple(result)
```

```python
import math

import jax
import jax.numpy as jnp
from jax import lax
from jax.experimental import pallas as pl
from jax.experimental.pallas import tpu as pltpu

F32 = jnp.float32
BF16 = jnp.bfloat16
MESH = pl.DeviceIdType.MESH
ANY = pl.BlockSpec(memory_space=pl.ANY)

N_DEV = 8
DEPTH = 4
EPS = 1e-6
SSM_GROUPS = 32
SSM_GROUP_CH = 16
SSM_STATE = 64
SSM_CH = SSM_GROUPS * SSM_STATE
SSM_SUPER = 2
POOL_WINDOWS = (2, 4, 8, 16)
POOL_HALO = 16
ADAM_LR, ADAM_B1, ADAM_B2, ADAM_EPS, ADAM_WD, ADAM_STEP = 0.001, 0.9, 0.999, 1e-08, 0.01, 10

V7X_VMEM_BYTES = 64 * 1024 * 1024
VMEM_LIMIT_BYTES = V7X_VMEM_BYTES - 12 * 1024 * 1024
LANES = 128
SUBLANES = 8
PACK = SUBLANES * LANES


def _params(*sem):
    return pltpu.CompilerParams(dimension_semantics=sem or None, vmem_limit_bytes=VMEM_LIMIT_BYTES)


def _tile(n, pref):
    if n <= pref:
        return n
    t = pref - pref % LANES
    while t >= LANES:
        if n % t == 0:
            return t
        t -= LANES
    raise ValueError(f"no lane-aligned tile for {n}")


def _row_tile(rows, pref):
    if rows <= pref:
        return rows
    t = pref - pref % SUBLANES
    while t >= SUBLANES:
        if rows % t == 0:
            return t
        t -= SUBLANES
    raise ValueError(f"no sublane-aligned tile for {rows}")


_DIMS = {"nn": ((1,), (0,)), "nt": ((1,), (1,)), "tn": ((0,), (0,))}


def matmul(a, b, *, mode, name, out_dtypes=None, epi=None, extras=(), separate=False, diag=1, col_sums=0,
           stack_out=False, scale=1.0, exchange=None, tm=1024, tn=1024, tk=1024):
    a_list = list(a) if isinstance(a, list) else [a]
    b_list = list(b) if isinstance(b, list) else [b]
    a_planes = [x[1] if isinstance(x, tuple) else None for x in a_list]
    b_planes = [x[1] if isinstance(x, tuple) else None for x in b_list]
    a_list = [x[0] if isinstance(x, tuple) else x for x in a_list]
    b_list = [x[0] if isinstance(x, tuple) else x for x in b_list]
    a_shape, b_shape = a_list[0].shape[-2:], b_list[0].shape[-2:]
    n_terms = max(len(a_list), len(b_list))
    a_idx = [0] * n_terms if len(a_list) == 1 else list(range(n_terms))
    b_idx = [0] * n_terms if len(b_list) == 1 else list(range(n_terms))
    n_acc = n_terms if separate else 1
    assert not (stack_out or scale != 1.0) or epi is None
    if out_dtypes is None:
        out_dtypes = (F32,) * (1 if (epi is not None or stack_out) else n_acc)
    in_place = epi is None
    if mode == "tn":
        K, M = a_shape
        K2, N = b_shape
    elif mode == "nt":
        M, K = a_shape
        N, K2 = b_shape
    else:
        M, K = a_shape
        K2, N = b_shape
    assert K == K2, (name, a_shape, b_shape)
    if mode == "tn":
        tm, tn, tk = _tile(M // diag, tm), _tile(N // diag, tn), _tile(K, tk)
        nk = K // tk
        N = N // diag
        row_tiles, col_tiles = (M // diag) // tm, N // tn
        a_blk, a_map = (tk, tm), lambda i, j, k: (k, i)
        b_blk, b_map = (tk, tn), lambda i, j, k: (k, (i // row_tiles) * col_tiles + j)
    else:
        tm, tn, tk = _tile(M, tm), _tile(N // diag, tn), _tile(K // diag, tk)
        nk = (K // diag) // tk
        col_tiles = (N // diag) // tn
        a_blk, a_map = (tm, tk), lambda i, j, k: (i, (j // col_tiles) * nk + k)
        if mode == "nt":
            b_blk, b_map = (tn, tk), lambda i, j, k: (j, (j // col_tiles) * nk + k)
        else:
            b_blk, b_map = (tk, tn), lambda i, j, k: ((j // col_tiles) * nk + k, j)

    def plane_spec(blk, index_map, plane):
        if plane is None:
            return pl.BlockSpec(blk, index_map)
        return pl.BlockSpec((None,) + blk, lambda i, j, k: (plane,) + index_map(i, j, k))

    a_specs = [plane_spec(a_blk, a_map, p_) for p_ in a_planes]
    b_specs = [plane_spec(b_blk, b_map, p_) for p_ in b_planes]
    assert not col_sums or N == tn, (name, N, tn)
    ex_specs = []
    for e in extras:
        if e.shape == (M, N):
            ex_specs.append(pl.BlockSpec((tm, tn), lambda i, j, k: (i, j)))
        elif e.shape == (1, N):
            ex_specs.append(pl.BlockSpec((1, tn), lambda i, j, k: (0, j)))
        elif e.shape == (M, 1):
            ex_specs.append(pl.BlockSpec((tm, 1), lambda i, j, k: (i, 0)))
        else:
            raise ValueError((name, e.shape, (M, N)))
    na, nb, ne, no = len(a_list), len(b_list), len(extras), len(out_dtypes)
    dims = (_DIMS[mode], ((), ()))

    n_scratch = n_acc if (nk > 1 and not in_place) else 0
    x_ins, x_out, x_sems = _exchange_args(exchange)
    grid = (M // tm, N // tn, nk)

    def body(*refs):
        a_refs, b_refs, ex_refs, xin, out_refs, sum_refs, xout, acc_refs, xsems = _split_refs(
            refs, na, nb, ne, len(x_ins), no, col_sums, len(x_out), n_scratch)
        if exchange is not None:
            step = (pl.program_id(0) * grid[1] + pl.program_id(1)) * nk + pl.program_id(2)
            _run_exchange(exchange, step, grid[0] * grid[1] * nk, (xin, xout, xsems))
        a_vals = [r[...].astype(BF16) for r in a_refs]
        b_vals = [r[...].astype(BF16) for r in b_refs]
        prods = [lax.dot_general(a_vals[a_idx[t]], b_vals[b_idx[t]], dims, preferred_element_type=F32)
                 for t in range(n_terms)]
        if not separate:
            total = prods[0]
            for p_ in prods[1:]:
                total = total + p_
            prods = [total]

        def finish(accs):
            res = epi(*accs, *[e[...] for e in ex_refs]) if epi is not None else tuple(accs)
            for r, v in zip(out_refs, res[:no]):
                r[...] = v.astype(r.dtype)
            first_rows = pl.program_id(0) == 0
            for r, v in zip(sum_refs, res[no:]):
                @pl.when(first_rows)
                def _(r=r, v=v):
                    r[...] = v

                @pl.when(jnp.logical_not(first_rows))
                def _(r=r, v=v):
                    r[...] += v

        if in_place:
            dst = [(out_refs[0], t) for t in range(n_acc)] if stack_out else [(r, None) for r in out_refs]

            def read(r, t):
                return r[...] if t is None else r[t]

            assert nk == 1 or all(dt == F32 for dt in out_dtypes), name

            def write(r, t, v):
                if t is None:
                    r[...] = v.astype(r.dtype)
                else:
                    r[t] = v.astype(r.dtype)

            if nk == 1:
                for (r, t), v in zip(dst, prods):
                    write(r, t, v * scale if scale != 1.0 else v)
            else:
                k = pl.program_id(2)

                @pl.when(k == 0)
                def _():
                    for (r, t), v in zip(dst, prods):
                        write(r, t, v)

                @pl.when(jnp.logical_and(k > 0, k < nk - 1))
                def _():
                    for (r, t), v in zip(dst, prods):
                        write(r, t, read(r, t) + v)

                @pl.when(k == nk - 1)
                def _():
                    for (r, t), v in zip(dst, prods):
                        total = read(r, t) + v
                        write(r, t, total * scale if scale != 1.0 else total)
        elif nk == 1:
            finish(prods)
        else:
            k = pl.program_id(2)

            @pl.when(k == 0)
            def _():
                for r, v in zip(acc_refs, prods):
                    r[...] = v

            @pl.when(jnp.logical_and(k > 0, k < nk - 1))
            def _():
                for r, v in zip(acc_refs, prods):
                    r[...] += v

            @pl.when(k == nk - 1)
            def _():
                finish([r[...] + v for r, v in zip(acc_refs, prods)])

    if stack_out:
        out_specs = [pl.BlockSpec((n_acc, tm, tn), lambda i, j, k: (0, i, j))]
        out_shape = [jax.ShapeDtypeStruct((n_acc, M, N), F32)]
    else:
        out_specs = [pl.BlockSpec((tm, tn), lambda i, j, k: (i, j))] * no
        out_shape = [jax.ShapeDtypeStruct((M, N), dt) for dt in out_dtypes]
    sequential = col_sums or exchange is not None
    outs = pl.pallas_call(
        body,
        name=name,
        grid=grid,
        in_specs=a_specs + b_specs + ex_specs + [ANY] * len(x_ins),
        out_specs=out_specs + [pl.BlockSpec((1, tn), lambda i, j, k: (0, j))] * col_sums + [ANY] * len(x_out),
        out_shape=out_shape + [jax.ShapeDtypeStruct((1, N), F32)] * col_sums + x_out,
        scratch_shapes=[pltpu.VMEM((tm, tn), F32)] * n_scratch + x_sems,
        compiler_params=_params(*(("arbitrary",) * 3 if sequential else ("parallel", "parallel", "arbitrary"))),
    )(*a_list, *b_list, *extras, *x_ins)
    if exchange is not None:
        n_own = len(outs) - len(x_out)
        return list(outs[:n_own]), list(outs[n_own:])
    return outs


def rowwise(fn, ins, outs, accs=(), *, name, tr=512):
    R = max(x.shape[0] for x in ins)
    tr = _row_tile(R, tr)
    in_specs = []
    for x in ins:
        if x.shape[0] == R and x.ndim == 2:
            in_specs.append(pl.BlockSpec((tr, x.shape[1]), lambda i: (i, 0)))
        else:
            in_specs.append(pl.BlockSpec(x.shape, lambda i, _n=x.ndim: (0,) * _n))
    ni, no = len(ins), len(outs)

    def body(*refs):
        i = pl.program_id(0)
        row_vals, acc_vals = fn(*[r[...] for r in refs[:ni]])
        for r, v in zip(refs[ni:ni + no], row_vals):
            r[...] = v.astype(r.dtype)
        for r, v in zip(refs[ni + no:], acc_vals):
            @pl.when(i == 0)
            def _(r=r, v=v):
                r[...] = v

            @pl.when(i > 0)
            def _(r=r, v=v):
                r[...] += v

    return pl.pallas_call(
        body,
        name=name,
        grid=(R // tr,),
        in_specs=in_specs,
        out_specs=[pl.BlockSpec((tr, c), lambda i: (i, 0)) for c, _ in outs]
        + [pl.BlockSpec(s, lambda i: (0, 0)) for s in accs],
        out_shape=[jax.ShapeDtypeStruct((R, c), dt) for c, dt in outs]
        + [jax.ShapeDtypeStruct(s, F32) for s in accs],
        compiler_params=_params("arbitrary"),
    )(*ins)


def _sigmoid(x):
    return 1.0 / (1.0 + jnp.exp(-x))


_GELU_C = math.sqrt(2.0 / math.pi)


def _gelu(x):
    return 0.5 * x * (1.0 + jnp.tanh(_GELU_C * (x + 0.044715 * (x * x * x))))


def _gelu_grad(x):
    t = jnp.tanh(_GELU_C * (x + 0.044715 * (x * x * x)))
    return 0.5 * (1.0 + t) + 0.5 * x * (1.0 - t * t) * (_GELU_C * (1.0 + 3.0 * 0.044715 * (x * x)))


def rms_fwd(x, g, name):
    def fn(x, g):
        r = lax.rsqrt(jnp.mean(x * x, axis=-1, keepdims=True) + EPS)
        return [x * r * g], []

    return rowwise(fn, [x, g], [(x.shape[1], BF16)], name=name)[0]


def _rms_tile(x, g):
    return x * lax.rsqrt(jnp.mean(x * x, axis=-1, keepdims=True) + EPS) * g


def _rms_bwd_tile(dn, x, g, dres):
    r = lax.rsqrt(jnp.mean(x * x, axis=-1, keepdims=True) + EPS)
    w = dn * g
    dx = r * w - x * (r * r * r) * jnp.mean(x * w, axis=-1, keepdims=True)
    return dres + dx, jnp.sum(dn * (x * r), axis=0, keepdims=True)


def _whole(shape):
    return pl.BlockSpec(shape, lambda: (0,) * len(shape))


def _zoh(lr, li, ldt):
    dt = jnp.exp(ldt)
    mag = jnp.exp(lr * dt)
    ar, ai = mag * jnp.cos(li * dt), mag * jnp.sin(li * dt)
    den = lr * lr + li * li
    kr = ((ar - 1.0) * lr + ai * li) / den
    ki = (ai * lr - (ar - 1.0) * li) / den
    return dt, ar, ai, den, kr, ki


def ssm_prep(lam_re, lam_im, log_dt, b_re, b_im, name):
    n = SSM_CH

    def body(lr_ref, li_ref, ldt_ref, br_ref, bi_ref, ar_ref, ai_ref, bbr_ref, bbi_ref):
        _, ar, ai, _, kr, ki = _zoh(lr_ref[...], li_ref[...], ldt_ref[...])
        br, bi = br_ref[...], bi_ref[...]
        ar_ref[...] = ar
        ai_ref[...] = ai
        bbr_ref[...] = kr * br - ki * bi
        bbi_ref[...] = kr * bi + ki * br

    rows = lam_re.shape[0]
    col, mat = pl.BlockSpec((n, 1), lambda i: (i, 0)), pl.BlockSpec((n, SSM_GROUP_CH), lambda i: (i, 0))
    return pl.pallas_call(
        body, name=name,
        grid=(rows // n,),
        in_specs=[col] * 3 + [mat] * 2,
        out_specs=[col] * 2 + [mat] * 2,
        out_shape=[jax.ShapeDtypeStruct((rows, 1), F32)] * 2 + [jax.ShapeDtypeStruct((rows, SSM_GROUP_CH), F32)] * 2,
        compiler_params=_params("parallel"),
    )(lam_re, lam_im, log_dt, b_re, b_im)


def ssm_prep_bwd(lam_re, lam_im, log_dt, b_re, b_im, d_ar, d_ai, d_bbr, d_bbi, name):
    n = lam_re.shape[0]
    n_groups = n // SSM_STATE

    def body(lr_ref, li_ref, ldt_ref, br_ref, bi_ref, dar_ref, dai_ref, dbr_ref, dbi_ref,
             glr_ref, gli_ref, gdt_ref, gbr_ref, gbi_ref):
        lr, li = lr_ref[...], li_ref[...]
        dt, ar, ai, den, kr, ki = _zoh(lr, li, ldt_ref[...])
        br, bi, dbr, dbi = br_ref[...], bi_ref[...], dbr_ref[...], dbi_ref[...]
        gbr_ref[...] = kr * dbr + ki * dbi
        gbi_ref[...] = kr * dbi - ki * dbr
        gkr = jnp.sum(br * dbr + bi * dbi, axis=1, keepdims=True)
        gki = jnp.sum(br * dbi - bi * dbr, axis=1, keepdims=True)
        gar = dar_ref[...] + (gkr * lr - gki * li) / den
        gai = dai_ref[...] + (gki * lr + gkr * li) / den
        qr, qi = -(kr * lr + ki * li) / den, -(ki * lr - kr * li) / den
        g1r, g1i = qr * gkr + qi * gki, qr * gki - qi * gkr
        g2r, g2i = dt * (ar * gar + ai * gai), dt * (ar * gai - ai * gar)
        glr_ref[...] = g1r + g2r
        gli_ref[...] = g1i + g2i
        pr, pi_ = lr * ar - li * ai, lr * ai + li * ar
        gdt = (pr * gar + pi_ * gai) * dt
        grp = lax.broadcasted_iota(jnp.int32, (n, n_groups), 0) // SSM_STATE
        sel = grp == lax.broadcasted_iota(jnp.int32, (n, n_groups), 1)
        gdt_ref[...] = jnp.sum(jnp.where(sel, gdt, 0.0), axis=0, keepdims=True)

    col, mat = (n, 1), (n, SSM_GROUP_CH)
    return pl.pallas_call(
        body, name=name,
        in_specs=[_whole(col)] * 3 + [_whole(mat)] * 2 + [_whole(col)] * 2 + [_whole(mat)] * 2,
        out_specs=[_whole(col)] * 2 + [_whole((1, n_groups))] + [_whole(mat)] * 2,
        out_shape=[jax.ShapeDtypeStruct(col, F32)] * 2 + [jax.ShapeDtypeStruct((1, n_groups), F32)]
        + [jax.ShapeDtypeStruct(mat, F32)] * 2,
        compiler_params=_params(),
    )(lam_re, lam_im, log_dt, b_re, b_im, d_ar, d_ai, d_bbr, d_bbi)


def _cmul(ar, ai, br, bi):
    return ar * br - ai * bi, ar * bi + ai * br


def _scan_block(xr, xi, lr, li, carry_r, carry_i, or_ref, oi_ref, loc_r, loc_i, reverse):
    tb, cb = xr.shape
    ng = tb // SUBLANES
    xr = xr.reshape(ng, SUBLANES, cb)
    xi = xi.reshape(ng, SUBLANES, cb)
    rid = lax.broadcasted_iota(jnp.int32, (1, SUBLANES, cb), 1)
    pr, pi_ = lr.reshape(1, 1, cb), li.reshape(1, 1, cb)
    powers = []
    for k in (1, 2, 4):
        powers.append((pr, pi_))
        shift = SUBLANES - k if reverse else k
        sr, si = pltpu.roll(xr, shift, 1), pltpu.roll(xi, shift, 1)
        keep = (rid < SUBLANES - k) if reverse else (rid >= k)
        tr_, ti_ = _cmul(jnp.where(keep, pr, 0.0), jnp.where(keep, pi_, 0.0), sr, si)
        xr = xr + tr_
        xi = xi + ti_
        pr, pi_ = _cmul(pr, pi_, pr, pi_)
    loc_r[...] = xr
    loc_i[...] = xi
    (p1r, p1i), (p2r, p2i), (p4r, p4i) = powers
    dist = lax.broadcasted_iota(jnp.int32, (SUBLANES, cb), 0)
    if reverse:
        dist = SUBLANES - 1 - dist
    wr = jnp.broadcast_to(p1r.reshape(1, cb), (SUBLANES, cb))
    wi = jnp.broadcast_to(p1i.reshape(1, cb), (SUBLANES, cb))
    for bit, (qr, qi) in ((1, (p1r, p1i)), (2, (p2r, p2i)), (4, (p4r, p4i))):
        mr, mi = _cmul(wr, wi, qr.reshape(1, cb), qi.reshape(1, cb))
        on = (dist & bit) != 0
        wr, wi = jnp.where(on, mr, wr), jnp.where(on, mi, wi)
    last = 0 if reverse else SUBLANES - 1

    def step(j, carry):
        cr, ci = carry
        g = (ng - 1 - j) if reverse else j
        fr = loc_r[g] + (wr * cr - wi * ci)
        fi = loc_i[g] + (wr * ci + wi * cr)
        rows = pl.ds(pl.multiple_of(g * SUBLANES, SUBLANES), SUBLANES)
        or_ref[rows, :] = fr
        oi_ref[rows, :] = fi
        return fr[last:last + 1, :], fi[last:last + 1, :]

    cr, ci = lax.fori_loop(0, ng, step, (carry_r[...], carry_i[...]))
    carry_r[...] = cr
    carry_i[...] = ci


def _scan_tiles(seq_len, n_ch):
    return min(256, seq_len), min(512, n_ch)


def _run_exchange(plan, step, n_steps, refs):
    @pl.when(step == 0)
    def _():
        plan.start(*refs)

    for part, at in plan.relay_steps(n_steps):
        @pl.when(step == at)
        def _(part=part):
            plan.relay(part, *refs)

    @pl.when(step == n_steps - 1)
    def _():
        plan.finish(*refs)


def _exchange_args(plan):
    if plan is None:
        return [], [], []
    return list(plan.ins), list(plan.out_shape), list(plan.sems)


def _split_refs(refs, *counts):
    groups, at = [], 0
    for n in counts:
        groups.append(refs[at:at + n])
        at += n
    return groups + [refs[at:]]


def ssm_scan(x_re, x_im, lam_re, lam_im, batch, name, exchange=None):
    n, nch = x_re.shape
    seq = n // batch
    tb, cb = _scan_tiles(seq, nch)
    nt, nc = seq // tb, nch // cb
    ex_ins, ex_out, ex_sems = _exchange_args(exchange)

    def body(*refs):
        ins, xin, outs, xout, scratch, xsems = _split_refs(refs, 4, len(ex_ins), 2, len(ex_out), 6)
        xr_ref, xi_ref, lr_ref, li_ref = ins
        or_ref, oi_ref = outs
        car_r, car_i, loc_r, loc_i, s_r, s_i = scratch
        if exchange is not None:
            step = (pl.program_id(0) * batch + pl.program_id(1)) * nt + pl.program_id(2)
            _run_exchange(exchange, step, nc * batch * nt, (xin, xout, xsems))

        @pl.when(pl.program_id(2) == 0)
        def _():
            car_r[...] = jnp.zeros_like(car_r)
            car_i[...] = jnp.zeros_like(car_i)

        _scan_block(xr_ref[...].astype(F32), xi_ref[...].astype(F32), lr_ref[...], li_ref[...], car_r, car_i,
                    s_r, s_i, loc_r, loc_i, reverse=False)
        or_ref[...] = s_r[...].astype(or_ref.dtype)
        oi_ref[...] = s_i[...].astype(oi_ref.dtype)

    blk = pl.BlockSpec((tb, cb), lambda c, b, t: (b * nt + t, c))
    lam_spec = pl.BlockSpec((1, cb), lambda c, b, t: (0, c))
    res = pl.pallas_call(
        body, name=name,
        grid=(nc, batch, nt),
        in_specs=[blk, blk, lam_spec, lam_spec] + [ANY] * len(ex_ins),
        out_specs=[blk, blk] + [ANY] * len(ex_out),
        out_shape=[jax.ShapeDtypeStruct((n, nch), BF16)] * 2 + ex_out,
        scratch_shapes=[pltpu.VMEM((1, cb), F32)] * 2 + [pltpu.VMEM((tb // SUBLANES, SUBLANES, cb), F32)] * 2
        + [pltpu.VMEM((tb, cb), F32)] * 2 + ex_sems,
        compiler_params=_params("arbitrary", "arbitrary", "arbitrary"),
    )(x_re, x_im, lam_re, lam_im, *ex_ins)
    return res[0], res[1], list(res[2:])


def ssm_scan_bwd(d_re, d_im, s_re, s_im, lam_re, lam_im, batch, name, exchange=None):
    n, nch = d_re.shape
    seq = n // batch
    tb, cb = _scan_tiles(seq, nch)
    nt, nc = seq // tb, nch // cb
    halo_rows = 2 * SUBLANES
    hb = tb // halo_rows
    ex_ins, ex_out, ex_sems = _exchange_args(exchange)

    def body(*refs):
        ins, xin, outs, xout, scratch, xsems = _split_refs(refs, 8, len(ex_ins), 4, len(ex_out), 6)
        xr_ref, xi_ref, sr_ref, si_ref, hr_ref, hi_ref, lr_ref, li_ref = ins
        or_ref, oi_ref, dlr_ref, dli_ref = outs
        car_r, car_i, loc_r, loc_i, g_r, g_i = scratch
        b, t = pl.program_id(1), pl.program_id(2)
        if exchange is not None:
            step = (pl.program_id(0) * batch + b) * nt + t
            _run_exchange(exchange, step, nc * batch * nt, (xin, xout, xsems))

        @pl.when(t == 0)
        def _():
            car_r[...] = jnp.zeros_like(car_r)
            car_i[...] = jnp.zeros_like(car_i)

        _scan_block(xr_ref[...].astype(F32), xi_ref[...].astype(F32), lr_ref[...], -li_ref[...], car_r, car_i,
                    g_r, g_i, loc_r, loc_i, reverse=True)
        gr, gi = g_r[...], g_i[...]
        or_ref[...] = gr.astype(or_ref.dtype)
        oi_ref[...] = gi.astype(oi_ref.dtype)
        first_block = t == nt - 1
        row = lax.broadcasted_iota(jnp.int32, (tb, cb), 0)
        hr = jnp.where(first_block, 0.0, hr_ref[...].astype(F32)[halo_rows - 1:halo_rows, :])
        hi = jnp.where(first_block, 0.0, hi_ref[...].astype(F32)[halo_rows - 1:halo_rows, :])
        pr = jnp.where(row == 0, hr, pltpu.roll(sr_ref[...].astype(F32), 1, 0))
        pi_ = jnp.where(row == 0, hi, pltpu.roll(si_ref[...].astype(F32), 1, 0))
        dlr = jnp.sum(gr * pr + gi * pi_, axis=0, keepdims=True)
        dli = jnp.sum(gi * pr - gr * pi_, axis=0, keepdims=True)
        start = jnp.logical_and(b == 0, t == 0)

        @pl.when(start)
        def _():
            dlr_ref[...] = dlr
            dli_ref[...] = dli

        @pl.when(jnp.logical_not(start))
        def _():
            dlr_ref[...] += dlr
            dli_ref[...] += dli

    def blk(c, b, t):
        return b * nt + (nt - 1 - t)

    st_spec = pl.BlockSpec((tb, cb), lambda c, b, t: (blk(c, b, t), c))
    halo_spec = pl.BlockSpec((halo_rows, cb), lambda c, b, t: (jnp.maximum(blk(c, b, t) * hb - 1, 0), c))
    row_spec = pl.BlockSpec((1, cb), lambda c, b, t: (0, c))
    res = pl.pallas_call(
        body, name=name,
        grid=(nc, batch, nt),
        in_specs=[st_spec] * 4 + [halo_spec] * 2 + [row_spec] * 2 + [ANY] * len(ex_ins),
        out_specs=[st_spec, st_spec, row_spec, row_spec] + [ANY] * len(ex_out),
        out_shape=[jax.ShapeDtypeStruct((n, nch), BF16)] * 2 + [jax.ShapeDtypeStruct((1, nch), F32)] * 2 + ex_out,
        scratch_shapes=[pltpu.VMEM((1, cb), F32)] * 2 + [pltpu.VMEM((tb // SUBLANES, SUBLANES, cb), F32)] * 2
        + [pltpu.VMEM((tb, cb), F32)] * 2 + ex_sems,
        compiler_params=_params("arbitrary", "arbitrary", "arbitrary"),
    )(d_re, d_im, s_re, s_im, s_re, s_im, lam_re, lam_im, *ex_ins)
    return res[0], res[1], res[2], res[3], list(res[4:])


def _pool_tiles(seq_len):
    return min(512, seq_len)


def _window_sums(x, n_steps, forward_in_time):
    rows = x.shape[0]
    k = 1
    for _ in range(n_steps):
        x = x + pltpu.roll(x, k if forward_in_time else rows - k, 0)
        k *= 2
    return x


def pool_fwd(u, w_pool, scale, batch, name):
    n, c = u.shape
    seq = n // batch
    tb = _pool_tiles(seq)
    nt = seq // tb
    gc = c // len(POOL_WINDOWS)
    hb = tb // POOL_HALO

    def body(x_ref, halo_ref, w_ref, sc_ref, y_ref, q_ref):
        t = pl.program_id(1)
        halo = jnp.where(t == 0, 0.0, halo_ref[...])
        full = jnp.concatenate([halo, x_ref[...]], axis=0)
        pos = lax.broadcasted_iota(jnp.int32, (tb, gc), 0) + t * tb + 1
        for gi, win in enumerate(POOL_WINDOWS):
            cols = slice(gi * gc, (gi + 1) * gc)
            sums = _window_sums(full[:, cols], gi + 1, True)[POOL_HALO:, :]
            cnt = jnp.minimum(pos, win).astype(F32)
            q = sums / cnt - x_ref[:, cols]
            r = jnp.dot(q.astype(BF16), w_ref[gi].astype(BF16), preferred_element_type=F32)
            q_ref[:, cols] = q.astype(q_ref.dtype)
            y_ref[:, cols] = (r * sc_ref[:, cols]).astype(y_ref.dtype)

    return pl.pallas_call(
        body, name=name,
        grid=(batch, nt),
        in_specs=[pl.BlockSpec((tb, c), lambda b, t: (b * nt + t, 0)),
                  pl.BlockSpec((POOL_HALO, c), lambda b, t: (jnp.maximum((b * nt + t) * hb - 1, 0), 0)),
                  pl.BlockSpec(w_pool.shape, lambda b, t: (0, 0, 0)),
                  pl.BlockSpec((1, c), lambda b, t: (0, 0))],
        out_specs=[pl.BlockSpec((tb, c), lambda b, t: (b * nt + t, 0))] * 2,
        out_shape=[jax.ShapeDtypeStruct((n, c), BF16)] * 2,
        compiler_params=_params("parallel", "arbitrary"),
    )(u, u, w_pool, scale)


def pool_bwd(dy, q, w_pool, scale, batch, name):
    n, c = dy.shape
    seq = n // batch
    tb = _pool_tiles(seq)
    nt = seq // tb
    ng = len(POOL_WINDOWS)
    gc = c // ng
    hb = tb // POOL_HALO
    n_blocks = n // POOL_HALO

    def body(dy_ref, dyh_ref, q_ref, w_ref, sc_ref, du_ref, dw_ref, dsc_ref):
        b, t = pl.program_id(0), pl.program_id(1)
        last = t == nt - 1
        dy_full = jnp.concatenate([dy_ref[...], jnp.where(last, 0.0, dyh_ref[...])], axis=0)
        pos = lax.broadcasted_iota(jnp.int32, (tb + POOL_HALO, gc), 0) + t * tb + 1
        start = jnp.logical_and(b == 0, t == 0)
        for gi, win in enumerate(POOL_WINDOWS):
            cols = slice(gi * gc, (gi + 1) * gc)
            w = w_ref[gi].astype(BF16)
            dr = dy_full[:, cols] * sc_ref[:, cols]
            dq = lax.dot_general(dr.astype(BF16), w, (((1,), (1,)), ((), ())), preferred_element_type=F32)
            cnt = jnp.minimum(pos, win).astype(F32)
            back = _window_sums(dq / cnt, gi + 1, False)
            du_ref[:, cols] = back[:tb, :] - dq[:tb, :]
            qb = q_ref[:, cols]
            r = jnp.dot(qb, w, preferred_element_type=F32)
            dw = lax.dot_general(qb, dr[:tb, :].astype(BF16), (((0,), (0,)), ((), ())), preferred_element_type=F32)
            dsc = jnp.sum(dy_ref[:, cols] * r, axis=0, keepdims=True)

            @pl.when(start)
            def _(gi=gi, cols=cols, dw=dw, dsc=dsc):
                dw_ref[gi] = dw
                dsc_ref[:, cols] = dsc

            @pl.when(jnp.logical_not(start))
            def _(gi=gi, cols=cols, dw=dw, dsc=dsc):
                dw_ref[gi] += dw
                dsc_ref[:, cols] += dsc

    blk = pl.BlockSpec((tb, c), lambda b, t: (b * nt + t, 0))
    halo = pl.BlockSpec((POOL_HALO, c), lambda b, t: (jnp.minimum((b * nt + t + 1) * hb, n_blocks - 1), 0))
    return pl.pallas_call(
        body, name=name,
        grid=(batch, nt),
        in_specs=[blk, halo, blk,
                  pl.BlockSpec(w_pool.shape, lambda b, t: (0, 0, 0)),
                  pl.BlockSpec((1, c), lambda b, t: (0, 0))],
        out_specs=[blk, pl.BlockSpec(w_pool.shape, lambda b, t: (0, 0, 0)), pl.BlockSpec((1, c), lambda b, t: (0, 0))],
        out_shape=[jax.ShapeDtypeStruct((n, c), F32), jax.ShapeDtypeStruct(w_pool.shape, F32),
                   jax.ShapeDtypeStruct((1, c), F32)],
        compiler_params=_params("arbitrary", "arbitrary"),
    )(dy, dy, q, w_pool, scale)


def _place():
    return lax.axis_index("x"), lax.axis_index("y"), lax.axis_index("c")


class GatherPlan:
    def __init__(self, arrs):
        self.ins = list(arrs)
        na = len(arrs)
        self.out_shape = [jax.ShapeDtypeStruct((N_DEV,) + a.shape, a.dtype) for a in arrs]
        self.sems = [pltpu.SemaphoreType.DMA((na, 7)), pltpu.SemaphoreType.DMA((na, 7)), pltpu.SemaphoreType.DMA((na,))]
        self.sizes = [math.prod(a.shape) * a.dtype.itemsize for a in arrs]

    def relay_steps(self, n_steps):
        total, done, steps = sum(self.sizes), 0, []
        for a, size in enumerate(self.sizes):
            done += size
            steps.append((a, min(n_steps - 1, (done * (n_steps - 1)) // total)))
        return steps

    def _copy(self, outs, sems, a, k, block, to, src=None):
        dst = outs[a].at[4 * block[0] + 2 * block[1] + block[2]]
        return pltpu.make_async_remote_copy(
            src_ref=dst if src is None else src, dst_ref=dst,
            send_sem=sems[0].at[a, k], recv_sem=sems[1].at[a, k], device_id=to, device_id_type=MESH)

    @staticmethod
    def _chips(x, y):
        return [(1 - x, y), (x, 1 - y), (1 - x, 1 - y)]

    def _local(self, ins, outs, sems, a, me):
        return pltpu.make_async_copy(ins[a], outs[a].at[4 * me[0] + 2 * me[1] + me[2]], sems[2].at[a])

    def start(self, ins, outs, sems):
        x, y, c = _place()
        me = (x, y, c)
        for a in range(len(ins)):
            self._local(ins, outs, sems, a, me).start()
            self._copy(outs, sems, a, 0, me, (x, y, 1 - c), src=ins[a]).start()
            for j, chip in enumerate(self._chips(x, y)):
                self._copy(outs, sems, a, 1 + j, me, (*chip, c), src=ins[a]).start()

    def relay(self, a, ins, outs, sems):
        x, y, c = _place()
        for j, chip in enumerate(self._chips(x, y)):
            self._copy(outs, sems, a, 1 + j, (*chip, c), (x, y, c)).wait_recv()
            self._copy(outs, sems, a, 4 + j, (*chip, c), (x, y, 1 - c)).start()

    def finish(self, ins, outs, sems):
        x, y, c = _place()
        me, sibling = (x, y, c), (x, y, 1 - c)
        for a in range(len(ins)):
            self._copy(outs, sems, a, 0, sibling, me).wait_recv()
            for j, chip in enumerate(self._chips(x, y)):
                self._copy(outs, sems, a, 4 + j, (*chip, 1 - c), me).wait_recv()
        for a in range(len(ins)):
            self._copy(outs, sems, a, 0, me, sibling, src=ins[a]).wait_send()
            for j, chip in enumerate(self._chips(x, y)):
                self._copy(outs, sems, a, 1 + j, me, (*chip, c), src=ins[a]).wait_send()
                self._copy(outs, sems, a, 4 + j, (*chip, c), sibling).wait_send()
            self._local(ins, outs, sems, a, me).wait()


class ChipScatterPlan:
    def __init__(self, arrs):
        self.groups = [list(a) if isinstance(a, list) else [a] for a in arrs]
        self.ins = [piece for group in self.groups for piece in group]
        self.first = [sum(len(g) for g in self.groups[:a]) for a in range(len(self.groups))]
        na = len(arrs)
        self.out_shape = [jax.ShapeDtypeStruct((4,) + g[0].shape[1:], g[0].dtype) for g in self.groups]
        self.sems = [pltpu.SemaphoreType.DMA((na, 3)), pltpu.SemaphoreType.DMA((na, 3)), pltpu.SemaphoreType.DMA((na,))]

    def relay_steps(self, n_steps):
        return []

    def _row(self, ins, a, px, py):
        if len(self.groups[a]) == 1:
            return ins[self.first[a]].at[2 * px + py]
        return ins[self.first[a] + px].at[py]

    def start(self, ins, outs, sems):
        x, y, c = _place()
        mine = 2 * x + y
        for xs in (0, 1):
            @pl.when(x == xs)
            def _(xs=xs):
                for a in range(len(self.groups)):
                    pltpu.make_async_copy(self._row(ins, a, xs, y), outs[a].at[mine], sems[2].at[a]).start()
                    for j, (px, py) in enumerate([(1 - xs, y), (xs, 1 - y), (1 - xs, 1 - y)]):
                        pltpu.make_async_remote_copy(
                            src_ref=self._row(ins, a, px, py), dst_ref=outs[a].at[mine],
                            send_sem=sems[0].at[a, j], recv_sem=sems[1].at[a, j],
                            device_id=(px, py, c), device_id_type=MESH).start()

    def finish(self, ins, outs, sems):
        x, y, c = _place()
        for wait_recv in (True, False):
            for a in range(len(self.groups)):
                for j in range(3):
                    cp = pltpu.make_async_remote_copy(
                        src_ref=self._row(ins, a, 0, 0), dst_ref=outs[a].at[0],
                        send_sem=sems[0].at[a, j], recv_sem=sems[1].at[a, j],
                        device_id=(x, y, c), device_id_type=MESH)
                    if wait_recv:
                        cp.wait_recv()
                    else:
                        cp.wait_send()
        for a in range(len(self.groups)):
            pltpu.make_async_copy(self._row(ins, a, 0, 0), outs[a].at[0], sems[2].at[a]).wait()


def run_exchange(plan, name):
    n_in, n_out = len(plan.ins), len(plan.out_shape)

    def body(*refs):
        parts = (refs[:n_in], refs[n_in:n_in + n_out], refs[n_in + n_out:])
        plan.start(*parts)
        for part, _ in plan.relay_steps(1):
            plan.relay(part, *parts)
        plan.finish(*parts)

    return pl.pallas_call(
        body, name=name,
        in_specs=[ANY] * n_in, out_specs=[ANY] * n_out,
        out_shape=plan.out_shape, scratch_shapes=plan.sems,
    )(*plan.ins)


class SiblingSwapPlan:
    def __init__(self, arrs):
        self.ins = list(arrs)
        na = len(arrs)
        self.out_shape = [jax.ShapeDtypeStruct(a.shape, a.dtype) for a in arrs]
        self.sems = [pltpu.SemaphoreType.DMA((na,)), pltpu.SemaphoreType.DMA((na,))]

    def relay_steps(self, n_steps):
        return []

    @staticmethod
    def _copies(ins, outs, sems):
        x, y, c = _place()
        return [pltpu.make_async_remote_copy(
            src_ref=ins[a], dst_ref=outs[a], send_sem=sems[0].at[a], recv_sem=sems[1].at[a],
            device_id=(x, y, 1 - c), device_id_type=MESH) for a in range(len(ins))]

    def start(self, ins, outs, sems):
        for cp in self._copies(ins, outs, sems):
            cp.start()

    def finish(self, ins, outs, sems):
        for cp in self._copies(ins, outs, sems):
            cp.wait()


def adamw(w, gparts, m, v, name, tr=256):
    rows, cols = w.shape
    parts = gparts.shape[0]
    tr = _row_tile(rows, tr)
    c1 = 1.0 - ADAM_B1 ** ADAM_STEP
    c2 = 1.0 - ADAM_B2 ** ADAM_STEP

    def body(w_ref, g_ref, m_ref, v_ref, go_ref, d_ref, mo_ref, vo_ref):
        g = g_ref[0].astype(F32)
        for p_ in range(1, parts):
            g = g + g_ref[p_].astype(F32)
        m_new = ADAM_B1 * m_ref[...] + (1.0 - ADAM_B1) * g
        v_new = ADAM_B2 * v_ref[...] + (1.0 - ADAM_B2) * (g * g)
        m_hat = m_new / c1
        v_hat = v_new / c2
        go_ref[...] = g
        d_ref[...] = -ADAM_LR * (m_hat / (jnp.sqrt(v_hat) + ADAM_EPS) + ADAM_WD * w_ref[...])
        mo_ref[...] = m_new
        vo_ref[...] = v_new

    blk = pl.BlockSpec((tr, cols), lambda i: (i, 0))
    return pl.pallas_call(
        body, name=name,
        grid=(rows // tr,),
        in_specs=[blk, pl.BlockSpec((parts, tr, cols), lambda i: (0, i, 0)), blk, blk],
        out_specs=[blk] * 4,
        out_shape=[jax.ShapeDtypeStruct((rows, cols), F32)] * 4,
        compiler_params=_params("parallel"),
    )(w, gparts, m, v)


def add2(a, b, name, out_dtype):
    return rowwise(lambda a, b: ([a.astype(F32) + b.astype(F32)], []), [a, b], [(a.shape[1], out_dtype)],
                   name=name, tr=256)[0]


def _block_diag(x):
    g, a, b = x.shape
    tiled = jnp.tile(x.reshape(g * a, b), (1, g))
    rows = lax.broadcasted_iota(jnp.int32, (g * a, g * b), 0) // a
    cols = lax.broadcasted_iota(jnp.int32, (g * a, g * b), 1) // b
    return jnp.where(rows == cols, tiled, jnp.zeros_like(tiled))


def _diag_blocks(x, a, b):
    per = x.shape[1] // b
    x5 = x.reshape(SSM_SUPER, per, a, per, b)
    eye = jnp.eye(per, dtype=x.dtype)
    return jnp.sum(x5 * eye[None, :, None, :, None], axis=3).reshape(SSM_SUPER * per, a, b)


def _swiglu_epi(g, u):
    s = _sigmoid(g)
    silu = g * s
    return u * (s * (1.0 + g * (1.0 - s))), silu, silu * u


def _residual_epi(scale, with_norm):
    if with_norm:
        def epi(acc, res, gain):
            out = res + scale * acc
            return out, _rms_tile(out, gain)
    else:
        def epi(acc, res):
            return (res + scale * acc,)
    return epi


def _next_norm(next_gain):
    if next_gain is None:
        return [], (F32,)
    return [next_gain], (F32, BF16)


FFN_WIDE = 2816


def ffn_fwd(h, n, wi, wo, next_gain, tag, carry=None):
    if carry is None:
        carry = lambda where, run: run(None)[0]
    dact_g, dact_u, act = carry(f"{tag}_in", lambda ex: _with_exchange(matmul(
        n, [(wi, 0), (wi, 1)], mode="nn", name=f"{tag}_in", separate=True, epi=_swiglu_epi,
        out_dtypes=(BF16, BF16, BF16), exchange=ex, tm=512, tn=FFN_WIDE), ex))
    more, dtypes = _next_norm(next_gain)
    res = carry(f"{tag}_out", lambda ex: _with_exchange(matmul(
        act, wo, mode="nn", name=f"{tag}_out", epi=_residual_epi(0.5, bool(more)), extras=[h] + more,
        out_dtypes=dtypes, exchange=ex, tm=512, tk=FFN_WIDE), ex))
    return res[0], (res[1] if more else None), (h, n, dact_g, dact_u, act)


def ffn_bwd(dh, saved, gain, wi, wo, tag, carry=None, gate=None, on_weight_grads=None):
    h, n, dact_g, dact_u, act = saved
    if carry is None:
        carry = lambda where, run: run(None)[0]
    dg, du = carry(f"{tag}_out_dx", lambda ex: _with_exchange(matmul(
        dh, wo, mode="nt", name=f"{tag}_out_dx", extras=[dact_g, dact_u], out_dtypes=(BF16, BF16),
        epi=lambda acc, fg, fu: (0.5 * acc * fg, 0.5 * acc * fu), exchange=ex, tm=512, tn=FFN_WIDE), ex))
    d_wo, = carry(f"{tag}_out_dw", lambda ex: _with_exchange(matmul(
        act, dh, mode="tn", name=f"{tag}_out_dw", scale=0.5, exchange=ex, tm=FFN_WIDE), ex))
    d_wi = [carry(f"{tag}_in_dw_{k}", lambda ex, half=half: _with_exchange(matmul(
        n, half, mode="tn", name=f"{tag}_in_dw", exchange=ex, tn=FFN_WIDE), ex))[0] for k, half in (("g", dg), ("u", du))]
    if on_weight_grads is not None:
        on_weight_grads(d_wi, d_wo)
    if gate is None:
        epi, more, dtypes = _rms_bwd_tile, [], (F32,)
    else:
        def epi(acc, x, g, dres, e, pre):
            dh_new, d_gain = _rms_bwd_tile(acc, x, g, dres)
            return (dh_new, *_gate_cotangents(dh_new, e, pre), d_gain)
        more, dtypes = list(gate), (F32, BF16, BF16)
    res = carry(f"{tag}_in_dx", lambda ex: _with_exchange(matmul(
        [dg, du], [(wi, 0), (wi, 1)], mode="nt", name=f"{tag}_in_dx", epi=epi, extras=[h, gain, dh] + more,
        out_dtypes=dtypes, col_sums=1, exchange=ex, tm=256, tk=FFN_WIDE), ex))
    return res[0], res[-1], d_wi, d_wo, tuple(res[1:-1])


def _with_exchange(result, exchange):
    return result if exchange is not None else (result, [])


def ssm_tables(rep):
    rows = DEPTH * SSM_CH
    lam_r, lam_i, bb_r, bb_i = ssm_prep(
        rep["ssm_lambda_re"].reshape(rows, 1), rep["ssm_lambda_im"].reshape(rows, 1),
        jnp.repeat(rep["ssm_log_dt"].reshape(-1), SSM_STATE).reshape(rows, 1),
        rep["ssm_b_re"].reshape(rows, SSM_GROUP_CH), rep["ssm_b_im"].reshape(rows, SSM_GROUP_CH), "ssm_zoh")
    lam = (lam_r.reshape(DEPTH, 1, SSM_CH), lam_i.reshape(DEPTH, 1, SSM_CH))

    def dense(x):
        return jax.vmap(_block_diag)(x.astype(BF16))

    b_mats = [dense(bb.reshape(DEPTH, SSM_GROUPS, SSM_STATE, SSM_GROUP_CH).transpose(0, 1, 3, 2)) for bb in (bb_r, bb_i)]
    c_mats = [dense(cc.transpose(0, 1, 3, 2)) for cc in (rep["ssm_c_re"], -rep["ssm_c_im"])]
    return lam, b_mats, c_mats


def mix_fwd(h, n, lw, tables, layer, batch, next_gain, tag, exchange=None):
    sw = h.shape[1] // 2
    us, up = matmul(n, [lw["w_in"][:, :sw], lw["w_in"][:, sw:]], mode="nn", name=f"{tag}_in", separate=True,
                    out_dtypes=(F32, F32))
    lam = tuple(t[layer] for t in tables[0])
    b_mats = [(m, layer) for m in tables[1]]
    c_mats = [(m, layer) for m in tables[2]]
    bu_re, bu_im = matmul(us, b_mats, mode="nn", name=f"{tag}_bu", separate=True, diag=SSM_SUPER,
                          out_dtypes=(BF16, BF16))
    s_re, s_im, exchanged = ssm_scan(bu_re, bu_im, *lam, batch, f"{tag}_scan", exchange)
    y0, y1 = matmul([s_re, s_im], c_mats, mode="nn", name=f"{tag}_c", diag=SSM_SUPER,
                    epi=lambda acc, u, d: (acc + d * u, _gelu(acc + d * u)), extras=[us, lw["ssm_d"]],
                    out_dtypes=(F32, BF16))
    y2, gl = matmul(y1, lw["w_glu"], mode="nn", name=f"{tag}_glu",
                    epi=lambda acc, y0: (_gelu(y0) * _sigmoid(acc), acc), extras=[y0], out_dtypes=(BF16, F32))
    yp, q = pool_fwd(up, lw["pool_w"], lw["pool_scale"], batch, f"{tag}_pool")
    more, dtypes = _next_norm(next_gain)
    res = matmul([y2, yp], [lw["w_out"][:sw], lw["w_out"][sw:]], mode="nn", name=f"{tag}_out",
                 epi=_residual_epi(1.0, bool(more)), extras=[h] + more, out_dtypes=dtypes)
    saved = (h, n, us, lam, b_mats, c_mats, s_re, s_im, y0, y1, gl, y2, yp, q)
    return res[0], (res[1] if more else None), saved, exchanged


def mix_bwd(dh, saved, lw, batch, tag, exchange=None):
    h, n, us, lam, b_mats, c_mats, s_re, s_im, y0, y1, gl, y2, yp, q = saved
    sw = h.shape[1] // 2
    w_out_s, w_out_p = lw["w_out"][:sw], lw["w_out"][sw:]
    d_wo_s, d_wo_p = matmul([y2, yp], dh, mode="tn", name=f"{tag}_out_dw", separate=True)
    def out_dx_epi(dy2, dyp, y0, gl):
        sg = _sigmoid(gl)
        return dy2, dyp, dy2 * _gelu(y0) * sg * (1.0 - sg)

    dy2, dyp, tg = matmul(dh, [w_out_s, w_out_p], mode="nt", name=f"{tag}_out_dx", separate=True, epi=out_dx_epi,
                          extras=[y0, gl], out_dtypes=(F32, F32, BF16))
    dup, d_pool_w, d_pool_scale = pool_bwd(dyp, q, lw["pool_w"], lw["pool_scale"], batch, f"{tag}_pool_bwd")
    def dy0_epi(acc, dy2, gl, y0, u):
        dy0 = (acc + dy2 * _sigmoid(gl)) * _gelu_grad(y0)
        return dy0, jnp.sum(dy0 * u, axis=0, keepdims=True)

    dy0, d_d = matmul(tg, lw["w_glu"], mode="nt", name=f"{tag}_glu_dx", epi=dy0_epi, extras=[dy2, gl, y0, us],
                      out_dtypes=(F32,), col_sums=1)
    d_w_glu, = matmul(y1, tg, mode="tn", name=f"{tag}_glu_dw")
    gd_re, gd_im = matmul(dy0, c_mats, mode="nt", name=f"{tag}_c_dx", separate=True, diag=SSM_SUPER,
                          out_dtypes=(BF16, BF16))
    d_c_top, d_c_bot = matmul([s_re, s_im], dy0, mode="tn", name=f"{tag}_c_dw", separate=True, diag=SSM_SUPER)
    g_re, g_im, d_lam_r, d_lam_i, exchanged = ssm_scan_bwd(gd_re, gd_im, s_re, s_im, *lam, batch, f"{tag}_scan_bwd",
                                                           exchange)
    dus, = matmul([g_re, g_im], b_mats, mode="nt", name=f"{tag}_bu_dx", diag=SSM_SUPER,
                  epi=lambda acc, dy0, d: (acc + d * dy0,), extras=[dy0, lw["ssm_d"]])
    d_b_re, d_b_im = matmul(us, [g_re, g_im], mode="tn", name=f"{tag}_bu_dw", separate=True, diag=SSM_SUPER)
    d_bb_r = _diag_blocks(d_b_re, SSM_GROUP_CH, SSM_STATE).transpose(0, 2, 1).reshape(SSM_CH, SSM_GROUP_CH)
    d_bb_i = _diag_blocks(d_b_im, SSM_GROUP_CH, SSM_STATE).transpose(0, 2, 1).reshape(SSM_CH, SSM_GROUP_CH)
    d_lr, d_li, d_ldt, d_br, d_bi = ssm_prep_bwd(
        lw["lam_re"], lw["lam_im"], lw["log_dt"], lw["b_re"], lw["b_im"],
        d_lam_r.reshape(SSM_CH, 1), d_lam_i.reshape(SSM_CH, 1), d_bb_r, d_bb_i, f"{tag}_zoh_bwd")
    d_c_re = _diag_blocks(d_c_top, SSM_STATE, SSM_GROUP_CH).transpose(0, 2, 1)
    d_c_im = -_diag_blocks(d_c_bot, SSM_STATE, SSM_GROUP_CH).transpose(0, 2, 1)
    d_w_in_s, d_w_in_p = matmul(n, [dus, dup], mode="tn", name=f"{tag}_in_dw", separate=True)
    dh_new, d_gain = matmul([dus, dup], [lw["w_in"][:, :sw], lw["w_in"][:, sw:]], mode="nt", name=f"{tag}_in_dx",
                            epi=_rms_bwd_tile, extras=[h, lw["mix_norm"], dh], out_dtypes=(F32,), col_sums=1, tm=512)
    grads = dict(mix_norm=d_gain, w_in=jnp.concatenate([d_w_in_s, d_w_in_p], axis=1),
                 ssm_lambda_re=d_lr, ssm_lambda_im=d_li, ssm_log_dt=d_ldt, ssm_b_re=d_br, ssm_b_im=d_bi,
                 ssm_c_re=d_c_re, ssm_c_im=d_c_im, ssm_d=d_d, ssm_w_glu=d_w_glu, pool_w=d_pool_w,
                 pool_scale=d_pool_scale, w_out=jnp.concatenate([d_wo_s, d_wo_p], axis=0))
    return dh_new, grads, exchanged


def ple_fwd(h, n, p, w_gate, w_proj, next_gain, tag):
    e, = matmul(p, w_proj, mode="nn", name=f"{tag}_proj")
    if next_gain is None:
        def epi(acc, e, res):
            return res + _sigmoid(acc) * e, acc
        more, dtypes = [], (F32, F32)
    else:
        def epi(acc, e, res, gain):
            out = res + _sigmoid(acc) * e
            return out, acc, _rms_tile(out, gain)
        more, dtypes = [next_gain], (F32, F32, BF16)
    res = matmul(n, w_gate, mode="nn", name=f"{tag}_gate", epi=epi, extras=[e, h] + more, out_dtypes=dtypes, tm=512)
    return res[0], (res[2] if more else None), (h, n, e, res[1])


def _gate_cotangents(dh, e, pre):
    s = _sigmoid(pre)
    return dh * e * s * (1.0 - s), dh * s


def ple_bwd(dh, dpre, de, saved, p, gain, w_gate, tag):
    h, n, e, pre = saved
    d_w_gate, = matmul(n, dpre, mode="tn", name=f"{tag}_gate_dw")
    d_w_proj, = matmul(p, de, mode="tn", name=f"{tag}_proj_dw")
    dh_new, d_gain = matmul(dpre, w_gate, mode="nt", name=f"{tag}_gate_dx", epi=_rms_bwd_tile,
                            extras=[h, gain, dh], out_dtypes=(F32,), col_sums=1, tm=512)
    return dh_new, d_gain, d_w_gate, d_w_proj


def loss_head(h, gain, target, e, pre, name):
    d = h.shape[1]

    def fn(h, g, t, e, pre):
        r = lax.rsqrt(jnp.mean(h * h, axis=-1, keepdims=True) + EPS)
        diff = h * r * g - t
        sq = jnp.sum(jnp.sum(diff * diff, axis=1, keepdims=True), axis=0, keepdims=True)
        dy = diff * (1.0 / d)
        w = dy * g
        dh = r * w - h * (r * r * r) * jnp.mean(h * w, axis=-1, keepdims=True)
        return [dh, *_gate_cotangents(dh, e, pre)], [sq, jnp.sum(dy * (h * r), axis=0, keepdims=True)]

    dh, dpre, de, sq, d_gain = rowwise(fn, [h, gain, target, e, pre], [(d, F32), (d, BF16), (d, BF16)],
                                       [(1, 1), (1, d)], name=name, tr=256)
    return 0.5 / d * sq[0, 0], dh, dpre, de, d_gain


SHARDED = {
    "ffn1_wi": 1, "ffn1_wo": 0, "w_in": 0, "ssm_w_glu": 0, "w_out": 0, "ffn2_wi": 1, "ffn2_wo": 0,
    "ple_w_gate": 0, "ple_w_proj": 1,
}
WEIGHTS = ["ffn1_norm", "ffn1_wi", "ffn1_wo", "mix_norm", "w_in", "ssm_lambda_re", "ssm_lambda_im", "ssm_log_dt",
           "ssm_b_re", "ssm_b_im", "ssm_c_re", "ssm_c_im", "ssm_d", "ssm_w_glu", "pool_w", "pool_scale", "w_out",
           "ffn2_norm", "ffn2_wi", "ffn2_wo", "ple_norm", "ple_w_gate", "ple_w_proj", "final_norm"]
REPLICATED = [n for n in WEIGHTS if n not in SHARDED]


HALVED = ("ffn1_wi", "ffn2_wi")


def _unshard(gathered, axis, halved):
    if halved:
        _, rows, cols = gathered.shape
        return gathered.reshape(2, 4, rows, cols).transpose(0, 2, 1, 3).reshape(2, rows, 4 * cols)
    g = jnp.moveaxis(gathered, 0, axis)
    shp = g.shape
    return g.reshape(shp[:axis] + (shp[axis] * shp[axis + 1],) + shp[axis + 2:])


def _split_for_scatter(full, axis, c, halved):
    if halved:
        rows, cols = full[0].shape

        def pick(cc):
            return [lax.dynamic_index_in_dim(h.reshape(rows, 2, 2, cols // 4), cc, 2, keepdims=False).transpose(1, 0, 2)
                    for h in full]

        return pick(c), [s.astype(BF16) for s in pick(1 - c)]
    shp = full.shape
    g = full.reshape(shp[:axis] + (4, 2, shp[axis] // N_DEV) + shp[axis + 1:])
    keep = lax.dynamic_index_in_dim(g, c, axis + 1, keepdims=False)
    send = lax.dynamic_index_in_dim(g, 1 - c, axis + 1, keepdims=False)
    return jnp.moveaxis(keep, axis, 0), jnp.moveaxis(send, axis, 0).astype(BF16)


def _pack(arrs):
    pieces = []
    for a in arrs:
        flat = a.reshape(-1)
        pad = (-flat.shape[0]) % PACK
        pieces.append(jnp.pad(flat, (0, pad)).reshape(-1, LANES))
    return jnp.concatenate(pieces, axis=0)


def _unpack(packed, shapes):
    out, row = [], 0
    for s in shapes:
        size = math.prod(s)
        rows = (size + PACK - 1) // PACK * SUBLANES
        out.append(packed[row:row + rows].reshape(-1)[:size].reshape(s))
        row += rows
    return out


def local_step(x, p, target, rep, hooks):
    batch, seq, d = x.shape
    n_tok = batch * seq
    h = x.reshape(n_tok, d)
    saved = []
    n = rms_fwd(h, rep["ffn1_norm"][0].reshape(1, d), "first_norm")
    tables = ssm_tables(rep)
    for i in range(DEPTH):
        lw = _layer_weights(rep, i, d)
        big = lambda name, i=i: hooks.big(i, name)
        next_gain = rep["ffn1_norm"][i + 1].reshape(1, d) if i + 1 < DEPTH else None
        def carry(where, run, i=i):
            outs, exchanged = run(hooks.fwd_exchange(i, where))
            hooks.fwd_done(i, where, exchanged)
            return outs

        h, n, s1 = ffn_fwd(h, n, big("ffn1_wi"), big("ffn1_wo"), lw["mix_norm"], "ffn1", carry)
        lw.update(w_in=big("w_in"), w_glu=big("ssm_w_glu"), w_out=big("w_out"))
        h, n, s2, exchanged = mix_fwd(h, n, lw, tables, i, batch, lw["ffn2_norm"], "mix", hooks.fwd_exchange(i, "scan"))
        hooks.fwd_done(i, "scan", exchanged)
        h, n, s3 = ffn_fwd(h, n, big("ffn2_wi"), big("ffn2_wo"), lw["ple_norm"], "ffn2", carry)
        p_i = p[i].reshape(n_tok, -1)
        h, n, s4 = ple_fwd(h, n, p_i, big("ple_w_gate"), big("ple_w_proj"), next_gain, "ple")
        saved.append((s1, s2, s3, s4, p_i))
    last_gate = saved[-1][3][2:]
    loss, dh, dpre, de, d_final = loss_head(h, rep["final_norm"].reshape(1, d), target.reshape(n_tok, d), *last_gate,
                                            "loss_head")
    hooks.final_grad(d_final)
    per_layer = [None] * DEPTH
    for i in reversed(range(DEPTH)):
        lw = _layer_weights(rep, i, d)
        big = lambda name, i=i: hooks.big(i, name)
        lw.update(w_in=big("w_in"), w_glu=big("ssm_w_glu"), w_out=big("w_out"))
        s1, s2, s3, s4, p_i = saved[i]
        g = {}
        dh, g["ple_norm"], g["ple_w_gate"], g["ple_w_proj"] = ple_bwd(dh, dpre, de, s4, p_i, lw["ple_norm"],
                                                                     big("ple_w_gate"), "ple")
        def carry(where, run, i=i):
            outs, exchanged = run(hooks.bwd_exchange(i, where))
            hooks.bwd_done(i, where, exchanged)
            return outs

        dh, g["ffn2_norm"], g["ffn2_wi"], g["ffn2_wo"], _ = ffn_bwd(
            dh, s3, lw["ffn2_norm"], big("ffn2_wi"), big("ffn2_wo"), "ffn2", carry)
        dh, gm, exchanged = mix_bwd(dh, s2, lw, batch, "mix", hooks.bwd_exchange(i, "scan"))
        hooks.bwd_done(i, "scan", exchanged)
        g.update(gm)
        hooks.layer_grads(i, g, [k for k in SHARDED if k not in LAST_GRADS])
        hooks.small_grads(i, {k: g[k] for k in REPLICATED if k in g})
        below = saved[i - 1][3][2:] if i > 0 else None

        def ffn1_grads(d_wi, d_wo, i=i, g=g):
            g["ffn1_wi"], g["ffn1_wo"] = d_wi, d_wo
            hooks.layer_grads(i, g, list(LAST_GRADS))

        dh, g["ffn1_norm"], _, _, gate_ct = ffn_bwd(
            dh, s1, lw["ffn1_norm"], big("ffn1_wi"), big("ffn1_wo"), "ffn1", carry, below, ffn1_grads)
        hooks.small_grads(i, {"ffn1_norm": g["ffn1_norm"]})
        if i > 0:
            dpre, de = gate_ct
        per_layer[i] = g
    return loss, dh.reshape(batch, seq, d), per_layer, d_final


LAST_GRADS = ("ffn1_wi", "ffn1_wo")


def _layer_weights(w, i, d):
    sw = d // 2
    lw = {}
    lw["pool_w"] = w["pool_w"][i]
    for k in ("ffn1_norm", "mix_norm", "ffn2_norm", "ple_norm"):
        lw[k] = w[k][i].reshape(1, d)
    lw["ssm_d"] = w["ssm_d"][i].reshape(1, sw)
    lw["pool_scale"] = w["pool_scale"][i].reshape(1, sw)
    lw["lam_re"] = w["ssm_lambda_re"][i].reshape(SSM_CH, 1)
    lw["lam_im"] = w["ssm_lambda_im"][i].reshape(SSM_CH, 1)
    lw["log_dt"] = jnp.repeat(w["ssm_log_dt"][i], SSM_STATE).reshape(SSM_CH, 1)
    lw["b_re"] = w["ssm_b_re"][i].reshape(SSM_CH, SSM_GROUP_CH)
    lw["b_im"] = w["ssm_b_im"][i].reshape(SSM_CH, SSM_GROUP_CH)
    lw["c_re"] = w["ssm_c_re"][i]
    lw["c_im"] = w["ssm_c_im"][i]
    return lw


class MeshExchange:
    FIRST = ("ffn1_wi", "ffn1_wo")
    FIRST_LAYER_PLAN = {"ffn1_in": ("w_in", "ssm_w_glu", "w_out", "ffn2_wi", "ffn2_wo"),
                        "ffn1_out": ("ple_w_gate", "ple_w_proj")}
    FWD_PLAN = {"scan": ("ffn1_wi", "ffn1_wo", "w_in", "ssm_w_glu", "w_out"),
                "ffn2_in": ("ffn2_wi", "ffn2_wo"), "ffn2_out": ("ple_w_gate", "ple_w_proj")}
    BWD_PLAN = {"ffn2_in_dx": ("ffn1_wo", "ffn2_wo", "ssm_w_glu", "w_out", "ple_w_gate", "ple_w_proj"),
                "scan": ("ffn1_wi", "ffn2_wi", "w_in")}
    LAST_LAYER_PLAN = {"ffn1_out_dx": ("ffn2_wi", "w_in"),
                       "ffn1_out_dw": ("ffn2_wo", "w_out", "ple_w_gate", "ssm_w_glu", "ple_w_proj"),
                       "ffn1_in_dx": LAST_GRADS}

    def __init__(self, shards):
        self.small = {}
        self.d_final = None
        self.small_parts = []
        self.small_rest = None
        self.swaps = {}
        self.shards = shards
        self.c = lax.axis_index("c")
        self.gathered = {}
        self.chip_sums = {}
        self.from_chips = {}
        self.pending = None
        first = run_exchange(GatherPlan([shards[0][k] for k in self.FIRST]), "gather_first_weights")
        self._store(self.gathered, 0, self.FIRST, first)

    @staticmethod
    def _store(where, layer, names, results):
        for k, r in zip(names, results):
            where[(layer, k)] = r

    def big(self, i, name):
        return _unshard(self.gathered[(i, name)], SHARDED[name], name in HALVED)

    def fwd_exchange(self, i, where):
        if i == 0 and where in self.FIRST_LAYER_PLAN:
            layer, names = 0, self.FIRST_LAYER_PLAN[where]
        elif where in self.FWD_PLAN and i + 1 < DEPTH:
            layer, names = i + 1, self.FWD_PLAN[where]
        else:
            return None
        self.pending = (layer, names)
        return GatherPlan([self.shards[layer][k] for k in names])

    def fwd_done(self, i, where, results):
        if results:
            self._store(self.gathered, *self.pending, results)

    def layer_grads(self, i, grads, names):
        pieces, sends = [], []
        for k in names:
            keep, send = _split_for_scatter(grads[k], SHARDED[k], self.c, k in HALVED)
            keep, send = (keep, send) if isinstance(keep, list) else ([keep], [send])
            pieces.append(keep)
            sends += send
        if i == 0:
            self._add_sibling(0, names, pieces, run_exchange(SiblingSwapPlan(sends), "reduce_core_pair"))
        else:
            self.swaps[(i, tuple(names) == LAST_GRADS)] = (names, pieces, sends)

    def _add_sibling(self, i, names, pieces, from_sibling):
        got = iter(from_sibling)
        for k, keep in zip(names, pieces):
            sums = []
            for part in keep:
                cols = part.shape[-1]
                sums.append(add2(part.reshape(-1, cols), next(got).reshape(-1, cols), f"sum_core_pair_{k}",
                                 BF16).reshape(part.shape))
            self.chip_sums[(i, k)] = sums if len(sums) > 1 else sums[0]

    def small_grads(self, i, grads):
        self.small.setdefault(i, {}).update(grads)

    def final_grad(self, d_final):
        self.d_final = d_final

    def _small_pack(self, d_final, first_norm_grad=None):
        pieces = []
        for k in REPLICATED:
            if k == "final_norm":
                pieces.append(d_final)
            elif k == "ffn1_norm" and first_norm_grad is None:
                pieces.append(jnp.stack([self.small[i][k] for i in range(1, DEPTH)], axis=0))
            else:
                pieces.append(jnp.stack([self.small[i][k] for i in range(DEPTH)], axis=0))
        return _pack(pieces)

    def bwd_exchange(self, i, where):
        if where in self.BWD_PLAN and i + 1 < DEPTH:
            layer, names = i + 1, self.BWD_PLAN[where]
        elif i == 0 and where in self.LAST_LAYER_PLAN:
            layer, names = 0, self.LAST_LAYER_PLAN[where]
        elif i == 0 and where in ("ffn1_in_dw_g", "ffn1_in_dw_u"):
            if where.endswith("g"):
                pack = self._small_pack(self.d_final)
                half = pack.shape[0] // 2 // SUBLANES * SUBLANES
                part, self.small_rest = pack[:half], pack[half:]
            else:
                part = self.small_rest
            self.pending = "small"
            return GatherPlan([part])
        elif i > 0 and where in ("ffn1_out_dx", "ffn1_in_dx"):
            self.pending = ("swap", i, where == "ffn1_in_dx")
            return SiblingSwapPlan(self.swaps[self.pending[1:]][2])
        else:
            return None
        self.pending = (layer, names)
        return ChipScatterPlan([self.chip_sums[(layer, k)] for k in names])

    def bwd_done(self, i, where, results):
        if not results:
            return
        if self.pending == "small":
            self.small_parts += results
        elif self.pending[0] == "swap":
            names, pieces, _ = self.swaps[self.pending[1:]]
            self._add_sibling(self.pending[1], names, pieces, results)
        else:
            self._store(self.from_chips, *self.pending, results)

    def small_gathered(self):
        first = self.small[0]["ffn1_norm"].reshape(SUBLANES, LANES)
        first_all, = run_exchange(GatherPlan([first]), "gather_first_gain_grad")
        return jnp.concatenate([first_all] + self.small_parts, axis=1)


def kernel(x, p, ffn1_norm, ffn1_wi, ffn1_wo, mix_norm, w_in, ssm_lambda_re, ssm_lambda_im, ssm_log_dt, ssm_b_re, ssm_b_im, ssm_c_re, ssm_c_im, ssm_d, ssm_w_glu, pool_w, pool_scale, w_out, ffn2_norm, ffn2_wi, ffn2_wo, ple_norm, ple_w_gate, ple_w_proj, final_norm, loss_target, m_ffn1_norm, m_ffn1_wi, m_ffn1_wo, m_mix_norm, m_w_in, m_ssm_lambda_re, m_ssm_lambda_im, m_ssm_log_dt, m_ssm_b_re, m_ssm_b_im, m_ssm_c_re, m_ssm_c_im, m_ssm_d, m_ssm_w_glu, m_pool_w, m_pool_scale, m_w_out, m_ffn2_norm, m_ffn2_wi, m_ffn2_wo, m_ple_norm, m_ple_w_gate, m_ple_w_proj, m_final_norm, v_ffn1_norm, v_ffn1_wi, v_ffn1_wo, v_mix_norm, v_w_in, v_ssm_lambda_re, v_ssm_lambda_im, v_ssm_log_dt, v_ssm_b_re, v_ssm_b_im, v_ssm_c_re, v_ssm_c_im, v_ssm_d, v_ssm_w_glu, v_pool_w, v_pool_scale, v_w_out, v_ffn2_norm, v_ffn2_wi, v_ffn2_wo, v_ple_norm, v_ple_w_gate, v_ple_w_proj, v_final_norm):
    args = dict(locals())
    wts = {k: args[k] for k in WEIGHTS}
    rep = {k: wts[k] for k in REPLICATED}

    shards = [{k: wts[k][i].astype(BF16) for k in SHARDED} for i in range(DEPTH)]
    exchange = MeshExchange(shards)
    loss_local, grad_x, per_layer, d_final = local_step(x, p, loss_target, rep, exchange)
    loss = lax.psum(loss_local, ("x", "y", "c"))

    outs = {}
    for k in SHARDED:
        shp = wts[k].shape
        cols = shp[-1]
        parts = jnp.stack([exchange.from_chips[(i, k)] for i in range(DEPTH)], axis=1)
        res = adamw(wts[k].reshape(-1, cols), parts.reshape(4, -1, cols), args["m_" + k].reshape(-1, cols),
                    args["v_" + k].reshape(-1, cols), f"adamw_{k}")
        outs[k] = [r.reshape(shp) for r in res]

    rep_shapes = [wts[k].shape for k in REPLICATED]
    all_g = exchange.small_gathered()
    res = adamw(_pack([wts[k] for k in REPLICATED]), all_g, _pack([args["m_" + k] for k in REPLICATED]),
                _pack([args["v_" + k] for k in REPLICATED]), "adamw_small")
    unpacked = [_unpack(r, rep_shapes) for r in res]
    for j, k in enumerate(REPLICATED):
        outs[k] = [unpacked[q][j] for q in range(4)]

    result = [loss, grad_x]
    for q in range(4):
        result += [outs[k][q] for k in WEIGHTS]
    return tuple(result)
```

```python
import math

import jax
import jax.numpy as jnp
from jax import lax
from jax.experimental import pallas as pl
from jax.experimental.pallas import tpu as pltpu

F32 = jnp.float32
BF16 = jnp.bfloat16
MESH = pl.DeviceIdType.MESH
ANY = pl.BlockSpec(memory_space=pl.ANY)

N_DEV = 8
DEPTH = 4
EPS = 1e-6
SSM_GROUPS = 32
SSM_GROUP_CH = 16
SSM_STATE = 64
SSM_CH = SSM_GROUPS * SSM_STATE
SSM_SUPER = 2
POOL_WINDOWS = (2, 4, 8, 16)
POOL_HALO = 16
ADAM_LR, ADAM_B1, ADAM_B2, ADAM_EPS, ADAM_WD, ADAM_STEP = 0.001, 0.9, 0.999, 1e-08, 0.01, 10

V7X_VMEM_BYTES = 64 * 1024 * 1024
VMEM_LIMIT_BYTES = V7X_VMEM_BYTES - 12 * 1024 * 1024
LANES = 128
SUBLANES = 8
PACK = SUBLANES * LANES


def _params(*sem):
    return pltpu.CompilerParams(dimension_semantics=sem or None, vmem_limit_bytes=VMEM_LIMIT_BYTES)


def _tile(n, pref):
    if n <= pref:
        return n
    t = pref - pref % LANES
    while t >= LANES:
        if n % t == 0:
            return t
        t -= LANES
    raise ValueError(f"no lane-aligned tile for {n}")


def _row_tile(rows, pref):
    if rows <= pref:
        return rows
    t = pref - pref % SUBLANES
    while t >= SUBLANES:
        if rows % t == 0:
            return t
        t -= SUBLANES
    raise ValueError(f"no sublane-aligned tile for {rows}")


_DIMS = {"nn": ((1,), (0,)), "nt": ((1,), (1,)), "tn": ((0,), (0,))}


def matmul(a, b, *, mode, name, out_dtypes=None, epi=None, extras=(), separate=False, diag=1, col_sums=0,
           stack_out=False, scale=1.0, exchange=None, single_buffer_weights=False, tm=1024, tn=1024, tk=1024):
    a_list = list(a) if isinstance(a, list) else [a]
    b_list = list(b) if isinstance(b, list) else [b]
    a_planes = [x[1] if isinstance(x, tuple) else None for x in a_list]
    b_planes = [x[1] if isinstance(x, tuple) else None for x in b_list]
    a_list = [x[0] if isinstance(x, tuple) else x for x in a_list]
    b_list = [x[0] if isinstance(x, tuple) else x for x in b_list]
    a_shape, b_shape = a_list[0].shape[-2:], b_list[0].shape[-2:]
    n_terms = max(len(a_list), len(b_list))
    a_idx = [0] * n_terms if len(a_list) == 1 else list(range(n_terms))
    b_idx = [0] * n_terms if len(b_list) == 1 else list(range(n_terms))
    n_acc = n_terms if separate else 1
    assert not (stack_out or scale != 1.0) or epi is None
    if out_dtypes is None:
        out_dtypes = (F32,) * (1 if (epi is not None or stack_out) else n_acc)
    in_place = epi is None
    if mode == "tn":
        K, M = a_shape
        K2, N = b_shape
    elif mode == "nt":
        M, K = a_shape
        N, K2 = b_shape
    else:
        M, K = a_shape
        K2, N = b_shape
    assert K == K2, (name, a_shape, b_shape)
    if mode == "tn":
        tm, tn, tk = _tile(M // diag, tm), _tile(N // diag, tn), _tile(K, tk)
        nk = K // tk
        N = N // diag
        row_tiles, col_tiles = (M // diag) // tm, N // tn
        a_blk, a_map = (tk, tm), lambda i, j, k: (k, i)
        b_blk, b_map = (tk, tn), lambda i, j, k: (k, (i // row_tiles) * col_tiles + j)
    else:
        tm, tn, tk = _tile(M, tm), _tile(N // diag, tn), _tile(K // diag, tk)
        nk = (K // diag) // tk
        col_tiles = (N // diag) // tn
        a_blk, a_map = (tm, tk), lambda i, j, k: (i, (j // col_tiles) * nk + k)
        if mode == "nt":
            b_blk, b_map = (tn, tk), lambda i, j, k: (j, (j // col_tiles) * nk + k)
        else:
            b_blk, b_map = (tk, tn), lambda i, j, k: ((j // col_tiles) * nk + k, j)

    resident = {}
    if single_buffer_weights:
        assert mode != "tn" and N == tn and nk == 1, name
        resident = {"pipeline_mode": pl.Buffered(1)}

    def plane_spec(blk, index_map, plane, **kw):
        if plane is None:
            return pl.BlockSpec(blk, index_map, **kw)
        return pl.BlockSpec((None,) + blk, lambda i, j, k: (plane,) + index_map(i, j, k), **kw)

    a_specs = [plane_spec(a_blk, a_map, p_) for p_ in a_planes]
    b_specs = [plane_spec(b_blk, b_map, p_, **resident) for p_ in b_planes]
    assert not col_sums or N == tn, (name, N, tn)
    ex_specs = []
    for e in extras:
        if e.shape == (M, N):
            ex_specs.append(pl.BlockSpec((tm, tn), lambda i, j, k: (i, j)))
        elif e.shape == (1, N):
            ex_specs.append(pl.BlockSpec((1, tn), lambda i, j, k: (0, j)))
        elif e.shape == (M, 1):
            ex_specs.append(pl.BlockSpec((tm, 1), lambda i, j, k: (i, 0)))
        else:
            raise ValueError((name, e.shape, (M, N)))
    na, nb, ne, no = len(a_list), len(b_list), len(extras), len(out_dtypes)
    dims = (_DIMS[mode], ((), ()))

    n_scratch = n_acc if (nk > 1 and not in_place) else 0
    x_ins, x_out, x_sems = _exchange_args(exchange)
    grid = (M // tm, N // tn, nk)

    def body(*refs):
        a_refs, b_refs, ex_refs, xin, out_refs, sum_refs, xout, acc_refs, xsems = _split_refs(
            refs, na, nb, ne, len(x_ins), no, col_sums, len(x_out), n_scratch)
        if exchange is not None:
            step = (pl.program_id(0) * grid[1] + pl.program_id(1)) * nk + pl.program_id(2)
            _run_exchange(exchange, step, grid[0] * grid[1] * nk, (xin, xout, xsems))
        a_vals = [r[...].astype(BF16) for r in a_refs]
        b_vals = [r[...].astype(BF16) for r in b_refs]
        prods = [lax.dot_general(a_vals[a_idx[t]], b_vals[b_idx[t]], dims, preferred_element_type=F32)
                 for t in range(n_terms)]
        if not separate:
            total = prods[0]
            for p_ in prods[1:]:
                total = total + p_
            prods = [total]

        def finish(accs):
            res = epi(*accs, *[e[...] for e in ex_refs]) if epi is not None else tuple(accs)
            for r, v in zip(out_refs, res[:no]):
                r[...] = v.astype(r.dtype)
            first_rows = pl.program_id(0) == 0
            for r, v in zip(sum_refs, res[no:]):
                @pl.when(first_rows)
                def _(r=r, v=v):
                    r[...] = v

                @pl.when(jnp.logical_not(first_rows))
                def _(r=r, v=v):
                    r[...] += v

        if in_place:
            dst = [(out_refs[0], t) for t in range(n_acc)] if stack_out else [(r, None) for r in out_refs]

            def read(r, t):
                return r[...] if t is None else r[t]

            assert nk == 1 or all(dt == F32 for dt in out_dtypes), name

            def write(r, t, v):
                if t is None:
                    r[...] = v.astype(r.dtype)
                else:
                    r[t] = v.astype(r.dtype)

            if nk == 1:
                for (r, t), v in zip(dst, prods):
                    write(r, t, v * scale if scale != 1.0 else v)
            else:
                k = pl.program_id(2)

                @pl.when(k == 0)
                def _():
                    for (r, t), v in zip(dst, prods):
                        write(r, t, v)

                @pl.when(jnp.logical_and(k > 0, k < nk - 1))
                def _():
                    for (r, t), v in zip(dst, prods):
                        write(r, t, read(r, t) + v)

                @pl.when(k == nk - 1)
                def _():
                    for (r, t), v in zip(dst, prods):
                        total = read(r, t) + v
                        write(r, t, total * scale if scale != 1.0 else total)
        elif nk == 1:
            finish(prods)
        else:
            k = pl.program_id(2)

            @pl.when(k == 0)
            def _():
                for r, v in zip(acc_refs, prods):
                    r[...] = v

            @pl.when(jnp.logical_and(k > 0, k < nk - 1))
            def _():
                for r, v in zip(acc_refs, prods):
                    r[...] += v

            @pl.when(k == nk - 1)
            def _():
                finish([r[...] + v for r, v in zip(acc_refs, prods)])

    if stack_out:
        out_specs = [pl.BlockSpec((n_acc, tm, tn), lambda i, j, k: (0, i, j))]
        out_shape = [jax.ShapeDtypeStruct((n_acc, M, N), F32)]
    else:
        out_specs = [pl.BlockSpec((tm, tn), lambda i, j, k: (i, j))] * no
        out_shape = [jax.ShapeDtypeStruct((M, N), dt) for dt in out_dtypes]
    sequential = col_sums or exchange is not None
    outs = pl.pallas_call(
        body,
        name=name,
        grid=grid,
        in_specs=a_specs + b_specs + ex_specs + [ANY] * len(x_ins),
        out_specs=out_specs + [pl.BlockSpec((1, tn), lambda i, j, k: (0, j))] * col_sums + [ANY] * len(x_out),
        out_shape=out_shape + [jax.ShapeDtypeStruct((1, N), F32)] * col_sums + x_out,
        scratch_shapes=[pltpu.VMEM((tm, tn), F32)] * n_scratch + x_sems,
        compiler_params=_params(*(("arbitrary",) * 3 if sequential else ("parallel", "parallel", "arbitrary"))),
    )(*a_list, *b_list, *extras, *x_ins)
    if exchange is not None:
        n_own = len(outs) - len(x_out)
        return list(outs[:n_own]), list(outs[n_own:])
    return outs


def rowwise(fn, ins, outs, accs=(), *, name, tr=512):
    R = max(x.shape[0] for x in ins)
    tr = _row_tile(R, tr)
    in_specs = []
    for x in ins:
        if x.shape[0] == R and x.ndim == 2:
            in_specs.append(pl.BlockSpec((tr, x.shape[1]), lambda i: (i, 0)))
        else:
            in_specs.append(pl.BlockSpec(x.shape, lambda i, _n=x.ndim: (0,) * _n))
    ni, no = len(ins), len(outs)

    def body(*refs):
        i = pl.program_id(0)
        row_vals, acc_vals = fn(*[r[...] for r in refs[:ni]])
        for r, v in zip(refs[ni:ni + no], row_vals):
            r[...] = v.astype(r.dtype)
        for r, v in zip(refs[ni + no:], acc_vals):
            @pl.when(i == 0)
            def _(r=r, v=v):
                r[...] = v

            @pl.when(i > 0)
            def _(r=r, v=v):
                r[...] += v

    return pl.pallas_call(
        body,
        name=name,
        grid=(R // tr,),
        in_specs=in_specs,
        out_specs=[pl.BlockSpec((tr, c), lambda i: (i, 0)) for c, _ in outs]
        + [pl.BlockSpec(s, lambda i: (0, 0)) for s in accs],
        out_shape=[jax.ShapeDtypeStruct((R, c), dt) for c, dt in outs]
        + [jax.ShapeDtypeStruct(s, F32) for s in accs],
        compiler_params=_params("arbitrary"),
    )(*ins)


def _sigmoid(x):
    return 1.0 / (1.0 + jnp.exp(-x))


_GELU_C = math.sqrt(2.0 / math.pi)


def _gelu(x):
    return 0.5 * x * (1.0 + jnp.tanh(_GELU_C * (x + 0.044715 * (x * x * x))))


def _gelu_grad(x):
    t = jnp.tanh(_GELU_C * (x + 0.044715 * (x * x * x)))
    return 0.5 * (1.0 + t) + 0.5 * x * (1.0 - t * t) * (_GELU_C * (1.0 + 3.0 * 0.044715 * (x * x)))


def rms_fwd(x, g, name):
    def fn(x, g):
        r = lax.rsqrt(jnp.mean(x * x, axis=-1, keepdims=True) + EPS)
        return [x * r * g], []

    return rowwise(fn, [x, g], [(x.shape[1], BF16)], name=name)[0]


def _rms_tile(x, g):
    return x * lax.rsqrt(jnp.mean(x * x, axis=-1, keepdims=True) + EPS) * g


def _rms_bwd_tile(dn, x, g, dres):
    r = lax.rsqrt(jnp.mean(x * x, axis=-1, keepdims=True) + EPS)
    w = dn * g
    dx = r * w - x * (r * r * r) * jnp.mean(x * w, axis=-1, keepdims=True)
    return dres + dx, jnp.sum(dn * (x * r), axis=0, keepdims=True)


def _whole(shape):
    return pl.BlockSpec(shape, lambda: (0,) * len(shape))


def _zoh(lr, li, ldt):
    dt = jnp.exp(ldt)
    mag = jnp.exp(lr * dt)
    ar, ai = mag * jnp.cos(li * dt), mag * jnp.sin(li * dt)
    den = lr * lr + li * li
    kr = ((ar - 1.0) * lr + ai * li) / den
    ki = (ai * lr - (ar - 1.0) * li) / den
    return dt, ar, ai, den, kr, ki


def ssm_prep(lam_re, lam_im, log_dt, b_re, b_im, name):
    n = SSM_CH

    def body(lr_ref, li_ref, ldt_ref, br_ref, bi_ref, ar_ref, ai_ref, bbr_ref, bbi_ref):
        _, ar, ai, _, kr, ki = _zoh(lr_ref[...], li_ref[...], ldt_ref[...])
        br, bi = br_ref[...], bi_ref[...]
        ar_ref[...] = ar
        ai_ref[...] = ai
        bbr_ref[...] = kr * br - ki * bi
        bbi_ref[...] = kr * bi + ki * br

    rows = lam_re.shape[0]
    col, mat = pl.BlockSpec((n, 1), lambda i: (i, 0)), pl.BlockSpec((n, SSM_GROUP_CH), lambda i: (i, 0))
    return pl.pallas_call(
        body, name=name,
        grid=(rows // n,),
        in_specs=[col] * 3 + [mat] * 2,
        out_specs=[col] * 2 + [mat] * 2,
        out_shape=[jax.ShapeDtypeStruct((rows, 1), F32)] * 2 + [jax.ShapeDtypeStruct((rows, SSM_GROUP_CH), F32)] * 2,
        compiler_params=_params("parallel"),
    )(lam_re, lam_im, log_dt, b_re, b_im)


def ssm_prep_bwd(lam_re, lam_im, log_dt, b_re, b_im, d_ar, d_ai, d_bbr, d_bbi, name):
    n = lam_re.shape[0]
    n_groups = n // SSM_STATE

    def body(lr_ref, li_ref, ldt_ref, br_ref, bi_ref, dar_ref, dai_ref, dbr_ref, dbi_ref,
             glr_ref, gli_ref, gdt_ref, gbr_ref, gbi_ref):
        lr, li = lr_ref[...], li_ref[...]
        dt, ar, ai, den, kr, ki = _zoh(lr, li, ldt_ref[...])
        br, bi, dbr, dbi = br_ref[...], bi_ref[...], dbr_ref[...], dbi_ref[...]
        gbr_ref[...] = kr * dbr + ki * dbi
        gbi_ref[...] = kr * dbi - ki * dbr
        gkr = jnp.sum(br * dbr + bi * dbi, axis=1, keepdims=True)
        gki = jnp.sum(br * dbi - bi * dbr, axis=1, keepdims=True)
        gar = dar_ref[...] + (gkr * lr - gki * li) / den
        gai = dai_ref[...] + (gki * lr + gkr * li) / den
        qr, qi = -(kr * lr + ki * li) / den, -(ki * lr - kr * li) / den
        g1r, g1i = qr * gkr + qi * gki, qr * gki - qi * gkr
        g2r, g2i = dt * (ar * gar + ai * gai), dt * (ar * gai - ai * gar)
        glr_ref[...] = g1r + g2r
        gli_ref[...] = g1i + g2i
        pr, pi_ = lr * ar - li * ai, lr * ai + li * ar
        gdt = (pr * gar + pi_ * gai) * dt
        grp = lax.broadcasted_iota(jnp.int32, (n, n_groups), 0) // SSM_STATE
        sel = grp == lax.broadcasted_iota(jnp.int32, (n, n_groups), 1)
        gdt_ref[...] = jnp.sum(jnp.where(sel, gdt, 0.0), axis=0, keepdims=True)

    col, mat = (n, 1), (n, SSM_GROUP_CH)
    return pl.pallas_call(
        body, name=name,
        in_specs=[_whole(col)] * 3 + [_whole(mat)] * 2 + [_whole(col)] * 2 + [_whole(mat)] * 2,
        out_specs=[_whole(col)] * 2 + [_whole((1, n_groups))] + [_whole(mat)] * 2,
        out_shape=[jax.ShapeDtypeStruct(col, F32)] * 2 + [jax.ShapeDtypeStruct((1, n_groups), F32)]
        + [jax.ShapeDtypeStruct(mat, F32)] * 2,
        compiler_params=_params(),
    )(lam_re, lam_im, log_dt, b_re, b_im, d_ar, d_ai, d_bbr, d_bbi)


def _cmul(ar, ai, br, bi):
    return ar * br - ai * bi, ar * bi + ai * br


def _scan_block(xr, xi, lr, li, carry_r, carry_i, or_ref, oi_ref, loc_r, loc_i, reverse):
    tb, cb = xr.shape
    ng = tb // SUBLANES
    xr = xr.reshape(ng, SUBLANES, cb)
    xi = xi.reshape(ng, SUBLANES, cb)
    rid = lax.broadcasted_iota(jnp.int32, (1, SUBLANES, cb), 1)
    pr, pi_ = lr.reshape(1, 1, cb), li.reshape(1, 1, cb)
    powers = []
    for k in (1, 2, 4):
        powers.append((pr, pi_))
        shift = SUBLANES - k if reverse else k
        sr, si = pltpu.roll(xr, shift, 1), pltpu.roll(xi, shift, 1)
        keep = (rid < SUBLANES - k) if reverse else (rid >= k)
        tr_, ti_ = _cmul(jnp.where(keep, pr, 0.0), jnp.where(keep, pi_, 0.0), sr, si)
        xr = xr + tr_
        xi = xi + ti_
        pr, pi_ = _cmul(pr, pi_, pr, pi_)
    loc_r[...] = xr
    loc_i[...] = xi
    (p1r, p1i), (p2r, p2i), (p4r, p4i) = powers
    dist = lax.broadcasted_iota(jnp.int32, (SUBLANES, cb), 0)
    if reverse:
        dist = SUBLANES - 1 - dist
    wr = jnp.broadcast_to(p1r.reshape(1, cb), (SUBLANES, cb))
    wi = jnp.broadcast_to(p1i.reshape(1, cb), (SUBLANES, cb))
    for bit, (qr, qi) in ((1, (p1r, p1i)), (2, (p2r, p2i)), (4, (p4r, p4i))):
        mr, mi = _cmul(wr, wi, qr.reshape(1, cb), qi.reshape(1, cb))
        on = (dist & bit) != 0
        wr, wi = jnp.where(on, mr, wr), jnp.where(on, mi, wi)
    last = 0 if reverse else SUBLANES - 1

    def step(j, carry):
        cr, ci = carry
        g = (ng - 1 - j) if reverse else j
        fr = loc_r[g] + (wr * cr - wi * ci)
        fi = loc_i[g] + (wr * ci + wi * cr)
        rows = pl.ds(pl.multiple_of(g * SUBLANES, SUBLANES), SUBLANES)
        or_ref[rows, :] = fr
        oi_ref[rows, :] = fi
        return fr[last:last + 1, :], fi[last:last + 1, :]

    cr, ci = lax.fori_loop(0, ng, step, (carry_r[...], carry_i[...]))
    carry_r[...] = cr
    carry_i[...] = ci


def _scan_tiles(seq_len, n_ch):
    return min(256, seq_len), min(512, n_ch)


def _run_exchange(plan, step, n_steps, refs):
    @pl.when(step == 0)
    def _():
        plan.start(*refs)

    for part, at in plan.relay_steps(n_steps):
        @pl.when(step == at)
        def _(part=part):
            plan.relay(part, *refs)

    @pl.when(step == n_steps - 1)
    def _():
        plan.finish(*refs)


def _exchange_args(plan):
    if plan is None:
        return [], [], []
    return list(plan.ins), list(plan.out_shape), list(plan.sems)


def _split_refs(refs, *counts):
    groups, at = [], 0
    for n in counts:
        groups.append(refs[at:at + n])
        at += n
    return groups + [refs[at:]]


def ssm_scan(x_re, x_im, lam_re, lam_im, batch, name, exchange=None):
    n, nch = x_re.shape
    seq = n // batch
    tb, cb = _scan_tiles(seq, nch)
    nt, nc = seq // tb, nch // cb
    ex_ins, ex_out, ex_sems = _exchange_args(exchange)

    def body(*refs):
        ins, xin, outs, xout, scratch, xsems = _split_refs(refs, 4, len(ex_ins), 2, len(ex_out), 6)
        xr_ref, xi_ref, lr_ref, li_ref = ins
        or_ref, oi_ref = outs
        car_r, car_i, loc_r, loc_i, s_r, s_i = scratch
        if exchange is not None:
            step = (pl.program_id(0) * batch + pl.program_id(1)) * nt + pl.program_id(2)
            _run_exchange(exchange, step, nc * batch * nt, (xin, xout, xsems))

        @pl.when(pl.program_id(2) == 0)
        def _():
            car_r[...] = jnp.zeros_like(car_r)
            car_i[...] = jnp.zeros_like(car_i)

        _scan_block(xr_ref[...].astype(F32), xi_ref[...].astype(F32), lr_ref[...], li_ref[...], car_r, car_i,
                    s_r, s_i, loc_r, loc_i, reverse=False)
        or_ref[...] = s_r[...].astype(or_ref.dtype)
        oi_ref[...] = s_i[...].astype(oi_ref.dtype)

    blk = pl.BlockSpec((tb, cb), lambda c, b, t: (b * nt + t, c))
    lam_spec = pl.BlockSpec((1, cb), lambda c, b, t: (0, c))
    res = pl.pallas_call(
        body, name=name,
        grid=(nc, batch, nt),
        in_specs=[blk, blk, lam_spec, lam_spec] + [ANY] * len(ex_ins),
        out_specs=[blk, blk] + [ANY] * len(ex_out),
        out_shape=[jax.ShapeDtypeStruct((n, nch), BF16)] * 2 + ex_out,
        scratch_shapes=[pltpu.VMEM((1, cb), F32)] * 2 + [pltpu.VMEM((tb // SUBLANES, SUBLANES, cb), F32)] * 2
        + [pltpu.VMEM((tb, cb), F32)] * 2 + ex_sems,
        compiler_params=_params("arbitrary", "arbitrary", "arbitrary"),
    )(x_re, x_im, lam_re, lam_im, *ex_ins)
    return res[0], res[1], list(res[2:])


def ssm_scan_bwd(d_re, d_im, s_re, s_im, lam_re, lam_im, batch, name, exchange=None):
    n, nch = d_re.shape
    seq = n // batch
    tb, cb = _scan_tiles(seq, nch)
    nt, nc = seq // tb, nch // cb
    halo_rows = 2 * SUBLANES
    hb = tb // halo_rows
    ex_ins, ex_out, ex_sems = _exchange_args(exchange)

    def body(*refs):
        ins, xin, outs, xout, scratch, xsems = _split_refs(refs, 8, len(ex_ins), 4, len(ex_out), 6)
        xr_ref, xi_ref, sr_ref, si_ref, hr_ref, hi_ref, lr_ref, li_ref = ins
        or_ref, oi_ref, dlr_ref, dli_ref = outs
        car_r, car_i, loc_r, loc_i, g_r, g_i = scratch
        b, t = pl.program_id(1), pl.program_id(2)
        if exchange is not None:
            step = (pl.program_id(0) * batch + b) * nt + t
            _run_exchange(exchange, step, nc * batch * nt, (xin, xout, xsems))

        @pl.when(t == 0)
        def _():
            car_r[...] = jnp.zeros_like(car_r)
            car_i[...] = jnp.zeros_like(car_i)

        _scan_block(xr_ref[...].astype(F32), xi_ref[...].astype(F32), lr_ref[...], -li_ref[...], car_r, car_i,
                    g_r, g_i, loc_r, loc_i, reverse=True)
        gr, gi = g_r[...], g_i[...]
        or_ref[...] = gr.astype(or_ref.dtype)
        oi_ref[...] = gi.astype(oi_ref.dtype)
        first_block = t == nt - 1
        row = lax.broadcasted_iota(jnp.int32, (tb, cb), 0)
        hr = jnp.where(first_block, 0.0, hr_ref[...].astype(F32)[halo_rows - 1:halo_rows, :])
        hi = jnp.where(first_block, 0.0, hi_ref[...].astype(F32)[halo_rows - 1:halo_rows, :])
        pr = jnp.where(row == 0, hr, pltpu.roll(sr_ref[...].astype(F32), 1, 0))
        pi_ = jnp.where(row == 0, hi, pltpu.roll(si_ref[...].astype(F32), 1, 0))
        dlr = jnp.sum(gr * pr + gi * pi_, axis=0, keepdims=True)
        dli = jnp.sum(gi * pr - gr * pi_, axis=0, keepdims=True)
        start = jnp.logical_and(b == 0, t == 0)

        @pl.when(start)
        def _():
            dlr_ref[...] = dlr
            dli_ref[...] = dli

        @pl.when(jnp.logical_not(start))
        def _():
            dlr_ref[...] += dlr
            dli_ref[...] += dli

    def blk(c, b, t):
        return b * nt + (nt - 1 - t)

    st_spec = pl.BlockSpec((tb, cb), lambda c, b, t: (blk(c, b, t), c))
    halo_spec = pl.BlockSpec((halo_rows, cb), lambda c, b, t: (jnp.maximum(blk(c, b, t) * hb - 1, 0), c))
    row_spec = pl.BlockSpec((1, cb), lambda c, b, t: (0, c))
    res = pl.pallas_call(
        body, name=name,
        grid=(nc, batch, nt),
        in_specs=[st_spec] * 4 + [halo_spec] * 2 + [row_spec] * 2 + [ANY] * len(ex_ins),
        out_specs=[st_spec, st_spec, row_spec, row_spec] + [ANY] * len(ex_out),
        out_shape=[jax.ShapeDtypeStruct((n, nch), BF16)] * 2 + [jax.ShapeDtypeStruct((1, nch), F32)] * 2 + ex_out,
        scratch_shapes=[pltpu.VMEM((1, cb), F32)] * 2 + [pltpu.VMEM((tb // SUBLANES, SUBLANES, cb), F32)] * 2
        + [pltpu.VMEM((tb, cb), F32)] * 2 + ex_sems,
        compiler_params=_params("arbitrary", "arbitrary", "arbitrary"),
    )(d_re, d_im, s_re, s_im, s_re, s_im, lam_re, lam_im, *ex_ins)
    return res[0], res[1], res[2], res[3], list(res[4:])


def _pool_tiles(seq_len):
    return min(512, seq_len)


def _window_sums(x, n_steps, forward_in_time):
    rows = x.shape[0]
    k = 1
    for _ in range(n_steps):
        x = x + pltpu.roll(x, k if forward_in_time else rows - k, 0)
        k *= 2
    return x


def pool_fwd(u, w_pool, scale, batch, name):
    n, c = u.shape
    seq = n // batch
    tb = _pool_tiles(seq)
    nt = seq // tb
    gc = c // len(POOL_WINDOWS)
    hb = tb // POOL_HALO

    def body(x_ref, halo_ref, w_ref, sc_ref, y_ref, q_ref):
        t = pl.program_id(1)
        halo = jnp.where(t == 0, 0.0, halo_ref[...])
        full = jnp.concatenate([halo, x_ref[...]], axis=0)
        pos = lax.broadcasted_iota(jnp.int32, (tb, gc), 0) + t * tb + 1
        for gi, win in enumerate(POOL_WINDOWS):
            cols = slice(gi * gc, (gi + 1) * gc)
            sums = _window_sums(full[:, cols], gi + 1, True)[POOL_HALO:, :]
            cnt = jnp.minimum(pos, win).astype(F32)
            q = sums / cnt - x_ref[:, cols]
            r = jnp.dot(q.astype(BF16), w_ref[gi].astype(BF16), preferred_element_type=F32)
            q_ref[:, cols] = q.astype(q_ref.dtype)
            y_ref[:, cols] = (r * sc_ref[:, cols]).astype(y_ref.dtype)

    return pl.pallas_call(
        body, name=name,
        grid=(batch, nt),
        in_specs=[pl.BlockSpec((tb, c), lambda b, t: (b * nt + t, 0)),
                  pl.BlockSpec((POOL_HALO, c), lambda b, t: (jnp.maximum((b * nt + t) * hb - 1, 0), 0)),
                  pl.BlockSpec(w_pool.shape, lambda b, t: (0, 0, 0)),
                  pl.BlockSpec((1, c), lambda b, t: (0, 0))],
        out_specs=[pl.BlockSpec((tb, c), lambda b, t: (b * nt + t, 0))] * 2,
        out_shape=[jax.ShapeDtypeStruct((n, c), BF16)] * 2,
        compiler_params=_params("parallel", "arbitrary"),
    )(u, u, w_pool, scale)


def pool_bwd(dy, q, w_pool, scale, batch, name):
    n, c = dy.shape
    seq = n // batch
    tb = _pool_tiles(seq)
    nt = seq // tb
    ng = len(POOL_WINDOWS)
    gc = c // ng
    hb = tb // POOL_HALO
    n_blocks = n // POOL_HALO

    def body(dy_ref, dyh_ref, q_ref, w_ref, sc_ref, du_ref, dw_ref, dsc_ref):
        b, t = pl.program_id(0), pl.program_id(1)
        last = t == nt - 1
        dy_full = jnp.concatenate([dy_ref[...], jnp.where(last, 0.0, dyh_ref[...])], axis=0)
        pos = lax.broadcasted_iota(jnp.int32, (tb + POOL_HALO, gc), 0) + t * tb + 1
        start = jnp.logical_and(b == 0, t == 0)
        for gi, win in enumerate(POOL_WINDOWS):
            cols = slice(gi * gc, (gi + 1) * gc)
            w = w_ref[gi].astype(BF16)
            dr = dy_full[:, cols] * sc_ref[:, cols]
            dq = lax.dot_general(dr.astype(BF16), w, (((1,), (1,)), ((), ())), preferred_element_type=F32)
            cnt = jnp.minimum(pos, win).astype(F32)
            back = _window_sums(dq / cnt, gi + 1, False)
            du_ref[:, cols] = back[:tb, :] - dq[:tb, :]
            qb = q_ref[:, cols]
            r = jnp.dot(qb, w, preferred_element_type=F32)
            dw = lax.dot_general(qb, dr[:tb, :].astype(BF16), (((0,), (0,)), ((), ())), preferred_element_type=F32)
            dsc = jnp.sum(dy_ref[:, cols] * r, axis=0, keepdims=True)

            @pl.when(start)
            def _(gi=gi, cols=cols, dw=dw, dsc=dsc):
                dw_ref[gi] = dw
                dsc_ref[:, cols] = dsc

            @pl.when(jnp.logical_not(start))
            def _(gi=gi, cols=cols, dw=dw, dsc=dsc):
                dw_ref[gi] += dw
                dsc_ref[:, cols] += dsc

    blk = pl.BlockSpec((tb, c), lambda b, t: (b * nt + t, 0))
    halo = pl.BlockSpec((POOL_HALO, c), lambda b, t: (jnp.minimum((b * nt + t + 1) * hb, n_blocks - 1), 0))
    return pl.pallas_call(
        body, name=name,
        grid=(batch, nt),
        in_specs=[blk, halo, blk,
                  pl.BlockSpec(w_pool.shape, lambda b, t: (0, 0, 0)),
                  pl.BlockSpec((1, c), lambda b, t: (0, 0))],
        out_specs=[blk, pl.BlockSpec(w_pool.shape, lambda b, t: (0, 0, 0)), pl.BlockSpec((1, c), lambda b, t: (0, 0))],
        out_shape=[jax.ShapeDtypeStruct((n, c), F32), jax.ShapeDtypeStruct(w_pool.shape, F32),
                   jax.ShapeDtypeStruct((1, c), F32)],
        compiler_params=_params("arbitrary", "arbitrary"),
    )(dy, dy, q, w_pool, scale)


def _place():
    return lax.axis_index("x"), lax.axis_index("y"), lax.axis_index("c")


class GatherPlan:
    def __init__(self, arrs):
        self.ins = list(arrs)
        na = len(arrs)
        self.out_shape = [jax.ShapeDtypeStruct((N_DEV,) + a.shape, a.dtype) for a in arrs]
        self.sems = [pltpu.SemaphoreType.DMA((na, 7)), pltpu.SemaphoreType.DMA((na, 7)), pltpu.SemaphoreType.DMA((na,))]
        self.sizes = [math.prod(a.shape) * a.dtype.itemsize for a in arrs]

    def relay_steps(self, n_steps):
        total, done, steps = sum(self.sizes), 0, []
        for a, size in enumerate(self.sizes):
            done += size
            steps.append((a, min(n_steps - 1, (done * (n_steps - 1)) // total)))
        return steps

    def _copy(self, outs, sems, a, k, block, to, src=None):
        dst = outs[a].at[4 * block[0] + 2 * block[1] + block[2]]
        return pltpu.make_async_remote_copy(
            src_ref=dst if src is None else src, dst_ref=dst,
            send_sem=sems[0].at[a, k], recv_sem=sems[1].at[a, k], device_id=to, device_id_type=MESH)

    @staticmethod
    def _chips(x, y):
        return [(1 - x, y), (x, 1 - y), (1 - x, 1 - y)]

    def _local(self, ins, outs, sems, a, me):
        return pltpu.make_async_copy(ins[a], outs[a].at[4 * me[0] + 2 * me[1] + me[2]], sems[2].at[a])

    def start(self, ins, outs, sems):
        x, y, c = _place()
        me = (x, y, c)
        for a in range(len(ins)):
            self._local(ins, outs, sems, a, me).start()
            self._copy(outs, sems, a, 0, me, (x, y, 1 - c), src=ins[a]).start()
            for j, chip in enumerate(self._chips(x, y)):
                self._copy(outs, sems, a, 1 + j, me, (*chip, c), src=ins[a]).start()

    def relay(self, a, ins, outs, sems):
        x, y, c = _place()
        for j, chip in enumerate(self._chips(x, y)):
            self._copy(outs, sems, a, 1 + j, (*chip, c), (x, y, c)).wait_recv()
            self._copy(outs, sems, a, 4 + j, (*chip, c), (x, y, 1 - c)).start()

    def finish(self, ins, outs, sems):
        x, y, c = _place()
        me, sibling = (x, y, c), (x, y, 1 - c)
        for a in range(len(ins)):
            self._copy(outs, sems, a, 0, sibling, me).wait_recv()
            for j, chip in enumerate(self._chips(x, y)):
                self._copy(outs, sems, a, 4 + j, (*chip, 1 - c), me).wait_recv()
        for a in range(len(ins)):
            self._copy(outs, sems, a, 0, me, sibling, src=ins[a]).wait_send()
            for j, chip in enumerate(self._chips(x, y)):
                self._copy(outs, sems, a, 1 + j, me, (*chip, c), src=ins[a]).wait_send()
                self._copy(outs, sems, a, 4 + j, (*chip, c), sibling).wait_send()
            self._local(ins, outs, sems, a, me).wait()


class ChipScatterPlan:
    def __init__(self, arrs):
        self.groups = [list(a) if isinstance(a, list) else [a] for a in arrs]
        self.ins = [piece for group in self.groups for piece in group]
        self.first = [sum(len(g) for g in self.groups[:a]) for a in range(len(self.groups))]
        na = len(arrs)
        self.out_shape = [jax.ShapeDtypeStruct((4,) + g[0].shape[1:], g[0].dtype) for g in self.groups]
        self.sems = [pltpu.SemaphoreType.DMA((na, 3)), pltpu.SemaphoreType.DMA((na, 3)), pltpu.SemaphoreType.DMA((na,))]

    def relay_steps(self, n_steps):
        return []

    def _row(self, ins, a, px, py):
        if len(self.groups[a]) == 1:
            return ins[self.first[a]].at[2 * px + py]
        return ins[self.first[a] + px].at[py]

    def start(self, ins, outs, sems):
        x, y, c = _place()
        mine = 2 * x + y
        for xs in (0, 1):
            @pl.when(x == xs)
            def _(xs=xs):
                for a in range(len(self.groups)):
                    pltpu.make_async_copy(self._row(ins, a, xs, y), outs[a].at[mine], sems[2].at[a]).start()
                    for j, (px, py) in enumerate([(1 - xs, y), (xs, 1 - y), (1 - xs, 1 - y)]):
                        pltpu.make_async_remote_copy(
                            src_ref=self._row(ins, a, px, py), dst_ref=outs[a].at[mine],
                            send_sem=sems[0].at[a, j], recv_sem=sems[1].at[a, j],
                            device_id=(px, py, c), device_id_type=MESH).start()

    def finish(self, ins, outs, sems):
        x, y, c = _place()
        for wait_recv in (True, False):
            for a in range(len(self.groups)):
                for j in range(3):
                    cp = pltpu.make_async_remote_copy(
                        src_ref=self._row(ins, a, 0, 0), dst_ref=outs[a].at[0],
                        send_sem=sems[0].at[a, j], recv_sem=sems[1].at[a, j],
                        device_id=(x, y, c), device_id_type=MESH)
                    if wait_recv:
                        cp.wait_recv()
                    else:
                        cp.wait_send()
        for a in range(len(self.groups)):
            pltpu.make_async_copy(self._row(ins, a, 0, 0), outs[a].at[0], sems[2].at[a]).wait()


def run_exchange(plan, name):
    n_in, n_out = len(plan.ins), len(plan.out_shape)

    def body(*refs):
        parts = (refs[:n_in], refs[n_in:n_in + n_out], refs[n_in + n_out:])
        plan.start(*parts)
        for part, _ in plan.relay_steps(1):
            plan.relay(part, *parts)
        plan.finish(*parts)

    return pl.pallas_call(
        body, name=name,
        in_specs=[ANY] * n_in, out_specs=[ANY] * n_out,
        out_shape=plan.out_shape, scratch_shapes=plan.sems,
    )(*plan.ins)


class SiblingSwapPlan:
    def __init__(self, arrs):
        self.ins = list(arrs)
        na = len(arrs)
        self.out_shape = [jax.ShapeDtypeStruct(a.shape, a.dtype) for a in arrs]
        self.sems = [pltpu.SemaphoreType.DMA((na,)), pltpu.SemaphoreType.DMA((na,))]

    def relay_steps(self, n_steps):
        return []

    @staticmethod
    def _copies(ins, outs, sems):
        x, y, c = _place()
        return [pltpu.make_async_remote_copy(
            src_ref=ins[a], dst_ref=outs[a], send_sem=sems[0].at[a], recv_sem=sems[1].at[a],
            device_id=(x, y, 1 - c), device_id_type=MESH) for a in range(len(ins))]

    def start(self, ins, outs, sems):
        for cp in self._copies(ins, outs, sems):
            cp.start()

    def finish(self, ins, outs, sems):
        for cp in self._copies(ins, outs, sems):
            cp.wait()


def adamw(w, gparts, m, v, name, tr=256):
    rows, cols = w.shape
    parts = gparts.shape[0]
    tr = _row_tile(rows, tr)
    c1 = 1.0 - ADAM_B1 ** ADAM_STEP
    c2 = 1.0 - ADAM_B2 ** ADAM_STEP

    def body(w_ref, g_ref, m_ref, v_ref, go_ref, d_ref, mo_ref, vo_ref):
        g = g_ref[0].astype(F32)
        for p_ in range(1, parts):
            g = g + g_ref[p_].astype(F32)
        m_new = ADAM_B1 * m_ref[...] + (1.0 - ADAM_B1) * g
        v_new = ADAM_B2 * v_ref[...] + (1.0 - ADAM_B2) * (g * g)
        m_hat = m_new / c1
        v_hat = v_new / c2
        go_ref[...] = g
        d_ref[...] = -ADAM_LR * (m_hat / (jnp.sqrt(v_hat) + ADAM_EPS) + ADAM_WD * w_ref[...])
        mo_ref[...] = m_new
        vo_ref[...] = v_new

    blk = pl.BlockSpec((tr, cols), lambda i: (i, 0))
    return pl.pallas_call(
        body, name=name,
        grid=(rows // tr,),
        in_specs=[blk, pl.BlockSpec((parts, tr, cols), lambda i: (0, i, 0)), blk, blk],
        out_specs=[blk] * 4,
        out_shape=[jax.ShapeDtypeStruct((rows, cols), F32)] * 4,
        compiler_params=_params("parallel"),
    )(w, gparts, m, v)


def add2(a, b, name, out_dtype):
    return rowwise(lambda a, b: ([a.astype(F32) + b.astype(F32)], []), [a, b], [(a.shape[1], out_dtype)],
                   name=name, tr=256)[0]


def _block_diag(x):
    g, a, b = x.shape
    tiled = jnp.tile(x.reshape(g * a, b), (1, g))
    rows = lax.broadcasted_iota(jnp.int32, (g * a, g * b), 0) // a
    cols = lax.broadcasted_iota(jnp.int32, (g * a, g * b), 1) // b
    return jnp.where(rows == cols, tiled, jnp.zeros_like(tiled))


def _diag_blocks(x, a, b):
    per = x.shape[1] // b
    x5 = x.reshape(SSM_SUPER, per, a, per, b)
    eye = jnp.eye(per, dtype=x.dtype)
    return jnp.sum(x5 * eye[None, :, None, :, None], axis=3).reshape(SSM_SUPER * per, a, b)


def _swiglu_epi(g, u):
    s = _sigmoid(g)
    silu = g * s
    return u * (s * (1.0 + g * (1.0 - s))), silu, silu * u


def _residual_epi(scale, with_norm):
    if with_norm:
        def epi(acc, res, gain):
            out = res + scale * acc
            return out, _rms_tile(out, gain)
    else:
        def epi(acc, res):
            return (res + scale * acc,)
    return epi


def _next_norm(next_gain):
    if next_gain is None:
        return [], (F32,)
    return [next_gain], (F32, BF16)


FFN_WIDE = 2816


def ffn_fwd(h, n, wi, wo, next_gain, tag, carry=None):
    if carry is None:
        carry = lambda where, run: run(None)[0]
    dact_g, dact_u, act = carry(f"{tag}_in", lambda ex: _with_exchange(matmul(
        n, [(wi, 0), (wi, 1)], mode="nn", name=f"{tag}_in", separate=True, epi=_swiglu_epi,
        out_dtypes=(BF16, BF16, BF16), exchange=ex, tm=512, tn=FFN_WIDE), ex))
    more, dtypes = _next_norm(next_gain)
    res = carry(f"{tag}_out", lambda ex: _with_exchange(matmul(
        act, wo, mode="nn", name=f"{tag}_out", epi=_residual_epi(0.5, bool(more)), extras=[h] + more,
        out_dtypes=dtypes, exchange=ex, tm=512, tk=FFN_WIDE), ex))
    return res[0], (res[1] if more else None), (h, n, dact_g, dact_u, act)


def ffn_bwd(dh, saved, gain, wi, wo, tag, carry=None, gate=None, on_weight_grads=None):
    h, n, dact_g, dact_u, act = saved
    if carry is None:
        carry = lambda where, run: run(None)[0]
    dg, du = carry(f"{tag}_out_dx", lambda ex: _with_exchange(matmul(
        dh, wo, mode="nt", name=f"{tag}_out_dx", extras=[dact_g, dact_u], out_dtypes=(BF16, BF16),
        epi=lambda acc, fg, fu: (0.5 * acc * fg, 0.5 * acc * fu), exchange=ex, tm=512, tn=FFN_WIDE), ex))
    d_wo, = carry(f"{tag}_out_dw", lambda ex: _with_exchange(matmul(
        act, dh, mode="tn", name=f"{tag}_out_dw", scale=0.5, exchange=ex, tm=FFN_WIDE), ex))
    d_wi = [carry(f"{tag}_in_dw_{k}", lambda ex, half=half: _with_exchange(matmul(
        n, half, mode="tn", name=f"{tag}_in_dw", exchange=ex, tn=FFN_WIDE), ex))[0] for k, half in (("g", dg), ("u", du))]
    if on_weight_grads is not None:
        on_weight_grads(d_wi, d_wo)
    if gate is None:
        epi, more, dtypes = _rms_bwd_tile, [], (F32,)
    else:
        def epi(acc, x, g, dres, e, pre):
            dh_new, d_gain = _rms_bwd_tile(acc, x, g, dres)
            return (dh_new, *_gate_cotangents(dh_new, e, pre), d_gain)
        more, dtypes = list(gate), (F32, BF16, BF16)
    res = carry(f"{tag}_in_dx", lambda ex: _with_exchange(matmul(
        [dg, du], [(wi, 0), (wi, 1)], mode="nt", name=f"{tag}_in_dx", epi=epi, extras=[h, gain, dh] + more,
        out_dtypes=dtypes, col_sums=1, exchange=ex, single_buffer_weights=True, tm=512, tk=FFN_WIDE), ex))
    return res[0], res[-1], d_wi, d_wo, tuple(res[1:-1])


def _with_exchange(result, exchange):
    return result if exchange is not None else (result, [])


def ssm_tables(rep):
    rows = DEPTH * SSM_CH
    lam_r, lam_i, bb_r, bb_i = ssm_prep(
        rep["ssm_lambda_re"].reshape(rows, 1), rep["ssm_lambda_im"].reshape(rows, 1),
        jnp.repeat(rep["ssm_log_dt"].reshape(-1), SSM_STATE).reshape(rows, 1),
        rep["ssm_b_re"].reshape(rows, SSM_GROUP_CH), rep["ssm_b_im"].reshape(rows, SSM_GROUP_CH), "ssm_zoh")
    lam = (lam_r.reshape(DEPTH, 1, SSM_CH), lam_i.reshape(DEPTH, 1, SSM_CH))

    def dense(x):
        return jax.vmap(_block_diag)(x.astype(BF16))

    b_mats = [dense(bb.reshape(DEPTH, SSM_GROUPS, SSM_STATE, SSM_GROUP_CH).transpose(0, 1, 3, 2)) for bb in (bb_r, bb_i)]
    c_mats = [dense(cc.transpose(0, 1, 3, 2)) for cc in (rep["ssm_c_re"], -rep["ssm_c_im"])]
    return lam, b_mats, c_mats


def mix_fwd(h, n, lw, tables, layer, batch, next_gain, tag, exchange=None):
    sw = h.shape[1] // 2
    us, up = matmul(n, [lw["w_in"][:, :sw], lw["w_in"][:, sw:]], mode="nn", name=f"{tag}_in", separate=True,
                    out_dtypes=(F32, F32))
    lam = tuple(t[layer] for t in tables[0])
    b_mats = [(m, layer) for m in tables[1]]
    c_mats = [(m, layer) for m in tables[2]]
    bu_re, bu_im = matmul(us, b_mats, mode="nn", name=f"{tag}_bu", separate=True, diag=SSM_SUPER,
                          out_dtypes=(BF16, BF16))
    s_re, s_im, exchanged = ssm_scan(bu_re, bu_im, *lam, batch, f"{tag}_scan", exchange)
    y0, y1 = matmul([s_re, s_im], c_mats, mode="nn", name=f"{tag}_c", diag=SSM_SUPER,
                    epi=lambda acc, u, d: (acc + d * u, _gelu(acc + d * u)), extras=[us, lw["ssm_d"]],
                    out_dtypes=(F32, BF16))
    y2, gl = matmul(y1, lw["w_glu"], mode="nn", name=f"{tag}_glu",
                    epi=lambda acc, y0: (_gelu(y0) * _sigmoid(acc), acc), extras=[y0], out_dtypes=(BF16, F32))
    yp, q = pool_fwd(up, lw["pool_w"], lw["pool_scale"], batch, f"{tag}_pool")
    more, dtypes = _next_norm(next_gain)
    res = matmul([y2, yp], [lw["w_out"][:sw], lw["w_out"][sw:]], mode="nn", name=f"{tag}_out",
                 epi=_residual_epi(1.0, bool(more)), extras=[h] + more, out_dtypes=dtypes)
    saved = (h, n, us, lam, b_mats, c_mats, s_re, s_im, y0, y1, gl, y2, yp, q)
    return res[0], (res[1] if more else None), saved, exchanged


def mix_bwd(dh, saved, lw, batch, tag, exchange=None):
    h, n, us, lam, b_mats, c_mats, s_re, s_im, y0, y1, gl, y2, yp, q = saved
    sw = h.shape[1] // 2
    w_out_s, w_out_p = lw["w_out"][:sw], lw["w_out"][sw:]
    d_wo_s, d_wo_p = matmul([y2, yp], dh, mode="tn", name=f"{tag}_out_dw", separate=True)
    def out_dx_epi(dy2, dyp, y0, gl):
        sg = _sigmoid(gl)
        return dy2, dyp, dy2 * _gelu(y0) * sg * (1.0 - sg)

    dy2, dyp, tg = matmul(dh, [w_out_s, w_out_p], mode="nt", name=f"{tag}_out_dx", separate=True, epi=out_dx_epi,
                          extras=[y0, gl], out_dtypes=(F32, F32, BF16))
    dup, d_pool_w, d_pool_scale = pool_bwd(dyp, q, lw["pool_w"], lw["pool_scale"], batch, f"{tag}_pool_bwd")
    def dy0_epi(acc, dy2, gl, y0, u):
        dy0 = (acc + dy2 * _sigmoid(gl)) * _gelu_grad(y0)
        return dy0, jnp.sum(dy0 * u, axis=0, keepdims=True)

    dy0, d_d = matmul(tg, lw["w_glu"], mode="nt", name=f"{tag}_glu_dx", epi=dy0_epi, extras=[dy2, gl, y0, us],
                      out_dtypes=(F32,), col_sums=1)
    d_w_glu, = matmul(y1, tg, mode="tn", name=f"{tag}_glu_dw")
    gd_re, gd_im = matmul(dy0, c_mats, mode="nt", name=f"{tag}_c_dx", separate=True, diag=SSM_SUPER,
                          out_dtypes=(BF16, BF16))
    d_c_top, d_c_bot = matmul([s_re, s_im], dy0, mode="tn", name=f"{tag}_c_dw", separate=True, diag=SSM_SUPER)
    g_re, g_im, d_lam_r, d_lam_i, exchanged = ssm_scan_bwd(gd_re, gd_im, s_re, s_im, *lam, batch, f"{tag}_scan_bwd",
                                                           exchange)
    dus, = matmul([g_re, g_im], b_mats, mode="nt", name=f"{tag}_bu_dx", diag=SSM_SUPER,
                  epi=lambda acc, dy0, d: (acc + d * dy0,), extras=[dy0, lw["ssm_d"]])
    d_b_re, d_b_im = matmul(us, [g_re, g_im], mode="tn", name=f"{tag}_bu_dw", separate=True, diag=SSM_SUPER)
    d_bb_r = _diag_blocks(d_b_re, SSM_GROUP_CH, SSM_STATE).transpose(0, 2, 1).reshape(SSM_CH, SSM_GROUP_CH)
    d_bb_i = _diag_blocks(d_b_im, SSM_GROUP_CH, SSM_STATE).transpose(0, 2, 1).reshape(SSM_CH, SSM_GROUP_CH)
    d_lr, d_li, d_ldt, d_br, d_bi = ssm_prep_bwd(
        lw["lam_re"], lw["lam_im"], lw["log_dt"], lw["b_re"], lw["b_im"],
        d_lam_r.reshape(SSM_CH, 1), d_lam_i.reshape(SSM_CH, 1), d_bb_r, d_bb_i, f"{tag}_zoh_bwd")
    d_c_re = _diag_blocks(d_c_top, SSM_STATE, SSM_GROUP_CH).transpose(0, 2, 1)
    d_c_im = -_diag_blocks(d_c_bot, SSM_STATE, SSM_GROUP_CH).transpose(0, 2, 1)
    d_w_in_s, d_w_in_p = matmul(n, [dus, dup], mode="tn", name=f"{tag}_in_dw", separate=True)
    dh_new, d_gain = matmul([dus, dup], [lw["w_in"][:, :sw], lw["w_in"][:, sw:]], mode="nt", name=f"{tag}_in_dx",
                            epi=_rms_bwd_tile, extras=[h, lw["mix_norm"], dh], out_dtypes=(F32,), col_sums=1, tm=512)
    grads = dict(mix_norm=d_gain, w_in=jnp.concatenate([d_w_in_s, d_w_in_p], axis=1),
                 ssm_lambda_re=d_lr, ssm_lambda_im=d_li, ssm_log_dt=d_ldt, ssm_b_re=d_br, ssm_b_im=d_bi,
                 ssm_c_re=d_c_re, ssm_c_im=d_c_im, ssm_d=d_d, ssm_w_glu=d_w_glu, pool_w=d_pool_w,
                 pool_scale=d_pool_scale, w_out=jnp.concatenate([d_wo_s, d_wo_p], axis=0))
    return dh_new, grads, exchanged


def ple_fwd(h, n, p, w_gate, w_proj, next_gain, tag):
    e, = matmul(p, w_proj, mode="nn", name=f"{tag}_proj")
    if next_gain is None:
        def epi(acc, e, res):
            return res + _sigmoid(acc) * e, acc
        more, dtypes = [], (F32, F32)
    else:
        def epi(acc, e, res, gain):
            out = res + _sigmoid(acc) * e
            return out, acc, _rms_tile(out, gain)
        more, dtypes = [next_gain], (F32, F32, BF16)
    res = matmul(n, w_gate, mode="nn", name=f"{tag}_gate", epi=epi, extras=[e, h] + more, out_dtypes=dtypes, tm=512)
    return res[0], (res[2] if more else None), (h, n, e, res[1])


def _gate_cotangents(dh, e, pre):
    s = _sigmoid(pre)
    return dh * e * s * (1.0 - s), dh * s


def ple_bwd(dh, dpre, de, saved, p, gain, w_gate, tag):
    h, n, e, pre = saved
    d_w_gate, = matmul(n, dpre, mode="tn", name=f"{tag}_gate_dw")
    d_w_proj, = matmul(p, de, mode="tn", name=f"{tag}_proj_dw")
    dh_new, d_gain = matmul(dpre, w_gate, mode="nt", name=f"{tag}_gate_dx", epi=_rms_bwd_tile,
                            extras=[h, gain, dh], out_dtypes=(F32,), col_sums=1, tm=512)
    return dh_new, d_gain, d_w_gate, d_w_proj


def loss_head(h, gain, target, e, pre, name):
    d = h.shape[1]

    def fn(h, g, t, e, pre):
        r = lax.rsqrt(jnp.mean(h * h, axis=-1, keepdims=True) + EPS)
        diff = h * r * g - t
        sq = jnp.sum(jnp.sum(diff * diff, axis=1, keepdims=True), axis=0, keepdims=True)
        dy = diff * (1.0 / d)
        w = dy * g
        dh = r * w - h * (r * r * r) * jnp.mean(h * w, axis=-1, keepdims=True)
        return [dh, *_gate_cotangents(dh, e, pre)], [sq, jnp.sum(dy * (h * r), axis=0, keepdims=True)]

    dh, dpre, de, sq, d_gain = rowwise(fn, [h, gain, target, e, pre], [(d, F32), (d, BF16), (d, BF16)],
                                       [(1, 1), (1, d)], name=name, tr=256)
    return 0.5 / d * sq[0, 0], dh, dpre, de, d_gain


SHARDED = {
    "ffn1_wi": 1, "ffn1_wo": 0, "w_in": 0, "ssm_w_glu": 0, "w_out": 0, "ffn2_wi": 1, "ffn2_wo": 0,
    "ple_w_gate": 0, "ple_w_proj": 1,
}
WEIGHTS = ["ffn1_norm", "ffn1_wi", "ffn1_wo", "mix_norm", "w_in", "ssm_lambda_re", "ssm_lambda_im", "ssm_log_dt",
           "ssm_b_re", "ssm_b_im", "ssm_c_re", "ssm_c_im", "ssm_d", "ssm_w_glu", "pool_w", "pool_scale", "w_out",
           "ffn2_norm", "ffn2_wi", "ffn2_wo", "ple_norm", "ple_w_gate", "ple_w_proj", "final_norm"]
REPLICATED = [n for n in WEIGHTS if n not in SHARDED]


HALVED = ("ffn1_wi", "ffn2_wi")


def _unshard(gathered, axis, halved):
    if halved:
        _, rows, cols = gathered.shape
        return gathered.reshape(2, 4, rows, cols).transpose(0, 2, 1, 3).reshape(2, rows, 4 * cols)
    g = jnp.moveaxis(gathered, 0, axis)
    shp = g.shape
    return g.reshape(shp[:axis] + (shp[axis] * shp[axis + 1],) + shp[axis + 2:])


def _split_for_scatter(full, axis, c, halved):
    if halved:
        rows, cols = full[0].shape

        def pick(cc):
            return [lax.dynamic_index_in_dim(h.reshape(rows, 2, 2, cols // 4), cc, 2, keepdims=False).transpose(1, 0, 2)
                    for h in full]

        return pick(c), [s.astype(BF16) for s in pick(1 - c)]
    shp = full.shape
    g = full.reshape(shp[:axis] + (4, 2, shp[axis] // N_DEV) + shp[axis + 1:])
    keep = lax.dynamic_index_in_dim(g, c, axis + 1, keepdims=False)
    send = lax.dynamic_index_in_dim(g, 1 - c, axis + 1, keepdims=False)
    return jnp.moveaxis(keep, axis, 0), jnp.moveaxis(send, axis, 0).astype(BF16)


def _pack(arrs):
    pieces = []
    for a in arrs:
        flat = a.reshape(-1)
        pad = (-flat.shape[0]) % PACK
        pieces.append(jnp.pad(flat, (0, pad)).reshape(-1, LANES))
    return jnp.concatenate(pieces, axis=0)


def _unpack(packed, shapes):
    out, row = [], 0
    for s in shapes:
        size = math.prod(s)
        rows = (size + PACK - 1) // PACK * SUBLANES
        out.append(packed[row:row + rows].reshape(-1)[:size].reshape(s))
        row += rows
    return out


def local_step(x, p, target, rep, hooks):
    batch, seq, d = x.shape
    n_tok = batch * seq
    h = x.reshape(n_tok, d)
    saved = []
    n = rms_fwd(h, rep["ffn1_norm"][0].reshape(1, d), "first_norm")
    tables = ssm_tables(rep)
    for i in range(DEPTH):
        lw = _layer_weights(rep, i, d)
        big = lambda name, i=i: hooks.big(i, name)
        next_gain = rep["ffn1_norm"][i + 1].reshape(1, d) if i + 1 < DEPTH else None
        def carry(where, run, i=i):
            outs, exchanged = run(hooks.fwd_exchange(i, where))
            hooks.fwd_done(i, where, exchanged)
            return outs

        h, n, s1 = ffn_fwd(h, n, big("ffn1_wi"), big("ffn1_wo"), lw["mix_norm"], "ffn1", carry)
        lw.update(w_in=big("w_in"), w_glu=big("ssm_w_glu"), w_out=big("w_out"))
        h, n, s2, exchanged = mix_fwd(h, n, lw, tables, i, batch, lw["ffn2_norm"], "mix", hooks.fwd_exchange(i, "scan"))
        hooks.fwd_done(i, "scan", exchanged)
        h, n, s3 = ffn_fwd(h, n, big("ffn2_wi"), big("ffn2_wo"), lw["ple_norm"], "ffn2", carry)
        p_i = p[i].reshape(n_tok, -1)
        h, n, s4 = ple_fwd(h, n, p_i, big("ple_w_gate"), big("ple_w_proj"), next_gain, "ple")
        saved.append((s1, s2, s3, s4, p_i))
    last_gate = saved[-1][3][2:]
    loss, dh, dpre, de, d_final = loss_head(h, rep["final_norm"].reshape(1, d), target.reshape(n_tok, d), *last_gate,
                                            "loss_head")
    hooks.final_grad(d_final)
    per_layer = [None] * DEPTH
    for i in reversed(range(DEPTH)):
        lw = _layer_weights(rep, i, d)
        big = lambda name, i=i: hooks.big(i, name)
        lw.update(w_in=big("w_in"), w_glu=big("ssm_w_glu"), w_out=big("w_out"))
        s1, s2, s3, s4, p_i = saved[i]
        g = {}
        dh, g["ple_norm"], g["ple_w_gate"], g["ple_w_proj"] = ple_bwd(dh, dpre, de, s4, p_i, lw["ple_norm"],
                                                                     big("ple_w_gate"), "ple")
        def carry(where, run, i=i):
            outs, exchanged = run(hooks.bwd_exchange(i, where))
            hooks.bwd_done(i, where, exchanged)
            return outs

        dh, g["ffn2_norm"], g["ffn2_wi"], g["ffn2_wo"], _ = ffn_bwd(
            dh, s3, lw["ffn2_norm"], big("ffn2_wi"), big("ffn2_wo"), "ffn2", carry)
        dh, gm, exchanged = mix_bwd(dh, s2, lw, batch, "mix", hooks.bwd_exchange(i, "scan"))
        hooks.bwd_done(i, "scan", exchanged)
        g.update(gm)
        hooks.layer_grads(i, g, [k for k in SHARDED if k not in LAST_GRADS])
        hooks.small_grads(i, {k: g[k] for k in REPLICATED if k in g})
        below = saved[i - 1][3][2:] if i > 0 else None

        def ffn1_grads(d_wi, d_wo, i=i, g=g):
            g["ffn1_wi"], g["ffn1_wo"] = d_wi, d_wo
            hooks.layer_grads(i, g, list(LAST_GRADS))

        dh, g["ffn1_norm"], _, _, gate_ct = ffn_bwd(
            dh, s1, lw["ffn1_norm"], big("ffn1_wi"), big("ffn1_wo"), "ffn1", carry, below, ffn1_grads)
        hooks.small_grads(i, {"ffn1_norm": g["ffn1_norm"]})
        if i > 0:
            dpre, de = gate_ct
        per_layer[i] = g
    return loss, dh.reshape(batch, seq, d), per_layer, d_final


LAST_GRADS = ("ffn1_wi", "ffn1_wo")


def _layer_weights(w, i, d):
    sw = d // 2
    lw = {}
    lw["pool_w"] = w["pool_w"][i]
    for k in ("ffn1_norm", "mix_norm", "ffn2_norm", "ple_norm"):
        lw[k] = w[k][i].reshape(1, d)
    lw["ssm_d"] = w["ssm_d"][i].reshape(1, sw)
    lw["pool_scale"] = w["pool_scale"][i].reshape(1, sw)
    lw["lam_re"] = w["ssm_lambda_re"][i].reshape(SSM_CH, 1)
    lw["lam_im"] = w["ssm_lambda_im"][i].reshape(SSM_CH, 1)
    lw["log_dt"] = jnp.repeat(w["ssm_log_dt"][i], SSM_STATE).reshape(SSM_CH, 1)
    lw["b_re"] = w["ssm_b_re"][i].reshape(SSM_CH, SSM_GROUP_CH)
    lw["b_im"] = w["ssm_b_im"][i].reshape(SSM_CH, SSM_GROUP_CH)
    lw["c_re"] = w["ssm_c_re"][i]
    lw["c_im"] = w["ssm_c_im"][i]
    return lw


class MeshExchange:
    FIRST = ("ffn1_wi", "ffn1_wo")
    FIRST_LAYER_PLAN = {"ffn1_in": ("w_in", "ssm_w_glu", "w_out", "ffn2_wi", "ffn2_wo"),
                        "ffn1_out": ("ple_w_gate", "ple_w_proj")}
    FWD_PLAN = {"scan": ("ffn1_wi", "ffn1_wo", "w_in", "ssm_w_glu", "w_out"),
                "ffn2_in": ("ffn2_wi", "ffn2_wo"), "ffn2_out": ("ple_w_gate", "ple_w_proj")}
    BWD_PLAN = {"ffn2_in_dx": ("ffn1_wo", "ffn2_wo", "ssm_w_glu", "w_out", "ple_w_gate", "ple_w_proj"),
                "scan": ("ffn1_wi", "ffn2_wi", "w_in")}
    LAST_LAYER_PLAN = {"ffn1_out_dx": ("ffn2_wi", "w_in"),
                       "ffn1_out_dw": ("ffn2_wo", "w_out", "ple_w_gate", "ssm_w_glu", "ple_w_proj"),
                       "ffn1_in_dx": LAST_GRADS}

    def __init__(self, shards):
        self.small = {}
        self.d_final = None
        self.small_parts = []
        self.small_rest = None
        self.swaps = {}
        self.shards = shards
        self.c = lax.axis_index("c")
        self.gathered = {}
        self.chip_sums = {}
        self.from_chips = {}
        self.pending = None
        first = run_exchange(GatherPlan([shards[0][k] for k in self.FIRST]), "gather_first_weights")
        self._store(self.gathered, 0, self.FIRST, first)

    @staticmethod
    def _store(where, layer, names, results):
        for k, r in zip(names, results):
            where[(layer, k)] = r

    def big(self, i, name):
        return _unshard(self.gathered[(i, name)], SHARDED[name], name in HALVED)

    def fwd_exchange(self, i, where):
        if i == 0 and where in self.FIRST_LAYER_PLAN:
            layer, names = 0, self.FIRST_LAYER_PLAN[where]
        elif where in self.FWD_PLAN and i + 1 < DEPTH:
            layer, names = i + 1, self.FWD_PLAN[where]
        else:
            return None
        self.pending = (layer, names)
        return GatherPlan([self.shards[layer][k] for k in names])

    def fwd_done(self, i, where, results):
        if results:
            self._store(self.gathered, *self.pending, results)

    def layer_grads(self, i, grads, names):
        pieces, sends = [], []
        for k in names:
            keep, send = _split_for_scatter(grads[k], SHARDED[k], self.c, k in HALVED)
            keep, send = (keep, send) if isinstance(keep, list) else ([keep], [send])
            pieces.append(keep)
            sends += send
        if i == 0:
            self._add_sibling(0, names, pieces, run_exchange(SiblingSwapPlan(sends), "reduce_core_pair"))
        else:
            self.swaps[(i, tuple(names) == LAST_GRADS)] = (names, pieces, sends)

    def _add_sibling(self, i, names, pieces, from_sibling):
        got = iter(from_sibling)
        for k, keep in zip(names, pieces):
            sums = []
            for part in keep:
                cols = part.shape[-1]
                sums.append(add2(part.reshape(-1, cols), next(got).reshape(-1, cols), f"sum_core_pair_{k}",
                                 BF16).reshape(part.shape))
            self.chip_sums[(i, k)] = sums if len(sums) > 1 else sums[0]

    def small_grads(self, i, grads):
        self.small.setdefault(i, {}).update(grads)

    def final_grad(self, d_final):
        self.d_final = d_final

    def _small_pack(self, d_final, first_norm_grad=None):
        pieces = []
        for k in REPLICATED:
            if k == "final_norm":
                pieces.append(d_final)
            elif k == "ffn1_norm" and first_norm_grad is None:
                pieces.append(jnp.stack([self.small[i][k] for i in range(1, DEPTH)], axis=0))
            else:
                pieces.append(jnp.stack([self.small[i][k] for i in range(DEPTH)], axis=0))
        return _pack(pieces)

    def bwd_exchange(self, i, where):
        if where in self.BWD_PLAN and i + 1 < DEPTH:
            layer, names = i + 1, self.BWD_PLAN[where]
        elif i == 0 and where in self.LAST_LAYER_PLAN:
            layer, names = 0, self.LAST_LAYER_PLAN[where]
        elif i == 0 and where in ("ffn1_in_dw_g", "ffn1_in_dw_u"):
            if where.endswith("g"):
                pack = self._small_pack(self.d_final)
                half = pack.shape[0] // 2 // SUBLANES * SUBLANES
                part, self.small_rest = pack[:half], pack[half:]
            else:
                part = self.small_rest
            self.pending = "small"
            return GatherPlan([part])
        elif i > 0 and where in ("ffn1_out_dx", "ffn1_in_dx"):
            self.pending = ("swap", i, where == "ffn1_in_dx")
            return SiblingSwapPlan(self.swaps[self.pending[1:]][2])
        else:
            return None
        self.pending = (layer, names)
        return ChipScatterPlan([self.chip_sums[(layer, k)] for k in names])

    def bwd_done(self, i, where, results):
        if not results:
            return
        if self.pending == "small":
            self.small_parts += results
        elif self.pending[0] == "swap":
            names, pieces, _ = self.swaps[self.pending[1:]]
            self._add_sibling(self.pending[1], names, pieces, results)
        else:
            self._store(self.from_chips, *self.pending, results)

    def small_gathered(self):
        first = self.small[0]["ffn1_norm"].reshape(SUBLANES, LANES)
        first_all, = run_exchange(GatherPlan([first]), "gather_first_gain_grad")
        return jnp.concatenate([first_all] + self.small_parts, axis=1)


def kernel(x, p, ffn1_norm, ffn1_wi, ffn1_wo, mix_norm, w_in, ssm_lambda_re, ssm_lambda_im, ssm_log_dt, ssm_b_re, ssm_b_im, ssm_c_re, ssm_c_im, ssm_d, ssm_w_glu, pool_w, pool_scale, w_out, ffn2_norm, ffn2_wi, ffn2_wo, ple_norm, ple_w_gate, ple_w_proj, final_norm, loss_target, m_ffn1_norm, m_ffn1_wi, m_ffn1_wo, m_mix_norm, m_w_in, m_ssm_lambda_re, m_ssm_lambda_im, m_ssm_log_dt, m_ssm_b_re, m_ssm_b_im, m_ssm_c_re, m_ssm_c_im, m_ssm_d, m_ssm_w_glu, m_pool_w, m_pool_scale, m_w_out, m_ffn2_norm, m_ffn2_wi, m_ffn2_wo, m_ple_norm, m_ple_w_gate, m_ple_w_proj, m_final_norm, v_ffn1_norm, v_ffn1_wi, v_ffn1_wo, v_mix_norm, v_w_in, v_ssm_lambda_re, v_ssm_lambda_im, v_ssm_log_dt, v_ssm_b_re, v_ssm_b_im, v_ssm_c_re, v_ssm_c_im, v_ssm_d, v_ssm_w_glu, v_pool_w, v_pool_scale, v_w_out, v_ffn2_norm, v_ffn2_wi, v_ffn2_wo, v_ple_norm, v_ple_w_gate, v_ple_w_proj, v_final_norm):
    args = dict(locals())
    wts = {k: args[k] for k in WEIGHTS}
    rep = {k: wts[k] for k in REPLICATED}

    shards = [{k: wts[k][i].astype(BF16) for k in SHARDED} for i in range(DEPTH)]
    exchange = MeshExchange(shards)
    loss_local, grad_x, per_layer, d_final = local_step(x, p, loss_target, rep, exchange)
    loss = lax.psum(loss_local, ("x", "y", "c"))

    outs = {}
    for k in SHARDED:
        shp = wts[k].shape
        cols = shp[-1]
        parts = jnp.stack([exchange.from_chips[(i, k)] for i in range(DEPTH)], axis=1)
        res = adamw(wts[k].reshape(-1, cols), parts.reshape(4, -1, cols), args["m_" + k].reshape(-1, cols),
                    args["v_" + k].reshape(-1, cols), f"adamw_{k}")
        outs[k] = [r.reshape(shp) for r in res]

    rep_shapes = [wts[k].shape for k in REPLICATED]
    all_g = exchange.small_gathered()
    res = adamw(_pack([wts[k] for k in REPLICATED]), all_g, _pack([args["m_" + k] for k in REPLICATED]),
                _pack([args["v_" + k] for k in REPLICATED]), "adamw_small")
    unpacked = [_unpack(r, rep_shapes) for r in res]
    for j, k in enumerate(REPLICATED):
        outs[k] = [unpacked[q][j] for q in range(4)]

    result = [loss, grad_x]
    for q in range(4):
        result += [outs[k][q] for k in WEIGHTS]
    return tuple(result)
```

```python
import math

import jax
import jax.numpy as jnp
from jax import lax
from jax.experimental import pallas as pl
from jax.experimental.pallas import tpu as pltpu

F32 = jnp.float32
BF16 = jnp.bfloat16
MESH = pl.DeviceIdType.MESH
ANY = pl.BlockSpec(memory_space=pl.ANY)

N_DEV = 8
DEPTH = 4
EPS = 1e-6
SSM_GROUPS = 32
SSM_GROUP_CH = 16
SSM_STATE = 64
SSM_CH = SSM_GROUPS * SSM_STATE
SSM_SUPER = 2
POOL_WINDOWS = (2, 4, 8, 16)
POOL_HALO = 16
ADAM_LR, ADAM_B1, ADAM_B2, ADAM_EPS, ADAM_WD, ADAM_STEP = 0.001, 0.9, 0.999, 1e-08, 0.01, 10

V7X_VMEM_BYTES = 64 * 1024 * 1024
VMEM_LIMIT_BYTES = V7X_VMEM_BYTES - 12 * 1024 * 1024
LANES = 128
SUBLANES = 8
PACK = SUBLANES * LANES


def _params(*sem):
    return pltpu.CompilerParams(dimension_semantics=sem or None, vmem_limit_bytes=VMEM_LIMIT_BYTES)


def _tile(n, pref):
    if n <= pref:
        return n
    t = pref - pref % LANES
    while t >= LANES:
        if n % t == 0:
            return t
        t -= LANES
    raise ValueError(f"no lane-aligned tile for {n}")


def _row_tile(rows, pref):
    if rows <= pref:
        return rows
    t = pref - pref % SUBLANES
    while t >= SUBLANES:
        if rows % t == 0:
            return t
        t -= SUBLANES
    raise ValueError(f"no sublane-aligned tile for {rows}")


_DIMS = {"nn": ((1,), (0,)), "nt": ((1,), (1,)), "tn": ((0,), (0,))}


def matmul(a, b, *, mode, name, out_dtypes=None, epi=None, extras=(), separate=False, diag=1, col_sums=0,
           stack_out=False, scale=1.0, exchange=None, single_buffer_weights=False, tm=1024, tn=1024, tk=1024):
    a_list = list(a) if isinstance(a, list) else [a]
    b_list = list(b) if isinstance(b, list) else [b]
    a_planes = [x[1] if isinstance(x, tuple) else None for x in a_list]
    b_planes = [x[1] if isinstance(x, tuple) else None for x in b_list]
    a_list = [x[0] if isinstance(x, tuple) else x for x in a_list]
    b_list = [x[0] if isinstance(x, tuple) else x for x in b_list]
    a_shape, b_shape = a_list[0].shape[-2:], b_list[0].shape[-2:]
    n_terms = max(len(a_list), len(b_list))
    a_idx = [0] * n_terms if len(a_list) == 1 else list(range(n_terms))
    b_idx = [0] * n_terms if len(b_list) == 1 else list(range(n_terms))
    n_acc = n_terms if separate else 1
    assert not (stack_out or scale != 1.0) or epi is None
    if out_dtypes is None:
        out_dtypes = (F32,) * (1 if (epi is not None or stack_out) else n_acc)
    in_place = epi is None
    if mode == "tn":
        K, M = a_shape
        K2, N = b_shape
    elif mode == "nt":
        M, K = a_shape
        N, K2 = b_shape
    else:
        M, K = a_shape
        K2, N = b_shape
    assert K == K2, (name, a_shape, b_shape)
    if mode == "tn":
        tm, tn, tk = _tile(M // diag, tm), _tile(N // diag, tn), _tile(K, tk)
        nk = K // tk
        N = N // diag
        row_tiles, col_tiles = (M // diag) // tm, N // tn
        a_blk, a_map = (tk, tm), lambda i, j, k: (k, i)
        b_blk, b_map = (tk, tn), lambda i, j, k: (k, (i // row_tiles) * col_tiles + j)
    else:
        tm, tn, tk = _tile(M, tm), _tile(N // diag, tn), _tile(K // diag, tk)
        nk = (K // diag) // tk
        col_tiles = (N // diag) // tn
        a_blk, a_map = (tm, tk), lambda i, j, k: (i, (j // col_tiles) * nk + k)
        if mode == "nt":
            b_blk, b_map = (tn, tk), lambda i, j, k: (j, (j // col_tiles) * nk + k)
        else:
            b_blk, b_map = (tk, tn), lambda i, j, k: ((j // col_tiles) * nk + k, j)

    resident = {}
    if single_buffer_weights:
        assert mode != "tn" and N == tn and nk == 1, name
        resident = {"pipeline_mode": pl.Buffered(1)}

    def plane_spec(blk, index_map, plane, **kw):
        if plane is None:
            return pl.BlockSpec(blk, index_map, **kw)
        return pl.BlockSpec((None,) + blk, lambda i, j, k: (plane,) + index_map(i, j, k), **kw)

    a_specs = [plane_spec(a_blk, a_map, p_) for p_ in a_planes]
    b_specs = [plane_spec(b_blk, b_map, p_, **resident) for p_ in b_planes]
    assert not col_sums or N == tn, (name, N, tn)
    ex_specs = []
    for e in extras:
        if e.shape == (M, N):
            ex_specs.append(pl.BlockSpec((tm, tn), lambda i, j, k: (i, j)))
        elif e.shape == (1, N):
            ex_specs.append(pl.BlockSpec((1, tn), lambda i, j, k: (0, j)))
        elif e.shape == (M, 1):
            ex_specs.append(pl.BlockSpec((tm, 1), lambda i, j, k: (i, 0)))
        else:
            raise ValueError((name, e.shape, (M, N)))
    na, nb, ne, no = len(a_list), len(b_list), len(extras), len(out_dtypes)
    dims = (_DIMS[mode], ((), ()))

    n_scratch = n_acc if (nk > 1 and not in_place) else 0
    x_ins, x_out, x_sems = _exchange_args(exchange)
    grid = (M // tm, N // tn, nk)

    def body(*refs):
        a_refs, b_refs, ex_refs, xin, out_refs, sum_refs, xout, acc_refs, xsems = _split_refs(
            refs, na, nb, ne, len(x_ins), no, col_sums, len(x_out), n_scratch)
        if exchange is not None:
            step = (pl.program_id(0) * grid[1] + pl.program_id(1)) * nk + pl.program_id(2)
            _run_exchange(exchange, step, grid[0] * grid[1] * nk, (xin, xout, xsems))
        a_vals = [r[...].astype(BF16) for r in a_refs]
        b_vals = [r[...].astype(BF16) for r in b_refs]
        prods = [lax.dot_general(a_vals[a_idx[t]], b_vals[b_idx[t]], dims, preferred_element_type=F32)
                 for t in range(n_terms)]
        if not separate:
            total = prods[0]
            for p_ in prods[1:]:
                total = total + p_
            prods = [total]

        def finish(accs):
            res = epi(*accs, *[e[...] for e in ex_refs]) if epi is not None else tuple(accs)
            for r, v in zip(out_refs, res[:no]):
                r[...] = v.astype(r.dtype)
            first_rows = pl.program_id(0) == 0
            for r, v in zip(sum_refs, res[no:]):
                @pl.when(first_rows)
                def _(r=r, v=v):
                    r[...] = v

                @pl.when(jnp.logical_not(first_rows))
                def _(r=r, v=v):
                    r[...] += v

        if in_place:
            dst = [(out_refs[0], t) for t in range(n_acc)] if stack_out else [(r, None) for r in out_refs]

            def read(r, t):
                return r[...] if t is None else r[t]

            assert nk == 1 or all(dt == F32 for dt in out_dtypes), name

            def write(r, t, v):
                if t is None:
                    r[...] = v.astype(r.dtype)
                else:
                    r[t] = v.astype(r.dtype)

            if nk == 1:
                for (r, t), v in zip(dst, prods):
                    write(r, t, v * scale if scale != 1.0 else v)
            else:
                k = pl.program_id(2)

                @pl.when(k == 0)
                def _():
                    for (r, t), v in zip(dst, prods):
                        write(r, t, v)

                @pl.when(jnp.logical_and(k > 0, k < nk - 1))
                def _():
                    for (r, t), v in zip(dst, prods):
                        write(r, t, read(r, t) + v)

                @pl.when(k == nk - 1)
                def _():
                    for (r, t), v in zip(dst, prods):
                        total = read(r, t) + v
                        write(r, t, total * scale if scale != 1.0 else total)
        elif nk == 1:
            finish(prods)
        else:
            k = pl.program_id(2)

            @pl.when(k == 0)
            def _():
                for r, v in zip(acc_refs, prods):
                    r[...] = v

            @pl.when(jnp.logical_and(k > 0, k < nk - 1))
            def _():
                for r, v in zip(acc_refs, prods):
                    r[...] += v

            @pl.when(k == nk - 1)
            def _():
                finish([r[...] + v for r, v in zip(acc_refs, prods)])

    if stack_out:
        out_specs = [pl.BlockSpec((n_acc, tm, tn), lambda i, j, k: (0, i, j))]
        out_shape = [jax.ShapeDtypeStruct((n_acc, M, N), F32)]
    else:
        out_specs = [pl.BlockSpec((tm, tn), lambda i, j, k: (i, j))] * no
        out_shape = [jax.ShapeDtypeStruct((M, N), dt) for dt in out_dtypes]
    sequential = col_sums or exchange is not None
    outs = pl.pallas_call(
        body,
        name=name,
        grid=grid,
        in_specs=a_specs + b_specs + ex_specs + [ANY] * len(x_ins),
        out_specs=out_specs + [pl.BlockSpec((1, tn), lambda i, j, k: (0, j))] * col_sums + [ANY] * len(x_out),
        out_shape=out_shape + [jax.ShapeDtypeStruct((1, N), F32)] * col_sums + x_out,
        scratch_shapes=[pltpu.VMEM((tm, tn), F32)] * n_scratch + x_sems,
        compiler_params=_params(*(("arbitrary",) * 3 if sequential else ("parallel", "parallel", "arbitrary"))),
    )(*a_list, *b_list, *extras, *x_ins)
    if exchange is not None:
        n_own = len(outs) - len(x_out)
        return list(outs[:n_own]), list(outs[n_own:])
    return outs


def rowwise(fn, ins, outs, accs=(), *, name, tr=512):
    R = max(x.shape[0] for x in ins)
    tr = _row_tile(R, tr)
    in_specs = []
    for x in ins:
        if x.shape[0] == R and x.ndim == 2:
            in_specs.append(pl.BlockSpec((tr, x.shape[1]), lambda i: (i, 0)))
        else:
            in_specs.append(pl.BlockSpec(x.shape, lambda i, _n=x.ndim: (0,) * _n))
    ni, no = len(ins), len(outs)

    def body(*refs):
        i = pl.program_id(0)
        row_vals, acc_vals = fn(*[r[...] for r in refs[:ni]])
        for r, v in zip(refs[ni:ni + no], row_vals):
            r[...] = v.astype(r.dtype)
        for r, v in zip(refs[ni + no:], acc_vals):
            @pl.when(i == 0)
            def _(r=r, v=v):
                r[...] = v

            @pl.when(i > 0)
            def _(r=r, v=v):
                r[...] += v

    return pl.pallas_call(
        body,
        name=name,
        grid=(R // tr,),
        in_specs=in_specs,
        out_specs=[pl.BlockSpec((tr, c), lambda i: (i, 0)) for c, _ in outs]
        + [pl.BlockSpec(s, lambda i: (0, 0)) for s in accs],
        out_shape=[jax.ShapeDtypeStruct((R, c), dt) for c, dt in outs]
        + [jax.ShapeDtypeStruct(s, F32) for s in accs],
        compiler_params=_params("arbitrary"),
    )(*ins)


def _sigmoid(x):
    return 1.0 / (1.0 + jnp.exp(-x))


_GELU_C = math.sqrt(2.0 / math.pi)


def _gelu(x):
    return 0.5 * x * (1.0 + jnp.tanh(_GELU_C * (x + 0.044715 * (x * x * x))))


def _gelu_grad(x):
    t = jnp.tanh(_GELU_C * (x + 0.044715 * (x * x * x)))
    return 0.5 * (1.0 + t) + 0.5 * x * (1.0 - t * t) * (_GELU_C * (1.0 + 3.0 * 0.044715 * (x * x)))


def rms_fwd(x, g, name):
    def fn(x, g):
        r = lax.rsqrt(jnp.mean(x * x, axis=-1, keepdims=True) + EPS)
        return [x * r * g], []

    return rowwise(fn, [x, g], [(x.shape[1], BF16)], name=name)[0]


def _rms_tile(x, g):
    return x * lax.rsqrt(jnp.mean(x * x, axis=-1, keepdims=True) + EPS) * g


def _rms_bwd_tile(dn, x, g, dres):
    r = lax.rsqrt(jnp.mean(x * x, axis=-1, keepdims=True) + EPS)
    w = dn * g
    dx = r * w - x * (r * r * r) * jnp.mean(x * w, axis=-1, keepdims=True)
    return dres + dx, jnp.sum(dn * (x * r), axis=0, keepdims=True)


def _whole(shape):
    return pl.BlockSpec(shape, lambda: (0,) * len(shape))


def _zoh(lr, li, ldt):
    dt = jnp.exp(ldt)
    mag = jnp.exp(lr * dt)
    ar, ai = mag * jnp.cos(li * dt), mag * jnp.sin(li * dt)
    den = lr * lr + li * li
    kr = ((ar - 1.0) * lr + ai * li) / den
    ki = (ai * lr - (ar - 1.0) * li) / den
    return dt, ar, ai, den, kr, ki


def ssm_prep(lam_re, lam_im, log_dt, b_re, b_im, name):
    n = SSM_CH

    def body(lr_ref, li_ref, ldt_ref, br_ref, bi_ref, ar_ref, ai_ref, bbr_ref, bbi_ref):
        _, ar, ai, _, kr, ki = _zoh(lr_ref[...], li_ref[...], ldt_ref[...])
        br, bi = br_ref[...], bi_ref[...]
        ar_ref[...] = ar
        ai_ref[...] = ai
        bbr_ref[...] = kr * br - ki * bi
        bbi_ref[...] = kr * bi + ki * br

    rows = lam_re.shape[0]
    col, mat = pl.BlockSpec((n, 1), lambda i: (i, 0)), pl.BlockSpec((n, SSM_GROUP_CH), lambda i: (i, 0))
    return pl.pallas_call(
        body, name=name,
        grid=(rows // n,),
        in_specs=[col] * 3 + [mat] * 2,
        out_specs=[col] * 2 + [mat] * 2,
        out_shape=[jax.ShapeDtypeStruct((rows, 1), F32)] * 2 + [jax.ShapeDtypeStruct((rows, SSM_GROUP_CH), F32)] * 2,
        compiler_params=_params("parallel"),
    )(lam_re, lam_im, log_dt, b_re, b_im)


def ssm_prep_bwd(lam_re, lam_im, log_dt, b_re, b_im, d_ar, d_ai, d_bbr, d_bbi, name):
    n = lam_re.shape[0]
    n_groups = n // SSM_STATE

    def body(lr_ref, li_ref, ldt_ref, br_ref, bi_ref, dar_ref, dai_ref, dbr_ref, dbi_ref,
             glr_ref, gli_ref, gdt_ref, gbr_ref, gbi_ref):
        lr, li = lr_ref[...], li_ref[...]
        dt, ar, ai, den, kr, ki = _zoh(lr, li, ldt_ref[...])
        br, bi, dbr, dbi = br_ref[...], bi_ref[...], dbr_ref[...], dbi_ref[...]
        gbr_ref[...] = kr * dbr + ki * dbi
        gbi_ref[...] = kr * dbi - ki * dbr
        gkr = jnp.sum(br * dbr + bi * dbi, axis=1, keepdims=True)
        gki = jnp.sum(br * dbi - bi * dbr, axis=1, keepdims=True)
        gar = dar_ref[...] + (gkr * lr - gki * li) / den
        gai = dai_ref[...] + (gki * lr + gkr * li) / den
        qr, qi = -(kr * lr + ki * li) / den, -(ki * lr - kr * li) / den
        g1r, g1i = qr * gkr + qi * gki, qr * gki - qi * gkr
        g2r, g2i = dt * (ar * gar + ai * gai), dt * (ar * gai - ai * gar)
        glr_ref[...] = g1r + g2r
        gli_ref[...] = g1i + g2i
        pr, pi_ = lr * ar - li * ai, lr * ai + li * ar
        gdt = (pr * gar + pi_ * gai) * dt
        grp = lax.broadcasted_iota(jnp.int32, (n, n_groups), 0) // SSM_STATE
        sel = grp == lax.broadcasted_iota(jnp.int32, (n, n_groups), 1)
        gdt_ref[...] = jnp.sum(jnp.where(sel, gdt, 0.0), axis=0, keepdims=True)

    col, mat = (n, 1), (n, SSM_GROUP_CH)
    return pl.pallas_call(
        body, name=name,
        in_specs=[_whole(col)] * 3 + [_whole(mat)] * 2 + [_whole(col)] * 2 + [_whole(mat)] * 2,
        out_specs=[_whole(col)] * 2 + [_whole((1, n_groups))] + [_whole(mat)] * 2,
        out_shape=[jax.ShapeDtypeStruct(col, F32)] * 2 + [jax.ShapeDtypeStruct((1, n_groups), F32)]
        + [jax.ShapeDtypeStruct(mat, F32)] * 2,
        compiler_params=_params(),
    )(lam_re, lam_im, log_dt, b_re, b_im, d_ar, d_ai, d_bbr, d_bbi)


def _cmul(ar, ai, br, bi):
    return ar * br - ai * bi, ar * bi + ai * br


def _scan_block(xr, xi, lr, li, carry_r, carry_i, or_ref, oi_ref, loc_r, loc_i, reverse):
    tb, cb = xr.shape
    ng = tb // SUBLANES
    xr = xr.reshape(ng, SUBLANES, cb)
    xi = xi.reshape(ng, SUBLANES, cb)
    rid = lax.broadcasted_iota(jnp.int32, (1, SUBLANES, cb), 1)
    pr, pi_ = lr.reshape(1, 1, cb), li.reshape(1, 1, cb)
    powers = []
    for k in (1, 2, 4):
        powers.append((pr, pi_))
        shift = SUBLANES - k if reverse else k
        sr, si = pltpu.roll(xr, shift, 1), pltpu.roll(xi, shift, 1)
        keep = (rid < SUBLANES - k) if reverse else (rid >= k)
        tr_, ti_ = _cmul(jnp.where(keep, pr, 0.0), jnp.where(keep, pi_, 0.0), sr, si)
        xr = xr + tr_
        xi = xi + ti_
        pr, pi_ = _cmul(pr, pi_, pr, pi_)
    loc_r[...] = xr
    loc_i[...] = xi
    (p1r, p1i), (p2r, p2i), (p4r, p4i) = powers
    dist = lax.broadcasted_iota(jnp.int32, (SUBLANES, cb), 0)
    if reverse:
        dist = SUBLANES - 1 - dist
    wr = jnp.broadcast_to(p1r.reshape(1, cb), (SUBLANES, cb))
    wi = jnp.broadcast_to(p1i.reshape(1, cb), (SUBLANES, cb))
    for bit, (qr, qi) in ((1, (p1r, p1i)), (2, (p2r, p2i)), (4, (p4r, p4i))):
        mr, mi = _cmul(wr, wi, qr.reshape(1, cb), qi.reshape(1, cb))
        on = (dist & bit) != 0
        wr, wi = jnp.where(on, mr, wr), jnp.where(on, mi, wi)
    last = 0 if reverse else SUBLANES - 1

    def step(j, carry):
        cr, ci = carry
        g = (ng - 1 - j) if reverse else j
        fr = loc_r[g] + (wr * cr - wi * ci)
        fi = loc_i[g] + (wr * ci + wi * cr)
        rows = pl.ds(pl.multiple_of(g * SUBLANES, SUBLANES), SUBLANES)
        or_ref[rows, :] = fr
        oi_ref[rows, :] = fi
        return fr[last:last + 1, :], fi[last:last + 1, :]

    cr, ci = lax.fori_loop(0, ng, step, (carry_r[...], carry_i[...]))
    carry_r[...] = cr
    carry_i[...] = ci


def _scan_tiles(seq_len, n_ch):
    return min(256, seq_len), min(512, n_ch)


def _run_exchange(plan, step, n_steps, refs):
    @pl.when(step == 0)
    def _():
        plan.start(*refs)

    for part, at in plan.relay_steps(n_steps):
        @pl.when(step == at)
        def _(part=part):
            plan.relay(part, *refs)

    @pl.when(step == n_steps - 1)
    def _():
        plan.finish(*refs)


def _exchange_args(plan):
    if plan is None:
        return [], [], []
    return list(plan.ins), list(plan.out_shape), list(plan.sems)


def _split_refs(refs, *counts):
    groups, at = [], 0
    for n in counts:
        groups.append(refs[at:at + n])
        at += n
    return groups + [refs[at:]]


def ssm_scan(x_re, x_im, lam_re, lam_im, batch, name, exchange=None):
    n, nch = x_re.shape
    seq = n // batch
    tb, cb = _scan_tiles(seq, nch)
    nt, nc = seq // tb, nch // cb
    ex_ins, ex_out, ex_sems = _exchange_args(exchange)

    def body(*refs):
        ins, xin, outs, xout, scratch, xsems = _split_refs(refs, 4, len(ex_ins), 2, len(ex_out), 6)
        xr_ref, xi_ref, lr_ref, li_ref = ins
        or_ref, oi_ref = outs
        car_r, car_i, loc_r, loc_i, s_r, s_i = scratch
        if exchange is not None:
            step = (pl.program_id(0) * batch + pl.program_id(1)) * nt + pl.program_id(2)
            _run_exchange(exchange, step, nc * batch * nt, (xin, xout, xsems))

        @pl.when(pl.program_id(2) == 0)
        def _():
            car_r[...] = jnp.zeros_like(car_r)
            car_i[...] = jnp.zeros_like(car_i)

        _scan_block(xr_ref[...].astype(F32), xi_ref[...].astype(F32), lr_ref[...], li_ref[...], car_r, car_i,
                    s_r, s_i, loc_r, loc_i, reverse=False)
        or_ref[...] = s_r[...].astype(or_ref.dtype)
        oi_ref[...] = s_i[...].astype(oi_ref.dtype)

    blk = pl.BlockSpec((tb, cb), lambda c, b, t: (b * nt + t, c))
    lam_spec = pl.BlockSpec((1, cb), lambda c, b, t: (0, c))
    res = pl.pallas_call(
        body, name=name,
        grid=(nc, batch, nt),
        in_specs=[blk, blk, lam_spec, lam_spec] + [ANY] * len(ex_ins),
        out_specs=[blk, blk] + [ANY] * len(ex_out),
        out_shape=[jax.ShapeDtypeStruct((n, nch), BF16)] * 2 + ex_out,
        scratch_shapes=[pltpu.VMEM((1, cb), F32)] * 2 + [pltpu.VMEM((tb // SUBLANES, SUBLANES, cb), F32)] * 2
        + [pltpu.VMEM((tb, cb), F32)] * 2 + ex_sems,
        compiler_params=_params("arbitrary", "arbitrary", "arbitrary"),
    )(x_re, x_im, lam_re, lam_im, *ex_ins)
    return res[0], res[1], list(res[2:])


def ssm_scan_bwd(d_re, d_im, s_re, s_im, lam_re, lam_im, batch, name, exchange=None):
    n, nch = d_re.shape
    seq = n // batch
    tb, cb = _scan_tiles(seq, nch)
    nt, nc = seq // tb, nch // cb
    halo_rows = 2 * SUBLANES
    hb = tb // halo_rows
    ex_ins, ex_out, ex_sems = _exchange_args(exchange)

    def body(*refs):
        ins, xin, outs, xout, scratch, xsems = _split_refs(refs, 8, len(ex_ins), 4, len(ex_out), 6)
        xr_ref, xi_ref, sr_ref, si_ref, hr_ref, hi_ref, lr_ref, li_ref = ins
        or_ref, oi_ref, dlr_ref, dli_ref = outs
        car_r, car_i, loc_r, loc_i, g_r, g_i = scratch
        b, t = pl.program_id(1), pl.program_id(2)
        if exchange is not None:
            step = (pl.program_id(0) * batch + b) * nt + t
            _run_exchange(exchange, step, nc * batch * nt, (xin, xout, xsems))

        @pl.when(t == 0)
        def _():
            car_r[...] = jnp.zeros_like(car_r)
            car_i[...] = jnp.zeros_like(car_i)

        _scan_block(xr_ref[...].astype(F32), xi_ref[...].astype(F32), lr_ref[...], -li_ref[...], car_r, car_i,
                    g_r, g_i, loc_r, loc_i, reverse=True)
        gr, gi = g_r[...], g_i[...]
        or_ref[...] = gr.astype(or_ref.dtype)
        oi_ref[...] = gi.astype(oi_ref.dtype)
        first_block = t == nt - 1
        row = lax.broadcasted_iota(jnp.int32, (tb, cb), 0)
        hr = jnp.where(first_block, 0.0, hr_ref[...].astype(F32)[halo_rows - 1:halo_rows, :])
        hi = jnp.where(first_block, 0.0, hi_ref[...].astype(F32)[halo_rows - 1:halo_rows, :])
        pr = jnp.where(row == 0, hr, pltpu.roll(sr_ref[...].astype(F32), 1, 0))
        pi_ = jnp.where(row == 0, hi, pltpu.roll(si_ref[...].astype(F32), 1, 0))
        dlr = jnp.sum(gr * pr + gi * pi_, axis=0, keepdims=True)
        dli = jnp.sum(gi * pr - gr * pi_, axis=0, keepdims=True)
        start = jnp.logical_and(b == 0, t == 0)

        @pl.when(start)
        def _():
            dlr_ref[...] = dlr
            dli_ref[...] = dli

        @pl.when(jnp.logical_not(start))
        def _():
            dlr_ref[...] += dlr
            dli_ref[...] += dli

    def blk(c, b, t):
        return b * nt + (nt - 1 - t)

    st_spec = pl.BlockSpec((tb, cb), lambda c, b, t: (blk(c, b, t), c))
    halo_spec = pl.BlockSpec((halo_rows, cb), lambda c, b, t: (jnp.maximum(blk(c, b, t) * hb - 1, 0), c))
    row_spec = pl.BlockSpec((1, cb), lambda c, b, t: (0, c))
    res = pl.pallas_call(
        body, name=name,
        grid=(nc, batch, nt),
        in_specs=[st_spec] * 4 + [halo_spec] * 2 + [row_spec] * 2 + [ANY] * len(ex_ins),
        out_specs=[st_spec, st_spec, row_spec, row_spec] + [ANY] * len(ex_out),
        out_shape=[jax.ShapeDtypeStruct((n, nch), BF16)] * 2 + [jax.ShapeDtypeStruct((1, nch), F32)] * 2 + ex_out,
        scratch_shapes=[pltpu.VMEM((1, cb), F32)] * 2 + [pltpu.VMEM((tb // SUBLANES, SUBLANES, cb), F32)] * 2
        + [pltpu.VMEM((tb, cb), F32)] * 2 + ex_sems,
        compiler_params=_params("arbitrary", "arbitrary", "arbitrary"),
    )(d_re, d_im, s_re, s_im, s_re, s_im, lam_re, lam_im, *ex_ins)
    return res[0], res[1], res[2], res[3], list(res[4:])


def _pool_tiles(seq_len):
    return min(512, seq_len)


def _window_sums(x, n_steps, forward_in_time):
    rows = x.shape[0]
    k = 1
    for _ in range(n_steps):
        x = x + pltpu.roll(x, k if forward_in_time else rows - k, 0)
        k *= 2
    return x


def pool_fwd(u, w_pool, scale, batch, name):
    n, c = u.shape
    seq = n // batch
    tb = _pool_tiles(seq)
    nt = seq // tb
    gc = c // len(POOL_WINDOWS)
    hb = tb // POOL_HALO

    def body(x_ref, halo_ref, w_ref, sc_ref, y_ref, q_ref):
        t = pl.program_id(1)
        halo = jnp.where(t == 0, 0.0, halo_ref[...])
        full = jnp.concatenate([halo, x_ref[...]], axis=0)
        pos = lax.broadcasted_iota(jnp.int32, (tb, gc), 0) + t * tb + 1
        for gi, win in enumerate(POOL_WINDOWS):
            cols = slice(gi * gc, (gi + 1) * gc)
            sums = _window_sums(full[:, cols], gi + 1, True)[POOL_HALO:, :]
            cnt = jnp.minimum(pos, win).astype(F32)
            q = sums / cnt - x_ref[:, cols]
            r = jnp.dot(q.astype(BF16), w_ref[gi].astype(BF16), preferred_element_type=F32)
            q_ref[:, cols] = q.astype(q_ref.dtype)
            y_ref[:, cols] = (r * sc_ref[:, cols]).astype(y_ref.dtype)

    return pl.pallas_call(
        body, name=name,
        grid=(batch, nt),
        in_specs=[pl.BlockSpec((tb, c), lambda b, t: (b * nt + t, 0)),
                  pl.BlockSpec((POOL_HALO, c), lambda b, t: (jnp.maximum((b * nt + t) * hb - 1, 0), 0)),
                  pl.BlockSpec(w_pool.shape, lambda b, t: (0, 0, 0)),
                  pl.BlockSpec((1, c), lambda b, t: (0, 0))],
        out_specs=[pl.BlockSpec((tb, c), lambda b, t: (b * nt + t, 0))] * 2,
        out_shape=[jax.ShapeDtypeStruct((n, c), BF16)] * 2,
        compiler_params=_params("parallel", "arbitrary"),
    )(u, u, w_pool, scale)


def pool_bwd(dy, q, w_pool, scale, batch, name):
    n, c = dy.shape
    seq = n // batch
    tb = _pool_tiles(seq)
    nt = seq // tb
    ng = len(POOL_WINDOWS)
    gc = c // ng
    hb = tb // POOL_HALO
    n_blocks = n // POOL_HALO

    def body(dy_ref, dyh_ref, q_ref, w_ref, sc_ref, du_ref, dw_ref, dsc_ref):
        b, t = pl.program_id(0), pl.program_id(1)
        last = t == nt - 1
        dy_full = jnp.concatenate([dy_ref[...], jnp.where(last, 0.0, dyh_ref[...])], axis=0)
        pos = lax.broadcasted_iota(jnp.int32, (tb + POOL_HALO, gc), 0) + t * tb + 1
        start = jnp.logical_and(b == 0, t == 0)
        for gi, win in enumerate(POOL_WINDOWS):
            cols = slice(gi * gc, (gi + 1) * gc)
            w = w_ref[gi].astype(BF16)
            dr = dy_full[:, cols] * sc_ref[:, cols]
            dq = lax.dot_general(dr.astype(BF16), w, (((1,), (1,)), ((), ())), preferred_element_type=F32)
            cnt = jnp.minimum(pos, win).astype(F32)
            back = _window_sums(dq / cnt, gi + 1, False)
            du_ref[:, cols] = back[:tb, :] - dq[:tb, :]
            qb = q_ref[:, cols]
            r = jnp.dot(qb, w, preferred_element_type=F32)
            dw = lax.dot_general(qb, dr[:tb, :].astype(BF16), (((0,), (0,)), ((), ())), preferred_element_type=F32)
            dsc = jnp.sum(dy_ref[:, cols] * r, axis=0, keepdims=True)

            @pl.when(start)
            def _(gi=gi, cols=cols, dw=dw, dsc=dsc):
                dw_ref[gi] = dw
                dsc_ref[:, cols] = dsc

            @pl.when(jnp.logical_not(start))
            def _(gi=gi, cols=cols, dw=dw, dsc=dsc):
                dw_ref[gi] += dw
                dsc_ref[:, cols] += dsc

    blk = pl.BlockSpec((tb, c), lambda b, t: (b * nt + t, 0))
    halo = pl.BlockSpec((POOL_HALO, c), lambda b, t: (jnp.minimum((b * nt + t + 1) * hb, n_blocks - 1), 0))
    return pl.pallas_call(
        body, name=name,
        grid=(batch, nt),
        in_specs=[blk, halo, blk,
                  pl.BlockSpec(w_pool.shape, lambda b, t: (0, 0, 0)),
                  pl.BlockSpec((1, c), lambda b, t: (0, 0))],
        out_specs=[blk, pl.BlockSpec(w_pool.shape, lambda b, t: (0, 0, 0)), pl.BlockSpec((1, c), lambda b, t: (0, 0))],
        out_shape=[jax.ShapeDtypeStruct((n, c), F32), jax.ShapeDtypeStruct(w_pool.shape, F32),
                   jax.ShapeDtypeStruct((1, c), F32)],
        compiler_params=_params("arbitrary", "arbitrary"),
    )(dy, dy, q, w_pool, scale)


def _place():
    return lax.axis_index("x"), lax.axis_index("y"), lax.axis_index("c")


class GatherPlan:
    def __init__(self, arrs):
        self.ins = list(arrs)
        na = len(arrs)
        self.out_shape = [jax.ShapeDtypeStruct((N_DEV,) + a.shape, a.dtype) for a in arrs]
        self.sems = [pltpu.SemaphoreType.DMA((na, 7)), pltpu.SemaphoreType.DMA((na, 7)), pltpu.SemaphoreType.DMA((na,))]
        self.sizes = [math.prod(a.shape) * a.dtype.itemsize for a in arrs]

    def relay_steps(self, n_steps):
        total, done, steps = sum(self.sizes), 0, []
        for a, size in enumerate(self.sizes):
            done += size
            steps.append((a, min(n_steps - 1, (done * (n_steps - 1)) // total)))
        return steps

    def _copy(self, outs, sems, a, k, block, to, src=None):
        dst = outs[a].at[4 * block[0] + 2 * block[1] + block[2]]
        return pltpu.make_async_remote_copy(
            src_ref=dst if src is None else src, dst_ref=dst,
            send_sem=sems[0].at[a, k], recv_sem=sems[1].at[a, k], device_id=to, device_id_type=MESH)

    @staticmethod
    def _chips(x, y):
        return [(1 - x, y), (x, 1 - y), (1 - x, 1 - y)]

    def _local(self, ins, outs, sems, a, me):
        return pltpu.make_async_copy(ins[a], outs[a].at[4 * me[0] + 2 * me[1] + me[2]], sems[2].at[a])

    def start(self, ins, outs, sems):
        x, y, c = _place()
        me = (x, y, c)
        for a in range(len(ins)):
            self._local(ins, outs, sems, a, me).start()
            self._copy(outs, sems, a, 0, me, (x, y, 1 - c), src=ins[a]).start()
            for j, chip in enumerate(self._chips(x, y)):
                self._copy(outs, sems, a, 1 + j, me, (*chip, c), src=ins[a]).start()

    def relay(self, a, ins, outs, sems):
        x, y, c = _place()
        for j, chip in enumerate(self._chips(x, y)):
            self._copy(outs, sems, a, 1 + j, (*chip, c), (x, y, c)).wait_recv()
            self._copy(outs, sems, a, 4 + j, (*chip, c), (x, y, 1 - c)).start()

    def finish(self, ins, outs, sems):
        x, y, c = _place()
        me, sibling = (x, y, c), (x, y, 1 - c)
        for a in range(len(ins)):
            self._copy(outs, sems, a, 0, sibling, me).wait_recv()
            for j, chip in enumerate(self._chips(x, y)):
                self._copy(outs, sems, a, 4 + j, (*chip, 1 - c), me).wait_recv()
        for a in range(len(ins)):
            self._copy(outs, sems, a, 0, me, sibling, src=ins[a]).wait_send()
            for j, chip in enumerate(self._chips(x, y)):
                self._copy(outs, sems, a, 1 + j, me, (*chip, c), src=ins[a]).wait_send()
                self._copy(outs, sems, a, 4 + j, (*chip, c), sibling).wait_send()
            self._local(ins, outs, sems, a, me).wait()


class ChipScatterPlan:
    def __init__(self, arrs):
        self.groups = [list(a) if isinstance(a, list) else [a] for a in arrs]
        self.ins = [piece for group in self.groups for piece in group]
        self.first = [sum(len(g) for g in self.groups[:a]) for a in range(len(self.groups))]
        na = len(arrs)
        self.out_shape = [jax.ShapeDtypeStruct((4,) + g[0].shape[1:], g[0].dtype) for g in self.groups]
        self.sems = [pltpu.SemaphoreType.DMA((na, 3)), pltpu.SemaphoreType.DMA((na, 3)), pltpu.SemaphoreType.DMA((na,))]

    def relay_steps(self, n_steps):
        return []

    def _row(self, ins, a, px, py):
        if len(self.groups[a]) == 1:
            return ins[self.first[a]].at[2 * px + py]
        return ins[self.first[a] + px].at[py]

    def start(self, ins, outs, sems):
        x, y, c = _place()
        mine = 2 * x + y
        for xs in (0, 1):
            @pl.when(x == xs)
            def _(xs=xs):
                for a in range(len(self.groups)):
                    pltpu.make_async_copy(self._row(ins, a, xs, y), outs[a].at[mine], sems[2].at[a]).start()
                    for j, (px, py) in enumerate([(1 - xs, y), (xs, 1 - y), (1 - xs, 1 - y)]):
                        pltpu.make_async_remote_copy(
                            src_ref=self._row(ins, a, px, py), dst_ref=outs[a].at[mine],
                            send_sem=sems[0].at[a, j], recv_sem=sems[1].at[a, j],
                            device_id=(px, py, c), device_id_type=MESH).start()

    def finish(self, ins, outs, sems):
        x, y, c = _place()
        for wait_recv in (True, False):
            for a in range(len(self.groups)):
                for j in range(3):
                    cp = pltpu.make_async_remote_copy(
                        src_ref=self._row(ins, a, 0, 0), dst_ref=outs[a].at[0],
                        send_sem=sems[0].at[a, j], recv_sem=sems[1].at[a, j],
                        device_id=(x, y, c), device_id_type=MESH)
                    if wait_recv:
                        cp.wait_recv()
                    else:
                        cp.wait_send()
        for a in range(len(self.groups)):
            pltpu.make_async_copy(self._row(ins, a, 0, 0), outs[a].at[0], sems[2].at[a]).wait()


def run_exchange(plan, name):
    n_in, n_out = len(plan.ins), len(plan.out_shape)

    def body(*refs):
        parts = (refs[:n_in], refs[n_in:n_in + n_out], refs[n_in + n_out:])
        plan.start(*parts)
        for part, _ in plan.relay_steps(1):
            plan.relay(part, *parts)
        plan.finish(*parts)

    return pl.pallas_call(
        body, name=name,
        in_specs=[ANY] * n_in, out_specs=[ANY] * n_out,
        out_shape=plan.out_shape, scratch_shapes=plan.sems,
    )(*plan.ins)


class SiblingSwapPlan:
    def __init__(self, arrs):
        self.ins = list(arrs)
        na = len(arrs)
        self.out_shape = [jax.ShapeDtypeStruct(a.shape, a.dtype) for a in arrs]
        self.sems = [pltpu.SemaphoreType.DMA((na,)), pltpu.SemaphoreType.DMA((na,))]

    def relay_steps(self, n_steps):
        return []

    @staticmethod
    def _copies(ins, outs, sems):
        x, y, c = _place()
        return [pltpu.make_async_remote_copy(
            src_ref=ins[a], dst_ref=outs[a], send_sem=sems[0].at[a], recv_sem=sems[1].at[a],
            device_id=(x, y, 1 - c), device_id_type=MESH) for a in range(len(ins))]

    def start(self, ins, outs, sems):
        for cp in self._copies(ins, outs, sems):
            cp.start()

    def finish(self, ins, outs, sems):
        for cp in self._copies(ins, outs, sems):
            cp.wait()


def adamw(w, gparts, m, v, name, tr=256):
    rows, cols = w.shape
    parts = gparts.shape[0]
    tr = _row_tile(rows, tr)
    c1 = 1.0 - ADAM_B1 ** ADAM_STEP
    c2 = 1.0 - ADAM_B2 ** ADAM_STEP

    def body(w_ref, g_ref, m_ref, v_ref, go_ref, d_ref, mo_ref, vo_ref):
        g = g_ref[0].astype(F32)
        for p_ in range(1, parts):
            g = g + g_ref[p_].astype(F32)
        m_new = ADAM_B1 * m_ref[...] + (1.0 - ADAM_B1) * g
        v_new = ADAM_B2 * v_ref[...] + (1.0 - ADAM_B2) * (g * g)
        m_hat = m_new / c1
        v_hat = v_new / c2
        go_ref[...] = g
        d_ref[...] = -ADAM_LR * (m_hat / (jnp.sqrt(v_hat) + ADAM_EPS) + ADAM_WD * w_ref[...])
        mo_ref[...] = m_new
        vo_ref[...] = v_new

    blk = pl.BlockSpec((tr, cols), lambda i: (i, 0))
    return pl.pallas_call(
        body, name=name,
        grid=(rows // tr,),
        in_specs=[blk, pl.BlockSpec((parts, tr, cols), lambda i: (0, i, 0)), blk, blk],
        out_specs=[blk] * 4,
        out_shape=[jax.ShapeDtypeStruct((rows, cols), F32)] * 4,
        compiler_params=_params("parallel"),
    )(w, gparts, m, v)


def add2(a, b, name, out_dtype):
    return rowwise(lambda a, b: ([a.astype(F32) + b.astype(F32)], []), [a, b], [(a.shape[1], out_dtype)],
                   name=name, tr=256)[0]


def _block_diag(x):
    g, a, b = x.shape
    tiled = jnp.tile(x.reshape(g * a, b), (1, g))
    rows = lax.broadcasted_iota(jnp.int32, (g * a, g * b), 0) // a
    cols = lax.broadcasted_iota(jnp.int32, (g * a, g * b), 1) // b
    return jnp.where(rows == cols, tiled, jnp.zeros_like(tiled))


def _diag_blocks(x, a, b):
    per = x.shape[1] // b
    x5 = x.reshape(SSM_SUPER, per, a, per, b)
    eye = jnp.eye(per, dtype=x.dtype)
    return jnp.sum(x5 * eye[None, :, None, :, None], axis=3).reshape(SSM_SUPER * per, a, b)


def _swiglu_epi(g, u):
    s = _sigmoid(g)
    silu = g * s
    return u * (s * (1.0 + g * (1.0 - s))), silu, silu * u


def _residual_epi(scale, with_norm):
    if with_norm:
        def epi(acc, res, gain):
            out = res + scale * acc
            return out, _rms_tile(out, gain)
    else:
        def epi(acc, res):
            return (res + scale * acc,)
    return epi


def _next_norm(next_gain):
    if next_gain is None:
        return [], (F32,)
    return [next_gain], (F32, BF16)


FFN_WIDE = 2816


def ffn_fwd(h, n, wi, wo, next_gain, tag, carry=None):
    if carry is None:
        carry = lambda where, run: run(None)[0]
    dact_g, dact_u, act = carry(f"{tag}_in", lambda ex: _with_exchange(matmul(
        n, [(wi, 0), (wi, 1)], mode="nn", name=f"{tag}_in", separate=True, epi=_swiglu_epi,
        out_dtypes=(BF16, BF16, BF16), exchange=ex, tm=512, tn=FFN_WIDE), ex))
    more, dtypes = _next_norm(next_gain)
    res = carry(f"{tag}_out", lambda ex: _with_exchange(matmul(
        act, wo, mode="nn", name=f"{tag}_out", epi=_residual_epi(0.5, bool(more)), extras=[h] + more,
        out_dtypes=dtypes, exchange=ex, single_buffer_weights=True, tm=1024, tk=FFN_WIDE), ex))
    return res[0], (res[1] if more else None), (h, n, dact_g, dact_u, act)


def ffn_bwd(dh, saved, gain, wi, wo, tag, carry=None, gate=None, on_weight_grads=None):
    h, n, dact_g, dact_u, act = saved
    if carry is None:
        carry = lambda where, run: run(None)[0]
    dg, du = carry(f"{tag}_out_dx", lambda ex: _with_exchange(matmul(
        dh, wo, mode="nt", name=f"{tag}_out_dx", extras=[dact_g, dact_u], out_dtypes=(BF16, BF16),
        epi=lambda acc, fg, fu: (0.5 * acc * fg, 0.5 * acc * fu), exchange=ex, tm=512, tn=FFN_WIDE), ex))
    d_wo, = carry(f"{tag}_out_dw", lambda ex: _with_exchange(matmul(
        act, dh, mode="tn", name=f"{tag}_out_dw", scale=0.5, exchange=ex, tm=FFN_WIDE), ex))
    d_wi = [carry(f"{tag}_in_dw_{k}", lambda ex, half=half: _with_exchange(matmul(
        n, half, mode="tn", name=f"{tag}_in_dw", exchange=ex, tn=FFN_WIDE), ex))[0] for k, half in (("g", dg), ("u", du))]
    if on_weight_grads is not None:
        on_weight_grads(d_wi, d_wo)
    if gate is None:
        epi, more, dtypes = _rms_bwd_tile, [], (F32,)
    else:
        def epi(acc, x, g, dres, e, pre):
            dh_new, d_gain = _rms_bwd_tile(acc, x, g, dres)
            return (dh_new, *_gate_cotangents(dh_new, e, pre), d_gain)
        more, dtypes = list(gate), (F32, BF16, BF16)
    res = carry(f"{tag}_in_dx", lambda ex: _with_exchange(matmul(
        [dg, du], [(wi, 0), (wi, 1)], mode="nt", name=f"{tag}_in_dx", epi=epi, extras=[h, gain, dh] + more,
        out_dtypes=dtypes, col_sums=1, exchange=ex, single_buffer_weights=True, tm=512, tk=FFN_WIDE), ex))
    return res[0], res[-1], d_wi, d_wo, tuple(res[1:-1])


def _with_exchange(result, exchange):
    return result if exchange is not None else (result, [])


def ssm_tables(rep):
    rows = DEPTH * SSM_CH
    lam_r, lam_i, bb_r, bb_i = ssm_prep(
        rep["ssm_lambda_re"].reshape(rows, 1), rep["ssm_lambda_im"].reshape(rows, 1),
        jnp.repeat(rep["ssm_log_dt"].reshape(-1), SSM_STATE).reshape(rows, 1),
        rep["ssm_b_re"].reshape(rows, SSM_GROUP_CH), rep["ssm_b_im"].reshape(rows, SSM_GROUP_CH), "ssm_zoh")
    lam = (lam_r.reshape(DEPTH, 1, SSM_CH), lam_i.reshape(DEPTH, 1, SSM_CH))

    def dense(x):
        return jax.vmap(_block_diag)(x.astype(BF16))

    b_mats = [dense(bb.reshape(DEPTH, SSM_GROUPS, SSM_STATE, SSM_GROUP_CH).transpose(0, 1, 3, 2)) for bb in (bb_r, bb_i)]
    c_mats = [dense(cc.transpose(0, 1, 3, 2)) for cc in (rep["ssm_c_re"], -rep["ssm_c_im"])]
    return lam, b_mats, c_mats


def mix_fwd(h, n, lw, tables, layer, batch, next_gain, tag, exchange=None):
    sw = h.shape[1] // 2
    us, up = matmul(n, [lw["w_in"][:, :sw], lw["w_in"][:, sw:]], mode="nn", name=f"{tag}_in", separate=True,
                    out_dtypes=(F32, F32))
    lam = tuple(t[layer] for t in tables[0])
    b_mats = [(m, layer) for m in tables[1]]
    c_mats = [(m, layer) for m in tables[2]]
    bu_re, bu_im = matmul(us, b_mats, mode="nn", name=f"{tag}_bu", separate=True, diag=SSM_SUPER,
                          out_dtypes=(BF16, BF16))
    s_re, s_im, exchanged = ssm_scan(bu_re, bu_im, *lam, batch, f"{tag}_scan", exchange)
    y0, y1 = matmul([s_re, s_im], c_mats, mode="nn", name=f"{tag}_c", diag=SSM_SUPER,
                    epi=lambda acc, u, d: (acc + d * u, _gelu(acc + d * u)), extras=[us, lw["ssm_d"]],
                    out_dtypes=(F32, BF16))
    y2, gl = matmul(y1, lw["w_glu"], mode="nn", name=f"{tag}_glu",
                    epi=lambda acc, y0: (_gelu(y0) * _sigmoid(acc), acc), extras=[y0], out_dtypes=(BF16, F32))
    yp, q = pool_fwd(up, lw["pool_w"], lw["pool_scale"], batch, f"{tag}_pool")
    more, dtypes = _next_norm(next_gain)
    res = matmul([y2, yp], [lw["w_out"][:sw], lw["w_out"][sw:]], mode="nn", name=f"{tag}_out",
                 epi=_residual_epi(1.0, bool(more)), extras=[h] + more, out_dtypes=dtypes)
    saved = (h, n, us, lam, b_mats, c_mats, s_re, s_im, y0, y1, gl, y2, yp, q)
    return res[0], (res[1] if more else None), saved, exchanged


def mix_bwd(dh, saved, lw, batch, tag, exchange=None):
    h, n, us, lam, b_mats, c_mats, s_re, s_im, y0, y1, gl, y2, yp, q = saved
    sw = h.shape[1] // 2
    w_out_s, w_out_p = lw["w_out"][:sw], lw["w_out"][sw:]
    d_wo_s, d_wo_p = matmul([y2, yp], dh, mode="tn", name=f"{tag}_out_dw", separate=True)
    def out_dx_epi(dy2, dyp, y0, gl):
        sg = _sigmoid(gl)
        return dy2, dyp, dy2 * _gelu(y0) * sg * (1.0 - sg)

    dy2, dyp, tg = matmul(dh, [w_out_s, w_out_p], mode="nt", name=f"{tag}_out_dx", separate=True, epi=out_dx_epi,
                          extras=[y0, gl], out_dtypes=(F32, F32, BF16))
    dup, d_pool_w, d_pool_scale = pool_bwd(dyp, q, lw["pool_w"], lw["pool_scale"], batch, f"{tag}_pool_bwd")
    def dy0_epi(acc, dy2, gl, y0, u):
        dy0 = (acc + dy2 * _sigmoid(gl)) * _gelu_grad(y0)
        return dy0, jnp.sum(dy0 * u, axis=0, keepdims=True)

    dy0, d_d = matmul(tg, lw["w_glu"], mode="nt", name=f"{tag}_glu_dx", epi=dy0_epi, extras=[dy2, gl, y0, us],
                      out_dtypes=(F32,), col_sums=1)
    d_w_glu, = matmul(y1, tg, mode="tn", name=f"{tag}_glu_dw")
    gd_re, gd_im = matmul(dy0, c_mats, mode="nt", name=f"{tag}_c_dx", separate=True, diag=SSM_SUPER,
                          out_dtypes=(BF16, BF16))
    d_c_top, d_c_bot = matmul([s_re, s_im], dy0, mode="tn", name=f"{tag}_c_dw", separate=True, diag=SSM_SUPER)
    g_re, g_im, d_lam_r, d_lam_i, exchanged = ssm_scan_bwd(gd_re, gd_im, s_re, s_im, *lam, batch, f"{tag}_scan_bwd",
                                                           exchange)
    dus, = matmul([g_re, g_im], b_mats, mode="nt", name=f"{tag}_bu_dx", diag=SSM_SUPER,
                  epi=lambda acc, dy0, d: (acc + d * dy0,), extras=[dy0, lw["ssm_d"]])
    d_b_re, d_b_im = matmul(us, [g_re, g_im], mode="tn", name=f"{tag}_bu_dw", separate=True, diag=SSM_SUPER)
    d_bb_r = _diag_blocks(d_b_re, SSM_GROUP_CH, SSM_STATE).transpose(0, 2, 1).reshape(SSM_CH, SSM_GROUP_CH)
    d_bb_i = _diag_blocks(d_b_im, SSM_GROUP_CH, SSM_STATE).transpose(0, 2, 1).reshape(SSM_CH, SSM_GROUP_CH)
    d_lr, d_li, d_ldt, d_br, d_bi = ssm_prep_bwd(
        lw["lam_re"], lw["lam_im"], lw["log_dt"], lw["b_re"], lw["b_im"],
        d_lam_r.reshape(SSM_CH, 1), d_lam_i.reshape(SSM_CH, 1), d_bb_r, d_bb_i, f"{tag}_zoh_bwd")
    d_c_re = _diag_blocks(d_c_top, SSM_STATE, SSM_GROUP_CH).transpose(0, 2, 1)
    d_c_im = -_diag_blocks(d_c_bot, SSM_STATE, SSM_GROUP_CH).transpose(0, 2, 1)
    d_w_in_s, d_w_in_p = matmul(n, [dus, dup], mode="tn", name=f"{tag}_in_dw", separate=True)
    dh_new, d_gain = matmul([dus, dup], [lw["w_in"][:, :sw], lw["w_in"][:, sw:]], mode="nt", name=f"{tag}_in_dx",
                            epi=_rms_bwd_tile, extras=[h, lw["mix_norm"], dh], out_dtypes=(F32,), col_sums=1, tm=512)
    grads = dict(mix_norm=d_gain, w_in=jnp.concatenate([d_w_in_s, d_w_in_p], axis=1),
                 ssm_lambda_re=d_lr, ssm_lambda_im=d_li, ssm_log_dt=d_ldt, ssm_b_re=d_br, ssm_b_im=d_bi,
                 ssm_c_re=d_c_re, ssm_c_im=d_c_im, ssm_d=d_d, ssm_w_glu=d_w_glu, pool_w=d_pool_w,
                 pool_scale=d_pool_scale, w_out=jnp.concatenate([d_wo_s, d_wo_p], axis=0))
    return dh_new, grads, exchanged


def ple_fwd(h, n, p, w_gate, w_proj, next_gain, tag):
    e, = matmul(p, w_proj, mode="nn", name=f"{tag}_proj")
    if next_gain is None:
        def epi(acc, e, res):
            return res + _sigmoid(acc) * e, acc
        more, dtypes = [], (F32, F32)
    else:
        def epi(acc, e, res, gain):
            out = res + _sigmoid(acc) * e
            return out, acc, _rms_tile(out, gain)
        more, dtypes = [next_gain], (F32, F32, BF16)
    res = matmul(n, w_gate, mode="nn", name=f"{tag}_gate", epi=epi, extras=[e, h] + more, out_dtypes=dtypes, tm=512)
    return res[0], (res[2] if more else None), (h, n, e, res[1])


def _gate_cotangents(dh, e, pre):
    s = _sigmoid(pre)
    return dh * e * s * (1.0 - s), dh * s


def ple_bwd(dh, dpre, de, saved, p, gain, w_gate, tag):
    h, n, e, pre = saved
    d_w_gate, = matmul(n, dpre, mode="tn", name=f"{tag}_gate_dw")
    d_w_proj, = matmul(p, de, mode="tn", name=f"{tag}_proj_dw")
    dh_new, d_gain = matmul(dpre, w_gate, mode="nt", name=f"{tag}_gate_dx", epi=_rms_bwd_tile,
                            extras=[h, gain, dh], out_dtypes=(F32,), col_sums=1, tm=512)
    return dh_new, d_gain, d_w_gate, d_w_proj


def loss_head(h, gain, target, e, pre, name):
    d = h.shape[1]

    def fn(h, g, t, e, pre):
        r = lax.rsqrt(jnp.mean(h * h, axis=-1, keepdims=True) + EPS)
        diff = h * r * g - t
        sq = jnp.sum(jnp.sum(diff * diff, axis=1, keepdims=True), axis=0, keepdims=True)
        dy = diff * (1.0 / d)
        w = dy * g
        dh = r * w - h * (r * r * r) * jnp.mean(h * w, axis=-1, keepdims=True)
        return [dh, *_gate_cotangents(dh, e, pre)], [sq, jnp.sum(dy * (h * r), axis=0, keepdims=True)]

    dh, dpre, de, sq, d_gain = rowwise(fn, [h, gain, target, e, pre], [(d, F32), (d, BF16), (d, BF16)],
                                       [(1, 1), (1, d)], name=name, tr=256)
    return 0.5 / d * sq[0, 0], dh, dpre, de, d_gain


SHARDED = {
    "ffn1_wi": 1, "ffn1_wo": 0, "w_in": 0, "ssm_w_glu": 0, "w_out": 0, "ffn2_wi": 1, "ffn2_wo": 0,
    "ple_w_gate": 0, "ple_w_proj": 1,
}
WEIGHTS = ["ffn1_norm", "ffn1_wi", "ffn1_wo", "mix_norm", "w_in", "ssm_lambda_re", "ssm_lambda_im", "ssm_log_dt",
           "ssm_b_re", "ssm_b_im", "ssm_c_re", "ssm_c_im", "ssm_d", "ssm_w_glu", "pool_w", "pool_scale", "w_out",
           "ffn2_norm", "ffn2_wi", "ffn2_wo", "ple_norm", "ple_w_gate", "ple_w_proj", "final_norm"]
REPLICATED = [n for n in WEIGHTS if n not in SHARDED]


HALVED = ("ffn1_wi", "ffn2_wi")


def _unshard(gathered, axis, halved):
    if halved:
        _, rows, cols = gathered.shape
        return gathered.reshape(2, 4, rows, cols).transpose(0, 2, 1, 3).reshape(2, rows, 4 * cols)
    g = jnp.moveaxis(gathered, 0, axis)
    shp = g.shape
    return g.reshape(shp[:axis] + (shp[axis] * shp[axis + 1],) + shp[axis + 2:])


def _split_for_scatter(full, axis, c, halved):
    if halved:
        rows, cols = full[0].shape

        def pick(cc):
            return [lax.dynamic_index_in_dim(h.reshape(rows, 2, 2, cols // 4), cc, 2, keepdims=False).transpose(1, 0, 2)
                    for h in full]

        return pick(c), [s.astype(BF16) for s in pick(1 - c)]
    shp = full.shape
    g = full.reshape(shp[:axis] + (4, 2, shp[axis] // N_DEV) + shp[axis + 1:])
    keep = lax.dynamic_index_in_dim(g, c, axis + 1, keepdims=False)
    send = lax.dynamic_index_in_dim(g, 1 - c, axis + 1, keepdims=False)
    return jnp.moveaxis(keep, axis, 0), jnp.moveaxis(send, axis, 0).astype(BF16)


def _pack(arrs):
    pieces = []
    for a in arrs:
        flat = a.reshape(-1)
        pad = (-flat.shape[0]) % PACK
        pieces.append(jnp.pad(flat, (0, pad)).reshape(-1, LANES))
    return jnp.concatenate(pieces, axis=0)


def _unpack(packed, shapes):
    out, row = [], 0
    for s in shapes:
        size = math.prod(s)
        rows = (size + PACK - 1) // PACK * SUBLANES
        out.append(packed[row:row + rows].reshape(-1)[:size].reshape(s))
        row += rows
    return out


def local_step(x, p, target, rep, hooks):
    batch, seq, d = x.shape
    n_tok = batch * seq
    h = x.reshape(n_tok, d)
    saved = []
    n = rms_fwd(h, rep["ffn1_norm"][0].reshape(1, d), "first_norm")
    tables = ssm_tables(rep)
    for i in range(DEPTH):
        lw = _layer_weights(rep, i, d)
        big = lambda name, i=i: hooks.big(i, name)
        next_gain = rep["ffn1_norm"][i + 1].reshape(1, d) if i + 1 < DEPTH else None
        def carry(where, run, i=i):
            outs, exchanged = run(hooks.fwd_exchange(i, where))
            hooks.fwd_done(i, where, exchanged)
            return outs

        h, n, s1 = ffn_fwd(h, n, big("ffn1_wi"), big("ffn1_wo"), lw["mix_norm"], "ffn1", carry)
        lw.update(w_in=big("w_in"), w_glu=big("ssm_w_glu"), w_out=big("w_out"))
        h, n, s2, exchanged = mix_fwd(h, n, lw, tables, i, batch, lw["ffn2_norm"], "mix", hooks.fwd_exchange(i, "scan"))
        hooks.fwd_done(i, "scan", exchanged)
        h, n, s3 = ffn_fwd(h, n, big("ffn2_wi"), big("ffn2_wo"), lw["ple_norm"], "ffn2", carry)
        p_i = p[i].reshape(n_tok, -1)
        h, n, s4 = ple_fwd(h, n, p_i, big("ple_w_gate"), big("ple_w_proj"), next_gain, "ple")
        saved.append((s1, s2, s3, s4, p_i))
    last_gate = saved[-1][3][2:]
    loss, dh, dpre, de, d_final = loss_head(h, rep["final_norm"].reshape(1, d), target.reshape(n_tok, d), *last_gate,
                                            "loss_head")
    hooks.final_grad(d_final)
    per_layer = [None] * DEPTH
    for i in reversed(range(DEPTH)):
        lw = _layer_weights(rep, i, d)
        big = lambda name, i=i: hooks.big(i, name)
        lw.update(w_in=big("w_in"), w_glu=big("ssm_w_glu"), w_out=big("w_out"))
        s1, s2, s3, s4, p_i = saved[i]
        g = {}
        dh, g["ple_norm"], g["ple_w_gate"], g["ple_w_proj"] = ple_bwd(dh, dpre, de, s4, p_i, lw["ple_norm"],
                                                                     big("ple_w_gate"), "ple")
        def carry(where, run, i=i):
            outs, exchanged = run(hooks.bwd_exchange(i, where))
            hooks.bwd_done(i, where, exchanged)
            return outs

        dh, g["ffn2_norm"], g["ffn2_wi"], g["ffn2_wo"], _ = ffn_bwd(
            dh, s3, lw["ffn2_norm"], big("ffn2_wi"), big("ffn2_wo"), "ffn2", carry)
        dh, gm, exchanged = mix_bwd(dh, s2, lw, batch, "mix", hooks.bwd_exchange(i, "scan"))
        hooks.bwd_done(i, "scan", exchanged)
        g.update(gm)
        hooks.layer_grads(i, g, [k for k in SHARDED if k not in LAST_GRADS])
        hooks.small_grads(i, {k: g[k] for k in REPLICATED if k in g})
        below = saved[i - 1][3][2:] if i > 0 else None

        def ffn1_grads(d_wi, d_wo, i=i, g=g):
            g["ffn1_wi"], g["ffn1_wo"] = d_wi, d_wo
            hooks.layer_grads(i, g, list(LAST_GRADS))

        dh, g["ffn1_norm"], _, _, gate_ct = ffn_bwd(
            dh, s1, lw["ffn1_norm"], big("ffn1_wi"), big("ffn1_wo"), "ffn1", carry, below, ffn1_grads)
        hooks.small_grads(i, {"ffn1_norm": g["ffn1_norm"]})
        if i > 0:
            dpre, de = gate_ct
        per_layer[i] = g
    return loss, dh.reshape(batch, seq, d), per_layer, d_final


LAST_GRADS = ("ffn1_wi", "ffn1_wo")


def _layer_weights(w, i, d):
    sw = d // 2
    lw = {}
    lw["pool_w"] = w["pool_w"][i]
    for k in ("ffn1_norm", "mix_norm", "ffn2_norm", "ple_norm"):
        lw[k] = w[k][i].reshape(1, d)
    lw["ssm_d"] = w["ssm_d"][i].reshape(1, sw)
    lw["pool_scale"] = w["pool_scale"][i].reshape(1, sw)
    lw["lam_re"] = w["ssm_lambda_re"][i].reshape(SSM_CH, 1)
    lw["lam_im"] = w["ssm_lambda_im"][i].reshape(SSM_CH, 1)
    lw["log_dt"] = jnp.repeat(w["ssm_log_dt"][i], SSM_STATE).reshape(SSM_CH, 1)
    lw["b_re"] = w["ssm_b_re"][i].reshape(SSM_CH, SSM_GROUP_CH)
    lw["b_im"] = w["ssm_b_im"][i].reshape(SSM_CH, SSM_GROUP_CH)
    lw["c_re"] = w["ssm_c_re"][i]
    lw["c_im"] = w["ssm_c_im"][i]
    return lw


class MeshExchange:
    FIRST = ("ffn1_wi", "ffn1_wo")
    FIRST_LAYER_PLAN = {"ffn1_in": ("w_in", "ssm_w_glu", "w_out", "ffn2_wi", "ffn2_wo"),
                        "ffn1_out": ("ple_w_gate", "ple_w_proj")}
    FWD_PLAN = {"scan": ("ffn1_wi", "ffn1_wo", "w_in", "ssm_w_glu", "w_out"),
                "ffn2_in": ("ffn2_wi", "ffn2_wo"), "ffn2_out": ("ple_w_gate", "ple_w_proj")}
    BWD_PLAN = {"ffn2_in_dx": ("ffn1_wo", "ffn2_wo", "ssm_w_glu", "w_out", "ple_w_gate", "ple_w_proj"),
                "scan": ("ffn1_wi", "ffn2_wi", "w_in")}
    LAST_LAYER_PLAN = {"ffn1_out_dx": ("ffn2_wi", "w_in"),
                       "ffn1_out_dw": ("ffn2_wo", "w_out", "ple_w_gate", "ssm_w_glu", "ple_w_proj"),
                       "ffn1_in_dx": LAST_GRADS}

    def __init__(self, shards):
        self.small = {}
        self.d_final = None
        self.small_parts = []
        self.small_rest = None
        self.swaps = {}
        self.shards = shards
        self.c = lax.axis_index("c")
        self.gathered = {}
        self.chip_sums = {}
        self.from_chips = {}
        self.pending = None
        first = run_exchange(GatherPlan([shards[0][k] for k in self.FIRST]), "gather_first_weights")
        self._store(self.gathered, 0, self.FIRST, first)

    @staticmethod
    def _store(where, layer, names, results):
        for k, r in zip(names, results):
            where[(layer, k)] = r

    def big(self, i, name):
        return _unshard(self.gathered[(i, name)], SHARDED[name], name in HALVED)

    def fwd_exchange(self, i, where):
        if i == 0 and where in self.FIRST_LAYER_PLAN:
            layer, names = 0, self.FIRST_LAYER_PLAN[where]
        elif where in self.FWD_PLAN and i + 1 < DEPTH:
            layer, names = i + 1, self.FWD_PLAN[where]
        else:
            return None
        self.pending = (layer, names)
        return GatherPlan([self.shards[layer][k] for k in names])

    def fwd_done(self, i, where, results):
        if results:
            self._store(self.gathered, *self.pending, results)

    def layer_grads(self, i, grads, names):
        pieces, sends = [], []
        for k in names:
            keep, send = _split_for_scatter(grads[k], SHARDED[k], self.c, k in HALVED)
            keep, send = (keep, send) if isinstance(keep, list) else ([keep], [send])
            pieces.append(keep)
            sends += send
        if i == 0:
            self._add_sibling(0, names, pieces, run_exchange(SiblingSwapPlan(sends), "reduce_core_pair"))
        else:
            self.swaps[(i, tuple(names) == LAST_GRADS)] = (names, pieces, sends)

    def _add_sibling(self, i, names, pieces, from_sibling):
        got = iter(from_sibling)
        for k, keep in zip(names, pieces):
            sums = []
            for part in keep:
                cols = part.shape[-1]
                sums.append(add2(part.reshape(-1, cols), next(got).reshape(-1, cols), f"sum_core_pair_{k}",
                                 BF16).reshape(part.shape))
            self.chip_sums[(i, k)] = sums if len(sums) > 1 else sums[0]

    def small_grads(self, i, grads):
        self.small.setdefault(i, {}).update(grads)

    def final_grad(self, d_final):
        self.d_final = d_final

    def _small_pack(self, d_final, first_norm_grad=None):
        pieces = []
        for k in REPLICATED:
            if k == "final_norm":
                pieces.append(d_final)
            elif k == "ffn1_norm" and first_norm_grad is None:
                pieces.append(jnp.stack([self.small[i][k] for i in range(1, DEPTH)], axis=0))
            else:
                pieces.append(jnp.stack([self.small[i][k] for i in range(DEPTH)], axis=0))
        return _pack(pieces)

    def bwd_exchange(self, i, where):
        if where in self.BWD_PLAN and i + 1 < DEPTH:
            layer, names = i + 1, self.BWD_PLAN[where]
        elif i == 0 and where in self.LAST_LAYER_PLAN:
            layer, names = 0, self.LAST_LAYER_PLAN[where]
        elif i == 0 and where in ("ffn1_in_dw_g", "ffn1_in_dw_u"):
            if where.endswith("g"):
                pack = self._small_pack(self.d_final)
                half = pack.shape[0] // 2 // SUBLANES * SUBLANES
                part, self.small_rest = pack[:half], pack[half:]
            else:
                part = self.small_rest
            self.pending = "small"
            return GatherPlan([part])
        elif i > 0 and where in ("ffn1_out_dx", "ffn1_in_dx"):
            self.pending = ("swap", i, where == "ffn1_in_dx")
            return SiblingSwapPlan(self.swaps[self.pending[1:]][2])
        else:
            return None
        self.pending = (layer, names)
        return ChipScatterPlan([self.chip_sums[(layer, k)] for k in names])

    def bwd_done(self, i, where, results):
        if not results:
            return
        if self.pending == "small":
            self.small_parts += results
        elif self.pending[0] == "swap":
            names, pieces, _ = self.swaps[self.pending[1:]]
            self._add_sibling(self.pending[1], names, pieces, results)
        else:
            self._store(self.from_chips, *self.pending, results)

    def small_gathered(self):
        first = self.small[0]["ffn1_norm"].reshape(SUBLANES, LANES)
        first_all, = run_exchange(GatherPlan([first]), "gather_first_gain_grad")
        return jnp.concatenate([first_all] + self.small_parts, axis=1)


def kernel(x, p, ffn1_norm, ffn1_wi, ffn1_wo, mix_norm, w_in, ssm_lambda_re, ssm_lambda_im, ssm_log_dt, ssm_b_re, ssm_b_im, ssm_c_re, ssm_c_im, ssm_d, ssm_w_glu, pool_w, pool_scale, w_out, ffn2_norm, ffn2_wi, ffn2_wo, ple_norm, ple_w_gate, ple_w_proj, final_norm, loss_target, m_ffn1_norm, m_ffn1_wi, m_ffn1_wo, m_mix_norm, m_w_in, m_ssm_lambda_re, m_ssm_lambda_im, m_ssm_log_dt, m_ssm_b_re, m_ssm_b_im, m_ssm_c_re, m_ssm_c_im, m_ssm_d, m_ssm_w_glu, m_pool_w, m_pool_scale, m_w_out, m_ffn2_norm, m_ffn2_wi, m_ffn2_wo, m_ple_norm, m_ple_w_gate, m_ple_w_proj, m_final_norm, v_ffn1_norm, v_ffn1_wi, v_ffn1_wo, v_mix_norm, v_w_in, v_ssm_lambda_re, v_ssm_lambda_im, v_ssm_log_dt, v_ssm_b_re, v_ssm_b_im, v_ssm_c_re, v_ssm_c_im, v_ssm_d, v_ssm_w_glu, v_pool_w, v_pool_scale, v_w_out, v_ffn2_norm, v_ffn2_wi, v_ffn2_wo, v_ple_norm, v_ple_w_gate, v_ple_w_proj, v_final_norm):
    args = dict(locals())
    wts = {k: args[k] for k in WEIGHTS}
    rep = {k: wts[k] for k in REPLICATED}

    shards = [{k: wts[k][i].astype(BF16) for k in SHARDED} for i in range(DEPTH)]
    exchange = MeshExchange(shards)
    loss_local, grad_x, per_layer, d_final = local_step(x, p, loss_target, rep, exchange)
    loss = lax.psum(loss_local, ("x", "y", "c"))

    outs = {}
    for k in SHARDED:
        shp = wts[k].shape
        cols = shp[-1]
        parts = jnp.stack([exchange.from_chips[(i, k)] for i in range(DEPTH)], axis=1)
        res = adamw(wts[k].reshape(-1, cols), parts.reshape(4, -1, cols), args["m_" + k].reshape(-1, cols),
                    args["v_" + k].reshape(-1, cols), f"adamw_{k}")
        outs[k] = [r.reshape(shp) for r in res]

    rep_shapes = [wts[k].shape for k in REPLICATED]
    all_g = exchange.small_gathered()
    res = adamw(_pack([wts[k] for k in REPLICATED]), all_g, _pack([args["m_" + k] for k in REPLICATED]),
                _pack([args["v_" + k] for k in REPLICATED]), "adamw_small")
    unpacked = [_unpack(r, rep_shapes) for r in res]
    for j, k in enumerate(REPLICATED):
        outs[k] = [unpacked[q][j] for q in range(4)]

    result = [loss, grad_x]
    for q in range(4):
        result += [outs[k][q] for k in WEIGHTS]
    return tuple(result)
```
